```python
import math
import jax
import jax.numpy as jnp
from jax import lax
import numpy as np

D_MODEL = 1024
BATCH = 8
SEQ = 8192
DEPTH = 1

N_META = 16
CHUNK = 64
META_PAD = CHUNK - N_META
HG_HEADS = 8
HG_DK = 128
HG_DV = 128
ML_HEADS = 4
ML_DK = 128
ML_DV = 256
CONV_W = 4
FG_BIAS_LO = 3.0
FG_BIAS_HI = 6.0
N_EXPERTS = 64
TOP_K = 8
N_GROUPS = 8
TOPK_GROUPS = 4
D_EXPERT = 256
ROUTED_SCALE = 2.5
MOE_BLOCK = 256
DN_ALPHA = (2.0 * DEPTH) ** 0.25
DN_BETA = (8.0 * DEPTH) ** -0.25
EPS = 1e-5

HG_KW = HG_HEADS * HG_DK
HG_VW = HG_HEADS * HG_DV
ML_KW = ML_HEADS * ML_DK
ML_VW = ML_HEADS * ML_DV
IN_SPLITS = (HG_KW, HG_KW, HG_VW, HG_VW, ML_KW, ML_KW, ML_VW, ML_VW, ML_HEADS, ML_HEADS, D_MODEL, D_MODEL)
IN_WIDTH = sum(IN_SPLITS)

kernel_name = 'hybrid_hgrn2_mlstm_moe_block'


def _layer_norm(x, g, b):
    xf = x.astype(jnp.float32)
    xc = xf - jnp.mean(xf, axis=-1, keepdims=True)
    var = jnp.mean(xc * xc, axis=-1, keepdims=True)
    return (xc * lax.rsqrt(var + EPS) * g + b).astype(x.dtype)


def _head_norm(x, heads, g, center):
    b_, l_, w_ = x.shape
    xf = x.astype(jnp.float32).reshape(b_, l_, heads, w_ // heads)
    if center:
        xf = xf - jnp.mean(xf, axis=-1, keepdims=True)
    var = jnp.mean(xf * xf, axis=-1, keepdims=True)
    return (xf * lax.rsqrt(var + EPS)).reshape(b_, l_, w_) * g


def _front_pad(a, value):
    pads = [(0, 0), (META_PAD, 0)] + [(0, 0)] * (a.ndim - 2)
    return jnp.pad(a, pads, constant_values=value)


def _to_chunks(a, heads):
    b_, lp, w_ = a.shape
    return a.reshape(b_, lp // CHUNK, CHUNK, heads, w_ // heads).transpose(1, 0, 3, 2, 4)


def _from_chunks(o):
    n_, b_, h_, c_, d_ = o.shape
    return o.transpose(1, 0, 3, 2, 4).reshape(b_, n_ * c_, h_ * d_)


def _causal_dwconv(x, w, b):
    y = lax.conv_general_dilated(x, w[:, None, :], window_strides=(1,), padding=[(CONV_W - 1, 0)],
                                 dimension_numbers=('NWC', 'WIO', 'NWC'), feature_group_count=x.shape[-1])
    return y + b


def _hgrn2_chunkwise(q, k, v, log_f):
    q, k, v, log_f = (_to_chunks(_front_pad(a.astype(jnp.float32), 0.0), HG_HEADS) for a in (q, k, v, log_f))
    causal = jnp.tril(jnp.ones((CHUNK, CHUNK), dtype=bool))

    def step(state, inp):
        qc, kc, vc, lfc = inp
        b = jnp.cumsum(lfc, axis=2)
        o_inter = jnp.einsum('bhtk,bhkv->bhtv', qc * jnp.exp(b), state)
        rel = b[:, :, :, None, :] - b[:, :, None, :, :]
        decay = jnp.exp(jnp.where(causal[:, :, None], rel, -jnp.inf))
        scores = jnp.einsum('bhtk,bhsk,bhtsk->bhts', qc, kc, decay)
        o = o_inter + jnp.einsum('bhts,bhsv->bhtv', scores, vc)
        b_last = b[:, :, -1:, :]
        state = jnp.exp(b_last[:, :, 0, :])[..., None] * state + jnp.einsum(
            'bhsk,bhsv->bhkv', kc * jnp.exp(b_last - b), vc)
        return state, o

    s0 = jnp.zeros((q.shape[1], HG_HEADS, HG_DK, HG_DV), jnp.float32)
    _, o = lax.scan(step, s0, (q, k, v, log_f))
    return _from_chunks(o)[:, META_PAD:]


def _mlstm_chunkwise(q, k, v, ig, log_f):
    q, k, v = (_to_chunks(_front_pad(a.astype(jnp.float32), 0.0), ML_HEADS) for a in (q, k, v))
    ig = _to_chunks(_front_pad(ig.astype(jnp.float32), -jnp.inf), ML_HEADS)[..., 0]
    log_f = _to_chunks(_front_pad(log_f.astype(jnp.float32), 0.0), ML_HEADS)[..., 0]
    causal = jnp.tril(jnp.ones((CHUNK, CHUNK), dtype=bool))

    def step(carry, inp):
        c_st, n_st, m_st = carry
        qc, kc, vc, igc, lfc = inp
        b = jnp.cumsum(lfc, axis=-1)
        log_intra = jnp.where(causal, b[..., :, None] - b[..., None, :] + igc[..., None, :], -jnp.inf)
        log_inter = b + m_st[..., None]
        m_t = jnp.maximum(log_inter, jnp.max(log_intra, axis=-1))
        w_intra = jnp.exp(log_intra - m_t[..., None])
        w_inter = jnp.exp(log_inter - m_t)
        s = jnp.einsum('bhtd,bhsd->bhts', qc, kc) * w_intra
        num = w_inter[..., None] * jnp.einsum('bhtd,bhdv->bhtv', qc, c_st) + jnp.einsum('bhts,bhsv->bhtv', s, vc)
        den = w_inter * jnp.einsum('bhtd,bhd->bht', qc, n_st) + jnp.sum(s, axis=-1)
        h = num / jnp.maximum(jnp.abs(den), jnp.exp(-m_t))[..., None]
        b_last = b[..., -1]
        log_w = b_last[..., None] - b + igc
        m_new = jnp.maximum(b_last + m_st, jnp.max(log_w, axis=-1))
        w_s = jnp.exp(log_w - m_new[..., None])
        decay = jnp.exp(b_last + m_st - m_new)
        c_st = decay[..., None, None] * c_st + jnp.einsum('bhsd,bhsv->bhdv', kc * w_s[..., None], vc)
        n_st = decay[..., None] * n_st + jnp.einsum('bhsd,bhs->bhd', kc, w_s)
        return (c_st, n_st, m_new), h

    b_ = q.shape[1]
    carry0 = (jnp.zeros((b_, ML_HEADS, ML_DK, ML_DV), jnp.float32),
              jnp.zeros((b_, ML_HEADS, ML_DK), jnp.float32),
              jnp.zeros((b_, ML_HEADS), jnp.float32))
    _, h = lax.scan(step, carry0, (q, k, v, ig, log_f))
    return _from_chunks(h)[:, META_PAD:]


def _hybrid_mixer(h, w_in, lb, hg_norm_g, conv_w, conv_b, ig_bias, fg_bias, ml_norm_g, w_a, w_b, w_out):
    proj = jnp.einsum('bld,de->ble', h, w_in)
    split_at = [int(s) for s in np.cumsum(IN_SPLITS)[:-1]]
    (q_a, f_a, i_a, g_a, q_b, k_b, v_b, o_b, ig_b, fg_b, gate_a, gate_b) = jnp.split(proj, split_at, axis=-1)
    f = lb + (1.0 - lb) * jax.nn.sigmoid(f_a.astype(jnp.float32))
    o_a = _hgrn2_chunkwise(q_a, 1.0 - f, i_a, jnp.log(f))
    y_a = (_head_norm(o_a, HG_HEADS, hg_norm_g, False) * jax.nn.silu(g_a.astype(jnp.float32))).astype(h.dtype)
    qk = jax.nn.silu(_causal_dwconv(jnp.concatenate([q_b, k_b], axis=-1), conv_w, conv_b))
    q_b, k_b = jnp.split(qk, 2, axis=-1)
    h_b = _mlstm_chunkwise(q_b * (ML_DK ** -0.5), k_b, v_b,
                           ig_b.astype(jnp.float32) + ig_bias,
                           jax.nn.log_sigmoid(fg_b.astype(jnp.float32) + fg_bias))
    y_b = (_head_norm(h_b, ML_HEADS, ml_norm_g, True) * jax.nn.sigmoid(o_b.astype(jnp.float32))).astype(h.dtype)
    merged = (jax.nn.sigmoid(gate_a) * jnp.einsum('blv,vd->bld', y_a, w_a)
              + jax.nn.sigmoid(gate_b) * jnp.einsum('blv,vd->bld', y_b, w_b))
    return jnp.einsum('bld,de->ble', merged, w_out)


def _moe_ffn(h, w_router, router_bias, w_gate, w_up, w_down, w_sh_gate, w_sh_up, w_sh_down):
    b_, l_, d_ = h.shape
    n_tok = b_ * l_
    hf = h.reshape(n_tok, d_)
    scores = jax.nn.sigmoid(jnp.einsum('td,de->te', hf, w_router).astype(jnp.float32))
    biased = scores + router_bias.astype(jnp.float32)
    grouped = biased.reshape(n_tok, N_GROUPS, N_EXPERTS // N_GROUPS)
    group_score = jnp.sum(lax.top_k(grouped, 2)[0], axis=-1)
    _, gidx = lax.top_k(group_score, TOPK_GROUPS)
    gmask = jnp.any(gidx[..., None] == jnp.arange(N_GROUPS)[None, None, :], axis=1)
    emask = jnp.repeat(gmask, N_EXPERTS // N_GROUPS, axis=1)
    _, eidx = lax.top_k(jnp.where(emask, biased, -jnp.inf), TOP_K)
    gw = jnp.take_along_axis(scores, eidx, axis=1)
    gw = gw / jnp.sum(gw, axis=-1, keepdims=True) * ROUTED_SCALE
    n_assign = n_tok * TOP_K
    e_flat = eidx.reshape(-1).astype(jnp.int32)
    tok_flat = jnp.arange(n_assign, dtype=jnp.int32) // TOP_K
    w_flat = gw.reshape(-1)
    order = jnp.argsort(e_flat)
    se, stok, sw = e_flat[order], tok_flat[order], w_flat[order]
    counts = jnp.bincount(e_flat, length=N_EXPERTS).astype(jnp.int32)
    starts = jnp.cumsum(counts) - counts
    padded = (counts + MOE_BLOCK - 1) // MOE_BLOCK * MOE_BLOCK
    pends = jnp.cumsum(padded)
    pstarts = pends - padded
    dest = pstarts[se] + jnp.arange(n_assign, dtype=jnp.int32) - starts[se]
    n_blocks = -(-n_assign // MOE_BLOCK) + N_EXPERTS
    n_slots = n_blocks * MOE_BLOCK
    tok_buf = jnp.full((n_slots,), n_tok, jnp.int32).at[dest].set(stok)
    w_buf = jnp.zeros((n_slots,), jnp.float32).at[dest].set(sw)
    blk_start = jnp.arange(n_blocks, dtype=jnp.int32) * MOE_BLOCK
    blk_expert = jnp.minimum(jnp.searchsorted(pends, blk_start, side='right'), N_EXPERTS - 1)
    h_pad = jnp.concatenate([hf, jnp.zeros((1, d_), hf.dtype)], axis=0)

    def step(acc, inp):
        tok, w, e = inp
        xb = h_pad[tok]
        a = jnp.einsum('td,df->tf', xb, w_gate[e])
        u = jnp.einsum('td,df->tf', xb, w_up[e])
        y = jnp.einsum('tf,fd->td', jax.nn.silu(a) * u, w_down[e]) * w[:, None]
        return acc.at[tok].add(y.astype(acc.dtype)), None

    acc0 = jnp.zeros((n_tok + 1, d_), jnp.float32)
    routed, _ = lax.scan(step, acc0, (tok_buf.reshape(n_blocks, MOE_BLOCK),
                                      w_buf.reshape(n_blocks, MOE_BLOCK), blk_expert))
    shared = jnp.einsum('tf,fd->td', jax.nn.silu(jnp.einsum('td,df->tf', hf, w_sh_gate))
                        * jnp.einsum('td,df->tf', hf, w_sh_up), w_sh_down)
    return (routed[:n_tok] + shared).reshape(b_, l_, d_).astype(h.dtype)


def setup_inputs(seed: int = 0) -> dict:
    key = jax.random.key(seed)
    ks = jax.random.split(key, 27)
    nrm = lambda k, shp: jax.random.normal(k, shp, jnp.float32)
    L = DEPTH
    return {
        'x': nrm(ks[0], (BATCH, SEQ, D_MODEL)),
        'meta_tokens': nrm(ks[1], (N_META, D_MODEL)),
        'ln_emb_g': 1.0 + 0.02 * nrm(ks[2], (D_MODEL,)),
        'ln_emb_b': 0.02 * nrm(ks[3], (D_MODEL,)),
        'w_in': nrm(ks[4], (L, D_MODEL, IN_WIDTH)) * D_MODEL ** -0.5,
        'hg_lb_logits': 0.5 * nrm(ks[5], (DEPTH + 1, HG_KW)),
        'hg_norm_g': 1.0 + 0.02 * nrm(ks[6], (L, HG_VW)),
        'ml_conv_w': nrm(ks[7], (L, CONV_W, 2 * ML_KW)) * CONV_W ** -0.5,
        'ml_conv_b': 0.02 * nrm(ks[8], (L, 2 * ML_KW)),
        'ml_ig_bias': 0.1 * nrm(ks[9], (L, ML_HEADS)),
        'ml_fg_bias': jnp.linspace(FG_BIAS_LO, FG_BIAS_HI, ML_HEADS, dtype=jnp.float32)[None, :] + 0.1 * nrm(ks[10], (L, ML_HEADS)),
        'ml_norm_g': 1.0 + 0.02 * nrm(ks[11], (L, ML_VW)),
        'w_branch_a': nrm(ks[12], (L, HG_VW, D_MODEL)) * HG_VW ** -0.5 * DN_BETA,
        'w_branch_b': nrm(ks[13], (L, ML_VW, D_MODEL)) * ML_VW ** -0.5 * DN_BETA,
        'w_out': nrm(ks[14], (L, D_MODEL, D_MODEL)) * D_MODEL ** -0.5 * DN_BETA,
        'ln1_g': 1.0 + 0.02 * nrm(ks[15], (L, D_MODEL)),
        'ln1_b': 0.02 * nrm(ks[16], (L, D_MODEL)),
        'w_router': nrm(ks[17], (L, D_MODEL, N_EXPERTS)) * D_MODEL ** -0.5,
        'router_bias': 0.01 * nrm(ks[18], (L, N_EXPERTS)),
        'w_exp_gate': nrm(ks[19], (L, N_EXPERTS, D_MODEL, D_EXPERT)) * D_MODEL ** -0.5,
        'w_exp_up': nrm(ks[20], (L, N_EXPERTS, D_MODEL, D_EXPERT)) * D_MODEL ** -0.5,
        'w_exp_down': nrm(ks[21], (L, N_EXPERTS, D_EXPERT, D_MODEL)) * D_EXPERT ** -0.5 * DN_BETA,
        'w_sh_gate': nrm(ks[22], (L, D_MODEL, D_EXPERT)) * D_MODEL ** -0.5,
        'w_sh_up': nrm(ks[23], (L, D_MODEL, D_EXPERT)) * D_MODEL ** -0.5,
        'w_sh_down': nrm(ks[24], (L, D_EXPERT, D_MODEL)) * D_EXPERT ** -0.5 * DN_BETA,
        'ln2_g': 1.0 + 0.02 * nrm(ks[25], (L, D_MODEL)),
        'ln2_b': 0.02 * nrm(ks[26], (L, D_MODEL)),
    }


def reference(x, meta_tokens, ln_emb_g, ln_emb_b, w_in, hg_lb_logits, hg_norm_g, ml_conv_w, ml_conv_b,
              ml_ig_bias, ml_fg_bias, ml_norm_g, w_branch_a, w_branch_b, w_out, ln1_g, ln1_b,
              w_router, router_bias, w_exp_gate, w_exp_up, w_exp_down, w_sh_gate, w_sh_up, w_sh_down,
              ln2_g, ln2_b):
    b_ = x.shape[0]
    meta = jnp.broadcast_to(meta_tokens.astype(x.dtype)[None], (b_, N_META, D_MODEL))
    h = _layer_norm(jnp.concatenate([meta, x], axis=1), ln_emb_g, ln_emb_b)
    lower_bounds = jnp.cumsum(jax.nn.softmax(hg_lb_logits.astype(jnp.float32), axis=0), axis=0)
    for layer in range(DEPTH):
        mix = _hybrid_mixer(h, w_in[layer], lower_bounds[layer], hg_norm_g[layer], ml_conv_w[layer],
                            ml_conv_b[layer], ml_ig_bias[layer], ml_fg_bias[layer], ml_norm_g[layer],
                            w_branch_a[layer], w_branch_b[layer], w_out[layer])
        h = _layer_norm(DN_ALPHA * h + mix, ln1_g[layer], ln1_b[layer])
        ffn = _moe_ffn(h, w_router[layer], router_bias[layer], w_exp_gate[layer], w_exp_up[layer],
                       w_exp_down[layer], w_sh_gate[layer], w_sh_up[layer], w_sh_down[layer])
        h = _layer_norm(DN_ALPHA * h + ffn, ln2_g[layer], ln2_b[layer])
    return h[:, N_META:]
```

```python
import functools

import jax
import jax.numpy as jnp
from jax import lax
from jax.experimental import pallas as pl
from jax.experimental.pallas import tpu as pltpu

F32, BF16, I32 = jnp.float32, jnp.bfloat16, jnp.int32

D_MODEL = 1024
N_META = 16
HG_HEADS = 8
HG_DK = 128
ML_HEADS = 4
ML_DK = 128
ML_DV = 256
ML_AUG = ML_DV + 128
N_EXPERTS = 64
TOP_K = 8
N_GROUPS = 8
GROUP_SIZE = N_EXPERTS // N_GROUPS
TOPK_GROUPS = 4
D_EXPERT = 256
ROUTED_SCALE = 2.5
MOE_BLOCK = 256
DN_ALPHA = 2.0 ** 0.25
EPS = 1e-5
EXP_CLAMP = 80.0
SUBLANES = 8

P_QA, P_KA, P_IA, P_GA, P_QKB, P_VB, P_OB, P_MA, P_MB = range(9)
N_SLABS = 9

VMEM_LIMIT = 56 * 1024 * 1024


def _params(*sem):
    return pltpu.CompilerParams(dimension_semantics=sem, vmem_limit_bytes=VMEM_LIMIT)


def _sigmoid(x):
    return 1.0 / (1.0 + jnp.exp(-x))


def _log_sigmoid(x):
    return jnp.minimum(x, 0.0) - jnp.log(1.0 + jnp.exp(-jnp.abs(x)))


def _layer_norm(x, g, b):
    xc = x - jnp.mean(x, axis=-1, keepdims=True)
    var = jnp.mean(xc * xc, axis=-1, keepdims=True)
    return xc * lax.rsqrt(var + EPS) * g + b


def _dot(a, b):
    return jnp.dot(a, b, preferred_element_type=F32)


def _dot_nt(a, b):
    return lax.dot_general(a, b, (((1,), (1,)), ((), ())), preferred_element_type=F32)


def _dot_tn(a, b):
    return lax.dot_general(a, b, (((0,), (0,)), ((), ())), preferred_element_type=F32)


def _split_bf16(x):
    hi = x.astype(BF16)
    lo = (x - hi.astype(F32)).astype(BF16)
    return hi, lo


def _ones_where(cond):
    return jnp.where(cond, 1.0, 0.0).astype(BF16)


def _inproj_kernel(x_ref, g_ref, b_ref, w_ref, ws_ref, lb_ref, gb_ref, p_ref, lf_ref, sg_ref, h_scr):
    n = pl.program_id(1)

    @pl.when(n == 0)
    def _():
        hb = _layer_norm(x_ref[...], g_ref[...], b_ref[...]).astype(BF16)
        h_scr[...] = hb
        s = _dot(hb, ws_ref[...]) + gb_ref[...]
        lane = lax.broadcasted_iota(I32, s.shape, 1)
        sg_ref[...] = jnp.where(lane < ML_HEADS, s, _log_sigmoid(s))

    acc = _dot(h_scr[...], w_ref[...])

    @pl.when((n == P_QA) | (n == P_IA) | (n == P_QKB) | (n == P_VB))
    def _():
        p_ref[...] = acc.astype(BF16)

    @pl.when(n == P_KA)
    def _():
        lb = lb_ref[...]
        f = lb + (1.0 - lb) * _sigmoid(acc)
        lf_ref[...] = jnp.log(f)
        p_ref[...] = (1.0 - f).astype(BF16)

    @pl.when(n == P_GA)
    def _():
        p_ref[...] = (acc * _sigmoid(acc)).astype(BF16)

    @pl.when(n >= P_OB)
    def _():
        p_ref[...] = _sigmoid(acc).astype(BF16)


def _inproj(x2d, ln_g, ln_b, w_cat, w_small, lb, gate_bias, tm):
    m = x2d.shape[0]
    grid = (m // tm, N_SLABS)
    row = lambda i, n: (0, 0)
    return pl.pallas_call(
        _inproj_kernel,
        grid=grid,
        in_specs=[
            pl.BlockSpec((tm, D_MODEL), lambda i, n: (i, 0)),
            pl.BlockSpec((1, D_MODEL), row),
            pl.BlockSpec((1, D_MODEL), row),
            pl.BlockSpec((D_MODEL, D_MODEL), lambda i, n: (0, n)),
            pl.BlockSpec((D_MODEL, 128), row),
            pl.BlockSpec((1, D_MODEL), row),
            pl.BlockSpec((1, 128), row),
        ],
        out_specs=[
            pl.BlockSpec((None, tm, D_MODEL), lambda i, n: (n, i, 0)),
            pl.BlockSpec((tm, D_MODEL), lambda i, n: (i, 0)),
            pl.BlockSpec((tm, 128), lambda i, n: (i, 0)),
        ],
        out_shape=[
            jax.ShapeDtypeStruct((N_SLABS, m, D_MODEL), BF16),
            jax.ShapeDtypeStruct((m, D_MODEL), F32),
            jax.ShapeDtypeStruct((m, 128), F32),
        ],
        scratch_shapes=[pltpu.VMEM((tm, D_MODEL), BF16)],
        compiler_params=_params("parallel", "arbitrary"),
        name="inproj",
    )(x2d, ln_g, ln_b, w_cat, w_small, lb, gate_bias)


def _block_rows(b, block, pick):
    c, w = b.shape
    parts = [jnp.broadcast_to(b[j * block + pick:j * block + pick + 1, :], (block, w))
             for j in range(c // block)]
    return parts[0] if len(parts) == 1 else jnp.concatenate(parts, axis=0)


def _hgrn_kernel(q_ref, k_ref, v_ref, g_ref, lf_ref, ng_ref, s0_ref, y_ref, sfin_ref, s_scr, *, chunk):
    c = pl.program_id(1)

    @pl.when(c == 0)
    def _():
        s_scr[...] = s0_ref[...]

    cs = chunk
    row = lax.broadcasted_iota(I32, (cs, cs), 0)
    col = lax.broadcasted_iota(I32, (cs, cs), 1)
    tri = _ones_where(col <= row)
    lf_hi, lf_lo = _split_bf16(lf_ref[...])
    b = _dot(tri, lf_hi) + _dot(tri, lf_lo)
    q = q_ref[...].astype(F32)
    k = k_ref[...].astype(F32)
    v = v_ref[...]
    blast = b[cs - 1:cs, :]
    qg = (q * jnp.exp(b)).astype(BF16)
    kg = (k * jnp.exp(blast - b)).astype(BF16)
    dec = jnp.exp(blast)

    levels = []
    m = SUBLANES
    while 2 * m <= cs:
        w = jnp.exp(-jnp.abs(b - _block_rows(b, 2 * m, m - 1)))
        sh = (2 * m).bit_length() - 1
        mask = ((row >> sh) == (col >> sh)) & ((row & (2 * m - 1)) >= m) & ((col & (2 * m - 1)) < m)
        levels.append(((q * w).astype(BF16), (k * w).astype(BF16), mask))
        m *= 2
    e = jnp.clip(b - _block_rows(b, SUBLANES, SUBLANES // 2 - 1), -EXP_CLAMP, EXP_CLAMP)
    levels.append(((q * jnp.exp(e)).astype(BF16), (k * jnp.exp(-e)).astype(BF16),
                   ((row >> 3) == (col >> 3)) & (col <= row)))

    ng = ng_ref[...]
    for h in range(HG_HEADS):
        sl = slice(h * HG_DK, (h + 1) * HG_DK)
        st = s_scr[h]
        o = _dot_nt(qg[:, sl], st.astype(BF16))
        sc = jnp.zeros((cs, cs), F32)
        for lq, lk, mask in levels:
            sc = jnp.where(mask, _dot_nt(lq[:, sl], lk[:, sl]), sc)
        o = o + _dot(sc.astype(BF16), v[:, sl])
        s_scr[h] = dec[:, sl] * st + _dot_tn(v[:, sl], kg[:, sl])
        ms = jnp.mean(o * o, axis=-1, keepdims=True)
        y = o * lax.rsqrt(ms + EPS) * ng[:, sl] * g_ref[:, sl].astype(F32)
        y_ref[:, sl] = y.astype(BF16)

    @pl.when(c == pl.num_programs(1) - 1)
    def _():
        sfin_ref[...] = s_scr[...]


def _hgrn(proj, logf, norm_g, s0, nb, chunk):
    m = logf.shape[0]
    nc = m // nb // chunk
    slab = lambda s: pl.BlockSpec((None, chunk, D_MODEL), lambda b, c, s=s: (s, b * nc + c, 0))
    return pl.pallas_call(
        functools.partial(_hgrn_kernel, chunk=chunk),
        grid=(nb, nc),
        in_specs=[
            slab(P_QA), slab(P_KA), slab(P_IA), slab(P_GA),
            pl.BlockSpec((chunk, D_MODEL), lambda b, c: (b * nc + c, 0)),
            pl.BlockSpec((1, D_MODEL), lambda b, c: (0, 0)),
            pl.BlockSpec((HG_HEADS, HG_DK, HG_DK), lambda b, c: (0, 0, 0)),
        ],
        out_specs=[
            pl.BlockSpec((chunk, D_MODEL), lambda b, c: (b * nc + c, 0)),
            pl.BlockSpec((None, HG_HEADS, HG_DK, HG_DK), lambda b, c: (b, 0, 0, 0)),
        ],
        out_shape=[
            jax.ShapeDtypeStruct((m, D_MODEL), BF16),
            jax.ShapeDtypeStruct((nb, HG_HEADS, HG_DK, HG_DK), F32),
        ],
        scratch_shapes=[pltpu.VMEM((HG_HEADS, HG_DK, HG_DK), F32)],
        compiler_params=_params("parallel", "arbitrary"),
        name="hgrn",
    )(proj, proj, proj, proj, logf, norm_g, s0)


def _mlstm_kernel(qk_ref, v_ref, og_ref, sg_ref, sgt_ref, cw_ref, cb_ref, ng_ref, c0_ref, m0_ref, x0_ref,
                  y_ref, cfin_ref, mfin_ref, xfin_ref, c_scr, m_scr, x_scr, *, chunk):
    c = pl.program_id(1)

    @pl.when(c == 0)
    def _():
        c_scr[...] = c0_ref[...]
        m_scr[...] = m0_ref[...]
        x_scr[...] = x0_ref[...]

    cs = chunk
    x = qk_ref[...].astype(F32)
    prev = x_scr[...]
    sub = lax.broadcasted_iota(I32, (SUBLANES, D_MODEL), 0)
    cw = cw_ref[...]
    conv = cw[3:4, :] * x + cb_ref[...]
    for j in (1, 2, 3):
        xs = pltpu.roll(x, j, 0)
        head = jnp.where(sub < j, pltpu.roll(prev, j, 0), xs[:SUBLANES, :])
        xs = jnp.concatenate([head, xs[SUBLANES:, :]], axis=0)
        conv = conv + cw[3 - j:4 - j, :] * xs
    x_scr[...] = x[cs - SUBLANES:, :]
    qk = conv * _sigmoid(conv)
    q_all = (qk[:, :ML_HEADS * ML_DK] * (ML_DK ** -0.5)).astype(BF16)
    k_all = qk[:, ML_HEADS * ML_DK:]

    row = lax.broadcasted_iota(I32, (cs, cs), 0)
    col = lax.broadcasted_iota(I32, (cs, cs), 1)
    causal = col <= row
    tri = _ones_where(causal)
    sg = sg_ref[...]
    sgt = sgt_ref[...]
    sg_hi, sg_lo = _split_bf16(sg)
    bcol_all = _dot(tri, sg_hi) + _dot(tri, sg_lo)
    sgt_hi, sgt_lo = _split_bf16(sgt)
    brow_all = _dot_nt(sgt_hi, tri) + _dot_nt(sgt_lo, tri)
    lane128 = lax.broadcasted_iota(I32, (cs, 128), 1)
    ones_col = _ones_where(lane128 == 0)
    v = v_ref[...]
    ng = ng_ref[...]

    for h in range(ML_HEADS):
        b_col = bcol_all[:, ML_HEADS + h:ML_HEADS + h + 1]
        b_row = brow_all[ML_HEADS + h:ML_HEADS + h + 1, :]
        ig_col = sg[:, h:h + 1]
        ig_row = sgt[h:h + 1, :]
        m_prev = m_scr[h:h + 1, 0:1]
        q_h = q_all[:, h * ML_DK:(h + 1) * ML_DK]
        k_h = k_all[:, h * ML_DK:(h + 1) * ML_DK]
        v_aug = jnp.concatenate([v[:, h * ML_DV:(h + 1) * ML_DV], ones_col], axis=1)
        c_st = c_scr[h]

        log_intra = jnp.where(causal, b_col - b_row + ig_row, -jnp.inf)
        log_inter = b_col + m_prev
        m_t = jnp.maximum(log_inter, jnp.max(log_intra, axis=-1, keepdims=True))
        w_intra = jnp.exp(log_intra - m_t)
        w_inter = jnp.exp(log_inter - m_t)
        s = _dot_nt(q_h, k_h.astype(BF16)) * w_intra
        tot = w_inter * _dot(q_h, c_st.astype(BF16)) + _dot(s.astype(BF16), v_aug)
        num = tot[:, :ML_DV]
        den = tot[:, ML_DV:ML_DV + 1]
        hid = num / jnp.maximum(jnp.abs(den), jnp.exp(-m_t))

        b_last = b_col[cs - 1:cs, :]
        log_w = b_last - b_col + ig_col
        m_new = jnp.maximum(b_last + m_prev, jnp.max(log_w, axis=0, keepdims=True))
        w_s = jnp.exp(log_w - m_new)
        decay = jnp.exp(b_last + m_prev - m_new)
        c_scr[h] = decay * c_st + _dot_tn((k_h * w_s).astype(BF16), v_aug)
        m_scr[h:h + 1, :] = jnp.broadcast_to(m_new, (1, 128))

        hc = hid - jnp.mean(hid, axis=-1, keepdims=True)
        var = jnp.mean(hc * hc, axis=-1, keepdims=True)
        sl = slice(h * ML_DV, (h + 1) * ML_DV)
        y = hc * lax.rsqrt(var + EPS) * ng[:, sl] * og_ref[:, sl].astype(F32)
        y_ref[:, sl] = y.astype(BF16)

    @pl.when(c == pl.num_programs(1) - 1)
    def _():
        cfin_ref[...] = c_scr[...]
        mfin_ref[...] = m_scr[...]
        xfin_ref[...] = x_scr[...]


def _mlstm(proj, sg, sgt, conv_w, conv_b, norm_g, c0, m0, x0, nb, chunk):
    m = sg.shape[0]
    nc = m // nb // chunk
    slab = lambda s: pl.BlockSpec((None, chunk, D_MODEL), lambda b, c, s=s: (s, b * nc + c, 0))
    const2 = lambda b, c: (0, 0)
    return pl.pallas_call(
        functools.partial(_mlstm_kernel, chunk=chunk),
        grid=(nb, nc),
        in_specs=[
            slab(P_QKB), slab(P_VB), slab(P_OB),
            pl.BlockSpec((chunk, 128), lambda b, c: (b * nc + c, 0)),
            pl.BlockSpec((None, SUBLANES, chunk), lambda b, c: (b * nc + c, 0, 0)),
            pl.BlockSpec((4, D_MODEL), const2),
            pl.BlockSpec((1, D_MODEL), const2),
            pl.BlockSpec((1, D_MODEL), const2),
            pl.BlockSpec((ML_HEADS, ML_DK, ML_AUG), lambda b, c: (0, 0, 0)),
            pl.BlockSpec((SUBLANES, 128), const2),
            pl.BlockSpec((SUBLANES, D_MODEL), const2),
        ],
        out_specs=[
            pl.BlockSpec((chunk, D_MODEL), lambda b, c: (b * nc + c, 0)),
            pl.BlockSpec((None, ML_HEADS, ML_DK, ML_AUG), lambda b, c: (b, 0, 0, 0)),
            pl.BlockSpec((None, SUBLANES, 128), lambda b, c: (b, 0, 0)),
            pl.BlockSpec((None, SUBLANES, D_MODEL), lambda b, c: (b, 0, 0)),
        ],
        out_shape=[
            jax.ShapeDtypeStruct((m, D_MODEL), BF16),
            jax.ShapeDtypeStruct((nb, ML_HEADS, ML_DK, ML_AUG), F32),
            jax.ShapeDtypeStruct((nb, SUBLANES, 128), F32),
            jax.ShapeDtypeStruct((nb, SUBLANES, D_MODEL), F32),
        ],
        scratch_shapes=[
            pltpu.VMEM((ML_HEADS, ML_DK, ML_AUG), F32),
            pltpu.VMEM((SUBLANES, 128), F32),
            pltpu.VMEM((SUBLANES, D_MODEL), F32),
        ],
        compiler_params=_params("parallel", "arbitrary"),
        name="mlstm",
    )(proj, proj, proj, sg, sgt, conv_w, conv_b, norm_g, c0, m0, x0)


def _merge_kernel(x_ref, ya_ref, yb_ref, ma_ref, mb_ref, wa_ref, wb_ref, wo_ref, eg_ref, eb_ref, g1_ref, b1_ref,
                  wrh_ref, wrl_ref, rb_ref,
                  h1_ref, eidx_ref, gw_ref, rloc_ref, cnt_ref, carry_scr, val_scr, *, tm):
    i = pl.program_id(0)

    @pl.when(i == 0)
    def _():
        carry_scr[...] = jnp.zeros_like(carry_scr)

    h0 = _layer_norm(x_ref[...], eg_ref[...], eb_ref[...])
    merged = (ma_ref[...].astype(F32) * _dot(ya_ref[...], wa_ref[...])
              + mb_ref[...].astype(F32) * _dot(yb_ref[...], wb_ref[...]))
    mix = _dot(merged.astype(BF16), wo_ref[...])
    h1 = _layer_norm(DN_ALPHA * h0 + mix, g1_ref[...], b1_ref[...])
    h1_ref[...] = h1

    h_hi, h_lo = _split_bf16(h1)
    logits = _dot_nt(wrh_ref[...], h_hi) + _dot_nt(wrh_ref[...], h_lo) + _dot_nt(wrl_ref[...], h_hi)
    scores = _sigmoid(logits)
    biased = scores + rb_ref[:, 0:1]
    neg_inf = -jnp.inf

    g3 = biased.reshape(N_GROUPS, GROUP_SIZE, tm)
    sub3 = lax.broadcasted_iota(I32, g3.shape, 1)
    top1 = jnp.max(g3, axis=1, keepdims=True)
    first = jnp.min(jnp.where(g3 == top1, sub3, GROUP_SIZE), axis=1, keepdims=True)
    top2 = jnp.max(jnp.where(sub3 == first, neg_inf, g3), axis=1, keepdims=True)
    gs = (top1 + top2).reshape(N_GROUPS, tm)
    gi = lax.broadcasted_iota(I32, gs.shape, 0)
    grank = jnp.zeros(gs.shape, F32)
    for j in range(N_GROUPS):
        r = gs[j:j + 1, :]
        grank = grank + jnp.where((r > gs) | ((r == gs) & (gi > j)), 1.0, 0.0)
    gsel = grank < float(TOPK_GROUPS)
    emask = jnp.broadcast_to(gsel.reshape(N_GROUPS, 1, tm), (N_GROUPS, GROUP_SIZE, tm)).reshape(N_EXPERTS, tm)
    masked = jnp.where(emask, biased, neg_inf)

    val_scr[...] = masked
    ei = lax.broadcasted_iota(I32, masked.shape, 0)

    def rank_body(j, rank):
        r = val_scr[pl.ds(j, 1), :]
        return rank + jnp.where((r > masked) | ((r == masked) & (ei > j)), 1.0, 0.0)

    rank = lax.fori_loop(0, N_EXPERTS, rank_body, jnp.zeros(masked.shape, F32))
    sel = rank < float(TOP_K)
    sel_w = jnp.where(sel, scores, 0.0)
    gwd = sel_w / jnp.sum(sel_w, axis=0, keepdims=True) * ROUTED_SCALE

    tr = lax.broadcasted_iota(I32, (tm, tm), 0)
    tc = lax.broadcasted_iota(I32, (tm, tm), 1)
    upper = _ones_where(tr < tc)
    sel_b = _ones_where(sel)
    rloc = _dot(sel_b, upper) + carry_scr[:, 0:1]
    carry_scr[...] = carry_scr[...] + _dot(sel_b, jnp.ones((tm, 128), BF16))

    ef = ei.astype(F32)
    e_rows, w_rows, r_rows = [], [], []
    for kk in range(TOP_K):
        pick = sel & (rank == float(kk))
        e_rows.append(jnp.sum(jnp.where(pick, ef, 0.0), axis=0, keepdims=True))
        w_rows.append(jnp.sum(jnp.where(pick, gwd, 0.0), axis=0, keepdims=True))
        r_rows.append(jnp.sum(jnp.where(pick, rloc, 0.0), axis=0, keepdims=True))
    eidx_ref[...] = jnp.concatenate(e_rows, axis=0).astype(I32)
    gw_ref[...] = jnp.concatenate(w_rows, axis=0)
    rloc_ref[...] = jnp.concatenate(r_rows, axis=0).astype(I32)

    @pl.when(i == pl.num_programs(0) - 1)
    def _():
        cnt_ref[...] = carry_scr[...]


def _merge(x2d, ya, yb, proj, w_a, w_b, w_o, eg, eb, g1, b1, wr_hi, wr_lo, rbias, tm):
    m = x2d.shape[0]
    tile = lambda i: (i, 0)
    const = lambda i: (0, 0)
    slab = lambda s: pl.BlockSpec((None, tm, D_MODEL), lambda i, s=s: (s, i, 0))
    wspec = pl.BlockSpec((D_MODEL, D_MODEL), const)
    vspec = pl.BlockSpec((1, D_MODEL), const)
    lane_tile = pl.BlockSpec((TOP_K, tm), lambda i: (0, i))
    return pl.pallas_call(
        functools.partial(_merge_kernel, tm=tm),
        grid=(m // tm,),
        in_specs=[
            pl.BlockSpec((tm, D_MODEL), tile), pl.BlockSpec((tm, D_MODEL), tile), pl.BlockSpec((tm, D_MODEL), tile),
            slab(P_MA), slab(P_MB), wspec, wspec, wspec, vspec, vspec, vspec, vspec,
            pl.BlockSpec((N_EXPERTS, D_MODEL), const), pl.BlockSpec((N_EXPERTS, D_MODEL), const),
            pl.BlockSpec((N_EXPERTS, 128), const),
        ],
        out_specs=[
            pl.BlockSpec((tm, D_MODEL), tile), lane_tile, lane_tile, lane_tile,
            pl.BlockSpec((N_EXPERTS, 128), const),
        ],
        out_shape=[
            jax.ShapeDtypeStruct((m, D_MODEL), F32),
            jax.ShapeDtypeStruct((TOP_K, m), I32),
            jax.ShapeDtypeStruct((TOP_K, m), F32),
            jax.ShapeDtypeStruct((TOP_K, m), I32),
            jax.ShapeDtypeStruct((N_EXPERTS, 128), F32),
        ],
        scratch_shapes=[pltpu.VMEM((N_EXPERTS, 128), F32), pltpu.VMEM((N_EXPERTS, tm), F32)],
        compiler_params=_params("arbitrary"),
        name="merge_router",
    )(x2d, ya, yb, proj, proj, w_a, w_b, w_o, eg, eb, g1, b1, wr_hi, wr_lo, rbias)


def _dispatch_kernel(dest_ref, h_ref, xs_in_ref, xs_ref, sem, *, tm):
    del xs_in_ref

    def row_copy(t, kk):
        d = dest_ref[0, 0, kk * tm + t]
        return pltpu.make_async_copy(h_ref.at[pl.ds(t, 1), :], xs_ref.at[pl.ds(d, 1), :], sem)

    def issue(t, carry):
        for kk in range(TOP_K):
            row_copy(t, kk).start()
        return carry

    def drain(t, carry):
        for kk in range(TOP_K):
            row_copy(t, kk).wait()
        return carry

    lax.fori_loop(0, tm, issue, 0)
    lax.fori_loop(0, tm, drain, 0)


def _dispatch(dest_tiles, h1, xs_init, tm):
    m = h1.shape[0]
    return pl.pallas_call(
        functools.partial(_dispatch_kernel, tm=tm),
        grid=(m // tm,),
        in_specs=[
            pl.BlockSpec((1, 1, TOP_K * tm), lambda i: (i, 0, 0), memory_space=pltpu.SMEM),
            pl.BlockSpec((tm, D_MODEL), lambda i: (i, 0)),
            pl.BlockSpec(memory_space=pl.ANY),
        ],
        out_specs=pl.BlockSpec(memory_space=pl.ANY),
        out_shape=jax.ShapeDtypeStruct(xs_init.shape, F32),
        scratch_shapes=[pltpu.SemaphoreType.DMA(())],
        input_output_aliases={2: 0},
        compiler_params=_params("arbitrary"),
        name="dispatch",
    )(dest_tiles, h1, xs_init)


def _experts_kernel(be_ref, nu_ref, x_ref, wg_ref, wu_ref, wd_ref, y_ref):
    del be_ref
    i = pl.program_id(0)

    @pl.when(i < nu_ref[0])
    def _():
        xb = x_ref[...].astype(BF16)
        a = _dot(xb, wg_ref[...])
        u = _dot(xb, wu_ref[...])
        y_ref[...] = _dot((a * _sigmoid(a) * u).astype(BF16), wd_ref[...])

    @pl.when(i >= nu_ref[0])
    def _():
        y_ref[...] = jnp.zeros_like(y_ref)


def _experts(blk_expert, n_used, xs, wg, wu, wd):
    n_slots = xs.shape[0]
    n_blocks = n_slots // MOE_BLOCK
    grid_spec = pltpu.PrefetchScalarGridSpec(
        num_scalar_prefetch=2,
        grid=(n_blocks,),
        in_specs=[
            pl.BlockSpec((MOE_BLOCK, D_MODEL), lambda i, be, nu: (i, 0)),
            pl.BlockSpec((None, D_MODEL, D_EXPERT), lambda i, be, nu: (be[i], 0, 0)),
            pl.BlockSpec((None, D_MODEL, D_EXPERT), lambda i, be, nu: (be[i], 0, 0)),
            pl.BlockSpec((None, D_EXPERT, D_MODEL), lambda i, be, nu: (be[i], 0, 0)),
        ],
        out_specs=pl.BlockSpec((MOE_BLOCK, D_MODEL), lambda i, be, nu: (i, 0)),
    )
    return pl.pallas_call(
        _experts_kernel,
        grid_spec=grid_spec,
        out_shape=jax.ShapeDtypeStruct((n_slots, D_MODEL), F32),
        compiler_params=_params("arbitrary"),
        name="experts",
    )(blk_expert, n_used, xs, wg, wu, wd)


def _combine_kernel(dest_ref, h_ref, gw_ref, y_ref, sg_ref, su_ref, sd_ref, g2_ref, b2_ref, o_ref, ybuf, sem, *, tm):
    def row_copy(t, kk):
        d = dest_ref[0, 0, kk * tm + t]
        return pltpu.make_async_copy(y_ref.at[pl.ds(d, 1), :], ybuf.at[kk, pl.ds(t, 1), :], sem)

    def issue(t, carry):
        for kk in range(TOP_K):
            row_copy(t, kk).start()
        return carry

    def drain(t, carry):
        for kk in range(TOP_K):
            row_copy(t, kk).wait()
        return carry

    lax.fori_loop(0, tm, issue, 0)
    h1 = h_ref[...]
    hb = h1.astype(BF16)
    a = _dot(hb, sg_ref[...])
    u = _dot(hb, su_ref[...])
    shared = _dot((a * _sigmoid(a) * u).astype(BF16), sd_ref[...])
    lax.fori_loop(0, tm, drain, 0)
    gw = gw_ref[...]
    routed = gw[:, 0:1] * ybuf[0]
    for kk in range(1, TOP_K):
        routed = routed + gw[:, kk:kk + 1] * ybuf[kk]
    o_ref[...] = _layer_norm(DN_ALPHA * h1 + (routed + shared), g2_ref[...], b2_ref[...])


def _combine(dest_tiles, h1, gw_t, y, wsg, wsu, wsd, g2, b2, tm):
    m = h1.shape[0]
    const = lambda i: (0, 0)
    return pl.pallas_call(
        functools.partial(_combine_kernel, tm=tm),
        grid=(m // tm,),
        in_specs=[
            pl.BlockSpec((1, 1, TOP_K * tm), lambda i: (i, 0, 0), memory_space=pltpu.SMEM),
            pl.BlockSpec((tm, D_MODEL), lambda i: (i, 0)),
            pl.BlockSpec((tm, TOP_K), lambda i: (i, 0)),
            pl.BlockSpec(memory_space=pl.ANY),
            pl.BlockSpec((D_MODEL, D_EXPERT), const),
            pl.BlockSpec((D_MODEL, D_EXPERT), const),
            pl.BlockSpec((D_EXPERT, D_MODEL), const),
            pl.BlockSpec((1, D_MODEL), const),
            pl.BlockSpec((1, D_MODEL), const),
        ],
        out_specs=pl.BlockSpec((tm, D_MODEL), lambda i: (i, 0)),
        out_shape=jax.ShapeDtypeStruct((m, D_MODEL), F32),
        scratch_shapes=[pltpu.VMEM((TOP_K, tm, D_MODEL), F32), pltpu.SemaphoreType.DMA(())],
        compiler_params=_params("arbitrary"),
        name="combine",
    )(dest_tiles, h1, gw_t, y, wsg, wsu, wsd, g2, b2)


def _pick_tile(m, pref):
    t = min(pref, m)
    while m % t:
        t //= 2
    return t


def _tiles_of(a, tm):
    kk, m = a.shape
    return a.reshape(kk, m // tm, tm).transpose(1, 0, 2).reshape(m // tm, 1, kk * tm)


def _gates_time_on_lanes(sg, chunk):
    m = sg.shape[0]
    return sg[:, :SUBLANES].reshape(m // chunk, chunk, SUBLANES).transpose(0, 2, 1)


def _forward(x, meta_tokens, ln_emb_g, ln_emb_b, w_in, hg_lb_logits, hg_norm_g, ml_conv_w, ml_conv_b,
             ml_ig_bias, ml_fg_bias, ml_norm_g, w_branch_a, w_branch_b, w_out, ln1_g, ln1_b,
             w_router, router_bias, w_exp_gate, w_exp_up, w_exp_down, w_sh_gate, w_sh_up, w_sh_down,
             ln2_g, ln2_b, *, chunk, tm_proj, tm_merge, tm_moe):
    nb, seq, d = x.shape
    m = nb * seq
    row = lambda a: a.reshape(1, -1).astype(F32)

    w = w_in[0]
    kw = HG_HEADS * HG_DK
    o_qa, o_fa, o_ia, o_ga = 0, kw, 2 * kw, 3 * kw
    o_qb = 4 * kw
    o_kb = o_qb + ML_HEADS * ML_DK
    o_vb = o_kb + ML_HEADS * ML_DK
    o_ob = o_vb + ML_HEADS * ML_DV
    o_ig = o_ob + ML_HEADS * ML_DV
    o_fg = o_ig + ML_HEADS
    o_ma = o_fg + ML_HEADS
    o_mb = o_ma + D_MODEL
    cols = lambda o, n: w[:, o:o + n]
    w_cat = jnp.concatenate([
        cols(o_qa, kw), cols(o_fa, kw), cols(o_ia, kw), cols(o_ga, kw),
        cols(o_qb, 2 * ML_HEADS * ML_DK), cols(o_vb, ML_HEADS * ML_DV), cols(o_ob, ML_HEADS * ML_DV),
        cols(o_ma, D_MODEL), cols(o_mb, D_MODEL)], axis=1).astype(BF16)
    w_small = jnp.pad(cols(o_ig, 2 * ML_HEADS), ((0, 0), (0, 128 - 2 * ML_HEADS))).astype(BF16)
    gate_bias = jnp.pad(jnp.concatenate([ml_ig_bias[0], ml_fg_bias[0]]).astype(F32), (0, 128 - 2 * ML_HEADS)).reshape(1, 128)
    lb = jax.nn.softmax(hg_lb_logits.astype(F32), axis=0)[0].reshape(1, -1)
    eg, eb = row(ln_emb_g), row(ln_emb_b)
    conv_w = ml_conv_w[0].astype(F32)
    conv_b = row(ml_conv_b[0])
    hgn, mln = row(hg_norm_g[0]), row(ml_norm_g[0])

    p_m, lf_m, sg_m = _inproj(meta_tokens.astype(F32), eg, eb, w_cat, w_small, lb, gate_bias, N_META)
    s_zero = jnp.zeros((HG_HEADS, HG_DK, HG_DK), F32)
    _, s0 = _hgrn(p_m, lf_m, hgn, s_zero, 1, N_META)
    sgt_m = _gates_time_on_lanes(sg_m, N_META)
    _, c0, m0, x0 = _mlstm(p_m, sg_m, sgt_m, conv_w, conv_b, mln,
                           jnp.zeros((ML_HEADS, ML_DK, ML_AUG), F32), jnp.zeros((SUBLANES, 128), F32),
                           jnp.zeros((SUBLANES, D_MODEL), F32), 1, N_META)

    x2d = x.reshape(m, d).astype(F32)
    proj, logf, sg = _inproj(x2d, eg, eb, w_cat, w_small, lb, gate_bias, tm_proj)
    ya, _ = _hgrn(proj, logf, hgn, s0[0], nb, chunk)
    sgt = _gates_time_on_lanes(sg, chunk)
    yb, _, _, _ = _mlstm(proj, sg, sgt, conv_w, conv_b, mln, c0[0], m0[0], x0[0], nb, chunk)

    wr = w_router[0].T.astype(F32)
    wr_hi, wr_lo = _split_bf16(wr)
    rbias = jnp.broadcast_to(router_bias[0].astype(F32).reshape(N_EXPERTS, 1), (N_EXPERTS, 128))
    h1, eidx, gw, rloc, cnt = _merge(
        x2d, ya, yb, proj, w_branch_a[0].astype(BF16), w_branch_b[0].astype(BF16), w_out[0].astype(BF16),
        eg, eb, row(ln1_g[0]), row(ln1_b[0]), wr_hi, wr_lo, rbias, tm_merge)

    counts = cnt[:, 0].astype(I32)
    padded = (counts + MOE_BLOCK - 1) // MOE_BLOCK * MOE_BLOCK
    pends = jnp.cumsum(padded)
    pstarts = pends - padded
    dest = pstarts[eidx] + rloc
    n_blocks = m * TOP_K // MOE_BLOCK + N_EXPERTS
    blk_start = jnp.arange(n_blocks, dtype=I32) * MOE_BLOCK
    blk_expert = jnp.minimum(jnp.searchsorted(pends, blk_start, side='right'), N_EXPERTS - 1).astype(I32)
    n_used = (pends[-1:] // MOE_BLOCK).astype(I32)
    dest_tiles = _tiles_of(dest, tm_moe)

    xs = _dispatch(dest_tiles, h1, jnp.zeros((n_blocks * MOE_BLOCK, D_MODEL), F32), tm_moe)
    y = _experts(blk_expert, n_used, xs, w_exp_gate[0].astype(BF16), w_exp_up[0].astype(BF16),
                 w_exp_down[0].astype(BF16))
    out = _combine(dest_tiles, h1, gw.T, y, w_sh_gate[0].astype(BF16), w_sh_up[0].astype(BF16),
                   w_sh_down[0].astype(BF16), row(ln2_g[0]), row(ln2_b[0]), tm_moe)
    return out.reshape(nb, seq, d).astype(x.dtype)


def kernel(x, meta_tokens, ln_emb_g, ln_emb_b, w_in, hg_lb_logits, hg_norm_g, ml_conv_w, ml_conv_b, ml_ig_bias, ml_fg_bias, ml_norm_g, w_branch_a, w_branch_b, w_out, ln1_g, ln1_b, w_router, router_bias, w_exp_gate, w_exp_up, w_exp_down, w_sh_gate, w_sh_up, w_sh_down, ln2_g, ln2_b):
    m = x.shape[0] * x.shape[1]
    return _forward(x, meta_tokens, ln_emb_g, ln_emb_b, w_in, hg_lb_logits, hg_norm_g, ml_conv_w, ml_conv_b,
                    ml_ig_bias, ml_fg_bias, ml_norm_g, w_branch_a, w_branch_b, w_out, ln1_g, ln1_b,
                    w_router, router_bias, w_exp_gate, w_exp_up, w_exp_down, w_sh_gate, w_sh_up, w_sh_down,
                    ln2_g, ln2_b, chunk=_pick_tile(x.shape[1], 64), tm_proj=_pick_tile(m, 1024),
                    tm_merge=_pick_tile(m, 512), tm_moe=_pick_tile(m, 128))
```

```python
import functools

import jax
import jax.numpy as jnp
from jax import lax
from jax.experimental import pallas as pl
from jax.experimental.pallas import tpu as pltpu

F32, BF16, I32 = jnp.float32, jnp.bfloat16, jnp.int32

D_MODEL = 1024
N_META = 16
HG_HEADS = 8
HG_DK = 128
ML_HEADS = 4
ML_DK = 128
ML_DV = 256
ML_AUG = ML_DV + 128
N_EXPERTS = 64
TOP_K = 8
N_GROUPS = 8
GROUP_SIZE = N_EXPERTS // N_GROUPS
TOPK_GROUPS = 4
D_EXPERT = 256
ROUTED_SCALE = 2.5
MOE_BLOCK = 512
DN_ALPHA = 2.0 ** 0.25
EPS = 1e-5
EXP_CLAMP = 80.0
SUBLANES = 8

P_QA, P_KA, P_IA, P_GA, P_QKB, P_VB, P_OB, P_MA, P_MB = range(9)
N_SLABS = 9

VMEM_LIMIT = 56 * 1024 * 1024


def _params(*sem):
    return pltpu.CompilerParams(dimension_semantics=sem, vmem_limit_bytes=VMEM_LIMIT)


def _sigmoid(x):
    return 1.0 / (1.0 + jnp.exp(-x))


def _log_sigmoid(x):
    return jnp.minimum(x, 0.0) - jnp.log(1.0 + jnp.exp(-jnp.abs(x)))


def _layer_norm(x, g, b):
    xc = x - jnp.mean(x, axis=-1, keepdims=True)
    var = jnp.mean(xc * xc, axis=-1, keepdims=True)
    return xc * lax.rsqrt(var + EPS) * g + b


def _dot(a, b):
    return jnp.dot(a, b, preferred_element_type=F32)


def _dot_nt(a, b):
    return lax.dot_general(a, b, (((1,), (1,)), ((), ())), preferred_element_type=F32)


def _dot_tn(a, b):
    return lax.dot_general(a, b, (((0,), (0,)), ((), ())), preferred_element_type=F32)


def _split_bf16(x):
    hi = x.astype(BF16)
    lo = (x - hi.astype(F32)).astype(BF16)
    return hi, lo


def _ones_where(cond):
    return jnp.where(cond, 1.0, 0.0).astype(BF16)


def _inproj_kernel(x_ref, g_ref, b_ref, w_ref, ws_ref, lb_ref, gb_ref, p_ref, lf_ref, sg_ref, h_scr):
    n = pl.program_id(1)

    @pl.when(n == 0)
    def _():
        hb = _layer_norm(x_ref[...], g_ref[...], b_ref[...]).astype(BF16)
        h_scr[...] = hb
        s = _dot(hb, ws_ref[...]) + gb_ref[...]
        lane = lax.broadcasted_iota(I32, s.shape, 1)
        sg_ref[...] = jnp.where(lane < ML_HEADS, s, _log_sigmoid(s))

    acc = _dot(h_scr[...], w_ref[...])

    @pl.when((n == P_QA) | (n == P_IA) | (n == P_QKB) | (n == P_VB))
    def _():
        p_ref[...] = acc.astype(BF16)

    @pl.when(n == P_KA)
    def _():
        lb = lb_ref[...]
        f = lb + (1.0 - lb) * _sigmoid(acc)
        lf_ref[...] = jnp.log(f)
        p_ref[...] = (1.0 - f).astype(BF16)

    @pl.when(n == P_GA)
    def _():
        p_ref[...] = (acc * _sigmoid(acc)).astype(BF16)

    @pl.when(n >= P_OB)
    def _():
        p_ref[...] = _sigmoid(acc).astype(BF16)


def _inproj(x2d, ln_g, ln_b, w_cat, w_small, lb, gate_bias, tm):
    m = x2d.shape[0]
    grid = (m // tm, N_SLABS)
    row = lambda i, n: (0, 0)
    return pl.pallas_call(
        _inproj_kernel,
        grid=grid,
        in_specs=[
            pl.BlockSpec((tm, D_MODEL), lambda i, n: (i, 0)),
            pl.BlockSpec((1, D_MODEL), row),
            pl.BlockSpec((1, D_MODEL), row),
            pl.BlockSpec((D_MODEL, D_MODEL), lambda i, n: (0, n)),
            pl.BlockSpec((D_MODEL, 128), row),
            pl.BlockSpec((1, D_MODEL), row),
            pl.BlockSpec((1, 128), row),
        ],
        out_specs=[
            pl.BlockSpec((None, tm, D_MODEL), lambda i, n: (n, i, 0)),
            pl.BlockSpec((tm, D_MODEL), lambda i, n: (i, 0)),
            pl.BlockSpec((tm, 128), lambda i, n: (i, 0)),
        ],
        out_shape=[
            jax.ShapeDtypeStruct((N_SLABS, m, D_MODEL), BF16),
            jax.ShapeDtypeStruct((m, D_MODEL), F32),
            jax.ShapeDtypeStruct((m, 128), F32),
        ],
        scratch_shapes=[pltpu.VMEM((tm, D_MODEL), BF16)],
        compiler_params=_params("parallel", "arbitrary"),
        name="inproj",
    )(x2d, ln_g, ln_b, w_cat, w_small, lb, gate_bias)


def _block_rows(b, block, pick):
    c, w = b.shape
    parts = [jnp.broadcast_to(b[j * block + pick:j * block + pick + 1, :], (block, w))
             for j in range(c // block)]
    return parts[0] if len(parts) == 1 else jnp.concatenate(parts, axis=0)


def _hgrn_kernel(q_ref, k_ref, v_ref, g_ref, lf_ref, ng_ref, s0_ref, y_ref, sfin_ref, s_scr, *, chunk):
    c = pl.program_id(1)

    @pl.when(c == 0)
    def _():
        s_scr[...] = s0_ref[...]

    cs = chunk
    row = lax.broadcasted_iota(I32, (cs, cs), 0)
    col = lax.broadcasted_iota(I32, (cs, cs), 1)
    tri = _ones_where(col <= row)
    lf_hi, lf_lo = _split_bf16(lf_ref[...])
    b = _dot(tri, lf_hi) + _dot(tri, lf_lo)
    q = q_ref[...].astype(F32)
    k = k_ref[...].astype(F32)
    v = v_ref[...]
    blast = b[cs - 1:cs, :]
    qg = (q * jnp.exp(b)).astype(BF16)
    kg = (k * jnp.exp(blast - b)).astype(BF16)
    dec = jnp.exp(blast)

    levels = []
    m = SUBLANES
    while 2 * m <= cs:
        w = jnp.exp(-jnp.abs(b - _block_rows(b, 2 * m, m - 1)))
        sh = (2 * m).bit_length() - 1
        mask = ((row >> sh) == (col >> sh)) & ((row & (2 * m - 1)) >= m) & ((col & (2 * m - 1)) < m)
        levels.append(((q * w).astype(BF16), (k * w).astype(BF16), mask))
        m *= 2
    e = jnp.clip(b - _block_rows(b, SUBLANES, SUBLANES // 2 - 1), -EXP_CLAMP, EXP_CLAMP)
    levels.append(((q * jnp.exp(e)).astype(BF16), (k * jnp.exp(-e)).astype(BF16),
                   ((row >> 3) == (col >> 3)) & (col <= row)))

    ng = ng_ref[...]
    for h in range(HG_HEADS):
        sl = slice(h * HG_DK, (h + 1) * HG_DK)
        st = s_scr[h]
        o = _dot_nt(qg[:, sl], st.astype(BF16))
        sc = jnp.zeros((cs, cs), F32)
        for lq, lk, mask in levels:
            sc = jnp.where(mask, _dot_nt(lq[:, sl], lk[:, sl]), sc)
        o = o + _dot(sc.astype(BF16), v[:, sl])
        s_scr[h] = dec[:, sl] * st + _dot_tn(v[:, sl], kg[:, sl])
        ms = jnp.mean(o * o, axis=-1, keepdims=True)
        y = o * lax.rsqrt(ms + EPS) * ng[:, sl] * g_ref[:, sl].astype(F32)
        y_ref[:, sl] = y.astype(BF16)

    @pl.when(c == pl.num_programs(1) - 1)
    def _():
        sfin_ref[...] = s_scr[...]


def _hgrn(proj, logf, norm_g, s0, nb, chunk):
    m = logf.shape[0]
    nc = m // nb // chunk
    slab = lambda s: pl.BlockSpec((None, chunk, D_MODEL), lambda b, c, s=s: (s, b * nc + c, 0))
    return pl.pallas_call(
        functools.partial(_hgrn_kernel, chunk=chunk),
        grid=(nb, nc),
        in_specs=[
            slab(P_QA), slab(P_KA), slab(P_IA), slab(P_GA),
            pl.BlockSpec((chunk, D_MODEL), lambda b, c: (b * nc + c, 0)),
            pl.BlockSpec((1, D_MODEL), lambda b, c: (0, 0)),
            pl.BlockSpec((HG_HEADS, HG_DK, HG_DK), lambda b, c: (0, 0, 0)),
        ],
        out_specs=[
            pl.BlockSpec((chunk, D_MODEL), lambda b, c: (b * nc + c, 0)),
            pl.BlockSpec((None, HG_HEADS, HG_DK, HG_DK), lambda b, c: (b, 0, 0, 0)),
        ],
        out_shape=[
            jax.ShapeDtypeStruct((m, D_MODEL), BF16),
            jax.ShapeDtypeStruct((nb, HG_HEADS, HG_DK, HG_DK), F32),
        ],
        scratch_shapes=[pltpu.VMEM((HG_HEADS, HG_DK, HG_DK), F32)],
        compiler_params=_params("parallel", "arbitrary"),
        name="hgrn",
    )(proj, proj, proj, proj, logf, norm_g, s0)


def _mlstm_kernel(qk_ref, v_ref, og_ref, sg_ref, sgt_ref, cw_ref, cb_ref, ng_ref, c0_ref, m0_ref, x0_ref,
                  y_ref, cfin_ref, mfin_ref, xfin_ref, c_scr, m_scr, x_scr, *, chunk):
    c = pl.program_id(1)

    @pl.when(c == 0)
    def _():
        c_scr[...] = c0_ref[...]
        m_scr[...] = m0_ref[...]
        x_scr[...] = x0_ref[...]

    cs = chunk
    x = qk_ref[...].astype(F32)
    prev = x_scr[...]
    sub = lax.broadcasted_iota(I32, (SUBLANES, D_MODEL), 0)
    cw = cw_ref[...]
    conv = cw[3:4, :] * x + cb_ref[...]
    for j in (1, 2, 3):
        xs = pltpu.roll(x, j, 0)
        head = jnp.where(sub < j, pltpu.roll(prev, j, 0), xs[:SUBLANES, :])
        xs = jnp.concatenate([head, xs[SUBLANES:, :]], axis=0)
        conv = conv + cw[3 - j:4 - j, :] * xs
    x_scr[...] = x[cs - SUBLANES:, :]
    qk = conv * _sigmoid(conv)
    q_all = (qk[:, :ML_HEADS * ML_DK] * (ML_DK ** -0.5)).astype(BF16)
    k_all = qk[:, ML_HEADS * ML_DK:]

    row = lax.broadcasted_iota(I32, (cs, cs), 0)
    col = lax.broadcasted_iota(I32, (cs, cs), 1)
    causal = col <= row
    tri = _ones_where(causal)
    sg = sg_ref[...]
    sgt = sgt_ref[...]
    sg_hi, sg_lo = _split_bf16(sg)
    bcol_all = _dot(tri, sg_hi) + _dot(tri, sg_lo)
    sgt_hi, sgt_lo = _split_bf16(sgt)
    brow_all = _dot_nt(sgt_hi, tri) + _dot_nt(sgt_lo, tri)
    lane128 = lax.broadcasted_iota(I32, (cs, 128), 1)
    ones_col = _ones_where(lane128 == 0)
    v = v_ref[...]
    ng = ng_ref[...]

    for h in range(ML_HEADS):
        b_col = bcol_all[:, ML_HEADS + h:ML_HEADS + h + 1]
        b_row = brow_all[ML_HEADS + h:ML_HEADS + h + 1, :]
        ig_col = sg[:, h:h + 1]
        ig_row = sgt[h:h + 1, :]
        m_prev = m_scr[h:h + 1, 0:1]
        q_h = q_all[:, h * ML_DK:(h + 1) * ML_DK]
        k_h = k_all[:, h * ML_DK:(h + 1) * ML_DK]
        v_aug = jnp.concatenate([v[:, h * ML_DV:(h + 1) * ML_DV], ones_col], axis=1)
        c_st = c_scr[h]

        log_intra = jnp.where(causal, b_col - b_row + ig_row, -jnp.inf)
        log_inter = b_col + m_prev
        m_t = jnp.maximum(log_inter, jnp.max(log_intra, axis=-1, keepdims=True))
        w_intra = jnp.exp(log_intra - m_t)
        w_inter = jnp.exp(log_inter - m_t)
        s = _dot_nt(q_h, k_h.astype(BF16)) * w_intra
        tot = w_inter * _dot(q_h, c_st.astype(BF16)) + _dot(s.astype(BF16), v_aug)
        num = tot[:, :ML_DV]
        den = tot[:, ML_DV:ML_DV + 1]
        hid = num / jnp.maximum(jnp.abs(den), jnp.exp(-m_t))

        b_last = b_col[cs - 1:cs, :]
        log_w = b_last - b_col + ig_col
        m_new = jnp.maximum(b_last + m_prev, jnp.max(log_w, axis=0, keepdims=True))
        w_s = jnp.exp(log_w - m_new)
        decay = jnp.exp(b_last + m_prev - m_new)
        c_scr[h] = decay * c_st + _dot_tn((k_h * w_s).astype(BF16), v_aug)
        m_scr[h:h + 1, :] = jnp.broadcast_to(m_new, (1, 128))

        hc = hid - jnp.mean(hid, axis=-1, keepdims=True)
        var = jnp.mean(hc * hc, axis=-1, keepdims=True)
        sl = slice(h * ML_DV, (h + 1) * ML_DV)
        y = hc * lax.rsqrt(var + EPS) * ng[:, sl] * og_ref[:, sl].astype(F32)
        y_ref[:, sl] = y.astype(BF16)

    @pl.when(c == pl.num_programs(1) - 1)
    def _():
        cfin_ref[...] = c_scr[...]
        mfin_ref[...] = m_scr[...]
        xfin_ref[...] = x_scr[...]


def _mlstm(proj, sg, sgt, conv_w, conv_b, norm_g, c0, m0, x0, nb, chunk):
    m = sg.shape[0]
    nc = m // nb // chunk
    slab = lambda s: pl.BlockSpec((None, chunk, D_MODEL), lambda b, c, s=s: (s, b * nc + c, 0))
    const2 = lambda b, c: (0, 0)
    return pl.pallas_call(
        functools.partial(_mlstm_kernel, chunk=chunk),
        grid=(nb, nc),
        in_specs=[
            slab(P_QKB), slab(P_VB), slab(P_OB),
            pl.BlockSpec((chunk, 128), lambda b, c: (b * nc + c, 0)),
            pl.BlockSpec((None, SUBLANES, chunk), lambda b, c: (b * nc + c, 0, 0)),
            pl.BlockSpec((4, D_MODEL), const2),
            pl.BlockSpec((1, D_MODEL), const2),
            pl.BlockSpec((1, D_MODEL), const2),
            pl.BlockSpec((ML_HEADS, ML_DK, ML_AUG), lambda b, c: (0, 0, 0)),
            pl.BlockSpec((SUBLANES, 128), const2),
            pl.BlockSpec((SUBLANES, D_MODEL), const2),
        ],
        out_specs=[
            pl.BlockSpec((chunk, D_MODEL), lambda b, c: (b * nc + c, 0)),
            pl.BlockSpec((None, ML_HEADS, ML_DK, ML_AUG), lambda b, c: (b, 0, 0, 0)),
            pl.BlockSpec((None, SUBLANES, 128), lambda b, c: (b, 0, 0)),
            pl.BlockSpec((None, SUBLANES, D_MODEL), lambda b, c: (b, 0, 0)),
        ],
        out_shape=[
            jax.ShapeDtypeStruct((m, D_MODEL), BF16),
            jax.ShapeDtypeStruct((nb, ML_HEADS, ML_DK, ML_AUG), F32),
            jax.ShapeDtypeStruct((nb, SUBLANES, 128), F32),
            jax.ShapeDtypeStruct((nb, SUBLANES, D_MODEL), F32),
        ],
        scratch_shapes=[
            pltpu.VMEM((ML_HEADS, ML_DK, ML_AUG), F32),
            pltpu.VMEM((SUBLANES, 128), F32),
            pltpu.VMEM((SUBLANES, D_MODEL), F32),
        ],
        compiler_params=_params("parallel", "arbitrary"),
        name="mlstm",
    )(proj, proj, proj, sg, sgt, conv_w, conv_b, norm_g, c0, m0, x0)


def _merge_kernel(x_ref, ya_ref, yb_ref, ma_ref, mb_ref, wa_ref, wb_ref, wo_ref, eg_ref, eb_ref, g1_ref, b1_ref,
                  wrh_ref, wrl_ref, rb_ref,
                  h1_ref, eidx_ref, gw_ref, rloc_ref, cnt_ref, carry_scr, val_scr, *, tm):
    i = pl.program_id(0)

    @pl.when(i == 0)
    def _():
        carry_scr[...] = jnp.zeros_like(carry_scr)

    h0 = _layer_norm(x_ref[...], eg_ref[...], eb_ref[...])
    merged = (ma_ref[...].astype(F32) * _dot(ya_ref[...], wa_ref[...])
              + mb_ref[...].astype(F32) * _dot(yb_ref[...], wb_ref[...]))
    mix = _dot(merged.astype(BF16), wo_ref[...])
    h1 = _layer_norm(DN_ALPHA * h0 + mix, g1_ref[...], b1_ref[...])
    h1_ref[...] = h1

    h_hi, h_lo = _split_bf16(h1)
    logits = _dot_nt(wrh_ref[...], h_hi) + _dot_nt(wrh_ref[...], h_lo) + _dot_nt(wrl_ref[...], h_hi)
    scores = _sigmoid(logits)
    biased = scores + rb_ref[:, 0:1]
    neg_inf = -jnp.inf

    g3 = biased.reshape(N_GROUPS, GROUP_SIZE, tm)
    sub3 = lax.broadcasted_iota(I32, g3.shape, 1)
    top1 = jnp.max(g3, axis=1, keepdims=True)
    first = jnp.min(jnp.where(g3 == top1, sub3, GROUP_SIZE), axis=1, keepdims=True)
    top2 = jnp.max(jnp.where(sub3 == first, neg_inf, g3), axis=1, keepdims=True)
    gs = (top1 + top2).reshape(N_GROUPS, tm)
    gi = lax.broadcasted_iota(I32, gs.shape, 0)
    grank = jnp.zeros(gs.shape, F32)
    for j in range(N_GROUPS):
        r = gs[j:j + 1, :]
        grank = grank + jnp.where((r > gs) | ((r == gs) & (gi > j)), 1.0, 0.0)
    gsel = grank < float(TOPK_GROUPS)
    emask = jnp.broadcast_to(gsel.reshape(N_GROUPS, 1, tm), (N_GROUPS, GROUP_SIZE, tm)).reshape(N_EXPERTS, tm)
    masked = jnp.where(emask, biased, neg_inf)

    val_scr[...] = masked
    ei = lax.broadcasted_iota(I32, masked.shape, 0)

    def rank_body(j, rank):
        r = val_scr[pl.ds(j, 1), :]
        return rank + jnp.where((r > masked) | ((r == masked) & (ei > j)), 1.0, 0.0)

    rank = lax.fori_loop(0, N_EXPERTS, rank_body, jnp.zeros(masked.shape, F32))
    sel = rank < float(TOP_K)
    sel_w = jnp.where(sel, scores, 0.0)
    gwd = sel_w / jnp.sum(sel_w, axis=0, keepdims=True) * ROUTED_SCALE

    tr = lax.broadcasted_iota(I32, (tm, tm), 0)
    tc = lax.broadcasted_iota(I32, (tm, tm), 1)
    upper = _ones_where(tr < tc)
    sel_b = _ones_where(sel)
    rloc = _dot(sel_b, upper) + carry_scr[:, 0:1]
    carry_scr[...] = carry_scr[...] + _dot(sel_b, jnp.ones((tm, 128), BF16))

    ef = ei.astype(F32)
    e_rows, w_rows, r_rows = [], [], []
    for kk in range(TOP_K):
        pick = sel & (rank == float(kk))
        e_rows.append(jnp.sum(jnp.where(pick, ef, 0.0), axis=0, keepdims=True))
        w_rows.append(jnp.sum(jnp.where(pick, gwd, 0.0), axis=0, keepdims=True))
        r_rows.append(jnp.sum(jnp.where(pick, rloc, 0.0), axis=0, keepdims=True))
    eidx_ref[...] = jnp.concatenate(e_rows, axis=0).astype(I32)
    gw_ref[...] = jnp.concatenate(w_rows, axis=0)
    rloc_ref[...] = jnp.concatenate(r_rows, axis=0).astype(I32)

    @pl.when(i == pl.num_programs(0) - 1)
    def _():
        cnt_ref[...] = carry_scr[...]


def _merge(x2d, ya, yb, proj, w_a, w_b, w_o, eg, eb, g1, b1, wr_hi, wr_lo, rbias, tm):
    m = x2d.shape[0]
    tile = lambda i: (i, 0)
    const = lambda i: (0, 0)
    slab = lambda s: pl.BlockSpec((None, tm, D_MODEL), lambda i, s=s: (s, i, 0))
    wspec = pl.BlockSpec((D_MODEL, D_MODEL), const)
    vspec = pl.BlockSpec((1, D_MODEL), const)
    lane_tile = pl.BlockSpec((TOP_K, tm), lambda i: (0, i))
    return pl.pallas_call(
        functools.partial(_merge_kernel, tm=tm),
        grid=(m // tm,),
        in_specs=[
            pl.BlockSpec((tm, D_MODEL), tile), pl.BlockSpec((tm, D_MODEL), tile), pl.BlockSpec((tm, D_MODEL), tile),
            slab(P_MA), slab(P_MB), wspec, wspec, wspec, vspec, vspec, vspec, vspec,
            pl.BlockSpec((N_EXPERTS, D_MODEL), const), pl.BlockSpec((N_EXPERTS, D_MODEL), const),
            pl.BlockSpec((N_EXPERTS, 128), const),
        ],
        out_specs=[
            pl.BlockSpec((tm, D_MODEL), tile), lane_tile, lane_tile, lane_tile,
            pl.BlockSpec((N_EXPERTS, 128), const),
        ],
        out_shape=[
            jax.ShapeDtypeStruct((m, D_MODEL), F32),
            jax.ShapeDtypeStruct((TOP_K, m), I32),
            jax.ShapeDtypeStruct((TOP_K, m), F32),
            jax.ShapeDtypeStruct((TOP_K, m), I32),
            jax.ShapeDtypeStruct((N_EXPERTS, 128), F32),
        ],
        scratch_shapes=[pltpu.VMEM((N_EXPERTS, 128), F32), pltpu.VMEM((N_EXPERTS, tm), F32)],
        compiler_params=_params("arbitrary"),
        name="merge_router",
    )(x2d, ya, yb, proj, proj, w_a, w_b, w_o, eg, eb, g1, b1, wr_hi, wr_lo, rbias)


def _dispatch_kernel(fill_ref, pend_ref, dest_ref, h_ref, xs_ref, zbuf, sem, zsem, *, tm):
    @pl.when(pl.program_id(0) == 0)
    def _():
        zbuf[...] = jnp.zeros_like(zbuf)

        def pad_copy(slot):
            return pltpu.make_async_copy(zbuf.at[pl.ds(0, 1), :], xs_ref.at[pl.ds(slot, 1), :], zsem)

        def issue_pads(e, carry):
            lax.fori_loop(fill_ref[e], pend_ref[e], lambda s, c: (pad_copy(s).start(), c)[1], 0)
            return carry

        def drain_pads(e, carry):
            lax.fori_loop(fill_ref[e], pend_ref[e], lambda s, c: (pad_copy(s).wait(), c)[1], 0)
            return carry

        lax.fori_loop(0, N_EXPERTS, issue_pads, 0)
        lax.fori_loop(0, N_EXPERTS, drain_pads, 0)

    def row_copy(t, kk):
        d = dest_ref[0, 0, kk * tm + t]
        return pltpu.make_async_copy(h_ref.at[pl.ds(t, 1), :], xs_ref.at[pl.ds(d, 1), :], sem)

    def issue(t, carry):
        for kk in range(TOP_K):
            row_copy(t, kk).start()
        return carry

    lax.fori_loop(0, tm, issue, 0)
    for kk in range(TOP_K):
        pltpu.make_async_copy(h_ref, xs_ref.at[pl.ds(0, tm), :], sem).wait()


def _dispatch(fill, pends, dest_tiles, h1, n_slots, tm):
    m = h1.shape[0]
    grid_spec = pltpu.PrefetchScalarGridSpec(
        num_scalar_prefetch=2,
        grid=(m // tm,),
        in_specs=[
            pl.BlockSpec((1, 1, TOP_K * tm), lambda i, f, p: (i, 0, 0), memory_space=pltpu.SMEM),
            pl.BlockSpec((tm, D_MODEL), lambda i, f, p: (i, 0)),
        ],
        out_specs=pl.BlockSpec(memory_space=pl.ANY),
        scratch_shapes=[pltpu.VMEM((SUBLANES, D_MODEL), F32), pltpu.SemaphoreType.DMA(()),
                        pltpu.SemaphoreType.DMA(())],
    )
    return pl.pallas_call(
        functools.partial(_dispatch_kernel, tm=tm),
        grid_spec=grid_spec,
        out_shape=jax.ShapeDtypeStruct((n_slots, D_MODEL), F32),
        compiler_params=_params("arbitrary"),
        name="dispatch",
    )(fill, pends, dest_tiles, h1)


def _experts_kernel(be_ref, nu_ref, x_ref, wg_ref, wu_ref, wd_ref, y_ref):
    del be_ref
    i = pl.program_id(0)

    @pl.when(i < nu_ref[0])
    def _():
        xb = x_ref[...].astype(BF16)
        a = _dot(xb, wg_ref[...])
        u = _dot(xb, wu_ref[...])
        y_ref[...] = _dot((a * _sigmoid(a) * u).astype(BF16), wd_ref[...])

    @pl.when(i >= nu_ref[0])
    def _():
        y_ref[...] = jnp.zeros_like(y_ref)


def _experts(blk_expert, n_used, xs, wg, wu, wd):
    n_slots = xs.shape[0]
    n_blocks = n_slots // MOE_BLOCK
    grid_spec = pltpu.PrefetchScalarGridSpec(
        num_scalar_prefetch=2,
        grid=(n_blocks,),
        in_specs=[
            pl.BlockSpec((MOE_BLOCK, D_MODEL), lambda i, be, nu: (i, 0)),
            pl.BlockSpec((None, D_MODEL, D_EXPERT), lambda i, be, nu: (be[i], 0, 0)),
            pl.BlockSpec((None, D_MODEL, D_EXPERT), lambda i, be, nu: (be[i], 0, 0)),
            pl.BlockSpec((None, D_EXPERT, D_MODEL), lambda i, be, nu: (be[i], 0, 0)),
        ],
        out_specs=pl.BlockSpec((MOE_BLOCK, D_MODEL), lambda i, be, nu: (i, 0)),
    )
    return pl.pallas_call(
        _experts_kernel,
        grid_spec=grid_spec,
        out_shape=jax.ShapeDtypeStruct((n_slots, D_MODEL), F32),
        compiler_params=_params("arbitrary"),
        name="experts",
    )(blk_expert, n_used, xs, wg, wu, wd)


def _combine_kernel(dest_ref, h_ref, gw_ref, y_ref, sg_ref, su_ref, sd_ref, g2_ref, b2_ref, o_ref, ybuf, sem, *, tm):
    def row_copy(t, kk):
        d = dest_ref[0, 0, kk * tm + t]
        return pltpu.make_async_copy(y_ref.at[pl.ds(d, 1), :], ybuf.at[kk, pl.ds(t, 1), :], sem)

    def issue(t, carry):
        for kk in range(TOP_K):
            row_copy(t, kk).start()
        return carry

    lax.fori_loop(0, tm, issue, 0)
    h1 = h_ref[...]
    hb = h1.astype(BF16)
    a = _dot(hb, sg_ref[...])
    u = _dot(hb, su_ref[...])
    shared = _dot((a * _sigmoid(a) * u).astype(BF16), sd_ref[...])
    for kk in range(TOP_K):
        pltpu.make_async_copy(y_ref.at[pl.ds(0, tm), :], ybuf.at[kk], sem).wait()
    gw = gw_ref[...]
    routed = gw[:, 0:1] * ybuf[0]
    for kk in range(1, TOP_K):
        routed = routed + gw[:, kk:kk + 1] * ybuf[kk]
    o_ref[...] = _layer_norm(DN_ALPHA * h1 + (routed + shared), g2_ref[...], b2_ref[...])


def _combine(dest_tiles, h1, gw_t, y, wsg, wsu, wsd, g2, b2, tm):
    m = h1.shape[0]
    const = lambda i: (0, 0)
    return pl.pallas_call(
        functools.partial(_combine_kernel, tm=tm),
        grid=(m // tm,),
        in_specs=[
            pl.BlockSpec((1, 1, TOP_K * tm), lambda i: (i, 0, 0), memory_space=pltpu.SMEM),
            pl.BlockSpec((tm, D_MODEL), lambda i: (i, 0)),
            pl.BlockSpec((tm, TOP_K), lambda i: (i, 0)),
            pl.BlockSpec(memory_space=pl.ANY),
            pl.BlockSpec((D_MODEL, D_EXPERT), const),
            pl.BlockSpec((D_MODEL, D_EXPERT), const),
            pl.BlockSpec((D_EXPERT, D_MODEL), const),
            pl.BlockSpec((1, D_MODEL), const),
            pl.BlockSpec((1, D_MODEL), const),
        ],
        out_specs=pl.BlockSpec((tm, D_MODEL), lambda i: (i, 0)),
        out_shape=jax.ShapeDtypeStruct((m, D_MODEL), F32),
        scratch_shapes=[pltpu.VMEM((TOP_K, tm, D_MODEL), F32), pltpu.SemaphoreType.DMA(())],
        compiler_params=_params("arbitrary"),
        name="combine",
    )(dest_tiles, h1, gw_t, y, wsg, wsu, wsd, g2, b2)


def _pick_tile(m, pref):
    t = min(pref, m)
    while m % t:
        t //= 2
    return t


def _tiles_of(a, tm):
    kk, m = a.shape
    return a.reshape(kk, m // tm, tm).transpose(1, 0, 2).reshape(m // tm, 1, kk * tm)


def _gates_time_on_lanes(sg, chunk):
    m = sg.shape[0]
    return sg[:, :SUBLANES].reshape(m // chunk, chunk, SUBLANES).transpose(0, 2, 1)


def _forward(x, meta_tokens, ln_emb_g, ln_emb_b, w_in, hg_lb_logits, hg_norm_g, ml_conv_w, ml_conv_b,
             ml_ig_bias, ml_fg_bias, ml_norm_g, w_branch_a, w_branch_b, w_out, ln1_g, ln1_b,
             w_router, router_bias, w_exp_gate, w_exp_up, w_exp_down, w_sh_gate, w_sh_up, w_sh_down,
             ln2_g, ln2_b, *, chunk, tm_proj, tm_merge, tm_moe):
    nb, seq, d = x.shape
    m = nb * seq
    row = lambda a: a.reshape(1, -1).astype(F32)

    w = w_in[0]
    kw = HG_HEADS * HG_DK
    o_qa, o_fa, o_ia, o_ga = 0, kw, 2 * kw, 3 * kw
    o_qb = 4 * kw
    o_kb = o_qb + ML_HEADS * ML_DK
    o_vb = o_kb + ML_HEADS * ML_DK
    o_ob = o_vb + ML_HEADS * ML_DV
    o_ig = o_ob + ML_HEADS * ML_DV
    o_fg = o_ig + ML_HEADS
    o_ma = o_fg + ML_HEADS
    o_mb = o_ma + D_MODEL
    cols = lambda o, n: w[:, o:o + n]
    w_cat = jnp.concatenate([
        cols(o_qa, kw), cols(o_fa, kw), cols(o_ia, kw), cols(o_ga, kw),
        cols(o_qb, 2 * ML_HEADS * ML_DK), cols(o_vb, ML_HEADS * ML_DV), cols(o_ob, ML_HEADS * ML_DV),
        cols(o_ma, D_MODEL), cols(o_mb, D_MODEL)], axis=1).astype(BF16)
    w_small = jnp.pad(cols(o_ig, 2 * ML_HEADS), ((0, 0), (0, 128 - 2 * ML_HEADS))).astype(BF16)
    gate_bias = jnp.pad(jnp.concatenate([ml_ig_bias[0], ml_fg_bias[0]]).astype(F32), (0, 128 - 2 * ML_HEADS)).reshape(1, 128)
    lb = jax.nn.softmax(hg_lb_logits.astype(F32), axis=0)[0].reshape(1, -1)
    eg, eb = row(ln_emb_g), row(ln_emb_b)
    conv_w = ml_conv_w[0].astype(F32)
    conv_b = row(ml_conv_b[0])
    hgn, mln = row(hg_norm_g[0]), row(ml_norm_g[0])

    p_m, lf_m, sg_m = _inproj(meta_tokens.astype(F32), eg, eb, w_cat, w_small, lb, gate_bias, N_META)
    s_zero = jnp.zeros((HG_HEADS, HG_DK, HG_DK), F32)
    _, s0 = _hgrn(p_m, lf_m, hgn, s_zero, 1, N_META)
    sgt_m = _gates_time_on_lanes(sg_m, N_META)
    _, c0, m0, x0 = _mlstm(p_m, sg_m, sgt_m, conv_w, conv_b, mln,
                           jnp.zeros((ML_HEADS, ML_DK, ML_AUG), F32), jnp.zeros((SUBLANES, 128), F32),
                           jnp.zeros((SUBLANES, D_MODEL), F32), 1, N_META)

    x2d = x.reshape(m, d).astype(F32)
    proj, logf, sg = _inproj(x2d, eg, eb, w_cat, w_small, lb, gate_bias, tm_proj)
    ya, _ = _hgrn(proj, logf, hgn, s0[0], nb, chunk)
    sgt = _gates_time_on_lanes(sg, chunk)
    yb, _, _, _ = _mlstm(proj, sg, sgt, conv_w, conv_b, mln, c0[0], m0[0], x0[0], nb, chunk)

    wr = w_router[0].T.astype(F32)
    wr_hi, wr_lo = _split_bf16(wr)
    rbias = jnp.broadcast_to(router_bias[0].astype(F32).reshape(N_EXPERTS, 1), (N_EXPERTS, 128))
    h1, eidx, gw, rloc, cnt = _merge(
        x2d, ya, yb, proj, w_branch_a[0].astype(BF16), w_branch_b[0].astype(BF16), w_out[0].astype(BF16),
        eg, eb, row(ln1_g[0]), row(ln1_b[0]), wr_hi, wr_lo, rbias, tm_merge)

    counts = cnt[:, 0].astype(I32)
    padded = (counts + MOE_BLOCK - 1) // MOE_BLOCK * MOE_BLOCK
    pends = jnp.cumsum(padded)
    pstarts = pends - padded
    experts = jnp.arange(N_EXPERTS, dtype=I32)
    dest = rloc + jnp.sum(jnp.where(eidx[..., None] == experts, pstarts, 0), axis=-1)
    n_blocks = m * TOP_K // MOE_BLOCK + N_EXPERTS
    blk_start = jnp.arange(n_blocks, dtype=I32) * MOE_BLOCK
    blk_expert = jnp.minimum(jnp.sum((pends[None, :] <= blk_start[:, None]).astype(I32), axis=1), N_EXPERTS - 1)
    n_used = (pends[-1:] // MOE_BLOCK).astype(I32)
    dest_tiles = _tiles_of(dest, tm_moe)

    xs = _dispatch(pstarts + counts, pends.astype(I32), dest_tiles, h1, n_blocks * MOE_BLOCK, tm_moe)
    y = _experts(blk_expert, n_used, xs, w_exp_gate[0].astype(BF16), w_exp_up[0].astype(BF16),
                 w_exp_down[0].astype(BF16))
    out = _combine(dest_tiles, h1, gw.T, y, w_sh_gate[0].astype(BF16), w_sh_up[0].astype(BF16),
                   w_sh_down[0].astype(BF16), row(ln2_g[0]), row(ln2_b[0]), tm_moe)
    return out.reshape(nb, seq, d).astype(x.dtype)


def kernel(x, meta_tokens, ln_emb_g, ln_emb_b, w_in, hg_lb_logits, hg_norm_g, ml_conv_w, ml_conv_b, ml_ig_bias, ml_fg_bias, ml_norm_g, w_branch_a, w_branch_b, w_out, ln1_g, ln1_b, w_router, router_bias, w_exp_gate, w_exp_up, w_exp_down, w_sh_gate, w_sh_up, w_sh_down, ln2_g, ln2_b):
    m = x.shape[0] * x.shape[1]
    return _forward(x, meta_tokens, ln_emb_g, ln_emb_b, w_in, hg_lb_logits, hg_norm_g, ml_conv_w, ml_conv_b,
                    ml_ig_bias, ml_fg_bias, ml_norm_g, w_branch_a, w_branch_b, w_out, ln1_g, ln1_b,
                    w_router, router_bias, w_exp_gate, w_exp_up, w_exp_down, w_sh_gate, w_sh_up, w_sh_down,
                    ln2_g, ln2_b, chunk=_pick_tile(x.shape[1], 64), tm_proj=_pick_tile(m, 1024),
                    tm_merge=_pick_tile(m, 512), tm_moe=_pick_tile(m, 128))
```

```python
import functools

import jax
import jax.numpy as jnp
from jax import lax
from jax.experimental import pallas as pl
from jax.experimental.pallas import tpu as pltpu

F32, BF16, I32 = jnp.float32, jnp.bfloat16, jnp.int32

D_MODEL = 1024
N_META = 16
HG_HEADS = 8
HG_DK = 128
ML_HEADS = 4
ML_DK = 128
ML_DV = 256
ML_AUG = ML_DV + 128
N_EXPERTS = 64
TOP_K = 8
N_GROUPS = 8
GROUP_SIZE = N_EXPERTS // N_GROUPS
TOPK_GROUPS = 4
D_EXPERT = 256
ROUTED_SCALE = 2.5
MOE_BLOCK = 512
DN_ALPHA = 2.0 ** 0.25
EPS = 1e-5
EXP_CLAMP = 80.0
SUBLANES = 8

P_QA, P_KA, P_IA, P_GA, P_QKB, P_VB, P_OB, P_MA, P_MB = range(9)
N_SLABS = 9

VMEM_LIMIT = 56 * 1024 * 1024


def _params(*sem):
    return pltpu.CompilerParams(dimension_semantics=sem, vmem_limit_bytes=VMEM_LIMIT)


def _sigmoid(x):
    return 1.0 / (1.0 + jnp.exp(-x))


def _log_sigmoid(x):
    return jnp.minimum(x, 0.0) - jnp.log(1.0 + jnp.exp(-jnp.abs(x)))


def _layer_norm(x, g, b):
    xc = x - jnp.mean(x, axis=-1, keepdims=True)
    var = jnp.mean(xc * xc, axis=-1, keepdims=True)
    return xc * lax.rsqrt(var + EPS) * g + b


def _dot(a, b):
    return jnp.dot(a, b, preferred_element_type=F32)


def _dot_nt(a, b):
    return lax.dot_general(a, b, (((1,), (1,)), ((), ())), preferred_element_type=F32)


def _dot_tn(a, b):
    return lax.dot_general(a, b, (((0,), (0,)), ((), ())), preferred_element_type=F32)


def _split_bf16(x):
    hi = x.astype(BF16)
    lo = (x - hi.astype(F32)).astype(BF16)
    return hi, lo


def _ones_where(cond):
    return jnp.where(cond, 1.0, 0.0).astype(BF16)


def _inproj_kernel(x_ref, g_ref, b_ref, w_ref, ws_ref, lb_ref, gb_ref, p_ref, lf_ref, sg_ref):
    hb = _layer_norm(x_ref[...], g_ref[...], b_ref[...]).astype(BF16)
    s = _dot(hb, ws_ref[...]) + gb_ref[...]
    lane = lax.broadcasted_iota(I32, s.shape, 1)
    sg_ref[...] = jnp.where(lane < ML_HEADS, s, _log_sigmoid(s))
    for n in range(N_SLABS):
        acc = _dot(hb, w_ref[:, n * D_MODEL:(n + 1) * D_MODEL])
        if n == P_KA:
            lb = lb_ref[...]
            f = lb + (1.0 - lb) * _sigmoid(acc)
            lf_ref[...] = jnp.log(f)
            acc = 1.0 - f
        elif n == P_GA:
            acc = acc * _sigmoid(acc)
        elif n >= P_OB:
            acc = _sigmoid(acc)
        p_ref[n] = acc.astype(BF16)


def _inproj(x2d, ln_g, ln_b, w_cat, w_small, lb, gate_bias, tm):
    m = x2d.shape[0]
    const = lambda i: (0, 0)
    resident = lambda shape: pl.BlockSpec(shape, const, pipeline_mode=pl.Buffered(1))
    return pl.pallas_call(
        _inproj_kernel,
        grid=(m // tm,),
        in_specs=[
            pl.BlockSpec((tm, D_MODEL), lambda i: (i, 0)),
            resident((1, D_MODEL)),
            resident((1, D_MODEL)),
            resident((D_MODEL, N_SLABS * D_MODEL)),
            resident((D_MODEL, 128)),
            resident((1, D_MODEL)),
            resident((1, 128)),
        ],
        out_specs=[
            pl.BlockSpec((N_SLABS, tm, D_MODEL), lambda i: (0, i, 0)),
            pl.BlockSpec((tm, D_MODEL), lambda i: (i, 0)),
            pl.BlockSpec((tm, 128), lambda i: (i, 0)),
        ],
        out_shape=[
            jax.ShapeDtypeStruct((N_SLABS, m, D_MODEL), BF16),
            jax.ShapeDtypeStruct((m, D_MODEL), F32),
            jax.ShapeDtypeStruct((m, 128), F32),
        ],
        compiler_params=_params("parallel"),
        name="inproj",
    )(x2d, ln_g, ln_b, w_cat, w_small, lb, gate_bias)


def _block_rows(b, block, pick):
    c, w = b.shape
    parts = [jnp.broadcast_to(b[j * block + pick:j * block + pick + 1, :], (block, w))
             for j in range(c // block)]
    return parts[0] if len(parts) == 1 else jnp.concatenate(parts, axis=0)


def _hgrn_kernel(q_ref, k_ref, v_ref, g_ref, lf_ref, ng_ref, s0_ref, y_ref, sfin_ref, s_scr, *, chunk):
    c = pl.program_id(1)

    @pl.when(c == 0)
    def _():
        s_scr[...] = s0_ref[...]

    cs = chunk
    row = lax.broadcasted_iota(I32, (cs, cs), 0)
    col = lax.broadcasted_iota(I32, (cs, cs), 1)
    tri = _ones_where(col <= row)
    lf_hi, lf_lo = _split_bf16(lf_ref[...])
    b = _dot(tri, lf_hi) + _dot(tri, lf_lo)
    q = q_ref[...].astype(F32)
    k = k_ref[...].astype(F32)
    v = v_ref[...]
    blast = b[cs - 1:cs, :]
    qg = (q * jnp.exp(b)).astype(BF16)
    kg = (k * jnp.exp(blast - b)).astype(BF16)
    dec = jnp.exp(blast)

    levels = []
    m = SUBLANES
    while 2 * m <= cs:
        w = jnp.exp(-jnp.abs(b - _block_rows(b, 2 * m, m - 1)))
        sh = (2 * m).bit_length() - 1
        mask = ((row >> sh) == (col >> sh)) & ((row & (2 * m - 1)) >= m) & ((col & (2 * m - 1)) < m)
        levels.append(((q * w).astype(BF16), (k * w).astype(BF16), mask))
        m *= 2
    e = jnp.clip(b - _block_rows(b, SUBLANES, SUBLANES // 2 - 1), -EXP_CLAMP, EXP_CLAMP)
    levels.append(((q * jnp.exp(e)).astype(BF16), (k * jnp.exp(-e)).astype(BF16),
                   ((row >> 3) == (col >> 3)) & (col <= row)))

    ng = ng_ref[...]
    for h in range(HG_HEADS):
        sl = slice(h * HG_DK, (h + 1) * HG_DK)
        st = s_scr[h]
        o = _dot_nt(qg[:, sl], st.astype(BF16))
        sc = jnp.zeros((cs, cs), F32)
        for lq, lk, mask in levels:
            sc = jnp.where(mask, _dot_nt(lq[:, sl], lk[:, sl]), sc)
        o = o + _dot(sc.astype(BF16), v[:, sl])
        s_scr[h] = dec[:, sl] * st + _dot_tn(v[:, sl], kg[:, sl])
        ms = jnp.mean(o * o, axis=-1, keepdims=True)
        y = o * lax.rsqrt(ms + EPS) * ng[:, sl] * g_ref[:, sl].astype(F32)
        y_ref[:, sl] = y.astype(BF16)

    @pl.when(c == pl.num_programs(1) - 1)
    def _():
        sfin_ref[...] = s_scr[...]


def _hgrn(proj, logf, norm_g, s0, nb, chunk):
    m = logf.shape[0]
    nc = m // nb // chunk
    slab = lambda s: pl.BlockSpec((None, chunk, D_MODEL), lambda b, c, s=s: (s, b * nc + c, 0))
    return pl.pallas_call(
        functools.partial(_hgrn_kernel, chunk=chunk),
        grid=(nb, nc),
        in_specs=[
            slab(P_QA), slab(P_KA), slab(P_IA), slab(P_GA),
            pl.BlockSpec((chunk, D_MODEL), lambda b, c: (b * nc + c, 0)),
            pl.BlockSpec((1, D_MODEL), lambda b, c: (0, 0)),
            pl.BlockSpec((HG_HEADS, HG_DK, HG_DK), lambda b, c: (0, 0, 0)),
        ],
        out_specs=[
            pl.BlockSpec((chunk, D_MODEL), lambda b, c: (b * nc + c, 0)),
            pl.BlockSpec((None, HG_HEADS, HG_DK, HG_DK), lambda b, c: (b, 0, 0, 0)),
        ],
        out_shape=[
            jax.ShapeDtypeStruct((m, D_MODEL), BF16),
            jax.ShapeDtypeStruct((nb, HG_HEADS, HG_DK, HG_DK), F32),
        ],
        scratch_shapes=[pltpu.VMEM((HG_HEADS, HG_DK, HG_DK), F32)],
        compiler_params=_params("parallel", "arbitrary"),
        name="hgrn",
    )(proj, proj, proj, proj, logf, norm_g, s0)


def _mlstm_kernel(qk_ref, v_ref, og_ref, sg_ref, sgt_ref, cw_ref, cb_ref, ng_ref, c0_ref, m0_ref, x0_ref,
                  y_ref, cfin_ref, mfin_ref, xfin_ref, c_scr, m_scr, x_scr, *, chunk):
    c = pl.program_id(1)

    @pl.when(c == 0)
    def _():
        c_scr[...] = c0_ref[...]
        m_scr[...] = m0_ref[...]
        x_scr[...] = x0_ref[...]

    cs = chunk
    x = qk_ref[...].astype(F32)
    prev = x_scr[...]
    sub = lax.broadcasted_iota(I32, (SUBLANES, D_MODEL), 0)
    cw = cw_ref[...]
    conv = cw[3:4, :] * x + cb_ref[...]
    for j in (1, 2, 3):
        xs = pltpu.roll(x, j, 0)
        head = jnp.where(sub < j, pltpu.roll(prev, j, 0), xs[:SUBLANES, :])
        xs = jnp.concatenate([head, xs[SUBLANES:, :]], axis=0)
        conv = conv + cw[3 - j:4 - j, :] * xs
    x_scr[...] = x[cs - SUBLANES:, :]
    qk = conv * _sigmoid(conv)
    q_all = (qk[:, :ML_HEADS * ML_DK] * (ML_DK ** -0.5)).astype(BF16)
    k_all = qk[:, ML_HEADS * ML_DK:]

    row = lax.broadcasted_iota(I32, (cs, cs), 0)
    col = lax.broadcasted_iota(I32, (cs, cs), 1)
    causal = col <= row
    tri = _ones_where(causal)
    sg = sg_ref[...]
    sgt = sgt_ref[...]
    sg_hi, sg_lo = _split_bf16(sg)
    bcol_all = _dot(tri, sg_hi) + _dot(tri, sg_lo)
    sgt_hi, sgt_lo = _split_bf16(sgt)
    brow_all = _dot_nt(sgt_hi, tri) + _dot_nt(sgt_lo, tri)
    lane128 = lax.broadcasted_iota(I32, (cs, 128), 1)
    ones_col = _ones_where(lane128 == 0)
    v = v_ref[...]
    ng = ng_ref[...]

    for h in range(ML_HEADS):
        b_col = bcol_all[:, ML_HEADS + h:ML_HEADS + h + 1]
        b_row = brow_all[ML_HEADS + h:ML_HEADS + h + 1, :]
        ig_col = sg[:, h:h + 1]
        ig_row = sgt[h:h + 1, :]
        m_prev = m_scr[h:h + 1, 0:1]
        q_h = q_all[:, h * ML_DK:(h + 1) * ML_DK]
        k_h = k_all[:, h * ML_DK:(h + 1) * ML_DK]
        v_aug = jnp.concatenate([v[:, h * ML_DV:(h + 1) * ML_DV], ones_col], axis=1)
        c_st = c_scr[h]

        log_intra = jnp.where(causal, b_col - b_row + ig_row, -jnp.inf)
        log_inter = b_col + m_prev
        m_t = jnp.maximum(log_inter, jnp.max(log_intra, axis=-1, keepdims=True))
        w_intra = jnp.exp(log_intra - m_t)
        w_inter = jnp.exp(log_inter - m_t)
        s = _dot_nt(q_h, k_h.astype(BF16)) * w_intra
        tot = w_inter * _dot(q_h, c_st.astype(BF16)) + _dot(s.astype(BF16), v_aug)
        num = tot[:, :ML_DV]
        den = tot[:, ML_DV:ML_DV + 1]
        hid = num / jnp.maximum(jnp.abs(den), jnp.exp(-m_t))

        b_last = b_col[cs - 1:cs, :]
        log_w = b_last - b_col + ig_col
        m_new = jnp.maximum(b_last + m_prev, jnp.max(log_w, axis=0, keepdims=True))
        w_s = jnp.exp(log_w - m_new)
        decay = jnp.exp(b_last + m_prev - m_new)
        c_scr[h] = decay * c_st + _dot_tn((k_h * w_s).astype(BF16), v_aug)
        m_scr[h:h + 1, :] = jnp.broadcast_to(m_new, (1, 128))

        hc = hid - jnp.mean(hid, axis=-1, keepdims=True)
        var = jnp.mean(hc * hc, axis=-1, keepdims=True)
        sl = slice(h * ML_DV, (h + 1) * ML_DV)
        y = hc * lax.rsqrt(var + EPS) * ng[:, sl] * og_ref[:, sl].astype(F32)
        y_ref[:, sl] = y.astype(BF16)

    @pl.when(c == pl.num_programs(1) - 1)
    def _():
        cfin_ref[...] = c_scr[...]
        mfin_ref[...] = m_scr[...]
        xfin_ref[...] = x_scr[...]


def _mlstm(proj, sg, sgt, conv_w, conv_b, norm_g, c0, m0, x0, nb, chunk):
    m = sg.shape[0]
    nc = m // nb // chunk
    slab = lambda s: pl.BlockSpec((None, chunk, D_MODEL), lambda b, c, s=s: (s, b * nc + c, 0))
    const2 = lambda b, c: (0, 0)
    return pl.pallas_call(
        functools.partial(_mlstm_kernel, chunk=chunk),
        grid=(nb, nc),
        in_specs=[
            slab(P_QKB), slab(P_VB), slab(P_OB),
            pl.BlockSpec((chunk, 128), lambda b, c: (b * nc + c, 0)),
            pl.BlockSpec((None, SUBLANES, chunk), lambda b, c: (b * nc + c, 0, 0)),
            pl.BlockSpec((4, D_MODEL), const2),
            pl.BlockSpec((1, D_MODEL), const2),
            pl.BlockSpec((1, D_MODEL), const2),
            pl.BlockSpec((ML_HEADS, ML_DK, ML_AUG), lambda b, c: (0, 0, 0)),
            pl.BlockSpec((SUBLANES, 128), const2),
            pl.BlockSpec((SUBLANES, D_MODEL), const2),
        ],
        out_specs=[
            pl.BlockSpec((chunk, D_MODEL), lambda b, c: (b * nc + c, 0)),
            pl.BlockSpec((None, ML_HEADS, ML_DK, ML_AUG), lambda b, c: (b, 0, 0, 0)),
            pl.BlockSpec((None, SUBLANES, 128), lambda b, c: (b, 0, 0)),
            pl.BlockSpec((None, SUBLANES, D_MODEL), lambda b, c: (b, 0, 0)),
        ],
        out_shape=[
            jax.ShapeDtypeStruct((m, D_MODEL), BF16),
            jax.ShapeDtypeStruct((nb, ML_HEADS, ML_DK, ML_AUG), F32),
            jax.ShapeDtypeStruct((nb, SUBLANES, 128), F32),
            jax.ShapeDtypeStruct((nb, SUBLANES, D_MODEL), F32),
        ],
        scratch_shapes=[
            pltpu.VMEM((ML_HEADS, ML_DK, ML_AUG), F32),
            pltpu.VMEM((SUBLANES, 128), F32),
            pltpu.VMEM((SUBLANES, D_MODEL), F32),
        ],
        compiler_params=_params("parallel", "arbitrary"),
        name="mlstm",
    )(proj, proj, proj, sg, sgt, conv_w, conv_b, norm_g, c0, m0, x0)


def _merge_kernel(x_ref, ya_ref, yb_ref, ma_ref, mb_ref, wa_ref, wb_ref, wo_ref, eg_ref, eb_ref, g1_ref, b1_ref,
                  wrh_ref, wrl_ref, rb_ref,
                  h1_ref, eidx_ref, gw_ref, rloc_ref, cnt_ref, carry_scr, val_scr, *, tm):
    i = pl.program_id(0)

    @pl.when(i == 0)
    def _():
        carry_scr[...] = jnp.zeros_like(carry_scr)

    h0 = _layer_norm(x_ref[...], eg_ref[...], eb_ref[...])
    merged = (ma_ref[...].astype(F32) * _dot(ya_ref[...], wa_ref[...])
              + mb_ref[...].astype(F32) * _dot(yb_ref[...], wb_ref[...]))
    mix = _dot(merged.astype(BF16), wo_ref[...])
    h1 = _layer_norm(DN_ALPHA * h0 + mix, g1_ref[...], b1_ref[...])
    h1_ref[...] = h1

    h_hi, h_lo = _split_bf16(h1)
    logits = _dot_nt(wrh_ref[...], h_hi) + _dot_nt(wrh_ref[...], h_lo) + _dot_nt(wrl_ref[...], h_hi)
    scores = _sigmoid(logits)
    biased = scores + rb_ref[:, 0:1]
    neg_inf = -jnp.inf

    g3 = biased.reshape(N_GROUPS, GROUP_SIZE, tm)
    sub3 = lax.broadcasted_iota(I32, g3.shape, 1)
    top1 = jnp.max(g3, axis=1, keepdims=True)
    first = jnp.min(jnp.where(g3 == top1, sub3, GROUP_SIZE), axis=1, keepdims=True)
    top2 = jnp.max(jnp.where(sub3 == first, neg_inf, g3), axis=1, keepdims=True)
    gs = (top1 + top2).reshape(N_GROUPS, tm)
    gi = lax.broadcasted_iota(I32, gs.shape, 0)
    grank = jnp.zeros(gs.shape, F32)
    for j in range(N_GROUPS):
        r = gs[j:j + 1, :]
        grank = grank + jnp.where((r > gs) | ((r == gs) & (gi > j)), 1.0, 0.0)
    gsel = grank < float(TOPK_GROUPS)
    emask = jnp.broadcast_to(gsel.reshape(N_GROUPS, 1, tm), (N_GROUPS, GROUP_SIZE, tm)).reshape(N_EXPERTS, tm)
    masked = jnp.where(emask, biased, neg_inf)

    val_scr[...] = masked
    ei = lax.broadcasted_iota(I32, masked.shape, 0)

    def rank_body(j, rank):
        r = val_scr[pl.ds(j, 1), :]
        return rank + jnp.where((r > masked) | ((r == masked) & (ei > j)), 1.0, 0.0)

    rank = lax.fori_loop(0, N_EXPERTS, rank_body, jnp.zeros(masked.shape, F32))
    sel = rank < float(TOP_K)
    sel_w = jnp.where(sel, scores, 0.0)
    gwd = sel_w / jnp.sum(sel_w, axis=0, keepdims=True) * ROUTED_SCALE

    tr = lax.broadcasted_iota(I32, (tm, tm), 0)
    tc = lax.broadcasted_iota(I32, (tm, tm), 1)
    upper = _ones_where(tr < tc)
    sel_b = _ones_where(sel)
    rloc = _dot(sel_b, upper) + carry_scr[:, 0:1]
    carry_scr[...] = carry_scr[...] + _dot(sel_b, jnp.ones((tm, 128), BF16))

    ef = ei.astype(F32)
    e_rows, w_rows, r_rows = [], [], []
    for kk in range(TOP_K):
        pick = sel & (rank == float(kk))
        e_rows.append(jnp.sum(jnp.where(pick, ef, 0.0), axis=0, keepdims=True))
        w_rows.append(jnp.sum(jnp.where(pick, gwd, 0.0), axis=0, keepdims=True))
        r_rows.append(jnp.sum(jnp.where(pick, rloc, 0.0), axis=0, keepdims=True))
    eidx_ref[...] = jnp.concatenate(e_rows, axis=0).astype(I32)
    gw_ref[...] = jnp.concatenate(w_rows, axis=0)
    rloc_ref[...] = jnp.concatenate(r_rows, axis=0).astype(I32)

    @pl.when(i == pl.num_programs(0) - 1)
    def _():
        cnt_ref[...] = carry_scr[...]


def _merge(x2d, ya, yb, proj, w_a, w_b, w_o, eg, eb, g1, b1, wr_hi, wr_lo, rbias, tm):
    m = x2d.shape[0]
    tile = lambda i: (i, 0)
    const = lambda i: (0, 0)
    slab = lambda s: pl.BlockSpec((None, tm, D_MODEL), lambda i, s=s: (s, i, 0))
    wspec = pl.BlockSpec((D_MODEL, D_MODEL), const)
    vspec = pl.BlockSpec((1, D_MODEL), const)
    lane_tile = pl.BlockSpec((TOP_K, tm), lambda i: (0, i))
    return pl.pallas_call(
        functools.partial(_merge_kernel, tm=tm),
        grid=(m // tm,),
        in_specs=[
            pl.BlockSpec((tm, D_MODEL), tile), pl.BlockSpec((tm, D_MODEL), tile), pl.BlockSpec((tm, D_MODEL), tile),
            slab(P_MA), slab(P_MB), wspec, wspec, wspec, vspec, vspec, vspec, vspec,
            pl.BlockSpec((N_EXPERTS, D_MODEL), const), pl.BlockSpec((N_EXPERTS, D_MODEL), const),
            pl.BlockSpec((N_EXPERTS, 128), const),
        ],
        out_specs=[
            pl.BlockSpec((tm, D_MODEL), tile), lane_tile, lane_tile, lane_tile,
            pl.BlockSpec((N_EXPERTS, 128), const),
        ],
        out_shape=[
            jax.ShapeDtypeStruct((m, D_MODEL), F32),
            jax.ShapeDtypeStruct((TOP_K, m), I32),
            jax.ShapeDtypeStruct((TOP_K, m), F32),
            jax.ShapeDtypeStruct((TOP_K, m), I32),
            jax.ShapeDtypeStruct((N_EXPERTS, 128), F32),
        ],
        scratch_shapes=[pltpu.VMEM((N_EXPERTS, 128), F32), pltpu.VMEM((N_EXPERTS, tm), F32)],
        compiler_params=_params("arbitrary"),
        name="merge_router",
    )(x2d, ya, yb, proj, proj, w_a, w_b, w_o, eg, eb, g1, b1, wr_hi, wr_lo, rbias)


def _dispatch_kernel(fill_ref, pend_ref, dest_ref, h_ref, xs_ref, zbuf, sem, zsem, *, tm):
    @pl.when(pl.program_id(0) == 0)
    def _():
        zbuf[...] = jnp.zeros_like(zbuf)

        def pad_copy(slot):
            return pltpu.make_async_copy(zbuf.at[pl.ds(0, 1), :], xs_ref.at[pl.ds(slot, 1), :], zsem)

        def issue_pads(e, carry):
            lax.fori_loop(fill_ref[e], pend_ref[e], lambda s, c: (pad_copy(s).start(), c)[1], 0)
            return carry

        def drain_pads(e, carry):
            lax.fori_loop(fill_ref[e], pend_ref[e], lambda s, c: (pad_copy(s).wait(), c)[1], 0)
            return carry

        lax.fori_loop(0, N_EXPERTS, issue_pads, 0)
        lax.fori_loop(0, N_EXPERTS, drain_pads, 0)

    def row_copy(t, kk):
        d = dest_ref[0, 0, kk * tm + t]
        return pltpu.make_async_copy(h_ref.at[pl.ds(t, 1), :], xs_ref.at[pl.ds(d, 1), :], sem)

    def issue(t, carry):
        for kk in range(TOP_K):
            row_copy(t, kk).start()
        return carry

    lax.fori_loop(0, tm, issue, 0)
    for kk in range(TOP_K):
        pltpu.make_async_copy(h_ref, xs_ref.at[pl.ds(0, tm), :], sem).wait()


def _dispatch(fill, pends, dest_tiles, h1, n_slots, tm):
    m = h1.shape[0]
    grid_spec = pltpu.PrefetchScalarGridSpec(
        num_scalar_prefetch=2,
        grid=(m // tm,),
        in_specs=[
            pl.BlockSpec((1, 1, TOP_K * tm), lambda i, f, p: (i, 0, 0), memory_space=pltpu.SMEM),
            pl.BlockSpec((tm, D_MODEL), lambda i, f, p: (i, 0)),
        ],
        out_specs=pl.BlockSpec(memory_space=pl.ANY),
        scratch_shapes=[pltpu.VMEM((SUBLANES, D_MODEL), F32), pltpu.SemaphoreType.DMA(()),
                        pltpu.SemaphoreType.DMA(())],
    )
    return pl.pallas_call(
        functools.partial(_dispatch_kernel, tm=tm),
        grid_spec=grid_spec,
        out_shape=jax.ShapeDtypeStruct((n_slots, D_MODEL), F32),
        compiler_params=_params("arbitrary"),
        name="dispatch",
    )(fill, pends, dest_tiles, h1)


def _experts_kernel(be_ref, nu_ref, x_ref, wg_ref, wu_ref, wd_ref, y_ref):
    del be_ref
    i = pl.program_id(0)

    @pl.when(i < nu_ref[0])
    def _():
        xb = x_ref[...].astype(BF16)
        a = _dot(xb, wg_ref[...])
        u = _dot(xb, wu_ref[...])
        y_ref[...] = _dot((a * _sigmoid(a) * u).astype(BF16), wd_ref[...])

    @pl.when(i >= nu_ref[0])
    def _():
        y_ref[...] = jnp.zeros_like(y_ref)


def _experts(blk_expert, n_used, xs, wg, wu, wd):
    n_slots = xs.shape[0]
    n_blocks = n_slots // MOE_BLOCK
    grid_spec = pltpu.PrefetchScalarGridSpec(
        num_scalar_prefetch=2,
        grid=(n_blocks,),
        in_specs=[
            pl.BlockSpec((MOE_BLOCK, D_MODEL), lambda i, be, nu: (i, 0)),
            pl.BlockSpec((None, D_MODEL, D_EXPERT), lambda i, be, nu: (be[i], 0, 0)),
            pl.BlockSpec((None, D_MODEL, D_EXPERT), lambda i, be, nu: (be[i], 0, 0)),
            pl.BlockSpec((None, D_EXPERT, D_MODEL), lambda i, be, nu: (be[i], 0, 0)),
        ],
        out_specs=pl.BlockSpec((MOE_BLOCK, D_MODEL), lambda i, be, nu: (i, 0)),
    )
    return pl.pallas_call(
        _experts_kernel,
        grid_spec=grid_spec,
        out_shape=jax.ShapeDtypeStruct((n_slots, D_MODEL), F32),
        compiler_params=_params("arbitrary"),
        name="experts",
    )(blk_expert, n_used, xs, wg, wu, wd)


def _combine_kernel(dest_ref, h_ref, gw_ref, y_ref, sg_ref, su_ref, sd_ref, g2_ref, b2_ref, o_ref, ybuf, sem, *, tm):
    def row_copy(t, kk):
        d = dest_ref[0, 0, kk * tm + t]
        return pltpu.make_async_copy(y_ref.at[pl.ds(d, 1), :], ybuf.at[kk, pl.ds(t, 1), :], sem)

    def issue(t, carry):
        for kk in range(TOP_K):
            row_copy(t, kk).start()
        return carry

    lax.fori_loop(0, tm, issue, 0)
    h1 = h_ref[...]
    hb = h1.astype(BF16)
    a = _dot(hb, sg_ref[...])
    u = _dot(hb, su_ref[...])
    shared = _dot((a * _sigmoid(a) * u).astype(BF16), sd_ref[...])
    for kk in range(TOP_K):
        pltpu.make_async_copy(y_ref.at[pl.ds(0, tm), :], ybuf.at[kk], sem).wait()
    gw = gw_ref[...]
    routed = gw[:, 0:1] * ybuf[0]
    for kk in range(1, TOP_K):
        routed = routed + gw[:, kk:kk + 1] * ybuf[kk]
    o_ref[...] = _layer_norm(DN_ALPHA * h1 + (routed + shared), g2_ref[...], b2_ref[...])


def _combine(dest_tiles, h1, gw_t, y, wsg, wsu, wsd, g2, b2, tm):
    m = h1.shape[0]
    const = lambda i: (0, 0)
    return pl.pallas_call(
        functools.partial(_combine_kernel, tm=tm),
        grid=(m // tm,),
        in_specs=[
            pl.BlockSpec((1, 1, TOP_K * tm), lambda i: (i, 0, 0), memory_space=pltpu.SMEM),
            pl.BlockSpec((tm, D_MODEL), lambda i: (i, 0)),
            pl.BlockSpec((tm, TOP_K), lambda i: (i, 0)),
            pl.BlockSpec(memory_space=pl.ANY),
            pl.BlockSpec((D_MODEL, D_EXPERT), const),
            pl.BlockSpec((D_MODEL, D_EXPERT), const),
            pl.BlockSpec((D_EXPERT, D_MODEL), const),
            pl.BlockSpec((1, D_MODEL), const),
            pl.BlockSpec((1, D_MODEL), const),
        ],
        out_specs=pl.BlockSpec((tm, D_MODEL), lambda i: (i, 0)),
        out_shape=jax.ShapeDtypeStruct((m, D_MODEL), F32),
        scratch_shapes=[pltpu.VMEM((TOP_K, tm, D_MODEL), F32), pltpu.SemaphoreType.DMA(())],
        compiler_params=_params("arbitrary"),
        name="combine",
    )(dest_tiles, h1, gw_t, y, wsg, wsu, wsd, g2, b2)


def _pick_tile(m, pref):
    t = min(pref, m)
    while m % t:
        t //= 2
    return t


def _tiles_of(a, tm):
    kk, m = a.shape
    return a.reshape(kk, m // tm, tm).transpose(1, 0, 2).reshape(m // tm, 1, kk * tm)


def _gates_time_on_lanes(sg, chunk):
    m = sg.shape[0]
    return sg[:, :SUBLANES].reshape(m // chunk, chunk, SUBLANES).transpose(0, 2, 1)


def _forward(x, meta_tokens, ln_emb_g, ln_emb_b, w_in, hg_lb_logits, hg_norm_g, ml_conv_w, ml_conv_b,
             ml_ig_bias, ml_fg_bias, ml_norm_g, w_branch_a, w_branch_b, w_out, ln1_g, ln1_b,
             w_router, router_bias, w_exp_gate, w_exp_up, w_exp_down, w_sh_gate, w_sh_up, w_sh_down,
             ln2_g, ln2_b, *, chunk, tm_proj, tm_merge, tm_moe):
    nb, seq, d = x.shape
    m = nb * seq
    row = lambda a: a.reshape(1, -1).astype(F32)

    w = w_in[0]
    kw = HG_HEADS * HG_DK
    o_qa, o_fa, o_ia, o_ga = 0, kw, 2 * kw, 3 * kw
    o_qb = 4 * kw
    o_kb = o_qb + ML_HEADS * ML_DK
    o_vb = o_kb + ML_HEADS * ML_DK
    o_ob = o_vb + ML_HEADS * ML_DV
    o_ig = o_ob + ML_HEADS * ML_DV
    o_fg = o_ig + ML_HEADS
    o_ma = o_fg + ML_HEADS
    o_mb = o_ma + D_MODEL
    cols = lambda o, n: w[:, o:o + n]
    w_cat = jnp.concatenate([
        cols(o_qa, kw), cols(o_fa, kw), cols(o_ia, kw), cols(o_ga, kw),
        cols(o_qb, 2 * ML_HEADS * ML_DK), cols(o_vb, ML_HEADS * ML_DV), cols(o_ob, ML_HEADS * ML_DV),
        cols(o_ma, D_MODEL), cols(o_mb, D_MODEL)], axis=1).astype(BF16)
    w_small = jnp.pad(cols(o_ig, 2 * ML_HEADS), ((0, 0), (0, 128 - 2 * ML_HEADS))).astype(BF16)
    gate_bias = jnp.pad(jnp.concatenate([ml_ig_bias[0], ml_fg_bias[0]]).astype(F32), (0, 128 - 2 * ML_HEADS)).reshape(1, 128)
    lb = jax.nn.softmax(hg_lb_logits.astype(F32), axis=0)[0].reshape(1, -1)
    eg, eb = row(ln_emb_g), row(ln_emb_b)
    conv_w = ml_conv_w[0].astype(F32)
    conv_b = row(ml_conv_b[0])
    hgn, mln = row(hg_norm_g[0]), row(ml_norm_g[0])

    p_m, lf_m, sg_m = _inproj(meta_tokens.astype(F32), eg, eb, w_cat, w_small, lb, gate_bias, N_META)
    s_zero = jnp.zeros((HG_HEADS, HG_DK, HG_DK), F32)
    _, s0 = _hgrn(p_m, lf_m, hgn, s_zero, 1, N_META)
    sgt_m = _gates_time_on_lanes(sg_m, N_META)
    _, c0, m0, x0 = _mlstm(p_m, sg_m, sgt_m, conv_w, conv_b, mln,
                           jnp.zeros((ML_HEADS, ML_DK, ML_AUG), F32), jnp.zeros((SUBLANES, 128), F32),
                           jnp.zeros((SUBLANES, D_MODEL), F32), 1, N_META)

    x2d = x.reshape(m, d).astype(F32)
    proj, logf, sg = _inproj(x2d, eg, eb, w_cat, w_small, lb, gate_bias, tm_proj)
    ya, _ = _hgrn(proj, logf, hgn, s0[0], nb, chunk)
    sgt = _gates_time_on_lanes(sg, chunk)
    yb, _, _, _ = _mlstm(proj, sg, sgt, conv_w, conv_b, mln, c0[0], m0[0], x0[0], nb, chunk)

    wr = w_router[0].T.astype(F32)
    wr_hi, wr_lo = _split_bf16(wr)
    rbias = jnp.broadcast_to(router_bias[0].astype(F32).reshape(N_EXPERTS, 1), (N_EXPERTS, 128))
    h1, eidx, gw, rloc, cnt = _merge(
        x2d, ya, yb, proj, w_branch_a[0].astype(BF16), w_branch_b[0].astype(BF16), w_out[0].astype(BF16),
        eg, eb, row(ln1_g[0]), row(ln1_b[0]), wr_hi, wr_lo, rbias, tm_merge)

    counts = cnt[:, 0].astype(I32)
    padded = (counts + MOE_BLOCK - 1) // MOE_BLOCK * MOE_BLOCK
    pends = jnp.cumsum(padded)
    pstarts = pends - padded
    experts = jnp.arange(N_EXPERTS, dtype=I32)
    dest = rloc + jnp.sum(jnp.where(eidx[..., None] == experts, pstarts, 0), axis=-1)
    n_blocks = m * TOP_K // MOE_BLOCK + N_EXPERTS
    blk_start = jnp.arange(n_blocks, dtype=I32) * MOE_BLOCK
    blk_expert = jnp.minimum(jnp.sum((pends[None, :] <= blk_start[:, None]).astype(I32), axis=1), N_EXPERTS - 1)
    n_used = (pends[-1:] // MOE_BLOCK).astype(I32)
    dest_tiles = _tiles_of(dest, tm_moe)

    xs = _dispatch(pstarts + counts, pends.astype(I32), dest_tiles, h1, n_blocks * MOE_BLOCK, tm_moe)
    y = _experts(blk_expert, n_used, xs, w_exp_gate[0].astype(BF16), w_exp_up[0].astype(BF16),
                 w_exp_down[0].astype(BF16))
    out = _combine(dest_tiles, h1, gw.T, y, w_sh_gate[0].astype(BF16), w_sh_up[0].astype(BF16),
                   w_sh_down[0].astype(BF16), row(ln2_g[0]), row(ln2_b[0]), tm_moe)
    return out.reshape(nb, seq, d).astype(x.dtype)


def kernel(x, meta_tokens, ln_emb_g, ln_emb_b, w_in, hg_lb_logits, hg_norm_g, ml_conv_w, ml_conv_b, ml_ig_bias, ml_fg_bias, ml_norm_g, w_branch_a, w_branch_b, w_out, ln1_g, ln1_b, w_router, router_bias, w_exp_gate, w_exp_up, w_exp_down, w_sh_gate, w_sh_up, w_sh_down, ln2_g, ln2_b):
    m = x.shape[0] * x.shape[1]
    return _forward(x, meta_tokens, ln_emb_g, ln_emb_b, w_in, hg_lb_logits, hg_norm_g, ml_conv_w, ml_conv_b,
                    ml_ig_bias, ml_fg_bias, ml_norm_g, w_branch_a, w_branch_b, w_out, ln1_g, ln1_b,
                    w_router, router_bias, w_exp_gate, w_exp_up, w_exp_down, w_sh_gate, w_sh_up, w_sh_down,
                    ln2_g, ln2_b, chunk=_pick_tile(x.shape[1], 256), tm_proj=_pick_tile(m, 512),
                    tm_merge=_pick_tile(m, 512), tm_moe=_pick_tile(m, 128))
```

```python
import functools

import jax
import jax.numpy as jnp
from jax import lax
from jax.experimental import pallas as pl
from jax.experimental.pallas import tpu as pltpu

F32, BF16, I32 = jnp.float32, jnp.bfloat16, jnp.int32

D_MODEL = 1024
N_META = 16
HG_HEADS = 8
HG_DK = 128
ML_HEADS = 4
ML_DK = 128
ML_DV = 256
ML_AUG = ML_DV + 128
N_EXPERTS = 64
TOP_K = 8
N_GROUPS = 8
GROUP_SIZE = N_EXPERTS // N_GROUPS
TOPK_GROUPS = 4
D_EXPERT = 256
ROUTED_SCALE = 2.5
MOE_BLOCK = 512
SLOT_BLOCK = 256
DN_ALPHA = 2.0 ** 0.25
EPS = 1e-5
EXP_CLAMP = 80.0
SUBLANES = 8

P_QA, P_KA, P_IA, P_GA, P_QKB, P_VB, P_OB, P_MA, P_MB = range(9)
N_SLABS = 9

VMEM_LIMIT = 56 * 1024 * 1024


def _params(*sem):
    return pltpu.CompilerParams(dimension_semantics=sem, vmem_limit_bytes=VMEM_LIMIT)


def _sigmoid(x):
    return 1.0 / (1.0 + jnp.exp(-x))


def _log_sigmoid(x):
    return jnp.minimum(x, 0.0) - jnp.log(1.0 + jnp.exp(-jnp.abs(x)))


def _layer_norm(x, g, b):
    xc = x - jnp.mean(x, axis=-1, keepdims=True)
    var = jnp.mean(xc * xc, axis=-1, keepdims=True)
    return xc * lax.rsqrt(var + EPS) * g + b


def _dot(a, b):
    return jnp.dot(a, b, preferred_element_type=F32)


def _dot_nt(a, b):
    return lax.dot_general(a, b, (((1,), (1,)), ((), ())), preferred_element_type=F32)


def _dot_tn(a, b):
    return lax.dot_general(a, b, (((0,), (0,)), ((), ())), preferred_element_type=F32)


def _split_bf16(x):
    hi = x.astype(BF16)
    lo = (x - hi.astype(F32)).astype(BF16)
    return hi, lo


def _ones_where(cond):
    return jnp.where(cond, 1.0, 0.0).astype(BF16)


def _inproj_kernel(x_ref, g_ref, b_ref, w_ref, ws_ref, lb_ref, gb_ref, p_ref, lf_ref, sg_ref):
    hb = _layer_norm(x_ref[...], g_ref[...], b_ref[...]).astype(BF16)
    s = _dot(hb, ws_ref[...]) + gb_ref[...]
    lane = lax.broadcasted_iota(I32, s.shape, 1)
    sg_ref[...] = jnp.where(lane < ML_HEADS, s, _log_sigmoid(s))
    for n in range(N_SLABS):
        acc = _dot(hb, w_ref[:, n * D_MODEL:(n + 1) * D_MODEL])
        if n == P_KA:
            lb = lb_ref[...]
            f = lb + (1.0 - lb) * _sigmoid(acc)
            lf_ref[...] = jnp.log(f)
            acc = 1.0 - f
        elif n == P_GA:
            acc = acc * _sigmoid(acc)
        elif n >= P_OB:
            acc = _sigmoid(acc)
        p_ref[n] = acc.astype(BF16)


def _inproj(x2d, ln_g, ln_b, w_cat, w_small, lb, gate_bias, tm):
    m = x2d.shape[0]
    const = lambda i: (0, 0)
    resident = lambda shape: pl.BlockSpec(shape, const, pipeline_mode=pl.Buffered(1))
    return pl.pallas_call(
        _inproj_kernel,
        grid=(m // tm,),
        in_specs=[
            pl.BlockSpec((tm, D_MODEL), lambda i: (i, 0)),
            resident((1, D_MODEL)),
            resident((1, D_MODEL)),
            resident((D_MODEL, N_SLABS * D_MODEL)),
            resident((D_MODEL, 128)),
            resident((1, D_MODEL)),
            resident((1, 128)),
        ],
        out_specs=[
            pl.BlockSpec((N_SLABS, tm, D_MODEL), lambda i: (0, i, 0)),
            pl.BlockSpec((tm, D_MODEL), lambda i: (i, 0)),
            pl.BlockSpec((tm, 128), lambda i: (i, 0)),
        ],
        out_shape=[
            jax.ShapeDtypeStruct((N_SLABS, m, D_MODEL), BF16),
            jax.ShapeDtypeStruct((m, D_MODEL), F32),
            jax.ShapeDtypeStruct((m, 128), F32),
        ],
        compiler_params=_params("parallel"),
        name="inproj",
    )(x2d, ln_g, ln_b, w_cat, w_small, lb, gate_bias)


def _block_rows(b, block, pick):
    c, w = b.shape
    parts = [jnp.broadcast_to(b[j * block + pick:j * block + pick + 1, :], (block, w))
             for j in range(c // block)]
    return parts[0] if len(parts) == 1 else jnp.concatenate(parts, axis=0)


def _hgrn_kernel(q_ref, k_ref, v_ref, g_ref, lf_ref, ng_ref, s0_ref, y_ref, sfin_ref, s_scr, *, chunk):
    c = pl.program_id(1)

    @pl.when(c == 0)
    def _():
        s_scr[...] = s0_ref[...]

    cs = chunk
    row = lax.broadcasted_iota(I32, (cs, cs), 0)
    col = lax.broadcasted_iota(I32, (cs, cs), 1)
    tri = _ones_where(col <= row)
    lf_hi, lf_lo = _split_bf16(lf_ref[...])
    b = _dot(tri, lf_hi) + _dot(tri, lf_lo)
    q = q_ref[...].astype(F32)
    k = k_ref[...].astype(F32)
    v = v_ref[...]
    blast = b[cs - 1:cs, :]
    qg = (q * jnp.exp(b)).astype(BF16)
    kg = (k * jnp.exp(blast - b)).astype(BF16)
    dec = jnp.exp(blast)

    levels = []
    m = SUBLANES
    while 2 * m <= cs:
        w = jnp.exp(-jnp.abs(b - _block_rows(b, 2 * m, m - 1)))
        sh = (2 * m).bit_length() - 1
        mask = ((row >> sh) == (col >> sh)) & ((row & (2 * m - 1)) >= m) & ((col & (2 * m - 1)) < m)
        levels.append(((q * w).astype(BF16), (k * w).astype(BF16), mask))
        m *= 2
    e = jnp.clip(b - _block_rows(b, SUBLANES, SUBLANES // 2 - 1), -EXP_CLAMP, EXP_CLAMP)
    levels.append(((q * jnp.exp(e)).astype(BF16), (k * jnp.exp(-e)).astype(BF16),
                   ((row >> 3) == (col >> 3)) & (col <= row)))

    ng = ng_ref[...]
    for h in range(HG_HEADS):
        sl = slice(h * HG_DK, (h + 1) * HG_DK)
        st = s_scr[h]
        o = _dot_nt(qg[:, sl], st.astype(BF16))
        sc = jnp.zeros((cs, cs), F32)
        for lq, lk, mask in levels:
            sc = jnp.where(mask, _dot_nt(lq[:, sl], lk[:, sl]), sc)
        o = o + _dot(sc.astype(BF16), v[:, sl])
        s_scr[h] = dec[:, sl] * st + _dot_tn(v[:, sl], kg[:, sl])
        ms = jnp.mean(o * o, axis=-1, keepdims=True)
        y = o * lax.rsqrt(ms + EPS) * ng[:, sl] * g_ref[:, sl].astype(F32)
        y_ref[:, sl] = y.astype(BF16)

    @pl.when(c == pl.num_programs(1) - 1)
    def _():
        sfin_ref[...] = s_scr[...]


def _hgrn(proj, logf, norm_g, s0, nb, chunk):
    m = logf.shape[0]
    nc = m // nb // chunk
    slab = lambda s: pl.BlockSpec((None, chunk, D_MODEL), lambda b, c, s=s: (s, b * nc + c, 0))
    return pl.pallas_call(
        functools.partial(_hgrn_kernel, chunk=chunk),
        grid=(nb, nc),
        in_specs=[
            slab(P_QA), slab(P_KA), slab(P_IA), slab(P_GA),
            pl.BlockSpec((chunk, D_MODEL), lambda b, c: (b * nc + c, 0)),
            pl.BlockSpec((1, D_MODEL), lambda b, c: (0, 0)),
            pl.BlockSpec((HG_HEADS, HG_DK, HG_DK), lambda b, c: (0, 0, 0)),
        ],
        out_specs=[
            pl.BlockSpec((chunk, D_MODEL), lambda b, c: (b * nc + c, 0)),
            pl.BlockSpec((None, HG_HEADS, HG_DK, HG_DK), lambda b, c: (b, 0, 0, 0)),
        ],
        out_shape=[
            jax.ShapeDtypeStruct((m, D_MODEL), BF16),
            jax.ShapeDtypeStruct((nb, HG_HEADS, HG_DK, HG_DK), F32),
        ],
        scratch_shapes=[pltpu.VMEM((HG_HEADS, HG_DK, HG_DK), F32)],
        compiler_params=_params("parallel", "arbitrary"),
        name="hgrn",
    )(proj, proj, proj, proj, logf, norm_g, s0)


def _mlstm_kernel(qk_ref, v_ref, og_ref, sg_ref, sgt_ref, cw_ref, cb_ref, ng_ref, c0_ref, m0_ref, x0_ref,
                  y_ref, cfin_ref, mfin_ref, xfin_ref, c_scr, m_scr, x_scr, *, chunk):
    c = pl.program_id(1)

    @pl.when(c == 0)
    def _():
        c_scr[...] = c0_ref[...]
        m_scr[...] = m0_ref[...]
        x_scr[...] = x0_ref[...]

    cs = chunk
    x = qk_ref[...].astype(F32)
    prev = x_scr[...]
    sub = lax.broadcasted_iota(I32, (SUBLANES, D_MODEL), 0)
    cw = cw_ref[...]
    conv = cw[3:4, :] * x + cb_ref[...]
    for j in (1, 2, 3):
        xs = pltpu.roll(x, j, 0)
        head = jnp.where(sub < j, pltpu.roll(prev, j, 0), xs[:SUBLANES, :])
        xs = jnp.concatenate([head, xs[SUBLANES:, :]], axis=0)
        conv = conv + cw[3 - j:4 - j, :] * xs
    x_scr[...] = x[cs - SUBLANES:, :]
    qk = conv * _sigmoid(conv)
    q_all = (qk[:, :ML_HEADS * ML_DK] * (ML_DK ** -0.5)).astype(BF16)
    k_all = qk[:, ML_HEADS * ML_DK:]

    row = lax.broadcasted_iota(I32, (cs, cs), 0)
    col = lax.broadcasted_iota(I32, (cs, cs), 1)
    causal = col <= row
    tri = _ones_where(causal)
    sg = sg_ref[...]
    sgt = sgt_ref[...]
    sg_hi, sg_lo = _split_bf16(sg)
    bcol_all = _dot(tri, sg_hi) + _dot(tri, sg_lo)
    sgt_hi, sgt_lo = _split_bf16(sgt)
    brow_all = _dot_nt(sgt_hi, tri) + _dot_nt(sgt_lo, tri)
    lane128 = lax.broadcasted_iota(I32, (cs, 128), 1)
    ones_col = _ones_where(lane128 == 0)
    v = v_ref[...]
    ng = ng_ref[...]

    for h in range(ML_HEADS):
        b_col = bcol_all[:, ML_HEADS + h:ML_HEADS + h + 1]
        b_row = brow_all[ML_HEADS + h:ML_HEADS + h + 1, :]
        ig_col = sg[:, h:h + 1]
        ig_row = sgt[h:h + 1, :]
        m_prev = m_scr[h:h + 1, 0:1]
        q_h = q_all[:, h * ML_DK:(h + 1) * ML_DK]
        k_h = k_all[:, h * ML_DK:(h + 1) * ML_DK]
        v_aug = jnp.concatenate([v[:, h * ML_DV:(h + 1) * ML_DV], ones_col], axis=1)
        c_st = c_scr[h]

        log_intra = jnp.where(causal, b_col - b_row + ig_row, -jnp.inf)
        log_inter = b_col + m_prev
        m_t = jnp.maximum(log_inter, jnp.max(log_intra, axis=-1, keepdims=True))
        w_intra = jnp.exp(log_intra - m_t)
        w_inter = jnp.exp(log_inter - m_t)
        s = _dot_nt(q_h, k_h.astype(BF16)) * w_intra
        tot = w_inter * _dot(q_h, c_st.astype(BF16)) + _dot(s.astype(BF16), v_aug)
        num = tot[:, :ML_DV]
        den = tot[:, ML_DV:ML_DV + 1]
        hid = num / jnp.maximum(jnp.abs(den), jnp.exp(-m_t))

        b_last = b_col[cs - 1:cs, :]
        log_w = b_last - b_col + ig_col
        m_new = jnp.maximum(b_last + m_prev, jnp.max(log_w, axis=0, keepdims=True))
        w_s = jnp.exp(log_w - m_new)
        decay = jnp.exp(b_last + m_prev - m_new)
        c_scr[h] = decay * c_st + _dot_tn((k_h * w_s).astype(BF16), v_aug)
        m_scr[h:h + 1, :] = jnp.broadcast_to(m_new, (1, 128))

        hc = hid - jnp.mean(hid, axis=-1, keepdims=True)
        var = jnp.mean(hc * hc, axis=-1, keepdims=True)
        sl = slice(h * ML_DV, (h + 1) * ML_DV)
        y = hc * lax.rsqrt(var + EPS) * ng[:, sl] * og_ref[:, sl].astype(F32)
        y_ref[:, sl] = y.astype(BF16)

    @pl.when(c == pl.num_programs(1) - 1)
    def _():
        cfin_ref[...] = c_scr[...]
        mfin_ref[...] = m_scr[...]
        xfin_ref[...] = x_scr[...]


def _mlstm(proj, sg, sgt, conv_w, conv_b, norm_g, c0, m0, x0, nb, chunk):
    m = sg.shape[0]
    nc = m // nb // chunk
    slab = lambda s: pl.BlockSpec((None, chunk, D_MODEL), lambda b, c, s=s: (s, b * nc + c, 0))
    const2 = lambda b, c: (0, 0)
    return pl.pallas_call(
        functools.partial(_mlstm_kernel, chunk=chunk),
        grid=(nb, nc),
        in_specs=[
            slab(P_QKB), slab(P_VB), slab(P_OB),
            pl.BlockSpec((chunk, 128), lambda b, c: (b * nc + c, 0)),
            pl.BlockSpec((None, SUBLANES, chunk), lambda b, c: (b * nc + c, 0, 0)),
            pl.BlockSpec((4, D_MODEL), const2),
            pl.BlockSpec((1, D_MODEL), const2),
            pl.BlockSpec((1, D_MODEL), const2),
            pl.BlockSpec((ML_HEADS, ML_DK, ML_AUG), lambda b, c: (0, 0, 0)),
            pl.BlockSpec((SUBLANES, 128), const2),
            pl.BlockSpec((SUBLANES, D_MODEL), const2),
        ],
        out_specs=[
            pl.BlockSpec((chunk, D_MODEL), lambda b, c: (b * nc + c, 0)),
            pl.BlockSpec((None, ML_HEADS, ML_DK, ML_AUG), lambda b, c: (b, 0, 0, 0)),
            pl.BlockSpec((None, SUBLANES, 128), lambda b, c: (b, 0, 0)),
            pl.BlockSpec((None, SUBLANES, D_MODEL), lambda b, c: (b, 0, 0)),
        ],
        out_shape=[
            jax.ShapeDtypeStruct((m, D_MODEL), BF16),
            jax.ShapeDtypeStruct((nb, ML_HEADS, ML_DK, ML_AUG), F32),
            jax.ShapeDtypeStruct((nb, SUBLANES, 128), F32),
            jax.ShapeDtypeStruct((nb, SUBLANES, D_MODEL), F32),
        ],
        scratch_shapes=[
            pltpu.VMEM((ML_HEADS, ML_DK, ML_AUG), F32),
            pltpu.VMEM((SUBLANES, 128), F32),
            pltpu.VMEM((SUBLANES, D_MODEL), F32),
        ],
        compiler_params=_params("parallel", "arbitrary"),
        name="mlstm",
    )(proj, proj, proj, sg, sgt, conv_w, conv_b, norm_g, c0, m0, x0)


def _merge_kernel(x_ref, ya_ref, yb_ref, ma_ref, mb_ref, wa_ref, wb_ref, wo_ref, eg_ref, eb_ref, g1_ref, b1_ref,
                  wrh_ref, wrl_ref, rb_ref,
                  h1_ref, slot_ref, gw_ref, cnt_ref, val_scr, *, tm):
    h0 = _layer_norm(x_ref[...], eg_ref[...], eb_ref[...])
    merged = (ma_ref[...].astype(F32) * _dot(ya_ref[...], wa_ref[...])
              + mb_ref[...].astype(F32) * _dot(yb_ref[...], wb_ref[...]))
    mix = _dot(merged.astype(BF16), wo_ref[...])
    h1 = _layer_norm(DN_ALPHA * h0 + mix, g1_ref[...], b1_ref[...])
    h1_ref[...] = h1

    h_hi, h_lo = _split_bf16(h1)
    logits = _dot_nt(wrh_ref[...], h_hi) + _dot_nt(wrh_ref[...], h_lo) + _dot_nt(wrl_ref[...], h_hi)
    scores = _sigmoid(logits)
    biased = scores + rb_ref[:, 0:1]
    neg_inf = -jnp.inf

    g3 = biased.reshape(N_GROUPS, GROUP_SIZE, tm)
    sub3 = lax.broadcasted_iota(I32, g3.shape, 1)
    top1 = jnp.max(g3, axis=1, keepdims=True)
    first = jnp.min(jnp.where(g3 == top1, sub3, GROUP_SIZE), axis=1, keepdims=True)
    top2 = jnp.max(jnp.where(sub3 == first, neg_inf, g3), axis=1, keepdims=True)
    gs = (top1 + top2).reshape(N_GROUPS, tm)
    gi = lax.broadcasted_iota(I32, gs.shape, 0)
    grank = jnp.zeros(gs.shape, F32)
    for j in range(N_GROUPS):
        r = gs[j:j + 1, :]
        grank = grank + jnp.where((r > gs) | ((r == gs) & (gi > j)), 1.0, 0.0)
    gsel = grank < float(TOPK_GROUPS)
    emask = jnp.broadcast_to(gsel.reshape(N_GROUPS, 1, tm), (N_GROUPS, GROUP_SIZE, tm)).reshape(N_EXPERTS, tm)
    masked = jnp.where(emask, biased, neg_inf)

    val_scr[...] = masked
    ei = lax.broadcasted_iota(I32, masked.shape, 0)

    def rank_body(j, rank):
        r = val_scr[pl.ds(j, 1), :]
        return rank + jnp.where((r > masked) | ((r == masked) & (ei > j)), 1.0, 0.0)

    rank = lax.fori_loop(0, N_EXPERTS, rank_body, jnp.zeros(masked.shape, F32))
    sel = rank < float(TOP_K)
    sel_w = jnp.where(sel, scores, 0.0)
    gwd = sel_w / jnp.sum(sel_w, axis=0, keepdims=True) * ROUTED_SCALE

    tr = lax.broadcasted_iota(I32, (tm, tm), 0)
    tc = lax.broadcasted_iota(I32, (tm, tm), 1)
    sel_b = _ones_where(sel)
    rloc = _dot(sel_b, _ones_where(tr < tc))
    cnt = _dot(sel_b, jnp.ones((tm, 128), BF16))
    cnt8 = jnp.floor((cnt + (SUBLANES - 1.0)) * (1.0 / SUBLANES)) * SUBLANES
    er = lax.broadcasted_iota(I32, (N_EXPERTS, N_EXPERTS), 0)
    ec = lax.broadcasted_iota(I32, (N_EXPERTS, N_EXPERTS), 1)
    seg_start = _dot(_ones_where(ec < er), cnt8.astype(BF16))
    slot_e = seg_start[:, 0:1] + rloc
    cnt_ref[...] = cnt

    s_rows, w_rows = [], []
    for kk in range(TOP_K):
        pick = sel & (rank == float(kk))
        s_rows.append(jnp.sum(jnp.where(pick, slot_e, 0.0), axis=0, keepdims=True))
        w_rows.append(jnp.sum(jnp.where(pick, gwd, 0.0), axis=0, keepdims=True))
    slot_ref[...] = jnp.concatenate(s_rows, axis=0).astype(I32)
    gw_ref[...] = jnp.concatenate(w_rows, axis=0)


def _merge(x2d, ya, yb, proj, w_a, w_b, w_o, eg, eb, g1, b1, wr_hi, wr_lo, rbias, tm):
    m = x2d.shape[0]
    tile = lambda i: (i, 0)
    const = lambda i: (0, 0)
    slab = lambda s: pl.BlockSpec((None, tm, D_MODEL), lambda i, s=s: (s, i, 0))
    wspec = pl.BlockSpec((D_MODEL, D_MODEL), const)
    vspec = pl.BlockSpec((1, D_MODEL), const)
    lane_tile = pl.BlockSpec((TOP_K, tm), lambda i: (0, i))
    return pl.pallas_call(
        functools.partial(_merge_kernel, tm=tm),
        grid=(m // tm,),
        in_specs=[
            pl.BlockSpec((tm, D_MODEL), tile), pl.BlockSpec((tm, D_MODEL), tile), pl.BlockSpec((tm, D_MODEL), tile),
            slab(P_MA), slab(P_MB), wspec, wspec, wspec, vspec, vspec, vspec, vspec,
            pl.BlockSpec((N_EXPERTS, D_MODEL), const), pl.BlockSpec((N_EXPERTS, D_MODEL), const),
            pl.BlockSpec((N_EXPERTS, 128), const),
        ],
        out_specs=[
            pl.BlockSpec((tm, D_MODEL), tile), lane_tile, lane_tile,
            pl.BlockSpec((None, N_EXPERTS, 128), lambda i: (i, 0, 0)),
        ],
        out_shape=[
            jax.ShapeDtypeStruct((m, D_MODEL), F32),
            jax.ShapeDtypeStruct((TOP_K, m), I32),
            jax.ShapeDtypeStruct((TOP_K, m), F32),
            jax.ShapeDtypeStruct((m // tm, N_EXPERTS, 128), F32),
        ],
        scratch_shapes=[pltpu.VMEM((N_EXPERTS, tm), F32)],
        compiler_params=_params("parallel"),
        name="merge_router",
    )(x2d, ya, yb, proj, proj, w_a, w_b, w_o, eg, eb, g1, b1, wr_hi, wr_lo, rbias)


def _tile_slots(tt):
    return -(-(TOP_K * tt + N_EXPERTS * (SUBLANES - 1)) // SLOT_BLOCK) * SLOT_BLOCK


def _dispatch_kernel(ng_ref, gdst_ref, slot_ref, h_ref, xs_ref, buf, sem, *, tt, s_tile):
    i = pl.program_id(0)
    cur = lax.rem(i, 2)

    def group_copy(b, g, d):
        src = buf.at[b, pl.ds(pl.multiple_of(g * SUBLANES, SUBLANES), SUBLANES), :]
        dst = xs_ref.at[pl.ds(pl.multiple_of(d, SUBLANES), SUBLANES), :]
        return pltpu.make_async_copy(src, dst, sem.at[b])

    def wait_tile(j, b):
        lax.fori_loop(0, ng_ref[j], lambda g, c: (group_copy(b, 0, 0).wait(), c)[1], 0)

    @pl.when(i >= 2)
    def _():
        wait_tile(i - 2, cur)

    hb = h_ref[...].astype(BF16)
    sl = slot_ref[...]
    for r in range(s_tile // SLOT_BLOCK):
        s_iota = lax.broadcasted_iota(I32, (SLOT_BLOCK, tt), 0) + r * SLOT_BLOCK
        p = jnp.zeros((SLOT_BLOCK, tt), F32)
        for kk in range(TOP_K):
            p = jnp.where(s_iota == sl[kk:kk + 1, :], 1.0, p)
        buf[cur, r * SLOT_BLOCK:(r + 1) * SLOT_BLOCK, :] = _dot(p.astype(BF16), hb)

    lax.fori_loop(0, ng_ref[i], lambda g, c: (group_copy(cur, g, gdst_ref[0, 0, g]).start(), c)[1], 0)

    @pl.when(i == pl.num_programs(0) - 1)
    def _():
        @pl.when(i >= 1)
        def _():
            wait_tile(i - 1, 1 - cur)
        wait_tile(i, cur)


def _dispatch(n_groups, gdst, slot_k, h1, n_slots, tt):
    m = h1.shape[0]
    s_tile = _tile_slots(tt)
    grid_spec = pltpu.PrefetchScalarGridSpec(
        num_scalar_prefetch=1,
        grid=(m // tt,),
        in_specs=[
            pl.BlockSpec((1, 1, s_tile // SUBLANES), lambda i, ng: (i, 0, 0), memory_space=pltpu.SMEM),
            pl.BlockSpec((TOP_K, tt), lambda i, ng: (0, i)),
            pl.BlockSpec((tt, D_MODEL), lambda i, ng: (i, 0)),
        ],
        out_specs=pl.BlockSpec(memory_space=pl.ANY),
        scratch_shapes=[pltpu.VMEM((2, s_tile, D_MODEL), F32), pltpu.SemaphoreType.DMA((2,))],
    )
    return pl.pallas_call(
        functools.partial(_dispatch_kernel, tt=tt, s_tile=s_tile),
        grid_spec=grid_spec,
        out_shape=jax.ShapeDtypeStruct((n_slots, D_MODEL), F32),
        compiler_params=_params("arbitrary"),
        name="dispatch",
    )(n_groups, gdst, slot_k, h1)


def _experts_kernel(be_ref, nu_ref, x_ref, wg_ref, wu_ref, wd_ref, y_ref):
    del be_ref
    i = pl.program_id(0)

    @pl.when(i < nu_ref[0])
    def _():
        xb = x_ref[...].astype(BF16)
        a = _dot(xb, wg_ref[...])
        u = _dot(xb, wu_ref[...])
        y_ref[...] = _dot((a * _sigmoid(a) * u).astype(BF16), wd_ref[...])


def _experts(blk_expert, n_used, xs, wg, wu, wd):
    n_slots = xs.shape[0]
    n_blocks = n_slots // MOE_BLOCK
    blk = lambda i, be, nu: (jnp.minimum(i, nu[0] - 1), 0)
    wsel = lambda i, be, nu: (be[jnp.minimum(i, nu[0] - 1)], 0, 0)
    grid_spec = pltpu.PrefetchScalarGridSpec(
        num_scalar_prefetch=2,
        grid=(n_blocks,),
        in_specs=[
            pl.BlockSpec((MOE_BLOCK, D_MODEL), blk),
            pl.BlockSpec((None, D_MODEL, D_EXPERT), wsel),
            pl.BlockSpec((None, D_MODEL, D_EXPERT), wsel),
            pl.BlockSpec((None, D_EXPERT, D_MODEL), wsel),
        ],
        out_specs=pl.BlockSpec((MOE_BLOCK, D_MODEL), blk),
    )
    return pl.pallas_call(
        _experts_kernel,
        grid_spec=grid_spec,
        out_shape=jax.ShapeDtypeStruct((n_slots, D_MODEL), F32),
        compiler_params=_params("arbitrary"),
        name="experts",
    )(blk_expert, n_used, xs, wg, wu, wd)


def _combine_kernel(ng_ref, gcur_ref, gnext_ref, slot_ref, gw_ref, h_ref, y_ref, sg_ref, su_ref, sd_ref, g2_ref, b2_ref,
                    o_ref, ybuf, sem, *, tt, s_tile):
    i = pl.program_id(0)
    cur = lax.rem(i, 2)

    def group_copy(b, g, d):
        src = y_ref.at[pl.ds(pl.multiple_of(d, SUBLANES), SUBLANES), :]
        dst = ybuf.at[b, pl.ds(pl.multiple_of(g * SUBLANES, SUBLANES), SUBLANES), :]
        return pltpu.make_async_copy(src, dst, sem.at[b])

    def fetch(j, b, table_ref):
        lax.fori_loop(0, ng_ref[j], lambda g, c: (group_copy(b, g, table_ref[0, 0, g]).start(), c)[1], 0)

    @pl.when(i == 0)
    def _():
        ybuf[...] = jnp.zeros_like(ybuf)
        fetch(0, 0, gcur_ref)

    @pl.when(i + 1 < pl.num_programs(0))
    def _():
        fetch(i + 1, 1 - cur, gnext_ref)

    h1 = h_ref[...]
    hb = h1.astype(BF16)
    a = _dot(hb, sg_ref[...])
    u = _dot(hb, su_ref[...])
    shared = _dot((a * _sigmoid(a) * u).astype(BF16), sd_ref[...])

    slot = slot_ref[...]
    gw = gw_ref[...]
    g_hi = gw.astype(BF16).astype(F32)
    g_lo = gw - g_hi
    lax.fori_loop(0, ng_ref[i], lambda g, c: (group_copy(cur, 0, 0).wait(), c)[1], 0)
    routed = jnp.zeros((tt, D_MODEL), F32)
    for r in range(s_tile // SLOT_BLOCK):
        lane = lax.broadcasted_iota(I32, (tt, SLOT_BLOCK), 1) + r * SLOT_BLOCK
        p_hi = jnp.zeros((tt, SLOT_BLOCK), F32)
        p_lo = jnp.zeros((tt, SLOT_BLOCK), F32)
        for kk in range(TOP_K):
            hit = lane == slot[:, kk:kk + 1]
            p_hi = jnp.where(hit, g_hi[:, kk:kk + 1], p_hi)
            p_lo = jnp.where(hit, g_lo[:, kk:kk + 1], p_lo)
        yb = ybuf[cur, r * SLOT_BLOCK:(r + 1) * SLOT_BLOCK, :].astype(BF16)
        routed = routed + _dot(p_hi.astype(BF16), yb) + _dot(p_lo.astype(BF16), yb)
    o_ref[...] = _layer_norm(DN_ALPHA * h1 + (routed + shared), g2_ref[...], b2_ref[...])


def _combine(n_groups, gdst, slot_t, gw_t, h1, y, wsg, wsu, wsd, g2, b2, tt):
    m = h1.shape[0]
    nt = m // tt
    s_tile = _tile_slots(tt)
    const = lambda i, ng: (0, 0)
    table = lambda f: pl.BlockSpec((1, 1, s_tile // SUBLANES), f, memory_space=pltpu.SMEM)
    grid_spec = pltpu.PrefetchScalarGridSpec(
        num_scalar_prefetch=1,
        grid=(nt,),
        in_specs=[
            table(lambda i, ng: (i, 0, 0)),
            table(lambda i, ng: (jnp.minimum(i + 1, nt - 1), 0, 0)),
            pl.BlockSpec((tt, TOP_K), lambda i, ng: (i, 0)),
            pl.BlockSpec((tt, TOP_K), lambda i, ng: (i, 0)),
            pl.BlockSpec((tt, D_MODEL), lambda i, ng: (i, 0)),
            pl.BlockSpec(memory_space=pl.ANY),
            pl.BlockSpec((D_MODEL, D_EXPERT), const),
            pl.BlockSpec((D_MODEL, D_EXPERT), const),
            pl.BlockSpec((D_EXPERT, D_MODEL), const),
            pl.BlockSpec((1, D_MODEL), const),
            pl.BlockSpec((1, D_MODEL), const),
        ],
        out_specs=pl.BlockSpec((tt, D_MODEL), lambda i, ng: (i, 0)),
        scratch_shapes=[pltpu.VMEM((2, s_tile, D_MODEL), F32), pltpu.SemaphoreType.DMA((2,))],
    )
    return pl.pallas_call(
        functools.partial(_combine_kernel, tt=tt, s_tile=s_tile),
        grid_spec=grid_spec,
        out_shape=jax.ShapeDtypeStruct((m, D_MODEL), F32),
        compiler_params=_params("arbitrary"),
        name="combine",
    )(n_groups, gdst, gdst, slot_t, gw_t, h1, y, wsg, wsu, wsd, g2, b2)


def _pick_tile(m, pref):
    t = min(pref, m)
    while m % t:
        t //= 2
    return t


def _gates_time_on_lanes(sg, chunk):
    m = sg.shape[0]
    return sg[:, :SUBLANES].reshape(m // chunk, chunk, SUBLANES).transpose(0, 2, 1)


def _forward(x, meta_tokens, ln_emb_g, ln_emb_b, w_in, hg_lb_logits, hg_norm_g, ml_conv_w, ml_conv_b,
             ml_ig_bias, ml_fg_bias, ml_norm_g, w_branch_a, w_branch_b, w_out, ln1_g, ln1_b,
             w_router, router_bias, w_exp_gate, w_exp_up, w_exp_down, w_sh_gate, w_sh_up, w_sh_down,
             ln2_g, ln2_b, *, chunk, tm_proj, tm_moe):
    nb, seq, d = x.shape
    m = nb * seq
    row = lambda a: a.reshape(1, -1).astype(F32)

    w = w_in[0]
    kw = HG_HEADS * HG_DK
    o_qa, o_fa, o_ia, o_ga = 0, kw, 2 * kw, 3 * kw
    o_qb = 4 * kw
    o_kb = o_qb + ML_HEADS * ML_DK
    o_vb = o_kb + ML_HEADS * ML_DK
    o_ob = o_vb + ML_HEADS * ML_DV
    o_ig = o_ob + ML_HEADS * ML_DV
    o_fg = o_ig + ML_HEADS
    o_ma = o_fg + ML_HEADS
    o_mb = o_ma + D_MODEL
    cols = lambda o, n: w[:, o:o + n]
    w_cat = jnp.concatenate([
        cols(o_qa, kw), cols(o_fa, kw), cols(o_ia, kw), cols(o_ga, kw),
        cols(o_qb, 2 * ML_HEADS * ML_DK), cols(o_vb, ML_HEADS * ML_DV), cols(o_ob, ML_HEADS * ML_DV),
        cols(o_ma, D_MODEL), cols(o_mb, D_MODEL)], axis=1).astype(BF16)
    w_small = jnp.pad(cols(o_ig, 2 * ML_HEADS), ((0, 0), (0, 128 - 2 * ML_HEADS))).astype(BF16)
    gate_bias = jnp.pad(jnp.concatenate([ml_ig_bias[0], ml_fg_bias[0]]).astype(F32), (0, 128 - 2 * ML_HEADS)).reshape(1, 128)
    lb = jax.nn.softmax(hg_lb_logits.astype(F32), axis=0)[0].reshape(1, -1)
    eg, eb = row(ln_emb_g), row(ln_emb_b)
    conv_w = ml_conv_w[0].astype(F32)
    conv_b = row(ml_conv_b[0])
    hgn, mln = row(hg_norm_g[0]), row(ml_norm_g[0])

    p_m, lf_m, sg_m = _inproj(meta_tokens.astype(F32), eg, eb, w_cat, w_small, lb, gate_bias, N_META)
    s_zero = jnp.zeros((HG_HEADS, HG_DK, HG_DK), F32)
    _, s0 = _hgrn(p_m, lf_m, hgn, s_zero, 1, N_META)
    sgt_m = _gates_time_on_lanes(sg_m, N_META)
    _, c0, m0, x0 = _mlstm(p_m, sg_m, sgt_m, conv_w, conv_b, mln,
                           jnp.zeros((ML_HEADS, ML_DK, ML_AUG), F32), jnp.zeros((SUBLANES, 128), F32),
                           jnp.zeros((SUBLANES, D_MODEL), F32), 1, N_META)

    x2d = x.reshape(m, d).astype(F32)
    proj, logf, sg = _inproj(x2d, eg, eb, w_cat, w_small, lb, gate_bias, tm_proj)
    ya, _ = _hgrn(proj, logf, hgn, s0[0], nb, chunk)
    sgt = _gates_time_on_lanes(sg, chunk)
    yb, _, _, _ = _mlstm(proj, sg, sgt, conv_w, conv_b, mln, c0[0], m0[0], x0[0], nb, chunk)

    wr = w_router[0].T.astype(F32)
    wr_hi, wr_lo = _split_bf16(wr)
    rbias = jnp.broadcast_to(router_bias[0].astype(F32).reshape(N_EXPERTS, 1), (N_EXPERTS, 128))
    tt = tm_moe
    h1, slot_k, gw, cnt = _merge(
        x2d, ya, yb, proj, w_branch_a[0].astype(BF16), w_branch_b[0].astype(BF16), w_out[0].astype(BF16),
        eg, eb, row(ln1_g[0]), row(ln1_b[0]), wr_hi, wr_lo, rbias, tt)

    nt = m // tt
    s_tile = _tile_slots(tt)
    cnt8 = (cnt[:, :, 0].astype(I32) + SUBLANES - 1) // SUBLANES * SUBLANES
    seg_end = jnp.cumsum(cnt8, axis=1)
    seg_off = seg_end - cnt8
    tile_rows = seg_end[:, -1]
    run = jnp.cumsum(cnt8, axis=0) - cnt8
    tot8 = jnp.sum(cnt8, axis=0)
    padded = (tot8 + MOE_BLOCK - 1) // MOE_BLOCK * MOE_BLOCK
    pends = jnp.cumsum(padded)
    gshift = (pends - padded)[None, :] + run - seg_off
    g_rows = jnp.arange(s_tile // SUBLANES, dtype=I32) * SUBLANES
    e_of_g = jnp.sum((seg_end[:, None, :] <= g_rows[None, :, None]).astype(I32), axis=-1)
    shift_g = jnp.sum(jnp.where(e_of_g[..., None] == jnp.arange(N_EXPERTS, dtype=I32), gshift[:, None, :], 0), axis=-1)
    gdst = jnp.where(g_rows[None, :] < tile_rows[:, None], g_rows[None, :] + shift_g, 0).astype(I32)
    gdst = gdst.reshape(nt, 1, s_tile // SUBLANES)
    n_groups = (tile_rows // SUBLANES).astype(I32)
    n_blocks = -(-(m * TOP_K + nt * N_EXPERTS * (SUBLANES - 1)) // MOE_BLOCK) + N_EXPERTS
    blk_start = jnp.arange(n_blocks, dtype=I32) * MOE_BLOCK
    blk_expert = jnp.minimum(jnp.sum((pends[None, :] <= blk_start[:, None]).astype(I32), axis=1), N_EXPERTS - 1)
    n_used = (pends[-1:] // MOE_BLOCK).astype(I32)

    xs = _dispatch(n_groups, gdst, slot_k, h1, n_blocks * MOE_BLOCK, tt)
    y = _experts(blk_expert, n_used, xs, w_exp_gate[0].astype(BF16), w_exp_up[0].astype(BF16),
                 w_exp_down[0].astype(BF16))
    out = _combine(n_groups, gdst, slot_k.T, gw.T, h1, y, w_sh_gate[0].astype(BF16), w_sh_up[0].astype(BF16),
                   w_sh_down[0].astype(BF16), row(ln2_g[0]), row(ln2_b[0]), tt)
    return out.reshape(nb, seq, d).astype(x.dtype)


def kernel(x, meta_tokens, ln_emb_g, ln_emb_b, w_in, hg_lb_logits, hg_norm_g, ml_conv_w, ml_conv_b, ml_ig_bias, ml_fg_bias, ml_norm_g, w_branch_a, w_branch_b, w_out, ln1_g, ln1_b, w_router, router_bias, w_exp_gate, w_exp_up, w_exp_down, w_sh_gate, w_sh_up, w_sh_down, ln2_g, ln2_b):
    m = x.shape[0] * x.shape[1]
    return _forward(x, meta_tokens, ln_emb_g, ln_emb_b, w_in, hg_lb_logits, hg_norm_g, ml_conv_w, ml_conv_b,
                    ml_ig_bias, ml_fg_bias, ml_norm_g, w_branch_a, w_branch_b, w_out, ln1_g, ln1_b,
                    w_router, router_bias, w_exp_gate, w_exp_up, w_exp_down, w_sh_gate, w_sh_up, w_sh_down,
                    ln2_g, ln2_b, chunk=_pick_tile(x.shape[1], 256), tm_proj=_pick_tile(m, 512),
                    tm_moe=_pick_tile(m, 256))
```

```python
import functools

import jax
import jax.numpy as jnp
from jax import lax
from jax.experimental import pallas as pl
from jax.experimental.pallas import tpu as pltpu

F32, BF16, I32 = jnp.float32, jnp.bfloat16, jnp.int32

D_MODEL = 1024
N_META = 16
HG_HEADS = 8
HG_DK = 128
ML_HEADS = 4
ML_DK = 128
ML_DV = 256
ML_AUG = ML_DV + 128
N_EXPERTS = 64
TOP_K = 8
N_GROUPS = 8
GROUP_SIZE = N_EXPERTS // N_GROUPS
TOPK_GROUPS = 4
D_EXPERT = 256
ROUTED_SCALE = 2.5
MOE_BLOCK = 512
SLOT_BLOCK = 256
DN_ALPHA = 2.0 ** 0.25
EPS = 1e-5
EXP_CLAMP = 80.0
SUBLANES = 8
GROUP_ROWS = 16

P_QA, P_KA, P_IA, P_GA, P_QKB, P_VB, P_OB, P_MA, P_MB = range(9)
N_SLABS = 9

VMEM_LIMIT = 56 * 1024 * 1024


def _params(*sem):
    return pltpu.CompilerParams(dimension_semantics=sem, vmem_limit_bytes=VMEM_LIMIT)


def _sigmoid(x):
    return 1.0 / (1.0 + jnp.exp(-x))


def _log_sigmoid(x):
    return jnp.minimum(x, 0.0) - jnp.log(1.0 + jnp.exp(-jnp.abs(x)))


def _layer_norm(x, g, b):
    xc = x - jnp.mean(x, axis=-1, keepdims=True)
    var = jnp.mean(xc * xc, axis=-1, keepdims=True)
    return xc * lax.rsqrt(var + EPS) * g + b


def _dot(a, b):
    return jnp.dot(a, b, preferred_element_type=F32)


def _dot_nt(a, b):
    return lax.dot_general(a, b, (((1,), (1,)), ((), ())), preferred_element_type=F32)


def _dot_tn(a, b):
    return lax.dot_general(a, b, (((0,), (0,)), ((), ())), preferred_element_type=F32)


def _split_bf16(x):
    hi = x.astype(BF16)
    lo = (x - hi.astype(F32)).astype(BF16)
    return hi, lo


def _ones_where(cond):
    return jnp.where(cond, 1.0, 0.0).astype(BF16)


def _inproj_kernel(x_ref, g_ref, b_ref, w_ref, ws_ref, lb_ref, gb_ref, p_ref, lf_ref, sg_ref):
    hb = _layer_norm(x_ref[...], g_ref[...], b_ref[...]).astype(BF16)
    s = _dot(hb, ws_ref[...]) + gb_ref[...]
    lane = lax.broadcasted_iota(I32, s.shape, 1)
    sg_ref[...] = jnp.where(lane < ML_HEADS, s, _log_sigmoid(s))
    for n in range(N_SLABS):
        acc = _dot(hb, w_ref[:, n * D_MODEL:(n + 1) * D_MODEL])
        if n == P_KA:
            lb = lb_ref[...]
            f = lb + (1.0 - lb) * _sigmoid(acc)
            lf_ref[...] = jnp.log(f)
            acc = 1.0 - f
        elif n == P_GA:
            acc = acc * _sigmoid(acc)
        elif n >= P_OB:
            acc = _sigmoid(acc)
        p_ref[n] = acc.astype(BF16)


def _inproj(x2d, ln_g, ln_b, w_cat, w_small, lb, gate_bias, tm):
    m = x2d.shape[0]
    const = lambda i: (0, 0)
    resident = lambda shape: pl.BlockSpec(shape, const, pipeline_mode=pl.Buffered(1))
    return pl.pallas_call(
        _inproj_kernel,
        grid=(m // tm,),
        in_specs=[
            pl.BlockSpec((tm, D_MODEL), lambda i: (i, 0)),
            resident((1, D_MODEL)),
            resident((1, D_MODEL)),
            resident((D_MODEL, N_SLABS * D_MODEL)),
            resident((D_MODEL, 128)),
            resident((1, D_MODEL)),
            resident((1, 128)),
        ],
        out_specs=[
            pl.BlockSpec((N_SLABS, tm, D_MODEL), lambda i: (0, i, 0)),
            pl.BlockSpec((tm, D_MODEL), lambda i: (i, 0)),
            pl.BlockSpec((tm, 128), lambda i: (i, 0)),
        ],
        out_shape=[
            jax.ShapeDtypeStruct((N_SLABS, m, D_MODEL), BF16),
            jax.ShapeDtypeStruct((m, D_MODEL), F32),
            jax.ShapeDtypeStruct((m, 128), F32),
        ],
        compiler_params=_params("parallel"),
        name="inproj",
    )(x2d, ln_g, ln_b, w_cat, w_small, lb, gate_bias)


def _block_rows(b, block, pick):
    c, w = b.shape
    parts = [jnp.broadcast_to(b[j * block + pick:j * block + pick + 1, :], (block, w))
             for j in range(c // block)]
    return parts[0] if len(parts) == 1 else jnp.concatenate(parts, axis=0)


def _hgrn_kernel(q_ref, k_ref, v_ref, g_ref, lf_ref, ng_ref, s0_ref, y_ref, sfin_ref, s_scr, *, chunk):
    c = pl.program_id(1)

    @pl.when(c == 0)
    def _():
        s_scr[...] = s0_ref[...]

    cs = chunk
    row = lax.broadcasted_iota(I32, (cs, cs), 0)
    col = lax.broadcasted_iota(I32, (cs, cs), 1)
    tri = _ones_where(col <= row)
    lf_hi, lf_lo = _split_bf16(lf_ref[...])
    b = _dot(tri, lf_hi) + _dot(tri, lf_lo)
    q = q_ref[...].astype(F32)
    k = k_ref[...].astype(F32)
    v = v_ref[...]
    blast = b[cs - 1:cs, :]
    qg = (q * jnp.exp(b)).astype(BF16)
    kg = (k * jnp.exp(blast - b)).astype(BF16)
    dec = jnp.exp(blast)

    levels = []
    m = SUBLANES
    while 2 * m <= cs:
        w = jnp.exp(-jnp.abs(b - _block_rows(b, 2 * m, m - 1)))
        sh = (2 * m).bit_length() - 1
        mask = ((row >> sh) == (col >> sh)) & ((row & (2 * m - 1)) >= m) & ((col & (2 * m - 1)) < m)
        levels.append(((q * w).astype(BF16), (k * w).astype(BF16), mask))
        m *= 2
    e = jnp.clip(b - _block_rows(b, SUBLANES, SUBLANES // 2 - 1), -EXP_CLAMP, EXP_CLAMP)
    levels.append(((q * jnp.exp(e)).astype(BF16), (k * jnp.exp(-e)).astype(BF16),
                   ((row >> 3) == (col >> 3)) & (col <= row)))

    ng = ng_ref[...]
    for h in range(HG_HEADS):
        sl = slice(h * HG_DK, (h + 1) * HG_DK)
        st = s_scr[h]
        o = _dot_nt(qg[:, sl], st.astype(BF16))
        sc = jnp.zeros((cs, cs), F32)
        for lq, lk, mask in levels:
            sc = jnp.where(mask, _dot_nt(lq[:, sl], lk[:, sl]), sc)
        o = o + _dot(sc.astype(BF16), v[:, sl])
        s_scr[h] = dec[:, sl] * st + _dot_tn(v[:, sl], kg[:, sl])
        ms = jnp.mean(o * o, axis=-1, keepdims=True)
        y = o * lax.rsqrt(ms + EPS) * ng[:, sl] * g_ref[:, sl].astype(F32)
        y_ref[:, sl] = y.astype(BF16)

    @pl.when(c == pl.num_programs(1) - 1)
    def _():
        sfin_ref[...] = s_scr[...]


def _hgrn(proj, logf, norm_g, s0, nb, chunk):
    m = logf.shape[0]
    nc = m // nb // chunk
    slab = lambda s: pl.BlockSpec((None, chunk, D_MODEL), lambda b, c, s=s: (s, b * nc + c, 0))
    return pl.pallas_call(
        functools.partial(_hgrn_kernel, chunk=chunk),
        grid=(nb, nc),
        in_specs=[
            slab(P_QA), slab(P_KA), slab(P_IA), slab(P_GA),
            pl.BlockSpec((chunk, D_MODEL), lambda b, c: (b * nc + c, 0)),
            pl.BlockSpec((1, D_MODEL), lambda b, c: (0, 0)),
            pl.BlockSpec((HG_HEADS, HG_DK, HG_DK), lambda b, c: (0, 0, 0)),
        ],
        out_specs=[
            pl.BlockSpec((chunk, D_MODEL), lambda b, c: (b * nc + c, 0)),
            pl.BlockSpec((None, HG_HEADS, HG_DK, HG_DK), lambda b, c: (b, 0, 0, 0)),
        ],
        out_shape=[
            jax.ShapeDtypeStruct((m, D_MODEL), BF16),
            jax.ShapeDtypeStruct((nb, HG_HEADS, HG_DK, HG_DK), F32),
        ],
        scratch_shapes=[pltpu.VMEM((HG_HEADS, HG_DK, HG_DK), F32)],
        compiler_params=_params("parallel", "arbitrary"),
        name="hgrn",
    )(proj, proj, proj, proj, logf, norm_g, s0)


def _mlstm_kernel(qk_ref, v_ref, og_ref, sg_ref, sgt_ref, cw_ref, cb_ref, ng_ref, c0_ref, m0_ref, x0_ref,
                  y_ref, cfin_ref, mfin_ref, xfin_ref, c_scr, m_scr, x_scr, *, chunk):
    c = pl.program_id(1)

    @pl.when(c == 0)
    def _():
        c_scr[...] = c0_ref[...]
        m_scr[...] = m0_ref[...]
        x_scr[...] = x0_ref[...]

    cs = chunk
    x = qk_ref[...].astype(F32)
    prev = x_scr[...]
    sub = lax.broadcasted_iota(I32, (SUBLANES, D_MODEL), 0)
    cw = cw_ref[...]
    conv = cw[3:4, :] * x + cb_ref[...]
    for j in (1, 2, 3):
        xs = pltpu.roll(x, j, 0)
        head = jnp.where(sub < j, pltpu.roll(prev, j, 0), xs[:SUBLANES, :])
        xs = jnp.concatenate([head, xs[SUBLANES:, :]], axis=0)
        conv = conv + cw[3 - j:4 - j, :] * xs
    x_scr[...] = x[cs - SUBLANES:, :]
    qk = conv * _sigmoid(conv)
    q_all = (qk[:, :ML_HEADS * ML_DK] * (ML_DK ** -0.5)).astype(BF16)
    k_all = qk[:, ML_HEADS * ML_DK:]

    row = lax.broadcasted_iota(I32, (cs, cs), 0)
    col = lax.broadcasted_iota(I32, (cs, cs), 1)
    causal = col <= row
    tri = _ones_where(causal)
    sg = sg_ref[...]
    sgt = sgt_ref[...]
    sg_hi, sg_lo = _split_bf16(sg)
    bcol_all = _dot(tri, sg_hi) + _dot(tri, sg_lo)
    sgt_hi, sgt_lo = _split_bf16(sgt)
    brow_all = _dot_nt(sgt_hi, tri) + _dot_nt(sgt_lo, tri)
    lane128 = lax.broadcasted_iota(I32, (cs, 128), 1)
    ones_col = _ones_where(lane128 == 0)
    v = v_ref[...]
    ng = ng_ref[...]

    for h in range(ML_HEADS):
        b_col = bcol_all[:, ML_HEADS + h:ML_HEADS + h + 1]
        b_row = brow_all[ML_HEADS + h:ML_HEADS + h + 1, :]
        ig_col = sg[:, h:h + 1]
        ig_row = sgt[h:h + 1, :]
        m_prev = m_scr[h:h + 1, 0:1]
        q_h = q_all[:, h * ML_DK:(h + 1) * ML_DK]
        k_h = k_all[:, h * ML_DK:(h + 1) * ML_DK]
        v_aug = jnp.concatenate([v[:, h * ML_DV:(h + 1) * ML_DV], ones_col], axis=1)
        c_st = c_scr[h]

        log_intra = jnp.where(causal, b_col - b_row + ig_row, -jnp.inf)
        log_inter = b_col + m_prev
        m_t = jnp.maximum(log_inter, jnp.max(log_intra, axis=-1, keepdims=True))
        w_intra = jnp.exp(log_intra - m_t)
        w_inter = jnp.exp(log_inter - m_t)
        s = _dot_nt(q_h, k_h.astype(BF16)) * w_intra
        tot = w_inter * _dot(q_h, c_st.astype(BF16)) + _dot(s.astype(BF16), v_aug)
        num = tot[:, :ML_DV]
        den = tot[:, ML_DV:ML_DV + 1]
        hid = num / jnp.maximum(jnp.abs(den), jnp.exp(-m_t))

        b_last = b_col[cs - 1:cs, :]
        log_w = b_last - b_col + ig_col
        m_new = jnp.maximum(b_last + m_prev, jnp.max(log_w, axis=0, keepdims=True))
        w_s = jnp.exp(log_w - m_new)
        decay = jnp.exp(b_last + m_prev - m_new)
        c_scr[h] = decay * c_st + _dot_tn((k_h * w_s).astype(BF16), v_aug)
        m_scr[h:h + 1, :] = jnp.broadcast_to(m_new, (1, 128))

        hc = hid - jnp.mean(hid, axis=-1, keepdims=True)
        var = jnp.mean(hc * hc, axis=-1, keepdims=True)
        sl = slice(h * ML_DV, (h + 1) * ML_DV)
        y = hc * lax.rsqrt(var + EPS) * ng[:, sl] * og_ref[:, sl].astype(F32)
        y_ref[:, sl] = y.astype(BF16)

    @pl.when(c == pl.num_programs(1) - 1)
    def _():
        cfin_ref[...] = c_scr[...]
        mfin_ref[...] = m_scr[...]
        xfin_ref[...] = x_scr[...]


def _mlstm(proj, sg, sgt, conv_w, conv_b, norm_g, c0, m0, x0, nb, chunk):
    m = sg.shape[0]
    nc = m // nb // chunk
    slab = lambda s: pl.BlockSpec((None, chunk, D_MODEL), lambda b, c, s=s: (s, b * nc + c, 0))
    const2 = lambda b, c: (0, 0)
    return pl.pallas_call(
        functools.partial(_mlstm_kernel, chunk=chunk),
        grid=(nb, nc),
        in_specs=[
            slab(P_QKB), slab(P_VB), slab(P_OB),
            pl.BlockSpec((chunk, 128), lambda b, c: (b * nc + c, 0)),
            pl.BlockSpec((None, SUBLANES, chunk), lambda b, c: (b * nc + c, 0, 0)),
            pl.BlockSpec((4, D_MODEL), const2),
            pl.BlockSpec((1, D_MODEL), const2),
            pl.BlockSpec((1, D_MODEL), const2),
            pl.BlockSpec((ML_HEADS, ML_DK, ML_AUG), lambda b, c: (0, 0, 0)),
            pl.BlockSpec((SUBLANES, 128), const2),
            pl.BlockSpec((SUBLANES, D_MODEL), const2),
        ],
        out_specs=[
            pl.BlockSpec((chunk, D_MODEL), lambda b, c: (b * nc + c, 0)),
            pl.BlockSpec((None, ML_HEADS, ML_DK, ML_AUG), lambda b, c: (b, 0, 0, 0)),
            pl.BlockSpec((None, SUBLANES, 128), lambda b, c: (b, 0, 0)),
            pl.BlockSpec((None, SUBLANES, D_MODEL), lambda b, c: (b, 0, 0)),
        ],
        out_shape=[
            jax.ShapeDtypeStruct((m, D_MODEL), BF16),
            jax.ShapeDtypeStruct((nb, ML_HEADS, ML_DK, ML_AUG), F32),
            jax.ShapeDtypeStruct((nb, SUBLANES, 128), F32),
            jax.ShapeDtypeStruct((nb, SUBLANES, D_MODEL), F32),
        ],
        scratch_shapes=[
            pltpu.VMEM((ML_HEADS, ML_DK, ML_AUG), F32),
            pltpu.VMEM((SUBLANES, 128), F32),
            pltpu.VMEM((SUBLANES, D_MODEL), F32),
        ],
        compiler_params=_params("parallel", "arbitrary"),
        name="mlstm",
    )(proj, proj, proj, sg, sgt, conv_w, conv_b, norm_g, c0, m0, x0)


def _merge_kernel(x_ref, ya_ref, yb_ref, ma_ref, mb_ref, wa_ref, wb_ref, wo_ref, eg_ref, eb_ref, g1_ref, b1_ref,
                  wrh_ref, wrl_ref, rb_ref,
                  h1_ref, slot_ref, gw_ref, cnt_ref, val_scr, *, tm):
    h0 = _layer_norm(x_ref[...], eg_ref[...], eb_ref[...])
    merged = (ma_ref[...].astype(F32) * _dot(ya_ref[...], wa_ref[...])
              + mb_ref[...].astype(F32) * _dot(yb_ref[...], wb_ref[...]))
    mix = _dot(merged.astype(BF16), wo_ref[...])
    h1 = _layer_norm(DN_ALPHA * h0 + mix, g1_ref[...], b1_ref[...])
    h1_ref[...] = h1

    h_hi, h_lo = _split_bf16(h1)
    logits = _dot_nt(wrh_ref[...], h_hi) + _dot_nt(wrh_ref[...], h_lo) + _dot_nt(wrl_ref[...], h_hi)
    scores = _sigmoid(logits)
    biased = scores + rb_ref[:, 0:1]
    neg_inf = -jnp.inf

    g3 = biased.reshape(N_GROUPS, GROUP_SIZE, tm)
    sub3 = lax.broadcasted_iota(I32, g3.shape, 1)
    top1 = jnp.max(g3, axis=1, keepdims=True)
    first = jnp.min(jnp.where(g3 == top1, sub3, GROUP_SIZE), axis=1, keepdims=True)
    top2 = jnp.max(jnp.where(sub3 == first, neg_inf, g3), axis=1, keepdims=True)
    gs = (top1 + top2).reshape(N_GROUPS, tm)
    gi = lax.broadcasted_iota(I32, gs.shape, 0)
    grank = jnp.zeros(gs.shape, F32)
    for j in range(N_GROUPS):
        r = gs[j:j + 1, :]
        grank = grank + jnp.where((r > gs) | ((r == gs) & (gi > j)), 1.0, 0.0)
    gsel = grank < float(TOPK_GROUPS)
    emask = jnp.broadcast_to(gsel.reshape(N_GROUPS, 1, tm), (N_GROUPS, GROUP_SIZE, tm)).reshape(N_EXPERTS, tm)
    masked = jnp.where(emask, biased, neg_inf)

    val_scr[...] = masked
    ei = lax.broadcasted_iota(I32, masked.shape, 0)

    def rank_body(j, rank):
        r = val_scr[pl.ds(j, 1), :]
        return rank + jnp.where((r > masked) | ((r == masked) & (ei > j)), 1.0, 0.0)

    rank = lax.fori_loop(0, N_EXPERTS, rank_body, jnp.zeros(masked.shape, F32))
    sel = rank < float(TOP_K)
    sel_w = jnp.where(sel, scores, 0.0)
    gwd = sel_w / jnp.sum(sel_w, axis=0, keepdims=True) * ROUTED_SCALE

    tr = lax.broadcasted_iota(I32, (tm, tm), 0)
    tc = lax.broadcasted_iota(I32, (tm, tm), 1)
    sel_b = _ones_where(sel)
    rloc = _dot(sel_b, _ones_where(tr < tc))
    cnt = _dot(sel_b, jnp.ones((tm, 128), BF16))
    cnt_g = jnp.floor((cnt + (GROUP_ROWS - 1.0)) * (1.0 / GROUP_ROWS)) * GROUP_ROWS
    er = lax.broadcasted_iota(I32, (N_EXPERTS, N_EXPERTS), 0)
    ec = lax.broadcasted_iota(I32, (N_EXPERTS, N_EXPERTS), 1)
    seg_start = _dot(_ones_where(ec < er), cnt_g.astype(BF16))
    slot_e = seg_start[:, 0:1] + rloc
    cnt_ref[...] = cnt

    s_rows, w_rows = [], []
    for kk in range(TOP_K):
        pick = sel & (rank == float(kk))
        s_rows.append(jnp.sum(jnp.where(pick, slot_e, 0.0), axis=0, keepdims=True))
        w_rows.append(jnp.sum(jnp.where(pick, gwd, 0.0), axis=0, keepdims=True))
    slot_ref[...] = jnp.concatenate(s_rows, axis=0).astype(I32)
    gw_ref[...] = jnp.concatenate(w_rows, axis=0)


def _merge(x2d, ya, yb, proj, w_a, w_b, w_o, eg, eb, g1, b1, wr_hi, wr_lo, rbias, tm):
    m = x2d.shape[0]
    tile = lambda i: (i, 0)
    const = lambda i: (0, 0)
    slab = lambda s: pl.BlockSpec((None, tm, D_MODEL), lambda i, s=s: (s, i, 0))
    wspec = pl.BlockSpec((D_MODEL, D_MODEL), const)
    vspec = pl.BlockSpec((1, D_MODEL), const)
    lane_tile = pl.BlockSpec((TOP_K, tm), lambda i: (0, i))
    return pl.pallas_call(
        functools.partial(_merge_kernel, tm=tm),
        grid=(m // tm,),
        in_specs=[
            pl.BlockSpec((tm, D_MODEL), tile), pl.BlockSpec((tm, D_MODEL), tile), pl.BlockSpec((tm, D_MODEL), tile),
            slab(P_MA), slab(P_MB), wspec, wspec, wspec, vspec, vspec, vspec, vspec,
            pl.BlockSpec((N_EXPERTS, D_MODEL), const), pl.BlockSpec((N_EXPERTS, D_MODEL), const),
            pl.BlockSpec((N_EXPERTS, 128), const),
        ],
        out_specs=[
            pl.BlockSpec((tm, D_MODEL), tile), lane_tile, lane_tile,
            pl.BlockSpec((None, N_EXPERTS, 128), lambda i: (i, 0, 0)),
        ],
        out_shape=[
            jax.ShapeDtypeStruct((m, D_MODEL), F32),
            jax.ShapeDtypeStruct((TOP_K, m), I32),
            jax.ShapeDtypeStruct((TOP_K, m), F32),
            jax.ShapeDtypeStruct((m // tm, N_EXPERTS, 128), F32),
        ],
        scratch_shapes=[pltpu.VMEM((N_EXPERTS, tm), F32)],
        compiler_params=_params("parallel"),
        name="merge_router",
    )(x2d, ya, yb, proj, proj, w_a, w_b, w_o, eg, eb, g1, b1, wr_hi, wr_lo, rbias)


def _tile_slots(tt):
    return -(-(TOP_K * tt + N_EXPERTS * (GROUP_ROWS - 1)) // SLOT_BLOCK) * SLOT_BLOCK


def _dispatch_kernel(ng_ref, gdst_ref, slot_ref, h_ref, xs_ref, buf, sem, *, tt, s_tile):
    i = pl.program_id(0)
    cur = lax.rem(i, 2)

    def group_copy(b, g, d):
        src = buf.at[b, pl.ds(pl.multiple_of(g * GROUP_ROWS, GROUP_ROWS), GROUP_ROWS), :]
        dst = xs_ref.at[pl.ds(pl.multiple_of(d, GROUP_ROWS), GROUP_ROWS), :]
        return pltpu.make_async_copy(src, dst, sem.at[b])

    def wait_tile(j, b):
        lax.fori_loop(0, ng_ref[j], lambda g, c: (group_copy(b, 0, 0).wait(), c)[1], 0)

    @pl.when(i >= 2)
    def _():
        wait_tile(i - 2, cur)

    hb = h_ref[...].astype(BF16)
    sl = slot_ref[...]
    for r in range(s_tile // SLOT_BLOCK):
        s_iota = lax.broadcasted_iota(I32, (SLOT_BLOCK, tt), 0) + r * SLOT_BLOCK
        p = jnp.zeros((SLOT_BLOCK, tt), F32)
        for kk in range(TOP_K):
            p = jnp.where(s_iota == sl[kk:kk + 1, :], 1.0, p)
        buf[cur, r * SLOT_BLOCK:(r + 1) * SLOT_BLOCK, :] = _dot(p.astype(BF16), hb).astype(BF16)

    lax.fori_loop(0, ng_ref[i], lambda g, c: (group_copy(cur, g, gdst_ref[0, 0, g]).start(), c)[1], 0)

    @pl.when(i == pl.num_programs(0) - 1)
    def _():
        @pl.when(i >= 1)
        def _():
            wait_tile(i - 1, 1 - cur)
        wait_tile(i, cur)


def _dispatch(n_groups, gdst, slot_k, h1, n_slots, tt):
    m = h1.shape[0]
    s_tile = _tile_slots(tt)
    grid_spec = pltpu.PrefetchScalarGridSpec(
        num_scalar_prefetch=1,
        grid=(m // tt,),
        in_specs=[
            pl.BlockSpec((1, 1, s_tile // GROUP_ROWS), lambda i, ng: (i, 0, 0), memory_space=pltpu.SMEM),
            pl.BlockSpec((TOP_K, tt), lambda i, ng: (0, i)),
            pl.BlockSpec((tt, D_MODEL), lambda i, ng: (i, 0)),
        ],
        out_specs=pl.BlockSpec(memory_space=pl.ANY),
        scratch_shapes=[pltpu.VMEM((2, s_tile, D_MODEL), BF16), pltpu.SemaphoreType.DMA((2,))],
    )
    return pl.pallas_call(
        functools.partial(_dispatch_kernel, tt=tt, s_tile=s_tile),
        grid_spec=grid_spec,
        out_shape=jax.ShapeDtypeStruct((n_slots, D_MODEL), BF16),
        compiler_params=_params("arbitrary"),
        name="dispatch",
    )(n_groups, gdst, slot_k, h1)


def _experts_kernel(be_ref, nu_ref, x_ref, wg_ref, wu_ref, wd_ref, y_ref):
    del be_ref
    i = pl.program_id(0)

    @pl.when(i < nu_ref[0])
    def _():
        xb = x_ref[...]
        a = _dot(xb, wg_ref[...])
        u = _dot(xb, wu_ref[...])
        y_ref[...] = _dot((a * _sigmoid(a) * u).astype(BF16), wd_ref[...]).astype(BF16)


def _experts(blk_expert, n_used, xs, wg, wu, wd):
    n_slots = xs.shape[0]
    n_blocks = n_slots // MOE_BLOCK
    blk = lambda i, be, nu: (jnp.minimum(i, nu[0] - 1), 0)
    wsel = lambda i, be, nu: (be[jnp.minimum(i, nu[0] - 1)], 0, 0)
    grid_spec = pltpu.PrefetchScalarGridSpec(
        num_scalar_prefetch=2,
        grid=(n_blocks,),
        in_specs=[
            pl.BlockSpec((MOE_BLOCK, D_MODEL), blk),
            pl.BlockSpec((None, D_MODEL, D_EXPERT), wsel),
            pl.BlockSpec((None, D_MODEL, D_EXPERT), wsel),
            pl.BlockSpec((None, D_EXPERT, D_MODEL), wsel),
        ],
        out_specs=pl.BlockSpec((MOE_BLOCK, D_MODEL), blk),
    )
    return pl.pallas_call(
        _experts_kernel,
        grid_spec=grid_spec,
        out_shape=jax.ShapeDtypeStruct((n_slots, D_MODEL), BF16),
        compiler_params=_params("arbitrary"),
        name="experts",
    )(blk_expert, n_used, xs, wg, wu, wd)


def _combine_kernel(ng_ref, gcur_ref, gnext_ref, slot_ref, gw_ref, h_ref, y_ref, sg_ref, su_ref, sd_ref, g2_ref, b2_ref,
                    o_ref, ybuf, sem, *, tt, s_tile):
    i = pl.program_id(0)
    cur = lax.rem(i, 2)

    def group_copy(b, g, d):
        src = y_ref.at[pl.ds(pl.multiple_of(d, GROUP_ROWS), GROUP_ROWS), :]
        dst = ybuf.at[b, pl.ds(pl.multiple_of(g * GROUP_ROWS, GROUP_ROWS), GROUP_ROWS), :]
        return pltpu.make_async_copy(src, dst, sem.at[b])

    def fetch(j, b, table_ref):
        lax.fori_loop(0, ng_ref[j], lambda g, c: (group_copy(b, g, table_ref[0, 0, g]).start(), c)[1], 0)

    @pl.when(i == 0)
    def _():
        ybuf[...] = jnp.zeros_like(ybuf)
        fetch(0, 0, gcur_ref)

    @pl.when(i + 1 < pl.num_programs(0))
    def _():
        fetch(i + 1, 1 - cur, gnext_ref)

    h1 = h_ref[...]
    hb = h1.astype(BF16)
    a = _dot(hb, sg_ref[...])
    u = _dot(hb, su_ref[...])
    shared = _dot((a * _sigmoid(a) * u).astype(BF16), sd_ref[...])

    slot = slot_ref[...]
    gw = gw_ref[...]
    g_hi = gw.astype(BF16).astype(F32)
    g_lo = gw - g_hi
    lax.fori_loop(0, ng_ref[i], lambda g, c: (group_copy(cur, 0, 0).wait(), c)[1], 0)
    routed = jnp.zeros((tt, D_MODEL), F32)
    for r in range(s_tile // SLOT_BLOCK):
        lane = lax.broadcasted_iota(I32, (tt, SLOT_BLOCK), 1) + r * SLOT_BLOCK
        p_hi = jnp.zeros((tt, SLOT_BLOCK), F32)
        p_lo = jnp.zeros((tt, SLOT_BLOCK), F32)
        for kk in range(TOP_K):
            hit = lane == slot[:, kk:kk + 1]
            p_hi = jnp.where(hit, g_hi[:, kk:kk + 1], p_hi)
            p_lo = jnp.where(hit, g_lo[:, kk:kk + 1], p_lo)
        yb = ybuf[cur, r * SLOT_BLOCK:(r + 1) * SLOT_BLOCK, :]
        routed = routed + _dot(p_hi.astype(BF16), yb) + _dot(p_lo.astype(BF16), yb)
    o_ref[...] = _layer_norm(DN_ALPHA * h1 + (routed + shared), g2_ref[...], b2_ref[...])


def _combine(n_groups, gdst, slot_t, gw_t, h1, y, wsg, wsu, wsd, g2, b2, tt):
    m = h1.shape[0]
    nt = m // tt
    s_tile = _tile_slots(tt)
    const = lambda i, ng: (0, 0)
    table = lambda f: pl.BlockSpec((1, 1, s_tile // GROUP_ROWS), f, memory_space=pltpu.SMEM)
    grid_spec = pltpu.PrefetchScalarGridSpec(
        num_scalar_prefetch=1,
        grid=(nt,),
        in_specs=[
            table(lambda i, ng: (i, 0, 0)),
            table(lambda i, ng: (jnp.minimum(i + 1, nt - 1), 0, 0)),
            pl.BlockSpec((tt, TOP_K), lambda i, ng: (i, 0)),
            pl.BlockSpec((tt, TOP_K), lambda i, ng: (i, 0)),
            pl.BlockSpec((tt, D_MODEL), lambda i, ng: (i, 0)),
            pl.BlockSpec(memory_space=pl.ANY),
            pl.BlockSpec((D_MODEL, D_EXPERT), const),
            pl.BlockSpec((D_MODEL, D_EXPERT), const),
            pl.BlockSpec((D_EXPERT, D_MODEL), const),
            pl.BlockSpec((1, D_MODEL), const),
            pl.BlockSpec((1, D_MODEL), const),
        ],
        out_specs=pl.BlockSpec((tt, D_MODEL), lambda i, ng: (i, 0)),
        scratch_shapes=[pltpu.VMEM((2, s_tile, D_MODEL), BF16), pltpu.SemaphoreType.DMA((2,))],
    )
    return pl.pallas_call(
        functools.partial(_combine_kernel, tt=tt, s_tile=s_tile),
        grid_spec=grid_spec,
        out_shape=jax.ShapeDtypeStruct((m, D_MODEL), F32),
        compiler_params=_params("arbitrary"),
        name="combine",
    )(n_groups, gdst, gdst, slot_t, gw_t, h1, y, wsg, wsu, wsd, g2, b2)


def _pick_tile(m, pref):
    t = min(pref, m)
    while m % t:
        t //= 2
    return t


def _gates_time_on_lanes(sg, chunk):
    m = sg.shape[0]
    return sg[:, :SUBLANES].reshape(m // chunk, chunk, SUBLANES).transpose(0, 2, 1)


def _forward(x, meta_tokens, ln_emb_g, ln_emb_b, w_in, hg_lb_logits, hg_norm_g, ml_conv_w, ml_conv_b,
             ml_ig_bias, ml_fg_bias, ml_norm_g, w_branch_a, w_branch_b, w_out, ln1_g, ln1_b,
             w_router, router_bias, w_exp_gate, w_exp_up, w_exp_down, w_sh_gate, w_sh_up, w_sh_down,
             ln2_g, ln2_b, *, chunk, tm_proj, tm_moe):
    nb, seq, d = x.shape
    m = nb * seq
    row = lambda a: a.reshape(1, -1).astype(F32)

    w = w_in[0]
    kw = HG_HEADS * HG_DK
    o_qa, o_fa, o_ia, o_ga = 0, kw, 2 * kw, 3 * kw
    o_qb = 4 * kw
    o_kb = o_qb + ML_HEADS * ML_DK
    o_vb = o_kb + ML_HEADS * ML_DK
    o_ob = o_vb + ML_HEADS * ML_DV
    o_ig = o_ob + ML_HEADS * ML_DV
    o_fg = o_ig + ML_HEADS
    o_ma = o_fg + ML_HEADS
    o_mb = o_ma + D_MODEL
    cols = lambda o, n: w[:, o:o + n]
    w_cat = jnp.concatenate([
        cols(o_qa, kw), cols(o_fa, kw), cols(o_ia, kw), cols(o_ga, kw),
        cols(o_qb, 2 * ML_HEADS * ML_DK), cols(o_vb, ML_HEADS * ML_DV), cols(o_ob, ML_HEADS * ML_DV),
        cols(o_ma, D_MODEL), cols(o_mb, D_MODEL)], axis=1).astype(BF16)
    w_small = jnp.pad(cols(o_ig, 2 * ML_HEADS), ((0, 0), (0, 128 - 2 * ML_HEADS))).astype(BF16)
    gate_bias = jnp.pad(jnp.concatenate([ml_ig_bias[0], ml_fg_bias[0]]).astype(F32), (0, 128 - 2 * ML_HEADS)).reshape(1, 128)
    lb = jax.nn.softmax(hg_lb_logits.astype(F32), axis=0)[0].reshape(1, -1)
    eg, eb = row(ln_emb_g), row(ln_emb_b)
    conv_w = ml_conv_w[0].astype(F32)
    conv_b = row(ml_conv_b[0])
    hgn, mln = row(hg_norm_g[0]), row(ml_norm_g[0])

    p_m, lf_m, sg_m = _inproj(meta_tokens.astype(F32), eg, eb, w_cat, w_small, lb, gate_bias, N_META)
    s_zero = jnp.zeros((HG_HEADS, HG_DK, HG_DK), F32)
    _, s0 = _hgrn(p_m, lf_m, hgn, s_zero, 1, N_META)
    sgt_m = _gates_time_on_lanes(sg_m, N_META)
    _, c0, m0, x0 = _mlstm(p_m, sg_m, sgt_m, conv_w, conv_b, mln,
                           jnp.zeros((ML_HEADS, ML_DK, ML_AUG), F32), jnp.zeros((SUBLANES, 128), F32),
                           jnp.zeros((SUBLANES, D_MODEL), F32), 1, N_META)

    x2d = x.reshape(m, d).astype(F32)
    proj, logf, sg = _inproj(x2d, eg, eb, w_cat, w_small, lb, gate_bias, tm_proj)
    ya, _ = _hgrn(proj, logf, hgn, s0[0], nb, chunk)
    sgt = _gates_time_on_lanes(sg, chunk)
    yb, _, _, _ = _mlstm(proj, sg, sgt, conv_w, conv_b, mln, c0[0], m0[0], x0[0], nb, chunk)

    wr = w_router[0].T.astype(F32)
    wr_hi, wr_lo = _split_bf16(wr)
    rbias = jnp.broadcast_to(router_bias[0].astype(F32).reshape(N_EXPERTS, 1), (N_EXPERTS, 128))
    tt = tm_moe
    h1, slot_k, gw, cnt = _merge(
        x2d, ya, yb, proj, w_branch_a[0].astype(BF16), w_branch_b[0].astype(BF16), w_out[0].astype(BF16),
        eg, eb, row(ln1_g[0]), row(ln1_b[0]), wr_hi, wr_lo, rbias, tt)

    nt = m // tt
    s_tile = _tile_slots(tt)
    cnt8 = (cnt[:, :, 0].astype(I32) + GROUP_ROWS - 1) // GROUP_ROWS * GROUP_ROWS
    seg_end = jnp.cumsum(cnt8, axis=1)
    seg_off = seg_end - cnt8
    tile_rows = seg_end[:, -1]
    run = jnp.cumsum(cnt8, axis=0) - cnt8
    tot8 = jnp.sum(cnt8, axis=0)
    padded = (tot8 + MOE_BLOCK - 1) // MOE_BLOCK * MOE_BLOCK
    pends = jnp.cumsum(padded)
    gshift = (pends - padded)[None, :] + run - seg_off
    g_rows = jnp.arange(s_tile // GROUP_ROWS, dtype=I32) * GROUP_ROWS
    e_of_g = jnp.sum((seg_end[:, None, :] <= g_rows[None, :, None]).astype(I32), axis=-1)
    shift_g = jnp.sum(jnp.where(e_of_g[..., None] == jnp.arange(N_EXPERTS, dtype=I32), gshift[:, None, :], 0), axis=-1)
    gdst = jnp.where(g_rows[None, :] < tile_rows[:, None], g_rows[None, :] + shift_g, 0).astype(I32)
    gdst = gdst.reshape(nt, 1, s_tile // GROUP_ROWS)
    n_groups = (tile_rows // GROUP_ROWS).astype(I32)
    n_blocks = -(-(m * TOP_K + nt * N_EXPERTS * (GROUP_ROWS - 1)) // MOE_BLOCK) + N_EXPERTS
    blk_start = jnp.arange(n_blocks, dtype=I32) * MOE_BLOCK
    blk_expert = jnp.minimum(jnp.sum((pends[None, :] <= blk_start[:, None]).astype(I32), axis=1), N_EXPERTS - 1)
    n_used = (pends[-1:] // MOE_BLOCK).astype(I32)

    xs = _dispatch(n_groups, gdst, slot_k, h1, n_blocks * MOE_BLOCK, tt)
    y = _experts(blk_expert, n_used, xs, w_exp_gate[0].astype(BF16), w_exp_up[0].astype(BF16),
                 w_exp_down[0].astype(BF16))
    out = _combine(n_groups, gdst, slot_k.T, gw.T, h1, y, w_sh_gate[0].astype(BF16), w_sh_up[0].astype(BF16),
                   w_sh_down[0].astype(BF16), row(ln2_g[0]), row(ln2_b[0]), tt)
    return out.reshape(nb, seq, d).astype(x.dtype)


def kernel(x, meta_tokens, ln_emb_g, ln_emb_b, w_in, hg_lb_logits, hg_norm_g, ml_conv_w, ml_conv_b, ml_ig_bias, ml_fg_bias, ml_norm_g, w_branch_a, w_branch_b, w_out, ln1_g, ln1_b, w_router, router_bias, w_exp_gate, w_exp_up, w_exp_down, w_sh_gate, w_sh_up, w_sh_down, ln2_g, ln2_b):
    m = x.shape[0] * x.shape[1]
    return _forward(x, meta_tokens, ln_emb_g, ln_emb_b, w_in, hg_lb_logits, hg_norm_g, ml_conv_w, ml_conv_b,
                    ml_ig_bias, ml_fg_bias, ml_norm_g, w_branch_a, w_branch_b, w_out, ln1_g, ln1_b,
                    w_router, router_bias, w_exp_gate, w_exp_up, w_exp_down, w_sh_gate, w_sh_up, w_sh_down,
                    ln2_g, ln2_b, chunk=_pick_tile(x.shape[1], 256), tm_proj=_pick_tile(m, 512),
                    tm_moe=_pick_tile(m, 256))
```

```python
import functools

import jax
import jax.numpy as jnp
from jax import lax
from jax.experimental import pallas as pl
from jax.experimental.pallas import tpu as pltpu

F32, BF16, I32 = jnp.float32, jnp.bfloat16, jnp.int32

D_MODEL = 1024
N_META = 16
HG_HEADS = 8
HG_DK = 128
ML_HEADS = 4
ML_DK = 128
ML_DV = 256
ML_AUG = ML_DV + 128
N_EXPERTS = 64
TOP_K = 8
N_GROUPS = 8
GROUP_SIZE = N_EXPERTS // N_GROUPS
TOPK_GROUPS = 4
D_EXPERT = 256
ROUTED_SCALE = 2.5
MOE_BLOCK = 1024
SLOT_BLOCK = 256
DN_ALPHA = 2.0 ** 0.25
EPS = 1e-5
EXP_CLAMP = 80.0
SUBLANES = 8
GROUP_ROWS = 16

P_QA, P_KA, P_IA, P_GA, P_QKB, P_VB, P_OB, P_MA, P_MB = range(9)
N_SLABS = 9

VMEM_LIMIT = 56 * 1024 * 1024


def _params(*sem):
    return pltpu.CompilerParams(dimension_semantics=sem, vmem_limit_bytes=VMEM_LIMIT)


def _sigmoid(x):
    return 1.0 / (1.0 + jnp.exp(-x))


def _log_sigmoid(x):
    return jnp.minimum(x, 0.0) - jnp.log(1.0 + jnp.exp(-jnp.abs(x)))


def _layer_norm(x, g, b):
    xc = x - jnp.mean(x, axis=-1, keepdims=True)
    var = jnp.mean(xc * xc, axis=-1, keepdims=True)
    return xc * lax.rsqrt(var + EPS) * g + b


def _dot(a, b):
    return jnp.dot(a, b, preferred_element_type=F32)


def _dot_nt(a, b):
    return lax.dot_general(a, b, (((1,), (1,)), ((), ())), preferred_element_type=F32)


def _dot_tn(a, b):
    return lax.dot_general(a, b, (((0,), (0,)), ((), ())), preferred_element_type=F32)


def _split_bf16(x):
    hi = x.astype(BF16)
    lo = (x - hi.astype(F32)).astype(BF16)
    return hi, lo


def _ones_where(cond):
    return jnp.where(cond, 1.0, 0.0).astype(BF16)


def _inproj_kernel(x_ref, g_ref, b_ref, w_ref, ws_ref, lb_ref, gb_ref, p_ref, lf_ref, sg_ref):
    hb = _layer_norm(x_ref[...], g_ref[...], b_ref[...]).astype(BF16)
    s = _dot(hb, ws_ref[...]) + gb_ref[...]
    lane = lax.broadcasted_iota(I32, s.shape, 1)
    sg_ref[...] = jnp.where(lane < ML_HEADS, s, _log_sigmoid(s))
    for n in range(N_SLABS):
        acc = _dot(hb, w_ref[:, n * D_MODEL:(n + 1) * D_MODEL])
        if n == P_KA:
            lb = lb_ref[...]
            f = lb + (1.0 - lb) * _sigmoid(acc)
            lf_ref[...] = jnp.log(f)
            acc = 1.0 - f
        elif n == P_GA:
            acc = acc * _sigmoid(acc)
        elif n >= P_OB:
            acc = _sigmoid(acc)
        p_ref[n] = acc.astype(BF16)


def _inproj(x2d, ln_g, ln_b, w_cat, w_small, lb, gate_bias, tm):
    m = x2d.shape[0]
    const = lambda i: (0, 0)
    resident = lambda shape: pl.BlockSpec(shape, const, pipeline_mode=pl.Buffered(1))
    return pl.pallas_call(
        _inproj_kernel,
        grid=(m // tm,),
        in_specs=[
            pl.BlockSpec((tm, D_MODEL), lambda i: (i, 0)),
            resident((1, D_MODEL)),
            resident((1, D_MODEL)),
            resident((D_MODEL, N_SLABS * D_MODEL)),
            resident((D_MODEL, 128)),
            resident((1, D_MODEL)),
            resident((1, 128)),
        ],
        out_specs=[
            pl.BlockSpec((N_SLABS, tm, D_MODEL), lambda i: (0, i, 0)),
            pl.BlockSpec((tm, D_MODEL), lambda i: (i, 0)),
            pl.BlockSpec((tm, 128), lambda i: (i, 0)),
        ],
        out_shape=[
            jax.ShapeDtypeStruct((N_SLABS, m, D_MODEL), BF16),
            jax.ShapeDtypeStruct((m, D_MODEL), F32),
            jax.ShapeDtypeStruct((m, 128), F32),
        ],
        compiler_params=_params("parallel"),
        name="inproj",
    )(x2d, ln_g, ln_b, w_cat, w_small, lb, gate_bias)


def _block_rows(b, block, pick):
    c, w = b.shape
    parts = [jnp.broadcast_to(b[j * block + pick:j * block + pick + 1, :], (block, w))
             for j in range(c // block)]
    return parts[0] if len(parts) == 1 else jnp.concatenate(parts, axis=0)


def _hgrn_kernel(q_ref, k_ref, v_ref, g_ref, lf_ref, ng_ref, s0_ref, y_ref, sfin_ref, s_scr, *, chunk):
    c = pl.program_id(1)

    @pl.when(c == 0)
    def _():
        s_scr[...] = s0_ref[...]

    cs = chunk
    row = lax.broadcasted_iota(I32, (cs, cs), 0)
    col = lax.broadcasted_iota(I32, (cs, cs), 1)
    tri = _ones_where(col <= row)
    lf_hi, lf_lo = _split_bf16(lf_ref[...])
    b = _dot(tri, lf_hi) + _dot(tri, lf_lo)
    q = q_ref[...].astype(F32)
    k = k_ref[...].astype(F32)
    v = v_ref[...]
    blast = b[cs - 1:cs, :]
    qg = (q * jnp.exp(b)).astype(BF16)
    kg = (k * jnp.exp(blast - b)).astype(BF16)
    dec = jnp.exp(blast)

    levels = []
    m = SUBLANES
    while 2 * m <= cs:
        w = jnp.exp(-jnp.abs(b - _block_rows(b, 2 * m, m - 1)))
        sh = (2 * m).bit_length() - 1
        mask = ((row >> sh) == (col >> sh)) & ((row & (2 * m - 1)) >= m) & ((col & (2 * m - 1)) < m)
        levels.append(((q * w).astype(BF16), (k * w).astype(BF16), mask))
        m *= 2
    e = jnp.clip(b - _block_rows(b, SUBLANES, SUBLANES // 2 - 1), -EXP_CLAMP, EXP_CLAMP)
    levels.append(((q * jnp.exp(e)).astype(BF16), (k * jnp.exp(-e)).astype(BF16),
                   ((row >> 3) == (col >> 3)) & (col <= row)))

    ng = ng_ref[...]
    for h in range(HG_HEADS):
        sl = slice(h * HG_DK, (h + 1) * HG_DK)
        st = s_scr[h]
        o = _dot_nt(qg[:, sl], st.astype(BF16))
        sc = jnp.zeros((cs, cs), F32)
        for lq, lk, mask in levels:
            sc = jnp.where(mask, _dot_nt(lq[:, sl], lk[:, sl]), sc)
        o = o + _dot(sc.astype(BF16), v[:, sl])
        s_scr[h] = dec[:, sl] * st + _dot_tn(v[:, sl], kg[:, sl])
        ms = jnp.mean(o * o, axis=-1, keepdims=True)
        y = o * lax.rsqrt(ms + EPS) * ng[:, sl] * g_ref[:, sl].astype(F32)
        y_ref[:, sl] = y.astype(BF16)

    @pl.when(c == pl.num_programs(1) - 1)
    def _():
        sfin_ref[...] = s_scr[...]


def _hgrn(proj, logf, norm_g, s0, nb, chunk):
    m = logf.shape[0]
    nc = m // nb // chunk
    slab = lambda s: pl.BlockSpec((None, chunk, D_MODEL), lambda b, c, s=s: (s, b * nc + c, 0))
    return pl.pallas_call(
        functools.partial(_hgrn_kernel, chunk=chunk),
        grid=(nb, nc),
        in_specs=[
            slab(P_QA), slab(P_KA), slab(P_IA), slab(P_GA),
            pl.BlockSpec((chunk, D_MODEL), lambda b, c: (b * nc + c, 0)),
            pl.BlockSpec((1, D_MODEL), lambda b, c: (0, 0)),
            pl.BlockSpec((HG_HEADS, HG_DK, HG_DK), lambda b, c: (0, 0, 0)),
        ],
        out_specs=[
            pl.BlockSpec((chunk, D_MODEL), lambda b, c: (b * nc + c, 0)),
            pl.BlockSpec((None, HG_HEADS, HG_DK, HG_DK), lambda b, c: (b, 0, 0, 0)),
        ],
        out_shape=[
            jax.ShapeDtypeStruct((m, D_MODEL), BF16),
            jax.ShapeDtypeStruct((nb, HG_HEADS, HG_DK, HG_DK), F32),
        ],
        scratch_shapes=[pltpu.VMEM((HG_HEADS, HG_DK, HG_DK), F32)],
        compiler_params=_params("parallel", "arbitrary"),
        name="hgrn",
    )(proj, proj, proj, proj, logf, norm_g, s0)


def _mlstm_kernel(qk_ref, v_ref, og_ref, sg_ref, sgt_ref, cw_ref, cb_ref, ng_ref, c0_ref, m0_ref, x0_ref,
                  y_ref, cfin_ref, mfin_ref, xfin_ref, c_scr, m_scr, x_scr, *, chunk):
    c = pl.program_id(1)

    @pl.when(c == 0)
    def _():
        c_scr[...] = c0_ref[...]
        m_scr[...] = m0_ref[...]
        x_scr[...] = x0_ref[...]

    cs = chunk
    x = qk_ref[...].astype(F32)
    prev = x_scr[...]
    sub = lax.broadcasted_iota(I32, (SUBLANES, D_MODEL), 0)
    cw = cw_ref[...]
    conv = cw[3:4, :] * x + cb_ref[...]
    for j in (1, 2, 3):
        xs = pltpu.roll(x, j, 0)
        head = jnp.where(sub < j, pltpu.roll(prev, j, 0), xs[:SUBLANES, :])
        xs = jnp.concatenate([head, xs[SUBLANES:, :]], axis=0)
        conv = conv + cw[3 - j:4 - j, :] * xs
    x_scr[...] = x[cs - SUBLANES:, :]
    qk = conv * _sigmoid(conv)
    q_all = (qk[:, :ML_HEADS * ML_DK] * (ML_DK ** -0.5)).astype(BF16)
    k_all = qk[:, ML_HEADS * ML_DK:]

    row = lax.broadcasted_iota(I32, (cs, cs), 0)
    col = lax.broadcasted_iota(I32, (cs, cs), 1)
    causal = col <= row
    tri = _ones_where(causal)
    sg = sg_ref[...]
    sgt = sgt_ref[...]
    sg_hi, sg_lo = _split_bf16(sg)
    bcol_all = _dot(tri, sg_hi) + _dot(tri, sg_lo)
    sgt_hi, sgt_lo = _split_bf16(sgt)
    brow_all = _dot_nt(sgt_hi, tri) + _dot_nt(sgt_lo, tri)
    lane128 = lax.broadcasted_iota(I32, (cs, 128), 1)
    ones_col = _ones_where(lane128 == 0)
    v = v_ref[...]
    ng = ng_ref[...]

    for h in range(ML_HEADS):
        b_col = bcol_all[:, ML_HEADS + h:ML_HEADS + h + 1]
        b_row = brow_all[ML_HEADS + h:ML_HEADS + h + 1, :]
        ig_col = sg[:, h:h + 1]
        ig_row = sgt[h:h + 1, :]
        m_prev = m_scr[h:h + 1, 0:1]
        q_h = q_all[:, h * ML_DK:(h + 1) * ML_DK]
        k_h = k_all[:, h * ML_DK:(h + 1) * ML_DK]
        v_aug = jnp.concatenate([v[:, h * ML_DV:(h + 1) * ML_DV], ones_col], axis=1)
        c_st = c_scr[h]

        log_intra = jnp.where(causal, b_col - b_row + ig_row, -jnp.inf)
        log_inter = b_col + m_prev
        m_t = jnp.maximum(log_inter, jnp.max(log_intra, axis=-1, keepdims=True))
        w_intra = jnp.exp(log_intra - m_t)
        w_inter = jnp.exp(log_inter - m_t)
        s = _dot_nt(q_h, k_h.astype(BF16)) * w_intra
        tot = w_inter * _dot(q_h, c_st.astype(BF16)) + _dot(s.astype(BF16), v_aug)
        num = tot[:, :ML_DV]
        den = tot[:, ML_DV:ML_DV + 1]
        hid = num / jnp.maximum(jnp.abs(den), jnp.exp(-m_t))

        b_last = b_col[cs - 1:cs, :]
        log_w = b_last - b_col + ig_col
        m_new = jnp.maximum(b_last + m_prev, jnp.max(log_w, axis=0, keepdims=True))
        w_s = jnp.exp(log_w - m_new)
        decay = jnp.exp(b_last + m_prev - m_new)
        c_scr[h] = decay * c_st + _dot_tn((k_h * w_s).astype(BF16), v_aug)
        m_scr[h:h + 1, :] = jnp.broadcast_to(m_new, (1, 128))

        hc = hid - jnp.mean(hid, axis=-1, keepdims=True)
        var = jnp.mean(hc * hc, axis=-1, keepdims=True)
        sl = slice(h * ML_DV, (h + 1) * ML_DV)
        y = hc * lax.rsqrt(var + EPS) * ng[:, sl] * og_ref[:, sl].astype(F32)
        y_ref[:, sl] = y.astype(BF16)

    @pl.when(c == pl.num_programs(1) - 1)
    def _():
        cfin_ref[...] = c_scr[...]
        mfin_ref[...] = m_scr[...]
        xfin_ref[...] = x_scr[...]


def _mlstm(proj, sg, sgt, conv_w, conv_b, norm_g, c0, m0, x0, nb, chunk):
    m = sg.shape[0]
    nc = m // nb // chunk
    slab = lambda s: pl.BlockSpec((None, chunk, D_MODEL), lambda b, c, s=s: (s, b * nc + c, 0))
    const2 = lambda b, c: (0, 0)
    return pl.pallas_call(
        functools.partial(_mlstm_kernel, chunk=chunk),
        grid=(nb, nc),
        in_specs=[
            slab(P_QKB), slab(P_VB), slab(P_OB),
            pl.BlockSpec((chunk, 128), lambda b, c: (b * nc + c, 0)),
            pl.BlockSpec((None, SUBLANES, chunk), lambda b, c: (b * nc + c, 0, 0)),
            pl.BlockSpec((4, D_MODEL), const2),
            pl.BlockSpec((1, D_MODEL), const2),
            pl.BlockSpec((1, D_MODEL), const2),
            pl.BlockSpec((ML_HEADS, ML_DK, ML_AUG), lambda b, c: (0, 0, 0)),
            pl.BlockSpec((SUBLANES, 128), const2),
            pl.BlockSpec((SUBLANES, D_MODEL), const2),
        ],
        out_specs=[
            pl.BlockSpec((chunk, D_MODEL), lambda b, c: (b * nc + c, 0)),
            pl.BlockSpec((None, ML_HEADS, ML_DK, ML_AUG), lambda b, c: (b, 0, 0, 0)),
            pl.BlockSpec((None, SUBLANES, 128), lambda b, c: (b, 0, 0)),
            pl.BlockSpec((None, SUBLANES, D_MODEL), lambda b, c: (b, 0, 0)),
        ],
        out_shape=[
            jax.ShapeDtypeStruct((m, D_MODEL), BF16),
            jax.ShapeDtypeStruct((nb, ML_HEADS, ML_DK, ML_AUG), F32),
            jax.ShapeDtypeStruct((nb, SUBLANES, 128), F32),
            jax.ShapeDtypeStruct((nb, SUBLANES, D_MODEL), F32),
        ],
        scratch_shapes=[
            pltpu.VMEM((ML_HEADS, ML_DK, ML_AUG), F32),
            pltpu.VMEM((SUBLANES, 128), F32),
            pltpu.VMEM((SUBLANES, D_MODEL), F32),
        ],
        compiler_params=_params("parallel", "arbitrary"),
        name="mlstm",
    )(proj, proj, proj, sg, sgt, conv_w, conv_b, norm_g, c0, m0, x0)


def _merge_kernel(x_ref, ya_ref, yb_ref, ma_ref, mb_ref, wa_ref, wb_ref, wo_ref, eg_ref, eb_ref, g1_ref, b1_ref,
                  wrh_ref, wrl_ref, rb_ref,
                  h1_ref, slot_ref, gw_ref, cnt_ref, *, tm):
    h0 = _layer_norm(x_ref[...], eg_ref[...], eb_ref[...])
    merged = (ma_ref[...].astype(F32) * _dot(ya_ref[...], wa_ref[...])
              + mb_ref[...].astype(F32) * _dot(yb_ref[...], wb_ref[...]))
    mix = _dot(merged.astype(BF16), wo_ref[...])
    h1 = _layer_norm(DN_ALPHA * h0 + mix, g1_ref[...], b1_ref[...])
    h1_ref[...] = h1

    h_hi, h_lo = _split_bf16(h1)
    logits = _dot_nt(wrh_ref[...], h_hi) + _dot_nt(wrh_ref[...], h_lo) + _dot_nt(wrl_ref[...], h_hi)
    scores = _sigmoid(logits)
    biased = scores + rb_ref[:, 0:1]
    neg_inf = -jnp.inf

    g3 = biased.reshape(N_GROUPS, GROUP_SIZE, tm)
    sub3 = lax.broadcasted_iota(I32, g3.shape, 1)
    top1 = jnp.max(g3, axis=1, keepdims=True)
    first = jnp.min(jnp.where(g3 == top1, sub3, GROUP_SIZE), axis=1, keepdims=True)
    top2 = jnp.max(jnp.where(sub3 == first, neg_inf, g3), axis=1, keepdims=True)
    gs = (top1 + top2).reshape(N_GROUPS, tm)
    gi = lax.broadcasted_iota(I32, gs.shape, 0)
    grank = jnp.zeros(gs.shape, F32)
    for j in range(N_GROUPS):
        r = gs[j:j + 1, :]
        grank = grank + jnp.where((r > gs) | ((r == gs) & (gi > j)), 1.0, 0.0)
    gsel = grank < float(TOPK_GROUPS)
    emask = jnp.broadcast_to(gsel.reshape(N_GROUPS, 1, tm), (N_GROUPS, GROUP_SIZE, tm)).reshape(N_EXPERTS, tm)
    masked = jnp.where(emask, biased, neg_inf)

    ei = lax.broadcasted_iota(I32, masked.shape, 0)
    work = masked
    rank = jnp.full(masked.shape, float(N_EXPERTS), F32)
    for kk in range(TOP_K):
        top = jnp.max(work, axis=0, keepdims=True)
        first = jnp.min(jnp.where(work == top, ei, N_EXPERTS), axis=0, keepdims=True)
        hit = ei == first
        rank = jnp.where(hit, float(kk), rank)
        work = jnp.where(hit, neg_inf, work)
    sel = rank < float(TOP_K)
    sel_w = jnp.where(sel, scores, 0.0)
    gwd = sel_w / jnp.sum(sel_w, axis=0, keepdims=True) * ROUTED_SCALE

    tr = lax.broadcasted_iota(I32, (tm, tm), 0)
    tc = lax.broadcasted_iota(I32, (tm, tm), 1)
    sel_b = _ones_where(sel)
    rloc = _dot(sel_b, _ones_where(tr < tc))
    cnt = _dot(sel_b, jnp.ones((tm, 128), BF16))
    cnt_g = jnp.floor((cnt + (GROUP_ROWS - 1.0)) * (1.0 / GROUP_ROWS)) * GROUP_ROWS
    er = lax.broadcasted_iota(I32, (N_EXPERTS, N_EXPERTS), 0)
    ec = lax.broadcasted_iota(I32, (N_EXPERTS, N_EXPERTS), 1)
    seg_start = _dot(_ones_where(ec < er), cnt_g.astype(BF16))
    slot_e = seg_start[:, 0:1] + rloc
    cnt_ref[...] = cnt

    s_rows, w_rows = [], []
    for kk in range(TOP_K):
        pick = sel & (rank == float(kk))
        s_rows.append(jnp.sum(jnp.where(pick, slot_e, 0.0), axis=0, keepdims=True))
        w_rows.append(jnp.sum(jnp.where(pick, gwd, 0.0), axis=0, keepdims=True))
    slot_ref[...] = jnp.concatenate(s_rows, axis=0).astype(I32)
    gw_ref[...] = jnp.concatenate(w_rows, axis=0)


def _merge(x2d, ya, yb, proj, w_a, w_b, w_o, eg, eb, g1, b1, wr_hi, wr_lo, rbias, tm):
    m = x2d.shape[0]
    tile = lambda i: (i, 0)
    const = lambda i: (0, 0)
    slab = lambda s: pl.BlockSpec((None, tm, D_MODEL), lambda i, s=s: (s, i, 0))
    wspec = pl.BlockSpec((D_MODEL, D_MODEL), const)
    vspec = pl.BlockSpec((1, D_MODEL), const)
    lane_tile = pl.BlockSpec((TOP_K, tm), lambda i: (0, i))
    return pl.pallas_call(
        functools.partial(_merge_kernel, tm=tm),
        grid=(m // tm,),
        in_specs=[
            pl.BlockSpec((tm, D_MODEL), tile), pl.BlockSpec((tm, D_MODEL), tile), pl.BlockSpec((tm, D_MODEL), tile),
            slab(P_MA), slab(P_MB), wspec, wspec, wspec, vspec, vspec, vspec, vspec,
            pl.BlockSpec((N_EXPERTS, D_MODEL), const), pl.BlockSpec((N_EXPERTS, D_MODEL), const),
            pl.BlockSpec((N_EXPERTS, 128), const),
        ],
        out_specs=[
            pl.BlockSpec((tm, D_MODEL), tile), lane_tile, lane_tile,
            pl.BlockSpec((None, N_EXPERTS, 128), lambda i: (i, 0, 0)),
        ],
        out_shape=[
            jax.ShapeDtypeStruct((m, D_MODEL), F32),
            jax.ShapeDtypeStruct((TOP_K, m), I32),
            jax.ShapeDtypeStruct((TOP_K, m), F32),
            jax.ShapeDtypeStruct((m // tm, N_EXPERTS, 128), F32),
        ],
        compiler_params=_params("parallel"),
        name="merge_router",
    )(x2d, ya, yb, proj, proj, w_a, w_b, w_o, eg, eb, g1, b1, wr_hi, wr_lo, rbias)


def _tile_slots(tt):
    return -(-(TOP_K * tt + N_EXPERTS * (GROUP_ROWS - 1)) // SLOT_BLOCK) * SLOT_BLOCK


def _dispatch_kernel(ng_ref, gdst_ref, slot_ref, h_ref, xs_ref, buf, sem, *, tt, s_tile):
    i = pl.program_id(0)
    cur = lax.rem(i, 2)

    def group_copy(b, g, d):
        src = buf.at[b, pl.ds(pl.multiple_of(g * GROUP_ROWS, GROUP_ROWS), GROUP_ROWS), :]
        dst = xs_ref.at[pl.ds(pl.multiple_of(d, GROUP_ROWS), GROUP_ROWS), :]
        return pltpu.make_async_copy(src, dst, sem.at[b])

    def wait_tile(j, b):
        lax.fori_loop(0, ng_ref[j], lambda g, c: (group_copy(b, 0, 0).wait(), c)[1], 0)

    @pl.when(i >= 2)
    def _():
        wait_tile(i - 2, cur)

    hb = h_ref[...].astype(BF16)
    sl = slot_ref[...]
    for r in range(s_tile // SLOT_BLOCK):
        s_iota = lax.broadcasted_iota(I32, (SLOT_BLOCK, tt), 0) + r * SLOT_BLOCK
        p = jnp.zeros((SLOT_BLOCK, tt), F32)
        for kk in range(TOP_K):
            p = jnp.where(s_iota == sl[kk:kk + 1, :], 1.0, p)
        buf[cur, r * SLOT_BLOCK:(r + 1) * SLOT_BLOCK, :] = _dot(p.astype(BF16), hb).astype(BF16)

    lax.fori_loop(0, ng_ref[i], lambda g, c: (group_copy(cur, g, gdst_ref[0, 0, g]).start(), c)[1], 0)

    @pl.when(i == pl.num_programs(0) - 1)
    def _():
        @pl.when(i >= 1)
        def _():
            wait_tile(i - 1, 1 - cur)
        wait_tile(i, cur)


def _dispatch(n_groups, gdst, slot_k, h1, n_slots, tt):
    m = h1.shape[0]
    s_tile = _tile_slots(tt)
    grid_spec = pltpu.PrefetchScalarGridSpec(
        num_scalar_prefetch=1,
        grid=(m // tt,),
        in_specs=[
            pl.BlockSpec((1, 1, s_tile // GROUP_ROWS), lambda i, ng: (i, 0, 0), memory_space=pltpu.SMEM),
            pl.BlockSpec((TOP_K, tt), lambda i, ng: (0, i)),
            pl.BlockSpec((tt, D_MODEL), lambda i, ng: (i, 0)),
        ],
        out_specs=pl.BlockSpec(memory_space=pl.ANY),
        scratch_shapes=[pltpu.VMEM((2, s_tile, D_MODEL), BF16), pltpu.SemaphoreType.DMA((2,))],
    )
    return pl.pallas_call(
        functools.partial(_dispatch_kernel, tt=tt, s_tile=s_tile),
        grid_spec=grid_spec,
        out_shape=jax.ShapeDtypeStruct((n_slots, D_MODEL), BF16),
        compiler_params=_params("arbitrary"),
        name="dispatch",
    )(n_groups, gdst, slot_k, h1)


def _experts_kernel(be_ref, nu_ref, x_ref, wg_ref, wu_ref, wd_ref, y_ref):
    del be_ref
    i = pl.program_id(0)

    @pl.when(i < nu_ref[0])
    def _():
        xb = x_ref[...]
        a = _dot(xb, wg_ref[...])
        u = _dot(xb, wu_ref[...])
        y_ref[...] = _dot((a * _sigmoid(a) * u).astype(BF16), wd_ref[...]).astype(BF16)


def _experts(blk_expert, n_used, xs, wg, wu, wd):
    n_slots = xs.shape[0]
    n_blocks = n_slots // MOE_BLOCK
    blk = lambda i, be, nu: (jnp.minimum(i, nu[0] - 1), 0)
    wsel = lambda i, be, nu: (be[jnp.minimum(i, nu[0] - 1)], 0, 0)
    grid_spec = pltpu.PrefetchScalarGridSpec(
        num_scalar_prefetch=2,
        grid=(n_blocks,),
        in_specs=[
            pl.BlockSpec((MOE_BLOCK, D_MODEL), blk),
            pl.BlockSpec((None, D_MODEL, D_EXPERT), wsel),
            pl.BlockSpec((None, D_MODEL, D_EXPERT), wsel),
            pl.BlockSpec((None, D_EXPERT, D_MODEL), wsel),
        ],
        out_specs=pl.BlockSpec((MOE_BLOCK, D_MODEL), blk),
    )
    return pl.pallas_call(
        _experts_kernel,
        grid_spec=grid_spec,
        out_shape=jax.ShapeDtypeStruct((n_slots, D_MODEL), BF16),
        compiler_params=_params("arbitrary"),
        name="experts",
    )(blk_expert, n_used, xs, wg, wu, wd)


def _combine_kernel(ng_ref, gcur_ref, gnext_ref, slot_ref, gw_ref, h_ref, y_ref, sg_ref, su_ref, sd_ref, g2_ref, b2_ref,
                    o_ref, ybuf, sem, *, tt, s_tile):
    i = pl.program_id(0)
    cur = lax.rem(i, 2)

    def group_copy(b, g, d):
        src = y_ref.at[pl.ds(pl.multiple_of(d, GROUP_ROWS), GROUP_ROWS), :]
        dst = ybuf.at[b, pl.ds(pl.multiple_of(g * GROUP_ROWS, GROUP_ROWS), GROUP_ROWS), :]
        return pltpu.make_async_copy(src, dst, sem.at[b])

    def fetch(j, b, table_ref):
        lax.fori_loop(0, ng_ref[j], lambda g, c: (group_copy(b, g, table_ref[0, 0, g]).start(), c)[1], 0)

    @pl.when(i == 0)
    def _():
        ybuf[...] = jnp.zeros_like(ybuf)
        fetch(0, 0, gcur_ref)

    @pl.when(i + 1 < pl.num_programs(0))
    def _():
        fetch(i + 1, 1 - cur, gnext_ref)

    h1 = h_ref[...]
    hb = h1.astype(BF16)
    a = _dot(hb, sg_ref[...])
    u = _dot(hb, su_ref[...])
    shared = _dot((a * _sigmoid(a) * u).astype(BF16), sd_ref[...])

    slot = slot_ref[...]
    gw = gw_ref[...]
    lax.fori_loop(0, ng_ref[i], lambda g, c: (group_copy(cur, 0, 0).wait(), c)[1], 0)
    routed = jnp.zeros((tt, D_MODEL), F32)
    for r in range(s_tile // SLOT_BLOCK):
        lane = lax.broadcasted_iota(I32, (tt, SLOT_BLOCK), 1) + r * SLOT_BLOCK
        p = jnp.zeros((tt, SLOT_BLOCK), F32)
        for kk in range(TOP_K):
            p = jnp.where(lane == slot[:, kk:kk + 1], gw[:, kk:kk + 1], p)
        routed = routed + _dot(p.astype(BF16), ybuf[cur, r * SLOT_BLOCK:(r + 1) * SLOT_BLOCK, :])
    o_ref[...] = _layer_norm(DN_ALPHA * h1 + (routed + shared), g2_ref[...], b2_ref[...])


def _combine(n_groups, gdst, slot_t, gw_t, h1, y, wsg, wsu, wsd, g2, b2, tt):
    m = h1.shape[0]
    nt = m // tt
    s_tile = _tile_slots(tt)
    const = lambda i, ng: (0, 0)
    table = lambda f: pl.BlockSpec((1, 1, s_tile // GROUP_ROWS), f, memory_space=pltpu.SMEM)
    grid_spec = pltpu.PrefetchScalarGridSpec(
        num_scalar_prefetch=1,
        grid=(nt,),
        in_specs=[
            table(lambda i, ng: (i, 0, 0)),
            table(lambda i, ng: (jnp.minimum(i + 1, nt - 1), 0, 0)),
            pl.BlockSpec((tt, TOP_K), lambda i, ng: (i, 0)),
            pl.BlockSpec((tt, TOP_K), lambda i, ng: (i, 0)),
            pl.BlockSpec((tt, D_MODEL), lambda i, ng: (i, 0)),
            pl.BlockSpec(memory_space=pl.ANY),
            pl.BlockSpec((D_MODEL, D_EXPERT), const),
            pl.BlockSpec((D_MODEL, D_EXPERT), const),
            pl.BlockSpec((D_EXPERT, D_MODEL), const),
            pl.BlockSpec((1, D_MODEL), const),
            pl.BlockSpec((1, D_MODEL), const),
        ],
        out_specs=pl.BlockSpec((tt, D_MODEL), lambda i, ng: (i, 0)),
        scratch_shapes=[pltpu.VMEM((2, s_tile, D_MODEL), BF16), pltpu.SemaphoreType.DMA((2,))],
    )
    return pl.pallas_call(
        functools.partial(_combine_kernel, tt=tt, s_tile=s_tile),
        grid_spec=grid_spec,
        out_shape=jax.ShapeDtypeStruct((m, D_MODEL), F32),
        compiler_params=_params("arbitrary"),
        name="combine",
    )(n_groups, gdst, gdst, slot_t, gw_t, h1, y, wsg, wsu, wsd, g2, b2)


def _pick_tile(m, pref):
    t = min(pref, m)
    while m % t:
        t //= 2
    return t


def _gates_time_on_lanes(sg, chunk):
    m = sg.shape[0]
    return sg[:, :SUBLANES].reshape(m // chunk, chunk, SUBLANES).transpose(0, 2, 1)


def _forward(x, meta_tokens, ln_emb_g, ln_emb_b, w_in, hg_lb_logits, hg_norm_g, ml_conv_w, ml_conv_b,
             ml_ig_bias, ml_fg_bias, ml_norm_g, w_branch_a, w_branch_b, w_out, ln1_g, ln1_b,
             w_router, router_bias, w_exp_gate, w_exp_up, w_exp_down, w_sh_gate, w_sh_up, w_sh_down,
             ln2_g, ln2_b, *, chunk, tm_proj, tm_moe):
    nb, seq, d = x.shape
    m = nb * seq
    row = lambda a: a.reshape(1, -1).astype(F32)

    w = w_in[0]
    kw = HG_HEADS * HG_DK
    o_qa, o_fa, o_ia, o_ga = 0, kw, 2 * kw, 3 * kw
    o_qb = 4 * kw
    o_kb = o_qb + ML_HEADS * ML_DK
    o_vb = o_kb + ML_HEADS * ML_DK
    o_ob = o_vb + ML_HEADS * ML_DV
    o_ig = o_ob + ML_HEADS * ML_DV
    o_fg = o_ig + ML_HEADS
    o_ma = o_fg + ML_HEADS
    o_mb = o_ma + D_MODEL
    cols = lambda o, n: w[:, o:o + n]
    w_cat = jnp.concatenate([
        cols(o_qa, kw), cols(o_fa, kw), cols(o_ia, kw), cols(o_ga, kw),
        cols(o_qb, 2 * ML_HEADS * ML_DK), cols(o_vb, ML_HEADS * ML_DV), cols(o_ob, ML_HEADS * ML_DV),
        cols(o_ma, D_MODEL), cols(o_mb, D_MODEL)], axis=1).astype(BF16)
    w_small = jnp.pad(cols(o_ig, 2 * ML_HEADS), ((0, 0), (0, 128 - 2 * ML_HEADS))).astype(BF16)
    gate_bias = jnp.pad(jnp.concatenate([ml_ig_bias[0], ml_fg_bias[0]]).astype(F32), (0, 128 - 2 * ML_HEADS)).reshape(1, 128)
    lb = jax.nn.softmax(hg_lb_logits.astype(F32), axis=0)[0].reshape(1, -1)
    eg, eb = row(ln_emb_g), row(ln_emb_b)
    conv_w = ml_conv_w[0].astype(F32)
    conv_b = row(ml_conv_b[0])
    hgn, mln = row(hg_norm_g[0]), row(ml_norm_g[0])

    p_m, lf_m, sg_m = _inproj(meta_tokens.astype(F32), eg, eb, w_cat, w_small, lb, gate_bias, N_META)
    s_zero = jnp.zeros((HG_HEADS, HG_DK, HG_DK), F32)
    _, s0 = _hgrn(p_m, lf_m, hgn, s_zero, 1, N_META)
    sgt_m = _gates_time_on_lanes(sg_m, N_META)
    _, c0, m0, x0 = _mlstm(p_m, sg_m, sgt_m, conv_w, conv_b, mln,
                           jnp.zeros((ML_HEADS, ML_DK, ML_AUG), F32), jnp.zeros((SUBLANES, 128), F32),
                           jnp.zeros((SUBLANES, D_MODEL), F32), 1, N_META)

    x2d = x.reshape(m, d).astype(F32)
    proj, logf, sg = _inproj(x2d, eg, eb, w_cat, w_small, lb, gate_bias, tm_proj)
    ya, _ = _hgrn(proj, logf, hgn, s0[0], nb, chunk)
    sgt = _gates_time_on_lanes(sg, chunk)
    yb, _, _, _ = _mlstm(proj, sg, sgt, conv_w, conv_b, mln, c0[0], m0[0], x0[0], nb, chunk)

    wr = w_router[0].T.astype(F32)
    wr_hi, wr_lo = _split_bf16(wr)
    rbias = jnp.broadcast_to(router_bias[0].astype(F32).reshape(N_EXPERTS, 1), (N_EXPERTS, 128))
    tt = tm_moe
    h1, slot_k, gw, cnt = _merge(
        x2d, ya, yb, proj, w_branch_a[0].astype(BF16), w_branch_b[0].astype(BF16), w_out[0].astype(BF16),
        eg, eb, row(ln1_g[0]), row(ln1_b[0]), wr_hi, wr_lo, rbias, tt)

    nt = m // tt
    s_tile = _tile_slots(tt)
    cnt8 = (cnt[:, :, 0].astype(I32) + GROUP_ROWS - 1) // GROUP_ROWS * GROUP_ROWS
    seg_end = jnp.cumsum(cnt8, axis=1)
    seg_off = seg_end - cnt8
    tile_rows = seg_end[:, -1]
    run = jnp.cumsum(cnt8, axis=0) - cnt8
    tot8 = jnp.sum(cnt8, axis=0)
    padded = (tot8 + MOE_BLOCK - 1) // MOE_BLOCK * MOE_BLOCK
    pends = jnp.cumsum(padded)
    gshift = (pends - padded)[None, :] + run - seg_off
    g_rows = jnp.arange(s_tile // GROUP_ROWS, dtype=I32) * GROUP_ROWS
    e_of_g = jnp.sum((seg_end[:, None, :] <= g_rows[None, :, None]).astype(I32), axis=-1)
    shift_g = jnp.sum(jnp.where(e_of_g[..., None] == jnp.arange(N_EXPERTS, dtype=I32), gshift[:, None, :], 0), axis=-1)
    gdst = jnp.where(g_rows[None, :] < tile_rows[:, None], g_rows[None, :] + shift_g, 0).astype(I32)
    gdst = gdst.reshape(nt, 1, s_tile // GROUP_ROWS)
    n_groups = (tile_rows // GROUP_ROWS).astype(I32)
    n_blocks = -(-(m * TOP_K + nt * N_EXPERTS * (GROUP_ROWS - 1)) // MOE_BLOCK) + N_EXPERTS
    blk_start = jnp.arange(n_blocks, dtype=I32) * MOE_BLOCK
    blk_expert = jnp.minimum(jnp.sum((pends[None, :] <= blk_start[:, None]).astype(I32), axis=1), N_EXPERTS - 1)
    n_used = (pends[-1:] // MOE_BLOCK).astype(I32)

    xs = _dispatch(n_groups, gdst, slot_k, h1, n_blocks * MOE_BLOCK, tt)
    y = _experts(blk_expert, n_used, xs, w_exp_gate[0].astype(BF16), w_exp_up[0].astype(BF16),
                 w_exp_down[0].astype(BF16))
    out = _combine(n_groups, gdst, slot_k.T, gw.T, h1, y, w_sh_gate[0].astype(BF16), w_sh_up[0].astype(BF16),
                   w_sh_down[0].astype(BF16), row(ln2_g[0]), row(ln2_b[0]), tt)
    return out.reshape(nb, seq, d).astype(x.dtype)


def kernel(x, meta_tokens, ln_emb_g, ln_emb_b, w_in, hg_lb_logits, hg_norm_g, ml_conv_w, ml_conv_b, ml_ig_bias, ml_fg_bias, ml_norm_g, w_branch_a, w_branch_b, w_out, ln1_g, ln1_b, w_router, router_bias, w_exp_gate, w_exp_up, w_exp_down, w_sh_gate, w_sh_up, w_sh_down, ln2_g, ln2_b):
    m = x.shape[0] * x.shape[1]
    return _forward(x, meta_tokens, ln_emb_g, ln_emb_b, w_in, hg_lb_logits, hg_norm_g, ml_conv_w, ml_conv_b,
                    ml_ig_bias, ml_fg_bias, ml_norm_g, w_branch_a, w_branch_b, w_out, ln1_g, ln1_b,
                    w_router, router_bias, w_exp_gate, w_exp_up, w_exp_down, w_sh_gate, w_sh_up, w_sh_down,
                    ln2_g, ln2_b, chunk=_pick_tile(x.shape[1], 256), tm_proj=_pick_tile(m, 512),
                    tm_moe=_pick_tile(m, 256))
```

```python
import functools

import jax
import jax.numpy as jnp
from jax import lax
from jax.experimental import pallas as pl
from jax.experimental.pallas import tpu as pltpu

F32, BF16, I32 = jnp.float32, jnp.bfloat16, jnp.int32

D_MODEL = 1024
N_META = 16
HG_HEADS = 8
HG_DK = 128
ML_HEADS = 4
ML_DK = 128
ML_DV = 256
ML_AUG = ML_DV + 128
N_EXPERTS = 64
TOP_K = 8
N_GROUPS = 8
GROUP_SIZE = N_EXPERTS // N_GROUPS
TOPK_GROUPS = 4
D_EXPERT = 256
ROUTED_SCALE = 2.5
MOE_BLOCK = 1024
SLOT_BLOCK = 256
DN_ALPHA = 2.0 ** 0.25
EPS = 1e-5
LOG2E = 1.4426950408889634
EXP2_CLAMP = 115.0
SUBLANES = 8
GROUP_ROWS = 16

P_QA, P_KA, P_IA, P_GA, P_QKB, P_VB, P_OB, P_MA, P_MB = range(9)
N_SLABS = 9

VMEM_LIMIT = 56 * 1024 * 1024


def _params(*sem):
    return pltpu.CompilerParams(dimension_semantics=sem, vmem_limit_bytes=VMEM_LIMIT)


def _sigmoid(x):
    return 1.0 / (1.0 + jnp.exp(-x))


def _log_sigmoid(x):
    return jnp.minimum(x, 0.0) - jnp.log(1.0 + jnp.exp(-jnp.abs(x)))


def _layer_norm(x, g, b):
    xc = x - jnp.mean(x, axis=-1, keepdims=True)
    var = jnp.mean(xc * xc, axis=-1, keepdims=True)
    return xc * lax.rsqrt(var + EPS) * g + b


def _dot(a, b):
    return jnp.dot(a, b, preferred_element_type=F32)


def _dot_nt(a, b):
    return lax.dot_general(a, b, (((1,), (1,)), ((), ())), preferred_element_type=F32)


def _dot_tn(a, b):
    return lax.dot_general(a, b, (((0,), (0,)), ((), ())), preferred_element_type=F32)


def _split_bf16(x):
    hi = x.astype(BF16)
    lo = (x - hi.astype(F32)).astype(BF16)
    return hi, lo


def _neg_abs(x):
    return lax.bitcast_convert_type(lax.bitcast_convert_type(x, I32) | jnp.int32(-2 ** 31), F32)


def _ones_where(cond):
    return jnp.where(cond, 1.0, 0.0).astype(BF16)


def _inproj_kernel(x_ref, g_ref, b_ref, w_ref, ws_ref, lb_ref, gb_ref, p_ref, lf_ref, sg_ref):
    hb = _layer_norm(x_ref[...], g_ref[...], b_ref[...]).astype(BF16)
    s = _dot(hb, ws_ref[...]) + gb_ref[...]
    lane = lax.broadcasted_iota(I32, s.shape, 1)
    sg_ref[...] = jnp.where(lane < ML_HEADS, s, _log_sigmoid(s))
    for n in range(N_SLABS):
        acc = _dot(hb, w_ref[:, n * D_MODEL:(n + 1) * D_MODEL])
        if n == P_KA:
            lb = lb_ref[...]
            f = lb + (1.0 - lb) * _sigmoid(acc)
            lf_ref[...] = jnp.log(f)
            acc = 1.0 - f
        elif n == P_GA:
            acc = acc * _sigmoid(acc)
        elif n >= P_OB:
            acc = _sigmoid(acc)
        p_ref[n] = acc.astype(BF16)


def _inproj(x2d, ln_g, ln_b, w_cat, w_small, lb, gate_bias, tm):
    m = x2d.shape[0]
    const = lambda i: (0, 0)
    resident = lambda shape: pl.BlockSpec(shape, const, pipeline_mode=pl.Buffered(1))
    return pl.pallas_call(
        _inproj_kernel,
        grid=(m // tm,),
        in_specs=[
            pl.BlockSpec((tm, D_MODEL), lambda i: (i, 0)),
            resident((1, D_MODEL)),
            resident((1, D_MODEL)),
            resident((D_MODEL, N_SLABS * D_MODEL)),
            resident((D_MODEL, 128)),
            resident((1, D_MODEL)),
            resident((1, 128)),
        ],
        out_specs=[
            pl.BlockSpec((N_SLABS, tm, D_MODEL), lambda i: (0, i, 0)),
            pl.BlockSpec((tm, D_MODEL), lambda i: (i, 0)),
            pl.BlockSpec((tm, 128), lambda i: (i, 0)),
        ],
        out_shape=[
            jax.ShapeDtypeStruct((N_SLABS, m, D_MODEL), BF16),
            jax.ShapeDtypeStruct((m, D_MODEL), F32),
            jax.ShapeDtypeStruct((m, 128), F32),
        ],
        compiler_params=_params("parallel"),
        name="inproj",
    )(x2d, ln_g, ln_b, w_cat, w_small, lb, gate_bias)


def _block_rows(b, block, pick):
    c, w = b.shape
    parts = [jnp.broadcast_to(b[j * block + pick:j * block + pick + 1, :], (block, w))
             for j in range(c // block)]
    return parts[0] if len(parts) == 1 else jnp.concatenate(parts, axis=0)


def _hgrn_body(q_ref, k_ref, v_ref, g_ref, lf_ref, ng_ref, y_ref, s_scr, cs):
    row = lax.broadcasted_iota(I32, (cs, cs), 0)
    col = lax.broadcasted_iota(I32, (cs, cs), 1)
    tri = _ones_where(col <= row)
    lf_hi, lf_lo = _split_bf16(lf_ref[...])
    b = (_dot(tri, lf_hi) + _dot(tri, lf_lo)) * LOG2E
    q = q_ref[...]
    k = k_ref[...]
    v = v_ref[...]
    blast = b[cs - 1:cs, :]
    qg = q * jnp.exp2(b).astype(BF16)
    kg = k * jnp.exp2(blast - b).astype(BF16)
    dec = jnp.exp2(blast)

    levels = []
    m = SUBLANES
    while 2 * m <= cs:
        w = jnp.exp2(_neg_abs(b - _block_rows(b, 2 * m, m - 1))).astype(BF16)
        sh = (2 * m).bit_length() - 1
        mask = ((row >> sh) == (col >> sh)) & ((row & (2 * m - 1)) >= m) & ((col & (2 * m - 1)) < m)
        levels.append((q * w, k * w, mask))
        m *= 2
    e = jnp.clip(b - _block_rows(b, SUBLANES, SUBLANES // 2 - 1), -EXP2_CLAMP, EXP2_CLAMP)
    levels.append((q * jnp.exp2(e).astype(BF16), k * jnp.exp2(-e).astype(BF16),
                   ((row >> 3) == (col >> 3)) & (col <= row)))

    ng = ng_ref[...]
    for h in range(HG_HEADS):
        sl = slice(h * HG_DK, (h + 1) * HG_DK)
        st = s_scr[h]
        o = _dot_nt(qg[:, sl], st.astype(BF16))
        sc = jnp.zeros((cs, cs), F32)
        for lq, lk, mask in levels:
            sc = jnp.where(mask, _dot_nt(lq[:, sl], lk[:, sl]), sc)
        o = o + _dot(sc.astype(BF16), v[:, sl])
        s_scr[h] = dec[:, sl] * st + _dot_tn(v[:, sl], kg[:, sl])
        ms = jnp.mean(o * o, axis=-1, keepdims=True)
        y = o * lax.rsqrt(ms + EPS) * ng[:, sl] * g_ref[:, sl].astype(F32)
        y_ref[:, sl] = y.astype(BF16)


def _mlstm_body(qk_ref, v_ref, og_ref, sg_ref, sgt_ref, cw_ref, cb_ref, ng_ref, y_ref, c_scr, m_scr, x_scr, cs):
    x = qk_ref[...].astype(F32)
    prev = x_scr[...]
    sub = lax.broadcasted_iota(I32, (SUBLANES, D_MODEL), 0)
    cw = cw_ref[...]
    conv = cw[3:4, :] * x + cb_ref[...]
    for j in (1, 2, 3):
        xs = pltpu.roll(x, j, 0)
        head = jnp.where(sub < j, pltpu.roll(prev, j, 0), xs[:SUBLANES, :])
        xs = jnp.concatenate([head, xs[SUBLANES:, :]], axis=0)
        conv = conv + cw[3 - j:4 - j, :] * xs
    x_scr[...] = x[cs - SUBLANES:, :]
    qk = conv * _sigmoid(conv)
    q_all = (qk[:, :ML_HEADS * ML_DK] * (ML_DK ** -0.5)).astype(BF16)
    k_all = qk[:, ML_HEADS * ML_DK:]

    row = lax.broadcasted_iota(I32, (cs, cs), 0)
    col = lax.broadcasted_iota(I32, (cs, cs), 1)
    causal = col <= row
    tri = _ones_where(causal)
    sg = sg_ref[...]
    sgt = sgt_ref[...]
    sg_hi, sg_lo = _split_bf16(sg)
    bcol_all = _dot(tri, sg_hi) + _dot(tri, sg_lo)
    sgt_hi, sgt_lo = _split_bf16(sgt)
    brow_all = _dot_nt(sgt_hi, tri) + _dot_nt(sgt_lo, tri)
    lane128 = lax.broadcasted_iota(I32, (cs, 128), 1)
    ones_col = _ones_where(lane128 == 0)
    v = v_ref[...]
    ng = ng_ref[...]

    for h in range(ML_HEADS):
        b_col = bcol_all[:, ML_HEADS + h:ML_HEADS + h + 1]
        b_row = brow_all[ML_HEADS + h:ML_HEADS + h + 1, :]
        ig_col = sg[:, h:h + 1]
        ig_row = sgt[h:h + 1, :]
        m_prev = m_scr[h:h + 1, 0:1]
        q_h = q_all[:, h * ML_DK:(h + 1) * ML_DK]
        k_h = k_all[:, h * ML_DK:(h + 1) * ML_DK]
        v_aug = jnp.concatenate([v[:, h * ML_DV:(h + 1) * ML_DV], ones_col], axis=1)
        c_st = c_scr[h]

        log_intra = jnp.where(causal, b_col - b_row + ig_row, -jnp.inf)
        log_inter = b_col + m_prev
        m_t = jnp.maximum(log_inter, jnp.max(log_intra, axis=-1, keepdims=True))
        w_intra = jnp.exp(log_intra - m_t)
        w_inter = jnp.exp(log_inter - m_t)
        s = _dot_nt(q_h, k_h.astype(BF16)) * w_intra
        tot = w_inter * _dot(q_h, c_st.astype(BF16)) + _dot(s.astype(BF16), v_aug)
        num = tot[:, :ML_DV]
        den = tot[:, ML_DV:ML_DV + 1]
        hid = num / jnp.maximum(jnp.abs(den), jnp.exp(-m_t))

        b_last = b_col[cs - 1:cs, :]
        log_w = b_last - b_col + ig_col
        m_new = jnp.maximum(b_last + m_prev, jnp.max(log_w, axis=0, keepdims=True))
        w_s = jnp.exp(log_w - m_new)
        decay = jnp.exp(b_last + m_prev - m_new)
        c_scr[h] = decay * c_st + _dot_tn((k_h * w_s).astype(BF16), v_aug)
        m_scr[h:h + 1, :] = jnp.broadcast_to(m_new, (1, 128))

        hc = hid - jnp.mean(hid, axis=-1, keepdims=True)
        var = jnp.mean(hc * hc, axis=-1, keepdims=True)
        sl = slice(h * ML_DV, (h + 1) * ML_DV)
        y = hc * lax.rsqrt(var + EPS) * ng[:, sl] * og_ref[:, sl].astype(F32)
        y_ref[:, sl] = y.astype(BF16)


def _mixer_kernel(q_ref, k_ref, v_ref, g_ref, lf_ref, hng_ref, s0_ref,
                  qk_ref, vb_ref, og_ref, sg_ref, sgt_ref, cw_ref, cb_ref, mng_ref, c0_ref, m0_ref, x0_ref,
                  ya_ref, yb_ref, sfin_ref, cfin_ref, mfin_ref, xfin_ref,
                  s_scr, c_scr, m_scr, x_scr, *, chunk):
    c = pl.program_id(1)

    @pl.when(c == 0)
    def _():
        s_scr[...] = s0_ref[...]
        c_scr[...] = c0_ref[...]
        m_scr[...] = m0_ref[...]
        x_scr[...] = x0_ref[...]

    _hgrn_body(q_ref, k_ref, v_ref, g_ref, lf_ref, hng_ref, ya_ref, s_scr, chunk)
    _mlstm_body(qk_ref, vb_ref, og_ref, sg_ref, sgt_ref, cw_ref, cb_ref, mng_ref, yb_ref, c_scr, m_scr, x_scr, chunk)

    @pl.when(c == pl.num_programs(1) - 1)
    def _():
        sfin_ref[...] = s_scr[...]
        cfin_ref[...] = c_scr[...]
        mfin_ref[...] = m_scr[...]
        xfin_ref[...] = x_scr[...]


def _mixer(proj, logf, sg, sgt, hg_norm, conv_w, conv_b, ml_norm, states, nb, chunk):
    m = logf.shape[0]
    nc = m // nb // chunk
    s0, c0, m0, x0 = states
    rows = lambda b, c: (b * nc + c, 0)
    slab = lambda s: pl.BlockSpec((None, chunk, D_MODEL), lambda b, c, s=s: (s, b * nc + c, 0))
    const2 = lambda b, c: (0, 0)
    const3 = lambda b, c: (0, 0, 0)
    per_batch3 = lambda b, c: (b, 0, 0)
    per_batch4 = lambda b, c: (b, 0, 0, 0)
    vec = pl.BlockSpec((1, D_MODEL), const2)
    outs = pl.pallas_call(
        functools.partial(_mixer_kernel, chunk=chunk),
        grid=(nb, nc),
        in_specs=[
            slab(P_QA), slab(P_KA), slab(P_IA), slab(P_GA),
            pl.BlockSpec((chunk, D_MODEL), rows), vec,
            pl.BlockSpec((HG_HEADS, HG_DK, HG_DK), const3),
            slab(P_QKB), slab(P_VB), slab(P_OB),
            pl.BlockSpec((chunk, 128), rows),
            pl.BlockSpec((None, SUBLANES, chunk), lambda b, c: (b * nc + c, 0, 0)),
            pl.BlockSpec((4, D_MODEL), const2), vec, vec,
            pl.BlockSpec((ML_HEADS, ML_DK, ML_AUG), const3),
            pl.BlockSpec((SUBLANES, 128), const2),
            pl.BlockSpec((SUBLANES, D_MODEL), const2),
        ],
        out_specs=[
            pl.BlockSpec((chunk, D_MODEL), rows),
            pl.BlockSpec((chunk, D_MODEL), rows),
            pl.BlockSpec((None, HG_HEADS, HG_DK, HG_DK), per_batch4),
            pl.BlockSpec((None, ML_HEADS, ML_DK, ML_AUG), per_batch4),
            pl.BlockSpec((None, SUBLANES, 128), per_batch3),
            pl.BlockSpec((None, SUBLANES, D_MODEL), per_batch3),
        ],
        out_shape=[
            jax.ShapeDtypeStruct((m, D_MODEL), BF16),
            jax.ShapeDtypeStruct((m, D_MODEL), BF16),
            jax.ShapeDtypeStruct((nb, HG_HEADS, HG_DK, HG_DK), F32),
            jax.ShapeDtypeStruct((nb, ML_HEADS, ML_DK, ML_AUG), F32),
            jax.ShapeDtypeStruct((nb, SUBLANES, 128), F32),
            jax.ShapeDtypeStruct((nb, SUBLANES, D_MODEL), F32),
        ],
        scratch_shapes=[
            pltpu.VMEM((HG_HEADS, HG_DK, HG_DK), F32),
            pltpu.VMEM((ML_HEADS, ML_DK, ML_AUG), F32),
            pltpu.VMEM((SUBLANES, 128), F32),
            pltpu.VMEM((SUBLANES, D_MODEL), F32),
        ],
        compiler_params=_params("parallel", "arbitrary"),
        name="mixer",
    )(proj, proj, proj, proj, logf, hg_norm, s0, proj, proj, proj, sg, sgt, conv_w, conv_b, ml_norm, c0, m0, x0)
    return outs[0], outs[1], tuple(outs[2:])


def _merge_kernel(x_ref, ya_ref, yb_ref, ma_ref, mb_ref, wa_ref, wb_ref, wo_ref, eg_ref, eb_ref, g1_ref, b1_ref,
                  wrh_ref, wrl_ref, rb_ref,
                  h1_ref, slot_ref, gw_ref, cnt_ref, *, tm):
    h0 = _layer_norm(x_ref[...], eg_ref[...], eb_ref[...])
    merged = (ma_ref[...].astype(F32) * _dot(ya_ref[...], wa_ref[...])
              + mb_ref[...].astype(F32) * _dot(yb_ref[...], wb_ref[...]))
    mix = _dot(merged.astype(BF16), wo_ref[...])
    h1 = _layer_norm(DN_ALPHA * h0 + mix, g1_ref[...], b1_ref[...])
    h1_ref[...] = h1

    h_hi, h_lo = _split_bf16(h1)
    logits = _dot_nt(wrh_ref[...], h_hi) + _dot_nt(wrh_ref[...], h_lo) + _dot_nt(wrl_ref[...], h_hi)
    scores = _sigmoid(logits)
    biased = scores + rb_ref[:, 0:1]
    neg_inf = -jnp.inf

    g3 = biased.reshape(N_GROUPS, GROUP_SIZE, tm)
    sub3 = lax.broadcasted_iota(I32, g3.shape, 1)
    top1 = jnp.max(g3, axis=1, keepdims=True)
    first = jnp.min(jnp.where(g3 == top1, sub3, GROUP_SIZE), axis=1, keepdims=True)
    top2 = jnp.max(jnp.where(sub3 == first, neg_inf, g3), axis=1, keepdims=True)
    gs = (top1 + top2).reshape(N_GROUPS, tm)
    gi = lax.broadcasted_iota(I32, gs.shape, 0)
    grank = jnp.zeros(gs.shape, F32)
    for j in range(N_GROUPS):
        r = gs[j:j + 1, :]
        grank = grank + jnp.where((r > gs) | ((r == gs) & (gi > j)), 1.0, 0.0)
    gsel = grank < float(TOPK_GROUPS)
    emask = jnp.broadcast_to(gsel.reshape(N_GROUPS, 1, tm), (N_GROUPS, GROUP_SIZE, tm)).reshape(N_EXPERTS, tm)
    masked = jnp.where(emask, biased, neg_inf)

    ei = lax.broadcasted_iota(I32, masked.shape, 0)
    work = masked
    rank = jnp.full(masked.shape, float(N_EXPERTS), F32)
    for kk in range(TOP_K):
        top = jnp.max(work, axis=0, keepdims=True)
        first = jnp.min(jnp.where(work == top, ei, N_EXPERTS), axis=0, keepdims=True)
        hit = ei == first
        rank = jnp.where(hit, float(kk), rank)
        work = jnp.where(hit, neg_inf, work)
    sel = rank < float(TOP_K)
    sel_w = jnp.where(sel, scores, 0.0)
    gwd = sel_w / jnp.sum(sel_w, axis=0, keepdims=True) * ROUTED_SCALE

    tr = lax.broadcasted_iota(I32, (tm, tm), 0)
    tc = lax.broadcasted_iota(I32, (tm, tm), 1)
    sel_b = _ones_where(sel)
    rloc = _dot(sel_b, _ones_where(tr < tc))
    cnt = _dot(sel_b, jnp.ones((tm, 128), BF16))
    cnt_g = jnp.floor((cnt + (GROUP_ROWS - 1.0)) * (1.0 / GROUP_ROWS)) * GROUP_ROWS
    er = lax.broadcasted_iota(I32, (N_EXPERTS, N_EXPERTS), 0)
    ec = lax.broadcasted_iota(I32, (N_EXPERTS, N_EXPERTS), 1)
    seg_start = _dot(_ones_where(ec < er), cnt_g.astype(BF16))
    slot_e = seg_start[:, 0:1] + rloc
    cnt_ref[...] = cnt

    s_rows, w_rows = [], []
    for kk in range(TOP_K):
        pick = sel & (rank == float(kk))
        s_rows.append(jnp.sum(jnp.where(pick, slot_e, 0.0), axis=0, keepdims=True))
        w_rows.append(jnp.sum(jnp.where(pick, gwd, 0.0), axis=0, keepdims=True))
    slot_ref[...] = jnp.concatenate(s_rows, axis=0).astype(I32)
    gw_ref[...] = jnp.concatenate(w_rows, axis=0)


def _merge(x2d, ya, yb, proj, w_a, w_b, w_o, eg, eb, g1, b1, wr_hi, wr_lo, rbias, tm):
    m = x2d.shape[0]
    tile = lambda i: (i, 0)
    const = lambda i: (0, 0)
    slab = lambda s: pl.BlockSpec((None, tm, D_MODEL), lambda i, s=s: (s, i, 0))
    wspec = pl.BlockSpec((D_MODEL, D_MODEL), const)
    vspec = pl.BlockSpec((1, D_MODEL), const)
    lane_tile = pl.BlockSpec((TOP_K, tm), lambda i: (0, i))
    return pl.pallas_call(
        functools.partial(_merge_kernel, tm=tm),
        grid=(m // tm,),
        in_specs=[
            pl.BlockSpec((tm, D_MODEL), tile), pl.BlockSpec((tm, D_MODEL), tile), pl.BlockSpec((tm, D_MODEL), tile),
            slab(P_MA), slab(P_MB), wspec, wspec, wspec, vspec, vspec, vspec, vspec,
            pl.BlockSpec((N_EXPERTS, D_MODEL), const), pl.BlockSpec((N_EXPERTS, D_MODEL), const),
            pl.BlockSpec((N_EXPERTS, 128), const),
        ],
        out_specs=[
            pl.BlockSpec((tm, D_MODEL), tile), lane_tile, lane_tile,
            pl.BlockSpec((None, N_EXPERTS, 128), lambda i: (i, 0, 0)),
        ],
        out_shape=[
            jax.ShapeDtypeStruct((m, D_MODEL), F32),
            jax.ShapeDtypeStruct((TOP_K, m), I32),
            jax.ShapeDtypeStruct((TOP_K, m), F32),
            jax.ShapeDtypeStruct((m // tm, N_EXPERTS, 128), F32),
        ],
        compiler_params=_params("parallel"),
        name="merge_router",
    )(x2d, ya, yb, proj, proj, w_a, w_b, w_o, eg, eb, g1, b1, wr_hi, wr_lo, rbias)


def _tile_slots(tt):
    return -(-(TOP_K * tt + N_EXPERTS * (GROUP_ROWS - 1)) // SLOT_BLOCK) * SLOT_BLOCK


def _dispatch_kernel(ng_ref, gdst_ref, slot_ref, h_ref, xs_ref, buf, sem, *, tt, s_tile):
    i = pl.program_id(0)
    cur = lax.rem(i, 2)

    def group_copy(b, g, d):
        src = buf.at[b, pl.ds(pl.multiple_of(g * GROUP_ROWS, GROUP_ROWS), GROUP_ROWS), :]
        dst = xs_ref.at[pl.ds(pl.multiple_of(d, GROUP_ROWS), GROUP_ROWS), :]
        return pltpu.make_async_copy(src, dst, sem.at[b])

    def wait_tile(j, b):
        lax.fori_loop(0, ng_ref[j], lambda g, c: (group_copy(b, 0, 0).wait(), c)[1], 0)

    @pl.when(i >= 2)
    def _():
        wait_tile(i - 2, cur)

    hb = h_ref[...].astype(BF16)
    sl = slot_ref[...]
    for r in range(s_tile // SLOT_BLOCK):
        s_iota = lax.broadcasted_iota(I32, (SLOT_BLOCK, tt), 0) + r * SLOT_BLOCK
        p = jnp.zeros((SLOT_BLOCK, tt), F32)
        for kk in range(TOP_K):
            p = jnp.where(s_iota == sl[kk:kk + 1, :], 1.0, p)
        buf[cur, r * SLOT_BLOCK:(r + 1) * SLOT_BLOCK, :] = _dot(p.astype(BF16), hb).astype(BF16)

    lax.fori_loop(0, ng_ref[i], lambda g, c: (group_copy(cur, g, gdst_ref[0, 0, g]).start(), c)[1], 0)

    @pl.when(i == pl.num_programs(0) - 1)
    def _():
        @pl.when(i >= 1)
        def _():
            wait_tile(i - 1, 1 - cur)
        wait_tile(i, cur)


def _dispatch(n_groups, gdst, slot_k, h1, n_slots, tt):
    m = h1.shape[0]
    s_tile = _tile_slots(tt)
    grid_spec = pltpu.PrefetchScalarGridSpec(
        num_scalar_prefetch=1,
        grid=(m // tt,),
        in_specs=[
            pl.BlockSpec((1, 1, s_tile // GROUP_ROWS), lambda i, ng: (i, 0, 0), memory_space=pltpu.SMEM),
            pl.BlockSpec((TOP_K, tt), lambda i, ng: (0, i)),
            pl.BlockSpec((tt, D_MODEL), lambda i, ng: (i, 0)),
        ],
        out_specs=pl.BlockSpec(memory_space=pl.ANY),
        scratch_shapes=[pltpu.VMEM((2, s_tile, D_MODEL), BF16), pltpu.SemaphoreType.DMA((2,))],
    )
    return pl.pallas_call(
        functools.partial(_dispatch_kernel, tt=tt, s_tile=s_tile),
        grid_spec=grid_spec,
        out_shape=jax.ShapeDtypeStruct((n_slots, D_MODEL), BF16),
        compiler_params=_params("arbitrary"),
        name="dispatch",
    )(n_groups, gdst, slot_k, h1)


def _experts_kernel(be_ref, nu_ref, x_ref, wg_ref, wu_ref, wd_ref, y_ref):
    del be_ref
    i = pl.program_id(0)

    @pl.when(i < nu_ref[0])
    def _():
        xb = x_ref[...]
        a = _dot(xb, wg_ref[...])
        u = _dot(xb, wu_ref[...])
        y_ref[...] = _dot((a * _sigmoid(a) * u).astype(BF16), wd_ref[...]).astype(BF16)


def _experts(blk_expert, n_used, xs, wg, wu, wd):
    n_slots = xs.shape[0]
    n_blocks = n_slots // MOE_BLOCK
    blk = lambda i, be, nu: (jnp.minimum(i, nu[0] - 1), 0)
    wsel = lambda i, be, nu: (be[jnp.minimum(i, nu[0] - 1)], 0, 0)
    grid_spec = pltpu.PrefetchScalarGridSpec(
        num_scalar_prefetch=2,
        grid=(n_blocks,),
        in_specs=[
            pl.BlockSpec((MOE_BLOCK, D_MODEL), blk),
            pl.BlockSpec((None, D_MODEL, D_EXPERT), wsel),
            pl.BlockSpec((None, D_MODEL, D_EXPERT), wsel),
            pl.BlockSpec((None, D_EXPERT, D_MODEL), wsel),
        ],
        out_specs=pl.BlockSpec((MOE_BLOCK, D_MODEL), blk),
    )
    return pl.pallas_call(
        _experts_kernel,
        grid_spec=grid_spec,
        out_shape=jax.ShapeDtypeStruct((n_slots, D_MODEL), BF16),
        compiler_params=_params("arbitrary"),
        name="experts",
    )(blk_expert, n_used, xs, wg, wu, wd)


def _combine_kernel(ng_ref, gcur_ref, gnext_ref, slot_ref, gw_ref, h_ref, y_ref, sg_ref, su_ref, sd_ref, g2_ref, b2_ref,
                    o_ref, ybuf, sem, *, tt, s_tile):
    i = pl.program_id(0)
    cur = lax.rem(i, 2)

    def group_copy(b, g, d):
        src = y_ref.at[pl.ds(pl.multiple_of(d, GROUP_ROWS), GROUP_ROWS), :]
        dst = ybuf.at[b, pl.ds(pl.multiple_of(g * GROUP_ROWS, GROUP_ROWS), GROUP_ROWS), :]
        return pltpu.make_async_copy(src, dst, sem.at[b])

    def fetch(j, b, table_ref):
        lax.fori_loop(0, ng_ref[j], lambda g, c: (group_copy(b, g, table_ref[0, 0, g]).start(), c)[1], 0)

    @pl.when(i == 0)
    def _():
        ybuf[...] = jnp.zeros_like(ybuf)
        fetch(0, 0, gcur_ref)

    @pl.when(i + 1 < pl.num_programs(0))
    def _():
        fetch(i + 1, 1 - cur, gnext_ref)

    h1 = h_ref[...]
    hb = h1.astype(BF16)
    a = _dot(hb, sg_ref[...])
    u = _dot(hb, su_ref[...])
    shared = _dot((a * _sigmoid(a) * u).astype(BF16), sd_ref[...])

    slot = slot_ref[...]
    gw = gw_ref[...]
    lax.fori_loop(0, ng_ref[i], lambda g, c: (group_copy(cur, 0, 0).wait(), c)[1], 0)
    routed = jnp.zeros((tt, D_MODEL), F32)
    for r in range(s_tile // SLOT_BLOCK):
        lane = lax.broadcasted_iota(I32, (tt, SLOT_BLOCK), 1) + r * SLOT_BLOCK
        p = jnp.zeros((tt, SLOT_BLOCK), F32)
        for kk in range(TOP_K):
            p = jnp.where(lane == slot[:, kk:kk + 1], gw[:, kk:kk + 1], p)
        routed = routed + _dot(p.astype(BF16), ybuf[cur, r * SLOT_BLOCK:(r + 1) * SLOT_BLOCK, :])
    o_ref[...] = _layer_norm(DN_ALPHA * h1 + (routed + shared), g2_ref[...], b2_ref[...])


def _combine(n_groups, gdst, slot_t, gw_t, h1, y, wsg, wsu, wsd, g2, b2, tt):
    m = h1.shape[0]
    nt = m // tt
    s_tile = _tile_slots(tt)
    const = lambda i, ng: (0, 0)
    table = lambda f: pl.BlockSpec((1, 1, s_tile // GROUP_ROWS), f, memory_space=pltpu.SMEM)
    grid_spec = pltpu.PrefetchScalarGridSpec(
        num_scalar_prefetch=1,
        grid=(nt,),
        in_specs=[
            table(lambda i, ng: (i, 0, 0)),
            table(lambda i, ng: (jnp.minimum(i + 1, nt - 1), 0, 0)),
            pl.BlockSpec((tt, TOP_K), lambda i, ng: (i, 0)),
            pl.BlockSpec((tt, TOP_K), lambda i, ng: (i, 0)),
            pl.BlockSpec((tt, D_MODEL), lambda i, ng: (i, 0)),
            pl.BlockSpec(memory_space=pl.ANY),
            pl.BlockSpec((D_MODEL, D_EXPERT), const),
            pl.BlockSpec((D_MODEL, D_EXPERT), const),
            pl.BlockSpec((D_EXPERT, D_MODEL), const),
            pl.BlockSpec((1, D_MODEL), const),
            pl.BlockSpec((1, D_MODEL), const),
        ],
        out_specs=pl.BlockSpec((tt, D_MODEL), lambda i, ng: (i, 0)),
        scratch_shapes=[pltpu.VMEM((2, s_tile, D_MODEL), BF16), pltpu.SemaphoreType.DMA((2,))],
    )
    return pl.pallas_call(
        functools.partial(_combine_kernel, tt=tt, s_tile=s_tile),
        grid_spec=grid_spec,
        out_shape=jax.ShapeDtypeStruct((m, D_MODEL), F32),
        compiler_params=_params("arbitrary"),
        name="combine",
    )(n_groups, gdst, gdst, slot_t, gw_t, h1, y, wsg, wsu, wsd, g2, b2)


def _pick_tile(m, pref):
    t = min(pref, m)
    while m % t:
        t //= 2
    return t


def _gates_time_on_lanes(sg, chunk):
    m = sg.shape[0]
    return sg[:, :SUBLANES].reshape(m // chunk, chunk, SUBLANES).transpose(0, 2, 1)


def _forward(x, meta_tokens, ln_emb_g, ln_emb_b, w_in, hg_lb_logits, hg_norm_g, ml_conv_w, ml_conv_b,
             ml_ig_bias, ml_fg_bias, ml_norm_g, w_branch_a, w_branch_b, w_out, ln1_g, ln1_b,
             w_router, router_bias, w_exp_gate, w_exp_up, w_exp_down, w_sh_gate, w_sh_up, w_sh_down,
             ln2_g, ln2_b, *, chunk, tm_proj, tm_moe):
    nb, seq, d = x.shape
    m = nb * seq
    row = lambda a: a.reshape(1, -1).astype(F32)

    w = w_in[0]
    kw = HG_HEADS * HG_DK
    o_qa, o_fa, o_ia, o_ga = 0, kw, 2 * kw, 3 * kw
    o_qb = 4 * kw
    o_kb = o_qb + ML_HEADS * ML_DK
    o_vb = o_kb + ML_HEADS * ML_DK
    o_ob = o_vb + ML_HEADS * ML_DV
    o_ig = o_ob + ML_HEADS * ML_DV
    o_fg = o_ig + ML_HEADS
    o_ma = o_fg + ML_HEADS
    o_mb = o_ma + D_MODEL
    cols = lambda o, n: w[:, o:o + n]
    w_cat = jnp.concatenate([
        cols(o_qa, kw), cols(o_fa, kw), cols(o_ia, kw), cols(o_ga, kw),
        cols(o_qb, 2 * ML_HEADS * ML_DK), cols(o_vb, ML_HEADS * ML_DV), cols(o_ob, ML_HEADS * ML_DV),
        cols(o_ma, D_MODEL), cols(o_mb, D_MODEL)], axis=1).astype(BF16)
    w_small = jnp.pad(cols(o_ig, 2 * ML_HEADS), ((0, 0), (0, 128 - 2 * ML_HEADS))).astype(BF16)
    gate_bias = jnp.pad(jnp.concatenate([ml_ig_bias[0], ml_fg_bias[0]]).astype(F32), (0, 128 - 2 * ML_HEADS)).reshape(1, 128)
    lb = jax.nn.softmax(hg_lb_logits.astype(F32), axis=0)[0].reshape(1, -1)
    eg, eb = row(ln_emb_g), row(ln_emb_b)
    conv_w = ml_conv_w[0].astype(F32)
    conv_b = row(ml_conv_b[0])
    hgn, mln = row(hg_norm_g[0]), row(ml_norm_g[0])

    p_m, lf_m, sg_m = _inproj(meta_tokens.astype(F32), eg, eb, w_cat, w_small, lb, gate_bias, N_META)
    zero_states = (jnp.zeros((HG_HEADS, HG_DK, HG_DK), F32), jnp.zeros((ML_HEADS, ML_DK, ML_AUG), F32),
                   jnp.zeros((SUBLANES, 128), F32), jnp.zeros((SUBLANES, D_MODEL), F32))
    _, _, meta_states = _mixer(p_m, lf_m, sg_m, _gates_time_on_lanes(sg_m, N_META), hgn, conv_w, conv_b, mln,
                               zero_states, 1, N_META)

    x2d = x.reshape(m, d).astype(F32)
    proj, logf, sg = _inproj(x2d, eg, eb, w_cat, w_small, lb, gate_bias, tm_proj)
    ya, yb, _ = _mixer(proj, logf, sg, _gates_time_on_lanes(sg, chunk), hgn, conv_w, conv_b, mln,
                       tuple(s[0] for s in meta_states), nb, chunk)

    wr = w_router[0].T.astype(F32)
    wr_hi, wr_lo = _split_bf16(wr)
    rbias = jnp.broadcast_to(router_bias[0].astype(F32).reshape(N_EXPERTS, 1), (N_EXPERTS, 128))
    tt = tm_moe
    h1, slot_k, gw, cnt = _merge(
        x2d, ya, yb, proj, w_branch_a[0].astype(BF16), w_branch_b[0].astype(BF16), w_out[0].astype(BF16),
        eg, eb, row(ln1_g[0]), row(ln1_b[0]), wr_hi, wr_lo, rbias, tt)

    nt = m // tt
    s_tile = _tile_slots(tt)
    cnt8 = (cnt[:, :, 0].astype(I32) + GROUP_ROWS - 1) // GROUP_ROWS * GROUP_ROWS
    seg_end = jnp.cumsum(cnt8, axis=1)
    seg_off = seg_end - cnt8
    tile_rows = seg_end[:, -1]
    run = jnp.cumsum(cnt8, axis=0) - cnt8
    tot8 = jnp.sum(cnt8, axis=0)
    padded = (tot8 + MOE_BLOCK - 1) // MOE_BLOCK * MOE_BLOCK
    pends = jnp.cumsum(padded)
    gshift = (pends - padded)[None, :] + run - seg_off
    g_rows = jnp.arange(s_tile // GROUP_ROWS, dtype=I32) * GROUP_ROWS
    e_of_g = jnp.sum((seg_end[:, None, :] <= g_rows[None, :, None]).astype(I32), axis=-1)
    shift_g = jnp.sum(jnp.where(e_of_g[..., None] == jnp.arange(N_EXPERTS, dtype=I32), gshift[:, None, :], 0), axis=-1)
    gdst = jnp.where(g_rows[None, :] < tile_rows[:, None], g_rows[None, :] + shift_g, 0).astype(I32)
    gdst = gdst.reshape(nt, 1, s_tile // GROUP_ROWS)
    n_groups = (tile_rows // GROUP_ROWS).astype(I32)
    n_blocks = -(-(m * TOP_K + nt * N_EXPERTS * (GROUP_ROWS - 1)) // MOE_BLOCK) + N_EXPERTS
    blk_start = jnp.arange(n_blocks, dtype=I32) * MOE_BLOCK
    blk_expert = jnp.minimum(jnp.sum((pends[None, :] <= blk_start[:, None]).astype(I32), axis=1), N_EXPERTS - 1)
    n_used = (pends[-1:] // MOE_BLOCK).astype(I32)

    xs = _dispatch(n_groups, gdst, slot_k, h1, n_blocks * MOE_BLOCK, tt)
    y = _experts(blk_expert, n_used, xs, w_exp_gate[0].astype(BF16), w_exp_up[0].astype(BF16),
                 w_exp_down[0].astype(BF16))
    out = _combine(n_groups, gdst, slot_k.T, gw.T, h1, y, w_sh_gate[0].astype(BF16), w_sh_up[0].astype(BF16),
                   w_sh_down[0].astype(BF16), row(ln2_g[0]), row(ln2_b[0]), tt)
    return out.reshape(nb, seq, d).astype(x.dtype)


def kernel(x, meta_tokens, ln_emb_g, ln_emb_b, w_in, hg_lb_logits, hg_norm_g, ml_conv_w, ml_conv_b, ml_ig_bias, ml_fg_bias, ml_norm_g, w_branch_a, w_branch_b, w_out, ln1_g, ln1_b, w_router, router_bias, w_exp_gate, w_exp_up, w_exp_down, w_sh_gate, w_sh_up, w_sh_down, ln2_g, ln2_b):
    m = x.shape[0] * x.shape[1]
    return _forward(x, meta_tokens, ln_emb_g, ln_emb_b, w_in, hg_lb_logits, hg_norm_g, ml_conv_w, ml_conv_b,
                    ml_ig_bias, ml_fg_bias, ml_norm_g, w_branch_a, w_branch_b, w_out, ln1_g, ln1_b,
                    w_router, router_bias, w_exp_gate, w_exp_up, w_exp_down, w_sh_gate, w_sh_up, w_sh_down,
                    ln2_g, ln2_b, chunk=_pick_tile(x.shape[1], 256), tm_proj=_pick_tile(m, 512),
                    tm_moe=_pick_tile(m, 256))
```

```python
import functools

import jax
import jax.numpy as jnp
from jax import lax
from jax.experimental import pallas as pl
from jax.experimental.pallas import tpu as pltpu

F32, BF16, I32 = jnp.float32, jnp.bfloat16, jnp.int32

D_MODEL = 1024
N_META = 16
HG_HEADS = 8
HG_DK = 128
ML_HEADS = 4
ML_DK = 128
ML_DV = 256
ML_AUG = ML_DV + 128
N_EXPERTS = 64
TOP_K = 8
N_GROUPS = 8
GROUP_SIZE = N_EXPERTS // N_GROUPS
TOPK_GROUPS = 4
D_EXPERT = 256
ROUTED_SCALE = 2.5
MOE_BLOCK = 1024
SLOT_BLOCK = 256
DN_ALPHA = 2.0 ** 0.25
EPS = 1e-5
LOG2E = 1.4426950408889634
EXP2_CLAMP = 115.0
SUBLANES = 8
GROUP_ROWS = 16

P_QA, P_KA, P_IA, P_GA, P_QKB, P_VB, P_OB, P_MA, P_MB = range(9)
N_SLABS = 9

VMEM_LIMIT = 56 * 1024 * 1024


def _params(*sem):
    return pltpu.CompilerParams(dimension_semantics=sem, vmem_limit_bytes=VMEM_LIMIT)


def _sigmoid(x):
    return 1.0 / (1.0 + jnp.exp(-x))


def _log_sigmoid(x):
    return jnp.minimum(x, 0.0) - jnp.log(1.0 + jnp.exp(-jnp.abs(x)))


def _layer_norm(x, g, b):
    xc = x - jnp.mean(x, axis=-1, keepdims=True)
    var = jnp.mean(xc * xc, axis=-1, keepdims=True)
    return xc * lax.rsqrt(var + EPS) * g + b


def _dot(a, b):
    return jnp.dot(a, b, preferred_element_type=F32)


def _dot_nt(a, b):
    return lax.dot_general(a, b, (((1,), (1,)), ((), ())), preferred_element_type=F32)


def _dot_tn(a, b):
    return lax.dot_general(a, b, (((0,), (0,)), ((), ())), preferred_element_type=F32)


def _split_bf16(x):
    hi = x.astype(BF16)
    lo = (x - hi.astype(F32)).astype(BF16)
    return hi, lo


def _neg_abs(x):
    return lax.bitcast_convert_type(lax.bitcast_convert_type(x, I32) | jnp.int32(-2 ** 31), F32)


def _ones_where(cond):
    return jnp.where(cond, 1.0, 0.0).astype(BF16)


def _inproj_kernel(x_ref, g_ref, b_ref, w_ref, ws_ref, lb_ref, gb_ref, p_ref, lf_ref, sg_ref):
    hb = _layer_norm(x_ref[...], g_ref[...], b_ref[...]).astype(BF16)
    s = _dot(hb, ws_ref[...]) + gb_ref[...]
    lane = lax.broadcasted_iota(I32, s.shape, 1)
    sg_ref[...] = jnp.where(lane < ML_HEADS, s, _log_sigmoid(s))
    for n in range(N_SLABS):
        acc = _dot(hb, w_ref[:, n * D_MODEL:(n + 1) * D_MODEL])
        if n == P_KA:
            lb = lb_ref[...]
            f = lb + (1.0 - lb) * _sigmoid(acc)
            lf_ref[...] = jnp.log(f)
            acc = 1.0 - f
        elif n == P_GA:
            acc = acc * _sigmoid(acc)
        elif n >= P_OB:
            acc = _sigmoid(acc)
        p_ref[n] = acc.astype(BF16)


def _inproj(x2d, ln_g, ln_b, w_cat, w_small, lb, gate_bias, tm):
    m = x2d.shape[0]
    const = lambda i: (0, 0)
    resident = lambda shape: pl.BlockSpec(shape, const, pipeline_mode=pl.Buffered(1))
    return pl.pallas_call(
        _inproj_kernel,
        grid=(m // tm,),
        in_specs=[
            pl.BlockSpec((tm, D_MODEL), lambda i: (i, 0)),
            resident((1, D_MODEL)),
            resident((1, D_MODEL)),
            resident((D_MODEL, N_SLABS * D_MODEL)),
            resident((D_MODEL, 128)),
            resident((1, D_MODEL)),
            resident((1, 128)),
        ],
        out_specs=[
            pl.BlockSpec((N_SLABS, tm, D_MODEL), lambda i: (0, i, 0)),
            pl.BlockSpec((tm, D_MODEL), lambda i: (i, 0)),
            pl.BlockSpec((tm, 128), lambda i: (i, 0)),
        ],
        out_shape=[
            jax.ShapeDtypeStruct((N_SLABS, m, D_MODEL), BF16),
            jax.ShapeDtypeStruct((m, D_MODEL), F32),
            jax.ShapeDtypeStruct((m, 128), F32),
        ],
        compiler_params=_params("parallel"),
        name="inproj",
    )(x2d, ln_g, ln_b, w_cat, w_small, lb, gate_bias)


def _block_rows(b, block, pick):
    c, w = b.shape
    parts = [jnp.broadcast_to(b[j * block + pick:j * block + pick + 1, :], (block, w))
             for j in range(c // block)]
    return parts[0] if len(parts) == 1 else jnp.concatenate(parts, axis=0)


def _hgrn_body(q_ref, k_ref, v_ref, g_ref, lf_ref, ng_ref, y_ref, s_scr, cs):
    row = lax.broadcasted_iota(I32, (cs, cs), 0)
    col = lax.broadcasted_iota(I32, (cs, cs), 1)
    tri = _ones_where(col <= row)
    lf_hi, lf_lo = _split_bf16(lf_ref[...])
    b = (_dot(tri, lf_hi) + _dot(tri, lf_lo)) * LOG2E
    q = q_ref[...]
    k = k_ref[...]
    v = v_ref[...]
    blast = b[cs - 1:cs, :]
    qg = q * jnp.exp2(b).astype(BF16)
    kg = k * jnp.exp2(blast - b).astype(BF16)
    dec = jnp.exp2(blast)

    levels = []
    m = SUBLANES
    while 2 * m <= cs:
        w = jnp.exp2(_neg_abs(b - _block_rows(b, 2 * m, m - 1))).astype(BF16)
        sh = (2 * m).bit_length() - 1
        mask = ((row >> sh) == (col >> sh)) & ((row & (2 * m - 1)) >= m) & ((col & (2 * m - 1)) < m)
        levels.append((q * w, k * w, mask))
        m *= 2
    e = jnp.clip(b - _block_rows(b, SUBLANES, SUBLANES // 2 - 1), -EXP2_CLAMP, EXP2_CLAMP)
    levels.append((q * jnp.exp2(e).astype(BF16), k * jnp.exp2(-e).astype(BF16),
                   ((row >> 3) == (col >> 3)) & (col <= row)))

    ng = ng_ref[...]
    for h in range(HG_HEADS):
        sl = slice(h * HG_DK, (h + 1) * HG_DK)
        st = s_scr[h]
        o = _dot_nt(qg[:, sl], st.astype(BF16))
        sc = jnp.zeros((cs, cs), F32)
        for lq, lk, mask in levels:
            sc = jnp.where(mask, _dot_nt(lq[:, sl], lk[:, sl]), sc)
        o = o + _dot(sc.astype(BF16), v[:, sl])
        s_scr[h] = dec[:, sl] * st + _dot_tn(v[:, sl], kg[:, sl])
        ms = jnp.mean(o * o, axis=-1, keepdims=True)
        y = o * lax.rsqrt(ms + EPS) * ng[:, sl] * g_ref[:, sl].astype(F32)
        y_ref[:, sl] = y.astype(BF16)


def _mlstm_body(qk_ref, v_ref, og_ref, sg_ref, sgt_ref, cw_ref, cb_ref, ng_ref, y_ref, c_scr, m_scr, x_scr, cs):
    x = qk_ref[...].astype(F32)
    prev = x_scr[...]
    sub = lax.broadcasted_iota(I32, (SUBLANES, D_MODEL), 0)
    cw = cw_ref[...]
    conv = cw[3:4, :] * x + cb_ref[...]
    for j in (1, 2, 3):
        xs = pltpu.roll(x, j, 0)
        head = jnp.where(sub < j, pltpu.roll(prev, j, 0), xs[:SUBLANES, :])
        xs = jnp.concatenate([head, xs[SUBLANES:, :]], axis=0)
        conv = conv + cw[3 - j:4 - j, :] * xs
    x_scr[...] = x[cs - SUBLANES:, :]
    qk = conv * _sigmoid(conv)
    q_all = (qk[:, :ML_HEADS * ML_DK] * (ML_DK ** -0.5)).astype(BF16)
    k_all = qk[:, ML_HEADS * ML_DK:]

    row = lax.broadcasted_iota(I32, (cs, cs), 0)
    col = lax.broadcasted_iota(I32, (cs, cs), 1)
    causal = col <= row
    tri = _ones_where(causal)
    sg = sg_ref[...]
    sgt = sgt_ref[...]
    sg_hi, sg_lo = _split_bf16(sg)
    bcol_all = _dot(tri, sg_hi) + _dot(tri, sg_lo)
    sgt_hi, sgt_lo = _split_bf16(sgt)
    brow_all = _dot_nt(sgt_hi, tri) + _dot_nt(sgt_lo, tri)
    lane128 = lax.broadcasted_iota(I32, (cs, 128), 1)
    ones_col = _ones_where(lane128 == 0)
    v = v_ref[...]
    ng = ng_ref[...]

    for h in range(ML_HEADS):
        b_col = bcol_all[:, ML_HEADS + h:ML_HEADS + h + 1]
        b_row = brow_all[ML_HEADS + h:ML_HEADS + h + 1, :]
        ig_col = sg[:, h:h + 1]
        ig_row = sgt[h:h + 1, :]
        m_prev = m_scr[h:h + 1, 0:1]
        q_h = q_all[:, h * ML_DK:(h + 1) * ML_DK]
        k_h = k_all[:, h * ML_DK:(h + 1) * ML_DK]
        v_aug = jnp.concatenate([v[:, h * ML_DV:(h + 1) * ML_DV], ones_col], axis=1)
        c_st = c_scr[h]

        log_intra = jnp.where(causal, b_col - b_row + ig_row, -jnp.inf)
        log_inter = b_col + m_prev
        m_t = jnp.maximum(log_inter, jnp.max(log_intra, axis=-1, keepdims=True))
        w_intra = jnp.exp(log_intra - m_t)
        w_inter = jnp.exp(log_inter - m_t)
        s = _dot_nt(q_h, k_h.astype(BF16)) * w_intra
        tot = w_inter * _dot(q_h, c_st.astype(BF16)) + _dot(s.astype(BF16), v_aug)
        num = tot[:, :ML_DV]
        den = tot[:, ML_DV:ML_DV + 1]
        hid = num / jnp.maximum(jnp.abs(den), jnp.exp(-m_t))

        b_last = b_col[cs - 1:cs, :]
        log_w = b_last - b_col + ig_col
        m_new = jnp.maximum(b_last + m_prev, jnp.max(log_w, axis=0, keepdims=True))
        w_s = jnp.exp(log_w - m_new)
        decay = jnp.exp(b_last + m_prev - m_new)
        c_scr[h] = decay * c_st + _dot_tn((k_h * w_s).astype(BF16), v_aug)
        m_scr[h:h + 1, :] = jnp.broadcast_to(m_new, (1, 128))

        hc = hid - jnp.mean(hid, axis=-1, keepdims=True)
        var = jnp.mean(hc * hc, axis=-1, keepdims=True)
        sl = slice(h * ML_DV, (h + 1) * ML_DV)
        y = hc * lax.rsqrt(var + EPS) * ng[:, sl] * og_ref[:, sl].astype(F32)
        y_ref[:, sl] = y.astype(BF16)


def _mixer_kernel(q_ref, k_ref, v_ref, g_ref, lf_ref, hng_ref, s0_ref,
                  qk_ref, vb_ref, og_ref, sg_ref, sgt_ref, cw_ref, cb_ref, mng_ref, c0_ref, m0_ref, x0_ref,
                  ya_ref, yb_ref, sfin_ref, cfin_ref, mfin_ref, xfin_ref,
                  s_scr, c_scr, m_scr, x_scr, *, chunk):
    c = pl.program_id(1)

    @pl.when(c == 0)
    def _():
        s_scr[...] = s0_ref[...]
        c_scr[...] = c0_ref[...]
        m_scr[...] = m0_ref[...]
        x_scr[...] = x0_ref[...]

    _hgrn_body(q_ref, k_ref, v_ref, g_ref, lf_ref, hng_ref, ya_ref, s_scr, chunk)
    _mlstm_body(qk_ref, vb_ref, og_ref, sg_ref, sgt_ref, cw_ref, cb_ref, mng_ref, yb_ref, c_scr, m_scr, x_scr, chunk)

    @pl.when(c == pl.num_programs(1) - 1)
    def _():
        sfin_ref[...] = s_scr[...]
        cfin_ref[...] = c_scr[...]
        mfin_ref[...] = m_scr[...]
        xfin_ref[...] = x_scr[...]


def _mixer(proj, logf, sg, sgt, hg_norm, conv_w, conv_b, ml_norm, states, nb, chunk):
    m = logf.shape[0]
    nc = m // nb // chunk
    s0, c0, m0, x0 = states
    rows = lambda b, c: (b * nc + c, 0)
    slab = lambda s: pl.BlockSpec((None, chunk, D_MODEL), lambda b, c, s=s: (s, b * nc + c, 0))
    const2 = lambda b, c: (0, 0)
    const3 = lambda b, c: (0, 0, 0)
    per_batch3 = lambda b, c: (b, 0, 0)
    per_batch4 = lambda b, c: (b, 0, 0, 0)
    vec = pl.BlockSpec((1, D_MODEL), const2)
    outs = pl.pallas_call(
        functools.partial(_mixer_kernel, chunk=chunk),
        grid=(nb, nc),
        in_specs=[
            slab(P_QA), slab(P_KA), slab(P_IA), slab(P_GA),
            pl.BlockSpec((chunk, D_MODEL), rows), vec,
            pl.BlockSpec((HG_HEADS, HG_DK, HG_DK), const3),
            slab(P_QKB), slab(P_VB), slab(P_OB),
            pl.BlockSpec((chunk, 128), rows),
            pl.BlockSpec((None, SUBLANES, chunk), lambda b, c: (b * nc + c, 0, 0)),
            pl.BlockSpec((4, D_MODEL), const2), vec, vec,
            pl.BlockSpec((ML_HEADS, ML_DK, ML_AUG), const3),
            pl.BlockSpec((SUBLANES, 128), const2),
            pl.BlockSpec((SUBLANES, D_MODEL), const2),
        ],
        out_specs=[
            pl.BlockSpec((chunk, D_MODEL), rows),
            pl.BlockSpec((chunk, D_MODEL), rows),
            pl.BlockSpec((None, HG_HEADS, HG_DK, HG_DK), per_batch4),
            pl.BlockSpec((None, ML_HEADS, ML_DK, ML_AUG), per_batch4),
            pl.BlockSpec((None, SUBLANES, 128), per_batch3),
            pl.BlockSpec((None, SUBLANES, D_MODEL), per_batch3),
        ],
        out_shape=[
            jax.ShapeDtypeStruct((m, D_MODEL), BF16),
            jax.ShapeDtypeStruct((m, D_MODEL), BF16),
            jax.ShapeDtypeStruct((nb, HG_HEADS, HG_DK, HG_DK), F32),
            jax.ShapeDtypeStruct((nb, ML_HEADS, ML_DK, ML_AUG), F32),
            jax.ShapeDtypeStruct((nb, SUBLANES, 128), F32),
            jax.ShapeDtypeStruct((nb, SUBLANES, D_MODEL), F32),
        ],
        scratch_shapes=[
            pltpu.VMEM((HG_HEADS, HG_DK, HG_DK), F32),
            pltpu.VMEM((ML_HEADS, ML_DK, ML_AUG), F32),
            pltpu.VMEM((SUBLANES, 128), F32),
            pltpu.VMEM((SUBLANES, D_MODEL), F32),
        ],
        compiler_params=_params("parallel", "arbitrary"),
        name="mixer",
    )(proj, proj, proj, proj, logf, hg_norm, s0, proj, proj, proj, sg, sgt, conv_w, conv_b, ml_norm, c0, m0, x0)
    return outs[0], outs[1], tuple(outs[2:])


def _merge_kernel(x_ref, ya_ref, yb_ref, ma_ref, mb_ref, wa_ref, wb_ref, wo_ref, eg_ref, eb_ref, g1_ref, b1_ref,
                  wrh_ref, wrl_ref, rb_ref,
                  h1_ref, slot_ref, gw_ref, cnt_ref, *, tm):
    h0 = _layer_norm(x_ref[...], eg_ref[...], eb_ref[...])
    merged = (ma_ref[...].astype(F32) * _dot(ya_ref[...], wa_ref[...])
              + mb_ref[...].astype(F32) * _dot(yb_ref[...], wb_ref[...]))
    mix = _dot(merged.astype(BF16), wo_ref[...])
    h1 = _layer_norm(DN_ALPHA * h0 + mix, g1_ref[...], b1_ref[...])
    h1_ref[...] = h1

    h_hi, h_lo = _split_bf16(h1)
    logits = _dot_nt(wrh_ref[...], h_hi) + _dot_nt(wrh_ref[...], h_lo) + _dot_nt(wrl_ref[...], h_hi)
    scores = _sigmoid(logits)
    biased = scores + rb_ref[:, 0:1]
    neg_inf = -jnp.inf

    g3 = biased.reshape(N_GROUPS, GROUP_SIZE, tm)
    sub3 = lax.broadcasted_iota(I32, g3.shape, 1)
    top1 = jnp.max(g3, axis=1, keepdims=True)
    first = jnp.min(jnp.where(g3 == top1, sub3, GROUP_SIZE), axis=1, keepdims=True)
    top2 = jnp.max(jnp.where(sub3 == first, neg_inf, g3), axis=1, keepdims=True)
    gs = (top1 + top2).reshape(N_GROUPS, tm)
    gi = lax.broadcasted_iota(I32, gs.shape, 0)
    grank = jnp.zeros(gs.shape, F32)
    for j in range(N_GROUPS):
        r = gs[j:j + 1, :]
        grank = grank + jnp.where((r > gs) | ((r == gs) & (gi > j)), 1.0, 0.0)
    gsel = grank < float(TOPK_GROUPS)
    emask = jnp.broadcast_to(gsel.reshape(N_GROUPS, 1, tm), (N_GROUPS, GROUP_SIZE, tm)).reshape(N_EXPERTS, tm)
    masked = jnp.where(emask, biased, neg_inf)

    ei = lax.broadcasted_iota(I32, masked.shape, 0)
    work = masked
    rank = jnp.full(masked.shape, float(N_EXPERTS), F32)
    for kk in range(TOP_K):
        top = jnp.max(work, axis=0, keepdims=True)
        first = jnp.min(jnp.where(work == top, ei, N_EXPERTS), axis=0, keepdims=True)
        hit = ei == first
        rank = jnp.where(hit, float(kk), rank)
        work = jnp.where(hit, neg_inf, work)
    sel = rank < float(TOP_K)
    sel_w = jnp.where(sel, scores, 0.0)
    gwd = sel_w / jnp.sum(sel_w, axis=0, keepdims=True) * ROUTED_SCALE

    tr = lax.broadcasted_iota(I32, (tm, tm), 0)
    tc = lax.broadcasted_iota(I32, (tm, tm), 1)
    sel_b = _ones_where(sel)
    rloc = _dot(sel_b, _ones_where(tr < tc))
    cnt = _dot(sel_b, jnp.ones((tm, 128), BF16))
    cnt_g = jnp.floor((cnt + (GROUP_ROWS - 1.0)) * (1.0 / GROUP_ROWS)) * GROUP_ROWS
    er = lax.broadcasted_iota(I32, (N_EXPERTS, N_EXPERTS), 0)
    ec = lax.broadcasted_iota(I32, (N_EXPERTS, N_EXPERTS), 1)
    seg_start = _dot(_ones_where(ec < er), cnt_g.astype(BF16))
    slot_e = seg_start[:, 0:1] + rloc
    cnt_ref[...] = cnt

    s_rows, w_rows = [], []
    for kk in range(TOP_K):
        pick = sel & (rank == float(kk))
        s_rows.append(jnp.sum(jnp.where(pick, slot_e, 0.0), axis=0, keepdims=True))
        w_rows.append(jnp.sum(jnp.where(pick, gwd, 0.0), axis=0, keepdims=True))
    slot_ref[...] = jnp.concatenate(s_rows, axis=0).astype(I32)
    gw_ref[...] = jnp.concatenate(w_rows, axis=0)


def _merge(x2d, ya, yb, proj, w_a, w_b, w_o, eg, eb, g1, b1, wr_hi, wr_lo, rbias, tm):
    m = x2d.shape[0]
    tile = lambda i: (i, 0)
    const = lambda i: (0, 0)
    slab = lambda s: pl.BlockSpec((None, tm, D_MODEL), lambda i, s=s: (s, i, 0))
    wspec = pl.BlockSpec((D_MODEL, D_MODEL), const)
    vspec = pl.BlockSpec((1, D_MODEL), const)
    lane_tile = pl.BlockSpec((TOP_K, tm), lambda i: (0, i))
    return pl.pallas_call(
        functools.partial(_merge_kernel, tm=tm),
        grid=(m // tm,),
        in_specs=[
            pl.BlockSpec((tm, D_MODEL), tile), pl.BlockSpec((tm, D_MODEL), tile), pl.BlockSpec((tm, D_MODEL), tile),
            slab(P_MA), slab(P_MB), wspec, wspec, wspec, vspec, vspec, vspec, vspec,
            pl.BlockSpec((N_EXPERTS, D_MODEL), const), pl.BlockSpec((N_EXPERTS, D_MODEL), const),
            pl.BlockSpec((N_EXPERTS, 128), const),
        ],
        out_specs=[
            pl.BlockSpec((tm, D_MODEL), tile), lane_tile, lane_tile,
            pl.BlockSpec((None, N_EXPERTS, 128), lambda i: (i, 0, 0)),
        ],
        out_shape=[
            jax.ShapeDtypeStruct((m, D_MODEL), F32),
            jax.ShapeDtypeStruct((TOP_K, m), I32),
            jax.ShapeDtypeStruct((TOP_K, m), F32),
            jax.ShapeDtypeStruct((m // tm, N_EXPERTS, 128), F32),
        ],
        compiler_params=_params("parallel"),
        name="merge_router",
    )(x2d, ya, yb, proj, proj, w_a, w_b, w_o, eg, eb, g1, b1, wr_hi, wr_lo, rbias)


def _front_kernel(x_ref, eg_ref, eb_ref, w_ref, ws_ref, lb_ref, gb_ref, wst_ref, gbt_ref,
                  hng_ref, s0_ref, cw_ref, cb_ref, mng_ref, c0_ref, m0_ref, x0_ref,
                  wa_ref, wb_ref, wo_ref, g1_ref, b1_ref, wrh_ref, wrl_ref, rb_ref,
                  h1_ref, slot_ref, gw_ref, cnt_ref, sfin_ref, cfin_ref, mfin_ref, xfin_ref,
                  p_scr, lf_scr, sg_scr, sgt_scr, ya_scr, yb_scr, s_scr, c_scr, m_scr, x_scr, *, chunk):
    c = pl.program_id(1)

    @pl.when(c == 0)
    def _():
        s_scr[...] = s0_ref[...]
        c_scr[...] = c0_ref[...]
        m_scr[...] = m0_ref[...]
        x_scr[...] = x0_ref[...]

    _inproj_kernel(x_ref, eg_ref, eb_ref, w_ref, ws_ref, lb_ref, gb_ref, p_scr, lf_scr, sg_scr)
    hb = _layer_norm(x_ref[...], eg_ref[...], eb_ref[...]).astype(BF16)
    st = _dot_nt(wst_ref[...], hb) + gbt_ref[:, 0:1]
    srow = lax.broadcasted_iota(I32, st.shape, 0)
    sgt_scr[...] = jnp.where(srow < ML_HEADS, st, _log_sigmoid(st))[:SUBLANES, :]

    _hgrn_body(p_scr.at[P_QA], p_scr.at[P_KA], p_scr.at[P_IA], p_scr.at[P_GA], lf_scr, hng_ref, ya_scr, s_scr, chunk)
    _mlstm_body(p_scr.at[P_QKB], p_scr.at[P_VB], p_scr.at[P_OB], sg_scr, sgt_scr, cw_ref, cb_ref, mng_ref,
                yb_scr, c_scr, m_scr, x_scr, chunk)
    _merge_kernel(x_ref, ya_scr, yb_scr, p_scr.at[P_MA], p_scr.at[P_MB], wa_ref, wb_ref, wo_ref, eg_ref, eb_ref,
                  g1_ref, b1_ref, wrh_ref, wrl_ref, rb_ref, h1_ref, slot_ref, gw_ref, cnt_ref, tm=chunk)

    @pl.when(c == pl.num_programs(1) - 1)
    def _():
        sfin_ref[...] = s_scr[...]
        cfin_ref[...] = c_scr[...]
        mfin_ref[...] = m_scr[...]
        xfin_ref[...] = x_scr[...]


def _front(x2d, weights, states, nb, chunk):
    m = x2d.shape[0]
    nc = m // nb // chunk
    s0, c0, m0, x0 = states
    rows = lambda b, c: (b * nc + c, 0)
    lanes = lambda b, c: (0, b * nc + c)
    const2 = lambda b, c: (0, 0)
    const3 = lambda b, c: (0, 0, 0)
    per_batch3 = lambda b, c: (b, 0, 0)
    per_batch4 = lambda b, c: (b, 0, 0, 0)

    def resident(a):
        idx = const2 if a.ndim == 2 else const3
        return pl.BlockSpec(a.shape, idx, pipeline_mode=pl.Buffered(1))

    outs = pl.pallas_call(
        functools.partial(_front_kernel, chunk=chunk),
        grid=(nb, nc),
        in_specs=[pl.BlockSpec((chunk, D_MODEL), rows)] + [resident(a) for a in weights[:8]]
        + [resident(weights[8]), resident(s0)] + [resident(a) for a in weights[9:12]]
        + [resident(c0), resident(m0), resident(x0)] + [resident(a) for a in weights[12:]],
        out_specs=[
            pl.BlockSpec((chunk, D_MODEL), rows),
            pl.BlockSpec((TOP_K, chunk), lanes),
            pl.BlockSpec((TOP_K, chunk), lanes),
            pl.BlockSpec((None, N_EXPERTS, 128), lambda b, c: (b * nc + c, 0, 0)),
            pl.BlockSpec((None, HG_HEADS, HG_DK, HG_DK), per_batch4),
            pl.BlockSpec((None, ML_HEADS, ML_DK, ML_AUG), per_batch4),
            pl.BlockSpec((None, SUBLANES, 128), per_batch3),
            pl.BlockSpec((None, SUBLANES, D_MODEL), per_batch3),
        ],
        out_shape=[
            jax.ShapeDtypeStruct((m, D_MODEL), F32),
            jax.ShapeDtypeStruct((TOP_K, m), I32),
            jax.ShapeDtypeStruct((TOP_K, m), F32),
            jax.ShapeDtypeStruct((m // chunk, N_EXPERTS, 128), F32),
            jax.ShapeDtypeStruct((nb, HG_HEADS, HG_DK, HG_DK), F32),
            jax.ShapeDtypeStruct((nb, ML_HEADS, ML_DK, ML_AUG), F32),
            jax.ShapeDtypeStruct((nb, SUBLANES, 128), F32),
            jax.ShapeDtypeStruct((nb, SUBLANES, D_MODEL), F32),
        ],
        scratch_shapes=[
            pltpu.VMEM((N_SLABS, chunk, D_MODEL), BF16),
            pltpu.VMEM((chunk, D_MODEL), F32),
            pltpu.VMEM((chunk, 128), F32),
            pltpu.VMEM((SUBLANES, chunk), F32),
            pltpu.VMEM((chunk, D_MODEL), BF16),
            pltpu.VMEM((chunk, D_MODEL), BF16),
            pltpu.VMEM((HG_HEADS, HG_DK, HG_DK), F32),
            pltpu.VMEM((ML_HEADS, ML_DK, ML_AUG), F32),
            pltpu.VMEM((SUBLANES, 128), F32),
            pltpu.VMEM((SUBLANES, D_MODEL), F32),
        ],
        compiler_params=_params("parallel", "arbitrary"),
        name="front",
    )(x2d, *weights[:9], s0, *weights[9:12], c0, m0, x0, *weights[12:])
    return outs[0], outs[1], outs[2], outs[3], tuple(outs[4:])


def _tile_slots(tt):
    return -(-(TOP_K * tt + N_EXPERTS * (GROUP_ROWS - 1)) // SLOT_BLOCK) * SLOT_BLOCK


def _dispatch_kernel(ng_ref, gdst_ref, slot_ref, h_ref, xs_ref, buf, sem, *, tt, s_tile):
    i = pl.program_id(0)
    cur = lax.rem(i, 2)

    def group_copy(b, g, d):
        src = buf.at[b, pl.ds(pl.multiple_of(g * GROUP_ROWS, GROUP_ROWS), GROUP_ROWS), :]
        dst = xs_ref.at[pl.ds(pl.multiple_of(d, GROUP_ROWS), GROUP_ROWS), :]
        return pltpu.make_async_copy(src, dst, sem.at[b])

    def wait_tile(j, b):
        lax.fori_loop(0, ng_ref[j], lambda g, c: (group_copy(b, 0, 0).wait(), c)[1], 0)

    @pl.when(i >= 2)
    def _():
        wait_tile(i - 2, cur)

    hb = h_ref[...].astype(BF16)
    sl = slot_ref[...]
    for r in range(s_tile // SLOT_BLOCK):
        s_iota = lax.broadcasted_iota(I32, (SLOT_BLOCK, tt), 0) + r * SLOT_BLOCK
        p = jnp.zeros((SLOT_BLOCK, tt), F32)
        for kk in range(TOP_K):
            p = jnp.where(s_iota == sl[kk:kk + 1, :], 1.0, p)
        buf[cur, r * SLOT_BLOCK:(r + 1) * SLOT_BLOCK, :] = _dot(p.astype(BF16), hb).astype(BF16)

    lax.fori_loop(0, ng_ref[i], lambda g, c: (group_copy(cur, g, gdst_ref[0, 0, g]).start(), c)[1], 0)

    @pl.when(i == pl.num_programs(0) - 1)
    def _():
        @pl.when(i >= 1)
        def _():
            wait_tile(i - 1, 1 - cur)
        wait_tile(i, cur)


def _dispatch(n_groups, gdst, slot_k, h1, n_slots, tt):
    m = h1.shape[0]
    s_tile = _tile_slots(tt)
    grid_spec = pltpu.PrefetchScalarGridSpec(
        num_scalar_prefetch=1,
        grid=(m // tt,),
        in_specs=[
            pl.BlockSpec((1, 1, s_tile // GROUP_ROWS), lambda i, ng: (i, 0, 0), memory_space=pltpu.SMEM),
            pl.BlockSpec((TOP_K, tt), lambda i, ng: (0, i)),
            pl.BlockSpec((tt, D_MODEL), lambda i, ng: (i, 0)),
        ],
        out_specs=pl.BlockSpec(memory_space=pl.ANY),
        scratch_shapes=[pltpu.VMEM((2, s_tile, D_MODEL), BF16), pltpu.SemaphoreType.DMA((2,))],
    )
    return pl.pallas_call(
        functools.partial(_dispatch_kernel, tt=tt, s_tile=s_tile),
        grid_spec=grid_spec,
        out_shape=jax.ShapeDtypeStruct((n_slots, D_MODEL), BF16),
        compiler_params=_params("arbitrary"),
        name="dispatch",
    )(n_groups, gdst, slot_k, h1)


def _experts_kernel(be_ref, nu_ref, x_ref, wg_ref, wu_ref, wd_ref, y_ref):
    del be_ref
    i = pl.program_id(0)

    @pl.when(i < nu_ref[0])
    def _():
        xb = x_ref[...]
        a = _dot(xb, wg_ref[...])
        u = _dot(xb, wu_ref[...])
        y_ref[...] = _dot((a * _sigmoid(a) * u).astype(BF16), wd_ref[...]).astype(BF16)


def _experts(blk_expert, n_used, xs, wg, wu, wd):
    n_slots = xs.shape[0]
    n_blocks = n_slots // MOE_BLOCK
    blk = lambda i, be, nu: (jnp.minimum(i, nu[0] - 1), 0)
    wsel = lambda i, be, nu: (be[jnp.minimum(i, nu[0] - 1)], 0, 0)
    grid_spec = pltpu.PrefetchScalarGridSpec(
        num_scalar_prefetch=2,
        grid=(n_blocks,),
        in_specs=[
            pl.BlockSpec((MOE_BLOCK, D_MODEL), blk),
            pl.BlockSpec((None, D_MODEL, D_EXPERT), wsel),
            pl.BlockSpec((None, D_MODEL, D_EXPERT), wsel),
            pl.BlockSpec((None, D_EXPERT, D_MODEL), wsel),
        ],
        out_specs=pl.BlockSpec((MOE_BLOCK, D_MODEL), blk),
    )
    return pl.pallas_call(
        _experts_kernel,
        grid_spec=grid_spec,
        out_shape=jax.ShapeDtypeStruct((n_slots, D_MODEL), BF16),
        compiler_params=_params("arbitrary"),
        name="experts",
    )(blk_expert, n_used, xs, wg, wu, wd)


def _combine_kernel(ng_ref, gcur_ref, gnext_ref, slot_ref, gw_ref, h_ref, y_ref, sg_ref, su_ref, sd_ref, g2_ref, b2_ref,
                    o_ref, ybuf, sem, *, tt, s_tile):
    i = pl.program_id(0)
    cur = lax.rem(i, 2)

    def group_copy(b, g, d):
        src = y_ref.at[pl.ds(pl.multiple_of(d, GROUP_ROWS), GROUP_ROWS), :]
        dst = ybuf.at[b, pl.ds(pl.multiple_of(g * GROUP_ROWS, GROUP_ROWS), GROUP_ROWS), :]
        return pltpu.make_async_copy(src, dst, sem.at[b])

    def fetch(j, b, table_ref):
        lax.fori_loop(0, ng_ref[j], lambda g, c: (group_copy(b, g, table_ref[0, 0, g]).start(), c)[1], 0)

    @pl.when(i == 0)
    def _():
        ybuf[...] = jnp.zeros_like(ybuf)
        fetch(0, 0, gcur_ref)

    @pl.when(i + 1 < pl.num_programs(0))
    def _():
        fetch(i + 1, 1 - cur, gnext_ref)

    h1 = h_ref[...]
    hb = h1.astype(BF16)
    a = _dot(hb, sg_ref[...])
    u = _dot(hb, su_ref[...])
    shared = _dot((a * _sigmoid(a) * u).astype(BF16), sd_ref[...])

    slot = slot_ref[...]
    gw = gw_ref[...]
    lax.fori_loop(0, ng_ref[i], lambda g, c: (group_copy(cur, 0, 0).wait(), c)[1], 0)
    routed = jnp.zeros((tt, D_MODEL), F32)
    for r in range(s_tile // SLOT_BLOCK):
        lane = lax.broadcasted_iota(I32, (tt, SLOT_BLOCK), 1) + r * SLOT_BLOCK
        p = jnp.zeros((tt, SLOT_BLOCK), F32)
        for kk in range(TOP_K):
            p = jnp.where(lane == slot[:, kk:kk + 1], gw[:, kk:kk + 1], p)
        routed = routed + _dot(p.astype(BF16), ybuf[cur, r * SLOT_BLOCK:(r + 1) * SLOT_BLOCK, :])
    o_ref[...] = _layer_norm(DN_ALPHA * h1 + (routed + shared), g2_ref[...], b2_ref[...])


def _combine(n_groups, gdst, slot_t, gw_t, h1, y, wsg, wsu, wsd, g2, b2, tt):
    m = h1.shape[0]
    nt = m // tt
    s_tile = _tile_slots(tt)
    const = lambda i, ng: (0, 0)
    table = lambda f: pl.BlockSpec((1, 1, s_tile // GROUP_ROWS), f, memory_space=pltpu.SMEM)
    grid_spec = pltpu.PrefetchScalarGridSpec(
        num_scalar_prefetch=1,
        grid=(nt,),
        in_specs=[
            table(lambda i, ng: (i, 0, 0)),
            table(lambda i, ng: (jnp.minimum(i + 1, nt - 1), 0, 0)),
            pl.BlockSpec((tt, TOP_K), lambda i, ng: (i, 0)),
            pl.BlockSpec((tt, TOP_K), lambda i, ng: (i, 0)),
            pl.BlockSpec((tt, D_MODEL), lambda i, ng: (i, 0)),
            pl.BlockSpec(memory_space=pl.ANY),
            pl.BlockSpec((D_MODEL, D_EXPERT), const),
            pl.BlockSpec((D_MODEL, D_EXPERT), const),
            pl.BlockSpec((D_EXPERT, D_MODEL), const),
            pl.BlockSpec((1, D_MODEL), const),
            pl.BlockSpec((1, D_MODEL), const),
        ],
        out_specs=pl.BlockSpec((tt, D_MODEL), lambda i, ng: (i, 0)),
        scratch_shapes=[pltpu.VMEM((2, s_tile, D_MODEL), BF16), pltpu.SemaphoreType.DMA((2,))],
    )
    return pl.pallas_call(
        functools.partial(_combine_kernel, tt=tt, s_tile=s_tile),
        grid_spec=grid_spec,
        out_shape=jax.ShapeDtypeStruct((m, D_MODEL), F32),
        compiler_params=_params("arbitrary"),
        name="combine",
    )(n_groups, gdst, gdst, slot_t, gw_t, h1, y, wsg, wsu, wsd, g2, b2)


def _pick_tile(m, pref):
    t = min(pref, m)
    while m % t:
        t //= 2
    return t


def _gates_time_on_lanes(sg, chunk):
    m = sg.shape[0]
    return sg[:, :SUBLANES].reshape(m // chunk, chunk, SUBLANES).transpose(0, 2, 1)


def _forward(x, meta_tokens, ln_emb_g, ln_emb_b, w_in, hg_lb_logits, hg_norm_g, ml_conv_w, ml_conv_b,
             ml_ig_bias, ml_fg_bias, ml_norm_g, w_branch_a, w_branch_b, w_out, ln1_g, ln1_b,
             w_router, router_bias, w_exp_gate, w_exp_up, w_exp_down, w_sh_gate, w_sh_up, w_sh_down,
             ln2_g, ln2_b, *, chunk, tm_proj, tm_moe):
    nb, seq, d = x.shape
    m = nb * seq
    row = lambda a: a.reshape(1, -1).astype(F32)

    w = w_in[0]
    kw = HG_HEADS * HG_DK
    o_qa, o_fa, o_ia, o_ga = 0, kw, 2 * kw, 3 * kw
    o_qb = 4 * kw
    o_kb = o_qb + ML_HEADS * ML_DK
    o_vb = o_kb + ML_HEADS * ML_DK
    o_ob = o_vb + ML_HEADS * ML_DV
    o_ig = o_ob + ML_HEADS * ML_DV
    o_fg = o_ig + ML_HEADS
    o_ma = o_fg + ML_HEADS
    o_mb = o_ma + D_MODEL
    cols = lambda o, n: w[:, o:o + n]
    w_cat = jnp.concatenate([
        cols(o_qa, kw), cols(o_fa, kw), cols(o_ia, kw), cols(o_ga, kw),
        cols(o_qb, 2 * ML_HEADS * ML_DK), cols(o_vb, ML_HEADS * ML_DV), cols(o_ob, ML_HEADS * ML_DV),
        cols(o_ma, D_MODEL), cols(o_mb, D_MODEL)], axis=1).astype(BF16)
    w_small = jnp.pad(cols(o_ig, 2 * ML_HEADS), ((0, 0), (0, 128 - 2 * ML_HEADS))).astype(BF16)
    gate_bias = jnp.pad(jnp.concatenate([ml_ig_bias[0], ml_fg_bias[0]]).astype(F32), (0, 128 - 2 * ML_HEADS)).reshape(1, 128)
    lb = jax.nn.softmax(hg_lb_logits.astype(F32), axis=0)[0].reshape(1, -1)
    eg, eb = row(ln_emb_g), row(ln_emb_b)
    conv_w = ml_conv_w[0].astype(F32)
    conv_b = row(ml_conv_b[0])
    hgn, mln = row(hg_norm_g[0]), row(ml_norm_g[0])

    w_small_t = jnp.pad(cols(o_ig, 2 * ML_HEADS).T, ((0, GROUP_ROWS - 2 * ML_HEADS), (0, 0))).astype(BF16)
    gate_bias_t = jnp.broadcast_to(jnp.pad(gate_bias[0, :2 * ML_HEADS], (0, GROUP_ROWS - 2 * ML_HEADS))[:, None],
                                   (GROUP_ROWS, 128))
    wr = w_router[0].T.astype(F32)
    wr_hi, wr_lo = _split_bf16(wr)
    rbias = jnp.broadcast_to(router_bias[0].astype(F32).reshape(N_EXPERTS, 1), (N_EXPERTS, 128))
    weights = [eg, eb, w_cat, w_small, lb, gate_bias, w_small_t, gate_bias_t, hgn, conv_w, conv_b, mln,
               w_branch_a[0].astype(BF16), w_branch_b[0].astype(BF16), w_out[0].astype(BF16),
               row(ln1_g[0]), row(ln1_b[0]), wr_hi, wr_lo, rbias]

    zero_states = (jnp.zeros((HG_HEADS, HG_DK, HG_DK), F32), jnp.zeros((ML_HEADS, ML_DK, ML_AUG), F32),
                   jnp.zeros((SUBLANES, 128), F32), jnp.zeros((SUBLANES, D_MODEL), F32))
    meta_states = _front(meta_tokens.astype(F32), weights, zero_states, 1, N_META)[4]

    x2d = x.reshape(m, d).astype(F32)
    tt = chunk
    h1, slot_k, gw, cnt, _ = _front(x2d, weights, tuple(s[0] for s in meta_states), nb, chunk)

    nt = m // tt
    s_tile = _tile_slots(tt)
    cnt8 = (cnt[:, :, 0].astype(I32) + GROUP_ROWS - 1) // GROUP_ROWS * GROUP_ROWS
    seg_end = jnp.cumsum(cnt8, axis=1)
    seg_off = seg_end - cnt8
    tile_rows = seg_end[:, -1]
    run = jnp.cumsum(cnt8, axis=0) - cnt8
    tot8 = jnp.sum(cnt8, axis=0)
    padded = (tot8 + MOE_BLOCK - 1) // MOE_BLOCK * MOE_BLOCK
    pends = jnp.cumsum(padded)
    gshift = (pends - padded)[None, :] + run - seg_off
    g_rows = jnp.arange(s_tile // GROUP_ROWS, dtype=I32) * GROUP_ROWS
    e_of_g = jnp.sum((seg_end[:, None, :] <= g_rows[None, :, None]).astype(I32), axis=-1)
    shift_g = jnp.sum(jnp.where(e_of_g[..., None] == jnp.arange(N_EXPERTS, dtype=I32), gshift[:, None, :], 0), axis=-1)
    gdst = jnp.where(g_rows[None, :] < tile_rows[:, None], g_rows[None, :] + shift_g, 0).astype(I32)
    gdst = gdst.reshape(nt, 1, s_tile // GROUP_ROWS)
    n_groups = (tile_rows // GROUP_ROWS).astype(I32)
    n_blocks = -(-(m * TOP_K + nt * N_EXPERTS * (GROUP_ROWS - 1)) // MOE_BLOCK) + N_EXPERTS
    blk_start = jnp.arange(n_blocks, dtype=I32) * MOE_BLOCK
    blk_expert = jnp.minimum(jnp.sum((pends[None, :] <= blk_start[:, None]).astype(I32), axis=1), N_EXPERTS - 1)
    n_used = (pends[-1:] // MOE_BLOCK).astype(I32)

    xs = _dispatch(n_groups, gdst, slot_k, h1, n_blocks * MOE_BLOCK, tt)
    y = _experts(blk_expert, n_used, xs, w_exp_gate[0].astype(BF16), w_exp_up[0].astype(BF16),
                 w_exp_down[0].astype(BF16))
    out = _combine(n_groups, gdst, slot_k.T, gw.T, h1, y, w_sh_gate[0].astype(BF16), w_sh_up[0].astype(BF16),
                   w_sh_down[0].astype(BF16), row(ln2_g[0]), row(ln2_b[0]), tt)
    return out.reshape(nb, seq, d).astype(x.dtype)


def kernel(x, meta_tokens, ln_emb_g, ln_emb_b, w_in, hg_lb_logits, hg_norm_g, ml_conv_w, ml_conv_b, ml_ig_bias, ml_fg_bias, ml_norm_g, w_branch_a, w_branch_b, w_out, ln1_g, ln1_b, w_router, router_bias, w_exp_gate, w_exp_up, w_exp_down, w_sh_gate, w_sh_up, w_sh_down, ln2_g, ln2_b):
    m = x.shape[0] * x.shape[1]
    return _forward(x, meta_tokens, ln_emb_g, ln_emb_b, w_in, hg_lb_logits, hg_norm_g, ml_conv_w, ml_conv_b,
                    ml_ig_bias, ml_fg_bias, ml_norm_g, w_branch_a, w_branch_b, w_out, ln1_g, ln1_b,
                    w_router, router_bias, w_exp_gate, w_exp_up, w_exp_down, w_sh_gate, w_sh_up, w_sh_down,
                    ln2_g, ln2_b, chunk=_pick_tile(x.shape[1], 256), tm_proj=_pick_tile(m, 512),
                    tm_moe=_pick_tile(m, 256))
```

```python
import functools

import jax
import jax.numpy as jnp
from jax import lax
from jax.experimental import pallas as pl
from jax.experimental.pallas import tpu as pltpu

F32, BF16, I32 = jnp.float32, jnp.bfloat16, jnp.int32

D_MODEL = 1024
N_META = 16
HG_HEADS = 8
HG_DK = 128
ML_HEADS = 4
ML_DK = 128
ML_DV = 256
ML_AUG = ML_DV + 128
N_EXPERTS = 64
TOP_K = 8
N_GROUPS = 8
GROUP_SIZE = N_EXPERTS // N_GROUPS
TOPK_GROUPS = 4
D_EXPERT = 256
ROUTED_SCALE = 2.5
MOE_BLOCK = 1024
SLOT_BLOCK = 256
DN_ALPHA = 2.0 ** 0.25
EPS = 1e-5
LOG2E = 1.4426950408889634
EXP2_CLAMP = 115.0
SUBLANES = 8
GROUP_ROWS = 16

P_QA, P_KA, P_IA, P_GA, P_QKB, P_VB, P_OB, P_MA, P_MB = range(9)
N_SLABS = 9

VMEM_LIMIT = 56 * 1024 * 1024


def _params(*sem):
    return pltpu.CompilerParams(dimension_semantics=sem, vmem_limit_bytes=VMEM_LIMIT)


def _sigmoid(x):
    return 1.0 / (1.0 + jnp.exp(-x))


def _log_sigmoid(x):
    return jnp.minimum(x, 0.0) - jnp.log(1.0 + jnp.exp(-jnp.abs(x)))


def _layer_norm(x, g, b):
    xc = x - jnp.mean(x, axis=-1, keepdims=True)
    var = jnp.mean(xc * xc, axis=-1, keepdims=True)
    return xc * lax.rsqrt(var + EPS) * g + b


def _dot(a, b):
    return jnp.dot(a, b, preferred_element_type=F32)


def _dot_nt(a, b):
    return lax.dot_general(a, b, (((1,), (1,)), ((), ())), preferred_element_type=F32)


def _dot_tn(a, b):
    return lax.dot_general(a, b, (((0,), (0,)), ((), ())), preferred_element_type=F32)


def _split_bf16(x):
    hi = x.astype(BF16)
    lo = (x - hi.astype(F32)).astype(BF16)
    return hi, lo


def _neg_abs(x):
    return lax.bitcast_convert_type(lax.bitcast_convert_type(x, I32) | jnp.int32(-2 ** 31), F32)


def _ones_where(cond):
    return jnp.where(cond, 1.0, 0.0).astype(BF16)


def _inproj_kernel(x_ref, g_ref, b_ref, w_ref, ws_ref, lb_ref, gb_ref, p_ref, lf_ref, sg_ref):
    hb = _layer_norm(x_ref[...], g_ref[...], b_ref[...]).astype(BF16)
    s = _dot(hb, ws_ref[...]) + gb_ref[...]
    lane = lax.broadcasted_iota(I32, s.shape, 1)
    sg_ref[...] = jnp.where(lane < ML_HEADS, s, _log_sigmoid(s))
    for n in range(N_SLABS):
        acc = _dot(hb, w_ref[:, n * D_MODEL:(n + 1) * D_MODEL])
        if n == P_KA:
            lb = lb_ref[...]
            f = lb + (1.0 - lb) * _sigmoid(acc)
            lf_ref[...] = jnp.log(f)
            acc = 1.0 - f
        elif n == P_GA:
            acc = acc * _sigmoid(acc)
        elif n >= P_OB:
            acc = _sigmoid(acc)
        p_ref[n] = acc.astype(BF16)


def _inproj(x2d, ln_g, ln_b, w_cat, w_small, lb, gate_bias, tm):
    m = x2d.shape[0]
    const = lambda i: (0, 0)
    resident = lambda shape: pl.BlockSpec(shape, const, pipeline_mode=pl.Buffered(1))
    return pl.pallas_call(
        _inproj_kernel,
        grid=(m // tm,),
        in_specs=[
            pl.BlockSpec((tm, D_MODEL), lambda i: (i, 0)),
            resident((1, D_MODEL)),
            resident((1, D_MODEL)),
            resident((D_MODEL, N_SLABS * D_MODEL)),
            resident((D_MODEL, 128)),
            resident((1, D_MODEL)),
            resident((1, 128)),
        ],
        out_specs=[
            pl.BlockSpec((N_SLABS, tm, D_MODEL), lambda i: (0, i, 0)),
            pl.BlockSpec((tm, D_MODEL), lambda i: (i, 0)),
            pl.BlockSpec((tm, 128), lambda i: (i, 0)),
        ],
        out_shape=[
            jax.ShapeDtypeStruct((N_SLABS, m, D_MODEL), BF16),
            jax.ShapeDtypeStruct((m, D_MODEL), F32),
            jax.ShapeDtypeStruct((m, 128), F32),
        ],
        compiler_params=_params("parallel"),
        name="inproj",
    )(x2d, ln_g, ln_b, w_cat, w_small, lb, gate_bias)


def _block_rows(b, block, pick):
    c, w = b.shape
    parts = [jnp.broadcast_to(b[j * block + pick:j * block + pick + 1, :], (block, w))
             for j in range(c // block)]
    return parts[0] if len(parts) == 1 else jnp.concatenate(parts, axis=0)


def _hgrn_body(q_ref, k_ref, v_ref, g_ref, lf_ref, ng_ref, y_ref, s_scr, cs):
    row = lax.broadcasted_iota(I32, (cs, cs), 0)
    col = lax.broadcasted_iota(I32, (cs, cs), 1)
    tri = _ones_where(col <= row)
    lf_hi, lf_lo = _split_bf16(lf_ref[...])
    b = (_dot(tri, lf_hi) + _dot(tri, lf_lo)) * LOG2E
    q = q_ref[...]
    k = k_ref[...]
    v = v_ref[...]
    blast = b[cs - 1:cs, :]
    qg = q * jnp.exp2(b).astype(BF16)
    kg = k * jnp.exp2(blast - b).astype(BF16)
    dec = jnp.exp2(blast)

    levels = []
    m = SUBLANES
    while 2 * m <= cs:
        w = jnp.exp2(_neg_abs(b - _block_rows(b, 2 * m, m - 1))).astype(BF16)
        sh = (2 * m).bit_length() - 1
        mask = ((row >> sh) == (col >> sh)) & ((row & (2 * m - 1)) >= m) & ((col & (2 * m - 1)) < m)
        levels.append((q * w, k * w, mask))
        m *= 2
    e = jnp.clip(b - _block_rows(b, SUBLANES, SUBLANES // 2 - 1), -EXP2_CLAMP, EXP2_CLAMP)
    levels.append((q * jnp.exp2(e).astype(BF16), k * jnp.exp2(-e).astype(BF16),
                   ((row >> 3) == (col >> 3)) & (col <= row)))

    ng = ng_ref[...]
    for h in range(HG_HEADS):
        sl = slice(h * HG_DK, (h + 1) * HG_DK)
        st = s_scr[h]
        o = _dot_nt(qg[:, sl], st.astype(BF16))
        sc = jnp.zeros((cs, cs), F32)
        for lq, lk, mask in levels:
            sc = jnp.where(mask, _dot_nt(lq[:, sl], lk[:, sl]), sc)
        o = o + _dot(sc.astype(BF16), v[:, sl])
        s_scr[h] = dec[:, sl] * st + _dot_tn(v[:, sl], kg[:, sl])
        ms = jnp.mean(o * o, axis=-1, keepdims=True)
        y = o * lax.rsqrt(ms + EPS) * ng[:, sl] * g_ref[:, sl].astype(F32)
        y_ref[:, sl] = y.astype(BF16)


def _mlstm_body(qk_ref, v_ref, og_ref, sg_ref, sgt_ref, cw_ref, cb_ref, ng_ref, y_ref, c_scr, m_scr, x_scr, cs):
    x = qk_ref[...].astype(F32)
    prev = x_scr[...]
    sub = lax.broadcasted_iota(I32, (SUBLANES, D_MODEL), 0)
    cw = cw_ref[...]
    conv = cw[3:4, :] * x + cb_ref[...]
    for j in (1, 2, 3):
        xs = pltpu.roll(x, j, 0)
        head = jnp.where(sub < j, pltpu.roll(prev, j, 0), xs[:SUBLANES, :])
        xs = jnp.concatenate([head, xs[SUBLANES:, :]], axis=0)
        conv = conv + cw[3 - j:4 - j, :] * xs
    x_scr[...] = x[cs - SUBLANES:, :]
    qk = conv * _sigmoid(conv)
    q_all = (qk[:, :ML_HEADS * ML_DK] * (ML_DK ** -0.5)).astype(BF16)
    k_all = qk[:, ML_HEADS * ML_DK:]

    row = lax.broadcasted_iota(I32, (cs, cs), 0)
    col = lax.broadcasted_iota(I32, (cs, cs), 1)
    causal = col <= row
    tri = _ones_where(causal)
    sg = sg_ref[...]
    sgt = sgt_ref[...]
    sg_hi, sg_lo = _split_bf16(sg)
    bcol_all = _dot(tri, sg_hi) + _dot(tri, sg_lo)
    sgt_hi, sgt_lo = _split_bf16(sgt)
    brow_all = _dot_nt(sgt_hi, tri) + _dot_nt(sgt_lo, tri)
    lane128 = lax.broadcasted_iota(I32, (cs, 128), 1)
    ones_col = _ones_where(lane128 == 0)
    v = v_ref[...]
    ng = ng_ref[...]

    for h in range(ML_HEADS):
        b_col = bcol_all[:, ML_HEADS + h:ML_HEADS + h + 1]
        b_row = brow_all[ML_HEADS + h:ML_HEADS + h + 1, :]
        ig_col = sg[:, h:h + 1]
        ig_row = sgt[h:h + 1, :]
        m_prev = m_scr[h:h + 1, 0:1]
        q_h = q_all[:, h * ML_DK:(h + 1) * ML_DK]
        k_h = k_all[:, h * ML_DK:(h + 1) * ML_DK]
        v_aug = jnp.concatenate([v[:, h * ML_DV:(h + 1) * ML_DV], ones_col], axis=1)
        c_st = c_scr[h]

        log_intra = jnp.where(causal, b_col - b_row + ig_row, -jnp.inf)
        log_inter = b_col + m_prev
        m_t = jnp.maximum(log_inter, jnp.max(log_intra, axis=-1, keepdims=True))
        w_intra = jnp.exp(log_intra - m_t)
        w_inter = jnp.exp(log_inter - m_t)
        s = _dot_nt(q_h, k_h.astype(BF16)) * w_intra
        tot = w_inter * _dot(q_h, c_st.astype(BF16)) + _dot(s.astype(BF16), v_aug)
        num = tot[:, :ML_DV]
        den = tot[:, ML_DV:ML_DV + 1]
        hid = num / jnp.maximum(jnp.abs(den), jnp.exp(-m_t))

        b_last = b_col[cs - 1:cs, :]
        log_w = b_last - b_col + ig_col
        m_new = jnp.maximum(b_last + m_prev, jnp.max(log_w, axis=0, keepdims=True))
        w_s = jnp.exp(log_w - m_new)
        decay = jnp.exp(b_last + m_prev - m_new)
        c_scr[h] = decay * c_st + _dot_tn((k_h * w_s).astype(BF16), v_aug)
        m_scr[h:h + 1, :] = jnp.broadcast_to(m_new, (1, 128))

        hc = hid - jnp.mean(hid, axis=-1, keepdims=True)
        var = jnp.mean(hc * hc, axis=-1, keepdims=True)
        sl = slice(h * ML_DV, (h + 1) * ML_DV)
        y = hc * lax.rsqrt(var + EPS) * ng[:, sl] * og_ref[:, sl].astype(F32)
        y_ref[:, sl] = y.astype(BF16)


def _mixer_kernel(q_ref, k_ref, v_ref, g_ref, lf_ref, hng_ref, s0_ref,
                  qk_ref, vb_ref, og_ref, sg_ref, sgt_ref, cw_ref, cb_ref, mng_ref, c0_ref, m0_ref, x0_ref,
                  ya_ref, yb_ref, sfin_ref, cfin_ref, mfin_ref, xfin_ref,
                  s_scr, c_scr, m_scr, x_scr, *, chunk):
    c = pl.program_id(1)

    @pl.when(c == 0)
    def _():
        s_scr[...] = s0_ref[...]
        c_scr[...] = c0_ref[...]
        m_scr[...] = m0_ref[...]
        x_scr[...] = x0_ref[...]

    _hgrn_body(q_ref, k_ref, v_ref, g_ref, lf_ref, hng_ref, ya_ref, s_scr, chunk)
    _mlstm_body(qk_ref, vb_ref, og_ref, sg_ref, sgt_ref, cw_ref, cb_ref, mng_ref, yb_ref, c_scr, m_scr, x_scr, chunk)

    @pl.when(c == pl.num_programs(1) - 1)
    def _():
        sfin_ref[...] = s_scr[...]
        cfin_ref[...] = c_scr[...]
        mfin_ref[...] = m_scr[...]
        xfin_ref[...] = x_scr[...]


def _mixer(proj, logf, sg, sgt, hg_norm, conv_w, conv_b, ml_norm, states, nb, chunk):
    m = logf.shape[0]
    nc = m // nb // chunk
    s0, c0, m0, x0 = states
    rows = lambda b, c: (b * nc + c, 0)
    slab = lambda s: pl.BlockSpec((None, chunk, D_MODEL), lambda b, c, s=s: (s, b * nc + c, 0))
    const2 = lambda b, c: (0, 0)
    const3 = lambda b, c: (0, 0, 0)
    per_batch3 = lambda b, c: (b, 0, 0)
    per_batch4 = lambda b, c: (b, 0, 0, 0)
    vec = pl.BlockSpec((1, D_MODEL), const2)
    outs = pl.pallas_call(
        functools.partial(_mixer_kernel, chunk=chunk),
        grid=(nb, nc),
        in_specs=[
            slab(P_QA), slab(P_KA), slab(P_IA), slab(P_GA),
            pl.BlockSpec((chunk, D_MODEL), rows), vec,
            pl.BlockSpec((HG_HEADS, HG_DK, HG_DK), const3),
            slab(P_QKB), slab(P_VB), slab(P_OB),
            pl.BlockSpec((chunk, 128), rows),
            pl.BlockSpec((None, SUBLANES, chunk), lambda b, c: (b * nc + c, 0, 0)),
            pl.BlockSpec((4, D_MODEL), const2), vec, vec,
            pl.BlockSpec((ML_HEADS, ML_DK, ML_AUG), const3),
            pl.BlockSpec((SUBLANES, 128), const2),
            pl.BlockSpec((SUBLANES, D_MODEL), const2),
        ],
        out_specs=[
            pl.BlockSpec((chunk, D_MODEL), rows),
            pl.BlockSpec((chunk, D_MODEL), rows),
            pl.BlockSpec((None, HG_HEADS, HG_DK, HG_DK), per_batch4),
            pl.BlockSpec((None, ML_HEADS, ML_DK, ML_AUG), per_batch4),
            pl.BlockSpec((None, SUBLANES, 128), per_batch3),
            pl.BlockSpec((None, SUBLANES, D_MODEL), per_batch3),
        ],
        out_shape=[
            jax.ShapeDtypeStruct((m, D_MODEL), BF16),
            jax.ShapeDtypeStruct((m, D_MODEL), BF16),
            jax.ShapeDtypeStruct((nb, HG_HEADS, HG_DK, HG_DK), F32),
            jax.ShapeDtypeStruct((nb, ML_HEADS, ML_DK, ML_AUG), F32),
            jax.ShapeDtypeStruct((nb, SUBLANES, 128), F32),
            jax.ShapeDtypeStruct((nb, SUBLANES, D_MODEL), F32),
        ],
        scratch_shapes=[
            pltpu.VMEM((HG_HEADS, HG_DK, HG_DK), F32),
            pltpu.VMEM((ML_HEADS, ML_DK, ML_AUG), F32),
            pltpu.VMEM((SUBLANES, 128), F32),
            pltpu.VMEM((SUBLANES, D_MODEL), F32),
        ],
        compiler_params=_params("parallel", "arbitrary"),
        name="mixer",
    )(proj, proj, proj, proj, logf, hg_norm, s0, proj, proj, proj, sg, sgt, conv_w, conv_b, ml_norm, c0, m0, x0)
    return outs[0], outs[1], tuple(outs[2:])


def _merge_kernel(x_ref, ya_ref, yb_ref, ma_ref, mb_ref, wa_ref, wb_ref, wo_ref, eg_ref, eb_ref, g1_ref, b1_ref,
                  wrh_ref, wrl_ref, rb_ref,
                  h1_ref, slot_ref, gw_ref, cnt_ref, *, tm):
    h0 = _layer_norm(x_ref[...], eg_ref[...], eb_ref[...])
    merged = (ma_ref[...].astype(F32) * _dot(ya_ref[...], wa_ref[...])
              + mb_ref[...].astype(F32) * _dot(yb_ref[...], wb_ref[...]))
    mix = _dot(merged.astype(BF16), wo_ref[...])
    h1 = _layer_norm(DN_ALPHA * h0 + mix, g1_ref[...], b1_ref[...])
    h1_ref[...] = h1

    h_hi, h_lo = _split_bf16(h1)
    logits = _dot_nt(wrh_ref[...], h_hi) + _dot_nt(wrh_ref[...], h_lo) + _dot_nt(wrl_ref[...], h_hi)
    scores = _sigmoid(logits)
    biased = scores + rb_ref[:, 0:1]
    neg_inf = -jnp.inf

    g3 = biased.reshape(N_GROUPS, GROUP_SIZE, tm)
    sub3 = lax.broadcasted_iota(I32, g3.shape, 1)
    top1 = jnp.max(g3, axis=1, keepdims=True)
    first = jnp.min(jnp.where(g3 == top1, sub3, GROUP_SIZE), axis=1, keepdims=True)
    top2 = jnp.max(jnp.where(sub3 == first, neg_inf, g3), axis=1, keepdims=True)
    gs = (top1 + top2).reshape(N_GROUPS, tm)
    gi = lax.broadcasted_iota(I32, gs.shape, 0)
    grank = jnp.zeros(gs.shape, F32)
    for j in range(N_GROUPS):
        r = gs[j:j + 1, :]
        grank = grank + jnp.where((r > gs) | ((r == gs) & (gi > j)), 1.0, 0.0)
    gsel = grank < float(TOPK_GROUPS)
    emask = jnp.broadcast_to(gsel.reshape(N_GROUPS, 1, tm), (N_GROUPS, GROUP_SIZE, tm)).reshape(N_EXPERTS, tm)
    masked = jnp.where(emask, biased, neg_inf)

    ei = lax.broadcasted_iota(I32, masked.shape, 0)
    work = masked
    rank = jnp.full(masked.shape, float(N_EXPERTS), F32)
    for kk in range(TOP_K):
        top = jnp.max(work, axis=0, keepdims=True)
        first = jnp.min(jnp.where(work == top, ei, N_EXPERTS), axis=0, keepdims=True)
        hit = ei == first
        rank = jnp.where(hit, float(kk), rank)
        work = jnp.where(hit, neg_inf, work)
    sel = rank < float(TOP_K)
    sel_w = jnp.where(sel, scores, 0.0)
    gwd = sel_w / jnp.sum(sel_w, axis=0, keepdims=True) * ROUTED_SCALE

    tr = lax.broadcasted_iota(I32, (tm, tm), 0)
    tc = lax.broadcasted_iota(I32, (tm, tm), 1)
    sel_b = _ones_where(sel)
    rloc = _dot(sel_b, _ones_where(tr < tc))
    cnt = _dot(sel_b, jnp.ones((tm, 128), BF16))
    cnt_g = jnp.floor((cnt + (GROUP_ROWS - 1.0)) * (1.0 / GROUP_ROWS)) * GROUP_ROWS
    er = lax.broadcasted_iota(I32, (N_EXPERTS, N_EXPERTS), 0)
    ec = lax.broadcasted_iota(I32, (N_EXPERTS, N_EXPERTS), 1)
    seg_start = _dot(_ones_where(ec < er), cnt_g.astype(BF16))
    slot_e = seg_start[:, 0:1] + rloc
    cnt_ref[...] = cnt

    s_rows, w_rows = [], []
    for kk in range(TOP_K):
        pick = sel & (rank == float(kk))
        s_rows.append(jnp.sum(jnp.where(pick, slot_e, 0.0), axis=0, keepdims=True))
        w_rows.append(jnp.sum(jnp.where(pick, gwd, 0.0), axis=0, keepdims=True))
    slot_ref[...] = jnp.concatenate(s_rows, axis=0).astype(I32)
    gw_ref[...] = jnp.concatenate(w_rows, axis=0)


def _merge(x2d, ya, yb, proj, w_a, w_b, w_o, eg, eb, g1, b1, wr_hi, wr_lo, rbias, tm):
    m = x2d.shape[0]
    tile = lambda i: (i, 0)
    const = lambda i: (0, 0)
    slab = lambda s: pl.BlockSpec((None, tm, D_MODEL), lambda i, s=s: (s, i, 0))
    wspec = pl.BlockSpec((D_MODEL, D_MODEL), const)
    vspec = pl.BlockSpec((1, D_MODEL), const)
    lane_tile = pl.BlockSpec((TOP_K, tm), lambda i: (0, i))
    return pl.pallas_call(
        functools.partial(_merge_kernel, tm=tm),
        grid=(m // tm,),
        in_specs=[
            pl.BlockSpec((tm, D_MODEL), tile), pl.BlockSpec((tm, D_MODEL), tile), pl.BlockSpec((tm, D_MODEL), tile),
            slab(P_MA), slab(P_MB), wspec, wspec, wspec, vspec, vspec, vspec, vspec,
            pl.BlockSpec((N_EXPERTS, D_MODEL), const), pl.BlockSpec((N_EXPERTS, D_MODEL), const),
            pl.BlockSpec((N_EXPERTS, 128), const),
        ],
        out_specs=[
            pl.BlockSpec((tm, D_MODEL), tile), lane_tile, lane_tile,
            pl.BlockSpec((None, N_EXPERTS, 128), lambda i: (i, 0, 0)),
        ],
        out_shape=[
            jax.ShapeDtypeStruct((m, D_MODEL), F32),
            jax.ShapeDtypeStruct((TOP_K, m), I32),
            jax.ShapeDtypeStruct((TOP_K, m), F32),
            jax.ShapeDtypeStruct((m // tm, N_EXPERTS, 128), F32),
        ],
        compiler_params=_params("parallel"),
        name="merge_router",
    )(x2d, ya, yb, proj, proj, w_a, w_b, w_o, eg, eb, g1, b1, wr_hi, wr_lo, rbias)


def _front_kernel(x_ref, eg_ref, eb_ref, w_ref, ws_ref, lb_ref, gb_ref, wst_ref, gbt_ref,
                  hng_ref, s0_ref, cw_ref, cb_ref, mng_ref, c0_ref, m0_ref, x0_ref,
                  wa_ref, wb_ref, wo_ref, g1_ref, b1_ref, wrh_ref, wrl_ref, rb_ref,
                  h1_ref, slot_ref, gw_ref, cnt_ref, sfin_ref, cfin_ref, mfin_ref, xfin_ref,
                  p_scr, lf_scr, sg_scr, sgt_scr, ya_scr, yb_scr, s_scr, c_scr, m_scr, x_scr, *, chunk):
    c = pl.program_id(1)

    @pl.when(c == 0)
    def _():
        s_scr[...] = s0_ref[...]
        c_scr[...] = c0_ref[...]
        m_scr[...] = m0_ref[...]
        x_scr[...] = x0_ref[...]

    _inproj_kernel(x_ref, eg_ref, eb_ref, w_ref, ws_ref, lb_ref, gb_ref, p_scr, lf_scr, sg_scr)
    hb = _layer_norm(x_ref[...], eg_ref[...], eb_ref[...]).astype(BF16)
    st = _dot_nt(wst_ref[...], hb) + gbt_ref[:, 0:1]
    srow = lax.broadcasted_iota(I32, st.shape, 0)
    sgt_scr[...] = jnp.where(srow < ML_HEADS, st, _log_sigmoid(st))[:SUBLANES, :]

    _hgrn_body(p_scr.at[P_QA], p_scr.at[P_KA], p_scr.at[P_IA], p_scr.at[P_GA], lf_scr, hng_ref, ya_scr, s_scr, chunk)
    _mlstm_body(p_scr.at[P_QKB], p_scr.at[P_VB], p_scr.at[P_OB], sg_scr, sgt_scr, cw_ref, cb_ref, mng_ref,
                yb_scr, c_scr, m_scr, x_scr, chunk)
    _merge_kernel(x_ref, ya_scr, yb_scr, p_scr.at[P_MA], p_scr.at[P_MB], wa_ref, wb_ref, wo_ref, eg_ref, eb_ref,
                  g1_ref, b1_ref, wrh_ref, wrl_ref, rb_ref, h1_ref, slot_ref, gw_ref, cnt_ref, tm=chunk)

    @pl.when(c == pl.num_programs(1) - 1)
    def _():
        sfin_ref[...] = s_scr[...]
        cfin_ref[...] = c_scr[...]
        mfin_ref[...] = m_scr[...]
        xfin_ref[...] = x_scr[...]


def _front(x2d, weights, states, nb, chunk):
    m = x2d.shape[0]
    nc = m // nb // chunk
    s0, c0, m0, x0 = states
    rows = lambda b, c: (b * nc + c, 0)
    lanes = lambda b, c: (0, b * nc + c)
    const2 = lambda b, c: (0, 0)
    const3 = lambda b, c: (0, 0, 0)
    per_batch3 = lambda b, c: (b, 0, 0)
    per_batch4 = lambda b, c: (b, 0, 0, 0)

    def resident(a):
        idx = const2 if a.ndim == 2 else const3
        return pl.BlockSpec(a.shape, idx, pipeline_mode=pl.Buffered(1))

    outs = pl.pallas_call(
        functools.partial(_front_kernel, chunk=chunk),
        grid=(nb, nc),
        in_specs=[pl.BlockSpec((chunk, D_MODEL), rows)] + [resident(a) for a in weights[:8]]
        + [resident(weights[8]), resident(s0)] + [resident(a) for a in weights[9:12]]
        + [resident(c0), resident(m0), resident(x0)] + [resident(a) for a in weights[12:]],
        out_specs=[
            pl.BlockSpec((chunk, D_MODEL), rows),
            pl.BlockSpec((TOP_K, chunk), lanes),
            pl.BlockSpec((TOP_K, chunk), lanes),
            pl.BlockSpec((None, N_EXPERTS, 128), lambda b, c: (b * nc + c, 0, 0)),
            pl.BlockSpec((None, HG_HEADS, HG_DK, HG_DK), per_batch4),
            pl.BlockSpec((None, ML_HEADS, ML_DK, ML_AUG), per_batch4),
            pl.BlockSpec((None, SUBLANES, 128), per_batch3),
            pl.BlockSpec((None, SUBLANES, D_MODEL), per_batch3),
        ],
        out_shape=[
            jax.ShapeDtypeStruct((m, D_MODEL), F32),
            jax.ShapeDtypeStruct((TOP_K, m), I32),
            jax.ShapeDtypeStruct((TOP_K, m), F32),
            jax.ShapeDtypeStruct((m // chunk, N_EXPERTS, 128), F32),
            jax.ShapeDtypeStruct((nb, HG_HEADS, HG_DK, HG_DK), F32),
            jax.ShapeDtypeStruct((nb, ML_HEADS, ML_DK, ML_AUG), F32),
            jax.ShapeDtypeStruct((nb, SUBLANES, 128), F32),
            jax.ShapeDtypeStruct((nb, SUBLANES, D_MODEL), F32),
        ],
        scratch_shapes=[
            pltpu.VMEM((N_SLABS, chunk, D_MODEL), BF16),
            pltpu.VMEM((chunk, D_MODEL), F32),
            pltpu.VMEM((chunk, 128), F32),
            pltpu.VMEM((SUBLANES, chunk), F32),
            pltpu.VMEM((chunk, D_MODEL), BF16),
            pltpu.VMEM((chunk, D_MODEL), BF16),
            pltpu.VMEM((HG_HEADS, HG_DK, HG_DK), F32),
            pltpu.VMEM((ML_HEADS, ML_DK, ML_AUG), F32),
            pltpu.VMEM((SUBLANES, 128), F32),
            pltpu.VMEM((SUBLANES, D_MODEL), F32),
        ],
        compiler_params=_params("parallel", "arbitrary"),
        name="front",
    )(x2d, *weights[:9], s0, *weights[9:12], c0, m0, x0, *weights[12:])
    return outs[0], outs[1], outs[2], outs[3], tuple(outs[4:])


def _tile_slots(tt):
    return -(-(TOP_K * tt + N_EXPERTS * (GROUP_ROWS - 1)) // SLOT_BLOCK) * SLOT_BLOCK


def _wait_groups(n, make_copy, s_tile):
    p = 1 << ((s_tile // GROUP_ROWS).bit_length() - 1)
    while p:
        @pl.when((n & p) != 0)
        def _():
            make_copy(p * GROUP_ROWS).wait()
        p >>= 1


def _for_slot_blocks(n, tt, s_tile, body):
    always = TOP_K * tt // SLOT_BLOCK
    for r in range(always):
        body(r)
    for r in range(always, s_tile // SLOT_BLOCK):
        @pl.when(n * GROUP_ROWS > r * SLOT_BLOCK)
        def _():
            body(r)


def _dispatch_kernel(ng_ref, gdst_ref, slot_ref, h_ref, xs_ref, buf, sem, *, tt, s_tile):
    i = pl.program_id(0)
    cur = lax.rem(i, 2)

    def group_copy(b, g, d):
        src = buf.at[b, pl.ds(pl.multiple_of(g * GROUP_ROWS, GROUP_ROWS), GROUP_ROWS), :]
        dst = xs_ref.at[pl.ds(pl.multiple_of(d, GROUP_ROWS), GROUP_ROWS), :]
        return pltpu.make_async_copy(src, dst, sem.at[b])

    def wait_tile(j, b):
        _wait_groups(ng_ref[j], lambda rows: pltpu.make_async_copy(
            buf.at[b, pl.ds(0, rows), :], xs_ref.at[pl.ds(0, rows), :], sem.at[b]), s_tile)

    @pl.when(i >= 2)
    def _():
        wait_tile(i - 2, cur)

    hb = h_ref[...].astype(BF16)
    sl = slot_ref[...]

    def fill_block(r):
        s_iota = lax.broadcasted_iota(I32, (SLOT_BLOCK, tt), 0) + r * SLOT_BLOCK
        p = jnp.zeros((SLOT_BLOCK, tt), F32)
        for kk in range(TOP_K):
            p = jnp.where(s_iota == sl[kk:kk + 1, :], 1.0, p)
        buf[cur, r * SLOT_BLOCK:(r + 1) * SLOT_BLOCK, :] = _dot(p.astype(BF16), hb).astype(BF16)

    _for_slot_blocks(ng_ref[i], tt, s_tile, fill_block)

    lax.fori_loop(0, ng_ref[i], lambda g, c: (group_copy(cur, g, gdst_ref[0, 0, g]).start(), c)[1], 0)

    @pl.when(i == pl.num_programs(0) - 1)
    def _():
        @pl.when(i >= 1)
        def _():
            wait_tile(i - 1, 1 - cur)
        wait_tile(i, cur)


def _dispatch(n_groups, gdst, slot_k, h1, n_slots, tt):
    m = h1.shape[0]
    s_tile = _tile_slots(tt)
    grid_spec = pltpu.PrefetchScalarGridSpec(
        num_scalar_prefetch=1,
        grid=(m // tt,),
        in_specs=[
            pl.BlockSpec((1, 1, s_tile // GROUP_ROWS), lambda i, ng: (i, 0, 0), memory_space=pltpu.SMEM),
            pl.BlockSpec((TOP_K, tt), lambda i, ng: (0, i)),
            pl.BlockSpec((tt, D_MODEL), lambda i, ng: (i, 0)),
        ],
        out_specs=pl.BlockSpec(memory_space=pl.ANY),
        scratch_shapes=[pltpu.VMEM((2, s_tile, D_MODEL), BF16), pltpu.SemaphoreType.DMA((2,))],
    )
    return pl.pallas_call(
        functools.partial(_dispatch_kernel, tt=tt, s_tile=s_tile),
        grid_spec=grid_spec,
        out_shape=jax.ShapeDtypeStruct((n_slots, D_MODEL), BF16),
        compiler_params=_params("arbitrary"),
        name="dispatch",
    )(n_groups, gdst, slot_k, h1)


def _experts_kernel(be_ref, nu_ref, x_ref, wg_ref, wu_ref, wd_ref, y_ref):
    del be_ref
    i = pl.program_id(0)

    @pl.when(i < nu_ref[0])
    def _():
        xb = x_ref[...]
        a = _dot(xb, wg_ref[...])
        u = _dot(xb, wu_ref[...])
        y_ref[...] = _dot((a * _sigmoid(a) * u).astype(BF16), wd_ref[...]).astype(BF16)


def _experts(blk_expert, n_used, xs, wg, wu, wd):
    n_slots = xs.shape[0]
    n_blocks = n_slots // MOE_BLOCK
    blk = lambda i, be, nu: (jnp.minimum(i, nu[0] - 1), 0)
    wsel = lambda i, be, nu: (be[jnp.minimum(i, nu[0] - 1)], 0, 0)
    grid_spec = pltpu.PrefetchScalarGridSpec(
        num_scalar_prefetch=2,
        grid=(n_blocks,),
        in_specs=[
            pl.BlockSpec((MOE_BLOCK, D_MODEL), blk),
            pl.BlockSpec((None, D_MODEL, D_EXPERT), wsel),
            pl.BlockSpec((None, D_MODEL, D_EXPERT), wsel),
            pl.BlockSpec((None, D_EXPERT, D_MODEL), wsel),
        ],
        out_specs=pl.BlockSpec((MOE_BLOCK, D_MODEL), blk),
    )
    return pl.pallas_call(
        _experts_kernel,
        grid_spec=grid_spec,
        out_shape=jax.ShapeDtypeStruct((n_slots, D_MODEL), BF16),
        compiler_params=_params("arbitrary"),
        name="experts",
    )(blk_expert, n_used, xs, wg, wu, wd)


def _combine_kernel(ng_ref, gcur_ref, gnext_ref, slot_ref, gw_ref, h_ref, y_ref, sg_ref, su_ref, sd_ref, g2_ref, b2_ref,
                    o_ref, ybuf, acc, sem, *, tt, s_tile):
    i = pl.program_id(0)
    cur = lax.rem(i, 2)

    def group_copy(b, g, d):
        src = y_ref.at[pl.ds(pl.multiple_of(d, GROUP_ROWS), GROUP_ROWS), :]
        dst = ybuf.at[b, pl.ds(pl.multiple_of(g * GROUP_ROWS, GROUP_ROWS), GROUP_ROWS), :]
        return pltpu.make_async_copy(src, dst, sem.at[b])

    def fetch(j, b, table_ref):
        lax.fori_loop(0, ng_ref[j], lambda g, c: (group_copy(b, g, table_ref[0, 0, g]).start(), c)[1], 0)

    @pl.when(i == 0)
    def _():
        ybuf[...] = jnp.zeros_like(ybuf)
        fetch(0, 0, gcur_ref)

    @pl.when(i + 1 < pl.num_programs(0))
    def _():
        fetch(i + 1, 1 - cur, gnext_ref)

    h1 = h_ref[...]
    hb = h1.astype(BF16)
    a = _dot(hb, sg_ref[...])
    u = _dot(hb, su_ref[...])
    shared = _dot((a * _sigmoid(a) * u).astype(BF16), sd_ref[...])

    slot = slot_ref[...]
    gw = gw_ref[...]
    _wait_groups(ng_ref[i], lambda rows: pltpu.make_async_copy(
        y_ref.at[pl.ds(0, rows), :], ybuf.at[cur, pl.ds(0, rows), :], sem.at[cur]), s_tile)

    def block_dot(r):
        lane = lax.broadcasted_iota(I32, (tt, SLOT_BLOCK), 1) + r * SLOT_BLOCK
        p = jnp.zeros((tt, SLOT_BLOCK), F32)
        for kk in range(TOP_K):
            p = jnp.where(lane == slot[:, kk:kk + 1], gw[:, kk:kk + 1], p)
        return _dot(p.astype(BF16), ybuf[cur, r * SLOT_BLOCK:(r + 1) * SLOT_BLOCK, :])

    always = TOP_K * tt // SLOT_BLOCK
    routed = block_dot(0)
    for r in range(1, always):
        routed = routed + block_dot(r)
    acc[...] = DN_ALPHA * h1 + (routed + shared)
    for r in range(always, s_tile // SLOT_BLOCK):
        @pl.when(ng_ref[i] * GROUP_ROWS > r * SLOT_BLOCK)
        def _():
            acc[...] += block_dot(r)
    o_ref[...] = _layer_norm(acc[...], g2_ref[...], b2_ref[...])


def _combine(n_groups, gdst, slot_t, gw_t, h1, y, wsg, wsu, wsd, g2, b2, tt):
    m = h1.shape[0]
    nt = m // tt
    s_tile = _tile_slots(tt)
    const = lambda i, ng: (0, 0)
    table = lambda f: pl.BlockSpec((1, 1, s_tile // GROUP_ROWS), f, memory_space=pltpu.SMEM)
    grid_spec = pltpu.PrefetchScalarGridSpec(
        num_scalar_prefetch=1,
        grid=(nt,),
        in_specs=[
            table(lambda i, ng: (i, 0, 0)),
            table(lambda i, ng: (jnp.minimum(i + 1, nt - 1), 0, 0)),
            pl.BlockSpec((tt, TOP_K), lambda i, ng: (i, 0)),
            pl.BlockSpec((tt, TOP_K), lambda i, ng: (i, 0)),
            pl.BlockSpec((tt, D_MODEL), lambda i, ng: (i, 0)),
            pl.BlockSpec(memory_space=pl.ANY),
            pl.BlockSpec((D_MODEL, D_EXPERT), const),
            pl.BlockSpec((D_MODEL, D_EXPERT), const),
            pl.BlockSpec((D_EXPERT, D_MODEL), const),
            pl.BlockSpec((1, D_MODEL), const),
            pl.BlockSpec((1, D_MODEL), const),
        ],
        out_specs=pl.BlockSpec((tt, D_MODEL), lambda i, ng: (i, 0)),
        scratch_shapes=[pltpu.VMEM((2, s_tile, D_MODEL), BF16), pltpu.VMEM((tt, D_MODEL), F32),
                        pltpu.SemaphoreType.DMA((2,))],
    )
    return pl.pallas_call(
        functools.partial(_combine_kernel, tt=tt, s_tile=s_tile),
        grid_spec=grid_spec,
        out_shape=jax.ShapeDtypeStruct((m, D_MODEL), F32),
        compiler_params=_params("arbitrary"),
        name="combine",
    )(n_groups, gdst, gdst, slot_t, gw_t, h1, y, wsg, wsu, wsd, g2, b2)


def _pick_tile(m, pref):
    t = min(pref, m)
    while m % t:
        t //= 2
    return t


def _gates_time_on_lanes(sg, chunk):
    m = sg.shape[0]
    return sg[:, :SUBLANES].reshape(m // chunk, chunk, SUBLANES).transpose(0, 2, 1)


def _forward(x, meta_tokens, ln_emb_g, ln_emb_b, w_in, hg_lb_logits, hg_norm_g, ml_conv_w, ml_conv_b,
             ml_ig_bias, ml_fg_bias, ml_norm_g, w_branch_a, w_branch_b, w_out, ln1_g, ln1_b,
             w_router, router_bias, w_exp_gate, w_exp_up, w_exp_down, w_sh_gate, w_sh_up, w_sh_down,
             ln2_g, ln2_b, *, chunk, tm_proj, tm_moe):
    nb, seq, d = x.shape
    m = nb * seq
    row = lambda a: a.reshape(1, -1).astype(F32)

    w = w_in[0]
    kw = HG_HEADS * HG_DK
    o_qa, o_fa, o_ia, o_ga = 0, kw, 2 * kw, 3 * kw
    o_qb = 4 * kw
    o_kb = o_qb + ML_HEADS * ML_DK
    o_vb = o_kb + ML_HEADS * ML_DK
    o_ob = o_vb + ML_HEADS * ML_DV
    o_ig = o_ob + ML_HEADS * ML_DV
    o_fg = o_ig + ML_HEADS
    o_ma = o_fg + ML_HEADS
    o_mb = o_ma + D_MODEL
    cols = lambda o, n: w[:, o:o + n]
    w_cat = jnp.concatenate([
        cols(o_qa, kw), cols(o_fa, kw), cols(o_ia, kw), cols(o_ga, kw),
        cols(o_qb, 2 * ML_HEADS * ML_DK), cols(o_vb, ML_HEADS * ML_DV), cols(o_ob, ML_HEADS * ML_DV),
        cols(o_ma, D_MODEL), cols(o_mb, D_MODEL)], axis=1).astype(BF16)
    w_small = jnp.pad(cols(o_ig, 2 * ML_HEADS), ((0, 0), (0, 128 - 2 * ML_HEADS))).astype(BF16)
    gate_bias = jnp.pad(jnp.concatenate([ml_ig_bias[0], ml_fg_bias[0]]).astype(F32), (0, 128 - 2 * ML_HEADS)).reshape(1, 128)
    lb = jax.nn.softmax(hg_lb_logits.astype(F32), axis=0)[0].reshape(1, -1)
    eg, eb = row(ln_emb_g), row(ln_emb_b)
    conv_w = ml_conv_w[0].astype(F32)
    conv_b = row(ml_conv_b[0])
    hgn, mln = row(hg_norm_g[0]), row(ml_norm_g[0])

    w_small_t = jnp.pad(cols(o_ig, 2 * ML_HEADS).T, ((0, GROUP_ROWS - 2 * ML_HEADS), (0, 0))).astype(BF16)
    gate_bias_t = jnp.broadcast_to(jnp.pad(gate_bias[0, :2 * ML_HEADS], (0, GROUP_ROWS - 2 * ML_HEADS))[:, None],
                                   (GROUP_ROWS, 128))
    wr = w_router[0].T.astype(F32)
    wr_hi, wr_lo = _split_bf16(wr)
    rbias = jnp.broadcast_to(router_bias[0].astype(F32).reshape(N_EXPERTS, 1), (N_EXPERTS, 128))
    weights = [eg, eb, w_cat, w_small, lb, gate_bias, w_small_t, gate_bias_t, hgn, conv_w, conv_b, mln,
               w_branch_a[0].astype(BF16), w_branch_b[0].astype(BF16), w_out[0].astype(BF16),
               row(ln1_g[0]), row(ln1_b[0]), wr_hi, wr_lo, rbias]

    zero_states = (jnp.zeros((HG_HEADS, HG_DK, HG_DK), F32), jnp.zeros((ML_HEADS, ML_DK, ML_AUG), F32),
                   jnp.zeros((SUBLANES, 128), F32), jnp.zeros((SUBLANES, D_MODEL), F32))
    meta_states = _front(meta_tokens.astype(F32), weights, zero_states, 1, N_META)[4]

    x2d = x.reshape(m, d).astype(F32)
    tt = chunk
    h1, slot_k, gw, cnt, _ = _front(x2d, weights, tuple(s[0] for s in meta_states), nb, chunk)

    nt = m // tt
    s_tile = _tile_slots(tt)
    cnt8 = (cnt[:, :, 0].astype(I32) + GROUP_ROWS - 1) // GROUP_ROWS * GROUP_ROWS
    seg_end = jnp.cumsum(cnt8, axis=1)
    seg_off = seg_end - cnt8
    tile_rows = seg_end[:, -1]
    run = jnp.cumsum(cnt8, axis=0) - cnt8
    tot8 = jnp.sum(cnt8, axis=0)
    padded = (tot8 + MOE_BLOCK - 1) // MOE_BLOCK * MOE_BLOCK
    pends = jnp.cumsum(padded)
    gshift = (pends - padded)[None, :] + run - seg_off
    g_rows = jnp.arange(s_tile // GROUP_ROWS, dtype=I32) * GROUP_ROWS
    e_of_g = jnp.sum((seg_end[:, None, :] <= g_rows[None, :, None]).astype(I32), axis=-1)
    shift_g = jnp.sum(jnp.where(e_of_g[..., None] == jnp.arange(N_EXPERTS, dtype=I32), gshift[:, None, :], 0), axis=-1)
    gdst = jnp.where(g_rows[None, :] < tile_rows[:, None], g_rows[None, :] + shift_g, 0).astype(I32)
    gdst = gdst.reshape(nt, 1, s_tile // GROUP_ROWS)
    n_groups = (tile_rows // GROUP_ROWS).astype(I32)
    n_blocks = -(-(m * TOP_K + nt * N_EXPERTS * (GROUP_ROWS - 1)) // MOE_BLOCK) + N_EXPERTS
    blk_start = jnp.arange(n_blocks, dtype=I32) * MOE_BLOCK
    blk_expert = jnp.minimum(jnp.sum((pends[None, :] <= blk_start[:, None]).astype(I32), axis=1), N_EXPERTS - 1)
    n_used = (pends[-1:] // MOE_BLOCK).astype(I32)

    xs = _dispatch(n_groups, gdst, slot_k, h1, n_blocks * MOE_BLOCK, tt)
    y = _experts(blk_expert, n_used, xs, w_exp_gate[0].astype(BF16), w_exp_up[0].astype(BF16),
                 w_exp_down[0].astype(BF16))
    out = _combine(n_groups, gdst, slot_k.T, gw.T, h1, y, w_sh_gate[0].astype(BF16), w_sh_up[0].astype(BF16),
                   w_sh_down[0].astype(BF16), row(ln2_g[0]), row(ln2_b[0]), tt)
    return out.reshape(nb, seq, d).astype(x.dtype)


def kernel(x, meta_tokens, ln_emb_g, ln_emb_b, w_in, hg_lb_logits, hg_norm_g, ml_conv_w, ml_conv_b, ml_ig_bias, ml_fg_bias, ml_norm_g, w_branch_a, w_branch_b, w_out, ln1_g, ln1_b, w_router, router_bias, w_exp_gate, w_exp_up, w_exp_down, w_sh_gate, w_sh_up, w_sh_down, ln2_g, ln2_b):
    m = x.shape[0] * x.shape[1]
    return _forward(x, meta_tokens, ln_emb_g, ln_emb_b, w_in, hg_lb_logits, hg_norm_g, ml_conv_w, ml_conv_b,
                    ml_ig_bias, ml_fg_bias, ml_norm_g, w_branch_a, w_branch_b, w_out, ln1_g, ln1_b,
                    w_router, router_bias, w_exp_gate, w_exp_up, w_exp_down, w_sh_gate, w_sh_up, w_sh_down,
                    ln2_g, ln2_b, chunk=_pick_tile(x.shape[1], 256), tm_proj=_pick_tile(m, 512),
                    tm_moe=_pick_tile(m, 256))
```

```python
import functools

import jax
import jax.numpy as jnp
from jax import lax
from jax.experimental import pallas as pl
from jax.experimental.pallas import tpu as pltpu

F32, BF16, I32 = jnp.float32, jnp.bfloat16, jnp.int32

D_MODEL = 1024
N_META = 16
HG_HEADS = 8
HG_DK = 128
ML_HEADS = 4
ML_DK = 128
ML_DV = 256
ML_AUG = ML_DV + 128
N_EXPERTS = 64
TOP_K = 8
N_GROUPS = 8
GROUP_SIZE = N_EXPERTS // N_GROUPS
TOPK_GROUPS = 4
D_EXPERT = 256
ROUTED_SCALE = 2.5
MOE_BLOCK = 1024
SLOT_BLOCK = 256
DN_ALPHA = 2.0 ** 0.25
EPS = 1e-5
LOG2E = 1.4426950408889634
EXP2_CLAMP = 115.0
SUBLANES = 8
GROUP_ROWS = 16

P_QA, P_KA, P_IA, P_GA, P_QKB, P_VB, P_OB, P_MA, P_MB = range(9)
N_SLABS = 9

VMEM_LIMIT = 56 * 1024 * 1024


def _params(*sem):
    return pltpu.CompilerParams(dimension_semantics=sem, vmem_limit_bytes=VMEM_LIMIT)


def _sigmoid(x):
    return 1.0 / (1.0 + jnp.exp(-x))


def _log_sigmoid(x):
    return jnp.minimum(x, 0.0) - jnp.log(1.0 + jnp.exp(-jnp.abs(x)))


def _layer_norm(x, g, b):
    xc = x - jnp.mean(x, axis=-1, keepdims=True)
    var = jnp.mean(xc * xc, axis=-1, keepdims=True)
    return xc * lax.rsqrt(var + EPS) * g + b


def _dot(a, b):
    return jnp.dot(a, b, preferred_element_type=F32)


def _dot_nt(a, b):
    return lax.dot_general(a, b, (((1,), (1,)), ((), ())), preferred_element_type=F32)


def _dot_tn(a, b):
    return lax.dot_general(a, b, (((0,), (0,)), ((), ())), preferred_element_type=F32)


def _split_bf16(x):
    hi = x.astype(BF16)
    lo = (x - hi.astype(F32)).astype(BF16)
    return hi, lo


def _neg_abs(x):
    return lax.bitcast_convert_type(lax.bitcast_convert_type(x, I32) | jnp.int32(-2 ** 31), F32)


def _ones_where(cond):
    return jnp.where(cond, 1.0, 0.0).astype(BF16)


def _inproj_body(x_ref, g_ref, b_ref, w_ref, ws_ref, lb_ref, gb_ref, wst_ref, gbt_ref, p_ref, lf_ref, sg_ref, sgt_ref):
    hb = _layer_norm(x_ref[...], g_ref[...], b_ref[...]).astype(BF16)
    for n in range(N_SLABS):
        acc = _dot(hb, w_ref[:, n * D_MODEL:(n + 1) * D_MODEL])
        if n == P_KA:
            lb = lb_ref[...]
            f = lb + (1.0 - lb) * _sigmoid(acc)
            lf_ref[...] = jnp.log(f)
            acc = 1.0 - f
        elif n == P_GA:
            acc = acc * _sigmoid(acc)
        elif n >= P_OB:
            acc = _sigmoid(acc)
        p_ref[n] = acc.astype(BF16)
    s = _dot(hb, ws_ref[...]) + gb_ref[...]
    lane = lax.broadcasted_iota(I32, s.shape, 1)
    sg_ref[...] = jnp.where(lane < ML_HEADS, s, _log_sigmoid(s))
    st = _dot_nt(wst_ref[...], hb) + gbt_ref[:, 0:1]
    srow = lax.broadcasted_iota(I32, st.shape, 0)
    sgt_ref[...] = jnp.where(srow < ML_HEADS, st, _log_sigmoid(st))[:SUBLANES, :]


def _block_rows(b, block, pick):
    c, w = b.shape
    parts = [jnp.broadcast_to(b[j * block + pick:j * block + pick + 1, :], (block, w))
             for j in range(c // block)]
    return parts[0] if len(parts) == 1 else jnp.concatenate(parts, axis=0)


def _hgrn_body(q_ref, k_ref, v_ref, g_ref, lf_ref, ng_ref, y_ref, s_scr, cs):
    row = lax.broadcasted_iota(I32, (cs, cs), 0)
    col = lax.broadcasted_iota(I32, (cs, cs), 1)
    tri = _ones_where(col <= row)
    lf_hi, lf_lo = _split_bf16(lf_ref[...])
    b = (_dot(tri, lf_hi) + _dot(tri, lf_lo)) * LOG2E
    q = q_ref[...]
    k = k_ref[...]
    v = v_ref[...]
    blast = b[cs - 1:cs, :]
    qg = q * jnp.exp2(b).astype(BF16)
    kg = k * jnp.exp2(blast - b).astype(BF16)
    dec = jnp.exp2(blast)

    levels = []
    m = SUBLANES
    while 2 * m <= cs:
        w = jnp.exp2(_neg_abs(b - _block_rows(b, 2 * m, m - 1))).astype(BF16)
        sh = (2 * m).bit_length() - 1
        mask = ((row >> sh) == (col >> sh)) & ((row & (2 * m - 1)) >= m) & ((col & (2 * m - 1)) < m)
        levels.append((q * w, k * w, mask))
        m *= 2
    e = jnp.clip(b - _block_rows(b, SUBLANES, SUBLANES // 2 - 1), -EXP2_CLAMP, EXP2_CLAMP)
    levels.append((q * jnp.exp2(e).astype(BF16), k * jnp.exp2(-e).astype(BF16),
                   ((row >> 3) == (col >> 3)) & (col <= row)))

    ng = ng_ref[...]
    for h in range(HG_HEADS):
        sl = slice(h * HG_DK, (h + 1) * HG_DK)
        st = s_scr[h]
        o = _dot_nt(qg[:, sl], st.astype(BF16))
        sc = jnp.zeros((cs, cs), F32)
        for lq, lk, mask in levels:
            sc = jnp.where(mask, _dot_nt(lq[:, sl], lk[:, sl]), sc)
        o = o + _dot(sc.astype(BF16), v[:, sl])
        s_scr[h] = dec[:, sl] * st + _dot_tn(v[:, sl], kg[:, sl])
        ms = jnp.mean(o * o, axis=-1, keepdims=True)
        y = o * lax.rsqrt(ms + EPS) * ng[:, sl] * g_ref[:, sl].astype(F32)
        y_ref[:, sl] = y.astype(BF16)


def _mlstm_body(qk_ref, v_ref, og_ref, sg_ref, sgt_ref, cw_ref, cb_ref, ng_ref, y_ref, c_scr, m_scr, x_scr, cs):
    x = qk_ref[...].astype(F32)
    prev = x_scr[...]
    sub = lax.broadcasted_iota(I32, (SUBLANES, D_MODEL), 0)
    cw = cw_ref[...]
    conv = cw[3:4, :] * x + cb_ref[...]
    for j in (1, 2, 3):
        xs = pltpu.roll(x, j, 0)
        head = jnp.where(sub < j, pltpu.roll(prev, j, 0), xs[:SUBLANES, :])
        xs = jnp.concatenate([head, xs[SUBLANES:, :]], axis=0)
        conv = conv + cw[3 - j:4 - j, :] * xs
    x_scr[...] = x[cs - SUBLANES:, :]
    qk = conv * _sigmoid(conv)
    q_all = (qk[:, :ML_HEADS * ML_DK] * (ML_DK ** -0.5)).astype(BF16)
    k_all = qk[:, ML_HEADS * ML_DK:]

    row = lax.broadcasted_iota(I32, (cs, cs), 0)
    col = lax.broadcasted_iota(I32, (cs, cs), 1)
    causal = col <= row
    tri = _ones_where(causal)
    sg = sg_ref[...]
    sgt = sgt_ref[...]
    sg_hi, sg_lo = _split_bf16(sg)
    bcol_all = _dot(tri, sg_hi) + _dot(tri, sg_lo)
    sgt_hi, sgt_lo = _split_bf16(sgt)
    brow_all = _dot_nt(sgt_hi, tri) + _dot_nt(sgt_lo, tri)
    lane128 = lax.broadcasted_iota(I32, (cs, 128), 1)
    ones_col = _ones_where(lane128 == 0)
    v = v_ref[...]
    ng = ng_ref[...]

    for h in range(ML_HEADS):
        b_col = bcol_all[:, ML_HEADS + h:ML_HEADS + h + 1]
        b_row = brow_all[ML_HEADS + h:ML_HEADS + h + 1, :]
        ig_col = sg[:, h:h + 1]
        ig_row = sgt[h:h + 1, :]
        m_prev = m_scr[h:h + 1, 0:1]
        q_h = q_all[:, h * ML_DK:(h + 1) * ML_DK]
        k_h = k_all[:, h * ML_DK:(h + 1) * ML_DK]
        v_aug = jnp.concatenate([v[:, h * ML_DV:(h + 1) * ML_DV], ones_col], axis=1)
        c_st = c_scr[h]

        log_intra = jnp.where(causal, b_col - b_row + ig_row, -jnp.inf)
        log_inter = b_col + m_prev
        m_t = jnp.maximum(log_inter, jnp.max(log_intra, axis=-1, keepdims=True))
        w_intra = jnp.exp(log_intra - m_t)
        w_inter = jnp.exp(log_inter - m_t)
        s = _dot_nt(q_h, k_h.astype(BF16)) * w_intra
        tot = w_inter * _dot(q_h, c_st.astype(BF16)) + _dot(s.astype(BF16), v_aug)
        num = tot[:, :ML_DV]
        den = tot[:, ML_DV:ML_DV + 1]
        hid = num / jnp.maximum(jnp.abs(den), jnp.exp(-m_t))

        b_last = b_col[cs - 1:cs, :]
        log_w = b_last - b_col + ig_col
        m_new = jnp.maximum(b_last + m_prev, jnp.max(log_w, axis=0, keepdims=True))
        w_s = jnp.exp(log_w - m_new)
        decay = jnp.exp(b_last + m_prev - m_new)
        c_scr[h] = decay * c_st + _dot_tn((k_h * w_s).astype(BF16), v_aug)
        m_scr[h:h + 1, :] = jnp.broadcast_to(m_new, (1, 128))

        hc = hid - jnp.mean(hid, axis=-1, keepdims=True)
        var = jnp.mean(hc * hc, axis=-1, keepdims=True)
        sl = slice(h * ML_DV, (h + 1) * ML_DV)
        y = hc * lax.rsqrt(var + EPS) * ng[:, sl] * og_ref[:, sl].astype(F32)
        y_ref[:, sl] = y.astype(BF16)


def _merge_body(x_ref, ya_ref, yb_ref, ma_ref, mb_ref, wa_ref, wb_ref, wo_ref, eg_ref, eb_ref, g1_ref, b1_ref,
                wrh_ref, wrl_ref, rb_ref, h1_ref, slot_ref, gw_ref, cnt_ref, tm):
    h0 = _layer_norm(x_ref[...], eg_ref[...], eb_ref[...])
    merged = (ma_ref[...].astype(F32) * _dot(ya_ref[...], wa_ref[...])
              + mb_ref[...].astype(F32) * _dot(yb_ref[...], wb_ref[...]))
    mix = _dot(merged.astype(BF16), wo_ref[...])
    h1 = _layer_norm(DN_ALPHA * h0 + mix, g1_ref[...], b1_ref[...])
    h1_ref[...] = h1

    h_hi, h_lo = _split_bf16(h1)
    logits = _dot_nt(wrh_ref[...], h_hi) + _dot_nt(wrh_ref[...], h_lo) + _dot_nt(wrl_ref[...], h_hi)
    scores = _sigmoid(logits)
    biased = scores + rb_ref[:, 0:1]
    neg_inf = -jnp.inf

    g3 = biased.reshape(N_GROUPS, GROUP_SIZE, tm)
    sub3 = lax.broadcasted_iota(I32, g3.shape, 1)
    top1 = jnp.max(g3, axis=1, keepdims=True)
    first = jnp.min(jnp.where(g3 == top1, sub3, GROUP_SIZE), axis=1, keepdims=True)
    top2 = jnp.max(jnp.where(sub3 == first, neg_inf, g3), axis=1, keepdims=True)
    gs = (top1 + top2).reshape(N_GROUPS, tm)
    gi = lax.broadcasted_iota(I32, gs.shape, 0)
    grank = jnp.zeros(gs.shape, F32)
    for j in range(N_GROUPS):
        r = gs[j:j + 1, :]
        grank = grank + jnp.where((r > gs) | ((r == gs) & (gi > j)), 1.0, 0.0)
    gsel = grank < float(TOPK_GROUPS)
    emask = jnp.broadcast_to(gsel.reshape(N_GROUPS, 1, tm), (N_GROUPS, GROUP_SIZE, tm)).reshape(N_EXPERTS, tm)
    masked = jnp.where(emask, biased, neg_inf)

    ei = lax.broadcasted_iota(I32, masked.shape, 0)
    work = masked
    rank = jnp.full(masked.shape, float(N_EXPERTS), F32)
    for kk in range(TOP_K):
        top = jnp.max(work, axis=0, keepdims=True)
        first = jnp.min(jnp.where(work == top, ei, N_EXPERTS), axis=0, keepdims=True)
        hit = ei == first
        rank = jnp.where(hit, float(kk), rank)
        work = jnp.where(hit, neg_inf, work)
    sel = rank < float(TOP_K)
    sel_w = jnp.where(sel, scores, 0.0)
    gwd = sel_w / jnp.sum(sel_w, axis=0, keepdims=True) * ROUTED_SCALE

    tr = lax.broadcasted_iota(I32, (tm, tm), 0)
    tc = lax.broadcasted_iota(I32, (tm, tm), 1)
    sel_b = _ones_where(sel)
    rloc = _dot(sel_b, _ones_where(tr < tc))
    cnt = _dot(sel_b, jnp.ones((tm, 128), BF16))
    cnt_g = jnp.floor((cnt + (GROUP_ROWS - 1.0)) * (1.0 / GROUP_ROWS)) * GROUP_ROWS
    er = lax.broadcasted_iota(I32, (N_EXPERTS, N_EXPERTS), 0)
    ec = lax.broadcasted_iota(I32, (N_EXPERTS, N_EXPERTS), 1)
    seg_start = _dot(_ones_where(ec < er), cnt_g.astype(BF16))
    slot_e = seg_start[:, 0:1] + rloc
    cnt_ref[...] = cnt

    s_rows, w_rows = [], []
    for kk in range(TOP_K):
        pick = sel & (rank == float(kk))
        s_rows.append(jnp.sum(jnp.where(pick, slot_e, 0.0), axis=0, keepdims=True))
        w_rows.append(jnp.sum(jnp.where(pick, gwd, 0.0), axis=0, keepdims=True))
    slot_ref[...] = jnp.concatenate(s_rows, axis=0).astype(I32)
    gw_ref[...] = jnp.concatenate(w_rows, axis=0)


def _front_kernel(x_ref, eg_ref, eb_ref, w_ref, ws_ref, lb_ref, gb_ref, wst_ref, gbt_ref,
                  hng_ref, s0_ref, cw_ref, cb_ref, mng_ref, c0_ref, m0_ref, x0_ref,
                  wa_ref, wb_ref, wo_ref, g1_ref, b1_ref, wrh_ref, wrl_ref, rb_ref,
                  h1_ref, slot_ref, gw_ref, cnt_ref, sfin_ref, cfin_ref, mfin_ref, xfin_ref,
                  p_scr, lf_scr, sg_scr, sgt_scr, ya_scr, yb_scr, s_scr, c_scr, m_scr, x_scr, *, chunk):
    c = pl.program_id(1)

    @pl.when(c == 0)
    def _():
        s_scr[...] = s0_ref[...]
        c_scr[...] = c0_ref[...]
        m_scr[...] = m0_ref[...]
        x_scr[...] = x0_ref[...]

    _inproj_body(x_ref, eg_ref, eb_ref, w_ref, ws_ref, lb_ref, gb_ref, wst_ref, gbt_ref,
                 p_scr, lf_scr, sg_scr, sgt_scr)
    _hgrn_body(p_scr.at[P_QA], p_scr.at[P_KA], p_scr.at[P_IA], p_scr.at[P_GA], lf_scr, hng_ref, ya_scr, s_scr, chunk)
    _mlstm_body(p_scr.at[P_QKB], p_scr.at[P_VB], p_scr.at[P_OB], sg_scr, sgt_scr, cw_ref, cb_ref, mng_ref,
                yb_scr, c_scr, m_scr, x_scr, chunk)
    _merge_body(x_ref, ya_scr, yb_scr, p_scr.at[P_MA], p_scr.at[P_MB], wa_ref, wb_ref, wo_ref, eg_ref, eb_ref,
                g1_ref, b1_ref, wrh_ref, wrl_ref, rb_ref, h1_ref, slot_ref, gw_ref, cnt_ref, chunk)

    @pl.when(c == pl.num_programs(1) - 1)
    def _():
        sfin_ref[...] = s_scr[...]
        cfin_ref[...] = c_scr[...]
        mfin_ref[...] = m_scr[...]
        xfin_ref[...] = x_scr[...]


def _front(x2d, weights, states, nb, chunk):
    m = x2d.shape[0]
    nc = m // nb // chunk
    s0, c0, m0, x0 = states
    rows = lambda b, c: (b * nc + c, 0)
    lanes = lambda b, c: (0, b * nc + c)
    const2 = lambda b, c: (0, 0)
    const3 = lambda b, c: (0, 0, 0)
    per_batch3 = lambda b, c: (b, 0, 0)
    per_batch4 = lambda b, c: (b, 0, 0, 0)

    def resident(a):
        idx = const2 if a.ndim == 2 else const3
        return pl.BlockSpec(a.shape, idx, pipeline_mode=pl.Buffered(1))

    outs = pl.pallas_call(
        functools.partial(_front_kernel, chunk=chunk),
        grid=(nb, nc),
        in_specs=[pl.BlockSpec((chunk, D_MODEL), rows)] + [resident(a) for a in weights[:8]]
        + [resident(weights[8]), resident(s0)] + [resident(a) for a in weights[9:12]]
        + [resident(c0), resident(m0), resident(x0)] + [resident(a) for a in weights[12:]],
        out_specs=[
            pl.BlockSpec((chunk, D_MODEL), rows),
            pl.BlockSpec((TOP_K, chunk), lanes),
            pl.BlockSpec((TOP_K, chunk), lanes),
            pl.BlockSpec((None, N_EXPERTS, 128), lambda b, c: (b * nc + c, 0, 0)),
            pl.BlockSpec((None, HG_HEADS, HG_DK, HG_DK), per_batch4),
            pl.BlockSpec((None, ML_HEADS, ML_DK, ML_AUG), per_batch4),
            pl.BlockSpec((None, SUBLANES, 128), per_batch3),
            pl.BlockSpec((None, SUBLANES, D_MODEL), per_batch3),
        ],
        out_shape=[
            jax.ShapeDtypeStruct((m, D_MODEL), F32),
            jax.ShapeDtypeStruct((TOP_K, m), I32),
            jax.ShapeDtypeStruct((TOP_K, m), F32),
            jax.ShapeDtypeStruct((m // chunk, N_EXPERTS, 128), F32),
            jax.ShapeDtypeStruct((nb, HG_HEADS, HG_DK, HG_DK), F32),
            jax.ShapeDtypeStruct((nb, ML_HEADS, ML_DK, ML_AUG), F32),
            jax.ShapeDtypeStruct((nb, SUBLANES, 128), F32),
            jax.ShapeDtypeStruct((nb, SUBLANES, D_MODEL), F32),
        ],
        scratch_shapes=[
            pltpu.VMEM((N_SLABS, chunk, D_MODEL), BF16),
            pltpu.VMEM((chunk, D_MODEL), F32),
            pltpu.VMEM((chunk, 128), F32),
            pltpu.VMEM((SUBLANES, chunk), F32),
            pltpu.VMEM((chunk, D_MODEL), BF16),
            pltpu.VMEM((chunk, D_MODEL), BF16),
            pltpu.VMEM((HG_HEADS, HG_DK, HG_DK), F32),
            pltpu.VMEM((ML_HEADS, ML_DK, ML_AUG), F32),
            pltpu.VMEM((SUBLANES, 128), F32),
            pltpu.VMEM((SUBLANES, D_MODEL), F32),
        ],
        compiler_params=_params("parallel", "arbitrary"),
        name="front",
    )(x2d, *weights[:9], s0, *weights[9:12], c0, m0, x0, *weights[12:])
    return outs[0], outs[1], outs[2], outs[3], tuple(outs[4:])


def _tile_slots(tt):
    return -(-(TOP_K * tt + N_EXPERTS * (GROUP_ROWS - 1)) // SLOT_BLOCK) * SLOT_BLOCK


def _wait_groups(n, make_copy, s_tile):
    p = 1 << ((s_tile // GROUP_ROWS).bit_length() - 1)
    while p:
        @pl.when((n & p) != 0)
        def _():
            make_copy(p * GROUP_ROWS).wait()
        p >>= 1


def _for_slot_blocks(n, tt, s_tile, body):
    always = TOP_K * tt // SLOT_BLOCK
    for r in range(always):
        body(r)
    for r in range(always, s_tile // SLOT_BLOCK):
        @pl.when(n * GROUP_ROWS > r * SLOT_BLOCK)
        def _():
            body(r)


def _issue_copies(n2, n1, plan_ref, make_copy, s_tile):
    max2 = s_tile // (2 * GROUP_ROWS)
    base1 = 2 * max2

    def start2(p, c):
        make_copy(plan_ref[0, 0, p], plan_ref[0, 0, max2 + p], 2 * GROUP_ROWS).start()
        return c

    def start1(q, c):
        make_copy(plan_ref[0, 0, base1 + q], plan_ref[0, 0, base1 + N_EXPERTS + q], GROUP_ROWS).start()
        return c

    lax.fori_loop(0, n2, start2, 0)
    lax.fori_loop(0, n1, start1, 0)


def _plan_len(s_tile):
    return 2 * (s_tile // (2 * GROUP_ROWS)) + 2 * N_EXPERTS


def _dispatch_kernel(ng_ref, n2_ref, n1_ref, plan_ref, slot_ref, h_ref, xs_ref, buf, sem, *, tt, s_tile):
    i = pl.program_id(0)
    cur = lax.rem(i, 2)

    def list_to_global(list_row, global_row, rows):
        src = buf.at[cur, pl.ds(pl.multiple_of(list_row, GROUP_ROWS), rows), :]
        dst = xs_ref.at[pl.ds(pl.multiple_of(global_row, GROUP_ROWS), rows), :]
        return pltpu.make_async_copy(src, dst, sem.at[cur])

    def wait_tile(j, b):
        _wait_groups(ng_ref[j], lambda rows: pltpu.make_async_copy(
            buf.at[b, pl.ds(0, rows), :], xs_ref.at[pl.ds(0, rows), :], sem.at[b]), s_tile)

    @pl.when(i >= 2)
    def _():
        wait_tile(i - 2, cur)

    hb = h_ref[...].astype(BF16)
    sl = slot_ref[...]

    def fill_block(r):
        s_iota = lax.broadcasted_iota(I32, (SLOT_BLOCK, tt), 0) + r * SLOT_BLOCK
        p = jnp.zeros((SLOT_BLOCK, tt), F32)
        for kk in range(TOP_K):
            p = jnp.where(s_iota == sl[kk:kk + 1, :], 1.0, p)
        buf[cur, r * SLOT_BLOCK:(r + 1) * SLOT_BLOCK, :] = _dot(p.astype(BF16), hb).astype(BF16)

    _for_slot_blocks(ng_ref[i], tt, s_tile, fill_block)

    _issue_copies(n2_ref[i], n1_ref[i], plan_ref, list_to_global, s_tile)

    @pl.when(i == pl.num_programs(0) - 1)
    def _():
        @pl.when(i >= 1)
        def _():
            wait_tile(i - 1, 1 - cur)
        wait_tile(i, cur)


def _dispatch(counts, plan, slot_k, h1, n_slots, tt):
    m = h1.shape[0]
    s_tile = _tile_slots(tt)
    grid_spec = pltpu.PrefetchScalarGridSpec(
        num_scalar_prefetch=3,
        grid=(m // tt,),
        in_specs=[
            pl.BlockSpec((1, 1, _plan_len(s_tile)), lambda i, *_: (i, 0, 0), memory_space=pltpu.SMEM),
            pl.BlockSpec((TOP_K, tt), lambda i, *_: (0, i)),
            pl.BlockSpec((tt, D_MODEL), lambda i, *_: (i, 0)),
        ],
        out_specs=pl.BlockSpec(memory_space=pl.ANY),
        scratch_shapes=[pltpu.VMEM((2, s_tile, D_MODEL), BF16), pltpu.SemaphoreType.DMA((2,))],
    )
    return pl.pallas_call(
        functools.partial(_dispatch_kernel, tt=tt, s_tile=s_tile),
        grid_spec=grid_spec,
        out_shape=jax.ShapeDtypeStruct((n_slots, D_MODEL), BF16),
        compiler_params=_params("arbitrary"),
        name="dispatch",
    )(*counts, plan, slot_k, h1)


def _experts_kernel(be_ref, nu_ref, x_ref, wg_ref, wu_ref, wd_ref, y_ref):
    del be_ref
    i = pl.program_id(0)

    @pl.when(i < nu_ref[0])
    def _():
        xb = x_ref[...]
        a = _dot(xb, wg_ref[...])
        u = _dot(xb, wu_ref[...])
        y_ref[...] = _dot((a * _sigmoid(a) * u).astype(BF16), wd_ref[...]).astype(BF16)


def _experts(blk_expert, n_used, xs, wg, wu, wd):
    n_slots = xs.shape[0]
    n_blocks = n_slots // MOE_BLOCK
    blk = lambda i, be, nu: (jnp.minimum(i, nu[0] - 1), 0)
    wsel = lambda i, be, nu: (be[jnp.minimum(i, nu[0] - 1)], 0, 0)
    grid_spec = pltpu.PrefetchScalarGridSpec(
        num_scalar_prefetch=2,
        grid=(n_blocks,),
        in_specs=[
            pl.BlockSpec((MOE_BLOCK, D_MODEL), blk),
            pl.BlockSpec((None, D_MODEL, D_EXPERT), wsel),
            pl.BlockSpec((None, D_MODEL, D_EXPERT), wsel),
            pl.BlockSpec((None, D_EXPERT, D_MODEL), wsel),
        ],
        out_specs=pl.BlockSpec((MOE_BLOCK, D_MODEL), blk),
    )
    return pl.pallas_call(
        _experts_kernel,
        grid_spec=grid_spec,
        out_shape=jax.ShapeDtypeStruct((n_slots, D_MODEL), BF16),
        compiler_params=_params("arbitrary"),
        name="experts",
    )(blk_expert, n_used, xs, wg, wu, wd)


def _combine_kernel(ng_ref, n2_ref, n1_ref, gcur_ref, gnext_ref, slot_ref, gw_ref, h_ref, y_ref, sg_ref, su_ref, sd_ref,
                    g2_ref, b2_ref, o_ref, ybuf, acc, sem, *, tt, s_tile):
    i = pl.program_id(0)
    cur = lax.rem(i, 2)

    def fetch(j, b, plan_ref):
        def global_to_list(list_row, global_row, rows):
            src = y_ref.at[pl.ds(pl.multiple_of(global_row, GROUP_ROWS), rows), :]
            dst = ybuf.at[b, pl.ds(pl.multiple_of(list_row, GROUP_ROWS), rows), :]
            return pltpu.make_async_copy(src, dst, sem.at[b])

        _issue_copies(n2_ref[j], n1_ref[j], plan_ref, global_to_list, s_tile)

    @pl.when(i == 0)
    def _():
        ybuf[...] = jnp.zeros_like(ybuf)
        fetch(0, 0, gcur_ref)

    @pl.when(i + 1 < pl.num_programs(0))
    def _():
        fetch(i + 1, 1 - cur, gnext_ref)

    h1 = h_ref[...]
    hb = h1.astype(BF16)
    a = _dot(hb, sg_ref[...])
    u = _dot(hb, su_ref[...])
    shared = _dot((a * _sigmoid(a) * u).astype(BF16), sd_ref[...])

    slot = slot_ref[...]
    gw = gw_ref[...]
    _wait_groups(ng_ref[i], lambda rows: pltpu.make_async_copy(
        y_ref.at[pl.ds(0, rows), :], ybuf.at[cur, pl.ds(0, rows), :], sem.at[cur]), s_tile)

    def block_dot(r):
        lane = lax.broadcasted_iota(I32, (tt, SLOT_BLOCK), 1) + r * SLOT_BLOCK
        p = jnp.zeros((tt, SLOT_BLOCK), F32)
        for kk in range(TOP_K):
            p = jnp.where(lane == slot[:, kk:kk + 1], gw[:, kk:kk + 1], p)
        return _dot(p.astype(BF16), ybuf[cur, r * SLOT_BLOCK:(r + 1) * SLOT_BLOCK, :])

    always = TOP_K * tt // SLOT_BLOCK
    routed = block_dot(0)
    for r in range(1, always):
        routed = routed + block_dot(r)
    acc[...] = DN_ALPHA * h1 + (routed + shared)
    for r in range(always, s_tile // SLOT_BLOCK):
        @pl.when(ng_ref[i] * GROUP_ROWS > r * SLOT_BLOCK)
        def _():
            acc[...] += block_dot(r)
    o_ref[...] = _layer_norm(acc[...], g2_ref[...], b2_ref[...])


def _combine(counts, plan, slot_t, gw_t, h1, y, wsg, wsu, wsd, g2, b2, tt):
    m = h1.shape[0]
    nt = m // tt
    s_tile = _tile_slots(tt)
    const = lambda i, *_: (0, 0)
    table = lambda f: pl.BlockSpec((1, 1, _plan_len(s_tile)), f, memory_space=pltpu.SMEM)
    grid_spec = pltpu.PrefetchScalarGridSpec(
        num_scalar_prefetch=3,
        grid=(nt,),
        in_specs=[
            table(lambda i, *_: (i, 0, 0)),
            table(lambda i, *_: (jnp.minimum(i + 1, nt - 1), 0, 0)),
            pl.BlockSpec((tt, TOP_K), lambda i, *_: (i, 0)),
            pl.BlockSpec((tt, TOP_K), lambda i, *_: (i, 0)),
            pl.BlockSpec((tt, D_MODEL), lambda i, *_: (i, 0)),
            pl.BlockSpec(memory_space=pl.ANY),
            pl.BlockSpec((D_MODEL, D_EXPERT), const),
            pl.BlockSpec((D_MODEL, D_EXPERT), const),
            pl.BlockSpec((D_EXPERT, D_MODEL), const),
            pl.BlockSpec((1, D_MODEL), const),
            pl.BlockSpec((1, D_MODEL), const),
        ],
        out_specs=pl.BlockSpec((tt, D_MODEL), lambda i, *_: (i, 0)),
        scratch_shapes=[pltpu.VMEM((2, s_tile, D_MODEL), BF16), pltpu.VMEM((tt, D_MODEL), F32),
                        pltpu.SemaphoreType.DMA((2,))],
    )
    return pl.pallas_call(
        functools.partial(_combine_kernel, tt=tt, s_tile=s_tile),
        grid_spec=grid_spec,
        out_shape=jax.ShapeDtypeStruct((m, D_MODEL), F32),
        compiler_params=_params("arbitrary"),
        name="combine",
    )(*counts, plan, plan, slot_t, gw_t, h1, y, wsg, wsu, wsd, g2, b2)


def _pick_tile(m, pref):
    t = min(pref, m)
    while m % t:
        t //= 2
    return t


def _forward(x, meta_tokens, ln_emb_g, ln_emb_b, w_in, hg_lb_logits, hg_norm_g, ml_conv_w, ml_conv_b,
             ml_ig_bias, ml_fg_bias, ml_norm_g, w_branch_a, w_branch_b, w_out, ln1_g, ln1_b,
             w_router, router_bias, w_exp_gate, w_exp_up, w_exp_down, w_sh_gate, w_sh_up, w_sh_down,
             ln2_g, ln2_b, *, chunk):
    nb, seq, d = x.shape
    m = nb * seq
    row = lambda a: a.reshape(1, -1).astype(F32)

    w = w_in[0]
    kw = HG_HEADS * HG_DK
    o_qa, o_fa, o_ia, o_ga = 0, kw, 2 * kw, 3 * kw
    o_qb = 4 * kw
    o_kb = o_qb + ML_HEADS * ML_DK
    o_vb = o_kb + ML_HEADS * ML_DK
    o_ob = o_vb + ML_HEADS * ML_DV
    o_ig = o_ob + ML_HEADS * ML_DV
    o_fg = o_ig + ML_HEADS
    o_ma = o_fg + ML_HEADS
    o_mb = o_ma + D_MODEL
    cols = lambda o, n: w[:, o:o + n]
    w_cat = jnp.concatenate([
        cols(o_qa, kw), cols(o_fa, kw), cols(o_ia, kw), cols(o_ga, kw),
        cols(o_qb, 2 * ML_HEADS * ML_DK), cols(o_vb, ML_HEADS * ML_DV), cols(o_ob, ML_HEADS * ML_DV),
        cols(o_ma, D_MODEL), cols(o_mb, D_MODEL)], axis=1).astype(BF16)
    w_small = jnp.pad(cols(o_ig, 2 * ML_HEADS), ((0, 0), (0, 128 - 2 * ML_HEADS))).astype(BF16)
    gate_bias = jnp.pad(jnp.concatenate([ml_ig_bias[0], ml_fg_bias[0]]).astype(F32), (0, 128 - 2 * ML_HEADS)).reshape(1, 128)
    lb = jax.nn.softmax(hg_lb_logits.astype(F32), axis=0)[0].reshape(1, -1)
    eg, eb = row(ln_emb_g), row(ln_emb_b)
    conv_w = ml_conv_w[0].astype(F32)
    conv_b = row(ml_conv_b[0])
    hgn, mln = row(hg_norm_g[0]), row(ml_norm_g[0])
    w_small_t = jnp.pad(cols(o_ig, 2 * ML_HEADS).T, ((0, GROUP_ROWS - 2 * ML_HEADS), (0, 0))).astype(BF16)
    gate_bias_t = jnp.broadcast_to(jnp.pad(gate_bias[0, :2 * ML_HEADS], (0, GROUP_ROWS - 2 * ML_HEADS))[:, None],
                                   (GROUP_ROWS, 128))
    wr = w_router[0].T.astype(F32)
    wr_hi, wr_lo = _split_bf16(wr)
    rbias = jnp.broadcast_to(router_bias[0].astype(F32).reshape(N_EXPERTS, 1), (N_EXPERTS, 128))
    weights = [eg, eb, w_cat, w_small, lb, gate_bias, w_small_t, gate_bias_t, hgn, conv_w, conv_b, mln,
               w_branch_a[0].astype(BF16), w_branch_b[0].astype(BF16), w_out[0].astype(BF16),
               row(ln1_g[0]), row(ln1_b[0]), wr_hi, wr_lo, rbias]

    zero_states = (jnp.zeros((HG_HEADS, HG_DK, HG_DK), F32), jnp.zeros((ML_HEADS, ML_DK, ML_AUG), F32),
                   jnp.zeros((SUBLANES, 128), F32), jnp.zeros((SUBLANES, D_MODEL), F32))
    meta_states = _front(meta_tokens.astype(F32), weights, zero_states, 1, N_META)[4]

    x2d = x.reshape(m, d).astype(F32)
    tt = chunk
    h1, slot_k, gw, cnt, _ = _front(x2d, weights, tuple(s[0] for s in meta_states), nb, chunk)

    nt = m // tt
    s_tile = _tile_slots(tt)
    cnt8 = (cnt[:, :, 0].astype(I32) + GROUP_ROWS - 1) // GROUP_ROWS * GROUP_ROWS
    seg_end = jnp.cumsum(cnt8, axis=1)
    seg_off = seg_end - cnt8
    tile_rows = seg_end[:, -1]
    run = jnp.cumsum(cnt8, axis=0) - cnt8
    tot8 = jnp.sum(cnt8, axis=0)
    padded = (tot8 + MOE_BLOCK - 1) // MOE_BLOCK * MOE_BLOCK
    pends = jnp.cumsum(padded)
    gshift = (pends - padded)[None, :] + run - seg_off
    experts = jnp.arange(N_EXPERTS, dtype=I32)

    def copy_list(per_expert, length, list_row0, step):
        ends = jnp.cumsum(per_expert, axis=1)
        idx = jnp.arange(length, dtype=I32)
        owner = jnp.sum((ends[:, None, :] <= idx[None, :, None]).astype(I32), axis=-1)
        pick = lambda a: jnp.sum(jnp.where(owner[..., None] == experts, a[:, None, :], 0), axis=-1)
        list_row = pick(list_row0 - step * (ends - per_expert)) + step * idx[None, :]
        valid = idx[None, :] < ends[:, -1:]
        return jnp.where(valid, list_row, 0), jnp.where(valid, list_row + pick(gshift), 0), ends[:, -1]

    groups = cnt8 // GROUP_ROWS
    doubles = groups // 2
    l2, g2, n2 = copy_list(doubles, s_tile // (2 * GROUP_ROWS), seg_off, 2 * GROUP_ROWS)
    l1, g1, n1 = copy_list(groups % 2, N_EXPERTS, seg_off + 2 * GROUP_ROWS * doubles, 0)
    plan = jnp.concatenate([l2, g2, l1, g1], axis=1).astype(I32).reshape(nt, 1, -1)
    counts = ((tile_rows // GROUP_ROWS).astype(I32), n2.astype(I32), n1.astype(I32))
    n_blocks = -(-(m * TOP_K + nt * N_EXPERTS * (GROUP_ROWS - 1)) // MOE_BLOCK) + N_EXPERTS
    blk_start = jnp.arange(n_blocks, dtype=I32) * MOE_BLOCK
    blk_expert = jnp.minimum(jnp.sum((pends[None, :] <= blk_start[:, None]).astype(I32), axis=1), N_EXPERTS - 1)
    n_used = (pends[-1:] // MOE_BLOCK).astype(I32)

    xs = _dispatch(counts, plan, slot_k, h1, n_blocks * MOE_BLOCK, tt)
    y = _experts(blk_expert, n_used, xs, w_exp_gate[0].astype(BF16), w_exp_up[0].astype(BF16),
                 w_exp_down[0].astype(BF16))
    out = _combine(counts, plan, slot_k.T, gw.T, h1, y, w_sh_gate[0].astype(BF16), w_sh_up[0].astype(BF16),
                   w_sh_down[0].astype(BF16), row(ln2_g[0]), row(ln2_b[0]), tt)
    return out.reshape(nb, seq, d).astype(x.dtype)


def kernel(x, meta_tokens, ln_emb_g, ln_emb_b, w_in, hg_lb_logits, hg_norm_g, ml_conv_w, ml_conv_b, ml_ig_bias, ml_fg_bias, ml_norm_g, w_branch_a, w_branch_b, w_out, ln1_g, ln1_b, w_router, router_bias, w_exp_gate, w_exp_up, w_exp_down, w_sh_gate, w_sh_up, w_sh_down, ln2_g, ln2_b):
    return _forward(x, meta_tokens, ln_emb_g, ln_emb_b, w_in, hg_lb_logits, hg_norm_g, ml_conv_w, ml_conv_b,
                    ml_ig_bias, ml_fg_bias, ml_norm_g, w_branch_a, w_branch_b, w_out, ln1_g, ln1_b,
                    w_router, router_bias, w_exp_gate, w_exp_up, w_exp_down, w_sh_gate, w_sh_up, w_sh_down,
                    ln2_g, ln2_b, chunk=_pick_tile(x.shape[1], 256))
```

```python
import functools

import jax
import jax.numpy as jnp
from jax import lax
from jax.experimental import pallas as pl
from jax.experimental.pallas import tpu as pltpu

F32, BF16, I32 = jnp.float32, jnp.bfloat16, jnp.int32

D_MODEL = 1024
N_META = 16
HG_HEADS = 8
HG_DK = 128
ML_HEADS = 4
ML_DK = 128
ML_DV = 256
ML_AUG = ML_DV + 128
N_EXPERTS = 64
TOP_K = 8
N_GROUPS = 8
GROUP_SIZE = N_EXPERTS // N_GROUPS
TOPK_GROUPS = 4
D_EXPERT = 256
ROUTED_SCALE = 2.5
MOE_BLOCK = 1024
SLOT_BLOCK = 256
DN_ALPHA = 2.0 ** 0.25
EPS = 1e-5
LOG2E = 1.4426950408889634
EXP2_CLAMP = 115.0
SUBLANES = 8
GROUP_ROWS = 16

P_QA, P_KA, P_IA, P_GA, P_QKB, P_VB, P_OB, P_MA, P_MB = range(9)
N_SLABS = 9

VMEM_LIMIT = 56 * 1024 * 1024


def _params(*sem):
    return pltpu.CompilerParams(dimension_semantics=sem, vmem_limit_bytes=VMEM_LIMIT)


def _sigmoid(x):
    return 1.0 / (1.0 + jnp.exp(-x))


def _log_sigmoid(x):
    return jnp.minimum(x, 0.0) - jnp.log(1.0 + jnp.exp(-jnp.abs(x)))


def _layer_norm(x, g, b):
    xc = x - jnp.mean(x, axis=-1, keepdims=True)
    var = jnp.mean(xc * xc, axis=-1, keepdims=True)
    return xc * lax.rsqrt(var + EPS) * g + b


def _dot(a, b):
    return jnp.dot(a, b, preferred_element_type=F32)


def _dot_nt(a, b):
    return lax.dot_general(a, b, (((1,), (1,)), ((), ())), preferred_element_type=F32)


def _dot_tn(a, b):
    return lax.dot_general(a, b, (((0,), (0,)), ((), ())), preferred_element_type=F32)


def _split_bf16(x):
    hi = x.astype(BF16)
    lo = (x - hi.astype(F32)).astype(BF16)
    return hi, lo


def _neg_abs(x):
    return lax.bitcast_convert_type(lax.bitcast_convert_type(x, I32) | jnp.int32(-2 ** 31), F32)


def _ones_where(cond):
    return jnp.where(cond, 1.0, 0.0).astype(BF16)


def _inproj_body(x_ref, g_ref, b_ref, w_ref, ws_ref, lb_ref, gb_ref, wst_ref, gbt_ref,
                 h0_ref, p_ref, lf_ref, sg_ref, sgt_ref):
    h0 = _layer_norm(x_ref[...], g_ref[...], b_ref[...])
    h0_ref[...] = h0
    hb = h0.astype(BF16)
    for n in range(N_SLABS):
        acc = _dot(hb, w_ref[:, n * D_MODEL:(n + 1) * D_MODEL])
        if n == P_KA:
            lb = lb_ref[...]
            f = lb + (1.0 - lb) * _sigmoid(acc)
            lf_ref[...] = jnp.log(f)
            acc = 1.0 - f
        elif n == P_GA:
            acc = acc * _sigmoid(acc)
        elif n >= P_OB:
            acc = _sigmoid(acc)
        p_ref[n] = acc.astype(BF16)
    s = _dot(hb, ws_ref[...]) + gb_ref[...]
    lane = lax.broadcasted_iota(I32, s.shape, 1)
    sg_ref[...] = jnp.where(lane < ML_HEADS, s, _log_sigmoid(s))
    st = _dot_nt(wst_ref[...], hb) + gbt_ref[:, 0:1]
    srow = lax.broadcasted_iota(I32, st.shape, 0)
    sgt_ref[...] = jnp.where(srow < ML_HEADS, st, _log_sigmoid(st))[:SUBLANES, :]


def _block_rows(b, block, pick):
    c, w = b.shape
    parts = [jnp.broadcast_to(b[j * block + pick:j * block + pick + 1, :], (block, w))
             for j in range(c // block)]
    return parts[0] if len(parts) == 1 else jnp.concatenate(parts, axis=0)


def _hgrn_body(q_ref, k_ref, v_ref, g_ref, lf_ref, ng_ref, y_ref, s_scr, cs):
    row = lax.broadcasted_iota(I32, (cs, cs), 0)
    col = lax.broadcasted_iota(I32, (cs, cs), 1)
    tri = _ones_where(col <= row)
    lf_hi, lf_lo = _split_bf16(lf_ref[...])
    b = (_dot(tri, lf_hi) + _dot(tri, lf_lo)) * LOG2E
    q = q_ref[...]
    k = k_ref[...]
    v = v_ref[...]
    blast = b[cs - 1:cs, :]
    qg = q * jnp.exp2(b).astype(BF16)
    kg = k * jnp.exp2(blast - b).astype(BF16)
    dec = jnp.exp2(blast)

    levels = []
    m = SUBLANES
    while 2 * m <= cs:
        w = jnp.exp2(_neg_abs(b - _block_rows(b, 2 * m, m - 1))).astype(BF16)
        sh = (2 * m).bit_length() - 1
        mask = ((row >> sh) == (col >> sh)) & ((row & (2 * m - 1)) >= m) & ((col & (2 * m - 1)) < m)
        levels.append((q * w, k * w, mask))
        m *= 2
    e = jnp.clip(b - _block_rows(b, SUBLANES, SUBLANES // 2 - 1), -EXP2_CLAMP, EXP2_CLAMP)
    levels.append((q * jnp.exp2(e).astype(BF16), k * jnp.exp2(-e).astype(BF16),
                   ((row >> 3) == (col >> 3)) & (col <= row)))

    ng = ng_ref[...]
    for h in range(HG_HEADS):
        sl = slice(h * HG_DK, (h + 1) * HG_DK)
        st = s_scr[h]
        o = _dot_nt(qg[:, sl], st.astype(BF16))
        sc = jnp.zeros((cs, cs), F32)
        for lq, lk, mask in levels:
            sc = jnp.where(mask, _dot_nt(lq[:, sl], lk[:, sl]), sc)
        o = o + _dot(sc.astype(BF16), v[:, sl])
        s_scr[h] = dec[:, sl] * st + _dot_tn(v[:, sl], kg[:, sl])
        ms = jnp.mean(o * o, axis=-1, keepdims=True)
        y = o * lax.rsqrt(ms + EPS) * ng[:, sl] * g_ref[:, sl].astype(F32)
        y_ref[:, sl] = y.astype(BF16)


def _mlstm_body(qk_ref, v_ref, og_ref, sg_ref, sgt_ref, cw_ref, cb_ref, ng_ref, y_ref, c_scr, m_scr, x_scr, cs):
    x = qk_ref[...].astype(F32)
    prev = x_scr[...]
    sub = lax.broadcasted_iota(I32, (SUBLANES, D_MODEL), 0)
    cw = cw_ref[...]
    conv = cw[3:4, :] * x + cb_ref[...]
    for j in (1, 2, 3):
        xs = pltpu.roll(x, j, 0)
        head = jnp.where(sub < j, pltpu.roll(prev, j, 0), xs[:SUBLANES, :])
        xs = jnp.concatenate([head, xs[SUBLANES:, :]], axis=0)
        conv = conv + cw[3 - j:4 - j, :] * xs
    x_scr[...] = x[cs - SUBLANES:, :]
    qk = conv * _sigmoid(conv)
    q_all = (qk[:, :ML_HEADS * ML_DK] * (ML_DK ** -0.5)).astype(BF16)
    k_all = qk[:, ML_HEADS * ML_DK:]

    row = lax.broadcasted_iota(I32, (cs, cs), 0)
    col = lax.broadcasted_iota(I32, (cs, cs), 1)
    causal = col <= row
    tri = _ones_where(causal)
    sg = sg_ref[...]
    sgt = sgt_ref[...]
    sg_hi, sg_lo = _split_bf16(sg)
    bcol_all = _dot(tri, sg_hi) + _dot(tri, sg_lo)
    sgt_hi, sgt_lo = _split_bf16(sgt)
    brow_all = _dot_nt(sgt_hi, tri) + _dot_nt(sgt_lo, tri)
    lane128 = lax.broadcasted_iota(I32, (cs, 128), 1)
    ones_col = _ones_where(lane128 == 0)
    v = v_ref[...]
    ng = ng_ref[...]

    for h in range(ML_HEADS):
        b_col = bcol_all[:, ML_HEADS + h:ML_HEADS + h + 1]
        b_row = brow_all[ML_HEADS + h:ML_HEADS + h + 1, :]
        ig_col = sg[:, h:h + 1]
        ig_row = sgt[h:h + 1, :]
        m_prev = m_scr[h:h + 1, 0:1]
        q_h = q_all[:, h * ML_DK:(h + 1) * ML_DK]
        k_h = k_all[:, h * ML_DK:(h + 1) * ML_DK]
        v_aug = jnp.concatenate([v[:, h * ML_DV:(h + 1) * ML_DV], ones_col], axis=1)
        c_st = c_scr[h]

        log_intra = jnp.where(causal, b_col - b_row + ig_row, -jnp.inf)
        log_inter = b_col + m_prev
        m_t = jnp.maximum(log_inter, jnp.max(log_intra, axis=-1, keepdims=True))
        w_intra = jnp.exp(log_intra - m_t)
        w_inter = jnp.exp(log_inter - m_t)
        s = _dot_nt(q_h, k_h.astype(BF16)) * w_intra
        tot = w_inter * _dot(q_h, c_st.astype(BF16)) + _dot(s.astype(BF16), v_aug)
        num = tot[:, :ML_DV]
        den = tot[:, ML_DV:ML_DV + 1]
        hid = num / jnp.maximum(jnp.abs(den), jnp.exp(-m_t))

        b_last = b_col[cs - 1:cs, :]
        log_w = b_last - b_col + ig_col
        m_new = jnp.maximum(b_last + m_prev, jnp.max(log_w, axis=0, keepdims=True))
        w_s = jnp.exp(log_w - m_new)
        decay = jnp.exp(b_last + m_prev - m_new)
        c_scr[h] = decay * c_st + _dot_tn((k_h * w_s).astype(BF16), v_aug)
        m_scr[h:h + 1, :] = jnp.broadcast_to(m_new, (1, 128))

        hc = hid - jnp.mean(hid, axis=-1, keepdims=True)
        var = jnp.mean(hc * hc, axis=-1, keepdims=True)
        sl = slice(h * ML_DV, (h + 1) * ML_DV)
        y = hc * lax.rsqrt(var + EPS) * ng[:, sl] * og_ref[:, sl].astype(F32)
        y_ref[:, sl] = y.astype(BF16)


def _merge_body(ya_ref, yb_ref, ma_ref, mb_ref, wa_ref, wb_ref, wo_ref, g1_ref, b1_ref,
                wrh_ref, wrl_ref, rb_ref, h1_ref, slot_ref, gw_ref, cnt_ref, tm):
    h0 = h1_ref[...]
    merged = (ma_ref[...].astype(F32) * _dot(ya_ref[...], wa_ref[...])
              + mb_ref[...].astype(F32) * _dot(yb_ref[...], wb_ref[...]))
    mix = _dot(merged.astype(BF16), wo_ref[...])
    h1 = _layer_norm(DN_ALPHA * h0 + mix, g1_ref[...], b1_ref[...])
    h1_ref[...] = h1

    h_hi, h_lo = _split_bf16(h1)
    logits = _dot_nt(wrh_ref[...], h_hi) + _dot_nt(wrh_ref[...], h_lo) + _dot_nt(wrl_ref[...], h_hi)
    scores = _sigmoid(logits)
    biased = scores + rb_ref[:, 0:1]
    neg_inf = -jnp.inf

    g3 = biased.reshape(N_GROUPS, GROUP_SIZE, tm)
    sub3 = lax.broadcasted_iota(I32, g3.shape, 1)
    top1 = jnp.max(g3, axis=1, keepdims=True)
    first = jnp.min(jnp.where(g3 == top1, sub3, GROUP_SIZE), axis=1, keepdims=True)
    top2 = jnp.max(jnp.where(sub3 == first, neg_inf, g3), axis=1, keepdims=True)
    gs = (top1 + top2).reshape(N_GROUPS, tm)
    gi = lax.broadcasted_iota(I32, gs.shape, 0)
    grank = jnp.zeros(gs.shape, F32)
    for j in range(N_GROUPS):
        r = gs[j:j + 1, :]
        grank = grank + jnp.where((r > gs) | ((r == gs) & (gi > j)), 1.0, 0.0)
    gsel = grank < float(TOPK_GROUPS)
    emask = jnp.broadcast_to(gsel.reshape(N_GROUPS, 1, tm), (N_GROUPS, GROUP_SIZE, tm)).reshape(N_EXPERTS, tm)
    masked = jnp.where(emask, biased, neg_inf)

    ei = lax.broadcasted_iota(I32, masked.shape, 0)
    work = masked
    rank = jnp.full(masked.shape, float(N_EXPERTS), F32)
    for kk in range(TOP_K):
        top = jnp.max(work, axis=0, keepdims=True)
        first = jnp.min(jnp.where(work == top, ei, N_EXPERTS), axis=0, keepdims=True)
        hit = ei == first
        rank = jnp.where(hit, float(kk), rank)
        work = jnp.where(hit, neg_inf, work)
    sel = rank < float(TOP_K)
    sel_w = jnp.where(sel, scores, 0.0)
    gwd = sel_w / jnp.sum(sel_w, axis=0, keepdims=True) * ROUTED_SCALE

    tr = lax.broadcasted_iota(I32, (tm, tm), 0)
    tc = lax.broadcasted_iota(I32, (tm, tm), 1)
    sel_b = _ones_where(sel)
    rloc = _dot(sel_b, _ones_where(tr < tc))
    cnt = _dot(sel_b, jnp.ones((tm, 128), BF16))
    cnt_g = jnp.floor((cnt + (GROUP_ROWS - 1.0)) * (1.0 / GROUP_ROWS)) * GROUP_ROWS
    er = lax.broadcasted_iota(I32, (N_EXPERTS, N_EXPERTS), 0)
    ec = lax.broadcasted_iota(I32, (N_EXPERTS, N_EXPERTS), 1)
    seg_start = _dot(_ones_where(ec < er), cnt_g.astype(BF16))
    slot_e = seg_start[:, 0:1] + rloc
    cnt_ref[...] = cnt

    s_rows, w_rows = [], []
    for kk in range(TOP_K):
        pick = sel & (rank == float(kk))
        s_rows.append(jnp.sum(jnp.where(pick, slot_e, 0.0), axis=0, keepdims=True))
        w_rows.append(jnp.sum(jnp.where(pick, gwd, 0.0), axis=0, keepdims=True))
    slot_ref[...] = jnp.concatenate(s_rows, axis=0).astype(I32)
    gw_ref[...] = jnp.concatenate(w_rows, axis=0)


def _front_kernel(x_ref, eg_ref, eb_ref, w_ref, ws_ref, lb_ref, gb_ref, wst_ref, gbt_ref,
                  hng_ref, s0_ref, cw_ref, cb_ref, mng_ref, c0_ref, m0_ref, x0_ref,
                  wa_ref, wb_ref, wo_ref, g1_ref, b1_ref, wrh_ref, wrl_ref, rb_ref,
                  h1_ref, slot_ref, gw_ref, cnt_ref, sfin_ref, cfin_ref, mfin_ref, xfin_ref,
                  p_scr, lf_scr, sg_scr, sgt_scr, ya_scr, yb_scr, s_scr, c_scr, m_scr, x_scr, *, chunk):
    c = pl.program_id(1)

    @pl.when(c == 0)
    def _():
        s_scr[...] = s0_ref[...]
        c_scr[...] = c0_ref[...]
        m_scr[...] = m0_ref[...]
        x_scr[...] = x0_ref[...]

    _inproj_body(x_ref, eg_ref, eb_ref, w_ref, ws_ref, lb_ref, gb_ref, wst_ref, gbt_ref,
                 h1_ref, p_scr, lf_scr, sg_scr, sgt_scr)
    _hgrn_body(p_scr.at[P_QA], p_scr.at[P_KA], p_scr.at[P_IA], p_scr.at[P_GA], lf_scr, hng_ref, ya_scr, s_scr, chunk)
    _mlstm_body(p_scr.at[P_QKB], p_scr.at[P_VB], p_scr.at[P_OB], sg_scr, sgt_scr, cw_ref, cb_ref, mng_ref,
                yb_scr, c_scr, m_scr, x_scr, chunk)
    _merge_body(ya_scr, yb_scr, p_scr.at[P_MA], p_scr.at[P_MB], wa_ref, wb_ref, wo_ref,
                g1_ref, b1_ref, wrh_ref, wrl_ref, rb_ref, h1_ref, slot_ref, gw_ref, cnt_ref, chunk)

    @pl.when(c == pl.num_programs(1) - 1)
    def _():
        sfin_ref[...] = s_scr[...]
        cfin_ref[...] = c_scr[...]
        mfin_ref[...] = m_scr[...]
        xfin_ref[...] = x_scr[...]


def _front(x2d, weights, states, nb, chunk):
    m = x2d.shape[0]
    nc = m // nb // chunk
    s0, c0, m0, x0 = states
    rows = lambda b, c: (b * nc + c, 0)
    lanes = lambda b, c: (0, b * nc + c)
    const2 = lambda b, c: (0, 0)
    const3 = lambda b, c: (0, 0, 0)
    per_batch3 = lambda b, c: (b, 0, 0)
    per_batch4 = lambda b, c: (b, 0, 0, 0)

    def resident(a):
        idx = const2 if a.ndim == 2 else const3
        return pl.BlockSpec(a.shape, idx, pipeline_mode=pl.Buffered(1))

    outs = pl.pallas_call(
        functools.partial(_front_kernel, chunk=chunk),
        grid=(nb, nc),
        in_specs=[pl.BlockSpec((chunk, D_MODEL), rows)] + [resident(a) for a in weights[:8]]
        + [resident(weights[8]), resident(s0)] + [resident(a) for a in weights[9:12]]
        + [resident(c0), resident(m0), resident(x0)] + [resident(a) for a in weights[12:]],
        out_specs=[
            pl.BlockSpec((chunk, D_MODEL), rows),
            pl.BlockSpec((TOP_K, chunk), lanes),
            pl.BlockSpec((TOP_K, chunk), lanes),
            pl.BlockSpec((None, N_EXPERTS, 128), lambda b, c: (b * nc + c, 0, 0)),
            pl.BlockSpec((None, HG_HEADS, HG_DK, HG_DK), per_batch4),
            pl.BlockSpec((None, ML_HEADS, ML_DK, ML_AUG), per_batch4),
            pl.BlockSpec((None, SUBLANES, 128), per_batch3),
            pl.BlockSpec((None, SUBLANES, D_MODEL), per_batch3),
        ],
        out_shape=[
            jax.ShapeDtypeStruct((m, D_MODEL), F32),
            jax.ShapeDtypeStruct((TOP_K, m), I32),
            jax.ShapeDtypeStruct((TOP_K, m), F32),
            jax.ShapeDtypeStruct((m // chunk, N_EXPERTS, 128), F32),
            jax.ShapeDtypeStruct((nb, HG_HEADS, HG_DK, HG_DK), F32),
            jax.ShapeDtypeStruct((nb, ML_HEADS, ML_DK, ML_AUG), F32),
            jax.ShapeDtypeStruct((nb, SUBLANES, 128), F32),
            jax.ShapeDtypeStruct((nb, SUBLANES, D_MODEL), F32),
        ],
        scratch_shapes=[
            pltpu.VMEM((N_SLABS, chunk, D_MODEL), BF16),
            pltpu.VMEM((chunk, D_MODEL), F32),
            pltpu.VMEM((chunk, 128), F32),
            pltpu.VMEM((SUBLANES, chunk), F32),
            pltpu.VMEM((chunk, D_MODEL), BF16),
            pltpu.VMEM((chunk, D_MODEL), BF16),
            pltpu.VMEM((HG_HEADS, HG_DK, HG_DK), F32),
            pltpu.VMEM((ML_HEADS, ML_DK, ML_AUG), F32),
            pltpu.VMEM((SUBLANES, 128), F32),
            pltpu.VMEM((SUBLANES, D_MODEL), F32),
        ],
        compiler_params=_params("parallel", "arbitrary"),
        name="front",
    )(x2d, *weights[:9], s0, *weights[9:12], c0, m0, x0, *weights[12:])
    return outs[0], outs[1], outs[2], outs[3], tuple(outs[4:])


def _tile_slots(tt):
    return -(-(TOP_K * tt + N_EXPERTS * (GROUP_ROWS - 1)) // SLOT_BLOCK) * SLOT_BLOCK


def _wait_groups(n, make_copy, s_tile):
    p = 1 << ((s_tile // GROUP_ROWS).bit_length() - 1)
    while p:
        @pl.when((n & p) != 0)
        def _():
            make_copy(p * GROUP_ROWS).wait()
        p >>= 1


def _for_slot_blocks(n, tt, s_tile, body):
    always = TOP_K * tt // SLOT_BLOCK
    for r in range(always):
        body(r)
    for r in range(always, s_tile // SLOT_BLOCK):
        @pl.when(n * GROUP_ROWS > r * SLOT_BLOCK)
        def _():
            body(r)


def _issue_copies(n2, n1, plan_ref, make_copy, s_tile):
    max2 = s_tile // (2 * GROUP_ROWS)
    base1 = 2 * max2

    def start2(p, c):
        make_copy(plan_ref[0, 0, p], plan_ref[0, 0, max2 + p], 2 * GROUP_ROWS).start()
        return c

    def start1(q, c):
        make_copy(plan_ref[0, 0, base1 + q], plan_ref[0, 0, base1 + N_EXPERTS + q], GROUP_ROWS).start()
        return c

    lax.fori_loop(0, n2, start2, 0)
    lax.fori_loop(0, n1, start1, 0)


def _plan_len(s_tile):
    return 2 * (s_tile // (2 * GROUP_ROWS)) + 2 * N_EXPERTS


def _dispatch_kernel(ng_ref, n2_ref, n1_ref, plan_ref, slot_ref, h_ref, xs_ref, buf, sem, *, tt, s_tile):
    i = pl.program_id(0)
    cur = lax.rem(i, 2)

    def list_to_global(list_row, global_row, rows):
        src = buf.at[cur, pl.ds(pl.multiple_of(list_row, GROUP_ROWS), rows), :]
        dst = xs_ref.at[pl.ds(pl.multiple_of(global_row, GROUP_ROWS), rows), :]
        return pltpu.make_async_copy(src, dst, sem.at[cur])

    def wait_tile(j, b):
        _wait_groups(ng_ref[j], lambda rows: pltpu.make_async_copy(
            buf.at[b, pl.ds(0, rows), :], xs_ref.at[pl.ds(0, rows), :], sem.at[b]), s_tile)

    @pl.when(i >= 2)
    def _():
        wait_tile(i - 2, cur)

    hb = h_ref[...].astype(BF16)
    sl = slot_ref[...]

    def fill_block(r):
        s_iota = lax.broadcasted_iota(I32, (SLOT_BLOCK, tt), 0) + r * SLOT_BLOCK
        p = jnp.zeros((SLOT_BLOCK, tt), F32)
        for kk in range(TOP_K):
            p = jnp.where(s_iota == sl[kk:kk + 1, :], 1.0, p)
        buf[cur, r * SLOT_BLOCK:(r + 1) * SLOT_BLOCK, :] = _dot(p.astype(BF16), hb).astype(BF16)

    _for_slot_blocks(ng_ref[i], tt, s_tile, fill_block)

    _issue_copies(n2_ref[i], n1_ref[i], plan_ref, list_to_global, s_tile)

    @pl.when(i == pl.num_programs(0) - 1)
    def _():
        @pl.when(i >= 1)
        def _():
            wait_tile(i - 1, 1 - cur)
        wait_tile(i, cur)


def _dispatch(counts, plan, slot_k, h1, n_slots, tt):
    m = h1.shape[0]
    s_tile = _tile_slots(tt)
    grid_spec = pltpu.PrefetchScalarGridSpec(
        num_scalar_prefetch=3,
        grid=(m // tt,),
        in_specs=[
            pl.BlockSpec((1, 1, _plan_len(s_tile)), lambda i, *_: (i, 0, 0), memory_space=pltpu.SMEM),
            pl.BlockSpec((TOP_K, tt), lambda i, *_: (0, i)),
            pl.BlockSpec((tt, D_MODEL), lambda i, *_: (i, 0)),
        ],
        out_specs=pl.BlockSpec(memory_space=pl.ANY),
        scratch_shapes=[pltpu.VMEM((2, s_tile, D_MODEL), BF16), pltpu.SemaphoreType.DMA((2,))],
    )
    return pl.pallas_call(
        functools.partial(_dispatch_kernel, tt=tt, s_tile=s_tile),
        grid_spec=grid_spec,
        out_shape=jax.ShapeDtypeStruct((n_slots, D_MODEL), BF16),
        compiler_params=_params("arbitrary"),
        name="dispatch",
    )(*counts, plan, slot_k, h1)


def _experts_kernel(be_ref, nu_ref, x_ref, wg_ref, wu_ref, wd_ref, y_ref, wg_b, wu_b, wd_b):
    i = pl.program_id(0)
    used = i < nu_ref[0]

    @pl.when(used & ((i == 0) | (be_ref[i] != be_ref[jnp.maximum(i - 1, 0)])))
    def _():
        wg_b[...] = wg_ref[...].astype(BF16)
        wu_b[...] = wu_ref[...].astype(BF16)
        wd_b[...] = wd_ref[...].astype(BF16)

    @pl.when(used)
    def _():
        xb = x_ref[...]
        a = _dot(xb, wg_b[...])
        u = _dot(xb, wu_b[...])
        y_ref[...] = _dot((a * _sigmoid(a) * u).astype(BF16), wd_b[...]).astype(BF16)


def _experts(blk_expert, n_used, xs, wg, wu, wd):
    n_slots = xs.shape[0]
    n_blocks = n_slots // MOE_BLOCK
    blk = lambda i, be, nu: (jnp.minimum(i, nu[0] - 1), 0)
    wsel = lambda i, be, nu: (be[jnp.minimum(i, nu[0] - 1)], 0, 0)
    grid_spec = pltpu.PrefetchScalarGridSpec(
        num_scalar_prefetch=2,
        grid=(n_blocks,),
        in_specs=[
            pl.BlockSpec((MOE_BLOCK, D_MODEL), blk),
            pl.BlockSpec((None, D_MODEL, D_EXPERT), wsel),
            pl.BlockSpec((None, D_MODEL, D_EXPERT), wsel),
            pl.BlockSpec((None, D_EXPERT, D_MODEL), wsel),
        ],
        out_specs=pl.BlockSpec((MOE_BLOCK, D_MODEL), blk),
        scratch_shapes=[pltpu.VMEM((D_MODEL, D_EXPERT), BF16), pltpu.VMEM((D_MODEL, D_EXPERT), BF16),
                        pltpu.VMEM((D_EXPERT, D_MODEL), BF16)],
    )
    return pl.pallas_call(
        _experts_kernel,
        grid_spec=grid_spec,
        out_shape=jax.ShapeDtypeStruct((n_slots, D_MODEL), BF16),
        compiler_params=_params("arbitrary"),
        name="experts",
    )(blk_expert, n_used, xs, wg, wu, wd)


def _combine_kernel(ng_ref, n2_ref, n1_ref, gcur_ref, gnext_ref, slot_ref, gw_ref, h_ref, y_ref, sg_ref, su_ref, sd_ref,
                    g2_ref, b2_ref, o_ref, ybuf, acc, sem, *, tt, s_tile):
    i = pl.program_id(0)
    cur = lax.rem(i, 2)

    def fetch(j, b, plan_ref):
        def global_to_list(list_row, global_row, rows):
            src = y_ref.at[pl.ds(pl.multiple_of(global_row, GROUP_ROWS), rows), :]
            dst = ybuf.at[b, pl.ds(pl.multiple_of(list_row, GROUP_ROWS), rows), :]
            return pltpu.make_async_copy(src, dst, sem.at[b])

        _issue_copies(n2_ref[j], n1_ref[j], plan_ref, global_to_list, s_tile)

    @pl.when(i == 0)
    def _():
        ybuf[...] = jnp.zeros_like(ybuf)
        fetch(0, 0, gcur_ref)

    @pl.when(i + 1 < pl.num_programs(0))
    def _():
        fetch(i + 1, 1 - cur, gnext_ref)

    h1 = h_ref[...]
    hb = h1.astype(BF16)
    a = _dot(hb, sg_ref[...])
    u = _dot(hb, su_ref[...])
    shared = _dot((a * _sigmoid(a) * u).astype(BF16), sd_ref[...])

    slot = slot_ref[...]
    gw = gw_ref[...]
    _wait_groups(ng_ref[i], lambda rows: pltpu.make_async_copy(
        y_ref.at[pl.ds(0, rows), :], ybuf.at[cur, pl.ds(0, rows), :], sem.at[cur]), s_tile)

    def block_dot(r):
        lane = lax.broadcasted_iota(I32, (tt, SLOT_BLOCK), 1) + r * SLOT_BLOCK
        p = jnp.zeros((tt, SLOT_BLOCK), F32)
        for kk in range(TOP_K):
            p = jnp.where(lane == slot[:, kk:kk + 1], gw[:, kk:kk + 1], p)
        return _dot(p.astype(BF16), ybuf[cur, r * SLOT_BLOCK:(r + 1) * SLOT_BLOCK, :])

    always = TOP_K * tt // SLOT_BLOCK
    routed = block_dot(0)
    for r in range(1, always):
        routed = routed + block_dot(r)
    acc[...] = DN_ALPHA * h1 + (routed + shared)
    for r in range(always, s_tile // SLOT_BLOCK):
        @pl.when(ng_ref[i] * GROUP_ROWS > r * SLOT_BLOCK)
        def _():
            acc[...] += block_dot(r)
    o_ref[...] = _layer_norm(acc[...], g2_ref[...], b2_ref[...])


def _combine(counts, plan, slot_t, gw_t, h1, y, wsg, wsu, wsd, g2, b2, tt):
    m = h1.shape[0]
    nt = m // tt
    s_tile = _tile_slots(tt)
    const = lambda i, *_: (0, 0)
    table = lambda f: pl.BlockSpec((1, 1, _plan_len(s_tile)), f, memory_space=pltpu.SMEM)
    grid_spec = pltpu.PrefetchScalarGridSpec(
        num_scalar_prefetch=3,
        grid=(nt,),
        in_specs=[
            table(lambda i, *_: (i, 0, 0)),
            table(lambda i, *_: (jnp.minimum(i + 1, nt - 1), 0, 0)),
            pl.BlockSpec((tt, TOP_K), lambda i, *_: (i, 0)),
            pl.BlockSpec((tt, TOP_K), lambda i, *_: (i, 0)),
            pl.BlockSpec((tt, D_MODEL), lambda i, *_: (i, 0)),
            pl.BlockSpec(memory_space=pl.ANY),
            pl.BlockSpec((D_MODEL, D_EXPERT), const),
            pl.BlockSpec((D_MODEL, D_EXPERT), const),
            pl.BlockSpec((D_EXPERT, D_MODEL), const),
            pl.BlockSpec((1, D_MODEL), const),
            pl.BlockSpec((1, D_MODEL), const),
        ],
        out_specs=pl.BlockSpec((tt, D_MODEL), lambda i, *_: (i, 0)),
        scratch_shapes=[pltpu.VMEM((2, s_tile, D_MODEL), BF16), pltpu.VMEM((tt, D_MODEL), F32),
                        pltpu.SemaphoreType.DMA((2,))],
    )
    return pl.pallas_call(
        functools.partial(_combine_kernel, tt=tt, s_tile=s_tile),
        grid_spec=grid_spec,
        out_shape=jax.ShapeDtypeStruct((m, D_MODEL), F32),
        compiler_params=_params("arbitrary"),
        name="combine",
    )(*counts, plan, plan, slot_t, gw_t, h1, y, wsg, wsu, wsd, g2, b2)


def _pick_tile(m, pref):
    t = min(pref, m)
    while m % t:
        t //= 2
    return t


def _forward(x, meta_tokens, ln_emb_g, ln_emb_b, w_in, hg_lb_logits, hg_norm_g, ml_conv_w, ml_conv_b,
             ml_ig_bias, ml_fg_bias, ml_norm_g, w_branch_a, w_branch_b, w_out, ln1_g, ln1_b,
             w_router, router_bias, w_exp_gate, w_exp_up, w_exp_down, w_sh_gate, w_sh_up, w_sh_down,
             ln2_g, ln2_b, *, chunk):
    nb, seq, d = x.shape
    m = nb * seq
    row = lambda a: a.reshape(1, -1).astype(F32)

    w = w_in[0]
    kw = HG_HEADS * HG_DK
    o_qa, o_fa, o_ia, o_ga = 0, kw, 2 * kw, 3 * kw
    o_qb = 4 * kw
    o_kb = o_qb + ML_HEADS * ML_DK
    o_vb = o_kb + ML_HEADS * ML_DK
    o_ob = o_vb + ML_HEADS * ML_DV
    o_ig = o_ob + ML_HEADS * ML_DV
    o_fg = o_ig + ML_HEADS
    o_ma = o_fg + ML_HEADS
    o_mb = o_ma + D_MODEL
    cols = lambda o, n: w[:, o:o + n]
    w_cat = jnp.concatenate([
        cols(o_qa, kw), cols(o_fa, kw), cols(o_ia, kw), cols(o_ga, kw),
        cols(o_qb, 2 * ML_HEADS * ML_DK), cols(o_vb, ML_HEADS * ML_DV), cols(o_ob, ML_HEADS * ML_DV),
        cols(o_ma, D_MODEL), cols(o_mb, D_MODEL)], axis=1).astype(BF16)
    w_small = jnp.pad(cols(o_ig, 2 * ML_HEADS), ((0, 0), (0, 128 - 2 * ML_HEADS))).astype(BF16)
    gate_bias = jnp.pad(jnp.concatenate([ml_ig_bias[0], ml_fg_bias[0]]).astype(F32), (0, 128 - 2 * ML_HEADS)).reshape(1, 128)
    lb = jax.nn.softmax(hg_lb_logits.astype(F32), axis=0)[0].reshape(1, -1)
    eg, eb = row(ln_emb_g), row(ln_emb_b)
    conv_w = ml_conv_w[0].astype(F32)
    conv_b = row(ml_conv_b[0])
    hgn, mln = row(hg_norm_g[0]), row(ml_norm_g[0])
    w_small_t = jnp.pad(cols(o_ig, 2 * ML_HEADS).T, ((0, GROUP_ROWS - 2 * ML_HEADS), (0, 0))).astype(BF16)
    gate_bias_t = jnp.broadcast_to(jnp.pad(gate_bias[0, :2 * ML_HEADS], (0, GROUP_ROWS - 2 * ML_HEADS))[:, None],
                                   (GROUP_ROWS, 128))
    wr = w_router[0].T.astype(F32)
    wr_hi, wr_lo = _split_bf16(wr)
    rbias = jnp.broadcast_to(router_bias[0].astype(F32).reshape(N_EXPERTS, 1), (N_EXPERTS, 128))
    weights = [eg, eb, w_cat, w_small, lb, gate_bias, w_small_t, gate_bias_t, hgn, conv_w, conv_b, mln,
               w_branch_a[0].astype(BF16), w_branch_b[0].astype(BF16), w_out[0].astype(BF16),
               row(ln1_g[0]), row(ln1_b[0]), wr_hi, wr_lo, rbias]

    zero_states = (jnp.zeros((HG_HEADS, HG_DK, HG_DK), F32), jnp.zeros((ML_HEADS, ML_DK, ML_AUG), F32),
                   jnp.zeros((SUBLANES, 128), F32), jnp.zeros((SUBLANES, D_MODEL), F32))
    meta_states = _front(meta_tokens.astype(F32), weights, zero_states, 1, N_META)[4]

    x2d = x.reshape(m, d).astype(F32)
    tt = chunk
    h1, slot_k, gw, cnt, _ = _front(x2d, weights, tuple(s[0] for s in meta_states), nb, chunk)

    nt = m // tt
    s_tile = _tile_slots(tt)
    cnt8 = (cnt[:, :, 0].astype(I32) + GROUP_ROWS - 1) // GROUP_ROWS * GROUP_ROWS
    seg_end = jnp.cumsum(cnt8, axis=1)
    seg_off = seg_end - cnt8
    tile_rows = seg_end[:, -1]
    run = jnp.cumsum(cnt8, axis=0) - cnt8
    tot8 = jnp.sum(cnt8, axis=0)
    padded = (tot8 + MOE_BLOCK - 1) // MOE_BLOCK * MOE_BLOCK
    pends = jnp.cumsum(padded)
    gshift = (pends - padded)[None, :] + run - seg_off
    experts = jnp.arange(N_EXPERTS, dtype=I32)

    def copy_list(per_expert, length, list_row0, step):
        ends = jnp.cumsum(per_expert, axis=1)
        idx = jnp.arange(length, dtype=I32)
        owner = jnp.sum((ends[:, None, :] <= idx[None, :, None]).astype(I32), axis=-1)
        pick = lambda a: jnp.sum(jnp.where(owner[..., None] == experts, a[:, None, :], 0), axis=-1)
        list_row = pick(list_row0 - step * (ends - per_expert)) + step * idx[None, :]
        valid = idx[None, :] < ends[:, -1:]
        return jnp.where(valid, list_row, 0), jnp.where(valid, list_row + pick(gshift), 0), ends[:, -1]

    groups = cnt8 // GROUP_ROWS
    doubles = groups // 2
    l2, g2, n2 = copy_list(doubles, s_tile // (2 * GROUP_ROWS), seg_off, 2 * GROUP_ROWS)
    l1, g1, n1 = copy_list(groups % 2, N_EXPERTS, seg_off + 2 * GROUP_ROWS * doubles, 0)
    plan = jnp.concatenate([l2, g2, l1, g1], axis=1).astype(I32).reshape(nt, 1, -1)
    counts = ((tile_rows // GROUP_ROWS).astype(I32), n2.astype(I32), n1.astype(I32))
    n_blocks = -(-(m * TOP_K + nt * N_EXPERTS * (GROUP_ROWS - 1)) // MOE_BLOCK) + N_EXPERTS
    blk_start = jnp.arange(n_blocks, dtype=I32) * MOE_BLOCK
    blk_expert = jnp.minimum(jnp.sum((pends[None, :] <= blk_start[:, None]).astype(I32), axis=1), N_EXPERTS - 1)
    n_used = (pends[-1:] // MOE_BLOCK).astype(I32)

    xs = _dispatch(counts, plan, slot_k, h1, n_blocks * MOE_BLOCK, tt)
    y = _experts(blk_expert, n_used, xs, w_exp_gate[0].astype(F32), w_exp_up[0].astype(F32),
                 w_exp_down[0].astype(F32))
    out = _combine(counts, plan, slot_k.T, gw.T, h1, y, w_sh_gate[0].astype(BF16), w_sh_up[0].astype(BF16),
                   w_sh_down[0].astype(BF16), row(ln2_g[0]), row(ln2_b[0]), tt)
    return out.reshape(nb, seq, d).astype(x.dtype)


def kernel(x, meta_tokens, ln_emb_g, ln_emb_b, w_in, hg_lb_logits, hg_norm_g, ml_conv_w, ml_conv_b, ml_ig_bias, ml_fg_bias, ml_norm_g, w_branch_a, w_branch_b, w_out, ln1_g, ln1_b, w_router, router_bias, w_exp_gate, w_exp_up, w_exp_down, w_sh_gate, w_sh_up, w_sh_down, ln2_g, ln2_b):
    return _forward(x, meta_tokens, ln_emb_g, ln_emb_b, w_in, hg_lb_logits, hg_norm_g, ml_conv_w, ml_conv_b,
                    ml_ig_bias, ml_fg_bias, ml_norm_g, w_branch_a, w_branch_b, w_out, ln1_g, ln1_b,
                    w_router, router_bias, w_exp_gate, w_exp_up, w_exp_down, w_sh_gate, w_sh_up, w_sh_down,
                    ln2_g, ln2_b, chunk=_pick_tile(x.shape[1], 256))
```

```python
import functools

import jax
import jax.numpy as jnp
from jax import lax
from jax.experimental import pallas as pl
from jax.experimental.pallas import tpu as pltpu

F32, BF16, I32 = jnp.float32, jnp.bfloat16, jnp.int32

D_MODEL = 1024
N_META = 16
HG_HEADS = 8
HG_DK = 128
ML_HEADS = 4
ML_DK = 128
ML_DV = 256
ML_AUG = ML_DV + 128
N_EXPERTS = 64
TOP_K = 8
N_GROUPS = 8
GROUP_SIZE = N_EXPERTS // N_GROUPS
TOPK_GROUPS = 4
D_EXPERT = 256
ROUTED_SCALE = 2.5
MOE_BLOCK = 1024
SLOT_BLOCK = 512
DN_ALPHA = 2.0 ** 0.25
EPS = 1e-5
LOG2E = 1.4426950408889634
EXP2_CLAMP = 115.0
SUBLANES = 8
GROUP_ROWS = 16

P_QA, P_KA, P_IA, P_GA, P_QKB, P_VB, P_OB, P_MA, P_MB = range(9)
N_SLABS = 9

VMEM_LIMIT = 56 * 1024 * 1024


def _params(*sem):
    return pltpu.CompilerParams(dimension_semantics=sem, vmem_limit_bytes=VMEM_LIMIT)


def _sigmoid(x):
    return 1.0 / (1.0 + jnp.exp(-x))


def _log_sigmoid(x):
    return jnp.minimum(x, 0.0) - jnp.log(1.0 + jnp.exp(-jnp.abs(x)))


def _layer_norm(x, g, b):
    xc = x - jnp.mean(x, axis=-1, keepdims=True)
    var = jnp.mean(xc * xc, axis=-1, keepdims=True)
    return xc * lax.rsqrt(var + EPS) * g + b


def _dot(a, b):
    return jnp.dot(a, b, preferred_element_type=F32)


def _dot_nt(a, b):
    return lax.dot_general(a, b, (((1,), (1,)), ((), ())), preferred_element_type=F32)


def _dot_tn(a, b):
    return lax.dot_general(a, b, (((0,), (0,)), ((), ())), preferred_element_type=F32)


def _split_bf16(x):
    hi = x.astype(BF16)
    lo = (x - hi.astype(F32)).astype(BF16)
    return hi, lo


def _neg_abs(x):
    return lax.bitcast_convert_type(lax.bitcast_convert_type(x, I32) | jnp.int32(-2 ** 31), F32)


def _ones_where(cond):
    return jnp.where(cond, 1.0, 0.0).astype(BF16)


def _inproj_body(x_ref, g_ref, b_ref, w_ref, ws_ref, lb_ref, gb_ref, wst_ref, gbt_ref,
                 h0_ref, p_ref, lf_ref, sg_ref, sgt_ref):
    h0 = _layer_norm(x_ref[...], g_ref[...], b_ref[...])
    h0_ref[...] = h0
    hb = h0.astype(BF16)
    for n in range(N_SLABS):
        acc = _dot(hb, w_ref[:, n * D_MODEL:(n + 1) * D_MODEL])
        if n == P_KA:
            lb = lb_ref[...]
            f = lb + (1.0 - lb) * _sigmoid(acc)
            lf_ref[...] = jnp.log(f)
            acc = 1.0 - f
        elif n == P_GA:
            acc = acc * _sigmoid(acc)
        elif n >= P_OB:
            acc = _sigmoid(acc)
        p_ref[n] = acc.astype(BF16)
    s = _dot(hb, ws_ref[...]) + gb_ref[...]
    lane = lax.broadcasted_iota(I32, s.shape, 1)
    sg_ref[...] = jnp.where(lane < ML_HEADS, s, _log_sigmoid(s))
    st = _dot_nt(wst_ref[...], hb) + gbt_ref[:, 0:1]
    srow = lax.broadcasted_iota(I32, st.shape, 0)
    sgt_ref[...] = jnp.where(srow < ML_HEADS, st, _log_sigmoid(st))[:SUBLANES, :]


def _block_rows(b, block, pick):
    c, w = b.shape
    parts = [jnp.broadcast_to(b[j * block + pick:j * block + pick + 1, :], (block, w))
             for j in range(c // block)]
    return parts[0] if len(parts) == 1 else jnp.concatenate(parts, axis=0)


def _hgrn_body(q_ref, k_ref, v_ref, g_ref, lf_ref, ng_ref, y_ref, s_scr, cs):
    row = lax.broadcasted_iota(I32, (cs, cs), 0)
    col = lax.broadcasted_iota(I32, (cs, cs), 1)
    tri = _ones_where(col <= row)
    lf_hi, lf_lo = _split_bf16(lf_ref[...])
    b = (_dot(tri, lf_hi) + _dot(tri, lf_lo)) * LOG2E
    q = q_ref[...]
    k = k_ref[...]
    v = v_ref[...]
    blast = b[cs - 1:cs, :]
    qg = q * jnp.exp2(b).astype(BF16)
    kg = k * jnp.exp2(blast - b).astype(BF16)
    dec = jnp.exp2(blast)

    levels = []
    m = SUBLANES
    while 2 * m <= cs:
        w = jnp.exp2(_neg_abs(b - _block_rows(b, 2 * m, m - 1))).astype(BF16)
        sh = (2 * m).bit_length() - 1
        mask = ((row >> sh) == (col >> sh)) & ((row & (2 * m - 1)) >= m) & ((col & (2 * m - 1)) < m)
        levels.append((q * w, k * w, mask))
        m *= 2
    e = jnp.clip(b - _block_rows(b, SUBLANES, SUBLANES // 2 - 1), -EXP2_CLAMP, EXP2_CLAMP)
    levels.append((q * jnp.exp2(e).astype(BF16), k * jnp.exp2(-e).astype(BF16),
                   ((row >> 3) == (col >> 3)) & (col <= row)))

    ng = ng_ref[...]
    for h in range(HG_HEADS):
        sl = slice(h * HG_DK, (h + 1) * HG_DK)
        st = s_scr[h]
        o = _dot_nt(qg[:, sl], st.astype(BF16))
        sc = jnp.zeros((cs, cs), F32)
        for lq, lk, mask in levels:
            sc = jnp.where(mask, _dot_nt(lq[:, sl], lk[:, sl]), sc)
        o = o + _dot(sc.astype(BF16), v[:, sl])
        s_scr[h] = dec[:, sl] * st + _dot_tn(v[:, sl], kg[:, sl])
        ms = jnp.mean(o * o, axis=-1, keepdims=True)
        y = o * lax.rsqrt(ms + EPS) * ng[:, sl] * g_ref[:, sl].astype(F32)
        y_ref[:, sl] = y.astype(BF16)


def _mlstm_body(qk_ref, v_ref, og_ref, sg_ref, sgt_ref, cw_ref, cb_ref, ng_ref, y_ref, c_scr, m_scr, x_scr, cs):
    x = qk_ref[...].astype(F32)
    prev = x_scr[...]
    sub = lax.broadcasted_iota(I32, (SUBLANES, D_MODEL), 0)
    cw = cw_ref[...]
    conv = cw[3:4, :] * x + cb_ref[...]
    for j in (1, 2, 3):
        xs = pltpu.roll(x, j, 0)
        head = jnp.where(sub < j, pltpu.roll(prev, j, 0), xs[:SUBLANES, :])
        xs = jnp.concatenate([head, xs[SUBLANES:, :]], axis=0)
        conv = conv + cw[3 - j:4 - j, :] * xs
    x_scr[...] = x[cs - SUBLANES:, :]
    qk = conv * _sigmoid(conv)
    q_all = (qk[:, :ML_HEADS * ML_DK] * (ML_DK ** -0.5)).astype(BF16)
    k_all = qk[:, ML_HEADS * ML_DK:]

    row = lax.broadcasted_iota(I32, (cs, cs), 0)
    col = lax.broadcasted_iota(I32, (cs, cs), 1)
    causal = col <= row
    tri = _ones_where(causal)
    sg = sg_ref[...]
    sgt = sgt_ref[...]
    sg_hi, sg_lo = _split_bf16(sg)
    bcol_all = _dot(tri, sg_hi) + _dot(tri, sg_lo)
    sgt_hi, sgt_lo = _split_bf16(sgt)
    brow_all = _dot_nt(sgt_hi, tri) + _dot_nt(sgt_lo, tri)
    lane128 = lax.broadcasted_iota(I32, (cs, 128), 1)
    ones_col = _ones_where(lane128 == 0)
    v = v_ref[...]
    ng = ng_ref[...]

    for h in range(ML_HEADS):
        b_col = bcol_all[:, ML_HEADS + h:ML_HEADS + h + 1]
        b_row = brow_all[ML_HEADS + h:ML_HEADS + h + 1, :]
        ig_col = sg[:, h:h + 1]
        ig_row = sgt[h:h + 1, :]
        m_prev = m_scr[h:h + 1, 0:1]
        q_h = q_all[:, h * ML_DK:(h + 1) * ML_DK]
        k_h = k_all[:, h * ML_DK:(h + 1) * ML_DK]
        v_aug = jnp.concatenate([v[:, h * ML_DV:(h + 1) * ML_DV], ones_col], axis=1)
        c_st = c_scr[h]

        log_intra = jnp.where(causal, b_col - b_row + ig_row, -jnp.inf)
        log_inter = b_col + m_prev
        m_t = jnp.maximum(log_inter, jnp.max(log_intra, axis=-1, keepdims=True))
        w_intra = jnp.exp(log_intra - m_t)
        w_inter = jnp.exp(log_inter - m_t)
        s = _dot_nt(q_h, k_h.astype(BF16)) * w_intra
        tot = w_inter * _dot(q_h, c_st.astype(BF16)) + _dot(s.astype(BF16), v_aug)
        num = tot[:, :ML_DV]
        den = tot[:, ML_DV:ML_DV + 1]
        hid = num / jnp.maximum(jnp.abs(den), jnp.exp(-m_t))

        b_last = b_col[cs - 1:cs, :]
        log_w = b_last - b_col + ig_col
        m_new = jnp.maximum(b_last + m_prev, jnp.max(log_w, axis=0, keepdims=True))
        w_s = jnp.exp(log_w - m_new)
        decay = jnp.exp(b_last + m_prev - m_new)
        c_scr[h] = decay * c_st + _dot_tn((k_h * w_s).astype(BF16), v_aug)
        m_scr[h:h + 1, :] = jnp.broadcast_to(m_new, (1, 128))

        hc = hid - jnp.mean(hid, axis=-1, keepdims=True)
        var = jnp.mean(hc * hc, axis=-1, keepdims=True)
        sl = slice(h * ML_DV, (h + 1) * ML_DV)
        y = hc * lax.rsqrt(var + EPS) * ng[:, sl] * og_ref[:, sl].astype(F32)
        y_ref[:, sl] = y.astype(BF16)


def _merge_body(ya_ref, yb_ref, ma_ref, mb_ref, wa_ref, wb_ref, wo_ref, g1_ref, b1_ref,
                wrh_ref, wrl_ref, rb_ref, h1_ref, slot_ref, gw_ref, cnt_ref, tm):
    h0 = h1_ref[...]
    merged = (ma_ref[...].astype(F32) * _dot(ya_ref[...], wa_ref[...])
              + mb_ref[...].astype(F32) * _dot(yb_ref[...], wb_ref[...]))
    mix = _dot(merged.astype(BF16), wo_ref[...])
    h1 = _layer_norm(DN_ALPHA * h0 + mix, g1_ref[...], b1_ref[...])
    h1_ref[...] = h1

    h_hi, h_lo = _split_bf16(h1)
    logits = _dot_nt(wrh_ref[...], h_hi) + _dot_nt(wrh_ref[...], h_lo) + _dot_nt(wrl_ref[...], h_hi)
    scores = _sigmoid(logits)
    biased = scores + rb_ref[:, 0:1]
    neg_inf = -jnp.inf

    g3 = biased.reshape(N_GROUPS, GROUP_SIZE, tm)
    sub3 = lax.broadcasted_iota(I32, g3.shape, 1)
    top1 = jnp.max(g3, axis=1, keepdims=True)
    first = jnp.min(jnp.where(g3 == top1, sub3, GROUP_SIZE), axis=1, keepdims=True)
    top2 = jnp.max(jnp.where(sub3 == first, neg_inf, g3), axis=1, keepdims=True)
    gs = (top1 + top2).reshape(N_GROUPS, tm)
    gi = lax.broadcasted_iota(I32, gs.shape, 0)
    grank = jnp.zeros(gs.shape, F32)
    for j in range(N_GROUPS):
        r = gs[j:j + 1, :]
        grank = grank + jnp.where((r > gs) | ((r == gs) & (gi > j)), 1.0, 0.0)
    gsel = grank < float(TOPK_GROUPS)
    emask = jnp.broadcast_to(gsel.reshape(N_GROUPS, 1, tm), (N_GROUPS, GROUP_SIZE, tm)).reshape(N_EXPERTS, tm)
    masked = jnp.where(emask, biased, neg_inf)

    ei = lax.broadcasted_iota(I32, masked.shape, 0)
    work = masked
    rank = jnp.full(masked.shape, float(N_EXPERTS), F32)
    for kk in range(TOP_K):
        top = jnp.max(work, axis=0, keepdims=True)
        first = jnp.min(jnp.where(work == top, ei, N_EXPERTS), axis=0, keepdims=True)
        hit = ei == first
        rank = jnp.where(hit, float(kk), rank)
        work = jnp.where(hit, neg_inf, work)
    sel = rank < float(TOP_K)
    sel_w = jnp.where(sel, scores, 0.0)
    gwd = sel_w / jnp.sum(sel_w, axis=0, keepdims=True) * ROUTED_SCALE

    tr = lax.broadcasted_iota(I32, (tm, tm), 0)
    tc = lax.broadcasted_iota(I32, (tm, tm), 1)
    sel_b = _ones_where(sel)
    rloc = _dot(sel_b, _ones_where(tr < tc))
    cnt = _dot(sel_b, jnp.ones((tm, 128), BF16))
    cnt_g = jnp.floor((cnt + (GROUP_ROWS - 1.0)) * (1.0 / GROUP_ROWS)) * GROUP_ROWS
    er = lax.broadcasted_iota(I32, (N_EXPERTS, N_EXPERTS), 0)
    ec = lax.broadcasted_iota(I32, (N_EXPERTS, N_EXPERTS), 1)
    seg_start = _dot(_ones_where(ec < er), cnt_g.astype(BF16))
    slot_e = seg_start[:, 0:1] + rloc
    cnt_ref[...] = cnt

    s_rows, w_rows = [], []
    for kk in range(TOP_K):
        pick = sel & (rank == float(kk))
        s_rows.append(jnp.sum(jnp.where(pick, slot_e, 0.0), axis=0, keepdims=True))
        w_rows.append(jnp.sum(jnp.where(pick, gwd, 0.0), axis=0, keepdims=True))
    slot_ref[...] = jnp.concatenate(s_rows, axis=0).astype(I32)
    gw_ref[...] = jnp.concatenate(w_rows, axis=0)


def _front_kernel(x_ref, eg_ref, eb_ref, w_ref, ws_ref, lb_ref, gb_ref, wst_ref, gbt_ref,
                  hng_ref, s0_ref, cw_ref, cb_ref, mng_ref, c0_ref, m0_ref, x0_ref,
                  wa_ref, wb_ref, wo_ref, g1_ref, b1_ref, wrh_ref, wrl_ref, rb_ref,
                  h1_ref, slot_ref, gw_ref, cnt_ref, sfin_ref, cfin_ref, mfin_ref, xfin_ref,
                  p_scr, lf_scr, sg_scr, sgt_scr, ya_scr, yb_scr, s_scr, c_scr, m_scr, x_scr, *, chunk):
    c = pl.program_id(1)

    @pl.when(c == 0)
    def _():
        s_scr[...] = s0_ref[...]
        c_scr[...] = c0_ref[...]
        m_scr[...] = m0_ref[...]
        x_scr[...] = x0_ref[...]

    _inproj_body(x_ref, eg_ref, eb_ref, w_ref, ws_ref, lb_ref, gb_ref, wst_ref, gbt_ref,
                 h1_ref, p_scr, lf_scr, sg_scr, sgt_scr)
    _hgrn_body(p_scr.at[P_QA], p_scr.at[P_KA], p_scr.at[P_IA], p_scr.at[P_GA], lf_scr, hng_ref, ya_scr, s_scr, chunk)
    _mlstm_body(p_scr.at[P_QKB], p_scr.at[P_VB], p_scr.at[P_OB], sg_scr, sgt_scr, cw_ref, cb_ref, mng_ref,
                yb_scr, c_scr, m_scr, x_scr, chunk)
    _merge_body(ya_scr, yb_scr, p_scr.at[P_MA], p_scr.at[P_MB], wa_ref, wb_ref, wo_ref,
                g1_ref, b1_ref, wrh_ref, wrl_ref, rb_ref, h1_ref, slot_ref, gw_ref, cnt_ref, chunk)

    @pl.when(c == pl.num_programs(1) - 1)
    def _():
        sfin_ref[...] = s_scr[...]
        cfin_ref[...] = c_scr[...]
        mfin_ref[...] = m_scr[...]
        xfin_ref[...] = x_scr[...]


def _front(x2d, weights, states, nb, chunk):
    m = x2d.shape[0]
    nc = m // nb // chunk
    s0, c0, m0, x0 = states
    rows = lambda b, c: (b * nc + c, 0)
    lanes = lambda b, c: (0, b * nc + c)
    const2 = lambda b, c: (0, 0)
    const3 = lambda b, c: (0, 0, 0)
    per_batch3 = lambda b, c: (b, 0, 0)
    per_batch4 = lambda b, c: (b, 0, 0, 0)

    def resident(a):
        idx = const2 if a.ndim == 2 else const3
        return pl.BlockSpec(a.shape, idx, pipeline_mode=pl.Buffered(1))

    outs = pl.pallas_call(
        functools.partial(_front_kernel, chunk=chunk),
        grid=(nb, nc),
        in_specs=[pl.BlockSpec((chunk, D_MODEL), rows)] + [resident(a) for a in weights[:8]]
        + [resident(weights[8]), resident(s0)] + [resident(a) for a in weights[9:12]]
        + [resident(c0), resident(m0), resident(x0)] + [resident(a) for a in weights[12:]],
        out_specs=[
            pl.BlockSpec((chunk, D_MODEL), rows),
            pl.BlockSpec((TOP_K, chunk), lanes),
            pl.BlockSpec((TOP_K, chunk), lanes),
            pl.BlockSpec((None, N_EXPERTS, 128), lambda b, c: (b * nc + c, 0, 0)),
            pl.BlockSpec((None, HG_HEADS, HG_DK, HG_DK), per_batch4),
            pl.BlockSpec((None, ML_HEADS, ML_DK, ML_AUG), per_batch4),
            pl.BlockSpec((None, SUBLANES, 128), per_batch3),
            pl.BlockSpec((None, SUBLANES, D_MODEL), per_batch3),
        ],
        out_shape=[
            jax.ShapeDtypeStruct((m, D_MODEL), F32),
            jax.ShapeDtypeStruct((TOP_K, m), I32),
            jax.ShapeDtypeStruct((TOP_K, m), F32),
            jax.ShapeDtypeStruct((m // chunk, N_EXPERTS, 128), F32),
            jax.ShapeDtypeStruct((nb, HG_HEADS, HG_DK, HG_DK), F32),
            jax.ShapeDtypeStruct((nb, ML_HEADS, ML_DK, ML_AUG), F32),
            jax.ShapeDtypeStruct((nb, SUBLANES, 128), F32),
            jax.ShapeDtypeStruct((nb, SUBLANES, D_MODEL), F32),
        ],
        scratch_shapes=[
            pltpu.VMEM((N_SLABS, chunk, D_MODEL), BF16),
            pltpu.VMEM((chunk, D_MODEL), F32),
            pltpu.VMEM((chunk, 128), F32),
            pltpu.VMEM((SUBLANES, chunk), F32),
            pltpu.VMEM((chunk, D_MODEL), BF16),
            pltpu.VMEM((chunk, D_MODEL), BF16),
            pltpu.VMEM((HG_HEADS, HG_DK, HG_DK), F32),
            pltpu.VMEM((ML_HEADS, ML_DK, ML_AUG), F32),
            pltpu.VMEM((SUBLANES, 128), F32),
            pltpu.VMEM((SUBLANES, D_MODEL), F32),
        ],
        compiler_params=_params("parallel", "arbitrary"),
        name="front",
    )(x2d, *weights[:9], s0, *weights[9:12], c0, m0, x0, *weights[12:])
    return outs[0], outs[1], outs[2], outs[3], tuple(outs[4:])


def _tile_slots(tt):
    return -(-(TOP_K * tt + N_EXPERTS * (GROUP_ROWS - 1)) // SLOT_BLOCK) * SLOT_BLOCK


def _wait_groups(n, make_copy, s_tile):
    p = 1 << ((s_tile // GROUP_ROWS).bit_length() - 1)
    while p:
        @pl.when((n & p) != 0)
        def _():
            make_copy(p * GROUP_ROWS).wait()
        p >>= 1


def _for_slot_blocks(n, tt, s_tile, body):
    always = TOP_K * tt // SLOT_BLOCK
    for r in range(always):
        body(r)
    for r in range(always, s_tile // SLOT_BLOCK):
        @pl.when(n * GROUP_ROWS > r * SLOT_BLOCK)
        def _():
            body(r)


def _issue_copies(n2, n1, plan_ref, make_copy, s_tile):
    max2 = s_tile // (2 * GROUP_ROWS)
    base1 = 2 * max2

    def start2(p, c):
        make_copy(plan_ref[0, 0, p], plan_ref[0, 0, max2 + p], 2 * GROUP_ROWS).start()
        return c

    def start1(q, c):
        make_copy(plan_ref[0, 0, base1 + q], plan_ref[0, 0, base1 + N_EXPERTS + q], GROUP_ROWS).start()
        return c

    lax.fori_loop(0, n2, start2, 0)
    lax.fori_loop(0, n1, start1, 0)


def _plan_len(s_tile):
    return 2 * (s_tile // (2 * GROUP_ROWS)) + 2 * N_EXPERTS


def _dispatch_kernel(ng_ref, n2_ref, n1_ref, plan_ref, slot_ref, h_ref, xs_ref, buf, sem, *, tt, s_tile):
    i = pl.program_id(0)
    cur = lax.rem(i, 2)

    def list_to_global(list_row, global_row, rows):
        src = buf.at[cur, pl.ds(pl.multiple_of(list_row, GROUP_ROWS), rows), :]
        dst = xs_ref.at[pl.ds(pl.multiple_of(global_row, GROUP_ROWS), rows), :]
        return pltpu.make_async_copy(src, dst, sem.at[cur])

    def wait_tile(j, b):
        _wait_groups(ng_ref[j], lambda rows: pltpu.make_async_copy(
            buf.at[b, pl.ds(0, rows), :], xs_ref.at[pl.ds(0, rows), :], sem.at[b]), s_tile)

    @pl.when(i >= 2)
    def _():
        wait_tile(i - 2, cur)

    hb = h_ref[...].astype(BF16)
    sl = slot_ref[...]

    def fill_block(r):
        s_iota = lax.broadcasted_iota(I32, (SLOT_BLOCK, tt), 0) + r * SLOT_BLOCK
        p = jnp.zeros((SLOT_BLOCK, tt), F32)
        for kk in range(TOP_K):
            p = jnp.where(s_iota == sl[kk:kk + 1, :], 1.0, p)
        buf[cur, r * SLOT_BLOCK:(r + 1) * SLOT_BLOCK, :] = _dot(p.astype(BF16), hb).astype(BF16)

    _for_slot_blocks(ng_ref[i], tt, s_tile, fill_block)

    _issue_copies(n2_ref[i], n1_ref[i], plan_ref, list_to_global, s_tile)

    @pl.when(i == pl.num_programs(0) - 1)
    def _():
        @pl.when(i >= 1)
        def _():
            wait_tile(i - 1, 1 - cur)
        wait_tile(i, cur)


def _dispatch(counts, plan, slot_k, h1, n_slots, tt):
    m = h1.shape[0]
    s_tile = _tile_slots(tt)
    grid_spec = pltpu.PrefetchScalarGridSpec(
        num_scalar_prefetch=3,
        grid=(m // tt,),
        in_specs=[
            pl.BlockSpec((1, 1, _plan_len(s_tile)), lambda i, *_: (i, 0, 0), memory_space=pltpu.SMEM),
            pl.BlockSpec((TOP_K, tt), lambda i, *_: (0, i)),
            pl.BlockSpec((tt, D_MODEL), lambda i, *_: (i, 0)),
        ],
        out_specs=pl.BlockSpec(memory_space=pl.ANY),
        scratch_shapes=[pltpu.VMEM((2, s_tile, D_MODEL), BF16), pltpu.SemaphoreType.DMA((2,))],
    )
    return pl.pallas_call(
        functools.partial(_dispatch_kernel, tt=tt, s_tile=s_tile),
        grid_spec=grid_spec,
        out_shape=jax.ShapeDtypeStruct((n_slots, D_MODEL), BF16),
        compiler_params=_params("arbitrary"),
        name="dispatch",
    )(*counts, plan, slot_k, h1)


def _experts_kernel(be_ref, nu_ref, x_ref, wg_ref, wu_ref, wd_ref, y_ref, wg_b, wu_b, wd_b):
    i = pl.program_id(0)
    used = i < nu_ref[0]

    @pl.when(used & ((i == 0) | (be_ref[i] != be_ref[jnp.maximum(i - 1, 0)])))
    def _():
        wg_b[...] = wg_ref[...].astype(BF16)
        wu_b[...] = wu_ref[...].astype(BF16)
        wd_b[...] = wd_ref[...].astype(BF16)

    @pl.when(used)
    def _():
        xb = x_ref[...]
        a = _dot(xb, wg_b[...])
        u = _dot(xb, wu_b[...])
        y_ref[...] = _dot((a * _sigmoid(a) * u).astype(BF16), wd_b[...]).astype(BF16)


def _experts(blk_expert, n_used, xs, wg, wu, wd):
    n_slots = xs.shape[0]
    n_blocks = n_slots // MOE_BLOCK
    blk = lambda i, be, nu: (jnp.minimum(i, nu[0] - 1), 0)
    wsel = lambda i, be, nu: (be[jnp.minimum(i, nu[0] - 1)], 0, 0)
    grid_spec = pltpu.PrefetchScalarGridSpec(
        num_scalar_prefetch=2,
        grid=(n_blocks,),
        in_specs=[
            pl.BlockSpec((MOE_BLOCK, D_MODEL), blk),
            pl.BlockSpec((None, D_MODEL, D_EXPERT), wsel),
            pl.BlockSpec((None, D_MODEL, D_EXPERT), wsel),
            pl.BlockSpec((None, D_EXPERT, D_MODEL), wsel),
        ],
        out_specs=pl.BlockSpec((MOE_BLOCK, D_MODEL), blk),
        scratch_shapes=[pltpu.VMEM((D_MODEL, D_EXPERT), BF16), pltpu.VMEM((D_MODEL, D_EXPERT), BF16),
                        pltpu.VMEM((D_EXPERT, D_MODEL), BF16)],
    )
    return pl.pallas_call(
        _experts_kernel,
        grid_spec=grid_spec,
        out_shape=jax.ShapeDtypeStruct((n_slots, D_MODEL), BF16),
        compiler_params=_params("arbitrary"),
        name="experts",
    )(blk_expert, n_used, xs, wg, wu, wd)


def _combine_kernel(ng_ref, n2_ref, n1_ref, gcur_ref, gnext_ref, slot_ref, gw_ref, h_ref, y_ref, sg_ref, su_ref, sd_ref,
                    g2_ref, b2_ref, o_ref, ybuf, acc, sem, *, tt, s_tile):
    i = pl.program_id(0)
    cur = lax.rem(i, 2)

    def fetch(j, b, plan_ref):
        def global_to_list(list_row, global_row, rows):
            src = y_ref.at[pl.ds(pl.multiple_of(global_row, GROUP_ROWS), rows), :]
            dst = ybuf.at[b, pl.ds(pl.multiple_of(list_row, GROUP_ROWS), rows), :]
            return pltpu.make_async_copy(src, dst, sem.at[b])

        _issue_copies(n2_ref[j], n1_ref[j], plan_ref, global_to_list, s_tile)

    @pl.when(i == 0)
    def _():
        ybuf[...] = jnp.zeros_like(ybuf)
        fetch(0, 0, gcur_ref)

    @pl.when(i + 1 < pl.num_programs(0))
    def _():
        fetch(i + 1, 1 - cur, gnext_ref)

    h1 = h_ref[...]
    hb = h1.astype(BF16)
    a = _dot(hb, sg_ref[...])
    u = _dot(hb, su_ref[...])
    shared = _dot((a * _sigmoid(a) * u).astype(BF16), sd_ref[...])

    slot = slot_ref[...]
    gw = gw_ref[...]
    _wait_groups(ng_ref[i], lambda rows: pltpu.make_async_copy(
        y_ref.at[pl.ds(0, rows), :], ybuf.at[cur, pl.ds(0, rows), :], sem.at[cur]), s_tile)

    def block_dot(r):
        lane = lax.broadcasted_iota(I32, (tt, SLOT_BLOCK), 1) + r * SLOT_BLOCK
        p = jnp.zeros((tt, SLOT_BLOCK), F32)
        for kk in range(TOP_K):
            p = jnp.where(lane == slot[:, kk:kk + 1], gw[:, kk:kk + 1], p)
        return _dot(p.astype(BF16), ybuf[cur, r * SLOT_BLOCK:(r + 1) * SLOT_BLOCK, :])

    always = TOP_K * tt // SLOT_BLOCK
    routed = block_dot(0)
    for r in range(1, always):
        routed = routed + block_dot(r)
    acc[...] = DN_ALPHA * h1 + (routed + shared)
    for r in range(always, s_tile // SLOT_BLOCK):
        @pl.when(ng_ref[i] * GROUP_ROWS > r * SLOT_BLOCK)
        def _():
            acc[...] += block_dot(r)
    o_ref[...] = _layer_norm(acc[...], g2_ref[...], b2_ref[...])


def _combine(counts, plan, slot_t, gw_t, h1, y, wsg, wsu, wsd, g2, b2, tt):
    m = h1.shape[0]
    nt = m // tt
    s_tile = _tile_slots(tt)
    const = lambda i, *_: (0, 0)
    table = lambda f: pl.BlockSpec((1, 1, _plan_len(s_tile)), f, memory_space=pltpu.SMEM)
    grid_spec = pltpu.PrefetchScalarGridSpec(
        num_scalar_prefetch=3,
        grid=(nt,),
        in_specs=[
            table(lambda i, *_: (i, 0, 0)),
            table(lambda i, *_: (jnp.minimum(i + 1, nt - 1), 0, 0)),
            pl.BlockSpec((tt, TOP_K), lambda i, *_: (i, 0)),
            pl.BlockSpec((tt, TOP_K), lambda i, *_: (i, 0)),
            pl.BlockSpec((tt, D_MODEL), lambda i, *_: (i, 0)),
            pl.BlockSpec(memory_space=pl.ANY),
            pl.BlockSpec((D_MODEL, D_EXPERT), const),
            pl.BlockSpec((D_MODEL, D_EXPERT), const),
            pl.BlockSpec((D_EXPERT, D_MODEL), const),
            pl.BlockSpec((1, D_MODEL), const),
            pl.BlockSpec((1, D_MODEL), const),
        ],
        out_specs=pl.BlockSpec((tt, D_MODEL), lambda i, *_: (i, 0)),
        scratch_shapes=[pltpu.VMEM((2, s_tile, D_MODEL), BF16), pltpu.VMEM((tt, D_MODEL), F32),
                        pltpu.SemaphoreType.DMA((2,))],
    )
    return pl.pallas_call(
        functools.partial(_combine_kernel, tt=tt, s_tile=s_tile),
        grid_spec=grid_spec,
        out_shape=jax.ShapeDtypeStruct((m, D_MODEL), F32),
        compiler_params=_params("arbitrary"),
        name="combine",
    )(*counts, plan, plan, slot_t, gw_t, h1, y, wsg, wsu, wsd, g2, b2)


def _pick_tile(m, pref):
    t = min(pref, m)
    while m % t:
        t //= 2
    return t


def _forward(x, meta_tokens, ln_emb_g, ln_emb_b, w_in, hg_lb_logits, hg_norm_g, ml_conv_w, ml_conv_b,
             ml_ig_bias, ml_fg_bias, ml_norm_g, w_branch_a, w_branch_b, w_out, ln1_g, ln1_b,
             w_router, router_bias, w_exp_gate, w_exp_up, w_exp_down, w_sh_gate, w_sh_up, w_sh_down,
             ln2_g, ln2_b, *, chunk):
    nb, seq, d = x.shape
    m = nb * seq
    row = lambda a: a.reshape(1, -1).astype(F32)

    w = w_in[0]
    kw = HG_HEADS * HG_DK
    o_qa, o_fa, o_ia, o_ga = 0, kw, 2 * kw, 3 * kw
    o_qb = 4 * kw
    o_kb = o_qb + ML_HEADS * ML_DK
    o_vb = o_kb + ML_HEADS * ML_DK
    o_ob = o_vb + ML_HEADS * ML_DV
    o_ig = o_ob + ML_HEADS * ML_DV
    o_fg = o_ig + ML_HEADS
    o_ma = o_fg + ML_HEADS
    o_mb = o_ma + D_MODEL
    cols = lambda o, n: w[:, o:o + n]
    w_cat = jnp.concatenate([
        cols(o_qa, kw), cols(o_fa, kw), cols(o_ia, kw), cols(o_ga, kw),
        cols(o_qb, 2 * ML_HEADS * ML_DK), cols(o_vb, ML_HEADS * ML_DV), cols(o_ob, ML_HEADS * ML_DV),
        cols(o_ma, D_MODEL), cols(o_mb, D_MODEL)], axis=1).astype(BF16)
    w_small = jnp.pad(cols(o_ig, 2 * ML_HEADS), ((0, 0), (0, 128 - 2 * ML_HEADS))).astype(BF16)
    gate_bias = jnp.pad(jnp.concatenate([ml_ig_bias[0], ml_fg_bias[0]]).astype(F32), (0, 128 - 2 * ML_HEADS)).reshape(1, 128)
    lb = jax.nn.softmax(hg_lb_logits.astype(F32), axis=0)[0].reshape(1, -1)
    eg, eb = row(ln_emb_g), row(ln_emb_b)
    conv_w = ml_conv_w[0].astype(F32)
    conv_b = row(ml_conv_b[0])
    hgn, mln = row(hg_norm_g[0]), row(ml_norm_g[0])
    w_small_t = jnp.pad(cols(o_ig, 2 * ML_HEADS).T, ((0, GROUP_ROWS - 2 * ML_HEADS), (0, 0))).astype(BF16)
    gate_bias_t = jnp.broadcast_to(jnp.pad(gate_bias[0, :2 * ML_HEADS], (0, GROUP_ROWS - 2 * ML_HEADS))[:, None],
                                   (GROUP_ROWS, 128))
    wr = w_router[0].T.astype(F32)
    wr_hi, wr_lo = _split_bf16(wr)
    rbias = jnp.broadcast_to(router_bias[0].astype(F32).reshape(N_EXPERTS, 1), (N_EXPERTS, 128))
    weights = [eg, eb, w_cat, w_small, lb, gate_bias, w_small_t, gate_bias_t, hgn, conv_w, conv_b, mln,
               w_branch_a[0].astype(BF16), w_branch_b[0].astype(BF16), w_out[0].astype(BF16),
               row(ln1_g[0]), row(ln1_b[0]), wr_hi, wr_lo, rbias]

    zero_states = (jnp.zeros((HG_HEADS, HG_DK, HG_DK), F32), jnp.zeros((ML_HEADS, ML_DK, ML_AUG), F32),
                   jnp.zeros((SUBLANES, 128), F32), jnp.zeros((SUBLANES, D_MODEL), F32))
    meta_states = _front(meta_tokens.astype(F32), weights, zero_states, 1, N_META)[4]

    x2d = x.reshape(m, d).astype(F32)
    tt = chunk
    h1, slot_k, gw, cnt, _ = _front(x2d, weights, tuple(s[0] for s in meta_states), nb, chunk)

    nt = m // tt
    s_tile = _tile_slots(tt)
    cnt8 = (cnt[:, :, 0].astype(I32) + GROUP_ROWS - 1) // GROUP_ROWS * GROUP_ROWS
    seg_end = jnp.cumsum(cnt8, axis=1)
    seg_off = seg_end - cnt8
    tile_rows = seg_end[:, -1]
    run = jnp.cumsum(cnt8, axis=0) - cnt8
    tot8 = jnp.sum(cnt8, axis=0)
    padded = (tot8 + MOE_BLOCK - 1) // MOE_BLOCK * MOE_BLOCK
    pends = jnp.cumsum(padded)
    gshift = (pends - padded)[None, :] + run - seg_off
    experts = jnp.arange(N_EXPERTS, dtype=I32)

    def copy_list(per_expert, length, list_row0, step):
        ends = jnp.cumsum(per_expert, axis=1)
        idx = jnp.arange(length, dtype=I32)
        owner = jnp.sum((ends[:, None, :] <= idx[None, :, None]).astype(I32), axis=-1)
        pick = lambda a: jnp.sum(jnp.where(owner[..., None] == experts, a[:, None, :], 0), axis=-1)
        list_row = pick(list_row0 - step * (ends - per_expert)) + step * idx[None, :]
        valid = idx[None, :] < ends[:, -1:]
        return jnp.where(valid, list_row, 0), jnp.where(valid, list_row + pick(gshift), 0), ends[:, -1]

    groups = cnt8 // GROUP_ROWS
    doubles = groups // 2
    l2, g2, n2 = copy_list(doubles, s_tile // (2 * GROUP_ROWS), seg_off, 2 * GROUP_ROWS)
    l1, g1, n1 = copy_list(groups % 2, N_EXPERTS, seg_off + 2 * GROUP_ROWS * doubles, 0)
    plan = jnp.concatenate([l2, g2, l1, g1], axis=1).astype(I32).reshape(nt, 1, -1)
    counts = ((tile_rows // GROUP_ROWS).astype(I32), n2.astype(I32), n1.astype(I32))
    n_blocks = -(-(m * TOP_K + nt * N_EXPERTS * (GROUP_ROWS - 1)) // MOE_BLOCK) + N_EXPERTS
    blk_start = jnp.arange(n_blocks, dtype=I32) * MOE_BLOCK
    blk_expert = jnp.minimum(jnp.sum((pends[None, :] <= blk_start[:, None]).astype(I32), axis=1), N_EXPERTS - 1)
    n_used = (pends[-1:] // MOE_BLOCK).astype(I32)

    xs = _dispatch(counts, plan, slot_k, h1, n_blocks * MOE_BLOCK, tt)
    y = _experts(blk_expert, n_used, xs, w_exp_gate[0].astype(F32), w_exp_up[0].astype(F32),
                 w_exp_down[0].astype(F32))
    out = _combine(counts, plan, slot_k.T, gw.T, h1, y, w_sh_gate[0].astype(BF16), w_sh_up[0].astype(BF16),
                   w_sh_down[0].astype(BF16), row(ln2_g[0]), row(ln2_b[0]), tt)
    return out.reshape(nb, seq, d).astype(x.dtype)


def kernel(x, meta_tokens, ln_emb_g, ln_emb_b, w_in, hg_lb_logits, hg_norm_g, ml_conv_w, ml_conv_b, ml_ig_bias, ml_fg_bias, ml_norm_g, w_branch_a, w_branch_b, w_out, ln1_g, ln1_b, w_router, router_bias, w_exp_gate, w_exp_up, w_exp_down, w_sh_gate, w_sh_up, w_sh_down, ln2_g, ln2_b):
    return _forward(x, meta_tokens, ln_emb_g, ln_emb_b, w_in, hg_lb_logits, hg_norm_g, ml_conv_w, ml_conv_b,
                    ml_ig_bias, ml_fg_bias, ml_norm_g, w_branch_a, w_branch_b, w_out, ln1_g, ln1_b,
                    w_router, router_bias, w_exp_gate, w_exp_up, w_exp_down, w_sh_gate, w_sh_up, w_sh_down,
                    ln2_g, ln2_b, chunk=_pick_tile(x.shape[1], 256))
```

```python
import functools

import jax
import jax.numpy as jnp
from jax import lax
from jax.experimental import pallas as pl
from jax.experimental.pallas import tpu as pltpu

F32, BF16, I32 = jnp.float32, jnp.bfloat16, jnp.int32

D_MODEL = 1024
N_META = 16
HG_HEADS = 8
HG_DK = 128
ML_HEADS = 4
ML_DK = 128
ML_DV = 256
ML_AUG = ML_DV + 128
N_EXPERTS = 64
TOP_K = 8
N_GROUPS = 8
GROUP_SIZE = N_EXPERTS // N_GROUPS
TOPK_GROUPS = 4
D_EXPERT = 256
ROUTED_SCALE = 2.5
MOE_BLOCK = 1024
SLOT_BLOCK = 512
DN_ALPHA = 2.0 ** 0.25
EPS = 1e-5
LOG2E = 1.4426950408889634
EXP2_CLAMP = 115.0
SUBLANES = 8
GROUP_ROWS = 16

P_QA, P_KA, P_IA, P_GA, P_QKB, P_VB, P_OB, P_MA, P_MB = range(9)
N_SLABS = 9

VMEM_LIMIT = 56 * 1024 * 1024


def _params(*sem):
    return pltpu.CompilerParams(dimension_semantics=sem, vmem_limit_bytes=VMEM_LIMIT)


def _sigmoid(x):
    return 1.0 / (1.0 + jnp.exp(-x))


def _log_sigmoid(x):
    return jnp.minimum(x, 0.0) - jnp.log(1.0 + jnp.exp(-jnp.abs(x)))


def _layer_norm(x, g, b):
    xc = x - jnp.mean(x, axis=-1, keepdims=True)
    var = jnp.mean(xc * xc, axis=-1, keepdims=True)
    return xc * lax.rsqrt(var + EPS) * g + b


def _dot(a, b):
    return jnp.dot(a, b, preferred_element_type=F32)


def _dot_nt(a, b):
    return lax.dot_general(a, b, (((1,), (1,)), ((), ())), preferred_element_type=F32)


def _dot_tn(a, b):
    return lax.dot_general(a, b, (((0,), (0,)), ((), ())), preferred_element_type=F32)


def _split_bf16(x):
    hi = x.astype(BF16)
    lo = (x - hi.astype(F32)).astype(BF16)
    return hi, lo


def _neg_abs(x):
    return lax.bitcast_convert_type(lax.bitcast_convert_type(x, I32) | jnp.int32(-2 ** 31), F32)


def _ones_where(cond):
    return jnp.where(cond, 1.0, 0.0).astype(BF16)


def _inproj_body(x_ref, g_ref, b_ref, w_ref, ws_ref, lb_ref, gb_ref, wst_ref, gbt_ref,
                 h0_ref, p_ref, lf_ref, sg_ref, sgt_ref, between):
    h0 = _layer_norm(x_ref[...], g_ref[...], b_ref[...])
    h0_ref[...] = h0
    hb = h0.astype(BF16)
    for n in range(N_SLABS):
        if between:
            between.pop(0)()
        acc = _dot(hb, w_ref[:, n * D_MODEL:(n + 1) * D_MODEL])
        if n == P_KA:
            lb = lb_ref[...]
            f = lb + (1.0 - lb) * _sigmoid(acc)
            lf_ref[...] = jnp.log(f)
            acc = 1.0 - f
        elif n == P_GA:
            acc = acc * _sigmoid(acc)
        elif n >= P_OB:
            acc = _sigmoid(acc)
        p_ref[n] = acc.astype(BF16)
    s = _dot(hb, ws_ref[...]) + gb_ref[...]
    lane = lax.broadcasted_iota(I32, s.shape, 1)
    sg_ref[...] = jnp.where(lane < ML_HEADS, s, _log_sigmoid(s))
    st = _dot_nt(wst_ref[...], hb) + gbt_ref[:, 0:1]
    srow = lax.broadcasted_iota(I32, st.shape, 0)
    sgt_ref[...] = jnp.where(srow < ML_HEADS, st, _log_sigmoid(st))[:SUBLANES, :]


def _block_rows(b, block, pick):
    c, w = b.shape
    parts = [jnp.broadcast_to(b[j * block + pick:j * block + pick + 1, :], (block, w))
             for j in range(c // block)]
    return parts[0] if len(parts) == 1 else jnp.concatenate(parts, axis=0)


def _hgrn_body(q_ref, k_ref, v_ref, g_ref, lf_ref, ng_ref, y_ref, s_scr, cs):
    row = lax.broadcasted_iota(I32, (cs, cs), 0)
    col = lax.broadcasted_iota(I32, (cs, cs), 1)
    tri = _ones_where(col <= row)
    lf_hi, lf_lo = _split_bf16(lf_ref[...])
    b = (_dot(tri, lf_hi) + _dot(tri, lf_lo)) * LOG2E
    q = q_ref[...]
    k = k_ref[...]
    v = v_ref[...]
    blast = b[cs - 1:cs, :]
    qg = q * jnp.exp2(b).astype(BF16)
    kg = k * jnp.exp2(blast - b).astype(BF16)
    dec = jnp.exp2(blast)

    levels = []
    m = SUBLANES
    while 2 * m <= cs:
        w = jnp.exp2(_neg_abs(b - _block_rows(b, 2 * m, m - 1))).astype(BF16)
        sh = (2 * m).bit_length() - 1
        mask = ((row >> sh) == (col >> sh)) & ((row & (2 * m - 1)) >= m) & ((col & (2 * m - 1)) < m)
        levels.append((q * w, k * w, mask))
        m *= 2
    e = jnp.clip(b - _block_rows(b, SUBLANES, SUBLANES // 2 - 1), -EXP2_CLAMP, EXP2_CLAMP)
    levels.append((q * jnp.exp2(e).astype(BF16), k * jnp.exp2(-e).astype(BF16),
                   ((row >> 3) == (col >> 3)) & (col <= row)))

    ng = ng_ref[...]
    for h in range(HG_HEADS):
        sl = slice(h * HG_DK, (h + 1) * HG_DK)
        st = s_scr[h]
        o = _dot_nt(qg[:, sl], st.astype(BF16))
        sc = jnp.zeros((cs, cs), F32)
        for lq, lk, mask in levels:
            sc = jnp.where(mask, _dot_nt(lq[:, sl], lk[:, sl]), sc)
        o = o + _dot(sc.astype(BF16), v[:, sl])
        s_scr[h] = dec[:, sl] * st + _dot_tn(v[:, sl], kg[:, sl])
        ms = jnp.mean(o * o, axis=-1, keepdims=True)
        y = o * lax.rsqrt(ms + EPS) * ng[:, sl] * g_ref[:, sl].astype(F32)
        y_ref[:, sl] = y.astype(BF16)


def _mlstm_body(qk_ref, v_ref, og_ref, sg_ref, sgt_ref, cw_ref, cb_ref, ng_ref, y_ref, c_scr, m_scr, x_scr, cs):
    x = qk_ref[...].astype(F32)
    prev = x_scr[...]
    sub = lax.broadcasted_iota(I32, (SUBLANES, D_MODEL), 0)
    cw = cw_ref[...]
    conv = cw[3:4, :] * x + cb_ref[...]
    for j in (1, 2, 3):
        xs = pltpu.roll(x, j, 0)
        head = jnp.where(sub < j, pltpu.roll(prev, j, 0), xs[:SUBLANES, :])
        xs = jnp.concatenate([head, xs[SUBLANES:, :]], axis=0)
        conv = conv + cw[3 - j:4 - j, :] * xs
    x_scr[...] = x[cs - SUBLANES:, :]
    qk = conv * _sigmoid(conv)
    q_all = (qk[:, :ML_HEADS * ML_DK] * (ML_DK ** -0.5)).astype(BF16)
    k_all = qk[:, ML_HEADS * ML_DK:]

    row = lax.broadcasted_iota(I32, (cs, cs), 0)
    col = lax.broadcasted_iota(I32, (cs, cs), 1)
    causal = col <= row
    tri = _ones_where(causal)
    sg = sg_ref[...]
    sgt = sgt_ref[...]
    sg_hi, sg_lo = _split_bf16(sg)
    bcol_all = _dot(tri, sg_hi) + _dot(tri, sg_lo)
    sgt_hi, sgt_lo = _split_bf16(sgt)
    brow_all = _dot_nt(sgt_hi, tri) + _dot_nt(sgt_lo, tri)
    lane128 = lax.broadcasted_iota(I32, (cs, 128), 1)
    ones_col = _ones_where(lane128 == 0)
    v = v_ref[...]
    ng = ng_ref[...]

    for h in range(ML_HEADS):
        b_col = bcol_all[:, ML_HEADS + h:ML_HEADS + h + 1]
        b_row = brow_all[ML_HEADS + h:ML_HEADS + h + 1, :]
        ig_col = sg[:, h:h + 1]
        ig_row = sgt[h:h + 1, :]
        m_prev = m_scr[h:h + 1, 0:1]
        q_h = q_all[:, h * ML_DK:(h + 1) * ML_DK]
        k_h = k_all[:, h * ML_DK:(h + 1) * ML_DK]
        v_aug = jnp.concatenate([v[:, h * ML_DV:(h + 1) * ML_DV], ones_col], axis=1)
        c_st = c_scr[h]

        log_intra = jnp.where(causal, b_col - b_row + ig_row, -jnp.inf)
        log_inter = b_col + m_prev
        m_t = jnp.maximum(log_inter, jnp.max(log_intra, axis=-1, keepdims=True))
        w_intra = jnp.exp(log_intra - m_t)
        w_inter = jnp.exp(log_inter - m_t)
        s = _dot_nt(q_h, k_h.astype(BF16)) * w_intra
        tot = w_inter * _dot(q_h, c_st.astype(BF16)) + _dot(s.astype(BF16), v_aug)
        num = tot[:, :ML_DV]
        den = tot[:, ML_DV:ML_DV + 1]
        hid = num / jnp.maximum(jnp.abs(den), jnp.exp(-m_t))

        b_last = b_col[cs - 1:cs, :]
        log_w = b_last - b_col + ig_col
        m_new = jnp.maximum(b_last + m_prev, jnp.max(log_w, axis=0, keepdims=True))
        w_s = jnp.exp(log_w - m_new)
        decay = jnp.exp(b_last + m_prev - m_new)
        c_scr[h] = decay * c_st + _dot_tn((k_h * w_s).astype(BF16), v_aug)
        m_scr[h:h + 1, :] = jnp.broadcast_to(m_new, (1, 128))

        hc = hid - jnp.mean(hid, axis=-1, keepdims=True)
        var = jnp.mean(hc * hc, axis=-1, keepdims=True)
        sl = slice(h * ML_DV, (h + 1) * ML_DV)
        y = hc * lax.rsqrt(var + EPS) * ng[:, sl] * og_ref[:, sl].astype(F32)
        y_ref[:, sl] = y.astype(BF16)


def _merge_body(ya_ref, yb_ref, ma_ref, mb_ref, wa_ref, wb_ref, wo_ref, g1_ref, b1_ref, h1_ref, hp_ref):
    h0 = h1_ref[...]
    merged = (ma_ref[...].astype(F32) * _dot(ya_ref[...], wa_ref[...])
              + mb_ref[...].astype(F32) * _dot(yb_ref[...], wb_ref[...]))
    mix = _dot(merged.astype(BF16), wo_ref[...])
    h1 = _layer_norm(DN_ALPHA * h0 + mix, g1_ref[...], b1_ref[...])
    h1_ref[...] = h1
    hp_ref[...] = h1


def _router_pieces(h_ref, wrh_ref, wrl_ref, rb_ref, slot_ref, gw_ref, cnt_ref, tm):
    neg_inf = -jnp.inf
    v = {}

    def logits():
        h_hi, h_lo = _split_bf16(h_ref[...])
        lg = _dot_nt(wrh_ref[...], h_hi) + _dot_nt(wrh_ref[...], h_lo) + _dot_nt(wrl_ref[...], h_hi)
        v['scores'] = _sigmoid(lg)
        v['biased'] = v['scores'] + rb_ref[:, 0:1]

    def group_scores():
        g3 = v['biased'].reshape(N_GROUPS, GROUP_SIZE, tm)
        sub3 = lax.broadcasted_iota(I32, g3.shape, 1)
        top1 = jnp.max(g3, axis=1, keepdims=True)
        first = jnp.min(jnp.where(g3 == top1, sub3, GROUP_SIZE), axis=1, keepdims=True)
        top2 = jnp.max(jnp.where(sub3 == first, neg_inf, g3), axis=1, keepdims=True)
        v['gs'] = (top1 + top2).reshape(N_GROUPS, tm)

    def group_select():
        gs = v['gs']
        gi = lax.broadcasted_iota(I32, gs.shape, 0)
        grank = jnp.zeros(gs.shape, F32)
        for j in range(N_GROUPS):
            r = gs[j:j + 1, :]
            grank = grank + jnp.where((r > gs) | ((r == gs) & (gi > j)), 1.0, 0.0)
        gsel = grank < float(TOPK_GROUPS)
        emask = jnp.broadcast_to(gsel.reshape(N_GROUPS, 1, tm), (N_GROUPS, GROUP_SIZE, tm)).reshape(N_EXPERTS, tm)
        v['work'] = jnp.where(emask, v['biased'], neg_inf)
        v['rank'] = jnp.full((N_EXPERTS, tm), float(N_EXPERTS), F32)

    def extract(kk):
        def run():
            ei = lax.broadcasted_iota(I32, (N_EXPERTS, tm), 0)
            for k2 in (kk, kk + 1):
                work = v['work']
                top = jnp.max(work, axis=0, keepdims=True)
                first = jnp.min(jnp.where(work == top, ei, N_EXPERTS), axis=0, keepdims=True)
                hit = ei == first
                v['rank'] = jnp.where(hit, float(k2), v['rank'])
                v['work'] = jnp.where(hit, neg_inf, work)
        return run

    def slots():
        sel = v['rank'] < float(TOP_K)
        sel_w = jnp.where(sel, v['scores'], 0.0)
        v['gwd'] = sel_w / jnp.sum(sel_w, axis=0, keepdims=True) * ROUTED_SCALE
        tr = lax.broadcasted_iota(I32, (tm, tm), 0)
        tc = lax.broadcasted_iota(I32, (tm, tm), 1)
        sel_b = _ones_where(sel)
        rloc = _dot(sel_b, _ones_where(tr < tc))
        cnt = _dot(sel_b, jnp.ones((tm, 128), BF16))
        cnt_g = jnp.floor((cnt + (GROUP_ROWS - 1.0)) * (1.0 / GROUP_ROWS)) * GROUP_ROWS
        er = lax.broadcasted_iota(I32, (N_EXPERTS, N_EXPERTS), 0)
        ec = lax.broadcasted_iota(I32, (N_EXPERTS, N_EXPERTS), 1)
        seg_start = _dot(_ones_where(ec < er), cnt_g.astype(BF16))
        v['slot_e'] = seg_start[:, 0:1] + rloc
        v['sel'] = sel
        cnt_ref[...] = cnt

    def picks():
        s_rows, w_rows = [], []
        for kk in range(TOP_K):
            pick = v['sel'] & (v['rank'] == float(kk))
            s_rows.append(jnp.sum(jnp.where(pick, v['slot_e'], 0.0), axis=0, keepdims=True))
            w_rows.append(jnp.sum(jnp.where(pick, v['gwd'], 0.0), axis=0, keepdims=True))
        slot_ref[...] = jnp.concatenate(s_rows, axis=0).astype(I32)
        gw_ref[...] = jnp.concatenate(w_rows, axis=0)

    return [logits, group_scores, group_select] + [extract(kk) for kk in range(0, TOP_K, 2)] + [slots, picks]


def _front_kernel(x_ref, eg_ref, eb_ref, w_ref, ws_ref, lb_ref, gb_ref, wst_ref, gbt_ref,
                  hng_ref, s0_ref, cw_ref, cb_ref, mng_ref, c0_ref, m0_ref, x0_ref,
                  wa_ref, wb_ref, wo_ref, g1_ref, b1_ref, wrh_ref, wrl_ref, rb_ref,
                  h1_ref, slot_ref, gw_ref, cnt_ref, sfin_ref, cfin_ref, mfin_ref, xfin_ref,
                  p_scr, lf_scr, sg_scr, sgt_scr, ya_scr, yb_scr, hp_scr, s_scr, c_scr, m_scr, x_scr,
                  *, chunk, chunks_per_seq):
    t = pl.program_id(0)
    nt = pl.num_programs(0) - 1

    @pl.when(lax.rem(t, chunks_per_seq) == 0)
    def _():
        s_scr[...] = s0_ref[...]
        c_scr[...] = c0_ref[...]
        m_scr[...] = m0_ref[...]
        x_scr[...] = x0_ref[...]

    @pl.when(t == 0)
    def _():
        hp_scr[...] = jnp.zeros_like(hp_scr)

    def router():
        return _router_pieces(hp_scr, wrh_ref, wrl_ref, rb_ref, slot_ref, gw_ref, cnt_ref, chunk)

    @pl.when(t < nt)
    def _():
        pending = router()
        _inproj_body(x_ref, eg_ref, eb_ref, w_ref, ws_ref, lb_ref, gb_ref, wst_ref, gbt_ref,
                     h1_ref, p_scr, lf_scr, sg_scr, sgt_scr, pending)
        while pending:
            pending.pop(0)()
        _hgrn_body(p_scr.at[P_QA], p_scr.at[P_KA], p_scr.at[P_IA], p_scr.at[P_GA], lf_scr, hng_ref, ya_scr, s_scr,
                   chunk)
        _mlstm_body(p_scr.at[P_QKB], p_scr.at[P_VB], p_scr.at[P_OB], sg_scr, sgt_scr, cw_ref, cb_ref, mng_ref,
                    yb_scr, c_scr, m_scr, x_scr, chunk)
        _merge_body(ya_scr, yb_scr, p_scr.at[P_MA], p_scr.at[P_MB], wa_ref, wb_ref, wo_ref, g1_ref, b1_ref,
                    h1_ref, hp_scr)

    @pl.when(t == nt)
    def _():
        for piece in router():
            piece()

    @pl.when(t == nt - 1)
    def _():
        sfin_ref[...] = s_scr[...]
        cfin_ref[...] = c_scr[...]
        mfin_ref[...] = m_scr[...]
        xfin_ref[...] = x_scr[...]


def _front(x2d, weights, states, nb, chunk):
    m = x2d.shape[0]
    nt = m // chunk
    s0, c0, m0, x0 = states
    cur_rows = lambda t: (jnp.minimum(t, nt - 1), 0)
    prev_lanes = lambda t: (0, jnp.maximum(t - 1, 0))
    const2 = lambda t: (0, 0)
    const3 = lambda t: (0, 0, 0)

    def resident(a):
        return pl.BlockSpec(a.shape, const2 if a.ndim == 2 else const3, pipeline_mode=pl.Buffered(1))

    outs = pl.pallas_call(
        functools.partial(_front_kernel, chunk=chunk, chunks_per_seq=nt // nb),
        grid=(nt + 1,),
        in_specs=[pl.BlockSpec((chunk, D_MODEL), cur_rows)] + [resident(a) for a in weights[:8]]
        + [resident(weights[8]), resident(s0)] + [resident(a) for a in weights[9:12]]
        + [resident(c0), resident(m0), resident(x0)] + [resident(a) for a in weights[12:]],
        out_specs=[
            pl.BlockSpec((chunk, D_MODEL), cur_rows),
            pl.BlockSpec((TOP_K, chunk), prev_lanes),
            pl.BlockSpec((TOP_K, chunk), prev_lanes),
            pl.BlockSpec((None, N_EXPERTS, 128), lambda t: (jnp.maximum(t - 1, 0), 0, 0)),
            pl.BlockSpec(s0.shape, const3),
            pl.BlockSpec(c0.shape, const3),
            pl.BlockSpec(m0.shape, const2),
            pl.BlockSpec(x0.shape, const2),
        ],
        out_shape=[
            jax.ShapeDtypeStruct((m, D_MODEL), F32),
            jax.ShapeDtypeStruct((TOP_K, m), I32),
            jax.ShapeDtypeStruct((TOP_K, m), F32),
            jax.ShapeDtypeStruct((nt, N_EXPERTS, 128), F32),
            jax.ShapeDtypeStruct(s0.shape, F32),
            jax.ShapeDtypeStruct(c0.shape, F32),
            jax.ShapeDtypeStruct(m0.shape, F32),
            jax.ShapeDtypeStruct(x0.shape, F32),
        ],
        scratch_shapes=[
            pltpu.VMEM((N_SLABS, chunk, D_MODEL), BF16),
            pltpu.VMEM((chunk, D_MODEL), F32),
            pltpu.VMEM((chunk, 128), F32),
            pltpu.VMEM((SUBLANES, chunk), F32),
            pltpu.VMEM((chunk, D_MODEL), BF16),
            pltpu.VMEM((chunk, D_MODEL), BF16),
            pltpu.VMEM((chunk, D_MODEL), F32),
            pltpu.VMEM((HG_HEADS, HG_DK, HG_DK), F32),
            pltpu.VMEM((ML_HEADS, ML_DK, ML_AUG), F32),
            pltpu.VMEM((SUBLANES, 128), F32),
            pltpu.VMEM((SUBLANES, D_MODEL), F32),
        ],
        compiler_params=_params("arbitrary"),
        name="front",
    )(x2d, *weights[:9], s0, *weights[9:12], c0, m0, x0, *weights[12:])
    return outs[0], outs[1], outs[2], outs[3], tuple(outs[4:])


def _tile_slots(tt):
    return -(-(TOP_K * tt + N_EXPERTS * (GROUP_ROWS - 1)) // SLOT_BLOCK) * SLOT_BLOCK


def _wait_groups(n, make_copy, s_tile):
    p = 1 << ((s_tile // GROUP_ROWS).bit_length() - 1)
    while p:
        @pl.when((n & p) != 0)
        def _():
            make_copy(p * GROUP_ROWS).wait()
        p >>= 1


def _for_slot_blocks(n, tt, s_tile, body):
    always = TOP_K * tt // SLOT_BLOCK
    for r in range(always):
        body(r)
    for r in range(always, s_tile // SLOT_BLOCK):
        @pl.when(n * GROUP_ROWS > r * SLOT_BLOCK)
        def _():
            body(r)


def _issue_copies(n2, n1, plan_ref, make_copy, s_tile):
    max2 = s_tile // (2 * GROUP_ROWS)
    base1 = 2 * max2

    def start2(p, c):
        make_copy(plan_ref[0, 0, p], plan_ref[0, 0, max2 + p], 2 * GROUP_ROWS).start()
        return c

    def start1(q, c):
        make_copy(plan_ref[0, 0, base1 + q], plan_ref[0, 0, base1 + N_EXPERTS + q], GROUP_ROWS).start()
        return c

    lax.fori_loop(0, n2, start2, 0)
    lax.fori_loop(0, n1, start1, 0)


def _plan_len(s_tile):
    return 2 * (s_tile // (2 * GROUP_ROWS)) + 2 * N_EXPERTS


def _dispatch_kernel(ng_ref, n2_ref, n1_ref, plan_ref, slot_ref, h_ref, xs_ref, buf, sem, *, tt, s_tile):
    i = pl.program_id(0)
    cur = lax.rem(i, 2)

    def list_to_global(list_row, global_row, rows):
        src = buf.at[cur, pl.ds(pl.multiple_of(list_row, GROUP_ROWS), rows), :]
        dst = xs_ref.at[pl.ds(pl.multiple_of(global_row, GROUP_ROWS), rows), :]
        return pltpu.make_async_copy(src, dst, sem.at[cur])

    def wait_tile(j, b):
        _wait_groups(ng_ref[j], lambda rows: pltpu.make_async_copy(
            buf.at[b, pl.ds(0, rows), :], xs_ref.at[pl.ds(0, rows), :], sem.at[b]), s_tile)

    @pl.when(i >= 2)
    def _():
        wait_tile(i - 2, cur)

    hb = h_ref[...].astype(BF16)
    sl = slot_ref[...]

    def fill_block(r):
        s_iota = lax.broadcasted_iota(I32, (SLOT_BLOCK, tt), 0) + r * SLOT_BLOCK
        p = jnp.zeros((SLOT_BLOCK, tt), F32)
        for kk in range(TOP_K):
            p = jnp.where(s_iota == sl[kk:kk + 1, :], 1.0, p)
        buf[cur, r * SLOT_BLOCK:(r + 1) * SLOT_BLOCK, :] = _dot(p.astype(BF16), hb).astype(BF16)

    _for_slot_blocks(ng_ref[i], tt, s_tile, fill_block)

    _issue_copies(n2_ref[i], n1_ref[i], plan_ref, list_to_global, s_tile)

    @pl.when(i == pl.num_programs(0) - 1)
    def _():
        @pl.when(i >= 1)
        def _():
            wait_tile(i - 1, 1 - cur)
        wait_tile(i, cur)


def _dispatch(counts, plan, slot_k, h1, n_slots, tt):
    m = h1.shape[0]
    s_tile = _tile_slots(tt)
    grid_spec = pltpu.PrefetchScalarGridSpec(
        num_scalar_prefetch=3,
        grid=(m // tt,),
        in_specs=[
            pl.BlockSpec((1, 1, _plan_len(s_tile)), lambda i, *_: (i, 0, 0), memory_space=pltpu.SMEM),
            pl.BlockSpec((TOP_K, tt), lambda i, *_: (0, i)),
            pl.BlockSpec((tt, D_MODEL), lambda i, *_: (i, 0)),
        ],
        out_specs=pl.BlockSpec(memory_space=pl.ANY),
        scratch_shapes=[pltpu.VMEM((2, s_tile, D_MODEL), BF16), pltpu.SemaphoreType.DMA((2,))],
    )
    return pl.pallas_call(
        functools.partial(_dispatch_kernel, tt=tt, s_tile=s_tile),
        grid_spec=grid_spec,
        out_shape=jax.ShapeDtypeStruct((n_slots, D_MODEL), BF16),
        compiler_params=_params("arbitrary"),
        name="dispatch",
    )(*counts, plan, slot_k, h1)


def _experts_kernel(be_ref, nu_ref, x_ref, wg_ref, wu_ref, wd_ref, y_ref, wg_b, wu_b, wd_b):
    i = pl.program_id(0)
    used = i < nu_ref[0]

    @pl.when(used & ((i == 0) | (be_ref[i] != be_ref[jnp.maximum(i - 1, 0)])))
    def _():
        wg_b[...] = wg_ref[...].astype(BF16)
        wu_b[...] = wu_ref[...].astype(BF16)
        wd_b[...] = wd_ref[...].astype(BF16)

    @pl.when(used)
    def _():
        xb = x_ref[...]
        a = _dot(xb, wg_b[...])
        u = _dot(xb, wu_b[...])
        y_ref[...] = _dot((a * _sigmoid(a) * u).astype(BF16), wd_b[...]).astype(BF16)


def _experts(blk_expert, n_used, xs, wg, wu, wd):
    n_slots = xs.shape[0]
    n_blocks = n_slots // MOE_BLOCK
    blk = lambda i, be, nu: (jnp.minimum(i, nu[0] - 1), 0)
    wsel = lambda i, be, nu: (be[jnp.minimum(i, nu[0] - 1)], 0, 0)
    grid_spec = pltpu.PrefetchScalarGridSpec(
        num_scalar_prefetch=2,
        grid=(n_blocks,),
        in_specs=[
            pl.BlockSpec((MOE_BLOCK, D_MODEL), blk),
            pl.BlockSpec((None, D_MODEL, D_EXPERT), wsel),
            pl.BlockSpec((None, D_MODEL, D_EXPERT), wsel),
            pl.BlockSpec((None, D_EXPERT, D_MODEL), wsel),
        ],
        out_specs=pl.BlockSpec((MOE_BLOCK, D_MODEL), blk),
        scratch_shapes=[pltpu.VMEM((D_MODEL, D_EXPERT), BF16), pltpu.VMEM((D_MODEL, D_EXPERT), BF16),
                        pltpu.VMEM((D_EXPERT, D_MODEL), BF16)],
    )
    return pl.pallas_call(
        _experts_kernel,
        grid_spec=grid_spec,
        out_shape=jax.ShapeDtypeStruct((n_slots, D_MODEL), BF16),
        compiler_params=_params("arbitrary"),
        name="experts",
    )(blk_expert, n_used, xs, wg, wu, wd)


def _combine_kernel(ng_ref, n2_ref, n1_ref, gcur_ref, gnext_ref, slot_ref, gw_ref, h_ref, y_ref, sg_ref, su_ref, sd_ref,
                    g2_ref, b2_ref, o_ref, ybuf, acc, sem, *, tt, s_tile):
    i = pl.program_id(0)
    cur = lax.rem(i, 2)

    def fetch(j, b, plan_ref):
        def global_to_list(list_row, global_row, rows):
            src = y_ref.at[pl.ds(pl.multiple_of(global_row, GROUP_ROWS), rows), :]
            dst = ybuf.at[b, pl.ds(pl.multiple_of(list_row, GROUP_ROWS), rows), :]
            return pltpu.make_async_copy(src, dst, sem.at[b])

        _issue_copies(n2_ref[j], n1_ref[j], plan_ref, global_to_list, s_tile)

    @pl.when(i == 0)
    def _():
        ybuf[...] = jnp.zeros_like(ybuf)
        fetch(0, 0, gcur_ref)

    @pl.when(i + 1 < pl.num_programs(0))
    def _():
        fetch(i + 1, 1 - cur, gnext_ref)

    h1 = h_ref[...]
    hb = h1.astype(BF16)
    a = _dot(hb, sg_ref[...])
    u = _dot(hb, su_ref[...])
    shared = _dot((a * _sigmoid(a) * u).astype(BF16), sd_ref[...])

    slot = slot_ref[...]
    gw = gw_ref[...]
    _wait_groups(ng_ref[i], lambda rows: pltpu.make_async_copy(
        y_ref.at[pl.ds(0, rows), :], ybuf.at[cur, pl.ds(0, rows), :], sem.at[cur]), s_tile)

    def block_dot(r):
        lane = lax.broadcasted_iota(I32, (tt, SLOT_BLOCK), 1) + r * SLOT_BLOCK
        p = jnp.zeros((tt, SLOT_BLOCK), F32)
        for kk in range(TOP_K):
            p = jnp.where(lane == slot[:, kk:kk + 1], gw[:, kk:kk + 1], p)
        return _dot(p.astype(BF16), ybuf[cur, r * SLOT_BLOCK:(r + 1) * SLOT_BLOCK, :])

    always = TOP_K * tt // SLOT_BLOCK
    routed = block_dot(0)
    for r in range(1, always):
        routed = routed + block_dot(r)
    acc[...] = DN_ALPHA * h1 + (routed + shared)
    for r in range(always, s_tile // SLOT_BLOCK):
        @pl.when(ng_ref[i] * GROUP_ROWS > r * SLOT_BLOCK)
        def _():
            acc[...] += block_dot(r)
    o_ref[...] = _layer_norm(acc[...], g2_ref[...], b2_ref[...])


def _combine(counts, plan, slot_t, gw_t, h1, y, wsg, wsu, wsd, g2, b2, tt):
    m = h1.shape[0]
    nt = m // tt
    s_tile = _tile_slots(tt)
    const = lambda i, *_: (0, 0)
    table = lambda f: pl.BlockSpec((1, 1, _plan_len(s_tile)), f, memory_space=pltpu.SMEM)
    grid_spec = pltpu.PrefetchScalarGridSpec(
        num_scalar_prefetch=3,
        grid=(nt,),
        in_specs=[
            table(lambda i, *_: (i, 0, 0)),
            table(lambda i, *_: (jnp.minimum(i + 1, nt - 1), 0, 0)),
            pl.BlockSpec((tt, TOP_K), lambda i, *_: (i, 0)),
            pl.BlockSpec((tt, TOP_K), lambda i, *_: (i, 0)),
            pl.BlockSpec((tt, D_MODEL), lambda i, *_: (i, 0)),
            pl.BlockSpec(memory_space=pl.ANY),
            pl.BlockSpec((D_MODEL, D_EXPERT), const),
            pl.BlockSpec((D_MODEL, D_EXPERT), const),
            pl.BlockSpec((D_EXPERT, D_MODEL), const),
            pl.BlockSpec((1, D_MODEL), const),
            pl.BlockSpec((1, D_MODEL), const),
        ],
        out_specs=pl.BlockSpec((tt, D_MODEL), lambda i, *_: (i, 0)),
        scratch_shapes=[pltpu.VMEM((2, s_tile, D_MODEL), BF16), pltpu.VMEM((tt, D_MODEL), F32),
                        pltpu.SemaphoreType.DMA((2,))],
    )
    return pl.pallas_call(
        functools.partial(_combine_kernel, tt=tt, s_tile=s_tile),
        grid_spec=grid_spec,
        out_shape=jax.ShapeDtypeStruct((m, D_MODEL), F32),
        compiler_params=_params("arbitrary"),
        name="combine",
    )(*counts, plan, plan, slot_t, gw_t, h1, y, wsg, wsu, wsd, g2, b2)


def _pick_tile(m, pref):
    t = min(pref, m)
    while m % t:
        t //= 2
    return t


def _forward(x, meta_tokens, ln_emb_g, ln_emb_b, w_in, hg_lb_logits, hg_norm_g, ml_conv_w, ml_conv_b,
             ml_ig_bias, ml_fg_bias, ml_norm_g, w_branch_a, w_branch_b, w_out, ln1_g, ln1_b,
             w_router, router_bias, w_exp_gate, w_exp_up, w_exp_down, w_sh_gate, w_sh_up, w_sh_down,
             ln2_g, ln2_b, *, chunk):
    nb, seq, d = x.shape
    m = nb * seq
    row = lambda a: a.reshape(1, -1).astype(F32)

    w = w_in[0]
    kw = HG_HEADS * HG_DK
    o_qa, o_fa, o_ia, o_ga = 0, kw, 2 * kw, 3 * kw
    o_qb = 4 * kw
    o_kb = o_qb + ML_HEADS * ML_DK
    o_vb = o_kb + ML_HEADS * ML_DK
    o_ob = o_vb + ML_HEADS * ML_DV
    o_ig = o_ob + ML_HEADS * ML_DV
    o_fg = o_ig + ML_HEADS
    o_ma = o_fg + ML_HEADS
    o_mb = o_ma + D_MODEL
    cols = lambda o, n: w[:, o:o + n]
    w_cat = jnp.concatenate([
        cols(o_qa, kw), cols(o_fa, kw), cols(o_ia, kw), cols(o_ga, kw),
        cols(o_qb, 2 * ML_HEADS * ML_DK), cols(o_vb, ML_HEADS * ML_DV), cols(o_ob, ML_HEADS * ML_DV),
        cols(o_ma, D_MODEL), cols(o_mb, D_MODEL)], axis=1).astype(BF16)
    w_small = jnp.pad(cols(o_ig, 2 * ML_HEADS), ((0, 0), (0, 128 - 2 * ML_HEADS))).astype(BF16)
    gate_bias = jnp.pad(jnp.concatenate([ml_ig_bias[0], ml_fg_bias[0]]).astype(F32), (0, 128 - 2 * ML_HEADS)).reshape(1, 128)
    lb = jax.nn.softmax(hg_lb_logits.astype(F32), axis=0)[0].reshape(1, -1)
    eg, eb = row(ln_emb_g), row(ln_emb_b)
    conv_w = ml_conv_w[0].astype(F32)
    conv_b = row(ml_conv_b[0])
    hgn, mln = row(hg_norm_g[0]), row(ml_norm_g[0])
    w_small_t = jnp.pad(cols(o_ig, 2 * ML_HEADS).T, ((0, GROUP_ROWS - 2 * ML_HEADS), (0, 0))).astype(BF16)
    gate_bias_t = jnp.broadcast_to(jnp.pad(gate_bias[0, :2 * ML_HEADS], (0, GROUP_ROWS - 2 * ML_HEADS))[:, None],
                                   (GROUP_ROWS, 128))
    wr = w_router[0].T.astype(F32)
    wr_hi, wr_lo = _split_bf16(wr)
    rbias = jnp.broadcast_to(router_bias[0].astype(F32).reshape(N_EXPERTS, 1), (N_EXPERTS, 128))
    weights = [eg, eb, w_cat, w_small, lb, gate_bias, w_small_t, gate_bias_t, hgn, conv_w, conv_b, mln,
               w_branch_a[0].astype(BF16), w_branch_b[0].astype(BF16), w_out[0].astype(BF16),
               row(ln1_g[0]), row(ln1_b[0]), wr_hi, wr_lo, rbias]

    zero_states = (jnp.zeros((HG_HEADS, HG_DK, HG_DK), F32), jnp.zeros((ML_HEADS, ML_DK, ML_AUG), F32),
                   jnp.zeros((SUBLANES, 128), F32), jnp.zeros((SUBLANES, D_MODEL), F32))
    meta_states = _front(meta_tokens.astype(F32), weights, zero_states, 1, N_META)[4]

    x2d = x.reshape(m, d).astype(F32)
    tt = chunk
    h1, slot_k, gw, cnt, _ = _front(x2d, weights, meta_states, nb, chunk)

    nt = m // tt
    s_tile = _tile_slots(tt)
    cnt8 = (cnt[:, :, 0].astype(I32) + GROUP_ROWS - 1) // GROUP_ROWS * GROUP_ROWS
    seg_end = jnp.cumsum(cnt8, axis=1)
    seg_off = seg_end - cnt8
    tile_rows = seg_end[:, -1]
    run = jnp.cumsum(cnt8, axis=0) - cnt8
    tot8 = jnp.sum(cnt8, axis=0)
    padded = (tot8 + MOE_BLOCK - 1) // MOE_BLOCK * MOE_BLOCK
    pends = jnp.cumsum(padded)
    gshift = (pends - padded)[None, :] + run - seg_off
    experts = jnp.arange(N_EXPERTS, dtype=I32)

    def copy_list(per_expert, length, list_row0, step):
        ends = jnp.cumsum(per_expert, axis=1)
        idx = jnp.arange(length, dtype=I32)
        owner = jnp.sum((ends[:, None, :] <= idx[None, :, None]).astype(I32), axis=-1)
        pick = lambda a: jnp.sum(jnp.where(owner[..., None] == experts, a[:, None, :], 0), axis=-1)
        list_row = pick(list_row0 - step * (ends - per_expert)) + step * idx[None, :]
        valid = idx[None, :] < ends[:, -1:]
        return jnp.where(valid, list_row, 0), jnp.where(valid, list_row + pick(gshift), 0), ends[:, -1]

    groups = cnt8 // GROUP_ROWS
    doubles = groups // 2
    l2, g2, n2 = copy_list(doubles, s_tile // (2 * GROUP_ROWS), seg_off, 2 * GROUP_ROWS)
    l1, g1, n1 = copy_list(groups % 2, N_EXPERTS, seg_off + 2 * GROUP_ROWS * doubles, 0)
    plan = jnp.concatenate([l2, g2, l1, g1], axis=1).astype(I32).reshape(nt, 1, -1)
    counts = ((tile_rows // GROUP_ROWS).astype(I32), n2.astype(I32), n1.astype(I32))
    n_blocks = -(-(m * TOP_K + nt * N_EXPERTS * (GROUP_ROWS - 1)) // MOE_BLOCK) + N_EXPERTS
    blk_start = jnp.arange(n_blocks, dtype=I32) * MOE_BLOCK
    blk_expert = jnp.minimum(jnp.sum((pends[None, :] <= blk_start[:, None]).astype(I32), axis=1), N_EXPERTS - 1)
    n_used = (pends[-1:] // MOE_BLOCK).astype(I32)

    xs = _dispatch(counts, plan, slot_k, h1, n_blocks * MOE_BLOCK, tt)
    y = _experts(blk_expert, n_used, xs, w_exp_gate[0].astype(F32), w_exp_up[0].astype(F32),
                 w_exp_down[0].astype(F32))
    out = _combine(counts, plan, slot_k.T, gw.T, h1, y, w_sh_gate[0].astype(BF16), w_sh_up[0].astype(BF16),
                   w_sh_down[0].astype(BF16), row(ln2_g[0]), row(ln2_b[0]), tt)
    return out.reshape(nb, seq, d).astype(x.dtype)


def kernel(x, meta_tokens, ln_emb_g, ln_emb_b, w_in, hg_lb_logits, hg_norm_g, ml_conv_w, ml_conv_b, ml_ig_bias, ml_fg_bias, ml_norm_g, w_branch_a, w_branch_b, w_out, ln1_g, ln1_b, w_router, router_bias, w_exp_gate, w_exp_up, w_exp_down, w_sh_gate, w_sh_up, w_sh_down, ln2_g, ln2_b):
    return _forward(x, meta_tokens, ln_emb_g, ln_emb_b, w_in, hg_lb_logits, hg_norm_g, ml_conv_w, ml_conv_b,
                    ml_ig_bias, ml_fg_bias, ml_norm_g, w_branch_a, w_branch_b, w_out, ln1_g, ln1_b,
                    w_router, router_bias, w_exp_gate, w_exp_up, w_exp_down, w_sh_gate, w_sh_up, w_sh_down,
                    ln2_g, ln2_b, chunk=_pick_tile(x.shape[1], 256))
```

```python
import functools

import jax
import jax.numpy as jnp
from jax import lax
from jax.experimental import pallas as pl
from jax.experimental.pallas import tpu as pltpu

F32, BF16, I32 = jnp.float32, jnp.bfloat16, jnp.int32

D_MODEL = 1024
N_META = 16
HG_HEADS = 8
HG_DK = 128
ML_HEADS = 4
ML_DK = 128
ML_DV = 256
ML_AUG = ML_DV + 128
N_EXPERTS = 64
TOP_K = 8
N_GROUPS = 8
GROUP_SIZE = N_EXPERTS // N_GROUPS
TOPK_GROUPS = 4
D_EXPERT = 256
ROUTED_SCALE = 2.5
MOE_BLOCK = 1024
SLOT_BLOCK = 512
DN_ALPHA = 2.0 ** 0.25
EPS = 1e-5
LOG2E = 1.4426950408889634
EXP2_CLAMP = 115.0
SUBLANES = 8
GROUP_ROWS = 16

P_QA, P_KA, P_IA, P_GA, P_QKB, P_VB, P_OB, P_MA, P_MB = range(9)
N_SLABS = 9

VMEM_LIMIT = 56 * 1024 * 1024


def _params(*sem):
    return pltpu.CompilerParams(dimension_semantics=sem, vmem_limit_bytes=VMEM_LIMIT)


def _sigmoid(x):
    return 1.0 / (1.0 + jnp.exp(-x))


def _log_sigmoid(x):
    return jnp.minimum(x, 0.0) - jnp.log(1.0 + jnp.exp(-jnp.abs(x)))


def _layer_norm(x, g, b):
    xc = x - jnp.mean(x, axis=-1, keepdims=True)
    var = jnp.mean(xc * xc, axis=-1, keepdims=True)
    return xc * lax.rsqrt(var + EPS) * g + b


def _dot(a, b):
    return jnp.dot(a, b, preferred_element_type=F32)


def _dot_nt(a, b):
    return lax.dot_general(a, b, (((1,), (1,)), ((), ())), preferred_element_type=F32)


def _dot_tn(a, b):
    return lax.dot_general(a, b, (((0,), (0,)), ((), ())), preferred_element_type=F32)


def _split_bf16(x):
    hi = x.astype(BF16)
    lo = (x - hi.astype(F32)).astype(BF16)
    return hi, lo


def _neg_abs(x):
    return lax.bitcast_convert_type(lax.bitcast_convert_type(x, I32) | jnp.int32(-2 ** 31), F32)


def _ones_where(cond):
    return jnp.where(cond, 1.0, 0.0).astype(BF16)


def _inproj_pieces(x_ref, g_ref, b_ref, w_ref, ws_ref, lb_ref, gb_ref, wst_ref, gbt_ref,
                   h0_ref, p_ref, lf_ref, sg_ref, sgt_ref):
    h0 = _layer_norm(x_ref[...], g_ref[...], b_ref[...])
    h0_ref[...] = h0
    hb = h0.astype(BF16)

    def slab(n):
        def run():
            acc = _dot(hb, w_ref[:, n * D_MODEL:(n + 1) * D_MODEL])
            if n == P_KA:
                lb = lb_ref[...]
                f = lb + (1.0 - lb) * _sigmoid(acc)
                lf_ref[...] = jnp.log(f)
                acc = 1.0 - f
            elif n == P_GA:
                acc = acc * _sigmoid(acc)
            elif n >= P_OB:
                acc = _sigmoid(acc)
            p_ref[n] = acc.astype(BF16)
        return run

    def gates():
        s = _dot(hb, ws_ref[...]) + gb_ref[...]
        lane = lax.broadcasted_iota(I32, s.shape, 1)
        sg_ref[...] = jnp.where(lane < ML_HEADS, s, _log_sigmoid(s))
        st = _dot_nt(wst_ref[...], hb) + gbt_ref[:, 0:1]
        srow = lax.broadcasted_iota(I32, st.shape, 0)
        sgt_ref[...] = jnp.where(srow < ML_HEADS, st, _log_sigmoid(st))[:SUBLANES, :]

    return [slab(n) for n in range(N_SLABS)], gates


def _block_rows(b, block, pick):
    c, w = b.shape
    parts = [jnp.broadcast_to(b[j * block + pick:j * block + pick + 1, :], (block, w))
             for j in range(c // block)]
    return parts[0] if len(parts) == 1 else jnp.concatenate(parts, axis=0)


def _hgrn_body(q_ref, k_ref, v_ref, g_ref, lf_ref, ng_ref, y_ref, s_scr, cs):
    row = lax.broadcasted_iota(I32, (cs, cs), 0)
    col = lax.broadcasted_iota(I32, (cs, cs), 1)
    tri = _ones_where(col <= row)
    lf_hi, lf_lo = _split_bf16(lf_ref[...])
    b = (_dot(tri, lf_hi) + _dot(tri, lf_lo)) * LOG2E
    q = q_ref[...]
    k = k_ref[...]
    v = v_ref[...]
    blast = b[cs - 1:cs, :]
    qg = q * jnp.exp2(b).astype(BF16)
    kg = k * jnp.exp2(blast - b).astype(BF16)
    dec = jnp.exp2(blast)

    levels = []
    m = SUBLANES
    while 2 * m <= cs:
        w = jnp.exp2(_neg_abs(b - _block_rows(b, 2 * m, m - 1))).astype(BF16)
        sh = (2 * m).bit_length() - 1
        mask = ((row >> sh) == (col >> sh)) & ((row & (2 * m - 1)) >= m) & ((col & (2 * m - 1)) < m)
        levels.append((q * w, k * w, mask))
        m *= 2
    e = jnp.clip(b - _block_rows(b, SUBLANES, SUBLANES // 2 - 1), -EXP2_CLAMP, EXP2_CLAMP)
    levels.append((q * jnp.exp2(e).astype(BF16), k * jnp.exp2(-e).astype(BF16),
                   ((row >> 3) == (col >> 3)) & (col <= row)))

    ng = ng_ref[...]
    for h in range(HG_HEADS):
        sl = slice(h * HG_DK, (h + 1) * HG_DK)
        st = s_scr[h]
        o = _dot_nt(qg[:, sl], st.astype(BF16))
        sc = jnp.zeros((cs, cs), F32)
        for lq, lk, mask in levels:
            sc = jnp.where(mask, _dot_nt(lq[:, sl], lk[:, sl]), sc)
        o = o + _dot(sc.astype(BF16), v[:, sl])
        s_scr[h] = dec[:, sl] * st + _dot_tn(v[:, sl], kg[:, sl])
        ms = jnp.mean(o * o, axis=-1, keepdims=True)
        y = o * lax.rsqrt(ms + EPS) * ng[:, sl] * g_ref[:, sl].astype(F32)
        y_ref[:, sl] = y.astype(BF16)


def _mlstm_conv(qk_ref, cw_ref, cb_ref, x_scr, cs):
    x = qk_ref[...].astype(F32)
    prev = x_scr[...]
    sub = lax.broadcasted_iota(I32, (SUBLANES, D_MODEL), 0)
    cw = cw_ref[...]
    conv = cw[3:4, :] * x + cb_ref[...]
    for j in (1, 2, 3):
        xs = pltpu.roll(x, j, 0)
        head = jnp.where(sub < j, pltpu.roll(prev, j, 0), xs[:SUBLANES, :])
        xs = jnp.concatenate([head, xs[SUBLANES:, :]], axis=0)
        conv = conv + cw[3 - j:4 - j, :] * xs
    x_scr[...] = x[cs - SUBLANES:, :]
    qk = conv * _sigmoid(conv)
    return (qk[:, :ML_HEADS * ML_DK] * (ML_DK ** -0.5)).astype(BF16), qk[:, ML_HEADS * ML_DK:]


def _mlstm_cumsums(sg_ref, sgt_ref, cs):
    row = lax.broadcasted_iota(I32, (cs, cs), 0)
    col = lax.broadcasted_iota(I32, (cs, cs), 1)
    tri = _ones_where(col <= row)
    sg_hi, sg_lo = _split_bf16(sg_ref[...])
    sgt_hi, sgt_lo = _split_bf16(sgt_ref[...])
    return _dot(tri, sg_hi) + _dot(tri, sg_lo), _dot_nt(sgt_hi, tri) + _dot_nt(sgt_lo, tri)


def _mlstm_heads(q_all, k_all, bcol_all, brow_all, v_ref, og_ref, sg_ref, sgt_ref, ng_ref, y_ref, c_scr, m_scr, cs):
    row = lax.broadcasted_iota(I32, (cs, cs), 0)
    col = lax.broadcasted_iota(I32, (cs, cs), 1)
    causal = col <= row
    sg = sg_ref[...]
    sgt = sgt_ref[...]
    lane128 = lax.broadcasted_iota(I32, (cs, 128), 1)
    ones_col = _ones_where(lane128 == 0)
    v = v_ref[...]
    ng = ng_ref[...]

    for h in range(ML_HEADS):
        b_col = bcol_all[:, ML_HEADS + h:ML_HEADS + h + 1]
        b_row = brow_all[ML_HEADS + h:ML_HEADS + h + 1, :]
        ig_col = sg[:, h:h + 1]
        ig_row = sgt[h:h + 1, :]
        m_prev = m_scr[h:h + 1, 0:1]
        q_h = q_all[:, h * ML_DK:(h + 1) * ML_DK]
        k_h = k_all[:, h * ML_DK:(h + 1) * ML_DK]
        v_aug = jnp.concatenate([v[:, h * ML_DV:(h + 1) * ML_DV], ones_col], axis=1)
        c_st = c_scr[h]

        log_intra = jnp.where(causal, b_col - b_row + ig_row, -jnp.inf)
        log_inter = b_col + m_prev
        m_t = jnp.maximum(log_inter, jnp.max(log_intra, axis=-1, keepdims=True))
        w_intra = jnp.exp(log_intra - m_t)
        w_inter = jnp.exp(log_inter - m_t)
        s = _dot_nt(q_h, k_h.astype(BF16)) * w_intra
        tot = w_inter * _dot(q_h, c_st.astype(BF16)) + _dot(s.astype(BF16), v_aug)
        num = tot[:, :ML_DV]
        den = tot[:, ML_DV:ML_DV + 1]
        hid = num / jnp.maximum(jnp.abs(den), jnp.exp(-m_t))

        b_last = b_col[cs - 1:cs, :]
        log_w = b_last - b_col + ig_col
        m_new = jnp.maximum(b_last + m_prev, jnp.max(log_w, axis=0, keepdims=True))
        w_s = jnp.exp(log_w - m_new)
        decay = jnp.exp(b_last + m_prev - m_new)
        c_scr[h] = decay * c_st + _dot_tn((k_h * w_s).astype(BF16), v_aug)
        m_scr[h:h + 1, :] = jnp.broadcast_to(m_new, (1, 128))

        hc = hid - jnp.mean(hid, axis=-1, keepdims=True)
        var = jnp.mean(hc * hc, axis=-1, keepdims=True)
        sl = slice(h * ML_DV, (h + 1) * ML_DV)
        y = hc * lax.rsqrt(var + EPS) * ng[:, sl] * og_ref[:, sl].astype(F32)
        y_ref[:, sl] = y.astype(BF16)


def _merge_body(h0_ref, ya_ref, yb_ref, ma_ref, mb_ref, wa_ref, wb_ref, wo_ref, pre_ref):
    merged = (ma_ref[...].astype(F32) * _dot(ya_ref[...], wa_ref[...])
              + mb_ref[...].astype(F32) * _dot(yb_ref[...], wb_ref[...]))
    pre_ref[...] = DN_ALPHA * h0_ref[...] + _dot(merged.astype(BF16), wo_ref[...])


def _router_pieces(pre_ref, g1_ref, b1_ref, wrh_ref, wrl_ref, rb_ref, h1_ref, slot_ref, gw_ref, cnt_ref, tm):
    neg_inf = -jnp.inf
    v = {}

    def ln1():
        v['h1'] = _layer_norm(pre_ref[...], g1_ref[...], b1_ref[...])
        h1_ref[...] = v['h1']

    def logits():
        h_hi, h_lo = _split_bf16(v['h1'])
        lg = _dot_nt(wrh_ref[...], h_hi) + _dot_nt(wrh_ref[...], h_lo) + _dot_nt(wrl_ref[...], h_hi)
        v['scores'] = _sigmoid(lg)
        v['biased'] = v['scores'] + rb_ref[:, 0:1]

    def group_scores():
        g3 = v['biased'].reshape(N_GROUPS, GROUP_SIZE, tm)
        sub3 = lax.broadcasted_iota(I32, g3.shape, 1)
        top1 = jnp.max(g3, axis=1, keepdims=True)
        first = jnp.min(jnp.where(g3 == top1, sub3, GROUP_SIZE), axis=1, keepdims=True)
        top2 = jnp.max(jnp.where(sub3 == first, neg_inf, g3), axis=1, keepdims=True)
        v['gs'] = (top1 + top2).reshape(N_GROUPS, tm)

    def group_select():
        gs = v['gs']
        gi = lax.broadcasted_iota(I32, gs.shape, 0)
        grank = jnp.zeros(gs.shape, F32)
        for j in range(N_GROUPS):
            r = gs[j:j + 1, :]
            grank = grank + jnp.where((r > gs) | ((r == gs) & (gi > j)), 1.0, 0.0)
        gsel = grank < float(TOPK_GROUPS)
        emask = jnp.broadcast_to(gsel.reshape(N_GROUPS, 1, tm), (N_GROUPS, GROUP_SIZE, tm)).reshape(N_EXPERTS, tm)
        v['work'] = jnp.where(emask, v['biased'], neg_inf)
        v['rank'] = jnp.full((N_EXPERTS, tm), float(N_EXPERTS), F32)

    def extract(kk):
        def run():
            ei = lax.broadcasted_iota(I32, (N_EXPERTS, tm), 0)
            for k2 in (kk, kk + 1):
                work = v['work']
                top = jnp.max(work, axis=0, keepdims=True)
                first = jnp.min(jnp.where(work == top, ei, N_EXPERTS), axis=0, keepdims=True)
                hit = ei == first
                v['rank'] = jnp.where(hit, float(k2), v['rank'])
                v['work'] = jnp.where(hit, neg_inf, work)
        return run

    def slots():
        sel = v['rank'] < float(TOP_K)
        sel_w = jnp.where(sel, v['scores'], 0.0)
        v['gwd'] = sel_w / jnp.sum(sel_w, axis=0, keepdims=True) * ROUTED_SCALE
        tr = lax.broadcasted_iota(I32, (tm, tm), 0)
        tc = lax.broadcasted_iota(I32, (tm, tm), 1)
        sel_b = _ones_where(sel)
        rloc = _dot(sel_b, _ones_where(tr < tc))
        cnt = _dot(sel_b, jnp.ones((tm, 128), BF16))
        cnt_g = jnp.floor((cnt + (GROUP_ROWS - 1.0)) * (1.0 / GROUP_ROWS)) * GROUP_ROWS
        er = lax.broadcasted_iota(I32, (N_EXPERTS, N_EXPERTS), 0)
        ec = lax.broadcasted_iota(I32, (N_EXPERTS, N_EXPERTS), 1)
        seg_start = _dot(_ones_where(ec < er), cnt_g.astype(BF16))
        v['slot_e'] = seg_start[:, 0:1] + rloc
        v['sel'] = sel
        cnt_ref[...] = cnt

    def picks():
        s_rows, w_rows = [], []
        for kk in range(TOP_K):
            pick = v['sel'] & (v['rank'] == float(kk))
            s_rows.append(jnp.sum(jnp.where(pick, v['slot_e'], 0.0), axis=0, keepdims=True))
            w_rows.append(jnp.sum(jnp.where(pick, v['gwd'], 0.0), axis=0, keepdims=True))
        slot_ref[...] = jnp.concatenate(s_rows, axis=0).astype(I32)
        gw_ref[...] = jnp.concatenate(w_rows, axis=0)

    return [ln1, logits, group_scores, group_select] + [extract(kk) for kk in range(0, TOP_K, 2)] + [slots, picks]


def _front_kernel(x_ref, eg_ref, eb_ref, w_ref, ws_ref, lb_ref, gb_ref, wst_ref, gbt_ref,
                  hng_ref, s0_ref, cw_ref, cb_ref, mng_ref, c0_ref, m0_ref, x0_ref,
                  wa_ref, wb_ref, wo_ref, g1_ref, b1_ref, wrh_ref, wrl_ref, rb_ref,
                  h1_ref, slot_ref, gw_ref, cnt_ref, sfin_ref, cfin_ref, mfin_ref, xfin_ref,
                  p_scr, lf_scr, sg_scr, sgt_scr, ya_scr, yb_scr, h0_scr, hp_scr, s_scr, c_scr, m_scr, x_scr,
                  *, chunk, chunks_per_seq):
    t = pl.program_id(0)
    nt = pl.num_programs(0) - 1

    @pl.when(lax.rem(t, chunks_per_seq) == 0)
    def _():
        s_scr[...] = s0_ref[...]
        c_scr[...] = c0_ref[...]
        m_scr[...] = m0_ref[...]
        x_scr[...] = x0_ref[...]

    @pl.when(t == 0)
    def _():
        hp_scr[...] = jnp.zeros_like(hp_scr)

    def router():
        return _router_pieces(hp_scr, g1_ref, b1_ref, wrh_ref, wrl_ref, rb_ref, h1_ref, slot_ref, gw_ref, cnt_ref,
                              chunk)

    @pl.when(t < nt)
    def _():
        slabs, gates = _inproj_pieces(x_ref, eg_ref, eb_ref, w_ref, ws_ref, lb_ref, gb_ref, wst_ref, gbt_ref,
                                      h0_scr, p_scr, lf_scr, sg_scr, sgt_scr)
        ml = {}

        def conv():
            ml['q'], ml['k'] = _mlstm_conv(p_scr.at[P_QKB], cw_ref, cb_ref, x_scr, chunk)

        def cumsums():
            ml['bcol'], ml['brow'] = _mlstm_cumsums(sg_scr, sgt_scr, chunk)

        r = router()
        light = [[], [r[0]], [conv, r[1]], [cumsums, r[2]]] + [[p] for p in r[3:8]] + [r[8:]]
        heavy = [slabs[P_QKB], gates] + [slabs[n] for n in range(N_SLABS) if n != P_QKB]
        for piece, fill in zip(heavy, light):
            piece()
            for f in fill:
                f()
        _hgrn_body(p_scr.at[P_QA], p_scr.at[P_KA], p_scr.at[P_IA], p_scr.at[P_GA], lf_scr, hng_ref, ya_scr, s_scr,
                   chunk)
        _mlstm_heads(ml['q'], ml['k'], ml['bcol'], ml['brow'], p_scr.at[P_VB], p_scr.at[P_OB], sg_scr, sgt_scr,
                     mng_ref, yb_scr, c_scr, m_scr, chunk)
        _merge_body(h0_scr, ya_scr, yb_scr, p_scr.at[P_MA], p_scr.at[P_MB], wa_ref, wb_ref, wo_ref, hp_scr)

    @pl.when(t == nt)
    def _():
        for piece in router():
            piece()

    @pl.when(t == nt - 1)
    def _():
        sfin_ref[...] = s_scr[...]
        cfin_ref[...] = c_scr[...]
        mfin_ref[...] = m_scr[...]
        xfin_ref[...] = x_scr[...]


def _front(x2d, weights, states, nb, chunk):
    m = x2d.shape[0]
    nt = m // chunk
    s0, c0, m0, x0 = states
    cur_rows = lambda t: (jnp.minimum(t, nt - 1), 0)
    prev_lanes = lambda t: (0, jnp.maximum(t - 1, 0))
    const2 = lambda t: (0, 0)
    const3 = lambda t: (0, 0, 0)

    def resident(a):
        return pl.BlockSpec(a.shape, const2 if a.ndim == 2 else const3, pipeline_mode=pl.Buffered(1))

    outs = pl.pallas_call(
        functools.partial(_front_kernel, chunk=chunk, chunks_per_seq=nt // nb),
        grid=(nt + 1,),
        in_specs=[pl.BlockSpec((chunk, D_MODEL), cur_rows)] + [resident(a) for a in weights[:8]]
        + [resident(weights[8]), resident(s0)] + [resident(a) for a in weights[9:12]]
        + [resident(c0), resident(m0), resident(x0)] + [resident(a) for a in weights[12:]],
        out_specs=[
            pl.BlockSpec((chunk, D_MODEL), lambda t: (jnp.maximum(t - 1, 0), 0)),
            pl.BlockSpec((TOP_K, chunk), prev_lanes),
            pl.BlockSpec((TOP_K, chunk), prev_lanes),
            pl.BlockSpec((None, N_EXPERTS, 128), lambda t: (jnp.maximum(t - 1, 0), 0, 0)),
            pl.BlockSpec(s0.shape, const3),
            pl.BlockSpec(c0.shape, const3),
            pl.BlockSpec(m0.shape, const2),
            pl.BlockSpec(x0.shape, const2),
        ],
        out_shape=[
            jax.ShapeDtypeStruct((m, D_MODEL), F32),
            jax.ShapeDtypeStruct((TOP_K, m), I32),
            jax.ShapeDtypeStruct((TOP_K, m), F32),
            jax.ShapeDtypeStruct((nt, N_EXPERTS, 128), F32),
            jax.ShapeDtypeStruct(s0.shape, F32),
            jax.ShapeDtypeStruct(c0.shape, F32),
            jax.ShapeDtypeStruct(m0.shape, F32),
            jax.ShapeDtypeStruct(x0.shape, F32),
        ],
        scratch_shapes=[
            pltpu.VMEM((N_SLABS, chunk, D_MODEL), BF16),
            pltpu.VMEM((chunk, D_MODEL), F32),
            pltpu.VMEM((chunk, 128), F32),
            pltpu.VMEM((SUBLANES, chunk), F32),
            pltpu.VMEM((chunk, D_MODEL), BF16),
            pltpu.VMEM((chunk, D_MODEL), BF16),
            pltpu.VMEM((chunk, D_MODEL), F32),
            pltpu.VMEM((chunk, D_MODEL), F32),
            pltpu.VMEM((HG_HEADS, HG_DK, HG_DK), F32),
            pltpu.VMEM((ML_HEADS, ML_DK, ML_AUG), F32),
            pltpu.VMEM((SUBLANES, 128), F32),
            pltpu.VMEM((SUBLANES, D_MODEL), F32),
        ],
        compiler_params=_params("arbitrary"),
        name="front",
    )(x2d, *weights[:9], s0, *weights[9:12], c0, m0, x0, *weights[12:])
    return outs[0], outs[1], outs[2], outs[3], tuple(outs[4:])


def _tile_slots(tt):
    return -(-(TOP_K * tt + N_EXPERTS * (GROUP_ROWS - 1)) // SLOT_BLOCK) * SLOT_BLOCK


def _wait_groups(n, make_copy, s_tile):
    p = 1 << ((s_tile // GROUP_ROWS).bit_length() - 1)
    while p:
        @pl.when((n & p) != 0)
        def _():
            make_copy(p * GROUP_ROWS).wait()
        p >>= 1


def _for_slot_blocks(n, tt, s_tile, body):
    always = TOP_K * tt // SLOT_BLOCK
    for r in range(always):
        body(r)
    for r in range(always, s_tile // SLOT_BLOCK):
        @pl.when(n * GROUP_ROWS > r * SLOT_BLOCK)
        def _():
            body(r)


def _issue_copies(n2, n1, plan_ref, make_copy, s_tile):
    max2 = s_tile // (2 * GROUP_ROWS)
    base1 = 2 * max2

    def start2(p, c):
        make_copy(plan_ref[0, 0, p], plan_ref[0, 0, max2 + p], 2 * GROUP_ROWS).start()
        return c

    def start1(q, c):
        make_copy(plan_ref[0, 0, base1 + q], plan_ref[0, 0, base1 + N_EXPERTS + q], GROUP_ROWS).start()
        return c

    lax.fori_loop(0, n2, start2, 0)
    lax.fori_loop(0, n1, start1, 0)


def _plan_len(s_tile):
    return 2 * (s_tile // (2 * GROUP_ROWS)) + 2 * N_EXPERTS


def _dispatch_kernel(ng_ref, n2_ref, n1_ref, plan_ref, slot_ref, h_ref, xs_ref, buf, sem, *, tt, s_tile):
    i = pl.program_id(0)
    cur = lax.rem(i, 2)

    def list_to_global(list_row, global_row, rows):
        src = buf.at[cur, pl.ds(pl.multiple_of(list_row, GROUP_ROWS), rows), :]
        dst = xs_ref.at[pl.ds(pl.multiple_of(global_row, GROUP_ROWS), rows), :]
        return pltpu.make_async_copy(src, dst, sem.at[cur])

    def wait_tile(j, b):
        _wait_groups(ng_ref[j], lambda rows: pltpu.make_async_copy(
            buf.at[b, pl.ds(0, rows), :], xs_ref.at[pl.ds(0, rows), :], sem.at[b]), s_tile)

    @pl.when(i >= 2)
    def _():
        wait_tile(i - 2, cur)

    hb = h_ref[...].astype(BF16)
    sl = slot_ref[...]

    def fill_block(r):
        s_iota = lax.broadcasted_iota(I32, (SLOT_BLOCK, tt), 0) + r * SLOT_BLOCK
        p = jnp.zeros((SLOT_BLOCK, tt), F32)
        for kk in range(TOP_K):
            p = jnp.where(s_iota == sl[kk:kk + 1, :], 1.0, p)
        buf[cur, r * SLOT_BLOCK:(r + 1) * SLOT_BLOCK, :] = _dot(p.astype(BF16), hb).astype(BF16)

    _for_slot_blocks(ng_ref[i], tt, s_tile, fill_block)

    _issue_copies(n2_ref[i], n1_ref[i], plan_ref, list_to_global, s_tile)

    @pl.when(i == pl.num_programs(0) - 1)
    def _():
        @pl.when(i >= 1)
        def _():
            wait_tile(i - 1, 1 - cur)
        wait_tile(i, cur)


def _dispatch(counts, plan, slot_k, h1, n_slots, tt):
    m = h1.shape[0]
    s_tile = _tile_slots(tt)
    grid_spec = pltpu.PrefetchScalarGridSpec(
        num_scalar_prefetch=3,
        grid=(m // tt,),
        in_specs=[
            pl.BlockSpec((1, 1, _plan_len(s_tile)), lambda i, *_: (i, 0, 0), memory_space=pltpu.SMEM),
            pl.BlockSpec((TOP_K, tt), lambda i, *_: (0, i)),
            pl.BlockSpec((tt, D_MODEL), lambda i, *_: (i, 0)),
        ],
        out_specs=pl.BlockSpec(memory_space=pl.ANY),
        scratch_shapes=[pltpu.VMEM((2, s_tile, D_MODEL), BF16), pltpu.SemaphoreType.DMA((2,))],
    )
    return pl.pallas_call(
        functools.partial(_dispatch_kernel, tt=tt, s_tile=s_tile),
        grid_spec=grid_spec,
        out_shape=jax.ShapeDtypeStruct((n_slots, D_MODEL), BF16),
        compiler_params=_params("arbitrary"),
        name="dispatch",
    )(*counts, plan, slot_k, h1)


def _experts_kernel(be_ref, nu_ref, x_ref, wg_ref, wu_ref, wd_ref, y_ref, wg_b, wu_b, wd_b):
    i = pl.program_id(0)
    used = i < nu_ref[0]

    @pl.when(used & ((i == 0) | (be_ref[i] != be_ref[jnp.maximum(i - 1, 0)])))
    def _():
        wg_b[...] = wg_ref[...].astype(BF16)
        wu_b[...] = wu_ref[...].astype(BF16)
        wd_b[...] = wd_ref[...].astype(BF16)

    @pl.when(used)
    def _():
        xb = x_ref[...]
        a = _dot(xb, wg_b[...])
        u = _dot(xb, wu_b[...])
        y_ref[...] = _dot((a * _sigmoid(a) * u).astype(BF16), wd_b[...]).astype(BF16)


def _experts(blk_expert, n_used, xs, wg, wu, wd):
    n_slots = xs.shape[0]
    n_blocks = n_slots // MOE_BLOCK
    blk = lambda i, be, nu: (jnp.minimum(i, nu[0] - 1), 0)
    wsel = lambda i, be, nu: (be[jnp.minimum(i, nu[0] - 1)], 0, 0)
    grid_spec = pltpu.PrefetchScalarGridSpec(
        num_scalar_prefetch=2,
        grid=(n_blocks,),
        in_specs=[
            pl.BlockSpec((MOE_BLOCK, D_MODEL), blk),
            pl.BlockSpec((None, D_MODEL, D_EXPERT), wsel),
            pl.BlockSpec((None, D_MODEL, D_EXPERT), wsel),
            pl.BlockSpec((None, D_EXPERT, D_MODEL), wsel),
        ],
        out_specs=pl.BlockSpec((MOE_BLOCK, D_MODEL), blk),
        scratch_shapes=[pltpu.VMEM((D_MODEL, D_EXPERT), BF16), pltpu.VMEM((D_MODEL, D_EXPERT), BF16),
                        pltpu.VMEM((D_EXPERT, D_MODEL), BF16)],
    )
    return pl.pallas_call(
        _experts_kernel,
        grid_spec=grid_spec,
        out_shape=jax.ShapeDtypeStruct((n_slots, D_MODEL), BF16),
        compiler_params=_params("arbitrary"),
        name="experts",
    )(blk_expert, n_used, xs, wg, wu, wd)


def _combine_kernel(ng_ref, n2_ref, n1_ref, gcur_ref, gnext_ref, slot_ref, gw_ref, h_ref, y_ref, sg_ref, su_ref, sd_ref,
                    g2_ref, b2_ref, o_ref, ybuf, acc, sem, *, tt, s_tile):
    i = pl.program_id(0)
    cur = lax.rem(i, 2)

    def fetch(j, b, plan_ref):
        def global_to_list(list_row, global_row, rows):
            src = y_ref.at[pl.ds(pl.multiple_of(global_row, GROUP_ROWS), rows), :]
            dst = ybuf.at[b, pl.ds(pl.multiple_of(list_row, GROUP_ROWS), rows), :]
            return pltpu.make_async_copy(src, dst, sem.at[b])

        _issue_copies(n2_ref[j], n1_ref[j], plan_ref, global_to_list, s_tile)

    @pl.when(i == 0)
    def _():
        ybuf[...] = jnp.zeros_like(ybuf)
        fetch(0, 0, gcur_ref)

    @pl.when(i + 1 < pl.num_programs(0))
    def _():
        fetch(i + 1, 1 - cur, gnext_ref)

    h1 = h_ref[...]
    hb = h1.astype(BF16)
    a = _dot(hb, sg_ref[...])
    u = _dot(hb, su_ref[...])
    shared = _dot((a * _sigmoid(a) * u).astype(BF16), sd_ref[...])

    slot = slot_ref[...]
    gw = gw_ref[...]
    _wait_groups(ng_ref[i], lambda rows: pltpu.make_async_copy(
        y_ref.at[pl.ds(0, rows), :], ybuf.at[cur, pl.ds(0, rows), :], sem.at[cur]), s_tile)

    def block_dot(r):
        lane = lax.broadcasted_iota(I32, (tt, SLOT_BLOCK), 1) + r * SLOT_BLOCK
        p = jnp.zeros((tt, SLOT_BLOCK), F32)
        for kk in range(TOP_K):
            p = jnp.where(lane == slot[:, kk:kk + 1], gw[:, kk:kk + 1], p)
        return _dot(p.astype(BF16), ybuf[cur, r * SLOT_BLOCK:(r + 1) * SLOT_BLOCK, :])

    always = TOP_K * tt // SLOT_BLOCK
    routed = block_dot(0)
    for r in range(1, always):
        routed = routed + block_dot(r)
    acc[...] = DN_ALPHA * h1 + (routed + shared)
    for r in range(always, s_tile // SLOT_BLOCK):
        @pl.when(ng_ref[i] * GROUP_ROWS > r * SLOT_BLOCK)
        def _():
            acc[...] += block_dot(r)
    o_ref[...] = _layer_norm(acc[...], g2_ref[...], b2_ref[...])


def _combine(counts, plan, slot_t, gw_t, h1, y, wsg, wsu, wsd, g2, b2, tt):
    m = h1.shape[0]
    nt = m // tt
    s_tile = _tile_slots(tt)
    const = lambda i, *_: (0, 0)
    table = lambda f: pl.BlockSpec((1, 1, _plan_len(s_tile)), f, memory_space=pltpu.SMEM)
    grid_spec = pltpu.PrefetchScalarGridSpec(
        num_scalar_prefetch=3,
        grid=(nt,),
        in_specs=[
            table(lambda i, *_: (i, 0, 0)),
            table(lambda i, *_: (jnp.minimum(i + 1, nt - 1), 0, 0)),
            pl.BlockSpec((tt, TOP_K), lambda i, *_: (i, 0)),
            pl.BlockSpec((tt, TOP_K), lambda i, *_: (i, 0)),
            pl.BlockSpec((tt, D_MODEL), lambda i, *_: (i, 0)),
            pl.BlockSpec(memory_space=pl.ANY),
            pl.BlockSpec((D_MODEL, D_EXPERT), const),
            pl.BlockSpec((D_MODEL, D_EXPERT), const),
            pl.BlockSpec((D_EXPERT, D_MODEL), const),
            pl.BlockSpec((1, D_MODEL), const),
            pl.BlockSpec((1, D_MODEL), const),
        ],
        out_specs=pl.BlockSpec((tt, D_MODEL), lambda i, *_: (i, 0)),
        scratch_shapes=[pltpu.VMEM((2, s_tile, D_MODEL), BF16), pltpu.VMEM((tt, D_MODEL), F32),
                        pltpu.SemaphoreType.DMA((2,))],
    )
    return pl.pallas_call(
        functools.partial(_combine_kernel, tt=tt, s_tile=s_tile),
        grid_spec=grid_spec,
        out_shape=jax.ShapeDtypeStruct((m, D_MODEL), F32),
        compiler_params=_params("arbitrary"),
        name="combine",
    )(*counts, plan, plan, slot_t, gw_t, h1, y, wsg, wsu, wsd, g2, b2)


def _pick_tile(m, pref):
    t = min(pref, m)
    while m % t:
        t //= 2
    return t


def _forward(x, meta_tokens, ln_emb_g, ln_emb_b, w_in, hg_lb_logits, hg_norm_g, ml_conv_w, ml_conv_b,
             ml_ig_bias, ml_fg_bias, ml_norm_g, w_branch_a, w_branch_b, w_out, ln1_g, ln1_b,
             w_router, router_bias, w_exp_gate, w_exp_up, w_exp_down, w_sh_gate, w_sh_up, w_sh_down,
             ln2_g, ln2_b, *, chunk):
    nb, seq, d = x.shape
    m = nb * seq
    row = lambda a: a.reshape(1, -1).astype(F32)

    w = w_in[0]
    kw = HG_HEADS * HG_DK
    o_qa, o_fa, o_ia, o_ga = 0, kw, 2 * kw, 3 * kw
    o_qb = 4 * kw
    o_kb = o_qb + ML_HEADS * ML_DK
    o_vb = o_kb + ML_HEADS * ML_DK
    o_ob = o_vb + ML_HEADS * ML_DV
    o_ig = o_ob + ML_HEADS * ML_DV
    o_fg = o_ig + ML_HEADS
    o_ma = o_fg + ML_HEADS
    o_mb = o_ma + D_MODEL
    cols = lambda o, n: w[:, o:o + n]
    w_cat = jnp.concatenate([
        cols(o_qa, kw), cols(o_fa, kw), cols(o_ia, kw), cols(o_ga, kw),
        cols(o_qb, 2 * ML_HEADS * ML_DK), cols(o_vb, ML_HEADS * ML_DV), cols(o_ob, ML_HEADS * ML_DV),
        cols(o_ma, D_MODEL), cols(o_mb, D_MODEL)], axis=1).astype(BF16)
    w_small = jnp.pad(cols(o_ig, 2 * ML_HEADS), ((0, 0), (0, 128 - 2 * ML_HEADS))).astype(BF16)
    gate_bias = jnp.pad(jnp.concatenate([ml_ig_bias[0], ml_fg_bias[0]]).astype(F32), (0, 128 - 2 * ML_HEADS)).reshape(1, 128)
    lb = jax.nn.softmax(hg_lb_logits.astype(F32), axis=0)[0].reshape(1, -1)
    eg, eb = row(ln_emb_g), row(ln_emb_b)
    conv_w = ml_conv_w[0].astype(F32)
    conv_b = row(ml_conv_b[0])
    hgn, mln = row(hg_norm_g[0]), row(ml_norm_g[0])
    w_small_t = jnp.pad(cols(o_ig, 2 * ML_HEADS).T, ((0, GROUP_ROWS - 2 * ML_HEADS), (0, 0))).astype(BF16)
    gate_bias_t = jnp.broadcast_to(jnp.pad(gate_bias[0, :2 * ML_HEADS], (0, GROUP_ROWS - 2 * ML_HEADS))[:, None],
                                   (GROUP_ROWS, 128))
    wr = w_router[0].T.astype(F32)
    wr_hi, wr_lo = _split_bf16(wr)
    rbias = jnp.broadcast_to(router_bias[0].astype(F32).reshape(N_EXPERTS, 1), (N_EXPERTS, 128))
    weights = [eg, eb, w_cat, w_small, lb, gate_bias, w_small_t, gate_bias_t, hgn, conv_w, conv_b, mln,
               w_branch_a[0].astype(BF16), w_branch_b[0].astype(BF16), w_out[0].astype(BF16),
               row(ln1_g[0]), row(ln1_b[0]), wr_hi, wr_lo, rbias]

    zero_states = (jnp.zeros((HG_HEADS, HG_DK, HG_DK), F32), jnp.zeros((ML_HEADS, ML_DK, ML_AUG), F32),
                   jnp.zeros((SUBLANES, 128), F32), jnp.zeros((SUBLANES, D_MODEL), F32))
    meta_states = _front(meta_tokens.astype(F32), weights, zero_states, 1, N_META)[4]

    x2d = x.reshape(m, d).astype(F32)
    tt = chunk
    h1, slot_k, gw, cnt, _ = _front(x2d, weights, meta_states, nb, chunk)

    nt = m // tt
    s_tile = _tile_slots(tt)
    cnt8 = (cnt[:, :, 0].astype(I32) + GROUP_ROWS - 1) // GROUP_ROWS * GROUP_ROWS
    seg_end = jnp.cumsum(cnt8, axis=1)
    seg_off = seg_end - cnt8
    tile_rows = seg_end[:, -1]
    run = jnp.cumsum(cnt8, axis=0) - cnt8
    tot8 = jnp.sum(cnt8, axis=0)
    padded = (tot8 + MOE_BLOCK - 1) // MOE_BLOCK * MOE_BLOCK
    pends = jnp.cumsum(padded)
    gshift = (pends - padded)[None, :] + run - seg_off
    experts = jnp.arange(N_EXPERTS, dtype=I32)

    def copy_list(per_expert, length, list_row0, step):
        ends = jnp.cumsum(per_expert, axis=1)
        idx = jnp.arange(length, dtype=I32)
        owner = jnp.sum((ends[:, None, :] <= idx[None, :, None]).astype(I32), axis=-1)
        pick = lambda a: jnp.sum(jnp.where(owner[..., None] == experts, a[:, None, :], 0), axis=-1)
        list_row = pick(list_row0 - step * (ends - per_expert)) + step * idx[None, :]
        valid = idx[None, :] < ends[:, -1:]
        return jnp.where(valid, list_row, 0), jnp.where(valid, list_row + pick(gshift), 0), ends[:, -1]

    groups = cnt8 // GROUP_ROWS
    doubles = groups // 2
    l2, g2, n2 = copy_list(doubles, s_tile // (2 * GROUP_ROWS), seg_off, 2 * GROUP_ROWS)
    l1, g1, n1 = copy_list(groups % 2, N_EXPERTS, seg_off + 2 * GROUP_ROWS * doubles, 0)
    plan = jnp.concatenate([l2, g2, l1, g1], axis=1).astype(I32).reshape(nt, 1, -1)
    counts = ((tile_rows // GROUP_ROWS).astype(I32), n2.astype(I32), n1.astype(I32))
    n_blocks = -(-(m * TOP_K + nt * N_EXPERTS * (GROUP_ROWS - 1)) // MOE_BLOCK) + N_EXPERTS
    blk_start = jnp.arange(n_blocks, dtype=I32) * MOE_BLOCK
    blk_expert = jnp.minimum(jnp.sum((pends[None, :] <= blk_start[:, None]).astype(I32), axis=1), N_EXPERTS - 1)
    n_used = (pends[-1:] // MOE_BLOCK).astype(I32)

    xs = _dispatch(counts, plan, slot_k, h1, n_blocks * MOE_BLOCK, tt)
    y = _experts(blk_expert, n_used, xs, w_exp_gate[0].astype(F32), w_exp_up[0].astype(F32),
                 w_exp_down[0].astype(F32))
    out = _combine(counts, plan, slot_k.T, gw.T, h1, y, w_sh_gate[0].astype(BF16), w_sh_up[0].astype(BF16),
                   w_sh_down[0].astype(BF16), row(ln2_g[0]), row(ln2_b[0]), tt)
    return out.reshape(nb, seq, d).astype(x.dtype)


def kernel(x, meta_tokens, ln_emb_g, ln_emb_b, w_in, hg_lb_logits, hg_norm_g, ml_conv_w, ml_conv_b, ml_ig_bias, ml_fg_bias, ml_norm_g, w_branch_a, w_branch_b, w_out, ln1_g, ln1_b, w_router, router_bias, w_exp_gate, w_exp_up, w_exp_down, w_sh_gate, w_sh_up, w_sh_down, ln2_g, ln2_b):
    return _forward(x, meta_tokens, ln_emb_g, ln_emb_b, w_in, hg_lb_logits, hg_norm_g, ml_conv_w, ml_conv_b,
                    ml_ig_bias, ml_fg_bias, ml_norm_g, w_branch_a, w_branch_b, w_out, ln1_g, ln1_b,
                    w_router, router_bias, w_exp_gate, w_exp_up, w_exp_down, w_sh_gate, w_sh_up, w_sh_down,
                    ln2_g, ln2_b, chunk=_pick_tile(x.shape[1], 256))
```

```python
import functools

import jax
import jax.numpy as jnp
from jax import lax
from jax.experimental import pallas as pl
from jax.experimental.pallas import tpu as pltpu

F32, BF16, I32 = jnp.float32, jnp.bfloat16, jnp.int32

D_MODEL = 1024
N_META = 16
HG_HEADS = 8
HG_DK = 128
ML_HEADS = 4
ML_DK = 128
ML_DV = 256
ML_AUG = ML_DV + 128
N_EXPERTS = 64
TOP_K = 8
N_GROUPS = 8
GROUP_SIZE = N_EXPERTS // N_GROUPS
TOPK_GROUPS = 4
D_EXPERT = 256
ROUTED_SCALE = 2.5
MOE_BLOCK = 1024
SLOT_BLOCK = 256
DN_ALPHA = 2.0 ** 0.25
EPS = 1e-5
LOG2E = 1.4426950408889634
EXP2_CLAMP = 115.0
SUBLANES = 8
GROUP_ROWS = 16

P_QA, P_KA, P_IA, P_GA, P_QKB, P_VB, P_OB, P_MA, P_MB = range(9)
N_SLABS = 9

VMEM_LIMIT = 56 * 1024 * 1024


def _params(*sem):
    return pltpu.CompilerParams(dimension_semantics=sem, vmem_limit_bytes=VMEM_LIMIT)


def _sigmoid(x):
    return 1.0 / (1.0 + jnp.exp(-x))


def _log_sigmoid(x):
    return jnp.minimum(x, 0.0) - jnp.log(1.0 + jnp.exp(-jnp.abs(x)))


def _layer_norm(x, g, b):
    xc = x - jnp.mean(x, axis=-1, keepdims=True)
    var = jnp.mean(xc * xc, axis=-1, keepdims=True)
    return xc * lax.rsqrt(var + EPS) * g + b


def _dot(a, b):
    return jnp.dot(a, b, preferred_element_type=F32)


def _dot_nt(a, b):
    return lax.dot_general(a, b, (((1,), (1,)), ((), ())), preferred_element_type=F32)


def _dot_tn(a, b):
    return lax.dot_general(a, b, (((0,), (0,)), ((), ())), preferred_element_type=F32)


def _split_bf16(x):
    hi = x.astype(BF16)
    lo = (x - hi.astype(F32)).astype(BF16)
    return hi, lo


def _neg_abs(x):
    return lax.bitcast_convert_type(lax.bitcast_convert_type(x, I32) | jnp.int32(-2 ** 31), F32)


def _ones_where(cond):
    return jnp.where(cond, 1.0, 0.0).astype(BF16)


def _inproj_pieces(x_ref, g_ref, b_ref, w_ref, ws_ref, lb_ref, gb_ref, wst_ref, gbt_ref,
                   h0_ref, p_ref, lf_ref, sg_ref, sgt_ref):
    h0 = _layer_norm(x_ref[...], g_ref[...], b_ref[...])
    h0_ref[...] = h0
    hb = h0.astype(BF16)

    def slab(n):
        def run():
            acc = _dot(hb, w_ref[:, n * D_MODEL:(n + 1) * D_MODEL])
            if n == P_KA:
                lb = lb_ref[...]
                f = lb + (1.0 - lb) * _sigmoid(acc)
                lf_ref[...] = jnp.log(f)
                acc = 1.0 - f
            elif n == P_GA:
                acc = acc * _sigmoid(acc)
            elif n >= P_OB:
                acc = _sigmoid(acc)
            p_ref[n] = acc.astype(BF16)
        return run

    def gates():
        s = _dot(hb, ws_ref[...]) + gb_ref[...]
        lane = lax.broadcasted_iota(I32, s.shape, 1)
        sg_ref[...] = jnp.where(lane < ML_HEADS, s, _log_sigmoid(s))
        st = _dot_nt(wst_ref[...], hb) + gbt_ref[:, 0:1]
        srow = lax.broadcasted_iota(I32, st.shape, 0)
        sgt_ref[...] = jnp.where(srow < ML_HEADS, st, _log_sigmoid(st))[:SUBLANES, :]

    return [slab(n) for n in range(N_SLABS)], gates


def _block_rows(b, block, pick):
    c, w = b.shape
    parts = [jnp.broadcast_to(b[j * block + pick:j * block + pick + 1, :], (block, w))
             for j in range(c // block)]
    return parts[0] if len(parts) == 1 else jnp.concatenate(parts, axis=0)


def _hgrn_body(q_ref, k_ref, v_ref, g_ref, lf_ref, ng_ref, y_ref, s_scr, cs):
    row = lax.broadcasted_iota(I32, (cs, cs), 0)
    col = lax.broadcasted_iota(I32, (cs, cs), 1)
    tri = _ones_where(col <= row)
    lf_hi, lf_lo = _split_bf16(lf_ref[...])
    b = (_dot(tri, lf_hi) + _dot(tri, lf_lo)) * LOG2E
    q = q_ref[...]
    k = k_ref[...]
    v = v_ref[...]
    blast = b[cs - 1:cs, :]
    qg = q * jnp.exp2(b).astype(BF16)
    kg = k * jnp.exp2(blast - b).astype(BF16)
    dec = jnp.exp2(blast)

    levels = []
    m = SUBLANES
    while 2 * m <= cs:
        w = jnp.exp2(_neg_abs(b - _block_rows(b, 2 * m, m - 1))).astype(BF16)
        sh = (2 * m).bit_length() - 1
        mask = ((row >> sh) == (col >> sh)) & ((row & (2 * m - 1)) >= m) & ((col & (2 * m - 1)) < m)
        levels.append((q * w, k * w, mask))
        m *= 2
    e = jnp.clip(b - _block_rows(b, SUBLANES, SUBLANES // 2 - 1), -EXP2_CLAMP, EXP2_CLAMP)
    levels.append((q * jnp.exp2(e).astype(BF16), k * jnp.exp2(-e).astype(BF16),
                   ((row >> 3) == (col >> 3)) & (col <= row)))

    ng = ng_ref[...]
    for h in range(HG_HEADS):
        sl = slice(h * HG_DK, (h + 1) * HG_DK)
        st = s_scr[h]
        o = _dot_nt(qg[:, sl], st.astype(BF16))
        sc = jnp.zeros((cs, cs), F32)
        for lq, lk, mask in levels:
            sc = jnp.where(mask, _dot_nt(lq[:, sl], lk[:, sl]), sc)
        o = o + _dot(sc.astype(BF16), v[:, sl])
        s_scr[h] = dec[:, sl] * st + _dot_tn(v[:, sl], kg[:, sl])
        ms = jnp.mean(o * o, axis=-1, keepdims=True)
        y = o * lax.rsqrt(ms + EPS) * ng[:, sl] * g_ref[:, sl].astype(F32)
        y_ref[:, sl] = y.astype(BF16)


def _mlstm_conv(qk_ref, cw_ref, cb_ref, x_scr, cs):
    x = qk_ref[...].astype(F32)
    prev = x_scr[...]
    sub = lax.broadcasted_iota(I32, (SUBLANES, D_MODEL), 0)
    cw = cw_ref[...]
    conv = cw[3:4, :] * x + cb_ref[...]
    for j in (1, 2, 3):
        xs = pltpu.roll(x, j, 0)
        head = jnp.where(sub < j, pltpu.roll(prev, j, 0), xs[:SUBLANES, :])
        xs = jnp.concatenate([head, xs[SUBLANES:, :]], axis=0)
        conv = conv + cw[3 - j:4 - j, :] * xs
    x_scr[...] = x[cs - SUBLANES:, :]
    qk = conv * _sigmoid(conv)
    return (qk[:, :ML_HEADS * ML_DK] * (ML_DK ** -0.5)).astype(BF16), qk[:, ML_HEADS * ML_DK:]


def _mlstm_cumsums(sg_ref, sgt_ref, cs):
    row = lax.broadcasted_iota(I32, (cs, cs), 0)
    col = lax.broadcasted_iota(I32, (cs, cs), 1)
    tri = _ones_where(col <= row)
    sg_hi, sg_lo = _split_bf16(sg_ref[...])
    sgt_hi, sgt_lo = _split_bf16(sgt_ref[...])
    return _dot(tri, sg_hi) + _dot(tri, sg_lo), _dot_nt(sgt_hi, tri) + _dot_nt(sgt_lo, tri)


def _mlstm_heads(q_all, k_all, bcol_all, brow_all, v_ref, og_ref, sg_ref, sgt_ref, ng_ref, y_ref, c_scr, m_scr, cs):
    row = lax.broadcasted_iota(I32, (cs, cs), 0)
    col = lax.broadcasted_iota(I32, (cs, cs), 1)
    causal = col <= row
    sg = sg_ref[...]
    sgt = sgt_ref[...]
    lane128 = lax.broadcasted_iota(I32, (cs, 128), 1)
    ones_col = _ones_where(lane128 == 0)
    v = v_ref[...]
    ng = ng_ref[...]

    for h in range(ML_HEADS):
        b_col = bcol_all[:, ML_HEADS + h:ML_HEADS + h + 1]
        b_row = brow_all[ML_HEADS + h:ML_HEADS + h + 1, :]
        ig_col = sg[:, h:h + 1]
        ig_row = sgt[h:h + 1, :]
        m_prev = m_scr[h:h + 1, 0:1]
        q_h = q_all[:, h * ML_DK:(h + 1) * ML_DK]
        k_h = k_all[:, h * ML_DK:(h + 1) * ML_DK]
        v_aug = jnp.concatenate([v[:, h * ML_DV:(h + 1) * ML_DV], ones_col], axis=1)
        c_st = c_scr[h]

        log_intra = jnp.where(causal, b_col - b_row + ig_row, -jnp.inf)
        log_inter = b_col + m_prev
        m_t = jnp.maximum(log_inter, jnp.max(log_intra, axis=-1, keepdims=True))
        w_intra = jnp.exp(log_intra - m_t)
        w_inter = jnp.exp(log_inter - m_t)
        s = _dot_nt(q_h, k_h.astype(BF16)) * w_intra
        tot = w_inter * _dot(q_h, c_st.astype(BF16)) + _dot(s.astype(BF16), v_aug)
        num = tot[:, :ML_DV]
        den = tot[:, ML_DV:ML_DV + 1]
        hid = num / jnp.maximum(jnp.abs(den), jnp.exp(-m_t))

        b_last = b_col[cs - 1:cs, :]
        log_w = b_last - b_col + ig_col
        m_new = jnp.maximum(b_last + m_prev, jnp.max(log_w, axis=0, keepdims=True))
        w_s = jnp.exp(log_w - m_new)
        decay = jnp.exp(b_last + m_prev - m_new)
        c_scr[h] = decay * c_st + _dot_tn((k_h * w_s).astype(BF16), v_aug)
        m_scr[h:h + 1, :] = jnp.broadcast_to(m_new, (1, 128))

        hc = hid - jnp.mean(hid, axis=-1, keepdims=True)
        var = jnp.mean(hc * hc, axis=-1, keepdims=True)
        sl = slice(h * ML_DV, (h + 1) * ML_DV)
        y = hc * lax.rsqrt(var + EPS) * ng[:, sl] * og_ref[:, sl].astype(F32)
        y_ref[:, sl] = y.astype(BF16)


def _merge_body(h0_ref, ya_ref, yb_ref, ma_ref, mb_ref, wa_ref, wb_ref, wo_ref, pre_ref):
    merged = (ma_ref[...].astype(F32) * _dot(ya_ref[...], wa_ref[...])
              + mb_ref[...].astype(F32) * _dot(yb_ref[...], wb_ref[...]))
    pre_ref[...] = DN_ALPHA * h0_ref[...] + _dot(merged.astype(BF16), wo_ref[...])


def _router_pieces(pre_ref, g1_ref, b1_ref, wrh_ref, wrl_ref, rb_ref, h1_ref, slot_ref, gw_ref, cnt_ref, tm):
    neg_inf = -jnp.inf
    v = {}

    def ln1():
        v['h1'] = _layer_norm(pre_ref[...], g1_ref[...], b1_ref[...])
        h1_ref[...] = v['h1']

    def logits():
        h_hi, h_lo = _split_bf16(v['h1'])
        lg = _dot_nt(wrh_ref[...], h_hi) + _dot_nt(wrh_ref[...], h_lo) + _dot_nt(wrl_ref[...], h_hi)
        v['scores'] = _sigmoid(lg)
        v['biased'] = v['scores'] + rb_ref[:, 0:1]

    def group_scores():
        g3 = v['biased'].reshape(N_GROUPS, GROUP_SIZE, tm)
        sub3 = lax.broadcasted_iota(I32, g3.shape, 1)
        top1 = jnp.max(g3, axis=1, keepdims=True)
        first = jnp.min(jnp.where(g3 == top1, sub3, GROUP_SIZE), axis=1, keepdims=True)
        top2 = jnp.max(jnp.where(sub3 == first, neg_inf, g3), axis=1, keepdims=True)
        v['gs'] = (top1 + top2).reshape(N_GROUPS, tm)

    def group_select():
        gs = v['gs']
        gi = lax.broadcasted_iota(I32, gs.shape, 0)
        grank = jnp.zeros(gs.shape, F32)
        for j in range(N_GROUPS):
            r = gs[j:j + 1, :]
            grank = grank + jnp.where((r > gs) | ((r == gs) & (gi > j)), 1.0, 0.0)
        gsel = grank < float(TOPK_GROUPS)
        emask = jnp.broadcast_to(gsel.reshape(N_GROUPS, 1, tm), (N_GROUPS, GROUP_SIZE, tm)).reshape(N_EXPERTS, tm)
        v['work'] = jnp.where(emask, v['biased'], neg_inf)
        v['rank'] = jnp.full((N_EXPERTS, tm), float(N_EXPERTS), F32)

    def extract(kk):
        def run():
            ei = lax.broadcasted_iota(I32, (N_EXPERTS, tm), 0)
            for k2 in (kk, kk + 1):
                work = v['work']
                top = jnp.max(work, axis=0, keepdims=True)
                first = jnp.min(jnp.where(work == top, ei, N_EXPERTS), axis=0, keepdims=True)
                hit = ei == first
                v['rank'] = jnp.where(hit, float(k2), v['rank'])
                v['work'] = jnp.where(hit, neg_inf, work)
        return run

    def slots():
        sel = v['rank'] < float(TOP_K)
        sel_w = jnp.where(sel, v['scores'], 0.0)
        v['gwd'] = sel_w / jnp.sum(sel_w, axis=0, keepdims=True) * ROUTED_SCALE
        tr = lax.broadcasted_iota(I32, (tm, tm), 0)
        tc = lax.broadcasted_iota(I32, (tm, tm), 1)
        sel_b = _ones_where(sel)
        rloc = _dot(sel_b, _ones_where(tr < tc))
        cnt = _dot(sel_b, jnp.ones((tm, 128), BF16))
        cnt_g = jnp.floor((cnt + (GROUP_ROWS - 1.0)) * (1.0 / GROUP_ROWS)) * GROUP_ROWS
        er = lax.broadcasted_iota(I32, (N_EXPERTS, N_EXPERTS), 0)
        ec = lax.broadcasted_iota(I32, (N_EXPERTS, N_EXPERTS), 1)
        seg_start = _dot(_ones_where(ec < er), cnt_g.astype(BF16))
        v['slot_e'] = seg_start[:, 0:1] + rloc
        v['sel'] = sel
        cnt_ref[...] = cnt

    def picks():
        s_rows, w_rows = [], []
        for kk in range(TOP_K):
            pick = v['sel'] & (v['rank'] == float(kk))
            s_rows.append(jnp.sum(jnp.where(pick, v['slot_e'], 0.0), axis=0, keepdims=True))
            w_rows.append(jnp.sum(jnp.where(pick, v['gwd'], 0.0), axis=0, keepdims=True))
        slot_ref[...] = jnp.concatenate(s_rows, axis=0).astype(I32)
        gw_ref[...] = jnp.concatenate(w_rows, axis=0)

    return [ln1, logits, group_scores, group_select] + [extract(kk) for kk in range(0, TOP_K, 2)] + [slots, picks]


def _front_kernel(x_ref, eg_ref, eb_ref, w_ref, ws_ref, lb_ref, gb_ref, wst_ref, gbt_ref,
                  hng_ref, s0_ref, cw_ref, cb_ref, mng_ref, c0_ref, m0_ref, x0_ref,
                  wa_ref, wb_ref, wo_ref, g1_ref, b1_ref, wrh_ref, wrl_ref, rb_ref,
                  h1_ref, slot_ref, gw_ref, cnt_ref, sfin_ref, cfin_ref, mfin_ref, xfin_ref,
                  p_scr, lf_scr, sg_scr, sgt_scr, ya_scr, yb_scr, h0_scr, hp_scr, s_scr, c_scr, m_scr, x_scr,
                  *, chunk, chunks_per_seq):
    t = pl.program_id(0)
    nt = pl.num_programs(0) - 1

    @pl.when(lax.rem(t, chunks_per_seq) == 0)
    def _():
        s_scr[...] = s0_ref[...]
        c_scr[...] = c0_ref[...]
        m_scr[...] = m0_ref[...]
        x_scr[...] = x0_ref[...]

    @pl.when(t == 0)
    def _():
        hp_scr[...] = jnp.zeros_like(hp_scr)

    def router():
        return _router_pieces(hp_scr, g1_ref, b1_ref, wrh_ref, wrl_ref, rb_ref, h1_ref, slot_ref, gw_ref, cnt_ref,
                              chunk)

    @pl.when(t < nt)
    def _():
        slabs, gates = _inproj_pieces(x_ref, eg_ref, eb_ref, w_ref, ws_ref, lb_ref, gb_ref, wst_ref, gbt_ref,
                                      h0_scr, p_scr, lf_scr, sg_scr, sgt_scr)
        ml = {}

        def conv():
            ml['q'], ml['k'] = _mlstm_conv(p_scr.at[P_QKB], cw_ref, cb_ref, x_scr, chunk)

        def cumsums():
            ml['bcol'], ml['brow'] = _mlstm_cumsums(sg_scr, sgt_scr, chunk)

        r = router()
        light = [[], [r[0]], [conv, r[1]], [cumsums, r[2]]] + [[p] for p in r[3:8]] + [r[8:]]
        heavy = [slabs[P_QKB], gates] + [slabs[n] for n in range(N_SLABS) if n != P_QKB]
        for piece, fill in zip(heavy, light):
            piece()
            for f in fill:
                f()
        _hgrn_body(p_scr.at[P_QA], p_scr.at[P_KA], p_scr.at[P_IA], p_scr.at[P_GA], lf_scr, hng_ref, ya_scr, s_scr,
                   chunk)
        _mlstm_heads(ml['q'], ml['k'], ml['bcol'], ml['brow'], p_scr.at[P_VB], p_scr.at[P_OB], sg_scr, sgt_scr,
                     mng_ref, yb_scr, c_scr, m_scr, chunk)
        _merge_body(h0_scr, ya_scr, yb_scr, p_scr.at[P_MA], p_scr.at[P_MB], wa_ref, wb_ref, wo_ref, hp_scr)

    @pl.when(t == nt)
    def _():
        for piece in router():
            piece()

    @pl.when(t == nt - 1)
    def _():
        sfin_ref[...] = s_scr[...]
        cfin_ref[...] = c_scr[...]
        mfin_ref[...] = m_scr[...]
        xfin_ref[...] = x_scr[...]


def _front(x2d, weights, states, nb, chunk):
    m = x2d.shape[0]
    nt = m // chunk
    s0, c0, m0, x0 = states
    cur_rows = lambda t: (jnp.minimum(t, nt - 1), 0)
    prev_lanes = lambda t: (0, jnp.maximum(t - 1, 0))
    const2 = lambda t: (0, 0)
    const3 = lambda t: (0, 0, 0)

    def resident(a):
        return pl.BlockSpec(a.shape, const2 if a.ndim == 2 else const3, pipeline_mode=pl.Buffered(1))

    outs = pl.pallas_call(
        functools.partial(_front_kernel, chunk=chunk, chunks_per_seq=nt // nb),
        grid=(nt + 1,),
        in_specs=[pl.BlockSpec((chunk, D_MODEL), cur_rows)] + [resident(a) for a in weights[:8]]
        + [resident(weights[8]), resident(s0)] + [resident(a) for a in weights[9:12]]
        + [resident(c0), resident(m0), resident(x0)] + [resident(a) for a in weights[12:]],
        out_specs=[
            pl.BlockSpec((chunk, D_MODEL), lambda t: (jnp.maximum(t - 1, 0), 0)),
            pl.BlockSpec((TOP_K, chunk), prev_lanes),
            pl.BlockSpec((TOP_K, chunk), prev_lanes),
            pl.BlockSpec((None, N_EXPERTS, 128), lambda t: (jnp.maximum(t - 1, 0), 0, 0)),
            pl.BlockSpec(s0.shape, const3),
            pl.BlockSpec(c0.shape, const3),
            pl.BlockSpec(m0.shape, const2),
            pl.BlockSpec(x0.shape, const2),
        ],
        out_shape=[
            jax.ShapeDtypeStruct((m, D_MODEL), F32),
            jax.ShapeDtypeStruct((TOP_K, m), I32),
            jax.ShapeDtypeStruct((TOP_K, m), F32),
            jax.ShapeDtypeStruct((nt, N_EXPERTS, 128), F32),
            jax.ShapeDtypeStruct(s0.shape, F32),
            jax.ShapeDtypeStruct(c0.shape, F32),
            jax.ShapeDtypeStruct(m0.shape, F32),
            jax.ShapeDtypeStruct(x0.shape, F32),
        ],
        scratch_shapes=[
            pltpu.VMEM((N_SLABS, chunk, D_MODEL), BF16),
            pltpu.VMEM((chunk, D_MODEL), F32),
            pltpu.VMEM((chunk, 128), F32),
            pltpu.VMEM((SUBLANES, chunk), F32),
            pltpu.VMEM((chunk, D_MODEL), BF16),
            pltpu.VMEM((chunk, D_MODEL), BF16),
            pltpu.VMEM((chunk, D_MODEL), F32),
            pltpu.VMEM((chunk, D_MODEL), F32),
            pltpu.VMEM((HG_HEADS, HG_DK, HG_DK), F32),
            pltpu.VMEM((ML_HEADS, ML_DK, ML_AUG), F32),
            pltpu.VMEM((SUBLANES, 128), F32),
            pltpu.VMEM((SUBLANES, D_MODEL), F32),
        ],
        compiler_params=_params("arbitrary"),
        name="front",
    )(x2d, *weights[:9], s0, *weights[9:12], c0, m0, x0, *weights[12:])
    return outs[0], outs[1], outs[2], outs[3], tuple(outs[4:])


def _tile_slots(tt):
    return -(-(TOP_K * tt + N_EXPERTS * (GROUP_ROWS - 1)) // SLOT_BLOCK) * SLOT_BLOCK


def _wait_groups(n, make_copy, s_tile):
    p = 1 << ((s_tile // GROUP_ROWS).bit_length() - 1)
    while p:
        @pl.when((n & p) != 0)
        def _():
            make_copy(p * GROUP_ROWS).wait()
        p >>= 1


def _block_relative(slots, r):
    return jnp.clip(slots - r * SLOT_BLOCK, -1, SLOT_BLOCK).astype(F32).astype(BF16)


def _for_slot_blocks(n, tt, s_tile, body):
    always = TOP_K * tt // SLOT_BLOCK
    for r in range(always):
        body(r)
    for r in range(always, s_tile // SLOT_BLOCK):
        @pl.when(n * GROUP_ROWS > r * SLOT_BLOCK)
        def _():
            body(r)


def _issue_copies(n2, n1, plan_ref, make_copy, s_tile):
    max2 = s_tile // (2 * GROUP_ROWS)
    base1 = 2 * max2

    def start2(p, c):
        make_copy(plan_ref[0, 0, p], plan_ref[0, 0, max2 + p], 2 * GROUP_ROWS).start()
        return c

    def start1(q, c):
        make_copy(plan_ref[0, 0, base1 + q], plan_ref[0, 0, base1 + N_EXPERTS + q], GROUP_ROWS).start()
        return c

    lax.fori_loop(0, n2, start2, 0)
    lax.fori_loop(0, n1, start1, 0)


def _plan_len(s_tile):
    return 2 * (s_tile // (2 * GROUP_ROWS)) + 2 * N_EXPERTS


def _dispatch_kernel(ng_ref, n2_ref, n1_ref, plan_ref, slot_ref, h_ref, xs_ref, buf, sem, *, tt, s_tile):
    i = pl.program_id(0)
    cur = lax.rem(i, 2)

    def list_to_global(list_row, global_row, rows):
        src = buf.at[cur, pl.ds(pl.multiple_of(list_row, GROUP_ROWS), rows), :]
        dst = xs_ref.at[pl.ds(pl.multiple_of(global_row, GROUP_ROWS), rows), :]
        return pltpu.make_async_copy(src, dst, sem.at[cur])

    def wait_tile(j, b):
        _wait_groups(ng_ref[j], lambda rows: pltpu.make_async_copy(
            buf.at[b, pl.ds(0, rows), :], xs_ref.at[pl.ds(0, rows), :], sem.at[b]), s_tile)

    @pl.when(i >= 2)
    def _():
        wait_tile(i - 2, cur)

    hb = h_ref[...].astype(BF16)
    sl = slot_ref[...]

    row_id = lax.broadcasted_iota(I32, (SLOT_BLOCK, tt), 0).astype(F32).astype(BF16)
    one = jnp.ones((), BF16)

    def fill_block(r):
        rel = _block_relative(sl, r)
        p = jnp.zeros((SLOT_BLOCK, tt), BF16)
        for kk in range(TOP_K):
            p = jnp.where(row_id == rel[kk:kk + 1, :], one, p)
        buf[cur, r * SLOT_BLOCK:(r + 1) * SLOT_BLOCK, :] = _dot(p, hb).astype(BF16)

    _for_slot_blocks(ng_ref[i], tt, s_tile, fill_block)

    _issue_copies(n2_ref[i], n1_ref[i], plan_ref, list_to_global, s_tile)

    @pl.when(i == pl.num_programs(0) - 1)
    def _():
        @pl.when(i >= 1)
        def _():
            wait_tile(i - 1, 1 - cur)
        wait_tile(i, cur)


def _dispatch(counts, plan, slot_k, h1, n_slots, tt):
    m = h1.shape[0]
    s_tile = _tile_slots(tt)
    grid_spec = pltpu.PrefetchScalarGridSpec(
        num_scalar_prefetch=3,
        grid=(m // tt,),
        in_specs=[
            pl.BlockSpec((1, 1, _plan_len(s_tile)), lambda i, *_: (i, 0, 0), memory_space=pltpu.SMEM),
            pl.BlockSpec((TOP_K, tt), lambda i, *_: (0, i)),
            pl.BlockSpec((tt, D_MODEL), lambda i, *_: (i, 0)),
        ],
        out_specs=pl.BlockSpec(memory_space=pl.ANY),
        scratch_shapes=[pltpu.VMEM((2, s_tile, D_MODEL), BF16), pltpu.SemaphoreType.DMA((2,))],
    )
    return pl.pallas_call(
        functools.partial(_dispatch_kernel, tt=tt, s_tile=s_tile),
        grid_spec=grid_spec,
        out_shape=jax.ShapeDtypeStruct((n_slots, D_MODEL), BF16),
        compiler_params=_params("arbitrary"),
        name="dispatch",
    )(*counts, plan, slot_k, h1)


def _experts_kernel(be_ref, nu_ref, x_ref, wg_ref, wu_ref, wd_ref, y_ref, wg_b, wu_b, wd_b):
    i = pl.program_id(0)
    used = i < nu_ref[0]

    @pl.when(used & ((i == 0) | (be_ref[i] != be_ref[jnp.maximum(i - 1, 0)])))
    def _():
        wg_b[...] = wg_ref[...].astype(BF16)
        wu_b[...] = wu_ref[...].astype(BF16)
        wd_b[...] = wd_ref[...].astype(BF16)

    @pl.when(used)
    def _():
        xb = x_ref[...]
        a = _dot(xb, wg_b[...])
        u = _dot(xb, wu_b[...])
        y_ref[...] = _dot((a * _sigmoid(a) * u).astype(BF16), wd_b[...]).astype(BF16)


def _experts(blk_expert, n_used, xs, wg, wu, wd):
    n_slots = xs.shape[0]
    n_blocks = n_slots // MOE_BLOCK
    blk = lambda i, be, nu: (jnp.minimum(i, nu[0] - 1), 0)
    wsel = lambda i, be, nu: (be[jnp.minimum(i, nu[0] - 1)], 0, 0)
    grid_spec = pltpu.PrefetchScalarGridSpec(
        num_scalar_prefetch=2,
        grid=(n_blocks,),
        in_specs=[
            pl.BlockSpec((MOE_BLOCK, D_MODEL), blk),
            pl.BlockSpec((None, D_MODEL, D_EXPERT), wsel),
            pl.BlockSpec((None, D_MODEL, D_EXPERT), wsel),
            pl.BlockSpec((None, D_EXPERT, D_MODEL), wsel),
        ],
        out_specs=pl.BlockSpec((MOE_BLOCK, D_MODEL), blk),
        scratch_shapes=[pltpu.VMEM((D_MODEL, D_EXPERT), BF16), pltpu.VMEM((D_MODEL, D_EXPERT), BF16),
                        pltpu.VMEM((D_EXPERT, D_MODEL), BF16)],
    )
    return pl.pallas_call(
        _experts_kernel,
        grid_spec=grid_spec,
        out_shape=jax.ShapeDtypeStruct((n_slots, D_MODEL), BF16),
        compiler_params=_params("arbitrary"),
        name="experts",
    )(blk_expert, n_used, xs, wg, wu, wd)


def _combine_kernel(ng_ref, n2_ref, n1_ref, gcur_ref, gnext_ref, slot_ref, gw_ref, h_ref, y_ref, sg_ref, su_ref, sd_ref,
                    g2_ref, b2_ref, o_ref, ybuf, acc, sem, *, tt, s_tile):
    i = pl.program_id(0)
    cur = lax.rem(i, 2)

    def fetch(j, b, plan_ref):
        def global_to_list(list_row, global_row, rows):
            src = y_ref.at[pl.ds(pl.multiple_of(global_row, GROUP_ROWS), rows), :]
            dst = ybuf.at[b, pl.ds(pl.multiple_of(list_row, GROUP_ROWS), rows), :]
            return pltpu.make_async_copy(src, dst, sem.at[b])

        _issue_copies(n2_ref[j], n1_ref[j], plan_ref, global_to_list, s_tile)

    @pl.when(i == 0)
    def _():
        ybuf[...] = jnp.zeros_like(ybuf)
        fetch(0, 0, gcur_ref)

    @pl.when(i + 1 < pl.num_programs(0))
    def _():
        fetch(i + 1, 1 - cur, gnext_ref)

    h1 = h_ref[...]
    hb = h1.astype(BF16)
    a = _dot(hb, sg_ref[...])
    u = _dot(hb, su_ref[...])
    shared = _dot((a * _sigmoid(a) * u).astype(BF16), sd_ref[...])

    slot = slot_ref[...]
    gw = gw_ref[...]
    _wait_groups(ng_ref[i], lambda rows: pltpu.make_async_copy(
        y_ref.at[pl.ds(0, rows), :], ybuf.at[cur, pl.ds(0, rows), :], sem.at[cur]), s_tile)

    n_always, n_all = TOP_K * tt // SLOT_BLOCK, s_tile // SLOT_BLOCK
    per = 2 if (n_always % 2 == 0 and n_all % 2 == 0) else 1
    rows = per * SLOT_BLOCK

    def block_dot(r):
        lane = lax.broadcasted_iota(I32, (tt, rows), 1) + r * rows
        p = jnp.zeros((tt, rows), F32)
        for kk in range(TOP_K):
            p = jnp.where(lane == slot[:, kk:kk + 1], gw[:, kk:kk + 1], p)
        return _dot(p.astype(BF16), ybuf[cur, r * rows:(r + 1) * rows, :])

    routed = block_dot(0)
    for r in range(1, n_always // per):
        routed = routed + block_dot(r)
    acc[...] = DN_ALPHA * h1 + (routed + shared)
    for r in range(n_always // per, n_all // per):
        @pl.when(ng_ref[i] * GROUP_ROWS > r * rows)
        def _():
            acc[...] += block_dot(r)
    o_ref[...] = _layer_norm(acc[...], g2_ref[...], b2_ref[...])


def _combine(counts, plan, slot_t, gw_t, h1, y, wsg, wsu, wsd, g2, b2, tt):
    m = h1.shape[0]
    nt = m // tt
    s_tile = _tile_slots(tt)
    const = lambda i, *_: (0, 0)
    table = lambda f: pl.BlockSpec((1, 1, _plan_len(s_tile)), f, memory_space=pltpu.SMEM)
    grid_spec = pltpu.PrefetchScalarGridSpec(
        num_scalar_prefetch=3,
        grid=(nt,),
        in_specs=[
            table(lambda i, *_: (i, 0, 0)),
            table(lambda i, *_: (jnp.minimum(i + 1, nt - 1), 0, 0)),
            pl.BlockSpec((tt, TOP_K), lambda i, *_: (i, 0)),
            pl.BlockSpec((tt, TOP_K), lambda i, *_: (i, 0)),
            pl.BlockSpec((tt, D_MODEL), lambda i, *_: (i, 0)),
            pl.BlockSpec(memory_space=pl.ANY),
            pl.BlockSpec((D_MODEL, D_EXPERT), const),
            pl.BlockSpec((D_MODEL, D_EXPERT), const),
            pl.BlockSpec((D_EXPERT, D_MODEL), const),
            pl.BlockSpec((1, D_MODEL), const),
            pl.BlockSpec((1, D_MODEL), const),
        ],
        out_specs=pl.BlockSpec((tt, D_MODEL), lambda i, *_: (i, 0)),
        scratch_shapes=[pltpu.VMEM((2, s_tile, D_MODEL), BF16), pltpu.VMEM((tt, D_MODEL), F32),
                        pltpu.SemaphoreType.DMA((2,))],
    )
    return pl.pallas_call(
        functools.partial(_combine_kernel, tt=tt, s_tile=s_tile),
        grid_spec=grid_spec,
        out_shape=jax.ShapeDtypeStruct((m, D_MODEL), F32),
        compiler_params=_params("arbitrary"),
        name="combine",
    )(*counts, plan, plan, slot_t, gw_t, h1, y, wsg, wsu, wsd, g2, b2)


def _pick_tile(m, pref):
    t = min(pref, m)
    while m % t:
        t //= 2
    return t


def _forward(x, meta_tokens, ln_emb_g, ln_emb_b, w_in, hg_lb_logits, hg_norm_g, ml_conv_w, ml_conv_b,
             ml_ig_bias, ml_fg_bias, ml_norm_g, w_branch_a, w_branch_b, w_out, ln1_g, ln1_b,
             w_router, router_bias, w_exp_gate, w_exp_up, w_exp_down, w_sh_gate, w_sh_up, w_sh_down,
             ln2_g, ln2_b, *, chunk):
    nb, seq, d = x.shape
    m = nb * seq
    row = lambda a: a.reshape(1, -1).astype(F32)

    w = w_in[0]
    kw = HG_HEADS * HG_DK
    o_qa, o_fa, o_ia, o_ga = 0, kw, 2 * kw, 3 * kw
    o_qb = 4 * kw
    o_kb = o_qb + ML_HEADS * ML_DK
    o_vb = o_kb + ML_HEADS * ML_DK
    o_ob = o_vb + ML_HEADS * ML_DV
    o_ig = o_ob + ML_HEADS * ML_DV
    o_fg = o_ig + ML_HEADS
    o_ma = o_fg + ML_HEADS
    o_mb = o_ma + D_MODEL
    cols = lambda o, n: w[:, o:o + n]
    w_cat = jnp.concatenate([
        cols(o_qa, kw), cols(o_fa, kw), cols(o_ia, kw), cols(o_ga, kw),
        cols(o_qb, 2 * ML_HEADS * ML_DK), cols(o_vb, ML_HEADS * ML_DV), cols(o_ob, ML_HEADS * ML_DV),
        cols(o_ma, D_MODEL), cols(o_mb, D_MODEL)], axis=1).astype(BF16)
    w_small = jnp.pad(cols(o_ig, 2 * ML_HEADS), ((0, 0), (0, 128 - 2 * ML_HEADS))).astype(BF16)
    gate_bias = jnp.pad(jnp.concatenate([ml_ig_bias[0], ml_fg_bias[0]]).astype(F32), (0, 128 - 2 * ML_HEADS)).reshape(1, 128)
    lb = jax.nn.softmax(hg_lb_logits.astype(F32), axis=0)[0].reshape(1, -1)
    eg, eb = row(ln_emb_g), row(ln_emb_b)
    conv_w = ml_conv_w[0].astype(F32)
    conv_b = row(ml_conv_b[0])
    hgn, mln = row(hg_norm_g[0]), row(ml_norm_g[0])
    w_small_t = jnp.pad(cols(o_ig, 2 * ML_HEADS).T, ((0, GROUP_ROWS - 2 * ML_HEADS), (0, 0))).astype(BF16)
    gate_bias_t = jnp.broadcast_to(jnp.pad(gate_bias[0, :2 * ML_HEADS], (0, GROUP_ROWS - 2 * ML_HEADS))[:, None],
                                   (GROUP_ROWS, 128))
    wr = w_router[0].T.astype(F32)
    wr_hi, wr_lo = _split_bf16(wr)
    rbias = jnp.broadcast_to(router_bias[0].astype(F32).reshape(N_EXPERTS, 1), (N_EXPERTS, 128))
    weights = [eg, eb, w_cat, w_small, lb, gate_bias, w_small_t, gate_bias_t, hgn, conv_w, conv_b, mln,
               w_branch_a[0].astype(BF16), w_branch_b[0].astype(BF16), w_out[0].astype(BF16),
               row(ln1_g[0]), row(ln1_b[0]), wr_hi, wr_lo, rbias]

    zero_states = (jnp.zeros((HG_HEADS, HG_DK, HG_DK), F32), jnp.zeros((ML_HEADS, ML_DK, ML_AUG), F32),
                   jnp.zeros((SUBLANES, 128), F32), jnp.zeros((SUBLANES, D_MODEL), F32))
    meta_states = _front(meta_tokens.astype(F32), weights, zero_states, 1, N_META)[4]

    x2d = x.reshape(m, d).astype(F32)
    tt = chunk
    h1, slot_k, gw, cnt, _ = _front(x2d, weights, meta_states, nb, chunk)

    nt = m // tt
    s_tile = _tile_slots(tt)
    cnt8 = (cnt[:, :, 0].astype(I32) + GROUP_ROWS - 1) // GROUP_ROWS * GROUP_ROWS
    seg_end = jnp.cumsum(cnt8, axis=1)
    seg_off = seg_end - cnt8
    tile_rows = seg_end[:, -1]
    run = jnp.cumsum(cnt8, axis=0) - cnt8
    tot8 = jnp.sum(cnt8, axis=0)
    padded = (tot8 + MOE_BLOCK - 1) // MOE_BLOCK * MOE_BLOCK
    pends = jnp.cumsum(padded)
    gshift = (pends - padded)[None, :] + run - seg_off
    experts = jnp.arange(N_EXPERTS, dtype=I32)

    def copy_list(per_expert, length, list_row0, step):
        ends = jnp.cumsum(per_expert, axis=1)
        idx = jnp.arange(length, dtype=I32)
        owner = jnp.sum((ends[:, None, :] <= idx[None, :, None]).astype(I32), axis=-1)
        pick = lambda a: jnp.sum(jnp.where(owner[..., None] == experts, a[:, None, :], 0), axis=-1)
        list_row = pick(list_row0 - step * (ends - per_expert)) + step * idx[None, :]
        valid = idx[None, :] < ends[:, -1:]
        return jnp.where(valid, list_row, 0), jnp.where(valid, list_row + pick(gshift), 0), ends[:, -1]

    groups = cnt8 // GROUP_ROWS
    doubles = groups // 2
    l2, g2, n2 = copy_list(doubles, s_tile // (2 * GROUP_ROWS), seg_off, 2 * GROUP_ROWS)
    l1, g1, n1 = copy_list(groups % 2, N_EXPERTS, seg_off + 2 * GROUP_ROWS * doubles, 0)
    plan = jnp.concatenate([l2, g2, l1, g1], axis=1).astype(I32).reshape(nt, 1, -1)
    counts = ((tile_rows // GROUP_ROWS).astype(I32), n2.astype(I32), n1.astype(I32))
    n_blocks = -(-(m * TOP_K + nt * N_EXPERTS * (GROUP_ROWS - 1)) // MOE_BLOCK) + N_EXPERTS
    blk_start = jnp.arange(n_blocks, dtype=I32) * MOE_BLOCK
    blk_expert = jnp.minimum(jnp.sum((pends[None, :] <= blk_start[:, None]).astype(I32), axis=1), N_EXPERTS - 1)
    n_used = (pends[-1:] // MOE_BLOCK).astype(I32)

    xs = _dispatch(counts, plan, slot_k, h1, n_blocks * MOE_BLOCK, tt)
    y = _experts(blk_expert, n_used, xs, w_exp_gate[0].astype(F32), w_exp_up[0].astype(F32),
                 w_exp_down[0].astype(F32))
    out = _combine(counts, plan, slot_k.T, gw.T, h1, y, w_sh_gate[0].astype(BF16), w_sh_up[0].astype(BF16),
                   w_sh_down[0].astype(BF16), row(ln2_g[0]), row(ln2_b[0]), tt)
    return out.reshape(nb, seq, d).astype(x.dtype)


def kernel(x, meta_tokens, ln_emb_g, ln_emb_b, w_in, hg_lb_logits, hg_norm_g, ml_conv_w, ml_conv_b, ml_ig_bias, ml_fg_bias, ml_norm_g, w_branch_a, w_branch_b, w_out, ln1_g, ln1_b, w_router, router_bias, w_exp_gate, w_exp_up, w_exp_down, w_sh_gate, w_sh_up, w_sh_down, ln2_g, ln2_b):
    return _forward(x, meta_tokens, ln_emb_g, ln_emb_b, w_in, hg_lb_logits, hg_norm_g, ml_conv_w, ml_conv_b,
                    ml_ig_bias, ml_fg_bias, ml_norm_g, w_branch_a, w_branch_b, w_out, ln1_g, ln1_b,
                    w_router, router_bias, w_exp_gate, w_exp_up, w_exp_down, w_sh_gate, w_sh_up, w_sh_down,
                    ln2_g, ln2_b, chunk=_pick_tile(x.shape[1], 256))
```

```python
import functools

import jax
import jax.numpy as jnp
from jax import lax
from jax.experimental import pallas as pl
from jax.experimental.pallas import tpu as pltpu

F32, BF16, I32 = jnp.float32, jnp.bfloat16, jnp.int32

D_MODEL = 1024
N_META = 16
HG_HEADS = 8
HG_DK = 128
ML_HEADS = 4
ML_DK = 128
ML_DV = 256
ML_AUG = ML_DV + 128
N_EXPERTS = 64
TOP_K = 8
N_GROUPS = 8
GROUP_SIZE = N_EXPERTS // N_GROUPS
TOPK_GROUPS = 4
D_EXPERT = 256
ROUTED_SCALE = 2.5
MOE_BLOCK = 1024
SLOT_BLOCK = 256
DN_ALPHA = 2.0 ** 0.25
EPS = 1e-5
LOG2E = 1.4426950408889634
EXP2_CLAMP = 115.0
SUBLANES = 8
GROUP_ROWS = 16

P_QA, P_KA, P_IA, P_GA, P_QKB, P_VB, P_OB, P_MA, P_MB = range(9)
N_SLABS = 9

VMEM_LIMIT = 56 * 1024 * 1024


def _params(*sem):
    return pltpu.CompilerParams(dimension_semantics=sem, vmem_limit_bytes=VMEM_LIMIT)


def _sigmoid(x):
    return 1.0 / (1.0 + jnp.exp(-x))


def _log_sigmoid(x):
    return jnp.minimum(x, 0.0) - jnp.log(1.0 + jnp.exp(-jnp.abs(x)))


def _layer_norm(x, g, b):
    xc = x - jnp.mean(x, axis=-1, keepdims=True)
    var = jnp.mean(xc * xc, axis=-1, keepdims=True)
    return xc * lax.rsqrt(var + EPS) * g + b


def _dot(a, b):
    return jnp.dot(a, b, preferred_element_type=F32)


def _dot_nt(a, b):
    return lax.dot_general(a, b, (((1,), (1,)), ((), ())), preferred_element_type=F32)


def _dot_tn(a, b):
    return lax.dot_general(a, b, (((0,), (0,)), ((), ())), preferred_element_type=F32)


def _split_bf16(x):
    hi = x.astype(BF16)
    lo = (x - hi.astype(F32)).astype(BF16)
    return hi, lo


def _neg_abs(x):
    return lax.bitcast_convert_type(lax.bitcast_convert_type(x, I32) | jnp.int32(-2 ** 31), F32)


def _ones_where(cond):
    return jnp.where(cond, 1.0, 0.0).astype(BF16)


def _inproj_pieces(x_ref, g_ref, b_ref, w_ref, ws_ref, lb_ref, gb_ref, wst_ref, gbt_ref,
                   h0_ref, p_ref, lf_ref, sg_ref, sgt_ref):
    h0 = _layer_norm(x_ref[...], g_ref[...], b_ref[...])
    h0_ref[...] = h0
    hb = h0.astype(BF16)

    def slab(n):
        def run():
            acc = _dot(hb, w_ref[:, n * D_MODEL:(n + 1) * D_MODEL])
            if n == P_KA:
                lb = lb_ref[...]
                f = lb + (1.0 - lb) * _sigmoid(acc)
                lf_ref[...] = jnp.log(f)
                acc = 1.0 - f
            elif n == P_GA:
                acc = acc * _sigmoid(acc)
            elif n >= P_OB:
                acc = _sigmoid(acc)
            p_ref[n] = acc.astype(BF16)
        return run

    def gates():
        s = _dot(hb, ws_ref[...]) + gb_ref[...]
        lane = lax.broadcasted_iota(I32, s.shape, 1)
        sg_ref[...] = jnp.where(lane < ML_HEADS, s, _log_sigmoid(s))
        st = _dot_nt(wst_ref[...], hb) + gbt_ref[:, 0:1]
        srow = lax.broadcasted_iota(I32, st.shape, 0)
        sgt_ref[...] = jnp.where(srow < ML_HEADS, st, _log_sigmoid(st))[:SUBLANES, :]

    return [slab(n) for n in range(N_SLABS)], gates


def _block_rows(b, block, pick):
    c, w = b.shape
    parts = [jnp.broadcast_to(b[j * block + pick:j * block + pick + 1, :], (block, w))
             for j in range(c // block)]
    return parts[0] if len(parts) == 1 else jnp.concatenate(parts, axis=0)


def _hgrn_body(q_ref, k_ref, v_ref, g_ref, lf_ref, ng_ref, y_ref, s_scr, cs):
    row = lax.broadcasted_iota(I32, (cs, cs), 0)
    col = lax.broadcasted_iota(I32, (cs, cs), 1)
    tri = _ones_where(col <= row)
    lf_hi, lf_lo = _split_bf16(lf_ref[...])
    b = (_dot(tri, lf_hi) + _dot(tri, lf_lo)) * LOG2E
    q = q_ref[...]
    k = k_ref[...]
    v = v_ref[...]
    blast = b[cs - 1:cs, :]
    qg = q * jnp.exp2(b).astype(BF16)
    kg = k * jnp.exp2(blast - b).astype(BF16)
    dec = jnp.exp2(blast)

    levels = []
    m = SUBLANES
    while 2 * m <= cs:
        w = jnp.exp2(_neg_abs(b - _block_rows(b, 2 * m, m - 1))).astype(BF16)
        sh = (2 * m).bit_length() - 1
        mask = ((row >> sh) == (col >> sh)) & ((row & (2 * m - 1)) >= m) & ((col & (2 * m - 1)) < m)
        levels.append((q * w, k * w, mask))
        m *= 2
    e = jnp.clip(b - _block_rows(b, SUBLANES, SUBLANES // 2 - 1), -EXP2_CLAMP, EXP2_CLAMP)
    levels.append((q * jnp.exp2(e).astype(BF16), k * jnp.exp2(-e).astype(BF16),
                   ((row >> 3) == (col >> 3)) & (col <= row)))

    ng = ng_ref[...]
    for h in range(HG_HEADS):
        sl = slice(h * HG_DK, (h + 1) * HG_DK)
        st = s_scr[h]
        o = _dot_nt(qg[:, sl], st.astype(BF16))
        sc = jnp.zeros((cs, cs), F32)
        for lq, lk, mask in levels:
            sc = jnp.where(mask, _dot_nt(lq[:, sl], lk[:, sl]), sc)
        o = o + _dot(sc.astype(BF16), v[:, sl])
        s_scr[h] = dec[:, sl] * st + _dot_tn(v[:, sl], kg[:, sl])
        ms = jnp.mean(o * o, axis=-1, keepdims=True)
        y = o * lax.rsqrt(ms + EPS) * ng[:, sl] * g_ref[:, sl].astype(F32)
        y_ref[:, sl] = y.astype(BF16)


def _mlstm_conv(qk_ref, cw_ref, cb_ref, x_scr, cs):
    x = qk_ref[...].astype(F32)
    prev = x_scr[...]
    sub = lax.broadcasted_iota(I32, (SUBLANES, D_MODEL), 0)
    cw = cw_ref[...]
    conv = cw[3:4, :] * x + cb_ref[...]
    for j in (1, 2, 3):
        xs = pltpu.roll(x, j, 0)
        head = jnp.where(sub < j, pltpu.roll(prev, j, 0), xs[:SUBLANES, :])
        xs = jnp.concatenate([head, xs[SUBLANES:, :]], axis=0)
        conv = conv + cw[3 - j:4 - j, :] * xs
    x_scr[...] = x[cs - SUBLANES:, :]
    qk = conv * _sigmoid(conv)
    return (qk[:, :ML_HEADS * ML_DK] * (ML_DK ** -0.5)).astype(BF16), qk[:, ML_HEADS * ML_DK:]


def _mlstm_cumsums(sg_ref, sgt_ref, cs):
    row = lax.broadcasted_iota(I32, (cs, cs), 0)
    col = lax.broadcasted_iota(I32, (cs, cs), 1)
    tri = _ones_where(col <= row)
    sg_hi, sg_lo = _split_bf16(sg_ref[...])
    sgt_hi, sgt_lo = _split_bf16(sgt_ref[...])
    return _dot(tri, sg_hi) + _dot(tri, sg_lo), _dot_nt(sgt_hi, tri) + _dot_nt(sgt_lo, tri)


def _mlstm_heads(q_all, k_all, bcol_all, brow_all, v_ref, og_ref, sg_ref, sgt_ref, ng_ref, y_ref, c_scr, m_scr, cs):
    row = lax.broadcasted_iota(I32, (cs, cs), 0)
    col = lax.broadcasted_iota(I32, (cs, cs), 1)
    causal = col <= row
    sg = sg_ref[...]
    sgt = sgt_ref[...]
    lane128 = lax.broadcasted_iota(I32, (cs, 128), 1)
    ones_col = _ones_where(lane128 == 0)
    v = v_ref[...]
    ng = ng_ref[...]

    for h in range(ML_HEADS):
        b_col = bcol_all[:, ML_HEADS + h:ML_HEADS + h + 1]
        b_row = brow_all[ML_HEADS + h:ML_HEADS + h + 1, :]
        ig_col = sg[:, h:h + 1]
        ig_row = sgt[h:h + 1, :]
        m_prev = m_scr[h:h + 1, 0:1]
        q_h = q_all[:, h * ML_DK:(h + 1) * ML_DK]
        k_h = k_all[:, h * ML_DK:(h + 1) * ML_DK]
        v_aug = jnp.concatenate([v[:, h * ML_DV:(h + 1) * ML_DV], ones_col], axis=1)
        c_st = c_scr[h]

        log_intra = jnp.where(causal, b_col - b_row + ig_row, -jnp.inf)
        log_inter = b_col + m_prev
        m_t = jnp.maximum(log_inter, jnp.max(log_intra, axis=-1, keepdims=True))
        w_intra = jnp.exp(log_intra - m_t)
        w_inter = jnp.exp(log_inter - m_t)
        s = _dot_nt(q_h, k_h.astype(BF16)) * w_intra
        tot = w_inter * _dot(q_h, c_st.astype(BF16)) + _dot(s.astype(BF16), v_aug)
        num = tot[:, :ML_DV]
        den = tot[:, ML_DV:ML_DV + 1]
        hid = num / jnp.maximum(jnp.abs(den), jnp.exp(-m_t))

        b_last = b_col[cs - 1:cs, :]
        log_w = b_last - b_col + ig_col
        m_new = jnp.maximum(b_last + m_prev, jnp.max(log_w, axis=0, keepdims=True))
        w_s = jnp.exp(log_w - m_new)
        decay = jnp.exp(b_last + m_prev - m_new)
        c_scr[h] = decay * c_st + _dot_tn((k_h * w_s).astype(BF16), v_aug)
        m_scr[h:h + 1, :] = jnp.broadcast_to(m_new, (1, 128))

        hc = hid - jnp.mean(hid, axis=-1, keepdims=True)
        var = jnp.mean(hc * hc, axis=-1, keepdims=True)
        sl = slice(h * ML_DV, (h + 1) * ML_DV)
        y = hc * lax.rsqrt(var + EPS) * ng[:, sl] * og_ref[:, sl].astype(F32)
        y_ref[:, sl] = y.astype(BF16)


def _merge_body(h0_ref, ya_ref, yb_ref, ma_ref, mb_ref, wa_ref, wb_ref, wo_ref, pre_ref):
    merged = (ma_ref[...].astype(F32) * _dot(ya_ref[...], wa_ref[...])
              + mb_ref[...].astype(F32) * _dot(yb_ref[...], wb_ref[...]))
    pre_ref[...] = DN_ALPHA * h0_ref[...] + _dot(merged.astype(BF16), wo_ref[...])


def _router_pieces(pre_ref, g1_ref, b1_ref, wrh_ref, wrl_ref, rb_ref, h1_ref, slot_ref, gw_ref, cnt_ref, tm):
    neg_inf = -jnp.inf
    v = {}

    def ln1():
        v['h1'] = _layer_norm(pre_ref[...], g1_ref[...], b1_ref[...])
        h1_ref[...] = v['h1']

    def logits():
        h_hi, h_lo = _split_bf16(v['h1'])
        lg = _dot_nt(wrh_ref[...], h_hi) + _dot_nt(wrh_ref[...], h_lo) + _dot_nt(wrl_ref[...], h_hi)
        v['scores'] = _sigmoid(lg)
        v['biased'] = v['scores'] + rb_ref[:, 0:1]

    def group_scores():
        g3 = v['biased'].reshape(N_GROUPS, GROUP_SIZE, tm)
        sub3 = lax.broadcasted_iota(I32, g3.shape, 1)
        top1 = jnp.max(g3, axis=1, keepdims=True)
        first = jnp.min(jnp.where(g3 == top1, sub3, GROUP_SIZE), axis=1, keepdims=True)
        top2 = jnp.max(jnp.where(sub3 == first, neg_inf, g3), axis=1, keepdims=True)
        v['gs'] = (top1 + top2).reshape(N_GROUPS, tm)

    def group_select():
        gs = v['gs']
        gi = lax.broadcasted_iota(I32, gs.shape, 0)
        grank = jnp.zeros(gs.shape, F32)
        for j in range(N_GROUPS):
            r = gs[j:j + 1, :]
            grank = grank + jnp.where((r > gs) | ((r == gs) & (gi > j)), 1.0, 0.0)
        gsel = grank < float(TOPK_GROUPS)
        emask = jnp.broadcast_to(gsel.reshape(N_GROUPS, 1, tm), (N_GROUPS, GROUP_SIZE, tm)).reshape(N_EXPERTS, tm)
        v['work'] = jnp.where(emask, v['biased'], neg_inf)
        v['rank'] = jnp.full((N_EXPERTS, tm), float(N_EXPERTS), F32)

    def extract(kk):
        def run():
            ei = lax.broadcasted_iota(I32, (N_EXPERTS, tm), 0)
            for k2 in (kk, kk + 1):
                work = v['work']
                top = jnp.max(work, axis=0, keepdims=True)
                first = jnp.min(jnp.where(work == top, ei, N_EXPERTS), axis=0, keepdims=True)
                hit = ei == first
                v['rank'] = jnp.where(hit, float(k2), v['rank'])
                v['work'] = jnp.where(hit, neg_inf, work)
        return run

    def slots():
        sel = v['rank'] < float(TOP_K)
        sel_w = jnp.where(sel, v['scores'], 0.0)
        v['gwd'] = sel_w / jnp.sum(sel_w, axis=0, keepdims=True) * ROUTED_SCALE
        tr = lax.broadcasted_iota(I32, (tm, tm), 0)
        tc = lax.broadcasted_iota(I32, (tm, tm), 1)
        sel_b = _ones_where(sel)
        rloc = _dot(sel_b, _ones_where(tr < tc))
        cnt = _dot(sel_b, jnp.ones((tm, 128), BF16))
        cnt_g = jnp.floor((cnt + (GROUP_ROWS - 1.0)) * (1.0 / GROUP_ROWS)) * GROUP_ROWS
        er = lax.broadcasted_iota(I32, (N_EXPERTS, N_EXPERTS), 0)
        ec = lax.broadcasted_iota(I32, (N_EXPERTS, N_EXPERTS), 1)
        seg_start = _dot(_ones_where(ec < er), cnt_g.astype(BF16))
        v['slot_e'] = seg_start[:, 0:1] + rloc
        v['sel'] = sel
        cnt_ref[...] = cnt

    def picks():
        s_rows, w_rows = [], []
        for kk in range(TOP_K):
            pick = v['sel'] & (v['rank'] == float(kk))
            s_rows.append(jnp.sum(jnp.where(pick, v['slot_e'], 0.0), axis=0, keepdims=True))
            w_rows.append(jnp.sum(jnp.where(pick, v['gwd'], 0.0), axis=0, keepdims=True))
        slot_ref[...] = jnp.concatenate(s_rows, axis=0).astype(I32)
        gw_ref[...] = jnp.concatenate(w_rows, axis=0)

    return [ln1, logits, group_scores, group_select] + [extract(kk) for kk in range(0, TOP_K, 2)] + [slots, picks]


def _front_kernel(x_ref, eg_ref, eb_ref, w_ref, ws_ref, lb_ref, gb_ref, wst_ref, gbt_ref,
                  hng_ref, s0_ref, cw_ref, cb_ref, mng_ref, c0_ref, m0_ref, x0_ref,
                  wa_ref, wb_ref, wo_ref, g1_ref, b1_ref, wrh_ref, wrl_ref, rb_ref,
                  h1_ref, slot_ref, gw_ref, cnt_ref, sfin_ref, cfin_ref, mfin_ref, xfin_ref,
                  p_scr, lf_scr, sg_scr, sgt_scr, ya_scr, yb_scr, h0_scr, hp_scr, s_scr, c_scr, m_scr, x_scr,
                  *, chunk, chunks_per_seq):
    t = pl.program_id(0)
    nt = pl.num_programs(0) - 1

    @pl.when(lax.rem(t, chunks_per_seq) == 0)
    def _():
        s_scr[...] = s0_ref[...]
        c_scr[...] = c0_ref[...]
        m_scr[...] = m0_ref[...]
        x_scr[...] = x0_ref[...]

    @pl.when(t == 0)
    def _():
        hp_scr[...] = jnp.zeros_like(hp_scr)

    def router():
        return _router_pieces(hp_scr, g1_ref, b1_ref, wrh_ref, wrl_ref, rb_ref, h1_ref, slot_ref, gw_ref, cnt_ref,
                              chunk)

    @pl.when(t < nt)
    def _():
        slabs, gates = _inproj_pieces(x_ref, eg_ref, eb_ref, w_ref, ws_ref, lb_ref, gb_ref, wst_ref, gbt_ref,
                                      h0_scr, p_scr, lf_scr, sg_scr, sgt_scr)
        ml = {}

        def conv():
            ml['q'], ml['k'] = _mlstm_conv(p_scr.at[P_QKB], cw_ref, cb_ref, x_scr, chunk)

        def cumsums():
            ml['bcol'], ml['brow'] = _mlstm_cumsums(sg_scr, sgt_scr, chunk)

        r = router()
        light = [[], [r[0]], [conv, r[1]], [cumsums, r[2]]] + [[p] for p in r[3:8]] + [r[8:]]
        heavy = [slabs[P_QKB], gates] + [slabs[n] for n in range(N_SLABS) if n != P_QKB]
        for piece, fill in zip(heavy, light):
            piece()
            for f in fill:
                f()
        _hgrn_body(p_scr.at[P_QA], p_scr.at[P_KA], p_scr.at[P_IA], p_scr.at[P_GA], lf_scr, hng_ref, ya_scr, s_scr,
                   chunk)
        _mlstm_heads(ml['q'], ml['k'], ml['bcol'], ml['brow'], p_scr.at[P_VB], p_scr.at[P_OB], sg_scr, sgt_scr,
                     mng_ref, yb_scr, c_scr, m_scr, chunk)
        _merge_body(h0_scr, ya_scr, yb_scr, p_scr.at[P_MA], p_scr.at[P_MB], wa_ref, wb_ref, wo_ref, hp_scr)

    @pl.when(t == nt)
    def _():
        for piece in router():
            piece()

    @pl.when(t == nt - 1)
    def _():
        sfin_ref[...] = s_scr[...]
        cfin_ref[...] = c_scr[...]
        mfin_ref[...] = m_scr[...]
        xfin_ref[...] = x_scr[...]


def _front(x2d, weights, states, nb, chunk):
    m = x2d.shape[0]
    nt = m // chunk
    s0, c0, m0, x0 = states
    cur_rows = lambda t: (jnp.minimum(t, nt - 1), 0)
    prev_lanes = lambda t: (0, jnp.maximum(t - 1, 0))
    const2 = lambda t: (0, 0)
    const3 = lambda t: (0, 0, 0)

    def resident(a):
        return pl.BlockSpec(a.shape, const2 if a.ndim == 2 else const3, pipeline_mode=pl.Buffered(1))

    outs = pl.pallas_call(
        functools.partial(_front_kernel, chunk=chunk, chunks_per_seq=nt // nb),
        grid=(nt + 1,),
        in_specs=[pl.BlockSpec((chunk, D_MODEL), cur_rows)] + [resident(a) for a in weights[:8]]
        + [resident(weights[8]), resident(s0)] + [resident(a) for a in weights[9:12]]
        + [resident(c0), resident(m0), resident(x0)] + [resident(a) for a in weights[12:]],
        out_specs=[
            pl.BlockSpec((chunk, D_MODEL), lambda t: (jnp.maximum(t - 1, 0), 0)),
            pl.BlockSpec((TOP_K, chunk), prev_lanes),
            pl.BlockSpec((TOP_K, chunk), prev_lanes),
            pl.BlockSpec((None, N_EXPERTS, 128), lambda t: (jnp.maximum(t - 1, 0), 0, 0)),
            pl.BlockSpec(s0.shape, const3),
            pl.BlockSpec(c0.shape, const3),
            pl.BlockSpec(m0.shape, const2),
            pl.BlockSpec(x0.shape, const2),
        ],
        out_shape=[
            jax.ShapeDtypeStruct((m, D_MODEL), F32),
            jax.ShapeDtypeStruct((TOP_K, m), I32),
            jax.ShapeDtypeStruct((TOP_K, m), F32),
            jax.ShapeDtypeStruct((nt, N_EXPERTS, 128), F32),
            jax.ShapeDtypeStruct(s0.shape, F32),
            jax.ShapeDtypeStruct(c0.shape, F32),
            jax.ShapeDtypeStruct(m0.shape, F32),
            jax.ShapeDtypeStruct(x0.shape, F32),
        ],
        scratch_shapes=[
            pltpu.VMEM((N_SLABS, chunk, D_MODEL), BF16),
            pltpu.VMEM((chunk, D_MODEL), F32),
            pltpu.VMEM((chunk, 128), F32),
            pltpu.VMEM((SUBLANES, chunk), F32),
            pltpu.VMEM((chunk, D_MODEL), BF16),
            pltpu.VMEM((chunk, D_MODEL), BF16),
            pltpu.VMEM((chunk, D_MODEL), F32),
            pltpu.VMEM((chunk, D_MODEL), F32),
            pltpu.VMEM((HG_HEADS, HG_DK, HG_DK), F32),
            pltpu.VMEM((ML_HEADS, ML_DK, ML_AUG), F32),
            pltpu.VMEM((SUBLANES, 128), F32),
            pltpu.VMEM((SUBLANES, D_MODEL), F32),
        ],
        compiler_params=_params("arbitrary"),
        name="front",
    )(x2d, *weights[:9], s0, *weights[9:12], c0, m0, x0, *weights[12:])
    return outs[0], outs[1], outs[2], outs[3], tuple(outs[4:])


def _tile_slots(tt):
    return -(-(TOP_K * tt + N_EXPERTS * (GROUP_ROWS - 1)) // SLOT_BLOCK) * SLOT_BLOCK


def _wait_groups(n, make_copy, s_tile):
    p = 1 << ((s_tile // GROUP_ROWS).bit_length() - 1)
    while p:
        @pl.when((n & p) != 0)
        def _():
            make_copy(p * GROUP_ROWS).wait()
        p >>= 1


def _block_relative(slots, r):
    return jnp.clip(slots - r * SLOT_BLOCK, -1, SLOT_BLOCK).astype(F32).astype(BF16)


def _for_slot_blocks(n, tt, s_tile, body):
    always = TOP_K * tt // SLOT_BLOCK
    for r in range(always):
        body(r)
    for r in range(always, s_tile // SLOT_BLOCK):
        @pl.when(n * GROUP_ROWS > r * SLOT_BLOCK)
        def _():
            body(r)


def _issue_copies(n2, n1, plan_ref, make_copy, s_tile):
    max2 = s_tile // (2 * GROUP_ROWS)
    base1 = 2 * max2

    def start2(p, c):
        make_copy(plan_ref[0, 0, p], plan_ref[0, 0, max2 + p], 2 * GROUP_ROWS).start()
        return c

    def start1(q, c):
        make_copy(plan_ref[0, 0, base1 + q], plan_ref[0, 0, base1 + N_EXPERTS + q], GROUP_ROWS).start()
        return c

    lax.fori_loop(0, n2, start2, 0)
    lax.fori_loop(0, n1, start1, 0)


def _plan_len(s_tile):
    return 2 * (s_tile // (2 * GROUP_ROWS)) + 2 * N_EXPERTS


def _dispatch_kernel(ng_ref, n2_ref, n1_ref, plan_ref, slot_ref, h_ref, xs_ref, buf, sem, *, tt, s_tile):
    i = pl.program_id(0)
    cur = lax.rem(i, 2)

    def list_to_global(list_row, global_row, rows):
        src = buf.at[cur, pl.ds(pl.multiple_of(list_row, GROUP_ROWS), rows), :]
        dst = xs_ref.at[pl.ds(pl.multiple_of(global_row, GROUP_ROWS), rows), :]
        return pltpu.make_async_copy(src, dst, sem.at[cur])

    def wait_tile(j, b):
        _wait_groups(ng_ref[j], lambda rows: pltpu.make_async_copy(
            buf.at[b, pl.ds(0, rows), :], xs_ref.at[pl.ds(0, rows), :], sem.at[b]), s_tile)

    @pl.when(i >= 2)
    def _():
        wait_tile(i - 2, cur)

    hb = h_ref[...].astype(BF16)
    sl = slot_ref[...]

    row_id = lax.broadcasted_iota(I32, (SLOT_BLOCK, tt), 0).astype(F32).astype(BF16)
    one = jnp.ones((), BF16)

    def fill_block(r):
        rel = _block_relative(sl, r)
        p = jnp.zeros((SLOT_BLOCK, tt), BF16)
        for kk in range(TOP_K):
            p = jnp.where(row_id == rel[kk:kk + 1, :], one, p)
        buf[cur, r * SLOT_BLOCK:(r + 1) * SLOT_BLOCK, :] = _dot(p, hb).astype(BF16)

    _for_slot_blocks(ng_ref[i], tt, s_tile, fill_block)

    _issue_copies(n2_ref[i], n1_ref[i], plan_ref, list_to_global, s_tile)

    @pl.when(i == pl.num_programs(0) - 1)
    def _():
        @pl.when(i >= 1)
        def _():
            wait_tile(i - 1, 1 - cur)
        wait_tile(i, cur)


def _dispatch(counts, plan, slot_k, h1, n_slots, tt):
    m = h1.shape[0]
    s_tile = _tile_slots(tt)
    grid_spec = pltpu.PrefetchScalarGridSpec(
        num_scalar_prefetch=3,
        grid=(m // tt,),
        in_specs=[
            pl.BlockSpec((1, 1, _plan_len(s_tile)), lambda i, *_: (i, 0, 0), memory_space=pltpu.SMEM),
            pl.BlockSpec((TOP_K, tt), lambda i, *_: (0, i)),
            pl.BlockSpec((tt, D_MODEL), lambda i, *_: (i, 0)),
        ],
        out_specs=pl.BlockSpec(memory_space=pl.ANY),
        scratch_shapes=[pltpu.VMEM((2, s_tile, D_MODEL), BF16), pltpu.SemaphoreType.DMA((2,))],
    )
    return pl.pallas_call(
        functools.partial(_dispatch_kernel, tt=tt, s_tile=s_tile),
        grid_spec=grid_spec,
        out_shape=jax.ShapeDtypeStruct((n_slots, D_MODEL), BF16),
        compiler_params=_params("arbitrary"),
        name="dispatch",
    )(*counts, plan, slot_k, h1)


def _experts_kernel(be_ref, nu_ref, x_ref, wg_ref, wu_ref, wd_ref, y_ref, wg_b, wu_b, wd_b):
    i = pl.program_id(0)
    used = i < nu_ref[0]

    @pl.when(used & ((i == 0) | (be_ref[i] != be_ref[jnp.maximum(i - 1, 0)])))
    def _():
        wg_b[...] = wg_ref[...].astype(BF16)
        wu_b[...] = wu_ref[...].astype(BF16)
        wd_b[...] = wd_ref[...].astype(BF16)

    @pl.when(used)
    def _():
        xb = x_ref[...]
        a = _dot(xb, wg_b[...])
        u = _dot(xb, wu_b[...])
        y_ref[...] = _dot((a * _sigmoid(a) * u).astype(BF16), wd_b[...]).astype(BF16)


def _experts(blk_expert, n_used, xs, wg, wu, wd):
    n_slots = xs.shape[0]
    n_blocks = n_slots // MOE_BLOCK
    blk = lambda i, be, nu: (jnp.minimum(i, nu[0] - 1), 0)
    wsel = lambda i, be, nu: (be[jnp.minimum(i, nu[0] - 1)], 0, 0)
    grid_spec = pltpu.PrefetchScalarGridSpec(
        num_scalar_prefetch=2,
        grid=(n_blocks,),
        in_specs=[
            pl.BlockSpec((MOE_BLOCK, D_MODEL), blk),
            pl.BlockSpec((None, D_MODEL, D_EXPERT), wsel),
            pl.BlockSpec((None, D_MODEL, D_EXPERT), wsel),
            pl.BlockSpec((None, D_EXPERT, D_MODEL), wsel),
        ],
        out_specs=pl.BlockSpec((MOE_BLOCK, D_MODEL), blk),
        scratch_shapes=[pltpu.VMEM((D_MODEL, D_EXPERT), BF16), pltpu.VMEM((D_MODEL, D_EXPERT), BF16),
                        pltpu.VMEM((D_EXPERT, D_MODEL), BF16)],
    )
    return pl.pallas_call(
        _experts_kernel,
        grid_spec=grid_spec,
        out_shape=jax.ShapeDtypeStruct((n_slots, D_MODEL), BF16),
        compiler_params=_params("arbitrary"),
        name="experts",
    )(blk_expert, n_used, xs, wg, wu, wd)


def _combine_kernel(ng_ref, n2_ref, n1_ref, gcur_ref, gnext_ref, slot_ref, gw_ref, h_ref, y_ref, sg_ref, su_ref, sd_ref,
                    g2_ref, b2_ref, o_ref, ybuf, acc, sem, *, tt, s_tile):
    i = pl.program_id(0)
    cur = lax.rem(i, 2)

    def fetch(j, b, plan_ref):
        def global_to_list(list_row, global_row, rows):
            src = y_ref.at[pl.ds(pl.multiple_of(global_row, GROUP_ROWS), rows), :]
            dst = ybuf.at[b, pl.ds(pl.multiple_of(list_row, GROUP_ROWS), rows), :]
            return pltpu.make_async_copy(src, dst, sem.at[b])

        _issue_copies(n2_ref[j], n1_ref[j], plan_ref, global_to_list, s_tile)

    @pl.when(i == 0)
    def _():
        ybuf[...] = jnp.zeros_like(ybuf)
        fetch(0, 0, gcur_ref)

    @pl.when(i + 1 < pl.num_programs(0))
    def _():
        fetch(i + 1, 1 - cur, gnext_ref)

    h1 = h_ref[...]
    hb = h1.astype(BF16)
    a = _dot(hb, sg_ref[...])
    u = _dot(hb, su_ref[...])
    shared = _dot((a * _sigmoid(a) * u).astype(BF16), sd_ref[...])

    slot = slot_ref[...]
    gwb = gw_ref[...].astype(BF16)
    _wait_groups(ng_ref[i], lambda rows: pltpu.make_async_copy(
        y_ref.at[pl.ds(0, rows), :], ybuf.at[cur, pl.ds(0, rows), :], sem.at[cur]), s_tile)

    n_always, n_all = TOP_K * tt // SLOT_BLOCK, s_tile // SLOT_BLOCK
    per = 2 if (n_always % 2 == 0 and n_all % 2 == 0) else 1
    rows = per * SLOT_BLOCK
    row_id = lax.broadcasted_iota(I32, (SLOT_BLOCK, tt), 0).astype(F32).astype(BF16)

    def one_hot(r):
        rel = _block_relative(slot, r)
        p = jnp.zeros((SLOT_BLOCK, tt), BF16)
        for kk in range(TOP_K):
            p = jnp.where(row_id == rel[kk:kk + 1, :], gwb[kk:kk + 1, :], p)
        return p

    def block_dot(r):
        p = jnp.concatenate([one_hot(per * r + j) for j in range(per)], axis=0) if per > 1 else one_hot(r)
        return _dot_tn(p, ybuf[cur, r * rows:(r + 1) * rows, :])

    routed = block_dot(0)
    for r in range(1, n_always // per):
        routed = routed + block_dot(r)
    acc[...] = DN_ALPHA * h1 + (routed + shared)
    for r in range(n_always // per, n_all // per):
        @pl.when(ng_ref[i] * GROUP_ROWS > r * rows)
        def _():
            acc[...] += block_dot(r)
    o_ref[...] = _layer_norm(acc[...], g2_ref[...], b2_ref[...])


def _combine(counts, plan, slot_k, gw_k, h1, y, wsg, wsu, wsd, g2, b2, tt):
    m = h1.shape[0]
    nt = m // tt
    s_tile = _tile_slots(tt)
    const = lambda i, *_: (0, 0)
    table = lambda f: pl.BlockSpec((1, 1, _plan_len(s_tile)), f, memory_space=pltpu.SMEM)
    grid_spec = pltpu.PrefetchScalarGridSpec(
        num_scalar_prefetch=3,
        grid=(nt,),
        in_specs=[
            table(lambda i, *_: (i, 0, 0)),
            table(lambda i, *_: (jnp.minimum(i + 1, nt - 1), 0, 0)),
            pl.BlockSpec((TOP_K, tt), lambda i, *_: (0, i)),
            pl.BlockSpec((TOP_K, tt), lambda i, *_: (0, i)),
            pl.BlockSpec((tt, D_MODEL), lambda i, *_: (i, 0)),
            pl.BlockSpec(memory_space=pl.ANY),
            pl.BlockSpec((D_MODEL, D_EXPERT), const),
            pl.BlockSpec((D_MODEL, D_EXPERT), const),
            pl.BlockSpec((D_EXPERT, D_MODEL), const),
            pl.BlockSpec((1, D_MODEL), const),
            pl.BlockSpec((1, D_MODEL), const),
        ],
        out_specs=pl.BlockSpec((tt, D_MODEL), lambda i, *_: (i, 0)),
        scratch_shapes=[pltpu.VMEM((2, s_tile, D_MODEL), BF16), pltpu.VMEM((tt, D_MODEL), F32),
                        pltpu.SemaphoreType.DMA((2,))],
    )
    return pl.pallas_call(
        functools.partial(_combine_kernel, tt=tt, s_tile=s_tile),
        grid_spec=grid_spec,
        out_shape=jax.ShapeDtypeStruct((m, D_MODEL), F32),
        compiler_params=_params("arbitrary"),
        name="combine",
    )(*counts, plan, plan, slot_k, gw_k, h1, y, wsg, wsu, wsd, g2, b2)


def _pick_tile(m, pref):
    t = min(pref, m)
    while m % t:
        t //= 2
    return t


def _forward(x, meta_tokens, ln_emb_g, ln_emb_b, w_in, hg_lb_logits, hg_norm_g, ml_conv_w, ml_conv_b,
             ml_ig_bias, ml_fg_bias, ml_norm_g, w_branch_a, w_branch_b, w_out, ln1_g, ln1_b,
             w_router, router_bias, w_exp_gate, w_exp_up, w_exp_down, w_sh_gate, w_sh_up, w_sh_down,
             ln2_g, ln2_b, *, chunk):
    nb, seq, d = x.shape
    m = nb * seq
    row = lambda a: a.reshape(1, -1).astype(F32)

    w = w_in[0]
    kw = HG_HEADS * HG_DK
    o_qa, o_fa, o_ia, o_ga = 0, kw, 2 * kw, 3 * kw
    o_qb = 4 * kw
    o_kb = o_qb + ML_HEADS * ML_DK
    o_vb = o_kb + ML_HEADS * ML_DK
    o_ob = o_vb + ML_HEADS * ML_DV
    o_ig = o_ob + ML_HEADS * ML_DV
    o_fg = o_ig + ML_HEADS
    o_ma = o_fg + ML_HEADS
    o_mb = o_ma + D_MODEL
    cols = lambda o, n: w[:, o:o + n]
    w_cat = jnp.concatenate([
        cols(o_qa, kw), cols(o_fa, kw), cols(o_ia, kw), cols(o_ga, kw),
        cols(o_qb, 2 * ML_HEADS * ML_DK), cols(o_vb, ML_HEADS * ML_DV), cols(o_ob, ML_HEADS * ML_DV),
        cols(o_ma, D_MODEL), cols(o_mb, D_MODEL)], axis=1).astype(BF16)
    w_small = jnp.pad(cols(o_ig, 2 * ML_HEADS), ((0, 0), (0, 128 - 2 * ML_HEADS))).astype(BF16)
    gate_bias = jnp.pad(jnp.concatenate([ml_ig_bias[0], ml_fg_bias[0]]).astype(F32), (0, 128 - 2 * ML_HEADS)).reshape(1, 128)
    lb = jax.nn.softmax(hg_lb_logits.astype(F32), axis=0)[0].reshape(1, -1)
    eg, eb = row(ln_emb_g), row(ln_emb_b)
    conv_w = ml_conv_w[0].astype(F32)
    conv_b = row(ml_conv_b[0])
    hgn, mln = row(hg_norm_g[0]), row(ml_norm_g[0])
    w_small_t = jnp.pad(cols(o_ig, 2 * ML_HEADS).T, ((0, GROUP_ROWS - 2 * ML_HEADS), (0, 0))).astype(BF16)
    gate_bias_t = jnp.broadcast_to(jnp.pad(gate_bias[0, :2 * ML_HEADS], (0, GROUP_ROWS - 2 * ML_HEADS))[:, None],
                                   (GROUP_ROWS, 128))
    wr = w_router[0].T.astype(F32)
    wr_hi, wr_lo = _split_bf16(wr)
    rbias = jnp.broadcast_to(router_bias[0].astype(F32).reshape(N_EXPERTS, 1), (N_EXPERTS, 128))
    weights = [eg, eb, w_cat, w_small, lb, gate_bias, w_small_t, gate_bias_t, hgn, conv_w, conv_b, mln,
               w_branch_a[0].astype(BF16), w_branch_b[0].astype(BF16), w_out[0].astype(BF16),
               row(ln1_g[0]), row(ln1_b[0]), wr_hi, wr_lo, rbias]

    zero_states = (jnp.zeros((HG_HEADS, HG_DK, HG_DK), F32), jnp.zeros((ML_HEADS, ML_DK, ML_AUG), F32),
                   jnp.zeros((SUBLANES, 128), F32), jnp.zeros((SUBLANES, D_MODEL), F32))
    meta_states = _front(meta_tokens.astype(F32), weights, zero_states, 1, N_META)[4]

    x2d = x.reshape(m, d).astype(F32)
    tt = chunk
    h1, slot_k, gw, cnt, _ = _front(x2d, weights, meta_states, nb, chunk)

    nt = m // tt
    s_tile = _tile_slots(tt)
    cnt8 = (cnt[:, :, 0].astype(I32) + GROUP_ROWS - 1) // GROUP_ROWS * GROUP_ROWS
    seg_end = jnp.cumsum(cnt8, axis=1)
    seg_off = seg_end - cnt8
    tile_rows = seg_end[:, -1]
    run = jnp.cumsum(cnt8, axis=0) - cnt8
    tot8 = jnp.sum(cnt8, axis=0)
    padded = (tot8 + MOE_BLOCK - 1) // MOE_BLOCK * MOE_BLOCK
    pends = jnp.cumsum(padded)
    gshift = (pends - padded)[None, :] + run - seg_off
    experts = jnp.arange(N_EXPERTS, dtype=I32)

    def copy_list(per_expert, length, list_row0, step):
        ends = jnp.cumsum(per_expert, axis=1)
        idx = jnp.arange(length, dtype=I32)
        owner = jnp.sum((ends[:, None, :] <= idx[None, :, None]).astype(I32), axis=-1)
        pick = lambda a: jnp.sum(jnp.where(owner[..., None] == experts, a[:, None, :], 0), axis=-1)
        list_row = pick(list_row0 - step * (ends - per_expert)) + step * idx[None, :]
        valid = idx[None, :] < ends[:, -1:]
        return jnp.where(valid, list_row, 0), jnp.where(valid, list_row + pick(gshift), 0), ends[:, -1]

    groups = cnt8 // GROUP_ROWS
    doubles = groups // 2
    l2, g2, n2 = copy_list(doubles, s_tile // (2 * GROUP_ROWS), seg_off, 2 * GROUP_ROWS)
    l1, g1, n1 = copy_list(groups % 2, N_EXPERTS, seg_off + 2 * GROUP_ROWS * doubles, 0)
    plan = jnp.concatenate([l2, g2, l1, g1], axis=1).astype(I32).reshape(nt, 1, -1)
    counts = ((tile_rows // GROUP_ROWS).astype(I32), n2.astype(I32), n1.astype(I32))
    n_blocks = -(-(m * TOP_K + nt * N_EXPERTS * (GROUP_ROWS - 1)) // MOE_BLOCK) + N_EXPERTS
    blk_start = jnp.arange(n_blocks, dtype=I32) * MOE_BLOCK
    blk_expert = jnp.minimum(jnp.sum((pends[None, :] <= blk_start[:, None]).astype(I32), axis=1), N_EXPERTS - 1)
    n_used = (pends[-1:] // MOE_BLOCK).astype(I32)

    xs = _dispatch(counts, plan, slot_k, h1, n_blocks * MOE_BLOCK, tt)
    y = _experts(blk_expert, n_used, xs, w_exp_gate[0].astype(F32), w_exp_up[0].astype(F32),
                 w_exp_down[0].astype(F32))
    out = _combine(counts, plan, slot_k, gw, h1, y, w_sh_gate[0].astype(BF16), w_sh_up[0].astype(BF16),
                   w_sh_down[0].astype(BF16), row(ln2_g[0]), row(ln2_b[0]), tt)
    return out.reshape(nb, seq, d).astype(x.dtype)


def kernel(x, meta_tokens, ln_emb_g, ln_emb_b, w_in, hg_lb_logits, hg_norm_g, ml_conv_w, ml_conv_b, ml_ig_bias, ml_fg_bias, ml_norm_g, w_branch_a, w_branch_b, w_out, ln1_g, ln1_b, w_router, router_bias, w_exp_gate, w_exp_up, w_exp_down, w_sh_gate, w_sh_up, w_sh_down, ln2_g, ln2_b):
    return _forward(x, meta_tokens, ln_emb_g, ln_emb_b, w_in, hg_lb_logits, hg_norm_g, ml_conv_w, ml_conv_b,
                    ml_ig_bias, ml_fg_bias, ml_norm_g, w_branch_a, w_branch_b, w_out, ln1_g, ln1_b,
                    w_router, router_bias, w_exp_gate, w_exp_up, w_exp_down, w_sh_gate, w_sh_up, w_sh_down,
                    ln2_g, ln2_b, chunk=_pick_tile(x.shape[1], 256))
```

```python
import functools

import jax
import jax.numpy as jnp
from jax import lax
from jax.experimental import pallas as pl
from jax.experimental.pallas import tpu as pltpu

F32, BF16, I32 = jnp.float32, jnp.bfloat16, jnp.int32

D_MODEL = 1024
N_META = 16
HG_HEADS = 8
HG_DK = 128
ML_HEADS = 4
ML_DK = 128
ML_DV = 256
ML_AUG = ML_DV + 128
N_EXPERTS = 64
TOP_K = 8
N_GROUPS = 8
GROUP_SIZE = N_EXPERTS // N_GROUPS
TOPK_GROUPS = 4
D_EXPERT = 256
ROUTED_SCALE = 2.5
MOE_BLOCK = 2048
SLOT_BLOCK = 256
DN_ALPHA = 2.0 ** 0.25
EPS = 1e-5
LOG2E = 1.4426950408889634
EXP2_CLAMP = 115.0
SUBLANES = 8
GROUP_ROWS = 16

P_QA, P_KA, P_IA, P_GA, P_QKB, P_VB, P_OB, P_MA, P_MB = range(9)
N_SLABS = 9

VMEM_LIMIT = 56 * 1024 * 1024


def _params(*sem):
    return pltpu.CompilerParams(dimension_semantics=sem, vmem_limit_bytes=VMEM_LIMIT)


def _sigmoid(x):
    return 1.0 / (1.0 + jnp.exp(-x))


def _log_sigmoid(x):
    return jnp.minimum(x, 0.0) - jnp.log(1.0 + jnp.exp(-jnp.abs(x)))


def _layer_norm(x, g, b):
    xc = x - jnp.mean(x, axis=-1, keepdims=True)
    var = jnp.mean(xc * xc, axis=-1, keepdims=True)
    return xc * lax.rsqrt(var + EPS) * g + b


def _dot(a, b):
    return jnp.dot(a, b, preferred_element_type=F32)


def _dot_nt(a, b):
    return lax.dot_general(a, b, (((1,), (1,)), ((), ())), preferred_element_type=F32)


def _dot_tn(a, b):
    return lax.dot_general(a, b, (((0,), (0,)), ((), ())), preferred_element_type=F32)


def _split_bf16(x):
    hi = x.astype(BF16)
    lo = (x - hi.astype(F32)).astype(BF16)
    return hi, lo


def _neg_abs(x):
    return lax.bitcast_convert_type(lax.bitcast_convert_type(x, I32) | jnp.int32(-2 ** 31), F32)


def _ones_where(cond):
    return jnp.where(cond, 1.0, 0.0).astype(BF16)


def _inproj_pieces(x_ref, g_ref, b_ref, w_ref, ws_ref, lb_ref, gb_ref, wst_ref, gbt_ref,
                   h0_ref, p_ref, lf_ref, sg_ref, sgt_ref):
    h0 = _layer_norm(x_ref[...], g_ref[...], b_ref[...])
    h0_ref[...] = h0
    hb = h0.astype(BF16)

    def slab(n):
        def run():
            acc = _dot(hb, w_ref[:, n * D_MODEL:(n + 1) * D_MODEL])
            if n == P_KA:
                lb = lb_ref[...]
                f = lb + (1.0 - lb) * _sigmoid(acc)
                lf_ref[...] = jnp.log(f)
                acc = 1.0 - f
            elif n == P_GA:
                acc = acc * _sigmoid(acc)
            elif n >= P_OB:
                acc = _sigmoid(acc)
            p_ref[n] = acc.astype(BF16)
        return run

    def gates():
        s = _dot(hb, ws_ref[...]) + gb_ref[...]
        lane = lax.broadcasted_iota(I32, s.shape, 1)
        sg_ref[...] = jnp.where(lane < ML_HEADS, s, _log_sigmoid(s))
        st = _dot_nt(wst_ref[...], hb) + gbt_ref[:, 0:1]
        srow = lax.broadcasted_iota(I32, st.shape, 0)
        sgt_ref[...] = jnp.where(srow < ML_HEADS, st, _log_sigmoid(st))[:SUBLANES, :]

    return [slab(n) for n in range(N_SLABS)], gates


def _block_rows(b, block, pick):
    c, w = b.shape
    parts = [jnp.broadcast_to(b[j * block + pick:j * block + pick + 1, :], (block, w))
             for j in range(c // block)]
    return parts[0] if len(parts) == 1 else jnp.concatenate(parts, axis=0)


def _hgrn_body(q_ref, k_ref, v_ref, g_ref, lf_ref, ng_ref, y_ref, s_scr, cs):
    row = lax.broadcasted_iota(I32, (cs, cs), 0)
    col = lax.broadcasted_iota(I32, (cs, cs), 1)
    tri = _ones_where(col <= row)
    lf_hi, lf_lo = _split_bf16(lf_ref[...])
    b = (_dot(tri, lf_hi) + _dot(tri, lf_lo)) * LOG2E
    q = q_ref[...]
    k = k_ref[...]
    v = v_ref[...]
    blast = b[cs - 1:cs, :]
    qg = q * jnp.exp2(b).astype(BF16)
    kg = k * jnp.exp2(blast - b).astype(BF16)
    dec = jnp.exp2(blast)

    levels = []
    m = SUBLANES
    while 2 * m <= cs:
        w = jnp.exp2(_neg_abs(b - _block_rows(b, 2 * m, m - 1))).astype(BF16)
        sh = (2 * m).bit_length() - 1
        mask = ((row >> sh) == (col >> sh)) & ((row & (2 * m - 1)) >= m) & ((col & (2 * m - 1)) < m)
        levels.append((q * w, k * w, mask))
        m *= 2
    e = jnp.clip(b - _block_rows(b, SUBLANES, SUBLANES // 2 - 1), -EXP2_CLAMP, EXP2_CLAMP)
    levels.append((q * jnp.exp2(e).astype(BF16), k * jnp.exp2(-e).astype(BF16),
                   ((row >> 3) == (col >> 3)) & (col <= row)))

    ng = ng_ref[...]
    for h in range(HG_HEADS):
        sl = slice(h * HG_DK, (h + 1) * HG_DK)
        st = s_scr[h]
        o = _dot_nt(qg[:, sl], st.astype(BF16))
        sc = jnp.zeros((cs, cs), F32)
        for lq, lk, mask in levels:
            sc = jnp.where(mask, _dot_nt(lq[:, sl], lk[:, sl]), sc)
        o = o + _dot(sc.astype(BF16), v[:, sl])
        s_scr[h] = dec[:, sl] * st + _dot_tn(v[:, sl], kg[:, sl])
        ms = jnp.mean(o * o, axis=-1, keepdims=True)
        y = o * lax.rsqrt(ms + EPS) * ng[:, sl] * g_ref[:, sl].astype(F32)
        y_ref[:, sl] = y.astype(BF16)


def _mlstm_conv(qk_ref, cw_ref, cb_ref, x_scr, cs):
    x = qk_ref[...].astype(F32)
    prev = x_scr[...]
    sub = lax.broadcasted_iota(I32, (SUBLANES, D_MODEL), 0)
    cw = cw_ref[...]
    conv = cw[3:4, :] * x + cb_ref[...]
    for j in (1, 2, 3):
        xs = pltpu.roll(x, j, 0)
        head = jnp.where(sub < j, pltpu.roll(prev, j, 0), xs[:SUBLANES, :])
        xs = jnp.concatenate([head, xs[SUBLANES:, :]], axis=0)
        conv = conv + cw[3 - j:4 - j, :] * xs
    x_scr[...] = x[cs - SUBLANES:, :]
    qk = conv * _sigmoid(conv)
    return (qk[:, :ML_HEADS * ML_DK] * (ML_DK ** -0.5)).astype(BF16), qk[:, ML_HEADS * ML_DK:]


def _mlstm_cumsums(sg_ref, sgt_ref, cs):
    row = lax.broadcasted_iota(I32, (cs, cs), 0)
    col = lax.broadcasted_iota(I32, (cs, cs), 1)
    tri = _ones_where(col <= row)
    sg_hi, sg_lo = _split_bf16(sg_ref[...])
    sgt_hi, sgt_lo = _split_bf16(sgt_ref[...])
    return _dot(tri, sg_hi) + _dot(tri, sg_lo), _dot_nt(sgt_hi, tri) + _dot_nt(sgt_lo, tri)


def _mlstm_heads(q_all, k_all, bcol_all, brow_all, v_ref, og_ref, sg_ref, sgt_ref, ng_ref, y_ref, c_scr, m_scr, cs):
    row = lax.broadcasted_iota(I32, (cs, cs), 0)
    col = lax.broadcasted_iota(I32, (cs, cs), 1)
    causal = col <= row
    sg = sg_ref[...]
    sgt = sgt_ref[...]
    lane128 = lax.broadcasted_iota(I32, (cs, 128), 1)
    ones_col = _ones_where(lane128 == 0)
    v = v_ref[...]
    ng = ng_ref[...]

    for h in range(ML_HEADS):
        b_col = bcol_all[:, ML_HEADS + h:ML_HEADS + h + 1]
        b_row = brow_all[ML_HEADS + h:ML_HEADS + h + 1, :]
        ig_col = sg[:, h:h + 1]
        ig_row = sgt[h:h + 1, :]
        m_prev = m_scr[h:h + 1, 0:1]
        q_h = q_all[:, h * ML_DK:(h + 1) * ML_DK]
        k_h = k_all[:, h * ML_DK:(h + 1) * ML_DK]
        v_aug = jnp.concatenate([v[:, h * ML_DV:(h + 1) * ML_DV], ones_col], axis=1)
        c_st = c_scr[h]

        log_intra = jnp.where(causal, b_col - b_row + ig_row, -jnp.inf)
        log_inter = b_col + m_prev
        m_t = jnp.maximum(log_inter, jnp.max(log_intra, axis=-1, keepdims=True))
        w_intra = jnp.exp(log_intra - m_t)
        w_inter = jnp.exp(log_inter - m_t)
        s = _dot_nt(q_h, k_h.astype(BF16)) * w_intra
        tot = w_inter * _dot(q_h, c_st.astype(BF16)) + _dot(s.astype(BF16), v_aug)
        num = tot[:, :ML_DV]
        den = tot[:, ML_DV:ML_DV + 1]
        hid = num / jnp.maximum(jnp.abs(den), jnp.exp(-m_t))

        b_last = b_col[cs - 1:cs, :]
        log_w = b_last - b_col + ig_col
        m_new = jnp.maximum(b_last + m_prev, jnp.max(log_w, axis=0, keepdims=True))
        w_s = jnp.exp(log_w - m_new)
        decay = jnp.exp(b_last + m_prev - m_new)
        c_scr[h] = decay * c_st + _dot_tn((k_h * w_s).astype(BF16), v_aug)
        m_scr[h:h + 1, :] = jnp.broadcast_to(m_new, (1, 128))

        hc = hid - jnp.mean(hid, axis=-1, keepdims=True)
        var = jnp.mean(hc * hc, axis=-1, keepdims=True)
        sl = slice(h * ML_DV, (h + 1) * ML_DV)
        y = hc * lax.rsqrt(var + EPS) * ng[:, sl] * og_ref[:, sl].astype(F32)
        y_ref[:, sl] = y.astype(BF16)


def _merge_body(h0_ref, ya_ref, yb_ref, ma_ref, mb_ref, wa_ref, wb_ref, wo_ref, pre_ref):
    merged = (ma_ref[...].astype(F32) * _dot(ya_ref[...], wa_ref[...])
              + mb_ref[...].astype(F32) * _dot(yb_ref[...], wb_ref[...]))
    pre_ref[...] = DN_ALPHA * h0_ref[...] + _dot(merged.astype(BF16), wo_ref[...])


def _router_pieces(pre_ref, g1_ref, b1_ref, wrh_ref, wrl_ref, rb_ref, h1_ref, slot_ref, gw_ref, cnt_ref, tm):
    neg_inf = -jnp.inf
    v = {}

    def ln1():
        v['h1'] = _layer_norm(pre_ref[...], g1_ref[...], b1_ref[...])
        h1_ref[...] = v['h1']

    def logits():
        h_hi, h_lo = _split_bf16(v['h1'])
        lg = _dot_nt(wrh_ref[...], h_hi) + _dot_nt(wrh_ref[...], h_lo) + _dot_nt(wrl_ref[...], h_hi)
        v['scores'] = _sigmoid(lg)
        v['biased'] = v['scores'] + rb_ref[:, 0:1]

    def group_scores():
        g3 = v['biased'].reshape(N_GROUPS, GROUP_SIZE, tm)
        sub3 = lax.broadcasted_iota(I32, g3.shape, 1)
        top1 = jnp.max(g3, axis=1, keepdims=True)
        first = jnp.min(jnp.where(g3 == top1, sub3, GROUP_SIZE), axis=1, keepdims=True)
        top2 = jnp.max(jnp.where(sub3 == first, neg_inf, g3), axis=1, keepdims=True)
        v['gs'] = (top1 + top2).reshape(N_GROUPS, tm)

    def group_select():
        gs = v['gs']
        gi = lax.broadcasted_iota(I32, gs.shape, 0)
        grank = jnp.zeros(gs.shape, F32)
        for j in range(N_GROUPS):
            r = gs[j:j + 1, :]
            grank = grank + jnp.where((r > gs) | ((r == gs) & (gi > j)), 1.0, 0.0)
        gsel = grank < float(TOPK_GROUPS)
        emask = jnp.broadcast_to(gsel.reshape(N_GROUPS, 1, tm), (N_GROUPS, GROUP_SIZE, tm)).reshape(N_EXPERTS, tm)
        v['work'] = jnp.where(emask, v['biased'], neg_inf)
        v['rank'] = jnp.full((N_EXPERTS, tm), float(N_EXPERTS), F32)

    def extract(kk):
        def run():
            ei = lax.broadcasted_iota(I32, (N_EXPERTS, tm), 0)
            for k2 in (kk, kk + 1):
                work = v['work']
                top = jnp.max(work, axis=0, keepdims=True)
                first = jnp.min(jnp.where(work == top, ei, N_EXPERTS), axis=0, keepdims=True)
                hit = ei == first
                v['rank'] = jnp.where(hit, float(k2), v['rank'])
                v['work'] = jnp.where(hit, neg_inf, work)
        return run

    def slots():
        sel = v['rank'] < float(TOP_K)
        sel_w = jnp.where(sel, v['scores'], 0.0)
        v['gwd'] = sel_w / jnp.sum(sel_w, axis=0, keepdims=True) * ROUTED_SCALE
        tr = lax.broadcasted_iota(I32, (tm, tm), 0)
        tc = lax.broadcasted_iota(I32, (tm, tm), 1)
        sel_b = _ones_where(sel)
        rloc = _dot(sel_b, _ones_where(tr < tc))
        cnt = _dot(sel_b, jnp.ones((tm, 128), BF16))
        cnt_g = jnp.floor((cnt + (GROUP_ROWS - 1.0)) * (1.0 / GROUP_ROWS)) * GROUP_ROWS
        er = lax.broadcasted_iota(I32, (N_EXPERTS, N_EXPERTS), 0)
        ec = lax.broadcasted_iota(I32, (N_EXPERTS, N_EXPERTS), 1)
        seg_start = _dot(_ones_where(ec < er), cnt_g.astype(BF16))
        v['slot_e'] = seg_start[:, 0:1] + rloc
        v['sel'] = sel
        cnt_ref[...] = cnt

    def picks():
        s_rows, w_rows = [], []
        for kk in range(TOP_K):
            pick = v['sel'] & (v['rank'] == float(kk))
            s_rows.append(jnp.sum(jnp.where(pick, v['slot_e'], 0.0), axis=0, keepdims=True))
            w_rows.append(jnp.sum(jnp.where(pick, v['gwd'], 0.0), axis=0, keepdims=True))
        slot_ref[...] = jnp.concatenate(s_rows, axis=0).astype(I32)
        gw_ref[...] = jnp.concatenate(w_rows, axis=0)

    return [ln1, logits, group_scores, group_select] + [extract(kk) for kk in range(0, TOP_K, 2)] + [slots, picks]


def _front_kernel(x_ref, eg_ref, eb_ref, w_ref, ws_ref, lb_ref, gb_ref, wst_ref, gbt_ref,
                  hng_ref, s0_ref, cw_ref, cb_ref, mng_ref, c0_ref, m0_ref, x0_ref,
                  wa_ref, wb_ref, wo_ref, g1_ref, b1_ref, wrh_ref, wrl_ref, rb_ref,
                  h1_ref, slot_ref, gw_ref, cnt_ref, sfin_ref, cfin_ref, mfin_ref, xfin_ref,
                  p_scr, lf_scr, sg_scr, sgt_scr, ya_scr, yb_scr, h0_scr, hp_scr, s_scr, c_scr, m_scr, x_scr,
                  *, chunk, chunks_per_seq):
    t = pl.program_id(0)
    nt = pl.num_programs(0) - 1

    @pl.when(lax.rem(t, chunks_per_seq) == 0)
    def _():
        s_scr[...] = s0_ref[...]
        c_scr[...] = c0_ref[...]
        m_scr[...] = m0_ref[...]
        x_scr[...] = x0_ref[...]

    @pl.when(t == 0)
    def _():
        hp_scr[...] = jnp.zeros_like(hp_scr)

    def router():
        return _router_pieces(hp_scr, g1_ref, b1_ref, wrh_ref, wrl_ref, rb_ref, h1_ref, slot_ref, gw_ref, cnt_ref,
                              chunk)

    @pl.when(t < nt)
    def _():
        slabs, gates = _inproj_pieces(x_ref, eg_ref, eb_ref, w_ref, ws_ref, lb_ref, gb_ref, wst_ref, gbt_ref,
                                      h0_scr, p_scr, lf_scr, sg_scr, sgt_scr)
        ml = {}

        def conv():
            ml['q'], ml['k'] = _mlstm_conv(p_scr.at[P_QKB], cw_ref, cb_ref, x_scr, chunk)

        def cumsums():
            ml['bcol'], ml['brow'] = _mlstm_cumsums(sg_scr, sgt_scr, chunk)

        r = router()
        light = [[], [r[0]], [conv, r[1]], [cumsums, r[2]]] + [[p] for p in r[3:8]] + [r[8:]]
        heavy = [slabs[P_QKB], gates] + [slabs[n] for n in range(N_SLABS) if n != P_QKB]
        for piece, fill in zip(heavy, light):
            piece()
            for f in fill:
                f()
        _hgrn_body(p_scr.at[P_QA], p_scr.at[P_KA], p_scr.at[P_IA], p_scr.at[P_GA], lf_scr, hng_ref, ya_scr, s_scr,
                   chunk)
        _mlstm_heads(ml['q'], ml['k'], ml['bcol'], ml['brow'], p_scr.at[P_VB], p_scr.at[P_OB], sg_scr, sgt_scr,
                     mng_ref, yb_scr, c_scr, m_scr, chunk)
        _merge_body(h0_scr, ya_scr, yb_scr, p_scr.at[P_MA], p_scr.at[P_MB], wa_ref, wb_ref, wo_ref, hp_scr)

    @pl.when(t == nt)
    def _():
        for piece in router():
            piece()

    @pl.when(t == nt - 1)
    def _():
        sfin_ref[...] = s_scr[...]
        cfin_ref[...] = c_scr[...]
        mfin_ref[...] = m_scr[...]
        xfin_ref[...] = x_scr[...]


def _front(x2d, weights, states, nb, chunk):
    m = x2d.shape[0]
    nt = m // chunk
    s0, c0, m0, x0 = states
    cur_rows = lambda t: (jnp.minimum(t, nt - 1), 0)
    prev_lanes = lambda t: (0, jnp.maximum(t - 1, 0))
    const2 = lambda t: (0, 0)
    const3 = lambda t: (0, 0, 0)

    def resident(a):
        return pl.BlockSpec(a.shape, const2 if a.ndim == 2 else const3, pipeline_mode=pl.Buffered(1))

    outs = pl.pallas_call(
        functools.partial(_front_kernel, chunk=chunk, chunks_per_seq=nt // nb),
        grid=(nt + 1,),
        in_specs=[pl.BlockSpec((chunk, D_MODEL), cur_rows)] + [resident(a) for a in weights[:8]]
        + [resident(weights[8]), resident(s0)] + [resident(a) for a in weights[9:12]]
        + [resident(c0), resident(m0), resident(x0)] + [resident(a) for a in weights[12:]],
        out_specs=[
            pl.BlockSpec((chunk, D_MODEL), lambda t: (jnp.maximum(t - 1, 0), 0)),
            pl.BlockSpec((TOP_K, chunk), prev_lanes),
            pl.BlockSpec((TOP_K, chunk), prev_lanes),
            pl.BlockSpec((None, N_EXPERTS, 128), lambda t: (jnp.maximum(t - 1, 0), 0, 0)),
            pl.BlockSpec(s0.shape, const3),
            pl.BlockSpec(c0.shape, const3),
            pl.BlockSpec(m0.shape, const2),
            pl.BlockSpec(x0.shape, const2),
        ],
        out_shape=[
            jax.ShapeDtypeStruct((m, D_MODEL), F32),
            jax.ShapeDtypeStruct((TOP_K, m), I32),
            jax.ShapeDtypeStruct((TOP_K, m), F32),
            jax.ShapeDtypeStruct((nt, N_EXPERTS, 128), F32),
            jax.ShapeDtypeStruct(s0.shape, F32),
            jax.ShapeDtypeStruct(c0.shape, F32),
            jax.ShapeDtypeStruct(m0.shape, F32),
            jax.ShapeDtypeStruct(x0.shape, F32),
        ],
        scratch_shapes=[
            pltpu.VMEM((N_SLABS, chunk, D_MODEL), BF16),
            pltpu.VMEM((chunk, D_MODEL), F32),
            pltpu.VMEM((chunk, 128), F32),
            pltpu.VMEM((SUBLANES, chunk), F32),
            pltpu.VMEM((chunk, D_MODEL), BF16),
            pltpu.VMEM((chunk, D_MODEL), BF16),
            pltpu.VMEM((chunk, D_MODEL), F32),
            pltpu.VMEM((chunk, D_MODEL), F32),
            pltpu.VMEM((HG_HEADS, HG_DK, HG_DK), F32),
            pltpu.VMEM((ML_HEADS, ML_DK, ML_AUG), F32),
            pltpu.VMEM((SUBLANES, 128), F32),
            pltpu.VMEM((SUBLANES, D_MODEL), F32),
        ],
        compiler_params=_params("arbitrary"),
        name="front",
    )(x2d, *weights[:9], s0, *weights[9:12], c0, m0, x0, *weights[12:])
    return outs[0], outs[1], outs[2], outs[3], tuple(outs[4:])


def _tile_slots(tt):
    return -(-(TOP_K * tt + N_EXPERTS * (GROUP_ROWS - 1)) // SLOT_BLOCK) * SLOT_BLOCK


def _wait_groups(n, make_copy, s_tile):
    p = 1 << ((s_tile // GROUP_ROWS).bit_length() - 1)
    while p:
        @pl.when((n & p) != 0)
        def _():
            make_copy(p * GROUP_ROWS).wait()
        p >>= 1


def _block_relative(slots, r):
    return jnp.clip(slots - r * SLOT_BLOCK, -1, SLOT_BLOCK).astype(F32).astype(BF16)


def _for_slot_blocks(n, tt, s_tile, body):
    always = TOP_K * tt // SLOT_BLOCK
    for r in range(always):
        body(r)
    for r in range(always, s_tile // SLOT_BLOCK):
        @pl.when(n * GROUP_ROWS > r * SLOT_BLOCK)
        def _():
            body(r)


def _issue_copies(n2, n1, plan_ref, make_copy, s_tile):
    max2 = s_tile // (2 * GROUP_ROWS)
    base1 = 2 * max2

    def start2(p, c):
        make_copy(plan_ref[0, 0, p], plan_ref[0, 0, max2 + p], 2 * GROUP_ROWS).start()
        return c

    def start1(q, c):
        make_copy(plan_ref[0, 0, base1 + q], plan_ref[0, 0, base1 + N_EXPERTS + q], GROUP_ROWS).start()
        return c

    lax.fori_loop(0, n2, start2, 0)
    lax.fori_loop(0, n1, start1, 0)


def _plan_len(s_tile):
    return 2 * (s_tile // (2 * GROUP_ROWS)) + 2 * N_EXPERTS


def _dispatch_kernel(ng_ref, n2_ref, n1_ref, plan_ref, slot_ref, h_ref, xs_ref, buf, sem, *, tt, s_tile):
    i = pl.program_id(0)
    cur = lax.rem(i, 2)

    def list_to_global(list_row, global_row, rows):
        src = buf.at[cur, pl.ds(pl.multiple_of(list_row, GROUP_ROWS), rows), :]
        dst = xs_ref.at[pl.ds(pl.multiple_of(global_row, GROUP_ROWS), rows), :]
        return pltpu.make_async_copy(src, dst, sem.at[cur])

    def wait_tile(j, b):
        _wait_groups(ng_ref[j], lambda rows: pltpu.make_async_copy(
            buf.at[b, pl.ds(0, rows), :], xs_ref.at[pl.ds(0, rows), :], sem.at[b]), s_tile)

    @pl.when(i >= 2)
    def _():
        wait_tile(i - 2, cur)

    hb = h_ref[...].astype(BF16)
    sl = slot_ref[...]

    row_id = lax.broadcasted_iota(I32, (SLOT_BLOCK, tt), 0).astype(F32).astype(BF16)
    one = jnp.ones((), BF16)

    def fill_block(r):
        rel = _block_relative(sl, r)
        p = jnp.zeros((SLOT_BLOCK, tt), BF16)
        for kk in range(TOP_K):
            p = jnp.where(row_id == rel[kk:kk + 1, :], one, p)
        buf[cur, r * SLOT_BLOCK:(r + 1) * SLOT_BLOCK, :] = _dot(p, hb).astype(BF16)

    _for_slot_blocks(ng_ref[i], tt, s_tile, fill_block)

    _issue_copies(n2_ref[i], n1_ref[i], plan_ref, list_to_global, s_tile)

    @pl.when(i == pl.num_programs(0) - 1)
    def _():
        @pl.when(i >= 1)
        def _():
            wait_tile(i - 1, 1 - cur)
        wait_tile(i, cur)


def _dispatch(counts, plan, slot_k, h1, n_slots, tt):
    m = h1.shape[0]
    s_tile = _tile_slots(tt)
    grid_spec = pltpu.PrefetchScalarGridSpec(
        num_scalar_prefetch=3,
        grid=(m // tt,),
        in_specs=[
            pl.BlockSpec((1, 1, _plan_len(s_tile)), lambda i, *_: (i, 0, 0), memory_space=pltpu.SMEM),
            pl.BlockSpec((TOP_K, tt), lambda i, *_: (0, i)),
            pl.BlockSpec((tt, D_MODEL), lambda i, *_: (i, 0)),
        ],
        out_specs=pl.BlockSpec(memory_space=pl.ANY),
        scratch_shapes=[pltpu.VMEM((2, s_tile, D_MODEL), BF16), pltpu.SemaphoreType.DMA((2,))],
    )
    return pl.pallas_call(
        functools.partial(_dispatch_kernel, tt=tt, s_tile=s_tile),
        grid_spec=grid_spec,
        out_shape=jax.ShapeDtypeStruct((n_slots, D_MODEL), BF16),
        compiler_params=_params("arbitrary"),
        name="dispatch",
    )(*counts, plan, slot_k, h1)


def _experts_kernel(be_ref, nu_ref, x_ref, wg_ref, wu_ref, wd_ref, y_ref, wg_b, wu_b, wd_b):
    i = pl.program_id(0)
    used = i < nu_ref[0]

    @pl.when(used & ((i == 0) | (be_ref[i] != be_ref[jnp.maximum(i - 1, 0)])))
    def _():
        wg_b[...] = wg_ref[...].astype(BF16)
        wu_b[...] = wu_ref[...].astype(BF16)
        wd_b[...] = wd_ref[...].astype(BF16)

    @pl.when(used)
    def _():
        xb = x_ref[...]
        a = _dot(xb, wg_b[...])
        u = _dot(xb, wu_b[...])
        y_ref[...] = _dot((a * _sigmoid(a) * u).astype(BF16), wd_b[...]).astype(BF16)


def _experts(blk_expert, n_used, xs, wg, wu, wd):
    n_slots = xs.shape[0]
    n_blocks = n_slots // MOE_BLOCK
    blk = lambda i, be, nu: (jnp.minimum(i, nu[0] - 1), 0)
    wsel = lambda i, be, nu: (be[jnp.minimum(i, nu[0] - 1)], 0, 0)
    grid_spec = pltpu.PrefetchScalarGridSpec(
        num_scalar_prefetch=2,
        grid=(n_blocks,),
        in_specs=[
            pl.BlockSpec((MOE_BLOCK, D_MODEL), blk),
            pl.BlockSpec((None, D_MODEL, D_EXPERT), wsel),
            pl.BlockSpec((None, D_MODEL, D_EXPERT), wsel),
            pl.BlockSpec((None, D_EXPERT, D_MODEL), wsel),
        ],
        out_specs=pl.BlockSpec((MOE_BLOCK, D_MODEL), blk),
        scratch_shapes=[pltpu.VMEM((D_MODEL, D_EXPERT), BF16), pltpu.VMEM((D_MODEL, D_EXPERT), BF16),
                        pltpu.VMEM((D_EXPERT, D_MODEL), BF16)],
    )
    return pl.pallas_call(
        _experts_kernel,
        grid_spec=grid_spec,
        out_shape=jax.ShapeDtypeStruct((n_slots, D_MODEL), BF16),
        compiler_params=_params("arbitrary"),
        name="experts",
    )(blk_expert, n_used, xs, wg, wu, wd)


def _combine_kernel(ng_ref, n2_ref, n1_ref, gcur_ref, gnext_ref, slot_ref, gw_ref, h_ref, y_ref, sg_ref, su_ref, sd_ref,
                    g2_ref, b2_ref, o_ref, ybuf, acc, sem, *, tt, s_tile):
    i = pl.program_id(0)
    cur = lax.rem(i, 2)

    def fetch(j, b, plan_ref):
        def global_to_list(list_row, global_row, rows):
            src = y_ref.at[pl.ds(pl.multiple_of(global_row, GROUP_ROWS), rows), :]
            dst = ybuf.at[b, pl.ds(pl.multiple_of(list_row, GROUP_ROWS), rows), :]
            return pltpu.make_async_copy(src, dst, sem.at[b])

        _issue_copies(n2_ref[j], n1_ref[j], plan_ref, global_to_list, s_tile)

    @pl.when(i == 0)
    def _():
        ybuf[...] = jnp.zeros_like(ybuf)
        fetch(0, 0, gcur_ref)

    @pl.when(i + 1 < pl.num_programs(0))
    def _():
        fetch(i + 1, 1 - cur, gnext_ref)

    h1 = h_ref[...]
    hb = h1.astype(BF16)
    a = _dot(hb, sg_ref[...])
    u = _dot(hb, su_ref[...])
    shared = _dot((a * _sigmoid(a) * u).astype(BF16), sd_ref[...])

    slot = slot_ref[...]
    gwb = gw_ref[...].astype(BF16)
    _wait_groups(ng_ref[i], lambda rows: pltpu.make_async_copy(
        y_ref.at[pl.ds(0, rows), :], ybuf.at[cur, pl.ds(0, rows), :], sem.at[cur]), s_tile)

    n_always, n_all = TOP_K * tt // SLOT_BLOCK, s_tile // SLOT_BLOCK
    per = 2 if (n_always % 2 == 0 and n_all % 2 == 0) else 1
    rows = per * SLOT_BLOCK
    row_id = lax.broadcasted_iota(I32, (SLOT_BLOCK, tt), 0).astype(F32).astype(BF16)

    def one_hot(r):
        rel = _block_relative(slot, r)
        p = jnp.zeros((SLOT_BLOCK, tt), BF16)
        for kk in range(TOP_K):
            p = jnp.where(row_id == rel[kk:kk + 1, :], gwb[kk:kk + 1, :], p)
        return p

    def block_dot(r):
        p = jnp.concatenate([one_hot(per * r + j) for j in range(per)], axis=0) if per > 1 else one_hot(r)
        return _dot_tn(p, ybuf[cur, r * rows:(r + 1) * rows, :])

    routed = block_dot(0)
    for r in range(1, n_always // per):
        routed = routed + block_dot(r)
    acc[...] = DN_ALPHA * h1 + (routed + shared)
    for r in range(n_always // per, n_all // per):
        @pl.when(ng_ref[i] * GROUP_ROWS > r * rows)
        def _():
            acc[...] += block_dot(r)
    o_ref[...] = _layer_norm(acc[...], g2_ref[...], b2_ref[...])


def _combine(counts, plan, slot_k, gw_k, h1, y, wsg, wsu, wsd, g2, b2, tt):
    m = h1.shape[0]
    nt = m // tt
    s_tile = _tile_slots(tt)
    const = lambda i, *_: (0, 0)
    table = lambda f: pl.BlockSpec((1, 1, _plan_len(s_tile)), f, memory_space=pltpu.SMEM)
    grid_spec = pltpu.PrefetchScalarGridSpec(
        num_scalar_prefetch=3,
        grid=(nt,),
        in_specs=[
            table(lambda i, *_: (i, 0, 0)),
            table(lambda i, *_: (jnp.minimum(i + 1, nt - 1), 0, 0)),
            pl.BlockSpec((TOP_K, tt), lambda i, *_: (0, i)),
            pl.BlockSpec((TOP_K, tt), lambda i, *_: (0, i)),
            pl.BlockSpec((tt, D_MODEL), lambda i, *_: (i, 0)),
            pl.BlockSpec(memory_space=pl.ANY),
            pl.BlockSpec((D_MODEL, D_EXPERT), const),
            pl.BlockSpec((D_MODEL, D_EXPERT), const),
            pl.BlockSpec((D_EXPERT, D_MODEL), const),
            pl.BlockSpec((1, D_MODEL), const),
            pl.BlockSpec((1, D_MODEL), const),
        ],
        out_specs=pl.BlockSpec((tt, D_MODEL), lambda i, *_: (i, 0)),
        scratch_shapes=[pltpu.VMEM((2, s_tile, D_MODEL), BF16), pltpu.VMEM((tt, D_MODEL), F32),
                        pltpu.SemaphoreType.DMA((2,))],
    )
    return pl.pallas_call(
        functools.partial(_combine_kernel, tt=tt, s_tile=s_tile),
        grid_spec=grid_spec,
        out_shape=jax.ShapeDtypeStruct((m, D_MODEL), F32),
        compiler_params=_params("arbitrary"),
        name="combine",
    )(*counts, plan, plan, slot_k, gw_k, h1, y, wsg, wsu, wsd, g2, b2)


def _pick_tile(m, pref):
    t = min(pref, m)
    while m % t:
        t //= 2
    return t


def _forward(x, meta_tokens, ln_emb_g, ln_emb_b, w_in, hg_lb_logits, hg_norm_g, ml_conv_w, ml_conv_b,
             ml_ig_bias, ml_fg_bias, ml_norm_g, w_branch_a, w_branch_b, w_out, ln1_g, ln1_b,
             w_router, router_bias, w_exp_gate, w_exp_up, w_exp_down, w_sh_gate, w_sh_up, w_sh_down,
             ln2_g, ln2_b, *, chunk):
    nb, seq, d = x.shape
    m = nb * seq
    row = lambda a: a.reshape(1, -1).astype(F32)

    w = w_in[0]
    kw = HG_HEADS * HG_DK
    o_qa, o_fa, o_ia, o_ga = 0, kw, 2 * kw, 3 * kw
    o_qb = 4 * kw
    o_kb = o_qb + ML_HEADS * ML_DK
    o_vb = o_kb + ML_HEADS * ML_DK
    o_ob = o_vb + ML_HEADS * ML_DV
    o_ig = o_ob + ML_HEADS * ML_DV
    o_fg = o_ig + ML_HEADS
    o_ma = o_fg + ML_HEADS
    o_mb = o_ma + D_MODEL
    cols = lambda o, n: w[:, o:o + n]
    w_cat = jnp.concatenate([
        cols(o_qa, kw), cols(o_fa, kw), cols(o_ia, kw), cols(o_ga, kw),
        cols(o_qb, 2 * ML_HEADS * ML_DK), cols(o_vb, ML_HEADS * ML_DV), cols(o_ob, ML_HEADS * ML_DV),
        cols(o_ma, D_MODEL), cols(o_mb, D_MODEL)], axis=1).astype(BF16)
    w_small = jnp.pad(cols(o_ig, 2 * ML_HEADS), ((0, 0), (0, 128 - 2 * ML_HEADS))).astype(BF16)
    gate_bias = jnp.pad(jnp.concatenate([ml_ig_bias[0], ml_fg_bias[0]]).astype(F32), (0, 128 - 2 * ML_HEADS)).reshape(1, 128)
    lb = jax.nn.softmax(hg_lb_logits.astype(F32), axis=0)[0].reshape(1, -1)
    eg, eb = row(ln_emb_g), row(ln_emb_b)
    conv_w = ml_conv_w[0].astype(F32)
    conv_b = row(ml_conv_b[0])
    hgn, mln = row(hg_norm_g[0]), row(ml_norm_g[0])
    w_small_t = jnp.pad(cols(o_ig, 2 * ML_HEADS).T, ((0, GROUP_ROWS - 2 * ML_HEADS), (0, 0))).astype(BF16)
    gate_bias_t = jnp.broadcast_to(jnp.pad(gate_bias[0, :2 * ML_HEADS], (0, GROUP_ROWS - 2 * ML_HEADS))[:, None],
                                   (GROUP_ROWS, 128))
    wr = w_router[0].T.astype(F32)
    wr_hi, wr_lo = _split_bf16(wr)
    rbias = jnp.broadcast_to(router_bias[0].astype(F32).reshape(N_EXPERTS, 1), (N_EXPERTS, 128))
    weights = [eg, eb, w_cat, w_small, lb, gate_bias, w_small_t, gate_bias_t, hgn, conv_w, conv_b, mln,
               w_branch_a[0].astype(BF16), w_branch_b[0].astype(BF16), w_out[0].astype(BF16),
               row(ln1_g[0]), row(ln1_b[0]), wr_hi, wr_lo, rbias]

    zero_states = (jnp.zeros((HG_HEADS, HG_DK, HG_DK), F32), jnp.zeros((ML_HEADS, ML_DK, ML_AUG), F32),
                   jnp.zeros((SUBLANES, 128), F32), jnp.zeros((SUBLANES, D_MODEL), F32))
    meta_states = _front(meta_tokens.astype(F32), weights, zero_states, 1, N_META)[4]

    x2d = x.reshape(m, d).astype(F32)
    tt = chunk
    h1, slot_k, gw, cnt, _ = _front(x2d, weights, meta_states, nb, chunk)

    nt = m // tt
    s_tile = _tile_slots(tt)
    cnt8 = (cnt[:, :, 0].astype(I32) + GROUP_ROWS - 1) // GROUP_ROWS * GROUP_ROWS
    seg_end = jnp.cumsum(cnt8, axis=1)
    seg_off = seg_end - cnt8
    tile_rows = seg_end[:, -1]
    run = jnp.cumsum(cnt8, axis=0) - cnt8
    tot8 = jnp.sum(cnt8, axis=0)
    padded = (tot8 + MOE_BLOCK - 1) // MOE_BLOCK * MOE_BLOCK
    pends = jnp.cumsum(padded)
    gshift = (pends - padded)[None, :] + run - seg_off
    experts = jnp.arange(N_EXPERTS, dtype=I32)

    def copy_list(per_expert, length, list_row0, step):
        ends = jnp.cumsum(per_expert, axis=1)
        idx = jnp.arange(length, dtype=I32)
        owner = jnp.sum((ends[:, None, :] <= idx[None, :, None]).astype(I32), axis=-1)
        pick = lambda a: jnp.sum(jnp.where(owner[..., None] == experts, a[:, None, :], 0), axis=-1)
        list_row = pick(list_row0 - step * (ends - per_expert)) + step * idx[None, :]
        valid = idx[None, :] < ends[:, -1:]
        return jnp.where(valid, list_row, 0), jnp.where(valid, list_row + pick(gshift), 0), ends[:, -1]

    groups = cnt8 // GROUP_ROWS
    doubles = groups // 2
    l2, g2, n2 = copy_list(doubles, s_tile // (2 * GROUP_ROWS), seg_off, 2 * GROUP_ROWS)
    l1, g1, n1 = copy_list(groups % 2, N_EXPERTS, seg_off + 2 * GROUP_ROWS * doubles, 0)
    plan = jnp.concatenate([l2, g2, l1, g1], axis=1).astype(I32).reshape(nt, 1, -1)
    counts = ((tile_rows // GROUP_ROWS).astype(I32), n2.astype(I32), n1.astype(I32))
    n_blocks = -(-(m * TOP_K + nt * N_EXPERTS * (GROUP_ROWS - 1)) // MOE_BLOCK) + N_EXPERTS
    blk_start = jnp.arange(n_blocks, dtype=I32) * MOE_BLOCK
    blk_expert = jnp.minimum(jnp.sum((pends[None, :] <= blk_start[:, None]).astype(I32), axis=1), N_EXPERTS - 1)
    n_used = (pends[-1:] // MOE_BLOCK).astype(I32)

    xs = _dispatch(counts, plan, slot_k, h1, n_blocks * MOE_BLOCK, tt)
    y = _experts(blk_expert, n_used, xs, w_exp_gate[0].astype(F32), w_exp_up[0].astype(F32),
                 w_exp_down[0].astype(F32))
    out = _combine(counts, plan, slot_k, gw, h1, y, w_sh_gate[0].astype(BF16), w_sh_up[0].astype(BF16),
                   w_sh_down[0].astype(BF16), row(ln2_g[0]), row(ln2_b[0]), tt)
    return out.reshape(nb, seq, d).astype(x.dtype)


def kernel(x, meta_tokens, ln_emb_g, ln_emb_b, w_in, hg_lb_logits, hg_norm_g, ml_conv_w, ml_conv_b, ml_ig_bias, ml_fg_bias, ml_norm_g, w_branch_a, w_branch_b, w_out, ln1_g, ln1_b, w_router, router_bias, w_exp_gate, w_exp_up, w_exp_down, w_sh_gate, w_sh_up, w_sh_down, ln2_g, ln2_b):
    return _forward(x, meta_tokens, ln_emb_g, ln_emb_b, w_in, hg_lb_logits, hg_norm_g, ml_conv_w, ml_conv_b,
                    ml_ig_bias, ml_fg_bias, ml_norm_g, w_branch_a, w_branch_b, w_out, ln1_g, ln1_b,
                    w_router, router_bias, w_exp_gate, w_exp_up, w_exp_down, w_sh_gate, w_sh_up, w_sh_down,
                    ln2_g, ln2_b, chunk=_pick_tile(x.shape[1], 256))
```

```python
import functools

import jax
import jax.numpy as jnp
from jax import lax
from jax.experimental import pallas as pl
from jax.experimental.pallas import tpu as pltpu

F32, BF16, I32 = jnp.float32, jnp.bfloat16, jnp.int32

D_MODEL = 1024
N_META = 16
HG_HEADS = 8
HG_DK = 128
ML_HEADS = 4
ML_DK = 128
ML_DV = 256
ML_AUG = ML_DV + 128
N_EXPERTS = 64
TOP_K = 8
N_GROUPS = 8
GROUP_SIZE = N_EXPERTS // N_GROUPS
TOPK_GROUPS = 4
D_EXPERT = 256
ROUTED_SCALE = 2.5
MOE_BLOCK = 2048
SLOT_BLOCK = 256
DN_ALPHA = 2.0 ** 0.25
EPS = 1e-5
LOG2E = 1.4426950408889634
EXP2_CLAMP = 115.0
SUBLANES = 8
GROUP_ROWS = 16

P_QA, P_KA, P_IA, P_GA, P_QKB, P_VB, P_OB, P_MA, P_MB = range(9)
N_SLABS = 9

VMEM_LIMIT = 56 * 1024 * 1024


def _params(*sem):
    return pltpu.CompilerParams(dimension_semantics=sem, vmem_limit_bytes=VMEM_LIMIT)


def _sigmoid(x):
    return 1.0 / (1.0 + jnp.exp(-x))


def _log_sigmoid(x):
    return jnp.minimum(x, 0.0) - jnp.log(1.0 + jnp.exp(-jnp.abs(x)))


def _layer_norm(x, g, b):
    xc = x - jnp.mean(x, axis=-1, keepdims=True)
    var = jnp.mean(xc * xc, axis=-1, keepdims=True)
    return xc * lax.rsqrt(var + EPS) * g + b


def _dot(a, b):
    return jnp.dot(a, b, preferred_element_type=F32)


def _dot_nt(a, b):
    return lax.dot_general(a, b, (((1,), (1,)), ((), ())), preferred_element_type=F32)


def _dot_tn(a, b):
    return lax.dot_general(a, b, (((0,), (0,)), ((), ())), preferred_element_type=F32)


def _split_bf16(x):
    hi = x.astype(BF16)
    lo = (x - hi.astype(F32)).astype(BF16)
    return hi, lo


def _neg_abs(x):
    return lax.bitcast_convert_type(lax.bitcast_convert_type(x, I32) | jnp.int32(-2 ** 31), F32)


def _ones_where(cond):
    return jnp.where(cond, 1.0, 0.0).astype(BF16)


def _embed_norm(x_ref, g_ref, b_ref, h0_ref, hb_ref):
    h0 = _layer_norm(x_ref[...], g_ref[...], b_ref[...])
    h0_ref[...] = h0
    hb_ref[...] = h0.astype(BF16)


def _inproj_pieces(hb, w_ref, ws_ref, lb_ref, gb_ref, wst_ref, gbt_ref, p_ref, lf_ref, sg_ref, sgt_ref):
    def slab(n):
        def run():
            acc = _dot(hb, w_ref[:, n * D_MODEL:(n + 1) * D_MODEL])
            if n == P_KA:
                lb = lb_ref[...]
                f = lb + (1.0 - lb) * _sigmoid(acc)
                lf_ref[...] = jnp.log(f)
                acc = 1.0 - f
            elif n == P_GA:
                acc = acc * _sigmoid(acc)
            elif n >= P_OB:
                acc = _sigmoid(acc)
            p_ref[n] = acc.astype(BF16)
        return run

    def gates():
        s = _dot(hb, ws_ref[...]) + gb_ref[...]
        lane = lax.broadcasted_iota(I32, s.shape, 1)
        sg_ref[...] = jnp.where(lane < ML_HEADS, s, _log_sigmoid(s))
        st = _dot_nt(wst_ref[...], hb) + gbt_ref[:, 0:1]
        srow = lax.broadcasted_iota(I32, st.shape, 0)
        sgt_ref[...] = jnp.where(srow < ML_HEADS, st, _log_sigmoid(st))[:SUBLANES, :]

    return [slab(n) for n in range(N_SLABS)], gates


def _block_rows(b, block, pick):
    c, w = b.shape
    parts = [jnp.broadcast_to(b[j * block + pick:j * block + pick + 1, :], (block, w))
             for j in range(c // block)]
    return parts[0] if len(parts) == 1 else jnp.concatenate(parts, axis=0)


def _hgrn_body(q_ref, k_ref, v_ref, g_ref, lf_ref, ng_ref, y_ref, s_scr, cs):
    row = lax.broadcasted_iota(I32, (cs, cs), 0)
    col = lax.broadcasted_iota(I32, (cs, cs), 1)
    tri = _ones_where(col <= row)
    lf_hi, lf_lo = _split_bf16(lf_ref[...])
    b = (_dot(tri, lf_hi) + _dot(tri, lf_lo)) * LOG2E
    q = q_ref[...]
    k = k_ref[...]
    v = v_ref[...]
    blast = b[cs - 1:cs, :]
    qg = q * jnp.exp2(b).astype(BF16)
    kg = k * jnp.exp2(blast - b).astype(BF16)
    dec = jnp.exp2(blast)

    levels = []
    m = SUBLANES
    while 2 * m <= cs:
        w = jnp.exp2(_neg_abs(b - _block_rows(b, 2 * m, m - 1))).astype(BF16)
        sh = (2 * m).bit_length() - 1
        mask = ((row >> sh) == (col >> sh)) & ((row & (2 * m - 1)) >= m) & ((col & (2 * m - 1)) < m)
        levels.append((q * w, k * w, mask))
        m *= 2
    e = jnp.clip(b - _block_rows(b, SUBLANES, SUBLANES // 2 - 1), -EXP2_CLAMP, EXP2_CLAMP)
    levels.append((q * jnp.exp2(e).astype(BF16), k * jnp.exp2(-e).astype(BF16),
                   ((row >> 3) == (col >> 3)) & (col <= row)))

    ng = ng_ref[...]
    for h in range(HG_HEADS):
        sl = slice(h * HG_DK, (h + 1) * HG_DK)
        st = s_scr[h]
        o = _dot_nt(qg[:, sl], st.astype(BF16))
        sc = jnp.zeros((cs, cs), F32)
        for lq, lk, mask in levels:
            sc = jnp.where(mask, _dot_nt(lq[:, sl], lk[:, sl]), sc)
        o = o + _dot(sc.astype(BF16), v[:, sl])
        s_scr[h] = dec[:, sl] * st + _dot_tn(v[:, sl], kg[:, sl])
        ms = jnp.mean(o * o, axis=-1, keepdims=True)
        y = o * lax.rsqrt(ms + EPS) * ng[:, sl] * g_ref[:, sl].astype(F32)
        y_ref[:, sl] = y.astype(BF16)


def _mlstm_conv(qk_ref, cw_ref, cb_ref, x_scr, cs):
    x = qk_ref[...].astype(F32)
    prev = x_scr[...]
    sub = lax.broadcasted_iota(I32, (SUBLANES, D_MODEL), 0)
    cw = cw_ref[...]
    conv = cw[3:4, :] * x + cb_ref[...]
    for j in (1, 2, 3):
        xs = pltpu.roll(x, j, 0)
        head = jnp.where(sub < j, pltpu.roll(prev, j, 0), xs[:SUBLANES, :])
        xs = jnp.concatenate([head, xs[SUBLANES:, :]], axis=0)
        conv = conv + cw[3 - j:4 - j, :] * xs
    x_scr[...] = x[cs - SUBLANES:, :]
    qk = conv * _sigmoid(conv)
    return (qk[:, :ML_HEADS * ML_DK] * (ML_DK ** -0.5)).astype(BF16), qk[:, ML_HEADS * ML_DK:]


def _mlstm_cumsums(sg_ref, sgt_ref, cs):
    row = lax.broadcasted_iota(I32, (cs, cs), 0)
    col = lax.broadcasted_iota(I32, (cs, cs), 1)
    tri = _ones_where(col <= row)
    sg_hi, sg_lo = _split_bf16(sg_ref[...])
    sgt_hi, sgt_lo = _split_bf16(sgt_ref[...])
    return _dot(tri, sg_hi) + _dot(tri, sg_lo), _dot_nt(sgt_hi, tri) + _dot_nt(sgt_lo, tri)


def _mlstm_heads(q_all, k_all, bcol_all, brow_all, v_ref, og_ref, sg_ref, sgt_ref, ng_ref, y_ref, c_scr, m_scr, cs):
    row = lax.broadcasted_iota(I32, (cs, cs), 0)
    col = lax.broadcasted_iota(I32, (cs, cs), 1)
    causal = col <= row
    sg = sg_ref[...]
    sgt = sgt_ref[...]
    lane128 = lax.broadcasted_iota(I32, (cs, 128), 1)
    ones_col = _ones_where(lane128 == 0)
    v = v_ref[...]
    ng = ng_ref[...]

    for h in range(ML_HEADS):
        b_col = bcol_all[:, ML_HEADS + h:ML_HEADS + h + 1]
        b_row = brow_all[ML_HEADS + h:ML_HEADS + h + 1, :]
        ig_col = sg[:, h:h + 1]
        ig_row = sgt[h:h + 1, :]
        m_prev = m_scr[h:h + 1, 0:1]
        q_h = q_all[:, h * ML_DK:(h + 1) * ML_DK]
        k_h = k_all[:, h * ML_DK:(h + 1) * ML_DK]
        v_aug = jnp.concatenate([v[:, h * ML_DV:(h + 1) * ML_DV], ones_col], axis=1)
        c_st = c_scr[h]

        log_intra = jnp.where(causal, b_col - b_row + ig_row, -jnp.inf)
        log_inter = b_col + m_prev
        m_t = jnp.maximum(log_inter, jnp.max(log_intra, axis=-1, keepdims=True))
        w_intra = jnp.exp(log_intra - m_t)
        w_inter = jnp.exp(log_inter - m_t)
        s = _dot_nt(q_h, k_h.astype(BF16)) * w_intra
        tot = w_inter * _dot(q_h, c_st.astype(BF16)) + _dot(s.astype(BF16), v_aug)
        num = tot[:, :ML_DV]
        den = tot[:, ML_DV:ML_DV + 1]
        hid = num / jnp.maximum(jnp.abs(den), jnp.exp(-m_t))

        b_last = b_col[cs - 1:cs, :]
        log_w = b_last - b_col + ig_col
        m_new = jnp.maximum(b_last + m_prev, jnp.max(log_w, axis=0, keepdims=True))
        w_s = jnp.exp(log_w - m_new)
        decay = jnp.exp(b_last + m_prev - m_new)
        c_scr[h] = decay * c_st + _dot_tn((k_h * w_s).astype(BF16), v_aug)
        m_scr[h:h + 1, :] = jnp.broadcast_to(m_new, (1, 128))

        hc = hid - jnp.mean(hid, axis=-1, keepdims=True)
        var = jnp.mean(hc * hc, axis=-1, keepdims=True)
        sl = slice(h * ML_DV, (h + 1) * ML_DV)
        y = hc * lax.rsqrt(var + EPS) * ng[:, sl] * og_ref[:, sl].astype(F32)
        y_ref[:, sl] = y.astype(BF16)


def _merge_body(h0_ref, ya_ref, yb_ref, ma_ref, mb_ref, wa_ref, wb_ref, wo_ref, pre_ref, after_h0):
    resid = DN_ALPHA * h0_ref[...]
    after_h0()
    merged = (ma_ref[...].astype(F32) * _dot(ya_ref[...], wa_ref[...])
              + mb_ref[...].astype(F32) * _dot(yb_ref[...], wb_ref[...]))
    pre_ref[...] = resid + _dot(merged.astype(BF16), wo_ref[...])


def _router_pieces(pre_ref, g1_ref, b1_ref, wrh_ref, wrl_ref, rb_ref, h1_ref, slot_ref, gw_ref, cnt_ref, tm):
    neg_inf = -jnp.inf
    v = {}

    def ln1():
        v['h1'] = _layer_norm(pre_ref[...], g1_ref[...], b1_ref[...])
        h1_ref[...] = v['h1']

    def logits():
        h_hi, h_lo = _split_bf16(v['h1'])
        lg = _dot_nt(wrh_ref[...], h_hi) + _dot_nt(wrh_ref[...], h_lo) + _dot_nt(wrl_ref[...], h_hi)
        v['scores'] = _sigmoid(lg)
        v['biased'] = v['scores'] + rb_ref[:, 0:1]

    def group_scores():
        g3 = v['biased'].reshape(N_GROUPS, GROUP_SIZE, tm)
        sub3 = lax.broadcasted_iota(I32, g3.shape, 1)
        top1 = jnp.max(g3, axis=1, keepdims=True)
        first = jnp.min(jnp.where(g3 == top1, sub3, GROUP_SIZE), axis=1, keepdims=True)
        top2 = jnp.max(jnp.where(sub3 == first, neg_inf, g3), axis=1, keepdims=True)
        v['gs'] = (top1 + top2).reshape(N_GROUPS, tm)

    def group_select():
        gs = v['gs']
        gi = lax.broadcasted_iota(I32, gs.shape, 0)
        grank = jnp.zeros(gs.shape, F32)
        for j in range(N_GROUPS):
            r = gs[j:j + 1, :]
            grank = grank + jnp.where((r > gs) | ((r == gs) & (gi > j)), 1.0, 0.0)
        gsel = grank < float(TOPK_GROUPS)
        emask = jnp.broadcast_to(gsel.reshape(N_GROUPS, 1, tm), (N_GROUPS, GROUP_SIZE, tm)).reshape(N_EXPERTS, tm)
        v['work'] = jnp.where(emask, v['biased'], neg_inf)
        v['rank'] = jnp.full((N_EXPERTS, tm), float(N_EXPERTS), F32)

    def extract(kk):
        def run():
            ei = lax.broadcasted_iota(I32, (N_EXPERTS, tm), 0)
            for k2 in (kk, kk + 1):
                work = v['work']
                top = jnp.max(work, axis=0, keepdims=True)
                first = jnp.min(jnp.where(work == top, ei, N_EXPERTS), axis=0, keepdims=True)
                hit = ei == first
                v['rank'] = jnp.where(hit, float(k2), v['rank'])
                v['work'] = jnp.where(hit, neg_inf, work)
        return run

    def slots():
        sel = v['rank'] < float(TOP_K)
        sel_w = jnp.where(sel, v['scores'], 0.0)
        v['gwd'] = sel_w / jnp.sum(sel_w, axis=0, keepdims=True) * ROUTED_SCALE
        tr = lax.broadcasted_iota(I32, (tm, tm), 0)
        tc = lax.broadcasted_iota(I32, (tm, tm), 1)
        sel_b = _ones_where(sel)
        rloc = _dot(sel_b, _ones_where(tr < tc))
        cnt = _dot(sel_b, jnp.ones((tm, 128), BF16))
        cnt_g = jnp.floor((cnt + (GROUP_ROWS - 1.0)) * (1.0 / GROUP_ROWS)) * GROUP_ROWS
        er = lax.broadcasted_iota(I32, (N_EXPERTS, N_EXPERTS), 0)
        ec = lax.broadcasted_iota(I32, (N_EXPERTS, N_EXPERTS), 1)
        seg_start = _dot(_ones_where(ec < er), cnt_g.astype(BF16))
        v['slot_e'] = seg_start[:, 0:1] + rloc
        v['sel'] = sel
        cnt_ref[...] = cnt

    def picks():
        s_rows, w_rows = [], []
        for kk in range(TOP_K):
            pick = v['sel'] & (v['rank'] == float(kk))
            s_rows.append(jnp.sum(jnp.where(pick, v['slot_e'], 0.0), axis=0, keepdims=True))
            w_rows.append(jnp.sum(jnp.where(pick, v['gwd'], 0.0), axis=0, keepdims=True))
        slot_ref[...] = jnp.concatenate(s_rows, axis=0).astype(I32)
        gw_ref[...] = jnp.concatenate(w_rows, axis=0)

    return [ln1, logits, group_scores, group_select] + [extract(kk) for kk in range(0, TOP_K, 2)] + [slots, picks]


def _front_kernel(xf_ref, xn_ref, eg_ref, eb_ref, w_ref, ws_ref, lb_ref, gb_ref, wst_ref, gbt_ref,
                  hng_ref, s0_ref, cw_ref, cb_ref, mng_ref, c0_ref, m0_ref, x0_ref,
                  wa_ref, wb_ref, wo_ref, g1_ref, b1_ref, wrh_ref, wrl_ref, rb_ref,
                  h1_ref, slot_ref, gw_ref, cnt_ref, sfin_ref, cfin_ref, mfin_ref, xfin_ref,
                  p_scr, lf_scr, sg_scr, sgt_scr, ya_scr, yb_scr, h0_scr, hb_scr, hp_scr, s_scr, c_scr, m_scr, x_scr,
                  *, chunk, chunks_per_seq):
    t = pl.program_id(0)
    nt = pl.num_programs(0) - 1

    @pl.when(t == 0)
    def _():
        _embed_norm(xf_ref, eg_ref, eb_ref, h0_scr, hb_scr)

    @pl.when(lax.rem(t, chunks_per_seq) == 0)
    def _():
        s_scr[...] = s0_ref[...]
        c_scr[...] = c0_ref[...]
        m_scr[...] = m0_ref[...]
        x_scr[...] = x0_ref[...]

    @pl.when(t == 0)
    def _():
        hp_scr[...] = jnp.zeros_like(hp_scr)

    def router():
        return _router_pieces(hp_scr, g1_ref, b1_ref, wrh_ref, wrl_ref, rb_ref, h1_ref, slot_ref, gw_ref, cnt_ref,
                              chunk)

    @pl.when(t < nt)
    def _():
        slabs, gates = _inproj_pieces(hb_scr[...], w_ref, ws_ref, lb_ref, gb_ref, wst_ref, gbt_ref,
                                      p_scr, lf_scr, sg_scr, sgt_scr)
        ml = {}

        def conv():
            ml['q'], ml['k'] = _mlstm_conv(p_scr.at[P_QKB], cw_ref, cb_ref, x_scr, chunk)

        def cumsums():
            ml['bcol'], ml['brow'] = _mlstm_cumsums(sg_scr, sgt_scr, chunk)

        r = router()
        light = [[], [r[0]], [conv, r[1]], [cumsums, r[2]]] + [[p] for p in r[3:8]] + [r[8:]]
        heavy = [slabs[P_QKB], gates] + [slabs[n] for n in range(N_SLABS) if n != P_QKB]
        for piece, fill in zip(heavy, light):
            piece()
            for f in fill:
                f()
        _hgrn_body(p_scr.at[P_QA], p_scr.at[P_KA], p_scr.at[P_IA], p_scr.at[P_GA], lf_scr, hng_ref, ya_scr, s_scr,
                   chunk)
        _mlstm_heads(ml['q'], ml['k'], ml['bcol'], ml['brow'], p_scr.at[P_VB], p_scr.at[P_OB], sg_scr, sgt_scr,
                     mng_ref, yb_scr, c_scr, m_scr, chunk)
        _merge_body(h0_scr, ya_scr, yb_scr, p_scr.at[P_MA], p_scr.at[P_MB], wa_ref, wb_ref, wo_ref, hp_scr,
                    functools.partial(_embed_norm, xn_ref, eg_ref, eb_ref, h0_scr, hb_scr))

    @pl.when(t == nt)
    def _():
        for piece in router():
            piece()

    @pl.when(t == nt - 1)
    def _():
        sfin_ref[...] = s_scr[...]
        cfin_ref[...] = c_scr[...]
        mfin_ref[...] = m_scr[...]
        xfin_ref[...] = x_scr[...]


def _front(x2d, weights, states, nb, chunk):
    m = x2d.shape[0]
    nt = m // chunk
    s0, c0, m0, x0 = states
    prev_lanes = lambda t: (0, jnp.maximum(t - 1, 0))
    const2 = lambda t: (0, 0)
    const3 = lambda t: (0, 0, 0)

    def resident(a):
        return pl.BlockSpec(a.shape, const2 if a.ndim == 2 else const3, pipeline_mode=pl.Buffered(1))

    outs = pl.pallas_call(
        functools.partial(_front_kernel, chunk=chunk, chunks_per_seq=nt // nb),
        grid=(nt + 1,),
        in_specs=[pl.BlockSpec((chunk, D_MODEL), const2, pipeline_mode=pl.Buffered(1)),
                  pl.BlockSpec((chunk, D_MODEL), lambda t: (jnp.minimum(t + 1, nt - 1), 0))]
        + [resident(a) for a in weights[:8]]
        + [resident(weights[8]), resident(s0)] + [resident(a) for a in weights[9:12]]
        + [resident(c0), resident(m0), resident(x0)] + [resident(a) for a in weights[12:]],
        out_specs=[
            pl.BlockSpec((chunk, D_MODEL), lambda t: (jnp.maximum(t - 1, 0), 0)),
            pl.BlockSpec((TOP_K, chunk), prev_lanes),
            pl.BlockSpec((TOP_K, chunk), prev_lanes),
            pl.BlockSpec((None, N_EXPERTS, 128), lambda t: (jnp.maximum(t - 1, 0), 0, 0)),
            pl.BlockSpec(s0.shape, const3),
            pl.BlockSpec(c0.shape, const3),
            pl.BlockSpec(m0.shape, const2),
            pl.BlockSpec(x0.shape, const2),
        ],
        out_shape=[
            jax.ShapeDtypeStruct((m, D_MODEL), F32),
            jax.ShapeDtypeStruct((TOP_K, m), I32),
            jax.ShapeDtypeStruct((TOP_K, m), F32),
            jax.ShapeDtypeStruct((nt, N_EXPERTS, 128), F32),
            jax.ShapeDtypeStruct(s0.shape, F32),
            jax.ShapeDtypeStruct(c0.shape, F32),
            jax.ShapeDtypeStruct(m0.shape, F32),
            jax.ShapeDtypeStruct(x0.shape, F32),
        ],
        scratch_shapes=[
            pltpu.VMEM((N_SLABS, chunk, D_MODEL), BF16),
            pltpu.VMEM((chunk, D_MODEL), F32),
            pltpu.VMEM((chunk, 128), F32),
            pltpu.VMEM((SUBLANES, chunk), F32),
            pltpu.VMEM((chunk, D_MODEL), BF16),
            pltpu.VMEM((chunk, D_MODEL), BF16),
            pltpu.VMEM((chunk, D_MODEL), F32),
            pltpu.VMEM((chunk, D_MODEL), BF16),
            pltpu.VMEM((chunk, D_MODEL), F32),
            pltpu.VMEM((HG_HEADS, HG_DK, HG_DK), F32),
            pltpu.VMEM((ML_HEADS, ML_DK, ML_AUG), F32),
            pltpu.VMEM((SUBLANES, 128), F32),
            pltpu.VMEM((SUBLANES, D_MODEL), F32),
        ],
        compiler_params=_params("arbitrary"),
        name="front",
    )(x2d, x2d, *weights[:9], s0, *weights[9:12], c0, m0, x0, *weights[12:])
    return outs[0], outs[1], outs[2], outs[3], tuple(outs[4:])


def _tile_slots(tt):
    return -(-(TOP_K * tt + N_EXPERTS * (GROUP_ROWS - 1)) // SLOT_BLOCK) * SLOT_BLOCK


def _wait_groups(n, make_copy, s_tile):
    p = 1 << ((s_tile // GROUP_ROWS).bit_length() - 1)
    while p:
        @pl.when((n & p) != 0)
        def _():
            make_copy(p * GROUP_ROWS).wait()
        p >>= 1


def _block_relative(slots, r):
    return jnp.clip(slots - r * SLOT_BLOCK, -1, SLOT_BLOCK).astype(F32).astype(BF16)


def _for_slot_blocks(n, tt, s_tile, body):
    always = TOP_K * tt // SLOT_BLOCK
    for r in range(always):
        body(r)
    for r in range(always, s_tile // SLOT_BLOCK):
        @pl.when(n * GROUP_ROWS > r * SLOT_BLOCK)
        def _():
            body(r)


def _issue_copies(n2, n1, plan_ref, make_copy, s_tile):
    max2 = s_tile // (2 * GROUP_ROWS)
    base1 = 2 * max2

    def start2(p, c):
        make_copy(plan_ref[0, 0, p], plan_ref[0, 0, max2 + p], 2 * GROUP_ROWS).start()
        return c

    def start1(q, c):
        make_copy(plan_ref[0, 0, base1 + q], plan_ref[0, 0, base1 + N_EXPERTS + q], GROUP_ROWS).start()
        return c

    lax.fori_loop(0, n2, start2, 0)
    lax.fori_loop(0, n1, start1, 0)


def _plan_len(s_tile):
    return 2 * (s_tile // (2 * GROUP_ROWS)) + 2 * N_EXPERTS


def _dispatch_kernel(ng_ref, n2_ref, n1_ref, plan_ref, slot_ref, h_ref, xs_ref, buf, sem, *, tt, s_tile):
    i = pl.program_id(0)
    cur = lax.rem(i, 2)

    def list_to_global(list_row, global_row, rows):
        src = buf.at[cur, pl.ds(pl.multiple_of(list_row, GROUP_ROWS), rows), :]
        dst = xs_ref.at[pl.ds(pl.multiple_of(global_row, GROUP_ROWS), rows), :]
        return pltpu.make_async_copy(src, dst, sem.at[cur])

    def wait_tile(j, b):
        _wait_groups(ng_ref[j], lambda rows: pltpu.make_async_copy(
            buf.at[b, pl.ds(0, rows), :], xs_ref.at[pl.ds(0, rows), :], sem.at[b]), s_tile)

    @pl.when(i >= 2)
    def _():
        wait_tile(i - 2, cur)

    hb = h_ref[...].astype(BF16)
    sl = slot_ref[...]

    row_id = lax.broadcasted_iota(I32, (SLOT_BLOCK, tt), 0).astype(F32).astype(BF16)
    one = jnp.ones((), BF16)

    def fill_block(r):
        rel = _block_relative(sl, r)
        p = jnp.zeros((SLOT_BLOCK, tt), BF16)
        for kk in range(TOP_K):
            p = jnp.where(row_id == rel[kk:kk + 1, :], one, p)
        buf[cur, r * SLOT_BLOCK:(r + 1) * SLOT_BLOCK, :] = _dot(p, hb).astype(BF16)

    _for_slot_blocks(ng_ref[i], tt, s_tile, fill_block)

    _issue_copies(n2_ref[i], n1_ref[i], plan_ref, list_to_global, s_tile)

    @pl.when(i == pl.num_programs(0) - 1)
    def _():
        @pl.when(i >= 1)
        def _():
            wait_tile(i - 1, 1 - cur)
        wait_tile(i, cur)


def _dispatch(counts, plan, slot_k, h1, n_slots, tt):
    m = h1.shape[0]
    s_tile = _tile_slots(tt)
    grid_spec = pltpu.PrefetchScalarGridSpec(
        num_scalar_prefetch=3,
        grid=(m // tt,),
        in_specs=[
            pl.BlockSpec((1, 1, _plan_len(s_tile)), lambda i, *_: (i, 0, 0), memory_space=pltpu.SMEM),
            pl.BlockSpec((TOP_K, tt), lambda i, *_: (0, i)),
            pl.BlockSpec((tt, D_MODEL), lambda i, *_: (i, 0)),
        ],
        out_specs=pl.BlockSpec(memory_space=pl.ANY),
        scratch_shapes=[pltpu.VMEM((2, s_tile, D_MODEL), BF16), pltpu.SemaphoreType.DMA((2,))],
    )
    return pl.pallas_call(
        functools.partial(_dispatch_kernel, tt=tt, s_tile=s_tile),
        grid_spec=grid_spec,
        out_shape=jax.ShapeDtypeStruct((n_slots, D_MODEL), BF16),
        compiler_params=_params("arbitrary"),
        name="dispatch",
    )(*counts, plan, slot_k, h1)


def _experts_kernel(be_ref, nu_ref, x_ref, wg_ref, wu_ref, wd_ref, y_ref, wg_b, wu_b, wd_b):
    i = pl.program_id(0)
    used = i < nu_ref[0]

    @pl.when(used & ((i == 0) | (be_ref[i] != be_ref[jnp.maximum(i - 1, 0)])))
    def _():
        wg_b[...] = wg_ref[...].astype(BF16)
        wu_b[...] = wu_ref[...].astype(BF16)
        wd_b[...] = wd_ref[...].astype(BF16)

    @pl.when(used)
    def _():
        xb = x_ref[...]
        a = _dot(xb, wg_b[...])
        u = _dot(xb, wu_b[...])
        y_ref[...] = _dot((a * _sigmoid(a) * u).astype(BF16), wd_b[...]).astype(BF16)


def _experts(blk_expert, n_used, xs, wg, wu, wd):
    n_slots = xs.shape[0]
    n_blocks = n_slots // MOE_BLOCK
    blk = lambda i, be, nu: (jnp.minimum(i, nu[0] - 1), 0)
    wsel = lambda i, be, nu: (be[jnp.minimum(i, nu[0] - 1)], 0, 0)
    grid_spec = pltpu.PrefetchScalarGridSpec(
        num_scalar_prefetch=2,
        grid=(n_blocks,),
        in_specs=[
            pl.BlockSpec((MOE_BLOCK, D_MODEL), blk),
            pl.BlockSpec((None, D_MODEL, D_EXPERT), wsel),
            pl.BlockSpec((None, D_MODEL, D_EXPERT), wsel),
            pl.BlockSpec((None, D_EXPERT, D_MODEL), wsel),
        ],
        out_specs=pl.BlockSpec((MOE_BLOCK, D_MODEL), blk),
        scratch_shapes=[pltpu.VMEM((D_MODEL, D_EXPERT), BF16), pltpu.VMEM((D_MODEL, D_EXPERT), BF16),
                        pltpu.VMEM((D_EXPERT, D_MODEL), BF16)],
    )
    return pl.pallas_call(
        _experts_kernel,
        grid_spec=grid_spec,
        out_shape=jax.ShapeDtypeStruct((n_slots, D_MODEL), BF16),
        compiler_params=_params("arbitrary"),
        name="experts",
    )(blk_expert, n_used, xs, wg, wu, wd)


def _combine_kernel(ng_ref, n2_ref, n1_ref, gcur_ref, gnext_ref, slot_ref, gw_ref, h_ref, y_ref, sg_ref, su_ref, sd_ref,
                    g2_ref, b2_ref, o_ref, ybuf, acc, sem, *, tt, s_tile):
    i = pl.program_id(0)
    cur = lax.rem(i, 2)

    def fetch(j, b, plan_ref):
        def global_to_list(list_row, global_row, rows):
            src = y_ref.at[pl.ds(pl.multiple_of(global_row, GROUP_ROWS), rows), :]
            dst = ybuf.at[b, pl.ds(pl.multiple_of(list_row, GROUP_ROWS), rows), :]
            return pltpu.make_async_copy(src, dst, sem.at[b])

        _issue_copies(n2_ref[j], n1_ref[j], plan_ref, global_to_list, s_tile)

    @pl.when(i == 0)
    def _():
        ybuf[...] = jnp.zeros_like(ybuf)
        fetch(0, 0, gcur_ref)

    @pl.when(i + 1 < pl.num_programs(0))
    def _():
        fetch(i + 1, 1 - cur, gnext_ref)

    h1 = h_ref[...]
    hb = h1.astype(BF16)
    a = _dot(hb, sg_ref[...])
    u = _dot(hb, su_ref[...])
    shared = _dot((a * _sigmoid(a) * u).astype(BF16), sd_ref[...])

    slot = slot_ref[...]
    gwb = gw_ref[...].astype(BF16)
    _wait_groups(ng_ref[i], lambda rows: pltpu.make_async_copy(
        y_ref.at[pl.ds(0, rows), :], ybuf.at[cur, pl.ds(0, rows), :], sem.at[cur]), s_tile)

    n_always, n_all = TOP_K * tt // SLOT_BLOCK, s_tile // SLOT_BLOCK
    per = 2 if (n_always % 2 == 0 and n_all % 2 == 0) else 1
    rows = per * SLOT_BLOCK
    row_id = lax.broadcasted_iota(I32, (SLOT_BLOCK, tt), 0).astype(F32).astype(BF16)

    def one_hot(r):
        rel = _block_relative(slot, r)
        p = jnp.zeros((SLOT_BLOCK, tt), BF16)
        for kk in range(TOP_K):
            p = jnp.where(row_id == rel[kk:kk + 1, :], gwb[kk:kk + 1, :], p)
        return p

    def block_dot(r):
        p = jnp.concatenate([one_hot(per * r + j) for j in range(per)], axis=0) if per > 1 else one_hot(r)
        return _dot_tn(p, ybuf[cur, r * rows:(r + 1) * rows, :])

    routed = block_dot(0)
    for r in range(1, n_always // per):
        routed = routed + block_dot(r)
    acc[...] = DN_ALPHA * h1 + (routed + shared)
    for r in range(n_always // per, n_all // per):
        @pl.when(ng_ref[i] * GROUP_ROWS > r * rows)
        def _():
            acc[...] += block_dot(r)
    o_ref[...] = _layer_norm(acc[...], g2_ref[...], b2_ref[...])


def _combine(counts, plan, slot_k, gw_k, h1, y, wsg, wsu, wsd, g2, b2, tt):
    m = h1.shape[0]
    nt = m // tt
    s_tile = _tile_slots(tt)
    const = lambda i, *_: (0, 0)
    table = lambda f: pl.BlockSpec((1, 1, _plan_len(s_tile)), f, memory_space=pltpu.SMEM)
    grid_spec = pltpu.PrefetchScalarGridSpec(
        num_scalar_prefetch=3,
        grid=(nt,),
        in_specs=[
            table(lambda i, *_: (i, 0, 0)),
            table(lambda i, *_: (jnp.minimum(i + 1, nt - 1), 0, 0)),
            pl.BlockSpec((TOP_K, tt), lambda i, *_: (0, i)),
            pl.BlockSpec((TOP_K, tt), lambda i, *_: (0, i)),
            pl.BlockSpec((tt, D_MODEL), lambda i, *_: (i, 0)),
            pl.BlockSpec(memory_space=pl.ANY),
            pl.BlockSpec((D_MODEL, D_EXPERT), const),
            pl.BlockSpec((D_MODEL, D_EXPERT), const),
            pl.BlockSpec((D_EXPERT, D_MODEL), const),
            pl.BlockSpec((1, D_MODEL), const),
            pl.BlockSpec((1, D_MODEL), const),
        ],
        out_specs=pl.BlockSpec((tt, D_MODEL), lambda i, *_: (i, 0)),
        scratch_shapes=[pltpu.VMEM((2, s_tile, D_MODEL), BF16), pltpu.VMEM((tt, D_MODEL), F32),
                        pltpu.SemaphoreType.DMA((2,))],
    )
    return pl.pallas_call(
        functools.partial(_combine_kernel, tt=tt, s_tile=s_tile),
        grid_spec=grid_spec,
        out_shape=jax.ShapeDtypeStruct((m, D_MODEL), F32),
        compiler_params=_params("arbitrary"),
        name="combine",
    )(*counts, plan, plan, slot_k, gw_k, h1, y, wsg, wsu, wsd, g2, b2)


def _pick_tile(m, pref):
    t = min(pref, m)
    while m % t:
        t //= 2
    return t


def _forward(x, meta_tokens, ln_emb_g, ln_emb_b, w_in, hg_lb_logits, hg_norm_g, ml_conv_w, ml_conv_b,
             ml_ig_bias, ml_fg_bias, ml_norm_g, w_branch_a, w_branch_b, w_out, ln1_g, ln1_b,
             w_router, router_bias, w_exp_gate, w_exp_up, w_exp_down, w_sh_gate, w_sh_up, w_sh_down,
             ln2_g, ln2_b, *, chunk):
    nb, seq, d = x.shape
    m = nb * seq
    row = lambda a: a.reshape(1, -1).astype(F32)

    w = w_in[0]
    kw = HG_HEADS * HG_DK
    o_qa, o_fa, o_ia, o_ga = 0, kw, 2 * kw, 3 * kw
    o_qb = 4 * kw
    o_kb = o_qb + ML_HEADS * ML_DK
    o_vb = o_kb + ML_HEADS * ML_DK
    o_ob = o_vb + ML_HEADS * ML_DV
    o_ig = o_ob + ML_HEADS * ML_DV
    o_fg = o_ig + ML_HEADS
    o_ma = o_fg + ML_HEADS
    o_mb = o_ma + D_MODEL
    cols = lambda o, n: w[:, o:o + n]
    w_cat = jnp.concatenate([
        cols(o_qa, kw), cols(o_fa, kw), cols(o_ia, kw), cols(o_ga, kw),
        cols(o_qb, 2 * ML_HEADS * ML_DK), cols(o_vb, ML_HEADS * ML_DV), cols(o_ob, ML_HEADS * ML_DV),
        cols(o_ma, D_MODEL), cols(o_mb, D_MODEL)], axis=1).astype(BF16)
    w_small = jnp.pad(cols(o_ig, 2 * ML_HEADS), ((0, 0), (0, 128 - 2 * ML_HEADS))).astype(BF16)
    gate_bias = jnp.pad(jnp.concatenate([ml_ig_bias[0], ml_fg_bias[0]]).astype(F32), (0, 128 - 2 * ML_HEADS)).reshape(1, 128)
    lb = jax.nn.softmax(hg_lb_logits.astype(F32), axis=0)[0].reshape(1, -1)
    eg, eb = row(ln_emb_g), row(ln_emb_b)
    conv_w = ml_conv_w[0].astype(F32)
    conv_b = row(ml_conv_b[0])
    hgn, mln = row(hg_norm_g[0]), row(ml_norm_g[0])
    w_small_t = jnp.pad(cols(o_ig, 2 * ML_HEADS).T, ((0, GROUP_ROWS - 2 * ML_HEADS), (0, 0))).astype(BF16)
    gate_bias_t = jnp.broadcast_to(jnp.pad(gate_bias[0, :2 * ML_HEADS], (0, GROUP_ROWS - 2 * ML_HEADS))[:, None],
                                   (GROUP_ROWS, 128))
    wr = w_router[0].T.astype(F32)
    wr_hi, wr_lo = _split_bf16(wr)
    rbias = jnp.broadcast_to(router_bias[0].astype(F32).reshape(N_EXPERTS, 1), (N_EXPERTS, 128))
    weights = [eg, eb, w_cat, w_small, lb, gate_bias, w_small_t, gate_bias_t, hgn, conv_w, conv_b, mln,
               w_branch_a[0].astype(BF16), w_branch_b[0].astype(BF16), w_out[0].astype(BF16),
               row(ln1_g[0]), row(ln1_b[0]), wr_hi, wr_lo, rbias]

    zero_states = (jnp.zeros((HG_HEADS, HG_DK, HG_DK), F32), jnp.zeros((ML_HEADS, ML_DK, ML_AUG), F32),
                   jnp.zeros((SUBLANES, 128), F32), jnp.zeros((SUBLANES, D_MODEL), F32))
    meta_states = _front(meta_tokens.astype(F32), weights, zero_states, 1, N_META)[4]

    x2d = x.reshape(m, d).astype(F32)
    tt = chunk
    h1, slot_k, gw, cnt, _ = _front(x2d, weights, meta_states, nb, chunk)

    nt = m // tt
    s_tile = _tile_slots(tt)
    cnt8 = (cnt[:, :, 0].astype(I32) + GROUP_ROWS - 1) // GROUP_ROWS * GROUP_ROWS
    seg_end = jnp.cumsum(cnt8, axis=1)
    seg_off = seg_end - cnt8
    tile_rows = seg_end[:, -1]
    run = jnp.cumsum(cnt8, axis=0) - cnt8
    tot8 = jnp.sum(cnt8, axis=0)
    padded = (tot8 + MOE_BLOCK - 1) // MOE_BLOCK * MOE_BLOCK
    pends = jnp.cumsum(padded)
    gshift = (pends - padded)[None, :] + run - seg_off
    experts = jnp.arange(N_EXPERTS, dtype=I32)

    def copy_list(per_expert, length, list_row0, step):
        ends = jnp.cumsum(per_expert, axis=1)
        idx = jnp.arange(length, dtype=I32)
        owner = jnp.sum((ends[:, None, :] <= idx[None, :, None]).astype(I32), axis=-1)
        pick = lambda a: jnp.sum(jnp.where(owner[..., None] == experts, a[:, None, :], 0), axis=-1)
        list_row = pick(list_row0 - step * (ends - per_expert)) + step * idx[None, :]
        valid = idx[None, :] < ends[:, -1:]
        return jnp.where(valid, list_row, 0), jnp.where(valid, list_row + pick(gshift), 0), ends[:, -1]

    groups = cnt8 // GROUP_ROWS
    doubles = groups // 2
    l2, g2, n2 = copy_list(doubles, s_tile // (2 * GROUP_ROWS), seg_off, 2 * GROUP_ROWS)
    l1, g1, n1 = copy_list(groups % 2, N_EXPERTS, seg_off + 2 * GROUP_ROWS * doubles, 0)
    plan = jnp.concatenate([l2, g2, l1, g1], axis=1).astype(I32).reshape(nt, 1, -1)
    counts = ((tile_rows // GROUP_ROWS).astype(I32), n2.astype(I32), n1.astype(I32))
    n_blocks = -(-(m * TOP_K + nt * N_EXPERTS * (GROUP_ROWS - 1)) // MOE_BLOCK) + N_EXPERTS
    blk_start = jnp.arange(n_blocks, dtype=I32) * MOE_BLOCK
    blk_expert = jnp.minimum(jnp.sum((pends[None, :] <= blk_start[:, None]).astype(I32), axis=1), N_EXPERTS - 1)
    n_used = (pends[-1:] // MOE_BLOCK).astype(I32)

    xs = _dispatch(counts, plan, slot_k, h1, n_blocks * MOE_BLOCK, tt)
    y = _experts(blk_expert, n_used, xs, w_exp_gate[0].astype(F32), w_exp_up[0].astype(F32),
                 w_exp_down[0].astype(F32))
    out = _combine(counts, plan, slot_k, gw, h1, y, w_sh_gate[0].astype(BF16), w_sh_up[0].astype(BF16),
                   w_sh_down[0].astype(BF16), row(ln2_g[0]), row(ln2_b[0]), tt)
    return out.reshape(nb, seq, d).astype(x.dtype)


def kernel(x, meta_tokens, ln_emb_g, ln_emb_b, w_in, hg_lb_logits, hg_norm_g, ml_conv_w, ml_conv_b, ml_ig_bias, ml_fg_bias, ml_norm_g, w_branch_a, w_branch_b, w_out, ln1_g, ln1_b, w_router, router_bias, w_exp_gate, w_exp_up, w_exp_down, w_sh_gate, w_sh_up, w_sh_down, ln2_g, ln2_b):
    return _forward(x, meta_tokens, ln_emb_g, ln_emb_b, w_in, hg_lb_logits, hg_norm_g, ml_conv_w, ml_conv_b,
                    ml_ig_bias, ml_fg_bias, ml_norm_g, w_branch_a, w_branch_b, w_out, ln1_g, ln1_b,
                    w_router, router_bias, w_exp_gate, w_exp_up, w_exp_down, w_sh_gate, w_sh_up, w_sh_down,
                    ln2_g, ln2_b, chunk=_pick_tile(x.shape[1], 256))
```

```python
import functools

import jax
import jax.numpy as jnp
from jax import lax
from jax.experimental import pallas as pl
from jax.experimental.pallas import tpu as pltpu

F32, BF16, I32 = jnp.float32, jnp.bfloat16, jnp.int32

D_MODEL = 1024
N_META = 16
HG_HEADS = 8
HG_DK = 128
ML_HEADS = 4
ML_DK = 128
ML_DV = 256
ML_AUG = ML_DV + 128
N_EXPERTS = 64
TOP_K = 8
N_GROUPS = 8
GROUP_SIZE = N_EXPERTS // N_GROUPS
TOPK_GROUPS = 4
D_EXPERT = 256
ROUTED_SCALE = 2.5
MOE_BLOCK = 2048
SLOT_BLOCK = 256
DN_ALPHA = 2.0 ** 0.25
EPS = 1e-5
LOG2E = 1.4426950408889634
EXP2_CLAMP = 115.0
SUBLANES = 8
GROUP_ROWS = 16

P_QA, P_KA, P_IA, P_GA, P_QKB, P_VB, P_OB, P_MA, P_MB = range(9)
N_SLABS = 9

VMEM_LIMIT = 56 * 1024 * 1024


def _params(*sem):
    return pltpu.CompilerParams(dimension_semantics=sem, vmem_limit_bytes=VMEM_LIMIT)


def _sigmoid(x):
    return 1.0 / (1.0 + jnp.exp(-x))


def _log_sigmoid(x):
    return jnp.minimum(x, 0.0) - jnp.log(1.0 + jnp.exp(-jnp.abs(x)))


def _layer_norm(x, g, b):
    xc = x - jnp.mean(x, axis=-1, keepdims=True)
    var = jnp.mean(xc * xc, axis=-1, keepdims=True)
    return xc * lax.rsqrt(var + EPS) * g + b


def _dot(a, b):
    return jnp.dot(a, b, preferred_element_type=F32)


def _dot_nt(a, b):
    return lax.dot_general(a, b, (((1,), (1,)), ((), ())), preferred_element_type=F32)


def _dot_tn(a, b):
    return lax.dot_general(a, b, (((0,), (0,)), ((), ())), preferred_element_type=F32)


def _split_bf16(x):
    hi = x.astype(BF16)
    lo = (x - hi.astype(F32)).astype(BF16)
    return hi, lo


def _neg_abs(x):
    return lax.bitcast_convert_type(lax.bitcast_convert_type(x, I32) | jnp.int32(-2 ** 31), F32)


def _ones_where(cond):
    return jnp.where(cond, 1.0, 0.0).astype(BF16)


def _embed_norm(x_ref, g_ref, b_ref, h0_ref, hb_ref):
    h0 = _layer_norm(x_ref[...], g_ref[...], b_ref[...])
    h0_ref[...] = h0
    hb_ref[...] = h0.astype(BF16)


def _inproj_pieces(hb, w_ref, ws_ref, lb_ref, gb_ref, wst_ref, gbt_ref, p_ref, lf_ref, sg_ref, sgt_ref):
    def slab(n):
        def run():
            acc = _dot(hb, w_ref[:, n * D_MODEL:(n + 1) * D_MODEL])
            if n == P_KA:
                lb = lb_ref[...]
                f = lb + (1.0 - lb) * _sigmoid(acc)
                lf_ref[...] = jnp.log(f)
                acc = 1.0 - f
            elif n == P_GA:
                acc = acc * _sigmoid(acc)
            elif n >= P_OB:
                acc = _sigmoid(acc)
            p_ref[n] = acc.astype(BF16)
        return run

    def gates():
        s = _dot(hb, ws_ref[...]) + gb_ref[...]
        lane = lax.broadcasted_iota(I32, s.shape, 1)
        sg_ref[...] = jnp.where(lane < ML_HEADS, s, _log_sigmoid(s))
        st = _dot_nt(wst_ref[...], hb) + gbt_ref[:, 0:1]
        srow = lax.broadcasted_iota(I32, st.shape, 0)
        sgt_ref[...] = jnp.where(srow < ML_HEADS, st, _log_sigmoid(st))[:SUBLANES, :]

    return [slab(n) for n in range(N_SLABS)], gates


def _block_rows(b, block, pick):
    c, w = b.shape
    parts = [jnp.broadcast_to(b[j * block + pick:j * block + pick + 1, :], (block, w))
             for j in range(c // block)]
    return parts[0] if len(parts) == 1 else jnp.concatenate(parts, axis=0)


def _hgrn_body(q_ref, k_ref, v_ref, g_ref, lf_ref, ng_ref, y_ref, s_scr, cs):
    row = lax.broadcasted_iota(I32, (cs, cs), 0)
    col = lax.broadcasted_iota(I32, (cs, cs), 1)
    tri = _ones_where(col <= row)
    lf_hi, lf_lo = _split_bf16(lf_ref[...])
    b = (_dot(tri, lf_hi) + _dot(tri, lf_lo)) * LOG2E
    q = q_ref[...]
    k = k_ref[...]
    v = v_ref[...]
    blast = b[cs - 1:cs, :]
    qg = q * jnp.exp2(b).astype(BF16)
    kg = k * jnp.exp2(blast - b).astype(BF16)
    dec = jnp.exp2(blast)

    levels = []
    m = SUBLANES
    while 2 * m <= cs:
        w = jnp.exp2(_neg_abs(b - _block_rows(b, 2 * m, m - 1))).astype(BF16)
        sh = (2 * m).bit_length() - 1
        mask = ((row >> sh) == (col >> sh)) & ((row & (2 * m - 1)) >= m) & ((col & (2 * m - 1)) < m)
        levels.append((q * w, k * w, mask))
        m *= 2
    e = jnp.clip(b - _block_rows(b, SUBLANES, SUBLANES // 2 - 1), -EXP2_CLAMP, EXP2_CLAMP)
    levels.append((q * jnp.exp2(e).astype(BF16), k * jnp.exp2(-e).astype(BF16),
                   ((row >> 3) == (col >> 3)) & (col <= row)))

    ng = ng_ref[...]
    for h in range(HG_HEADS):
        sl = slice(h * HG_DK, (h + 1) * HG_DK)
        st = s_scr[h]
        o = _dot_nt(qg[:, sl], st.astype(BF16))
        sc = jnp.zeros((cs, cs), F32)
        for lq, lk, mask in levels:
            sc = jnp.where(mask, _dot_nt(lq[:, sl], lk[:, sl]), sc)
        o = o + _dot(sc.astype(BF16), v[:, sl])
        s_scr[h] = dec[:, sl] * st + _dot_tn(v[:, sl], kg[:, sl])
        ms = jnp.mean(o * o, axis=-1, keepdims=True)
        y = o * lax.rsqrt(ms + EPS) * ng[:, sl] * g_ref[:, sl].astype(F32)
        y_ref[:, sl] = y.astype(BF16)


def _mlstm_conv(qk_ref, cw_ref, cb_ref, x_scr, cs):
    x = qk_ref[...].astype(F32)
    prev = x_scr[...]
    sub = lax.broadcasted_iota(I32, (SUBLANES, D_MODEL), 0)
    cw = cw_ref[...]
    conv = cw[3:4, :] * x + cb_ref[...]
    for j in (1, 2, 3):
        xs = pltpu.roll(x, j, 0)
        head = jnp.where(sub < j, pltpu.roll(prev, j, 0), xs[:SUBLANES, :])
        xs = jnp.concatenate([head, xs[SUBLANES:, :]], axis=0)
        conv = conv + cw[3 - j:4 - j, :] * xs
    x_scr[...] = x[cs - SUBLANES:, :]
    qk = conv * _sigmoid(conv)
    return (qk[:, :ML_HEADS * ML_DK] * (ML_DK ** -0.5)).astype(BF16), qk[:, ML_HEADS * ML_DK:]


def _mlstm_cumsums(sg_ref, sgt_ref, cs):
    row = lax.broadcasted_iota(I32, (cs, cs), 0)
    col = lax.broadcasted_iota(I32, (cs, cs), 1)
    tri = _ones_where(col <= row)
    sg_hi, sg_lo = _split_bf16(sg_ref[...])
    sgt_hi, sgt_lo = _split_bf16(sgt_ref[...])
    return _dot(tri, sg_hi) + _dot(tri, sg_lo), _dot_nt(sgt_hi, tri) + _dot_nt(sgt_lo, tri)


def _mlstm_heads(q_all, k_all, bcol_all, brow_all, v_ref, og_ref, sg_ref, sgt_ref, ng_ref, y_ref, c_scr, m_scr, cs):
    row = lax.broadcasted_iota(I32, (cs, cs), 0)
    col = lax.broadcasted_iota(I32, (cs, cs), 1)
    causal = col <= row
    sg = sg_ref[...]
    sgt = sgt_ref[...]
    lane128 = lax.broadcasted_iota(I32, (cs, 128), 1)
    ones_col = _ones_where(lane128 == 0)
    v = v_ref[...]
    ng = ng_ref[...]

    for h in range(ML_HEADS):
        b_col = bcol_all[:, ML_HEADS + h:ML_HEADS + h + 1]
        b_row = brow_all[ML_HEADS + h:ML_HEADS + h + 1, :]
        ig_col = sg[:, h:h + 1]
        ig_row = sgt[h:h + 1, :]
        m_prev = m_scr[h:h + 1, 0:1]
        q_h = q_all[:, h * ML_DK:(h + 1) * ML_DK]
        k_h = k_all[:, h * ML_DK:(h + 1) * ML_DK]
        v_aug = jnp.concatenate([v[:, h * ML_DV:(h + 1) * ML_DV], ones_col], axis=1)
        c_st = c_scr[h]

        log_intra = jnp.where(causal, b_col - b_row + ig_row, -jnp.inf)
        log_inter = b_col + m_prev
        m_t = jnp.maximum(log_inter, jnp.max(log_intra, axis=-1, keepdims=True))
        w_intra = jnp.exp(log_intra - m_t)
        w_inter = jnp.exp(log_inter - m_t)
        s = _dot_nt(q_h, k_h.astype(BF16)) * w_intra
        tot = w_inter * _dot(q_h, c_st.astype(BF16)) + _dot(s.astype(BF16), v_aug)
        num = tot[:, :ML_DV]
        den = tot[:, ML_DV:ML_DV + 1]
        hid = num / jnp.maximum(jnp.abs(den), jnp.exp(-m_t))

        b_last = b_col[cs - 1:cs, :]
        log_w = b_last - b_col + ig_col
        m_new = jnp.maximum(b_last + m_prev, jnp.max(log_w, axis=0, keepdims=True))
        w_s = jnp.exp(log_w - m_new)
        decay = jnp.exp(b_last + m_prev - m_new)
        c_scr[h] = decay * c_st + _dot_tn((k_h * w_s).astype(BF16), v_aug)
        m_scr[h:h + 1, :] = jnp.broadcast_to(m_new, (1, 128))

        hc = hid - jnp.mean(hid, axis=-1, keepdims=True)
        var = jnp.mean(hc * hc, axis=-1, keepdims=True)
        sl = slice(h * ML_DV, (h + 1) * ML_DV)
        y = hc * lax.rsqrt(var + EPS) * ng[:, sl] * og_ref[:, sl].astype(F32)
        y_ref[:, sl] = y.astype(BF16)


def _merge_body(h0_ref, ya_ref, yb_ref, ma_ref, mb_ref, wa_ref, wb_ref, wo_ref, pre_ref, after_h0):
    resid = DN_ALPHA * h0_ref[...]
    after_h0()
    merged = (ma_ref[...].astype(F32) * _dot(ya_ref[...], wa_ref[...])
              + mb_ref[...].astype(F32) * _dot(yb_ref[...], wb_ref[...]))
    pre_ref[...] = resid + _dot(merged.astype(BF16), wo_ref[...])


def _router_pieces(pre_ref, g1_ref, b1_ref, wrh_ref, wrl_ref, rb_ref, h1_ref, slot_ref, gw_ref, cnt_ref, tm):
    neg_inf = -jnp.inf
    v = {}

    def ln1():
        v['h1'] = _layer_norm(pre_ref[...], g1_ref[...], b1_ref[...])
        h1_ref[...] = v['h1']

    def logits():
        h_hi, h_lo = _split_bf16(v['h1'])
        lg = _dot_nt(wrh_ref[...], h_hi) + _dot_nt(wrh_ref[...], h_lo) + _dot_nt(wrl_ref[...], h_hi)
        v['scores'] = _sigmoid(lg)
        v['biased'] = v['scores'] + rb_ref[:, 0:1]

    def group_scores():
        g3 = v['biased'].reshape(N_GROUPS, GROUP_SIZE, tm)
        sub3 = lax.broadcasted_iota(I32, g3.shape, 1)
        top1 = jnp.max(g3, axis=1, keepdims=True)
        first = jnp.min(jnp.where(g3 == top1, sub3, GROUP_SIZE), axis=1, keepdims=True)
        top2 = jnp.max(jnp.where(sub3 == first, neg_inf, g3), axis=1, keepdims=True)
        v['gs'] = (top1 + top2).reshape(N_GROUPS, tm)

    def group_select():
        gs = v['gs']
        gi = lax.broadcasted_iota(I32, gs.shape, 0)
        grank = jnp.zeros(gs.shape, F32)
        for j in range(N_GROUPS):
            r = gs[j:j + 1, :]
            grank = grank + jnp.where((r > gs) | ((r == gs) & (gi > j)), 1.0, 0.0)
        gsel = grank < float(TOPK_GROUPS)
        emask = jnp.broadcast_to(gsel.reshape(N_GROUPS, 1, tm), (N_GROUPS, GROUP_SIZE, tm)).reshape(N_EXPERTS, tm)
        v['work'] = jnp.where(emask, v['biased'], neg_inf)
        v['rank'] = jnp.full((N_EXPERTS, tm), float(N_EXPERTS), F32)

    def extract(kk):
        def run():
            ei = lax.broadcasted_iota(I32, (N_EXPERTS, tm), 0)
            for k2 in (kk, kk + 1):
                work = v['work']
                top = jnp.max(work, axis=0, keepdims=True)
                first = jnp.min(jnp.where(work == top, ei, N_EXPERTS), axis=0, keepdims=True)
                hit = ei == first
                v['rank'] = jnp.where(hit, float(k2), v['rank'])
                v['work'] = jnp.where(hit, neg_inf, work)
        return run

    def slots():
        sel = v['rank'] < float(TOP_K)
        sel_w = jnp.where(sel, v['scores'], 0.0)
        v['gwd'] = sel_w / jnp.sum(sel_w, axis=0, keepdims=True) * ROUTED_SCALE
        tr = lax.broadcasted_iota(I32, (tm, tm), 0)
        tc = lax.broadcasted_iota(I32, (tm, tm), 1)
        sel_b = _ones_where(sel)
        rloc = _dot(sel_b, _ones_where(tr < tc))
        cnt = _dot(sel_b, jnp.ones((tm, 128), BF16))
        cnt_g = jnp.floor((cnt + (GROUP_ROWS - 1.0)) * (1.0 / GROUP_ROWS)) * GROUP_ROWS
        er = lax.broadcasted_iota(I32, (N_EXPERTS, N_EXPERTS), 0)
        ec = lax.broadcasted_iota(I32, (N_EXPERTS, N_EXPERTS), 1)
        seg_start = _dot(_ones_where(ec < er), cnt_g.astype(BF16))
        v['slot_e'] = seg_start[:, 0:1] + rloc
        v['sel'] = sel
        cnt_ref[...] = cnt

    def picks():
        s_rows, w_rows = [], []
        for kk in range(TOP_K):
            pick = v['sel'] & (v['rank'] == float(kk))
            s_rows.append(jnp.sum(jnp.where(pick, v['slot_e'], 0.0), axis=0, keepdims=True))
            w_rows.append(jnp.sum(jnp.where(pick, v['gwd'], 0.0), axis=0, keepdims=True))
        slot_ref[...] = jnp.concatenate(s_rows, axis=0).astype(I32)
        gw_ref[...] = jnp.concatenate(w_rows, axis=0)

    return [ln1, logits, group_scores, group_select] + [extract(kk) for kk in range(0, TOP_K, 2)] + [slots, picks]


def _front_kernel(xf_ref, xn_ref, eg_ref, eb_ref, w_ref, ws_ref, lb_ref, gb_ref, wst_ref, gbt_ref,
                  hng_ref, s0_ref, cw_ref, cb_ref, mng_ref, c0_ref, m0_ref, x0_ref,
                  wa_ref, wb_ref, wo_ref, g1_ref, b1_ref, wrh_ref, wrl_ref, rb_ref,
                  h1_ref, slot_ref, gw_ref, cnt_ref, sfin_ref, cfin_ref, mfin_ref, xfin_ref,
                  p_scr, lf_scr, sg_scr, sgt_scr, ya_scr, yb_scr, h0_scr, hb_scr, hp_scr, s_scr, c_scr, m_scr, x_scr,
                  *, chunk, chunks_per_seq):
    t = pl.program_id(0)
    nt = pl.num_programs(0) - 1

    @pl.when(t == 0)
    def _():
        _embed_norm(xf_ref, eg_ref, eb_ref, h0_scr, hb_scr)

    @pl.when(lax.rem(t, chunks_per_seq) == 0)
    def _():
        s_scr[...] = s0_ref[...]
        c_scr[...] = c0_ref[...]
        m_scr[...] = m0_ref[...]
        x_scr[...] = x0_ref[...]

    @pl.when(t == 0)
    def _():
        hp_scr[...] = jnp.zeros_like(hp_scr)

    def router():
        return _router_pieces(hp_scr, g1_ref, b1_ref, wrh_ref, wrl_ref, rb_ref, h1_ref, slot_ref, gw_ref, cnt_ref,
                              chunk)

    @pl.when(t < nt)
    def _():
        slabs, gates = _inproj_pieces(hb_scr[...], w_ref, ws_ref, lb_ref, gb_ref, wst_ref, gbt_ref,
                                      p_scr, lf_scr, sg_scr, sgt_scr)
        ml = {}

        def conv():
            ml['q'], ml['k'] = _mlstm_conv(p_scr.at[P_QKB], cw_ref, cb_ref, x_scr, chunk)

        def cumsums():
            ml['bcol'], ml['brow'] = _mlstm_cumsums(sg_scr, sgt_scr, chunk)

        r = router()
        light = [[], [r[0]], [conv, r[1]], [cumsums, r[2]]] + [[p] for p in r[3:8]] + [r[8:]]
        heavy = [slabs[P_QKB], gates] + [slabs[n] for n in range(N_SLABS) if n != P_QKB]
        for piece, fill in zip(heavy, light):
            piece()
            for f in fill:
                f()
        _hgrn_body(p_scr.at[P_QA], p_scr.at[P_KA], p_scr.at[P_IA], p_scr.at[P_GA], lf_scr, hng_ref, ya_scr, s_scr,
                   chunk)
        _mlstm_heads(ml['q'], ml['k'], ml['bcol'], ml['brow'], p_scr.at[P_VB], p_scr.at[P_OB], sg_scr, sgt_scr,
                     mng_ref, yb_scr, c_scr, m_scr, chunk)
        _merge_body(h0_scr, ya_scr, yb_scr, p_scr.at[P_MA], p_scr.at[P_MB], wa_ref, wb_ref, wo_ref, hp_scr,
                    functools.partial(_embed_norm, xn_ref, eg_ref, eb_ref, h0_scr, hb_scr))

    @pl.when(t == nt)
    def _():
        for piece in router():
            piece()

    @pl.when(t == nt - 1)
    def _():
        sfin_ref[...] = s_scr[...]
        cfin_ref[...] = c_scr[...]
        mfin_ref[...] = m_scr[...]
        xfin_ref[...] = x_scr[...]


def _front(x2d, weights, states, nb, chunk):
    m = x2d.shape[0]
    nt = m // chunk
    s0, c0, m0, x0 = states
    prev_lanes = lambda t: (0, jnp.maximum(t - 1, 0))
    const2 = lambda t: (0, 0)
    const3 = lambda t: (0, 0, 0)

    def resident(a):
        return pl.BlockSpec(a.shape, const2 if a.ndim == 2 else const3, pipeline_mode=pl.Buffered(1))

    outs = pl.pallas_call(
        functools.partial(_front_kernel, chunk=chunk, chunks_per_seq=nt // nb),
        grid=(nt + 1,),
        in_specs=[pl.BlockSpec((chunk, D_MODEL), const2, pipeline_mode=pl.Buffered(1)),
                  pl.BlockSpec((chunk, D_MODEL), lambda t: (jnp.minimum(t + 1, nt - 1), 0))]
        + [resident(a) for a in weights[:8]]
        + [resident(weights[8]), resident(s0)] + [resident(a) for a in weights[9:12]]
        + [resident(c0), resident(m0), resident(x0)] + [resident(a) for a in weights[12:]],
        out_specs=[
            pl.BlockSpec((chunk, D_MODEL), lambda t: (jnp.maximum(t - 1, 0), 0)),
            pl.BlockSpec((TOP_K, chunk), prev_lanes),
            pl.BlockSpec((TOP_K, chunk), prev_lanes),
            pl.BlockSpec((None, N_EXPERTS, 128), lambda t: (jnp.maximum(t - 1, 0), 0, 0)),
            pl.BlockSpec(s0.shape, const3),
            pl.BlockSpec(c0.shape, const3),
            pl.BlockSpec(m0.shape, const2),
            pl.BlockSpec(x0.shape, const2),
        ],
        out_shape=[
            jax.ShapeDtypeStruct((m, D_MODEL), F32),
            jax.ShapeDtypeStruct((TOP_K, m), I32),
            jax.ShapeDtypeStruct((TOP_K, m), F32),
            jax.ShapeDtypeStruct((nt, N_EXPERTS, 128), F32),
            jax.ShapeDtypeStruct(s0.shape, F32),
            jax.ShapeDtypeStruct(c0.shape, F32),
            jax.ShapeDtypeStruct(m0.shape, F32),
            jax.ShapeDtypeStruct(x0.shape, F32),
        ],
        scratch_shapes=[
            pltpu.VMEM((N_SLABS, chunk, D_MODEL), BF16),
            pltpu.VMEM((chunk, D_MODEL), F32),
            pltpu.VMEM((chunk, 128), F32),
            pltpu.VMEM((SUBLANES, chunk), F32),
            pltpu.VMEM((chunk, D_MODEL), BF16),
            pltpu.VMEM((chunk, D_MODEL), BF16),
            pltpu.VMEM((chunk, D_MODEL), F32),
            pltpu.VMEM((chunk, D_MODEL), BF16),
            pltpu.VMEM((chunk, D_MODEL), F32),
            pltpu.VMEM((HG_HEADS, HG_DK, HG_DK), F32),
            pltpu.VMEM((ML_HEADS, ML_DK, ML_AUG), F32),
            pltpu.VMEM((SUBLANES, 128), F32),
            pltpu.VMEM((SUBLANES, D_MODEL), F32),
        ],
        compiler_params=_params("arbitrary"),
        name="front",
    )(x2d, x2d, *weights[:9], s0, *weights[9:12], c0, m0, x0, *weights[12:])
    return outs[0], outs[1], outs[2], outs[3], tuple(outs[4:])


def _tile_slots(tt):
    return -(-(TOP_K * tt + N_EXPERTS * (GROUP_ROWS - 1)) // SLOT_BLOCK) * SLOT_BLOCK


def _wait_groups(n, make_copy, s_tile):
    p = 1 << ((s_tile // GROUP_ROWS).bit_length() - 1)
    while p:
        @pl.when((n & p) != 0)
        def _():
            make_copy(p * GROUP_ROWS).wait()
        p >>= 1


def _block_relative(slots, r):
    return jnp.clip(slots - r * SLOT_BLOCK, -1, SLOT_BLOCK).astype(F32).astype(BF16)


def _for_slot_blocks(n, tt, s_tile, body):
    always = TOP_K * tt // SLOT_BLOCK
    for r in range(always):
        body(r)
    for r in range(always, s_tile // SLOT_BLOCK):
        @pl.when(n * GROUP_ROWS > r * SLOT_BLOCK)
        def _():
            body(r)


def _issue_copies(n2, n1, plan_ref, make_copy, s_tile):
    max2 = s_tile // (2 * GROUP_ROWS)
    base1 = 2 * max2

    def start2(p, c):
        make_copy(plan_ref[0, 0, p], plan_ref[0, 0, max2 + p], 2 * GROUP_ROWS).start()
        return c

    def start1(q, c):
        make_copy(plan_ref[0, 0, base1 + q], plan_ref[0, 0, base1 + N_EXPERTS + q], GROUP_ROWS).start()
        return c

    lax.fori_loop(0, n2, start2, 0)
    lax.fori_loop(0, n1, start1, 0)


def _plan_len(s_tile):
    return 2 * (s_tile // (2 * GROUP_ROWS)) + 2 * N_EXPERTS


def _dispatch_kernel(ng_ref, n2_ref, n1_ref, plan_ref, slot_ref, h_ref, xs_ref, buf, sem, *, tt, s_tile):
    i = pl.program_id(0)
    cur = lax.rem(i, 2)

    def list_to_global(list_row, global_row, rows):
        src = buf.at[cur, pl.ds(pl.multiple_of(list_row, GROUP_ROWS), rows), :]
        dst = xs_ref.at[pl.ds(pl.multiple_of(global_row, GROUP_ROWS), rows), :]
        return pltpu.make_async_copy(src, dst, sem.at[cur])

    def wait_tile(j, b):
        _wait_groups(ng_ref[j], lambda rows: pltpu.make_async_copy(
            buf.at[b, pl.ds(0, rows), :], xs_ref.at[pl.ds(0, rows), :], sem.at[b]), s_tile)

    @pl.when(i >= 2)
    def _():
        wait_tile(i - 2, cur)

    hb = h_ref[...].astype(BF16)
    sl = slot_ref[...]

    row_id = lax.broadcasted_iota(I32, (SLOT_BLOCK, tt), 0).astype(F32).astype(BF16)
    one = jnp.ones((), BF16)

    def fill_block(r):
        rel = _block_relative(sl, r)
        p = jnp.zeros((SLOT_BLOCK, tt), BF16)
        for kk in range(TOP_K):
            p = jnp.where(row_id == rel[kk:kk + 1, :], one, p)
        buf[cur, r * SLOT_BLOCK:(r + 1) * SLOT_BLOCK, :] = _dot(p, hb).astype(BF16)

    _for_slot_blocks(ng_ref[i], tt, s_tile, fill_block)

    _issue_copies(n2_ref[i], n1_ref[i], plan_ref, list_to_global, s_tile)

    @pl.when(i == pl.num_programs(0) - 1)
    def _():
        @pl.when(i >= 1)
        def _():
            wait_tile(i - 1, 1 - cur)
        wait_tile(i, cur)


def _dispatch(counts, plan, slot_k, h1, n_slots, tt):
    m = h1.shape[0]
    s_tile = _tile_slots(tt)
    grid_spec = pltpu.PrefetchScalarGridSpec(
        num_scalar_prefetch=3,
        grid=(m // tt,),
        in_specs=[
            pl.BlockSpec((1, 1, _plan_len(s_tile)), lambda i, *_: (i, 0, 0), memory_space=pltpu.SMEM),
            pl.BlockSpec((TOP_K, tt), lambda i, *_: (0, i)),
            pl.BlockSpec((tt, D_MODEL), lambda i, *_: (i, 0)),
        ],
        out_specs=pl.BlockSpec(memory_space=pl.ANY),
        scratch_shapes=[pltpu.VMEM((2, s_tile, D_MODEL), BF16), pltpu.SemaphoreType.DMA((2,))],
    )
    return pl.pallas_call(
        functools.partial(_dispatch_kernel, tt=tt, s_tile=s_tile),
        grid_spec=grid_spec,
        out_shape=jax.ShapeDtypeStruct((n_slots, D_MODEL), BF16),
        compiler_params=_params("arbitrary"),
        name="dispatch",
    )(*counts, plan, slot_k, h1)


def _experts_kernel(be_ref, nu_ref, rows_ref, x_ref, wg_ref, wu_ref, wd_ref, y_ref, wg_b, wu_b, wd_b):
    i = pl.program_id(0)
    used = i < nu_ref[0]
    half = MOE_BLOCK // 2

    @pl.when(used & ((i == 0) | (be_ref[i] != be_ref[jnp.maximum(i - 1, 0)])))
    def _():
        wg_b[...] = wg_ref[...].astype(BF16)
        wu_b[...] = wu_ref[...].astype(BF16)
        wd_b[...] = wd_ref[...].astype(BF16)

    def mlp(rows):
        xb = x_ref[rows, :]
        a = _dot(xb, wg_b[...])
        u = _dot(xb, wu_b[...])
        y_ref[rows, :] = _dot((a * _sigmoid(a) * u).astype(BF16), wd_b[...]).astype(BF16)

    @pl.when(used)
    def _():
        mlp(slice(0, half))

    @pl.when(used & (rows_ref[i] > half))
    def _():
        mlp(slice(half, MOE_BLOCK))


def _experts(blk_expert, n_used, blk_rows, xs, wg, wu, wd):
    n_slots = xs.shape[0]
    n_blocks = n_slots // MOE_BLOCK
    blk = lambda i, be, nu, br: (jnp.minimum(i, nu[0] - 1), 0)
    wsel = lambda i, be, nu, br: (be[jnp.minimum(i, nu[0] - 1)], 0, 0)
    grid_spec = pltpu.PrefetchScalarGridSpec(
        num_scalar_prefetch=3,
        grid=(n_blocks,),
        in_specs=[
            pl.BlockSpec((MOE_BLOCK, D_MODEL), blk),
            pl.BlockSpec((None, D_MODEL, D_EXPERT), wsel),
            pl.BlockSpec((None, D_MODEL, D_EXPERT), wsel),
            pl.BlockSpec((None, D_EXPERT, D_MODEL), wsel),
        ],
        out_specs=pl.BlockSpec((MOE_BLOCK, D_MODEL), blk),
        scratch_shapes=[pltpu.VMEM((D_MODEL, D_EXPERT), BF16), pltpu.VMEM((D_MODEL, D_EXPERT), BF16),
                        pltpu.VMEM((D_EXPERT, D_MODEL), BF16)],
    )
    return pl.pallas_call(
        _experts_kernel,
        grid_spec=grid_spec,
        out_shape=jax.ShapeDtypeStruct((n_slots, D_MODEL), BF16),
        compiler_params=_params("arbitrary"),
        name="experts",
    )(blk_expert, n_used, blk_rows, xs, wg, wu, wd)


def _combine_kernel(ng_ref, n2_ref, n1_ref, gcur_ref, gnext_ref, slot_ref, gw_ref, h_ref, y_ref, sg_ref, su_ref, sd_ref,
                    g2_ref, b2_ref, o_ref, ybuf, acc, sem, *, tt, s_tile):
    i = pl.program_id(0)
    cur = lax.rem(i, 2)

    def fetch(j, b, plan_ref):
        def global_to_list(list_row, global_row, rows):
            src = y_ref.at[pl.ds(pl.multiple_of(global_row, GROUP_ROWS), rows), :]
            dst = ybuf.at[b, pl.ds(pl.multiple_of(list_row, GROUP_ROWS), rows), :]
            return pltpu.make_async_copy(src, dst, sem.at[b])

        _issue_copies(n2_ref[j], n1_ref[j], plan_ref, global_to_list, s_tile)

    @pl.when(i == 0)
    def _():
        ybuf[...] = jnp.zeros_like(ybuf)
        fetch(0, 0, gcur_ref)

    @pl.when(i + 1 < pl.num_programs(0))
    def _():
        fetch(i + 1, 1 - cur, gnext_ref)

    h1 = h_ref[...]
    hb = h1.astype(BF16)
    a = _dot(hb, sg_ref[...])
    u = _dot(hb, su_ref[...])
    shared = _dot((a * _sigmoid(a) * u).astype(BF16), sd_ref[...])

    slot = slot_ref[...]
    gwb = gw_ref[...].astype(BF16)
    _wait_groups(ng_ref[i], lambda rows: pltpu.make_async_copy(
        y_ref.at[pl.ds(0, rows), :], ybuf.at[cur, pl.ds(0, rows), :], sem.at[cur]), s_tile)

    n_always, n_all = TOP_K * tt // SLOT_BLOCK, s_tile // SLOT_BLOCK
    per = 2 if (n_always % 2 == 0 and n_all % 2 == 0) else 1
    rows = per * SLOT_BLOCK
    row_id = lax.broadcasted_iota(I32, (SLOT_BLOCK, tt), 0).astype(F32).astype(BF16)

    def one_hot(r):
        rel = _block_relative(slot, r)
        p = jnp.zeros((SLOT_BLOCK, tt), BF16)
        for kk in range(TOP_K):
            p = jnp.where(row_id == rel[kk:kk + 1, :], gwb[kk:kk + 1, :], p)
        return p

    def block_dot(r):
        p = jnp.concatenate([one_hot(per * r + j) for j in range(per)], axis=0) if per > 1 else one_hot(r)
        return _dot_tn(p, ybuf[cur, r * rows:(r + 1) * rows, :])

    routed = block_dot(0)
    for r in range(1, n_always // per):
        routed = routed + block_dot(r)
    acc[...] = DN_ALPHA * h1 + (routed + shared)
    for r in range(n_always // per, n_all // per):
        @pl.when(ng_ref[i] * GROUP_ROWS > r * rows)
        def _():
            acc[...] += block_dot(r)
    o_ref[...] = _layer_norm(acc[...], g2_ref[...], b2_ref[...])


def _combine(counts, plan, slot_k, gw_k, h1, y, wsg, wsu, wsd, g2, b2, tt):
    m = h1.shape[0]
    nt = m // tt
    s_tile = _tile_slots(tt)
    const = lambda i, *_: (0, 0)
    table = lambda f: pl.BlockSpec((1, 1, _plan_len(s_tile)), f, memory_space=pltpu.SMEM)
    grid_spec = pltpu.PrefetchScalarGridSpec(
        num_scalar_prefetch=3,
        grid=(nt,),
        in_specs=[
            table(lambda i, *_: (i, 0, 0)),
            table(lambda i, *_: (jnp.minimum(i + 1, nt - 1), 0, 0)),
            pl.BlockSpec((TOP_K, tt), lambda i, *_: (0, i)),
            pl.BlockSpec((TOP_K, tt), lambda i, *_: (0, i)),
            pl.BlockSpec((tt, D_MODEL), lambda i, *_: (i, 0)),
            pl.BlockSpec(memory_space=pl.ANY),
            pl.BlockSpec((D_MODEL, D_EXPERT), const),
            pl.BlockSpec((D_MODEL, D_EXPERT), const),
            pl.BlockSpec((D_EXPERT, D_MODEL), const),
            pl.BlockSpec((1, D_MODEL), const),
            pl.BlockSpec((1, D_MODEL), const),
        ],
        out_specs=pl.BlockSpec((tt, D_MODEL), lambda i, *_: (i, 0)),
        scratch_shapes=[pltpu.VMEM((2, s_tile, D_MODEL), BF16), pltpu.VMEM((tt, D_MODEL), F32),
                        pltpu.SemaphoreType.DMA((2,))],
    )
    return pl.pallas_call(
        functools.partial(_combine_kernel, tt=tt, s_tile=s_tile),
        grid_spec=grid_spec,
        out_shape=jax.ShapeDtypeStruct((m, D_MODEL), F32),
        compiler_params=_params("arbitrary"),
        name="combine",
    )(*counts, plan, plan, slot_k, gw_k, h1, y, wsg, wsu, wsd, g2, b2)


def _pick_tile(m, pref):
    t = min(pref, m)
    while m % t:
        t //= 2
    return t


def _forward(x, meta_tokens, ln_emb_g, ln_emb_b, w_in, hg_lb_logits, hg_norm_g, ml_conv_w, ml_conv_b,
             ml_ig_bias, ml_fg_bias, ml_norm_g, w_branch_a, w_branch_b, w_out, ln1_g, ln1_b,
             w_router, router_bias, w_exp_gate, w_exp_up, w_exp_down, w_sh_gate, w_sh_up, w_sh_down,
             ln2_g, ln2_b, *, chunk):
    nb, seq, d = x.shape
    m = nb * seq
    row = lambda a: a.reshape(1, -1).astype(F32)

    w = w_in[0]
    kw = HG_HEADS * HG_DK
    o_qa, o_fa, o_ia, o_ga = 0, kw, 2 * kw, 3 * kw
    o_qb = 4 * kw
    o_kb = o_qb + ML_HEADS * ML_DK
    o_vb = o_kb + ML_HEADS * ML_DK
    o_ob = o_vb + ML_HEADS * ML_DV
    o_ig = o_ob + ML_HEADS * ML_DV
    o_fg = o_ig + ML_HEADS
    o_ma = o_fg + ML_HEADS
    o_mb = o_ma + D_MODEL
    cols = lambda o, n: w[:, o:o + n]
    w_cat = jnp.concatenate([
        cols(o_qa, kw), cols(o_fa, kw), cols(o_ia, kw), cols(o_ga, kw),
        cols(o_qb, 2 * ML_HEADS * ML_DK), cols(o_vb, ML_HEADS * ML_DV), cols(o_ob, ML_HEADS * ML_DV),
        cols(o_ma, D_MODEL), cols(o_mb, D_MODEL)], axis=1).astype(BF16)
    w_small = jnp.pad(cols(o_ig, 2 * ML_HEADS), ((0, 0), (0, 128 - 2 * ML_HEADS))).astype(BF16)
    gate_bias = jnp.pad(jnp.concatenate([ml_ig_bias[0], ml_fg_bias[0]]).astype(F32), (0, 128 - 2 * ML_HEADS)).reshape(1, 128)
    lb = jax.nn.softmax(hg_lb_logits.astype(F32), axis=0)[0].reshape(1, -1)
    eg, eb = row(ln_emb_g), row(ln_emb_b)
    conv_w = ml_conv_w[0].astype(F32)
    conv_b = row(ml_conv_b[0])
    hgn, mln = row(hg_norm_g[0]), row(ml_norm_g[0])
    w_small_t = jnp.pad(cols(o_ig, 2 * ML_HEADS).T, ((0, GROUP_ROWS - 2 * ML_HEADS), (0, 0))).astype(BF16)
    gate_bias_t = jnp.broadcast_to(jnp.pad(gate_bias[0, :2 * ML_HEADS], (0, GROUP_ROWS - 2 * ML_HEADS))[:, None],
                                   (GROUP_ROWS, 128))
    wr = w_router[0].T.astype(F32)
    wr_hi, wr_lo = _split_bf16(wr)
    rbias = jnp.broadcast_to(router_bias[0].astype(F32).reshape(N_EXPERTS, 1), (N_EXPERTS, 128))
    weights = [eg, eb, w_cat, w_small, lb, gate_bias, w_small_t, gate_bias_t, hgn, conv_w, conv_b, mln,
               w_branch_a[0].astype(BF16), w_branch_b[0].astype(BF16), w_out[0].astype(BF16),
               row(ln1_g[0]), row(ln1_b[0]), wr_hi, wr_lo, rbias]

    zero_states = (jnp.zeros((HG_HEADS, HG_DK, HG_DK), F32), jnp.zeros((ML_HEADS, ML_DK, ML_AUG), F32),
                   jnp.zeros((SUBLANES, 128), F32), jnp.zeros((SUBLANES, D_MODEL), F32))
    meta_states = _front(meta_tokens.astype(F32), weights, zero_states, 1, N_META)[4]

    x2d = x.reshape(m, d).astype(F32)
    tt = chunk
    h1, slot_k, gw, cnt, _ = _front(x2d, weights, meta_states, nb, chunk)

    nt = m // tt
    s_tile = _tile_slots(tt)
    cnt8 = (cnt[:, :, 0].astype(I32) + GROUP_ROWS - 1) // GROUP_ROWS * GROUP_ROWS
    seg_end = jnp.cumsum(cnt8, axis=1)
    seg_off = seg_end - cnt8
    tile_rows = seg_end[:, -1]
    run = jnp.cumsum(cnt8, axis=0) - cnt8
    tot8 = jnp.sum(cnt8, axis=0)
    padded = (tot8 + MOE_BLOCK - 1) // MOE_BLOCK * MOE_BLOCK
    pends = jnp.cumsum(padded)
    gshift = (pends - padded)[None, :] + run - seg_off
    experts = jnp.arange(N_EXPERTS, dtype=I32)

    def copy_list(per_expert, length, list_row0, step):
        ends = jnp.cumsum(per_expert, axis=1)
        idx = jnp.arange(length, dtype=I32)
        owner = jnp.sum((ends[:, None, :] <= idx[None, :, None]).astype(I32), axis=-1)
        pick = lambda a: jnp.sum(jnp.where(owner[..., None] == experts, a[:, None, :], 0), axis=-1)
        list_row = pick(list_row0 - step * (ends - per_expert)) + step * idx[None, :]
        valid = idx[None, :] < ends[:, -1:]
        return jnp.where(valid, list_row, 0), jnp.where(valid, list_row + pick(gshift), 0), ends[:, -1]

    groups = cnt8 // GROUP_ROWS
    doubles = groups // 2
    l2, g2, n2 = copy_list(doubles, s_tile // (2 * GROUP_ROWS), seg_off, 2 * GROUP_ROWS)
    l1, g1, n1 = copy_list(groups % 2, N_EXPERTS, seg_off + 2 * GROUP_ROWS * doubles, 0)
    plan = jnp.concatenate([l2, g2, l1, g1], axis=1).astype(I32).reshape(nt, 1, -1)
    counts = ((tile_rows // GROUP_ROWS).astype(I32), n2.astype(I32), n1.astype(I32))
    n_blocks = -(-(m * TOP_K + nt * N_EXPERTS * (GROUP_ROWS - 1)) // MOE_BLOCK) + N_EXPERTS
    blk_start = jnp.arange(n_blocks, dtype=I32) * MOE_BLOCK
    blk_expert = jnp.minimum(jnp.sum((pends[None, :] <= blk_start[:, None]).astype(I32), axis=1), N_EXPERTS - 1)
    n_used = (pends[-1:] // MOE_BLOCK).astype(I32)
    run_end = pends - padded + tot8
    blk_rows = jnp.clip(jnp.sum(jnp.where(blk_expert[:, None] == experts, run_end[None, :], 0), axis=1) - blk_start,
                        0, MOE_BLOCK).astype(I32)

    xs = _dispatch(counts, plan, slot_k, h1, n_blocks * MOE_BLOCK, tt)
    y = _experts(blk_expert, n_used, blk_rows, xs, w_exp_gate[0].astype(F32), w_exp_up[0].astype(F32),
                 w_exp_down[0].astype(F32))
    out = _combine(counts, plan, slot_k, gw, h1, y, w_sh_gate[0].astype(BF16), w_sh_up[0].astype(BF16),
                   w_sh_down[0].astype(BF16), row(ln2_g[0]), row(ln2_b[0]), tt)
    return out.reshape(nb, seq, d).astype(x.dtype)


def kernel(x, meta_tokens, ln_emb_g, ln_emb_b, w_in, hg_lb_logits, hg_norm_g, ml_conv_w, ml_conv_b, ml_ig_bias, ml_fg_bias, ml_norm_g, w_branch_a, w_branch_b, w_out, ln1_g, ln1_b, w_router, router_bias, w_exp_gate, w_exp_up, w_exp_down, w_sh_gate, w_sh_up, w_sh_down, ln2_g, ln2_b):
    return _forward(x, meta_tokens, ln_emb_g, ln_emb_b, w_in, hg_lb_logits, hg_norm_g, ml_conv_w, ml_conv_b,
                    ml_ig_bias, ml_fg_bias, ml_norm_g, w_branch_a, w_branch_b, w_out, ln1_g, ln1_b,
                    w_router, router_bias, w_exp_gate, w_exp_up, w_exp_down, w_sh_gate, w_sh_up, w_sh_down,
                    ln2_g, ln2_b, chunk=_pick_tile(x.shape[1], 256))
```

```python
import functools

import jax
import jax.numpy as jnp
from jax import lax
from jax.experimental import pallas as pl
from jax.experimental.pallas import tpu as pltpu

F32, BF16, I32 = jnp.float32, jnp.bfloat16, jnp.int32

D_MODEL = 1024
N_META = 16
HG_HEADS = 8
HG_DK = 128
ML_HEADS = 4
ML_DK = 128
ML_DV = 256
ML_AUG = ML_DV + 128
N_EXPERTS = 64
TOP_K = 8
N_GROUPS = 8
GROUP_SIZE = N_EXPERTS // N_GROUPS
TOPK_GROUPS = 4
D_EXPERT = 256
ROUTED_SCALE = 2.5
MOE_BLOCK = 2048
SLOT_BLOCK = 256
DN_ALPHA = 2.0 ** 0.25
EPS = 1e-5
LOG2E = 1.4426950408889634
EXP2_CLAMP = 115.0
SUBLANES = 8
GROUP_ROWS = 16

P_QA, P_KA, P_IA, P_GA, P_QKB, P_VB, P_OB, P_MA, P_MB = range(9)
N_SLABS = 9

VMEM_LIMIT = 56 * 1024 * 1024


def _params(*sem):
    return pltpu.CompilerParams(dimension_semantics=sem, vmem_limit_bytes=VMEM_LIMIT)


def _sigmoid(x):
    return 1.0 / (1.0 + jnp.exp(-x))


def _log_sigmoid(x):
    return jnp.minimum(x, 0.0) - jnp.log(1.0 + jnp.exp(-jnp.abs(x)))


def _layer_norm(x, g, b):
    xc = x - jnp.mean(x, axis=-1, keepdims=True)
    var = jnp.mean(xc * xc, axis=-1, keepdims=True)
    return xc * lax.rsqrt(var + EPS) * g + b


def _dot(a, b):
    return jnp.dot(a, b, preferred_element_type=F32)


def _dot_nt(a, b):
    return lax.dot_general(a, b, (((1,), (1,)), ((), ())), preferred_element_type=F32)


def _dot_tn(a, b):
    return lax.dot_general(a, b, (((0,), (0,)), ((), ())), preferred_element_type=F32)


def _split_bf16(x):
    hi = x.astype(BF16)
    lo = (x - hi.astype(F32)).astype(BF16)
    return hi, lo


def _neg_abs(x):
    return lax.bitcast_convert_type(lax.bitcast_convert_type(x, I32) | jnp.int32(-2 ** 31), F32)


def _ones_where(cond):
    return jnp.where(cond, 1.0, 0.0).astype(BF16)


def _embed_norm(x_ref, g_ref, b_ref, h0_ref, hb_ref):
    h0 = _layer_norm(x_ref[...], g_ref[...], b_ref[...])
    h0_ref[...] = h0
    hb_ref[...] = h0.astype(BF16)


def _inproj_pieces(hb, w_ref, ws_ref, lb_ref, gb_ref, wst_ref, gbt_ref, p_ref, lf_ref, sg_ref, sgt_ref):
    def slab(n):
        def run():
            acc = _dot(hb, w_ref[:, n * D_MODEL:(n + 1) * D_MODEL])
            if n == P_KA:
                lb = lb_ref[...]
                f = lb + (1.0 - lb) * _sigmoid(acc)
                lf_ref[...] = jnp.log(f)
                acc = 1.0 - f
            elif n == P_GA:
                acc = acc * _sigmoid(acc)
            elif n >= P_OB:
                acc = _sigmoid(acc)
            p_ref[n] = acc.astype(BF16)
        return run

    def gates():
        s = _dot(hb, ws_ref[...]) + gb_ref[...]
        lane = lax.broadcasted_iota(I32, s.shape, 1)
        sg_ref[...] = jnp.where(lane < ML_HEADS, s, _log_sigmoid(s))
        st = _dot_nt(wst_ref[...], hb) + gbt_ref[:, 0:1]
        srow = lax.broadcasted_iota(I32, st.shape, 0)
        sgt_ref[...] = jnp.where(srow < ML_HEADS, st, _log_sigmoid(st))[:SUBLANES, :]

    return [slab(n) for n in range(N_SLABS)], gates


def _block_rows(b, block, pick):
    c, w = b.shape
    parts = [jnp.broadcast_to(b[j * block + pick:j * block + pick + 1, :], (block, w))
             for j in range(c // block)]
    return parts[0] if len(parts) == 1 else jnp.concatenate(parts, axis=0)


def _hgrn_body(q_ref, k_ref, v_ref, g_ref, lf_ref, ng_ref, y_ref, s_scr, cs):
    row = lax.broadcasted_iota(I32, (cs, cs), 0)
    col = lax.broadcasted_iota(I32, (cs, cs), 1)
    tri = _ones_where(col <= row)
    lf_hi, lf_lo = _split_bf16(lf_ref[...])
    b = (_dot(tri, lf_hi) + _dot(tri, lf_lo)) * LOG2E
    q = q_ref[...]
    k = k_ref[...]
    v = v_ref[...]
    blast = b[cs - 1:cs, :]
    qg = q * jnp.exp2(b).astype(BF16)
    kg = k * jnp.exp2(blast - b).astype(BF16)
    dec = jnp.exp2(blast)

    levels = []
    m = SUBLANES
    while 2 * m <= cs:
        w = jnp.exp2(_neg_abs(b - _block_rows(b, 2 * m, m - 1))).astype(BF16)
        sh = (2 * m).bit_length() - 1
        mask = ((row >> sh) == (col >> sh)) & ((row & (2 * m - 1)) >= m) & ((col & (2 * m - 1)) < m)
        levels.append((q * w, k * w, mask))
        m *= 2
    e = jnp.clip(b - _block_rows(b, SUBLANES, SUBLANES // 2 - 1), -EXP2_CLAMP, EXP2_CLAMP)
    levels.append((q * jnp.exp2(e).astype(BF16), k * jnp.exp2(-e).astype(BF16),
                   ((row >> 3) == (col >> 3)) & (col <= row)))

    ng = ng_ref[...]
    for h in range(HG_HEADS):
        sl = slice(h * HG_DK, (h + 1) * HG_DK)
        st = s_scr[h]
        o = _dot_nt(qg[:, sl], st.astype(BF16))
        sc = jnp.zeros((cs, cs), F32)
        for lq, lk, mask in levels:
            sc = jnp.where(mask, _dot_nt(lq[:, sl], lk[:, sl]), sc)
        o = o + _dot(sc.astype(BF16), v[:, sl])
        s_scr[h] = dec[:, sl] * st + _dot_tn(v[:, sl], kg[:, sl])
        ms = jnp.mean(o * o, axis=-1, keepdims=True)
        y = o * lax.rsqrt(ms + EPS) * ng[:, sl] * g_ref[:, sl].astype(F32)
        y_ref[:, sl] = y.astype(BF16)


def _mlstm_conv(qk_ref, cw_ref, cb_ref, x_scr, cs):
    x = qk_ref[...].astype(F32)
    prev = x_scr[...]
    sub = lax.broadcasted_iota(I32, (SUBLANES, D_MODEL), 0)
    cw = cw_ref[...]
    conv = cw[3:4, :] * x + cb_ref[...]
    for j in (1, 2, 3):
        xs = pltpu.roll(x, j, 0)
        head = jnp.where(sub < j, pltpu.roll(prev, j, 0), xs[:SUBLANES, :])
        xs = jnp.concatenate([head, xs[SUBLANES:, :]], axis=0)
        conv = conv + cw[3 - j:4 - j, :] * xs
    x_scr[...] = x[cs - SUBLANES:, :]
    qk = conv * _sigmoid(conv)
    return (qk[:, :ML_HEADS * ML_DK] * (ML_DK ** -0.5)).astype(BF16), qk[:, ML_HEADS * ML_DK:]


def _mlstm_cumsums(sg_ref, sgt_ref, cs):
    row = lax.broadcasted_iota(I32, (cs, cs), 0)
    col = lax.broadcasted_iota(I32, (cs, cs), 1)
    tri = _ones_where(col <= row)
    sg_hi, sg_lo = _split_bf16(sg_ref[...])
    sgt_hi, sgt_lo = _split_bf16(sgt_ref[...])
    return _dot(tri, sg_hi) + _dot(tri, sg_lo), _dot_nt(sgt_hi, tri) + _dot_nt(sgt_lo, tri)


def _mlstm_heads(q_all, k_all, bcol_all, brow_all, v_ref, og_ref, sg_ref, sgt_ref, ng_ref, y_ref, c_scr, m_scr, cs):
    row = lax.broadcasted_iota(I32, (cs, cs), 0)
    col = lax.broadcasted_iota(I32, (cs, cs), 1)
    causal = col <= row
    sg = sg_ref[...]
    sgt = sgt_ref[...]
    lane128 = lax.broadcasted_iota(I32, (cs, 128), 1)
    ones_col = _ones_where(lane128 == 0)
    v = v_ref[...]
    ng = ng_ref[...]

    for h in range(ML_HEADS):
        b_col = bcol_all[:, ML_HEADS + h:ML_HEADS + h + 1]
        b_row = brow_all[ML_HEADS + h:ML_HEADS + h + 1, :]
        ig_col = sg[:, h:h + 1]
        ig_row = sgt[h:h + 1, :]
        m_prev = m_scr[h:h + 1, 0:1]
        q_h = q_all[:, h * ML_DK:(h + 1) * ML_DK]
        k_h = k_all[:, h * ML_DK:(h + 1) * ML_DK]
        v_aug = jnp.concatenate([v[:, h * ML_DV:(h + 1) * ML_DV], ones_col], axis=1)
        c_st = c_scr[h]

        log_intra = jnp.where(causal, b_col - b_row + ig_row, -jnp.inf)
        log_inter = b_col + m_prev
        m_t = jnp.maximum(log_inter, jnp.max(log_intra, axis=-1, keepdims=True))
        w_intra = jnp.exp(log_intra - m_t)
        w_inter = jnp.exp(log_inter - m_t)
        s = _dot_nt(q_h, k_h.astype(BF16)) * w_intra
        tot = w_inter * _dot(q_h, c_st.astype(BF16)) + _dot(s.astype(BF16), v_aug)
        num = tot[:, :ML_DV]
        den = tot[:, ML_DV:ML_DV + 1]
        hid = num / jnp.maximum(jnp.abs(den), jnp.exp(-m_t))

        b_last = b_col[cs - 1:cs, :]
        log_w = b_last - b_col + ig_col
        m_new = jnp.maximum(b_last + m_prev, jnp.max(log_w, axis=0, keepdims=True))
        w_s = jnp.exp(log_w - m_new)
        decay = jnp.exp(b_last + m_prev - m_new)
        c_scr[h] = decay * c_st + _dot_tn((k_h * w_s).astype(BF16), v_aug)
        m_scr[h:h + 1, :] = jnp.broadcast_to(m_new, (1, 128))

        hc = hid - jnp.mean(hid, axis=-1, keepdims=True)
        var = jnp.mean(hc * hc, axis=-1, keepdims=True)
        sl = slice(h * ML_DV, (h + 1) * ML_DV)
        y = hc * lax.rsqrt(var + EPS) * ng[:, sl] * og_ref[:, sl].astype(F32)
        y_ref[:, sl] = y.astype(BF16)


def _merge_body(h0_ref, ya_ref, yb_ref, ma_ref, mb_ref, wa_ref, wb_ref, wo_ref, pre_ref, after_h0):
    resid = DN_ALPHA * h0_ref[...]
    after_h0()
    merged = (ma_ref[...].astype(F32) * _dot(ya_ref[...], wa_ref[...])
              + mb_ref[...].astype(F32) * _dot(yb_ref[...], wb_ref[...]))
    pre_ref[...] = resid + _dot(merged.astype(BF16), wo_ref[...])


def _router_pieces(pre_ref, g1_ref, b1_ref, wrh_ref, wrl_ref, rb_ref, h1_ref, slot_ref, gw_ref, cnt_ref, tm):
    neg_inf = -jnp.inf
    v = {}

    def ln1():
        v['h1'] = _layer_norm(pre_ref[...], g1_ref[...], b1_ref[...])
        h1_ref[...] = v['h1']

    def logits():
        h_hi, h_lo = _split_bf16(v['h1'])
        lg = _dot_nt(wrh_ref[...], h_hi) + _dot_nt(wrh_ref[...], h_lo) + _dot_nt(wrl_ref[...], h_hi)
        v['scores'] = _sigmoid(lg)
        v['biased'] = v['scores'] + rb_ref[:, 0:1]

    def group_scores():
        g3 = v['biased'].reshape(N_GROUPS, GROUP_SIZE, tm)
        sub3 = lax.broadcasted_iota(I32, g3.shape, 1)
        top1 = jnp.max(g3, axis=1, keepdims=True)
        first = jnp.min(jnp.where(g3 == top1, sub3, GROUP_SIZE), axis=1, keepdims=True)
        top2 = jnp.max(jnp.where(sub3 == first, neg_inf, g3), axis=1, keepdims=True)
        v['gs'] = (top1 + top2).reshape(N_GROUPS, tm)

    def group_select():
        gs = v['gs']
        gi = lax.broadcasted_iota(I32, gs.shape, 0)
        grank = jnp.zeros(gs.shape, F32)
        for j in range(N_GROUPS):
            r = gs[j:j + 1, :]
            grank = grank + jnp.where((r > gs) | ((r == gs) & (gi > j)), 1.0, 0.0)
        gsel = grank < float(TOPK_GROUPS)
        emask = jnp.broadcast_to(gsel.reshape(N_GROUPS, 1, tm), (N_GROUPS, GROUP_SIZE, tm)).reshape(N_EXPERTS, tm)
        v['work'] = jnp.where(emask, v['biased'], neg_inf)
        v['rank'] = jnp.full((N_EXPERTS, tm), float(N_EXPERTS), F32)

    def extract(kk):
        def run():
            ei = lax.broadcasted_iota(I32, (N_EXPERTS, tm), 0)
            for k2 in (kk, kk + 1):
                work = v['work']
                top = jnp.max(work, axis=0, keepdims=True)
                first = jnp.min(jnp.where(work == top, ei, N_EXPERTS), axis=0, keepdims=True)
                hit = ei == first
                v['rank'] = jnp.where(hit, float(k2), v['rank'])
                v['work'] = jnp.where(hit, neg_inf, work)
        return run

    def slots():
        sel = v['rank'] < float(TOP_K)
        sel_w = jnp.where(sel, v['scores'], 0.0)
        v['gwd'] = sel_w / jnp.sum(sel_w, axis=0, keepdims=True) * ROUTED_SCALE
        tr = lax.broadcasted_iota(I32, (tm, tm), 0)
        tc = lax.broadcasted_iota(I32, (tm, tm), 1)
        sel_b = _ones_where(sel)
        rloc = _dot(sel_b, _ones_where(tr < tc))
        cnt = _dot(sel_b, jnp.ones((tm, 128), BF16))
        cnt_g = jnp.floor((cnt + (GROUP_ROWS - 1.0)) * (1.0 / GROUP_ROWS)) * GROUP_ROWS
        er = lax.broadcasted_iota(I32, (N_EXPERTS, N_EXPERTS), 0)
        ec = lax.broadcasted_iota(I32, (N_EXPERTS, N_EXPERTS), 1)
        seg_start = _dot(_ones_where(ec < er), cnt_g.astype(BF16))
        v['slot_e'] = seg_start[:, 0:1] + rloc
        v['sel'] = sel
        cnt_ref[...] = cnt

    def picks():
        s_rows, w_rows = [], []
        for kk in range(TOP_K):
            pick = v['sel'] & (v['rank'] == float(kk))
            s_rows.append(jnp.sum(jnp.where(pick, v['slot_e'], 0.0), axis=0, keepdims=True))
            w_rows.append(jnp.sum(jnp.where(pick, v['gwd'], 0.0), axis=0, keepdims=True))
        slot_ref[...] = jnp.concatenate(s_rows, axis=0).astype(I32)
        gw_ref[...] = jnp.concatenate(w_rows, axis=0)

    return [ln1, logits, group_scores, group_select] + [extract(kk) for kk in range(0, TOP_K, 2)] + [slots, picks]


def _front_kernel(xf_ref, xn_ref, eg_ref, eb_ref, w_ref, ws_ref, lb_ref, gb_ref, wst_ref, gbt_ref,
                  hng_ref, s0_ref, cw_ref, cb_ref, mng_ref, c0_ref, m0_ref, x0_ref,
                  wa_ref, wb_ref, wo_ref, g1_ref, b1_ref, wrh_ref, wrl_ref, rb_ref,
                  h1_ref, slot_ref, gw_ref, cnt_ref, sfin_ref, cfin_ref, mfin_ref, xfin_ref,
                  p_scr, lf_scr, sg_scr, sgt_scr, ya_scr, yb_scr, h0_scr, hb_scr, hp_scr, s_scr, c_scr, m_scr, x_scr,
                  *, chunk, chunks_per_seq):
    t = pl.program_id(0)
    nt = pl.num_programs(0) - 1

    @pl.when(t == 0)
    def _():
        _embed_norm(xf_ref, eg_ref, eb_ref, h0_scr, hb_scr)

    @pl.when(lax.rem(t, chunks_per_seq) == 0)
    def _():
        s_scr[...] = s0_ref[...]
        c_scr[...] = c0_ref[...]
        m_scr[...] = m0_ref[...]
        x_scr[...] = x0_ref[...]

    @pl.when(t == 0)
    def _():
        hp_scr[...] = jnp.zeros_like(hp_scr)

    def router():
        return _router_pieces(hp_scr, g1_ref, b1_ref, wrh_ref, wrl_ref, rb_ref, h1_ref, slot_ref, gw_ref, cnt_ref,
                              chunk)

    @pl.when(t < nt)
    def _():
        slabs, gates = _inproj_pieces(hb_scr[...], w_ref, ws_ref, lb_ref, gb_ref, wst_ref, gbt_ref,
                                      p_scr, lf_scr, sg_scr, sgt_scr)
        ml = {}

        def conv():
            ml['q'], ml['k'] = _mlstm_conv(p_scr.at[P_QKB], cw_ref, cb_ref, x_scr, chunk)

        def cumsums():
            ml['bcol'], ml['brow'] = _mlstm_cumsums(sg_scr, sgt_scr, chunk)

        r = router()
        light = [[], [r[0]], [conv, r[1]], [cumsums, r[2]]] + [[p] for p in r[3:8]] + [r[8:]]
        heavy = [slabs[P_QKB], gates] + [slabs[n] for n in range(N_SLABS) if n != P_QKB]
        for piece, fill in zip(heavy, light):
            piece()
            for f in fill:
                f()
        _hgrn_body(p_scr.at[P_QA], p_scr.at[P_KA], p_scr.at[P_IA], p_scr.at[P_GA], lf_scr, hng_ref, ya_scr, s_scr,
                   chunk)
        _mlstm_heads(ml['q'], ml['k'], ml['bcol'], ml['brow'], p_scr.at[P_VB], p_scr.at[P_OB], sg_scr, sgt_scr,
                     mng_ref, yb_scr, c_scr, m_scr, chunk)
        _merge_body(h0_scr, ya_scr, yb_scr, p_scr.at[P_MA], p_scr.at[P_MB], wa_ref, wb_ref, wo_ref, hp_scr,
                    functools.partial(_embed_norm, xn_ref, eg_ref, eb_ref, h0_scr, hb_scr))

    @pl.when(t == nt)
    def _():
        for piece in router():
            piece()

    @pl.when(t == nt - 1)
    def _():
        sfin_ref[...] = s_scr[...]
        cfin_ref[...] = c_scr[...]
        mfin_ref[...] = m_scr[...]
        xfin_ref[...] = x_scr[...]


def _front(x2d, weights, states, nb, chunk):
    m = x2d.shape[0]
    nt = m // chunk
    s0, c0, m0, x0 = states
    prev_lanes = lambda t: (0, jnp.maximum(t - 1, 0))
    const2 = lambda t: (0, 0)
    const3 = lambda t: (0, 0, 0)

    def resident(a):
        return pl.BlockSpec(a.shape, const2 if a.ndim == 2 else const3, pipeline_mode=pl.Buffered(1))

    outs = pl.pallas_call(
        functools.partial(_front_kernel, chunk=chunk, chunks_per_seq=nt // nb),
        grid=(nt + 1,),
        in_specs=[pl.BlockSpec((chunk, D_MODEL), const2, pipeline_mode=pl.Buffered(1)),
                  pl.BlockSpec((chunk, D_MODEL), lambda t: (jnp.minimum(t + 1, nt - 1), 0))]
        + [resident(a) for a in weights[:8]]
        + [resident(weights[8]), resident(s0)] + [resident(a) for a in weights[9:12]]
        + [resident(c0), resident(m0), resident(x0)] + [resident(a) for a in weights[12:]],
        out_specs=[
            pl.BlockSpec((chunk, D_MODEL), lambda t: (jnp.maximum(t - 1, 0), 0)),
            pl.BlockSpec((TOP_K, chunk), prev_lanes),
            pl.BlockSpec((TOP_K, chunk), prev_lanes),
            pl.BlockSpec((None, N_EXPERTS, 128), lambda t: (jnp.maximum(t - 1, 0), 0, 0)),
            pl.BlockSpec(s0.shape, const3),
            pl.BlockSpec(c0.shape, const3),
            pl.BlockSpec(m0.shape, const2),
            pl.BlockSpec(x0.shape, const2),
        ],
        out_shape=[
            jax.ShapeDtypeStruct((m, D_MODEL), F32),
            jax.ShapeDtypeStruct((TOP_K, m), I32),
            jax.ShapeDtypeStruct((TOP_K, m), F32),
            jax.ShapeDtypeStruct((nt, N_EXPERTS, 128), F32),
            jax.ShapeDtypeStruct(s0.shape, F32),
            jax.ShapeDtypeStruct(c0.shape, F32),
            jax.ShapeDtypeStruct(m0.shape, F32),
            jax.ShapeDtypeStruct(x0.shape, F32),
        ],
        scratch_shapes=[
            pltpu.VMEM((N_SLABS, chunk, D_MODEL), BF16),
            pltpu.VMEM((chunk, D_MODEL), F32),
            pltpu.VMEM((chunk, 128), F32),
            pltpu.VMEM((SUBLANES, chunk), F32),
            pltpu.VMEM((chunk, D_MODEL), BF16),
            pltpu.VMEM((chunk, D_MODEL), BF16),
            pltpu.VMEM((chunk, D_MODEL), F32),
            pltpu.VMEM((chunk, D_MODEL), BF16),
            pltpu.VMEM((chunk, D_MODEL), F32),
            pltpu.VMEM((HG_HEADS, HG_DK, HG_DK), F32),
            pltpu.VMEM((ML_HEADS, ML_DK, ML_AUG), F32),
            pltpu.VMEM((SUBLANES, 128), F32),
            pltpu.VMEM((SUBLANES, D_MODEL), F32),
        ],
        compiler_params=_params("arbitrary"),
        name="front",
    )(x2d, x2d, *weights[:9], s0, *weights[9:12], c0, m0, x0, *weights[12:])
    return outs[0], outs[1], outs[2], outs[3], tuple(outs[4:])


def _tile_slots(tt):
    return -(-(TOP_K * tt + N_EXPERTS * (GROUP_ROWS - 1)) // SLOT_BLOCK) * SLOT_BLOCK


def _wait_groups(n, make_copy, s_tile):
    p = 1 << ((s_tile // GROUP_ROWS).bit_length() - 1)
    while p:
        @pl.when((n & p) != 0)
        def _():
            make_copy(p * GROUP_ROWS).wait()
        p >>= 1


def _block_relative(slots, r):
    return jnp.clip(slots - r * SLOT_BLOCK, -1, SLOT_BLOCK).astype(F32).astype(BF16)


def _for_slot_blocks(n, tt, s_tile, body):
    always = TOP_K * tt // SLOT_BLOCK
    for r in range(always):
        body(r)
    for r in range(always, s_tile // SLOT_BLOCK):
        @pl.when(n * GROUP_ROWS > r * SLOT_BLOCK)
        def _():
            body(r)


def _issue_copies(n2, n1, plan_ref, make_copy, s_tile):
    max2 = s_tile // (2 * GROUP_ROWS)
    base1 = 2 * max2

    def start2(p, priority):
        make_copy(plan_ref[0, 0, p], plan_ref[0, 0, max2 + p], 2 * GROUP_ROWS).start(priority)

    def start1(q, priority):
        make_copy(plan_ref[0, 0, base1 + q], plan_ref[0, 0, base1 + N_EXPERTS + q], GROUP_ROWS).start(priority)

    def alternating(n, start):
        def pair(j, c):
            start(2 * j, 0)
            start(2 * j + 1, 1)
            return c

        lax.fori_loop(0, n // 2, pair, 0)

        @pl.when((n & 1) != 0)
        def _():
            start(n - 1, 0)

    alternating(n2, start2)
    alternating(n1, start1)


def _plan_len(s_tile):
    return 2 * (s_tile // (2 * GROUP_ROWS)) + 2 * N_EXPERTS


def _dispatch_kernel(ng_ref, n2_ref, n1_ref, plan_ref, slot_ref, h_ref, xs_ref, buf, sem, *, tt, s_tile):
    i = pl.program_id(0)
    cur = lax.rem(i, 2)

    def list_to_global(list_row, global_row, rows):
        src = buf.at[cur, pl.ds(pl.multiple_of(list_row, GROUP_ROWS), rows), :]
        dst = xs_ref.at[pl.ds(pl.multiple_of(global_row, GROUP_ROWS), rows), :]
        return pltpu.make_async_copy(src, dst, sem.at[cur])

    def wait_tile(j, b):
        _wait_groups(ng_ref[j], lambda rows: pltpu.make_async_copy(
            buf.at[b, pl.ds(0, rows), :], xs_ref.at[pl.ds(0, rows), :], sem.at[b]), s_tile)

    @pl.when(i >= 2)
    def _():
        wait_tile(i - 2, cur)

    hb = h_ref[...].astype(BF16)
    sl = slot_ref[...]

    row_id = lax.broadcasted_iota(I32, (SLOT_BLOCK, tt), 0).astype(F32).astype(BF16)
    one = jnp.ones((), BF16)

    def fill_block(r):
        rel = _block_relative(sl, r)
        p = jnp.zeros((SLOT_BLOCK, tt), BF16)
        for kk in range(TOP_K):
            p = jnp.where(row_id == rel[kk:kk + 1, :], one, p)
        buf[cur, r * SLOT_BLOCK:(r + 1) * SLOT_BLOCK, :] = _dot(p, hb).astype(BF16)

    _for_slot_blocks(ng_ref[i], tt, s_tile, fill_block)

    _issue_copies(n2_ref[i], n1_ref[i], plan_ref, list_to_global, s_tile)

    @pl.when(i == pl.num_programs(0) - 1)
    def _():
        @pl.when(i >= 1)
        def _():
            wait_tile(i - 1, 1 - cur)
        wait_tile(i, cur)


def _dispatch(counts, plan, slot_k, h1, n_slots, tt):
    m = h1.shape[0]
    s_tile = _tile_slots(tt)
    grid_spec = pltpu.PrefetchScalarGridSpec(
        num_scalar_prefetch=3,
        grid=(m // tt,),
        in_specs=[
            pl.BlockSpec((1, 1, _plan_len(s_tile)), lambda i, *_: (i, 0, 0), memory_space=pltpu.SMEM),
            pl.BlockSpec((TOP_K, tt), lambda i, *_: (0, i)),
            pl.BlockSpec((tt, D_MODEL), lambda i, *_: (i, 0)),
        ],
        out_specs=pl.BlockSpec(memory_space=pl.ANY),
        scratch_shapes=[pltpu.VMEM((2, s_tile, D_MODEL), BF16), pltpu.SemaphoreType.DMA((2,))],
    )
    return pl.pallas_call(
        functools.partial(_dispatch_kernel, tt=tt, s_tile=s_tile),
        grid_spec=grid_spec,
        out_shape=jax.ShapeDtypeStruct((n_slots, D_MODEL), BF16),
        compiler_params=_params("arbitrary"),
        name="dispatch",
    )(*counts, plan, slot_k, h1)


def _experts_kernel(be_ref, nu_ref, x_ref, wg_ref, wu_ref, wd_ref, y_ref, wg_b, wu_b, wd_b):
    i = pl.program_id(0)
    used = i < nu_ref[0]

    @pl.when(used & ((i == 0) | (be_ref[i] != be_ref[jnp.maximum(i - 1, 0)])))
    def _():
        wg_b[...] = wg_ref[...].astype(BF16)
        wu_b[...] = wu_ref[...].astype(BF16)
        wd_b[...] = wd_ref[...].astype(BF16)

    @pl.when(used)
    def _():
        xb = x_ref[...]
        a = _dot(xb, wg_b[...])
        u = _dot(xb, wu_b[...])
        y_ref[...] = _dot((a * _sigmoid(a) * u).astype(BF16), wd_b[...]).astype(BF16)


def _experts(blk_expert, n_used, xs, wg, wu, wd):
    n_slots = xs.shape[0]
    n_blocks = n_slots // MOE_BLOCK
    blk = lambda i, be, nu: (jnp.minimum(i, nu[0] - 1), 0)
    wsel = lambda i, be, nu: (be[jnp.minimum(i, nu[0] - 1)], 0, 0)
    grid_spec = pltpu.PrefetchScalarGridSpec(
        num_scalar_prefetch=2,
        grid=(n_blocks,),
        in_specs=[
            pl.BlockSpec((MOE_BLOCK, D_MODEL), blk),
            pl.BlockSpec((None, D_MODEL, D_EXPERT), wsel),
            pl.BlockSpec((None, D_MODEL, D_EXPERT), wsel),
            pl.BlockSpec((None, D_EXPERT, D_MODEL), wsel),
        ],
        out_specs=pl.BlockSpec((MOE_BLOCK, D_MODEL), blk),
        scratch_shapes=[pltpu.VMEM((D_MODEL, D_EXPERT), BF16), pltpu.VMEM((D_MODEL, D_EXPERT), BF16),
                        pltpu.VMEM((D_EXPERT, D_MODEL), BF16)],
    )
    return pl.pallas_call(
        _experts_kernel,
        grid_spec=grid_spec,
        out_shape=jax.ShapeDtypeStruct((n_slots, D_MODEL), BF16),
        compiler_params=_params("arbitrary"),
        name="experts",
    )(blk_expert, n_used, xs, wg, wu, wd)


def _combine_kernel(ng_ref, n2_ref, n1_ref, gcur_ref, gnext_ref, slot_ref, gw_ref, h_ref, y_ref, sg_ref, su_ref, sd_ref,
                    g2_ref, b2_ref, o_ref, ybuf, acc, sem, *, tt, s_tile):
    i = pl.program_id(0)
    cur = lax.rem(i, 2)

    def fetch(j, b, plan_ref):
        def global_to_list(list_row, global_row, rows):
            src = y_ref.at[pl.ds(pl.multiple_of(global_row, GROUP_ROWS), rows), :]
            dst = ybuf.at[b, pl.ds(pl.multiple_of(list_row, GROUP_ROWS), rows), :]
            return pltpu.make_async_copy(src, dst, sem.at[b])

        _issue_copies(n2_ref[j], n1_ref[j], plan_ref, global_to_list, s_tile)

    @pl.when(i == 0)
    def _():
        ybuf[...] = jnp.zeros_like(ybuf)
        fetch(0, 0, gcur_ref)

    @pl.when(i + 1 < pl.num_programs(0))
    def _():
        fetch(i + 1, 1 - cur, gnext_ref)

    h1 = h_ref[...]
    hb = h1.astype(BF16)
    a = _dot(hb, sg_ref[...])
    u = _dot(hb, su_ref[...])
    shared = _dot((a * _sigmoid(a) * u).astype(BF16), sd_ref[...])

    slot = slot_ref[...]
    gwb = gw_ref[...].astype(BF16)
    _wait_groups(ng_ref[i], lambda rows: pltpu.make_async_copy(
        y_ref.at[pl.ds(0, rows), :], ybuf.at[cur, pl.ds(0, rows), :], sem.at[cur]), s_tile)

    n_always, n_all = TOP_K * tt // SLOT_BLOCK, s_tile // SLOT_BLOCK
    per = 2 if (n_always % 2 == 0 and n_all % 2 == 0) else 1
    rows = per * SLOT_BLOCK
    row_id = lax.broadcasted_iota(I32, (SLOT_BLOCK, tt), 0).astype(F32).astype(BF16)

    def one_hot(r):
        rel = _block_relative(slot, r)
        p = jnp.zeros((SLOT_BLOCK, tt), BF16)
        for kk in range(TOP_K):
            p = jnp.where(row_id == rel[kk:kk + 1, :], gwb[kk:kk + 1, :], p)
        return p

    def block_dot(r):
        p = jnp.concatenate([one_hot(per * r + j) for j in range(per)], axis=0) if per > 1 else one_hot(r)
        return _dot_tn(p, ybuf[cur, r * rows:(r + 1) * rows, :])

    routed = block_dot(0)
    for r in range(1, n_always // per):
        routed = routed + block_dot(r)
    acc[...] = DN_ALPHA * h1 + (routed + shared)
    for r in range(n_always // per, n_all // per):
        @pl.when(ng_ref[i] * GROUP_ROWS > r * rows)
        def _():
            acc[...] += block_dot(r)
    o_ref[...] = _layer_norm(acc[...], g2_ref[...], b2_ref[...])


def _combine(counts, plan, slot_k, gw_k, h1, y, wsg, wsu, wsd, g2, b2, tt):
    m = h1.shape[0]
    nt = m // tt
    s_tile = _tile_slots(tt)
    const = lambda i, *_: (0, 0)
    table = lambda f: pl.BlockSpec((1, 1, _plan_len(s_tile)), f, memory_space=pltpu.SMEM)
    grid_spec = pltpu.PrefetchScalarGridSpec(
        num_scalar_prefetch=3,
        grid=(nt,),
        in_specs=[
            table(lambda i, *_: (i, 0, 0)),
            table(lambda i, *_: (jnp.minimum(i + 1, nt - 1), 0, 0)),
            pl.BlockSpec((TOP_K, tt), lambda i, *_: (0, i)),
            pl.BlockSpec((TOP_K, tt), lambda i, *_: (0, i)),
            pl.BlockSpec((tt, D_MODEL), lambda i, *_: (i, 0)),
            pl.BlockSpec(memory_space=pl.ANY),
            pl.BlockSpec((D_MODEL, D_EXPERT), const),
            pl.BlockSpec((D_MODEL, D_EXPERT), const),
            pl.BlockSpec((D_EXPERT, D_MODEL), const),
            pl.BlockSpec((1, D_MODEL), const),
            pl.BlockSpec((1, D_MODEL), const),
        ],
        out_specs=pl.BlockSpec((tt, D_MODEL), lambda i, *_: (i, 0)),
        scratch_shapes=[pltpu.VMEM((2, s_tile, D_MODEL), BF16), pltpu.VMEM((tt, D_MODEL), F32),
                        pltpu.SemaphoreType.DMA((2,))],
    )
    return pl.pallas_call(
        functools.partial(_combine_kernel, tt=tt, s_tile=s_tile),
        grid_spec=grid_spec,
        out_shape=jax.ShapeDtypeStruct((m, D_MODEL), F32),
        compiler_params=_params("arbitrary"),
        name="combine",
    )(*counts, plan, plan, slot_k, gw_k, h1, y, wsg, wsu, wsd, g2, b2)


def _pick_tile(m, pref):
    t = min(pref, m)
    while m % t:
        t //= 2
    return t


def _forward(x, meta_tokens, ln_emb_g, ln_emb_b, w_in, hg_lb_logits, hg_norm_g, ml_conv_w, ml_conv_b,
             ml_ig_bias, ml_fg_bias, ml_norm_g, w_branch_a, w_branch_b, w_out, ln1_g, ln1_b,
             w_router, router_bias, w_exp_gate, w_exp_up, w_exp_down, w_sh_gate, w_sh_up, w_sh_down,
             ln2_g, ln2_b, *, chunk):
    nb, seq, d = x.shape
    m = nb * seq
    row = lambda a: a.reshape(1, -1).astype(F32)

    w = w_in[0]
    kw = HG_HEADS * HG_DK
    o_qa, o_fa, o_ia, o_ga = 0, kw, 2 * kw, 3 * kw
    o_qb = 4 * kw
    o_kb = o_qb + ML_HEADS * ML_DK
    o_vb = o_kb + ML_HEADS * ML_DK
    o_ob = o_vb + ML_HEADS * ML_DV
    o_ig = o_ob + ML_HEADS * ML_DV
    o_fg = o_ig + ML_HEADS
    o_ma = o_fg + ML_HEADS
    o_mb = o_ma + D_MODEL
    cols = lambda o, n: w[:, o:o + n]
    w_cat = jnp.concatenate([
        cols(o_qa, kw), cols(o_fa, kw), cols(o_ia, kw), cols(o_ga, kw),
        cols(o_qb, 2 * ML_HEADS * ML_DK), cols(o_vb, ML_HEADS * ML_DV), cols(o_ob, ML_HEADS * ML_DV),
        cols(o_ma, D_MODEL), cols(o_mb, D_MODEL)], axis=1).astype(BF16)
    w_small = jnp.pad(cols(o_ig, 2 * ML_HEADS), ((0, 0), (0, 128 - 2 * ML_HEADS))).astype(BF16)
    gate_bias = jnp.pad(jnp.concatenate([ml_ig_bias[0], ml_fg_bias[0]]).astype(F32), (0, 128 - 2 * ML_HEADS)).reshape(1, 128)
    lb = jax.nn.softmax(hg_lb_logits.astype(F32), axis=0)[0].reshape(1, -1)
    eg, eb = row(ln_emb_g), row(ln_emb_b)
    conv_w = ml_conv_w[0].astype(F32)
    conv_b = row(ml_conv_b[0])
    hgn, mln = row(hg_norm_g[0]), row(ml_norm_g[0])
    w_small_t = jnp.pad(cols(o_ig, 2 * ML_HEADS).T, ((0, GROUP_ROWS - 2 * ML_HEADS), (0, 0))).astype(BF16)
    gate_bias_t = jnp.broadcast_to(jnp.pad(gate_bias[0, :2 * ML_HEADS], (0, GROUP_ROWS - 2 * ML_HEADS))[:, None],
                                   (GROUP_ROWS, 128))
    wr = w_router[0].T.astype(F32)
    wr_hi, wr_lo = _split_bf16(wr)
    rbias = jnp.broadcast_to(router_bias[0].astype(F32).reshape(N_EXPERTS, 1), (N_EXPERTS, 128))
    weights = [eg, eb, w_cat, w_small, lb, gate_bias, w_small_t, gate_bias_t, hgn, conv_w, conv_b, mln,
               w_branch_a[0].astype(BF16), w_branch_b[0].astype(BF16), w_out[0].astype(BF16),
               row(ln1_g[0]), row(ln1_b[0]), wr_hi, wr_lo, rbias]

    zero_states = (jnp.zeros((HG_HEADS, HG_DK, HG_DK), F32), jnp.zeros((ML_HEADS, ML_DK, ML_AUG), F32),
                   jnp.zeros((SUBLANES, 128), F32), jnp.zeros((SUBLANES, D_MODEL), F32))
    meta_states = _front(meta_tokens.astype(F32), weights, zero_states, 1, N_META)[4]

    x2d = x.reshape(m, d).astype(F32)
    tt = chunk
    h1, slot_k, gw, cnt, _ = _front(x2d, weights, meta_states, nb, chunk)

    nt = m // tt
    s_tile = _tile_slots(tt)
    cnt8 = (cnt[:, :, 0].astype(I32) + GROUP_ROWS - 1) // GROUP_ROWS * GROUP_ROWS
    seg_end = jnp.cumsum(cnt8, axis=1)
    seg_off = seg_end - cnt8
    tile_rows = seg_end[:, -1]
    run = jnp.cumsum(cnt8, axis=0) - cnt8
    tot8 = jnp.sum(cnt8, axis=0)
    padded = (tot8 + MOE_BLOCK - 1) // MOE_BLOCK * MOE_BLOCK
    pends = jnp.cumsum(padded)
    gshift = (pends - padded)[None, :] + run - seg_off
    experts = jnp.arange(N_EXPERTS, dtype=I32)

    def copy_list(per_expert, length, list_row0, step):
        ends = jnp.cumsum(per_expert, axis=1)
        idx = jnp.arange(length, dtype=I32)
        owner = jnp.sum((ends[:, None, :] <= idx[None, :, None]).astype(I32), axis=-1)
        pick = lambda a: jnp.sum(jnp.where(owner[..., None] == experts, a[:, None, :], 0), axis=-1)
        list_row = pick(list_row0 - step * (ends - per_expert)) + step * idx[None, :]
        valid = idx[None, :] < ends[:, -1:]
        return jnp.where(valid, list_row, 0), jnp.where(valid, list_row + pick(gshift), 0), ends[:, -1]

    groups = cnt8 // GROUP_ROWS
    doubles = groups // 2
    l2, g2, n2 = copy_list(doubles, s_tile // (2 * GROUP_ROWS), seg_off, 2 * GROUP_ROWS)
    l1, g1, n1 = copy_list(groups % 2, N_EXPERTS, seg_off + 2 * GROUP_ROWS * doubles, 0)
    plan = jnp.concatenate([l2, g2, l1, g1], axis=1).astype(I32).reshape(nt, 1, -1)
    counts = ((tile_rows // GROUP_ROWS).astype(I32), n2.astype(I32), n1.astype(I32))
    n_blocks = -(-(m * TOP_K + nt * N_EXPERTS * (GROUP_ROWS - 1)) // MOE_BLOCK) + N_EXPERTS
    blk_start = jnp.arange(n_blocks, dtype=I32) * MOE_BLOCK
    blk_expert = jnp.minimum(jnp.sum((pends[None, :] <= blk_start[:, None]).astype(I32), axis=1), N_EXPERTS - 1)
    n_used = (pends[-1:] // MOE_BLOCK).astype(I32)

    xs = _dispatch(counts, plan, slot_k, h1, n_blocks * MOE_BLOCK, tt)
    y = _experts(blk_expert, n_used, xs, w_exp_gate[0].astype(F32), w_exp_up[0].astype(F32),
                 w_exp_down[0].astype(F32))
    out = _combine(counts, plan, slot_k, gw, h1, y, w_sh_gate[0].astype(BF16), w_sh_up[0].astype(BF16),
                   w_sh_down[0].astype(BF16), row(ln2_g[0]), row(ln2_b[0]), tt)
    return out.reshape(nb, seq, d).astype(x.dtype)


def kernel(x, meta_tokens, ln_emb_g, ln_emb_b, w_in, hg_lb_logits, hg_norm_g, ml_conv_w, ml_conv_b, ml_ig_bias, ml_fg_bias, ml_norm_g, w_branch_a, w_branch_b, w_out, ln1_g, ln1_b, w_router, router_bias, w_exp_gate, w_exp_up, w_exp_down, w_sh_gate, w_sh_up, w_sh_down, ln2_g, ln2_b):
    return _forward(x, meta_tokens, ln_emb_g, ln_emb_b, w_in, hg_lb_logits, hg_norm_g, ml_conv_w, ml_conv_b,
                    ml_ig_bias, ml_fg_bias, ml_norm_g, w_branch_a, w_branch_b, w_out, ln1_g, ln1_b,
                    w_router, router_bias, w_exp_gate, w_exp_up, w_exp_down, w_sh_gate, w_sh_up, w_sh_down,
                    ln2_g, ln2_b, chunk=_pick_tile(x.shape[1], 256))
```

```python
import functools

import jax
import jax.numpy as jnp
from jax import lax
from jax.experimental import pallas as pl
from jax.experimental.pallas import tpu as pltpu

F32, BF16, I32 = jnp.float32, jnp.bfloat16, jnp.int32

D_MODEL = 1024
N_META = 16
HG_HEADS = 8
HG_DK = 128
ML_HEADS = 4
ML_DK = 128
ML_DV = 256
ML_AUG = ML_DV + 128
N_EXPERTS = 64
TOP_K = 8
N_GROUPS = 8
GROUP_SIZE = N_EXPERTS // N_GROUPS
TOPK_GROUPS = 4
D_EXPERT = 256
ROUTED_SCALE = 2.5
MOE_BLOCK = 2048
SLOT_BLOCK = 256
DN_ALPHA = 2.0 ** 0.25
EPS = 1e-5
LOG2E = 1.4426950408889634
EXP2_CLAMP = 115.0
SUBLANES = 8
GROUP_ROWS = 16

P_QA, P_KA, P_IA, P_GA, P_QKB, P_VB, P_OB, P_MA, P_MB = range(9)
N_SLABS = 9

VMEM_LIMIT = 56 * 1024 * 1024


def _params(*sem):
    return pltpu.CompilerParams(dimension_semantics=sem, vmem_limit_bytes=VMEM_LIMIT)


def _sigmoid(x):
    return 1.0 / (1.0 + jnp.exp(-x))


def _log_sigmoid(x):
    return jnp.minimum(x, 0.0) - jnp.log(1.0 + jnp.exp(-jnp.abs(x)))


def _layer_norm(x, g, b):
    xc = x - jnp.mean(x, axis=-1, keepdims=True)
    var = jnp.mean(xc * xc, axis=-1, keepdims=True)
    return xc * lax.rsqrt(var + EPS) * g + b


def _dot(a, b):
    return jnp.dot(a, b, preferred_element_type=F32)


def _dot_nt(a, b):
    return lax.dot_general(a, b, (((1,), (1,)), ((), ())), preferred_element_type=F32)


def _dot_tn(a, b):
    return lax.dot_general(a, b, (((0,), (0,)), ((), ())), preferred_element_type=F32)


def _split_bf16(x):
    hi = x.astype(BF16)
    lo = (x - hi.astype(F32)).astype(BF16)
    return hi, lo


def _neg_abs(x):
    return lax.bitcast_convert_type(lax.bitcast_convert_type(x, I32) | jnp.int32(-2 ** 31), F32)


def _ones_where(cond):
    return jnp.where(cond, 1.0, 0.0).astype(BF16)


def _embed_norm(x_ref, g_ref, b_ref, h0_ref, hb_ref):
    h0 = _layer_norm(x_ref[...], g_ref[...], b_ref[...])
    h0_ref[...] = h0
    hb_ref[...] = h0.astype(BF16)


def _inproj_pieces(hb, w_ref, ws_ref, lb_ref, gb_ref, wst_ref, gbt_ref, p_ref, lf_ref, sg_ref, sgt_ref):
    def slab(n):
        def run():
            acc = _dot(hb, w_ref[:, n * D_MODEL:(n + 1) * D_MODEL])
            if n == P_KA:
                lb = lb_ref[...]
                f = lb + (1.0 - lb) * _sigmoid(acc)
                lf_ref[...] = jnp.log(f)
                acc = 1.0 - f
            elif n == P_GA:
                acc = acc * _sigmoid(acc)
            elif n >= P_OB:
                acc = _sigmoid(acc)
            p_ref[n] = acc.astype(BF16)
        return run

    def gates():
        s = _dot(hb, ws_ref[...]) + gb_ref[...]
        lane = lax.broadcasted_iota(I32, s.shape, 1)
        sg_ref[...] = jnp.where(lane < ML_HEADS, s, _log_sigmoid(s))
        st = _dot_nt(wst_ref[...], hb) + gbt_ref[:, 0:1]
        srow = lax.broadcasted_iota(I32, st.shape, 0)
        sgt_ref[...] = jnp.where(srow < ML_HEADS, st, _log_sigmoid(st))[:SUBLANES, :]

    return [slab(n) for n in range(N_SLABS)], gates


def _block_rows(b, block, pick):
    c, w = b.shape
    parts = [jnp.broadcast_to(b[j * block + pick:j * block + pick + 1, :], (block, w))
             for j in range(c // block)]
    return parts[0] if len(parts) == 1 else jnp.concatenate(parts, axis=0)


def _hgrn_body(q_ref, k_ref, v_ref, g_ref, lf_ref, ng_ref, y_ref, s_scr, cs):
    row = lax.broadcasted_iota(I32, (cs, cs), 0)
    col = lax.broadcasted_iota(I32, (cs, cs), 1)
    tri = _ones_where(col <= row)
    lf_hi, lf_lo = _split_bf16(lf_ref[...])
    b = (_dot(tri, lf_hi) + _dot(tri, lf_lo)) * LOG2E
    q = q_ref[...]
    k = k_ref[...]
    v = v_ref[...]
    blast = b[cs - 1:cs, :]
    qg = q * jnp.exp2(b).astype(BF16)
    kg = k * jnp.exp2(blast - b).astype(BF16)
    dec = jnp.exp2(blast)

    levels = []
    m = SUBLANES
    while 2 * m <= cs:
        w = jnp.exp2(_neg_abs(b - _block_rows(b, 2 * m, m - 1))).astype(BF16)
        sh = (2 * m).bit_length() - 1
        mask = ((row >> sh) == (col >> sh)) & ((row & (2 * m - 1)) >= m) & ((col & (2 * m - 1)) < m)
        levels.append((q * w, k * w, mask))
        m *= 2
    e = jnp.clip(b - _block_rows(b, SUBLANES, SUBLANES // 2 - 1), -EXP2_CLAMP, EXP2_CLAMP)
    levels.append((q * jnp.exp2(e).astype(BF16), k * jnp.exp2(-e).astype(BF16),
                   ((row >> 3) == (col >> 3)) & (col <= row)))

    ng = ng_ref[...]
    for h in range(HG_HEADS):
        sl = slice(h * HG_DK, (h + 1) * HG_DK)
        st = s_scr[h]
        o = _dot_nt(qg[:, sl], st.astype(BF16))
        sc = jnp.zeros((cs, cs), F32)
        for lq, lk, mask in levels:
            sc = jnp.where(mask, _dot_nt(lq[:, sl], lk[:, sl]), sc)
        o = o + _dot(sc.astype(BF16), v[:, sl])
        s_scr[h] = dec[:, sl] * st + _dot_tn(v[:, sl], kg[:, sl])
        ms = jnp.mean(o * o, axis=-1, keepdims=True)
        y = o * lax.rsqrt(ms + EPS) * ng[:, sl] * g_ref[:, sl].astype(F32)
        y_ref[:, sl] = y.astype(BF16)


def _mlstm_conv(qk_ref, cw_ref, cb_ref, x_scr, cs):
    x = qk_ref[...].astype(F32)
    prev = x_scr[...]
    sub = lax.broadcasted_iota(I32, (SUBLANES, D_MODEL), 0)
    cw = cw_ref[...]
    conv = cw[3:4, :] * x + cb_ref[...]
    for j in (1, 2, 3):
        xs = pltpu.roll(x, j, 0)
        head = jnp.where(sub < j, pltpu.roll(prev, j, 0), xs[:SUBLANES, :])
        xs = jnp.concatenate([head, xs[SUBLANES:, :]], axis=0)
        conv = conv + cw[3 - j:4 - j, :] * xs
    x_scr[...] = x[cs - SUBLANES:, :]
    qk = conv * _sigmoid(conv)
    return (qk[:, :ML_HEADS * ML_DK] * (ML_DK ** -0.5)).astype(BF16), qk[:, ML_HEADS * ML_DK:]


def _mlstm_cumsums(sg_ref, sgt_ref, cs):
    row = lax.broadcasted_iota(I32, (cs, cs), 0)
    col = lax.broadcasted_iota(I32, (cs, cs), 1)
    tri = _ones_where(col <= row)
    sg_hi, sg_lo = _split_bf16(sg_ref[...])
    sgt_hi, sgt_lo = _split_bf16(sgt_ref[...])
    return _dot(tri, sg_hi) + _dot(tri, sg_lo), _dot_nt(sgt_hi, tri) + _dot_nt(sgt_lo, tri)


def _mlstm_heads(q_all, k_all, bcol_all, brow_all, v_ref, og_ref, sg_ref, sgt_ref, ng_ref, y_ref, c_scr, m_scr, cs):
    row = lax.broadcasted_iota(I32, (cs, cs), 0)
    col = lax.broadcasted_iota(I32, (cs, cs), 1)
    causal = col <= row
    sg = sg_ref[...]
    sgt = sgt_ref[...]
    lane128 = lax.broadcasted_iota(I32, (cs, 128), 1)
    ones_col = _ones_where(lane128 == 0)
    v = v_ref[...]
    ng = ng_ref[...]

    for h in range(ML_HEADS):
        b_col = bcol_all[:, ML_HEADS + h:ML_HEADS + h + 1]
        b_row = brow_all[ML_HEADS + h:ML_HEADS + h + 1, :]
        ig_col = sg[:, h:h + 1]
        ig_row = sgt[h:h + 1, :]
        m_prev = m_scr[h:h + 1, 0:1]
        q_h = q_all[:, h * ML_DK:(h + 1) * ML_DK]
        k_h = k_all[:, h * ML_DK:(h + 1) * ML_DK]
        v_aug = jnp.concatenate([v[:, h * ML_DV:(h + 1) * ML_DV], ones_col], axis=1)
        c_st = c_scr[h]

        log_intra = jnp.where(causal, b_col - b_row + ig_row, -jnp.inf)
        log_inter = b_col + m_prev
        m_t = jnp.maximum(log_inter, jnp.max(log_intra, axis=-1, keepdims=True))
        w_intra = jnp.exp(log_intra - m_t)
        w_inter = jnp.exp(log_inter - m_t)
        s = _dot_nt(q_h, k_h.astype(BF16)) * w_intra
        tot = w_inter * _dot(q_h, c_st.astype(BF16)) + _dot(s.astype(BF16), v_aug)
        num = tot[:, :ML_DV]
        den = tot[:, ML_DV:ML_DV + 1]
        hid = num / jnp.maximum(jnp.abs(den), jnp.exp(-m_t))

        b_last = b_col[cs - 1:cs, :]
        log_w = b_last - b_col + ig_col
        m_new = jnp.maximum(b_last + m_prev, jnp.max(log_w, axis=0, keepdims=True))
        w_s = jnp.exp(log_w - m_new)
        decay = jnp.exp(b_last + m_prev - m_new)
        c_scr[h] = decay * c_st + _dot_tn((k_h * w_s).astype(BF16), v_aug)
        m_scr[h:h + 1, :] = jnp.broadcast_to(m_new, (1, 128))

        hc = hid - jnp.mean(hid, axis=-1, keepdims=True)
        var = jnp.mean(hc * hc, axis=-1, keepdims=True)
        sl = slice(h * ML_DV, (h + 1) * ML_DV)
        y = hc * lax.rsqrt(var + EPS) * ng[:, sl] * og_ref[:, sl].astype(F32)
        y_ref[:, sl] = y.astype(BF16)


def _merge_body(h0_ref, ya_ref, yb_ref, ma_ref, mb_ref, wa_ref, wb_ref, wo_ref, pre_ref, after_h0):
    resid = DN_ALPHA * h0_ref[...]
    after_h0()
    merged = (ma_ref[...].astype(F32) * _dot(ya_ref[...], wa_ref[...])
              + mb_ref[...].astype(F32) * _dot(yb_ref[...], wb_ref[...]))
    pre_ref[...] = resid + _dot(merged.astype(BF16), wo_ref[...])


def _router_pieces(pre_ref, g1_ref, b1_ref, wrh_ref, wrl_ref, rb_ref, h1_ref, slot_ref, gw_ref, cnt_ref, tm):
    neg_inf = -jnp.inf
    v = {}

    def ln1():
        v['h1'] = _layer_norm(pre_ref[...], g1_ref[...], b1_ref[...])
        h1_ref[...] = v['h1']

    def logits():
        h_hi, h_lo = _split_bf16(v['h1'])
        lg = _dot_nt(wrh_ref[...], h_hi) + _dot_nt(wrh_ref[...], h_lo) + _dot_nt(wrl_ref[...], h_hi)
        v['scores'] = _sigmoid(lg)
        v['biased'] = v['scores'] + rb_ref[:, 0:1]

    def group_scores():
        g3 = v['biased'].reshape(N_GROUPS, GROUP_SIZE, tm)
        sub3 = lax.broadcasted_iota(I32, g3.shape, 1)
        top1 = jnp.max(g3, axis=1, keepdims=True)
        first = jnp.min(jnp.where(g3 == top1, sub3, GROUP_SIZE), axis=1, keepdims=True)
        top2 = jnp.max(jnp.where(sub3 == first, neg_inf, g3), axis=1, keepdims=True)
        v['gs'] = (top1 + top2).reshape(N_GROUPS, tm)

    def group_select():
        gs = v['gs']
        gi = lax.broadcasted_iota(I32, gs.shape, 0)
        grank = jnp.zeros(gs.shape, F32)
        for j in range(N_GROUPS):
            r = gs[j:j + 1, :]
            grank = grank + jnp.where((r > gs) | ((r == gs) & (gi > j)), 1.0, 0.0)
        gsel = grank < float(TOPK_GROUPS)
        emask = jnp.broadcast_to(gsel.reshape(N_GROUPS, 1, tm), (N_GROUPS, GROUP_SIZE, tm)).reshape(N_EXPERTS, tm)
        v['work'] = jnp.where(emask, v['biased'], neg_inf)
        v['rank'] = jnp.full((N_EXPERTS, tm), float(N_EXPERTS), F32)

    def extract(kk):
        def run():
            ei = lax.broadcasted_iota(I32, (N_EXPERTS, tm), 0)
            for k2 in (kk, kk + 1):
                work = v['work']
                top = jnp.max(work, axis=0, keepdims=True)
                first = jnp.min(jnp.where(work == top, ei, N_EXPERTS), axis=0, keepdims=True)
                hit = ei == first
                v['rank'] = jnp.where(hit, float(k2), v['rank'])
                v['work'] = jnp.where(hit, neg_inf, work)
        return run

    def slots():
        sel = v['rank'] < float(TOP_K)
        sel_w = jnp.where(sel, v['scores'], 0.0)
        v['gwd'] = sel_w / jnp.sum(sel_w, axis=0, keepdims=True) * ROUTED_SCALE
        tr = lax.broadcasted_iota(I32, (tm, tm), 0)
        tc = lax.broadcasted_iota(I32, (tm, tm), 1)
        sel_b = _ones_where(sel)
        rloc = _dot(sel_b, _ones_where(tr < tc))
        cnt = _dot(sel_b, jnp.ones((tm, 128), BF16))
        cnt_g = jnp.floor((cnt + (GROUP_ROWS - 1.0)) * (1.0 / GROUP_ROWS)) * GROUP_ROWS
        er = lax.broadcasted_iota(I32, (N_EXPERTS, N_EXPERTS), 0)
        ec = lax.broadcasted_iota(I32, (N_EXPERTS, N_EXPERTS), 1)
        seg_start = _dot(_ones_where(ec < er), cnt_g.astype(BF16))
        v['slot_e'] = seg_start[:, 0:1] + rloc
        v['sel'] = sel
        cnt_ref[...] = cnt

    def picks():
        s_rows, w_rows = [], []
        for kk in range(TOP_K):
            pick = v['sel'] & (v['rank'] == float(kk))
            s_rows.append(jnp.sum(jnp.where(pick, v['slot_e'], 0.0), axis=0, keepdims=True))
            w_rows.append(jnp.sum(jnp.where(pick, v['gwd'], 0.0), axis=0, keepdims=True))
        slot_ref[...] = jnp.concatenate(s_rows, axis=0).astype(I32)
        gw_ref[...] = jnp.concatenate(w_rows, axis=0)

    return [ln1, logits, group_scores, group_select] + [extract(kk) for kk in range(0, TOP_K, 2)] + [slots, picks]


def _front_kernel(xf_ref, xn_ref, eg_ref, eb_ref, w_ref, ws_ref, lb_ref, gb_ref, wst_ref, gbt_ref,
                  hng_ref, s0_ref, cw_ref, cb_ref, mng_ref, c0_ref, m0_ref, x0_ref,
                  wa_ref, wb_ref, wo_ref, g1_ref, b1_ref, wrh_ref, wrl_ref, rb_ref,
                  h1_ref, slot_ref, gw_ref, cnt_ref, sfin_ref, cfin_ref, mfin_ref, xfin_ref,
                  p_scr, lf_scr, sg_scr, sgt_scr, ya_scr, yb_scr, h0_scr, hb_scr, hp_scr, s_scr, c_scr, m_scr, x_scr,
                  *, chunk, chunks_per_seq):
    t = pl.program_id(0)
    nt = pl.num_programs(0) - 1

    @pl.when(t == 0)
    def _():
        _embed_norm(xf_ref, eg_ref, eb_ref, h0_scr, hb_scr)

    @pl.when(lax.rem(t, chunks_per_seq) == 0)
    def _():
        s_scr[...] = s0_ref[...]
        c_scr[...] = c0_ref[...]
        m_scr[...] = m0_ref[...]
        x_scr[...] = x0_ref[...]

    @pl.when(t == 0)
    def _():
        hp_scr[...] = jnp.zeros_like(hp_scr)

    def router():
        return _router_pieces(hp_scr, g1_ref, b1_ref, wrh_ref, wrl_ref, rb_ref, h1_ref, slot_ref, gw_ref, cnt_ref,
                              chunk)

    @pl.when(t < nt)
    def _():
        slabs, gates = _inproj_pieces(hb_scr[...], w_ref, ws_ref, lb_ref, gb_ref, wst_ref, gbt_ref,
                                      p_scr, lf_scr, sg_scr, sgt_scr)
        ml = {}

        def conv():
            ml['q'], ml['k'] = _mlstm_conv(p_scr.at[P_QKB], cw_ref, cb_ref, x_scr, chunk)

        def cumsums():
            ml['bcol'], ml['brow'] = _mlstm_cumsums(sg_scr, sgt_scr, chunk)

        r = router()
        light = [[], [r[0]], [conv, r[1]], [cumsums, r[2]]] + [[p] for p in r[3:8]] + [r[8:]]
        heavy = [slabs[P_QKB], gates] + [slabs[n] for n in range(N_SLABS) if n != P_QKB]
        for piece, fill in zip(heavy, light):
            piece()
            for f in fill:
                f()
        _hgrn_body(p_scr.at[P_QA], p_scr.at[P_KA], p_scr.at[P_IA], p_scr.at[P_GA], lf_scr, hng_ref, ya_scr, s_scr,
                   chunk)
        _mlstm_heads(ml['q'], ml['k'], ml['bcol'], ml['brow'], p_scr.at[P_VB], p_scr.at[P_OB], sg_scr, sgt_scr,
                     mng_ref, yb_scr, c_scr, m_scr, chunk)
        _merge_body(h0_scr, ya_scr, yb_scr, p_scr.at[P_MA], p_scr.at[P_MB], wa_ref, wb_ref, wo_ref, hp_scr,
                    functools.partial(_embed_norm, xn_ref, eg_ref, eb_ref, h0_scr, hb_scr))

    @pl.when(t == nt)
    def _():
        for piece in router():
            piece()

    @pl.when(t == nt - 1)
    def _():
        sfin_ref[...] = s_scr[...]
        cfin_ref[...] = c_scr[...]
        mfin_ref[...] = m_scr[...]
        xfin_ref[...] = x_scr[...]


def _front(x2d, weights, states, nb, chunk):
    m = x2d.shape[0]
    nt = m // chunk
    s0, c0, m0, x0 = states
    prev_lanes = lambda t: (0, jnp.maximum(t - 1, 0))
    const2 = lambda t: (0, 0)
    const3 = lambda t: (0, 0, 0)

    def resident(a):
        return pl.BlockSpec(a.shape, const2 if a.ndim == 2 else const3, pipeline_mode=pl.Buffered(1))

    outs = pl.pallas_call(
        functools.partial(_front_kernel, chunk=chunk, chunks_per_seq=nt // nb),
        grid=(nt + 1,),
        in_specs=[pl.BlockSpec((chunk, D_MODEL), const2, pipeline_mode=pl.Buffered(1)),
                  pl.BlockSpec((chunk, D_MODEL), lambda t: (jnp.minimum(t + 1, nt - 1), 0))]
        + [resident(a) for a in weights[:8]]
        + [resident(weights[8]), resident(s0)] + [resident(a) for a in weights[9:12]]
        + [resident(c0), resident(m0), resident(x0)] + [resident(a) for a in weights[12:]],
        out_specs=[
            pl.BlockSpec((chunk, D_MODEL), lambda t: (jnp.maximum(t - 1, 0), 0)),
            pl.BlockSpec((TOP_K, chunk), prev_lanes),
            pl.BlockSpec((TOP_K, chunk), prev_lanes),
            pl.BlockSpec((None, N_EXPERTS, 128), lambda t: (jnp.maximum(t - 1, 0), 0, 0)),
            pl.BlockSpec(s0.shape, const3),
            pl.BlockSpec(c0.shape, const3),
            pl.BlockSpec(m0.shape, const2),
            pl.BlockSpec(x0.shape, const2),
        ],
        out_shape=[
            jax.ShapeDtypeStruct((m, D_MODEL), F32),
            jax.ShapeDtypeStruct((TOP_K, m), I32),
            jax.ShapeDtypeStruct((TOP_K, m), F32),
            jax.ShapeDtypeStruct((nt, N_EXPERTS, 128), F32),
            jax.ShapeDtypeStruct(s0.shape, F32),
            jax.ShapeDtypeStruct(c0.shape, F32),
            jax.ShapeDtypeStruct(m0.shape, F32),
            jax.ShapeDtypeStruct(x0.shape, F32),
        ],
        scratch_shapes=[
            pltpu.VMEM((N_SLABS, chunk, D_MODEL), BF16),
            pltpu.VMEM((chunk, D_MODEL), F32),
            pltpu.VMEM((chunk, 128), F32),
            pltpu.VMEM((SUBLANES, chunk), F32),
            pltpu.VMEM((chunk, D_MODEL), BF16),
            pltpu.VMEM((chunk, D_MODEL), BF16),
            pltpu.VMEM((chunk, D_MODEL), F32),
            pltpu.VMEM((chunk, D_MODEL), BF16),
            pltpu.VMEM((chunk, D_MODEL), F32),
            pltpu.VMEM((HG_HEADS, HG_DK, HG_DK), F32),
            pltpu.VMEM((ML_HEADS, ML_DK, ML_AUG), F32),
            pltpu.VMEM((SUBLANES, 128), F32),
            pltpu.VMEM((SUBLANES, D_MODEL), F32),
        ],
        compiler_params=_params("arbitrary"),
        name="front",
    )(x2d, x2d, *weights[:9], s0, *weights[9:12], c0, m0, x0, *weights[12:])
    return outs[0], outs[1], outs[2], outs[3], tuple(outs[4:])


def _tile_slots(tt):
    return -(-(TOP_K * tt + N_EXPERTS * (GROUP_ROWS - 1)) // SLOT_BLOCK) * SLOT_BLOCK


def _wait_groups(n, make_copy, s_tile):
    p = 1 << ((s_tile // GROUP_ROWS).bit_length() - 1)
    while p:
        @pl.when((n & p) != 0)
        def _():
            make_copy(p * GROUP_ROWS).wait()
        p >>= 1


def _block_relative(slots, r):
    return jnp.clip(slots - r * SLOT_BLOCK, -1, SLOT_BLOCK).astype(F32).astype(BF16)


def _always_blocks(tt, s_tile):
    return min(-(-(TOP_K * tt + N_EXPERTS * GROUP_ROWS // 2) // SLOT_BLOCK), s_tile // SLOT_BLOCK)


def _for_slot_blocks(n, tt, s_tile, body):
    always = _always_blocks(tt, s_tile)
    for r in range(always):
        body(r)
    for r in range(always, s_tile // SLOT_BLOCK):
        @pl.when(n * GROUP_ROWS > r * SLOT_BLOCK)
        def _():
            body(r)


def _issue_copies(n2, n1, plan_ref, make_copy, s_tile):
    max2 = s_tile // (2 * GROUP_ROWS)
    base1 = 2 * max2

    def start2(p, priority):
        make_copy(plan_ref[0, 0, p], plan_ref[0, 0, max2 + p], 2 * GROUP_ROWS).start(priority)

    def start1(q, priority):
        make_copy(plan_ref[0, 0, base1 + q], plan_ref[0, 0, base1 + N_EXPERTS + q], GROUP_ROWS).start(priority)

    def alternating(n, start):
        def pair(j, c):
            start(2 * j, 0)
            start(2 * j + 1, 1)
            return c

        lax.fori_loop(0, n // 2, pair, 0)

        @pl.when((n & 1) != 0)
        def _():
            start(n - 1, 0)

    alternating(n2, start2)
    alternating(n1, start1)


def _plan_len(s_tile):
    return 2 * (s_tile // (2 * GROUP_ROWS)) + 2 * N_EXPERTS


def _dispatch_kernel(ng_ref, n2_ref, n1_ref, plan_ref, slot_ref, h_ref, xs_ref, buf, sem, *, tt, s_tile):
    i = pl.program_id(0)
    cur = lax.rem(i, 2)

    def list_to_global(list_row, global_row, rows):
        src = buf.at[cur, pl.ds(pl.multiple_of(list_row, GROUP_ROWS), rows), :]
        dst = xs_ref.at[pl.ds(pl.multiple_of(global_row, GROUP_ROWS), rows), :]
        return pltpu.make_async_copy(src, dst, sem.at[cur])

    def wait_tile(j, b):
        _wait_groups(ng_ref[j], lambda rows: pltpu.make_async_copy(
            buf.at[b, pl.ds(0, rows), :], xs_ref.at[pl.ds(0, rows), :], sem.at[b]), s_tile)

    @pl.when(i >= 2)
    def _():
        wait_tile(i - 2, cur)

    hb = h_ref[...].astype(BF16)
    sl = slot_ref[...]

    row_id = lax.broadcasted_iota(I32, (SLOT_BLOCK, tt), 0).astype(F32).astype(BF16)
    one = jnp.ones((), BF16)

    def fill_block(r):
        rel = _block_relative(sl, r)
        p = jnp.zeros((SLOT_BLOCK, tt), BF16)
        for kk in range(TOP_K):
            p = jnp.where(row_id == rel[kk:kk + 1, :], one, p)
        buf[cur, r * SLOT_BLOCK:(r + 1) * SLOT_BLOCK, :] = _dot(p, hb).astype(BF16)

    _for_slot_blocks(ng_ref[i], tt, s_tile, fill_block)

    _issue_copies(n2_ref[i], n1_ref[i], plan_ref, list_to_global, s_tile)

    @pl.when(i == pl.num_programs(0) - 1)
    def _():
        @pl.when(i >= 1)
        def _():
            wait_tile(i - 1, 1 - cur)
        wait_tile(i, cur)


def _dispatch(counts, plan, slot_k, h1, n_slots, tt):
    m = h1.shape[0]
    s_tile = _tile_slots(tt)
    grid_spec = pltpu.PrefetchScalarGridSpec(
        num_scalar_prefetch=3,
        grid=(m // tt,),
        in_specs=[
            pl.BlockSpec((1, 1, _plan_len(s_tile)), lambda i, *_: (i, 0, 0), memory_space=pltpu.SMEM),
            pl.BlockSpec((TOP_K, tt), lambda i, *_: (0, i)),
            pl.BlockSpec((tt, D_MODEL), lambda i, *_: (i, 0)),
        ],
        out_specs=pl.BlockSpec(memory_space=pl.ANY),
        scratch_shapes=[pltpu.VMEM((2, s_tile, D_MODEL), BF16), pltpu.SemaphoreType.DMA((2,))],
    )
    return pl.pallas_call(
        functools.partial(_dispatch_kernel, tt=tt, s_tile=s_tile),
        grid_spec=grid_spec,
        out_shape=jax.ShapeDtypeStruct((n_slots, D_MODEL), BF16),
        compiler_params=_params("arbitrary"),
        name="dispatch",
    )(*counts, plan, slot_k, h1)


def _experts_kernel(be_ref, nu_ref, x_ref, wg_ref, wu_ref, wd_ref, y_ref, wg_b, wu_b, wd_b):
    i = pl.program_id(0)
    used = i < nu_ref[0]

    @pl.when(used & ((i == 0) | (be_ref[i] != be_ref[jnp.maximum(i - 1, 0)])))
    def _():
        wg_b[...] = wg_ref[...].astype(BF16)
        wu_b[...] = wu_ref[...].astype(BF16)
        wd_b[...] = wd_ref[...].astype(BF16)

    @pl.when(used)
    def _():
        xb = x_ref[...]
        a = _dot(xb, wg_b[...])
        u = _dot(xb, wu_b[...])
        y_ref[...] = _dot((a * _sigmoid(a) * u).astype(BF16), wd_b[...]).astype(BF16)


def _experts(blk_expert, n_used, xs, wg, wu, wd):
    n_slots = xs.shape[0]
    n_blocks = n_slots // MOE_BLOCK
    blk = lambda i, be, nu: (jnp.minimum(i, nu[0] - 1), 0)
    wsel = lambda i, be, nu: (be[jnp.minimum(i, nu[0] - 1)], 0, 0)
    grid_spec = pltpu.PrefetchScalarGridSpec(
        num_scalar_prefetch=2,
        grid=(n_blocks,),
        in_specs=[
            pl.BlockSpec((MOE_BLOCK, D_MODEL), blk),
            pl.BlockSpec((None, D_MODEL, D_EXPERT), wsel),
            pl.BlockSpec((None, D_MODEL, D_EXPERT), wsel),
            pl.BlockSpec((None, D_EXPERT, D_MODEL), wsel),
        ],
        out_specs=pl.BlockSpec((MOE_BLOCK, D_MODEL), blk),
        scratch_shapes=[pltpu.VMEM((D_MODEL, D_EXPERT), BF16), pltpu.VMEM((D_MODEL, D_EXPERT), BF16),
                        pltpu.VMEM((D_EXPERT, D_MODEL), BF16)],
    )
    return pl.pallas_call(
        _experts_kernel,
        grid_spec=grid_spec,
        out_shape=jax.ShapeDtypeStruct((n_slots, D_MODEL), BF16),
        compiler_params=_params("arbitrary"),
        name="experts",
    )(blk_expert, n_used, xs, wg, wu, wd)


def _combine_kernel(ng_ref, n2_ref, n1_ref, gcur_ref, gnext_ref, slot_ref, gw_ref, h_ref, y_ref, sg_ref, su_ref, sd_ref,
                    g2_ref, b2_ref, o_ref, ybuf, acc, sem, *, tt, s_tile):
    i = pl.program_id(0)
    cur = lax.rem(i, 2)

    def fetch(j, b, plan_ref):
        def global_to_list(list_row, global_row, rows):
            src = y_ref.at[pl.ds(pl.multiple_of(global_row, GROUP_ROWS), rows), :]
            dst = ybuf.at[b, pl.ds(pl.multiple_of(list_row, GROUP_ROWS), rows), :]
            return pltpu.make_async_copy(src, dst, sem.at[b])

        _issue_copies(n2_ref[j], n1_ref[j], plan_ref, global_to_list, s_tile)

    @pl.when(i == 0)
    def _():
        ybuf[...] = jnp.zeros_like(ybuf)
        fetch(0, 0, gcur_ref)

    @pl.when(i + 1 < pl.num_programs(0))
    def _():
        fetch(i + 1, 1 - cur, gnext_ref)

    h1 = h_ref[...]
    hb = h1.astype(BF16)
    a = _dot(hb, sg_ref[...])
    u = _dot(hb, su_ref[...])
    shared = _dot((a * _sigmoid(a) * u).astype(BF16), sd_ref[...])

    slot = slot_ref[...]
    gwb = gw_ref[...].astype(BF16)
    _wait_groups(ng_ref[i], lambda rows: pltpu.make_async_copy(
        y_ref.at[pl.ds(0, rows), :], ybuf.at[cur, pl.ds(0, rows), :], sem.at[cur]), s_tile)

    n_always, n_all = _always_blocks(tt, s_tile), s_tile // SLOT_BLOCK
    per = 2 if (n_always % 2 == 0 and n_all % 2 == 0) else 1
    rows = per * SLOT_BLOCK
    row_id = lax.broadcasted_iota(I32, (SLOT_BLOCK, tt), 0).astype(F32).astype(BF16)

    def one_hot(r):
        rel = _block_relative(slot, r)
        p = jnp.zeros((SLOT_BLOCK, tt), BF16)
        for kk in range(TOP_K):
            p = jnp.where(row_id == rel[kk:kk + 1, :], gwb[kk:kk + 1, :], p)
        return p

    def block_dot(r):
        p = jnp.concatenate([one_hot(per * r + j) for j in range(per)], axis=0) if per > 1 else one_hot(r)
        return _dot_tn(p, ybuf[cur, r * rows:(r + 1) * rows, :])

    routed = block_dot(0)
    for r in range(1, n_always // per):
        routed = routed + block_dot(r)
    acc[...] = DN_ALPHA * h1 + (routed + shared)
    for r in range(n_always // per, n_all // per):
        @pl.when(ng_ref[i] * GROUP_ROWS > r * rows)
        def _():
            acc[...] += block_dot(r)
    o_ref[...] = _layer_norm(acc[...], g2_ref[...], b2_ref[...])


def _combine(counts, plan, slot_k, gw_k, h1, y, wsg, wsu, wsd, g2, b2, tt):
    m = h1.shape[0]
    nt = m // tt
    s_tile = _tile_slots(tt)
    const = lambda i, *_: (0, 0)
    table = lambda f: pl.BlockSpec((1, 1, _plan_len(s_tile)), f, memory_space=pltpu.SMEM)
    grid_spec = pltpu.PrefetchScalarGridSpec(
        num_scalar_prefetch=3,
        grid=(nt,),
        in_specs=[
            table(lambda i, *_: (i, 0, 0)),
            table(lambda i, *_: (jnp.minimum(i + 1, nt - 1), 0, 0)),
            pl.BlockSpec((TOP_K, tt), lambda i, *_: (0, i)),
            pl.BlockSpec((TOP_K, tt), lambda i, *_: (0, i)),
            pl.BlockSpec((tt, D_MODEL), lambda i, *_: (i, 0)),
            pl.BlockSpec(memory_space=pl.ANY),
            pl.BlockSpec((D_MODEL, D_EXPERT), const),
            pl.BlockSpec((D_MODEL, D_EXPERT), const),
            pl.BlockSpec((D_EXPERT, D_MODEL), const),
            pl.BlockSpec((1, D_MODEL), const),
            pl.BlockSpec((1, D_MODEL), const),
        ],
        out_specs=pl.BlockSpec((tt, D_MODEL), lambda i, *_: (i, 0)),
        scratch_shapes=[pltpu.VMEM((2, s_tile, D_MODEL), BF16), pltpu.VMEM((tt, D_MODEL), F32),
                        pltpu.SemaphoreType.DMA((2,))],
    )
    return pl.pallas_call(
        functools.partial(_combine_kernel, tt=tt, s_tile=s_tile),
        grid_spec=grid_spec,
        out_shape=jax.ShapeDtypeStruct((m, D_MODEL), F32),
        compiler_params=_params("arbitrary"),
        name="combine",
    )(*counts, plan, plan, slot_k, gw_k, h1, y, wsg, wsu, wsd, g2, b2)


def _pick_tile(m, pref):
    t = min(pref, m)
    while m % t:
        t //= 2
    return t


def _forward(x, meta_tokens, ln_emb_g, ln_emb_b, w_in, hg_lb_logits, hg_norm_g, ml_conv_w, ml_conv_b,
             ml_ig_bias, ml_fg_bias, ml_norm_g, w_branch_a, w_branch_b, w_out, ln1_g, ln1_b,
             w_router, router_bias, w_exp_gate, w_exp_up, w_exp_down, w_sh_gate, w_sh_up, w_sh_down,
             ln2_g, ln2_b, *, chunk):
    nb, seq, d = x.shape
    m = nb * seq
    row = lambda a: a.reshape(1, -1).astype(F32)

    w = w_in[0]
    kw = HG_HEADS * HG_DK
    o_qa, o_fa, o_ia, o_ga = 0, kw, 2 * kw, 3 * kw
    o_qb = 4 * kw
    o_kb = o_qb + ML_HEADS * ML_DK
    o_vb = o_kb + ML_HEADS * ML_DK
    o_ob = o_vb + ML_HEADS * ML_DV
    o_ig = o_ob + ML_HEADS * ML_DV
    o_fg = o_ig + ML_HEADS
    o_ma = o_fg + ML_HEADS
    o_mb = o_ma + D_MODEL
    cols = lambda o, n: w[:, o:o + n]
    w_cat = jnp.concatenate([
        cols(o_qa, kw), cols(o_fa, kw), cols(o_ia, kw), cols(o_ga, kw),
        cols(o_qb, 2 * ML_HEADS * ML_DK), cols(o_vb, ML_HEADS * ML_DV), cols(o_ob, ML_HEADS * ML_DV),
        cols(o_ma, D_MODEL), cols(o_mb, D_MODEL)], axis=1).astype(BF16)
    w_small = jnp.pad(cols(o_ig, 2 * ML_HEADS), ((0, 0), (0, 128 - 2 * ML_HEADS))).astype(BF16)
    gate_bias = jnp.pad(jnp.concatenate([ml_ig_bias[0], ml_fg_bias[0]]).astype(F32), (0, 128 - 2 * ML_HEADS)).reshape(1, 128)
    lb = jax.nn.softmax(hg_lb_logits.astype(F32), axis=0)[0].reshape(1, -1)
    eg, eb = row(ln_emb_g), row(ln_emb_b)
    conv_w = ml_conv_w[0].astype(F32)
    conv_b = row(ml_conv_b[0])
    hgn, mln = row(hg_norm_g[0]), row(ml_norm_g[0])
    w_small_t = jnp.pad(cols(o_ig, 2 * ML_HEADS).T, ((0, GROUP_ROWS - 2 * ML_HEADS), (0, 0))).astype(BF16)
    gate_bias_t = jnp.broadcast_to(jnp.pad(gate_bias[0, :2 * ML_HEADS], (0, GROUP_ROWS - 2 * ML_HEADS))[:, None],
                                   (GROUP_ROWS, 128))
    wr = w_router[0].T.astype(F32)
    wr_hi, wr_lo = _split_bf16(wr)
    rbias = jnp.broadcast_to(router_bias[0].astype(F32).reshape(N_EXPERTS, 1), (N_EXPERTS, 128))
    weights = [eg, eb, w_cat, w_small, lb, gate_bias, w_small_t, gate_bias_t, hgn, conv_w, conv_b, mln,
               w_branch_a[0].astype(BF16), w_branch_b[0].astype(BF16), w_out[0].astype(BF16),
               row(ln1_g[0]), row(ln1_b[0]), wr_hi, wr_lo, rbias]

    zero_states = (jnp.zeros((HG_HEADS, HG_DK, HG_DK), F32), jnp.zeros((ML_HEADS, ML_DK, ML_AUG), F32),
                   jnp.zeros((SUBLANES, 128), F32), jnp.zeros((SUBLANES, D_MODEL), F32))
    meta_states = _front(meta_tokens.astype(F32), weights, zero_states, 1, N_META)[4]

    x2d = x.reshape(m, d).astype(F32)
    tt = chunk
    h1, slot_k, gw, cnt, _ = _front(x2d, weights, meta_states, nb, chunk)

    nt = m // tt
    s_tile = _tile_slots(tt)
    cnt8 = (cnt[:, :, 0].astype(I32) + GROUP_ROWS - 1) // GROUP_ROWS * GROUP_ROWS
    seg_end = jnp.cumsum(cnt8, axis=1)
    seg_off = seg_end - cnt8
    tile_rows = seg_end[:, -1]
    run = jnp.cumsum(cnt8, axis=0) - cnt8
    tot8 = jnp.sum(cnt8, axis=0)
    padded = (tot8 + MOE_BLOCK - 1) // MOE_BLOCK * MOE_BLOCK
    pends = jnp.cumsum(padded)
    gshift = (pends - padded)[None, :] + run - seg_off
    experts = jnp.arange(N_EXPERTS, dtype=I32)

    def copy_list(per_expert, length, list_row0, step):
        ends = jnp.cumsum(per_expert, axis=1)
        idx = jnp.arange(length, dtype=I32)
        owner = jnp.sum((ends[:, None, :] <= idx[None, :, None]).astype(I32), axis=-1)
        pick = lambda a: jnp.sum(jnp.where(owner[..., None] == experts, a[:, None, :], 0), axis=-1)
        list_row = pick(list_row0 - step * (ends - per_expert)) + step * idx[None, :]
        valid = idx[None, :] < ends[:, -1:]
        return jnp.where(valid, list_row, 0), jnp.where(valid, list_row + pick(gshift), 0), ends[:, -1]

    groups = cnt8 // GROUP_ROWS
    doubles = groups // 2
    l2, g2, n2 = copy_list(doubles, s_tile // (2 * GROUP_ROWS), seg_off, 2 * GROUP_ROWS)
    l1, g1, n1 = copy_list(groups % 2, N_EXPERTS, seg_off + 2 * GROUP_ROWS * doubles, 0)
    plan = jnp.concatenate([l2, g2, l1, g1], axis=1).astype(I32).reshape(nt, 1, -1)
    counts = ((tile_rows // GROUP_ROWS).astype(I32), n2.astype(I32), n1.astype(I32))
    n_blocks = -(-(m * TOP_K + nt * N_EXPERTS * (GROUP_ROWS - 1)) // MOE_BLOCK) + N_EXPERTS
    blk_start = jnp.arange(n_blocks, dtype=I32) * MOE_BLOCK
    blk_expert = jnp.minimum(jnp.sum((pends[None, :] <= blk_start[:, None]).astype(I32), axis=1), N_EXPERTS - 1)
    n_used = (pends[-1:] // MOE_BLOCK).astype(I32)

    xs = _dispatch(counts, plan, slot_k, h1, n_blocks * MOE_BLOCK, tt)
    y = _experts(blk_expert, n_used, xs, w_exp_gate[0].astype(F32), w_exp_up[0].astype(F32),
                 w_exp_down[0].astype(F32))
    out = _combine(counts, plan, slot_k, gw, h1, y, w_sh_gate[0].astype(BF16), w_sh_up[0].astype(BF16),
                   w_sh_down[0].astype(BF16), row(ln2_g[0]), row(ln2_b[0]), tt)
    return out.reshape(nb, seq, d).astype(x.dtype)


def kernel(x, meta_tokens, ln_emb_g, ln_emb_b, w_in, hg_lb_logits, hg_norm_g, ml_conv_w, ml_conv_b, ml_ig_bias, ml_fg_bias, ml_norm_g, w_branch_a, w_branch_b, w_out, ln1_g, ln1_b, w_router, router_bias, w_exp_gate, w_exp_up, w_exp_down, w_sh_gate, w_sh_up, w_sh_down, ln2_g, ln2_b):
    return _forward(x, meta_tokens, ln_emb_g, ln_emb_b, w_in, hg_lb_logits, hg_norm_g, ml_conv_w, ml_conv_b,
                    ml_ig_bias, ml_fg_bias, ml_norm_g, w_branch_a, w_branch_b, w_out, ln1_g, ln1_b,
                    w_router, router_bias, w_exp_gate, w_exp_up, w_exp_down, w_sh_gate, w_sh_up, w_sh_down,
                    ln2_g, ln2_b, chunk=_pick_tile(x.shape[1], 256))
```
